```python
import jax, jax.numpy as jnp
from jax import lax
import numpy as np

D_MODEL = 2048
BATCH = 8
SEQ = 2048
DEPTH = 1

HEAD_DIM = 64
N_Q_HEADS = (D_MODEL // 2) // HEAD_DIM
N_KV_HEADS = N_Q_HEADS // 4
GROUP = N_Q_HEADS // N_KV_HEADS
ATTN_WIDTH = N_Q_HEADS * HEAD_DIM
KV_WIDTH = N_KV_HEADS * HEAD_DIM
CONV_WIDTH = D_MODEL // 2
CONV_K = 3
WINDOW = 128
BLOCK = 128
ROPE_THETA = 500000.0
ROT_DIM = HEAD_DIM // 4
D_FF = ((8 * D_MODEL // 3 + 255) // 256) * 256
RMS_EPS = 1e-6
ATTN_SCALE = HEAD_DIM ** -0.5
NEG_INF = -1e30
IN_WIDTHS = (CONV_WIDTH, CONV_WIDTH, CONV_WIDTH, ATTN_WIDTH, KV_WIDTH, KV_WIDTH, D_MODEL, D_MODEL)
IN_SPLITS = tuple(int(s) for s in np.cumsum(IN_WIDTHS)[:-1])
IN_TOTAL = int(sum(IN_WIDTHS))

kernel_name = "hybrid_macaron_conv_swa_gated"


def rms_norm(x, g):
    xf = x.astype(jnp.float32)
    y = xf * lax.rsqrt(jnp.mean(xf * xf, axis=-1, keepdims=True) + RMS_EPS)
    return (y * g.astype(jnp.float32)).astype(x.dtype)


def swiglu(h, w_gu, w_down):
    g, u = jnp.split(h @ w_gu, 2, axis=-1)
    return (jax.nn.silu(g) * u) @ w_down


def rope_tables(seq_len):
    inv_freq = 1.0 / (ROPE_THETA ** (jnp.arange(0, ROT_DIM, 2, dtype=jnp.float32) / ROT_DIM))
    ang = jnp.arange(seq_len, dtype=jnp.float32)[:, None] * inv_freq[None, :]
    return jnp.cos(ang)[None, :, None, :], jnp.sin(ang)[None, :, None, :]


def partial_rope(x, cos, sin):
    half = ROT_DIM // 2
    xf = x.astype(jnp.float32)
    x1, x2, xp = xf[..., :half], xf[..., half:ROT_DIM], xf[..., ROT_DIM:]
    out = jnp.concatenate([x1 * cos - x2 * sin, x2 * cos + x1 * sin, xp], axis=-1)
    return out.astype(x.dtype)


def causal_short_conv(u, w):
    S = u.shape[1]
    up = jnp.pad(u, ((0, 0), (CONV_K - 1, 0), (0, 0)))
    y = up[:, 0:S] * w[0]
    for j in range(1, CONV_K):
        y = y + up[:, j:j + S] * w[j]
    return y


def banded(t, nb):
    B = t.shape[0]
    tp = jnp.pad(t, ((0, 0), (BLOCK, 0), (0, 0), (0, 0)))
    tb = tp.reshape(B, nb + 1, BLOCK, t.shape[2], t.shape[3])
    return jnp.concatenate([tb[:, :-1], tb[:, 1:]], axis=2)


def sliding_window_gqa_sinks(q, k, v, sinks):
    B, S = q.shape[0], q.shape[1]
    nb = S // BLOCK
    qb = q.reshape(B, nb, BLOCK, N_KV_HEADS, GROUP, HEAD_DIM)
    kb, vb = banded(k, nb), banded(v, nb)
    s = jnp.einsum('bnqhgd,bnkhd->bnhgqk', qb, kb).astype(jnp.float32) * ATTN_SCALE
    qi = jnp.arange(BLOCK)[:, None] + BLOCK
    ki = jnp.arange(2 * BLOCK)[None, :]
    diff = qi - ki
    in_window = (diff >= 0) & (diff < WINDOW)
    key_pos = jnp.arange(nb)[:, None] * BLOCK + jnp.arange(2 * BLOCK)[None, :] - BLOCK
    valid = in_window[None] & (key_pos >= 0)[:, None, :]
    s = jnp.where(valid[None, :, None, None], s, NEG_INF)
    sink = sinks.astype(jnp.float32).reshape(1, 1, N_KV_HEADS, GROUP, 1, 1)
    m = jnp.maximum(jnp.max(s, axis=-1, keepdims=True), sink)
    p = jnp.exp(s - m)
    denom = jnp.sum(p, axis=-1, keepdims=True) + jnp.exp(sink - m)
    o = jnp.einsum('bnhgqk,bnkhd->bnqhgd', p / denom, vb.astype(jnp.float32))
    return o.reshape(B, S, ATTN_WIDTH).astype(q.dtype)


def hybrid_mixer(h, w_in, conv_w, q_norm_g, k_norm_g, sinks, w_out_conv, w_out_attn, w_o, cos, sin):
    B, S, _ = h.shape
    xc, bg, cg, q, k, v, ga, gb = jnp.split(h @ w_in, IN_SPLITS, axis=-1)
    ya = (bg * causal_short_conv(cg * xc, conv_w)) @ w_out_conv
    q = rms_norm(q.reshape(B, S, N_Q_HEADS, HEAD_DIM), q_norm_g)
    k = rms_norm(k.reshape(B, S, N_KV_HEADS, HEAD_DIM), k_norm_g)
    v = v.reshape(B, S, N_KV_HEADS, HEAD_DIM)
    q = partial_rope(q, cos, sin)
    k = partial_rope(k, cos, sin)
    yb = sliding_window_gqa_sinks(q, k, v, sinks) @ w_out_attn
    merged = jax.nn.sigmoid(ga) * ya + jax.nn.sigmoid(gb) * yb
    return merged @ w_o


def _fwd_setup_inputs(seed: int = 0) -> dict:
    key = jax.random.key(seed)
    ks = jax.random.split(key, 18)
    f32 = jnp.float32

    def w(k, shape, fan_in):
        return jax.random.normal(k, shape, f32) * (fan_in ** -0.5)

    def gain(k, n):
        return 1.0 + 0.02 * jax.random.normal(k, (DEPTH, n), f32)

    return {
        "x": jax.random.normal(ks[0], (BATCH, SEQ, D_MODEL), f32),
        "g_ffn1": gain(ks[1], D_MODEL),
        "w_gu1": w(ks[2], (DEPTH, D_MODEL, 2 * D_FF), D_MODEL),
        "w_down1": w(ks[3], (DEPTH, D_FF, D_MODEL), D_FF),
        "g_mix": gain(ks[4], D_MODEL),
        "w_in": w(ks[5], (DEPTH, D_MODEL, IN_TOTAL), D_MODEL),
        "conv_w": w(ks[6], (DEPTH, CONV_K, CONV_WIDTH), CONV_K),
        "q_norm_g": gain(ks[7], HEAD_DIM),
        "k_norm_g": gain(ks[8], HEAD_DIM),
        "sinks": 0.5 * jax.random.normal(ks[9], (DEPTH, N_Q_HEADS), f32),
        "w_out_conv": w(ks[10], (DEPTH, CONV_WIDTH, D_MODEL), CONV_WIDTH),
        "w_out_attn": w(ks[11], (DEPTH, ATTN_WIDTH, D_MODEL), ATTN_WIDTH),
        "w_o": w(ks[12], (DEPTH, D_MODEL, D_MODEL), D_MODEL),
        "g_ffn2": gain(ks[13], D_MODEL),
        "w_gu2": w(ks[14], (DEPTH, D_MODEL, 2 * D_FF), D_MODEL),
        "w_down2": w(ks[15], (DEPTH, D_FF, D_MODEL), D_FF),
    }


def _fwd_reference(x, g_ffn1, w_gu1, w_down1, g_mix, w_in, conv_w, q_norm_g, k_norm_g, sinks,
              w_out_conv, w_out_attn, w_o, g_ffn2, w_gu2, w_down2):
    cos, sin = rope_tables(x.shape[1])
    for l in range(DEPTH):
        x = x + 0.5 * swiglu(rms_norm(x, g_ffn1[l]), w_gu1[l], w_down1[l])
        x = x + hybrid_mixer(rms_norm(x, g_mix[l]), w_in[l], conv_w[l], q_norm_g[l], k_norm_g[l],
                             sinks[l], w_out_conv[l], w_out_attn[l], w_o[l], cos, sin)
        x = x + 0.5 * swiglu(rms_norm(x, g_ffn2[l]), w_gu2[l], w_down2[l])
    return x


import jax as _jax
import jax.numpy as _jnp

TWIN_FORMAT = 'train_step'
FWD_PARAMS = ['x', 'g_ffn1', 'w_gu1', 'w_down1', 'g_mix', 'w_in', 'conv_w', 'q_norm_g', 'k_norm_g', 'sinks', 'w_out_conv', 'w_out_attn', 'w_o', 'g_ffn2', 'w_gu2', 'w_down2']
TWIN_WEIGHTS = ['g_ffn1', 'w_gu1', 'w_down1', 'g_mix', 'w_in', 'conv_w', 'q_norm_g', 'k_norm_g', 'sinks', 'w_out_conv', 'w_out_attn', 'w_o', 'g_ffn2', 'w_gu2', 'w_down2']
TWIN_DIFF_INPUT = 'x'
TWIN_INPUTS = ['x', 'g_ffn1', 'w_gu1', 'w_down1', 'g_mix', 'w_in', 'conv_w', 'q_norm_g', 'k_norm_g', 'sinks', 'w_out_conv', 'w_out_attn', 'w_o', 'g_ffn2', 'w_gu2', 'w_down2', 'loss_target', 'm_g_ffn1', 'm_w_gu1', 'm_w_down1', 'm_g_mix', 'm_w_in', 'm_conv_w', 'm_q_norm_g', 'm_k_norm_g', 'm_sinks', 'm_w_out_conv', 'm_w_out_attn', 'm_w_o', 'm_g_ffn2', 'm_w_gu2', 'm_w_down2', 'v_g_ffn1', 'v_w_gu1', 'v_w_down1', 'v_g_mix', 'v_w_in', 'v_conv_w', 'v_q_norm_g', 'v_k_norm_g', 'v_sinks', 'v_w_out_conv', 'v_w_out_attn', 'v_w_o', 'v_g_ffn2', 'v_w_gu2', 'v_w_down2']
TWIN_OUTPUTS = ['loss', 'grad_x', 'grad_g_ffn1', 'grad_w_gu1', 'grad_w_down1', 'grad_g_mix', 'grad_w_in', 'grad_conv_w', 'grad_q_norm_g', 'grad_k_norm_g', 'grad_sinks', 'grad_w_out_conv', 'grad_w_out_attn', 'grad_w_o', 'grad_g_ffn2', 'grad_w_gu2', 'grad_w_down2', 'delta_g_ffn1', 'delta_w_gu1', 'delta_w_down1', 'delta_g_mix', 'delta_w_in', 'delta_conv_w', 'delta_q_norm_g', 'delta_k_norm_g', 'delta_sinks', 'delta_w_out_conv', 'delta_w_out_attn', 'delta_w_o', 'delta_g_ffn2', 'delta_w_gu2', 'delta_w_down2', 'new_m_g_ffn1', 'new_m_w_gu1', 'new_m_w_down1', 'new_m_g_mix', 'new_m_w_in', 'new_m_conv_w', 'new_m_q_norm_g', 'new_m_k_norm_g', 'new_m_sinks', 'new_m_w_out_conv', 'new_m_w_out_attn', 'new_m_w_o', 'new_m_g_ffn2', 'new_m_w_gu2', 'new_m_w_down2', 'new_v_g_ffn1', 'new_v_w_gu1', 'new_v_w_down1', 'new_v_g_mix', 'new_v_w_in', 'new_v_conv_w', 'new_v_q_norm_g', 'new_v_k_norm_g', 'new_v_sinks', 'new_v_w_out_conv', 'new_v_w_out_attn', 'new_v_w_o', 'new_v_g_ffn2', 'new_v_w_gu2', 'new_v_w_down2']
TWIN_LEAF_KINDS = {'loss': 'loss', 'grad_x': 'grad_x', 'grad_g_ffn1': 'grad_w', 'grad_w_gu1': 'grad_w', 'grad_w_down1': 'grad_w', 'grad_g_mix': 'grad_w', 'grad_w_in': 'grad_w', 'grad_conv_w': 'grad_w', 'grad_q_norm_g': 'grad_w', 'grad_k_norm_g': 'grad_w', 'grad_sinks': 'grad_w', 'grad_w_out_conv': 'grad_w', 'grad_w_out_attn': 'grad_w', 'grad_w_o': 'grad_w', 'grad_g_ffn2': 'grad_w', 'grad_w_gu2': 'grad_w', 'grad_w_down2': 'grad_w', 'delta_g_ffn1': 'delta_w', 'delta_w_gu1': 'delta_w', 'delta_w_down1': 'delta_w', 'delta_g_mix': 'delta_w', 'delta_w_in': 'delta_w', 'delta_conv_w': 'delta_w', 'delta_q_norm_g': 'delta_w', 'delta_k_norm_g': 'delta_w', 'delta_sinks': 'delta_w', 'delta_w_out_conv': 'delta_w', 'delta_w_out_attn': 'delta_w', 'delta_w_o': 'delta_w', 'delta_g_ffn2': 'delta_w', 'delta_w_gu2': 'delta_w', 'delta_w_down2': 'delta_w', 'new_m_g_ffn1': 'new_m', 'new_m_w_gu1': 'new_m', 'new_m_w_down1': 'new_m', 'new_m_g_mix': 'new_m', 'new_m_w_in': 'new_m', 'new_m_conv_w': 'new_m', 'new_m_q_norm_g': 'new_m', 'new_m_k_norm_g': 'new_m', 'new_m_sinks': 'new_m', 'new_m_w_out_conv': 'new_m', 'new_m_w_out_attn': 'new_m', 'new_m_w_o': 'new_m', 'new_m_g_ffn2': 'new_m', 'new_m_w_gu2': 'new_m', 'new_m_w_down2': 'new_m', 'new_v_g_ffn1': 'new_v', 'new_v_w_gu1': 'new_v', 'new_v_w_down1': 'new_v', 'new_v_g_mix': 'new_v', 'new_v_w_in': 'new_v', 'new_v_conv_w': 'new_v', 'new_v_q_norm_g': 'new_v', 'new_v_k_norm_g': 'new_v', 'new_v_sinks': 'new_v', 'new_v_w_out_conv': 'new_v', 'new_v_w_out_attn': 'new_v', 'new_v_w_o': 'new_v', 'new_v_g_ffn2': 'new_v', 'new_v_w_gu2': 'new_v', 'new_v_w_down2': 'new_v'}


def _forward(args):
    return _fwd_reference(*[args[k] for k in FWD_PARAMS])


def _output_shape():
    out = _jax.eval_shape(lambda: _forward(_fwd_setup_inputs(0)))
    return out.shape, out.dtype

N_MICROBATCH = 1
ADAM_LR = 0.001
ADAM_B1 = 0.9
ADAM_B2 = 0.999
ADAM_EPS = 1e-08
ADAM_WD = 0.01
ADAM_STEP = 10
PER_EXAMPLE_BATCH_AXIS = {'x': 0, 'loss_target': 0}
SHARED_INPUTS = []
_WEIGHT_DTYPES = {'g_ffn1': _jnp.float32, 'w_gu1': _jnp.float32, 'w_down1': _jnp.float32, 'g_mix': _jnp.float32, 'w_in': _jnp.float32, 'conv_w': _jnp.float32, 'q_norm_g': _jnp.float32, 'k_norm_g': _jnp.float32, 'sinks': _jnp.float32, 'w_out_conv': _jnp.float32, 'w_out_attn': _jnp.float32, 'w_o': _jnp.float32, 'g_ffn2': _jnp.float32, 'w_gu2': _jnp.float32, 'w_down2': _jnp.float32}
MOMENT_SCALE = {'g_ffn1': 1.520576e+00, 'w_gu1': 3.943362e-02, 'w_down1': 6.487249e-02, 'g_mix': 7.239171e+00, 'w_in': 1.052690e-01, 'conv_w': 2.672583e+00, 'q_norm_g': 1.006711e+00, 'k_norm_g': 1.007793e+00, 'sinks': 1.556668e-01, 'w_out_conv': 1.090339e-01, 'w_out_attn': 1.289611e-02, 'w_o': 8.904032e-02, 'g_ffn2': 1.538359e+00, 'w_gu2': 2.051380e-02, 'w_down2': 3.391442e-02}


def _to_microbatches(a, axis):
    t = _jnp.moveaxis(a, axis, 0)
    t = t.reshape((N_MICROBATCH, t.shape[0] // N_MICROBATCH) + t.shape[1:])
    return _jnp.moveaxis(t, 1, axis + 1)


def setup_inputs(seed: int = 0) -> dict:
    inp = _fwd_setup_inputs(seed)
    key = _jax.random.fold_in(_jax.random.key(seed), 7919)
    shape, _ = _output_shape()
    out = dict(inp)
    out["loss_target"] = _jax.random.normal(_jax.random.fold_in(key, 0), shape, _jnp.float32)
    for i, name in enumerate(TWIN_WEIGHTS):
        w = inp[name].astype(_jnp.float32)
        if MOMENT_SCALE is None:
            s = _jnp.sqrt(_jnp.mean(_jnp.square(w)) + 1e-30)
        else:
            s = MOMENT_SCALE[name]
        km, kv = _jax.random.split(_jax.random.fold_in(key, i + 1))
        out[name] = w
        out["m_" + name] = s * _jax.random.normal(km, w.shape, _jnp.float32)
        out["v_" + name] = (s * s) * _jax.random.uniform(kv, w.shape, _jnp.float32, 0.5, 1.5)
    if N_MICROBATCH > 1:
        for name, axis in PER_EXAMPLE_BATCH_AXIS.items():
            out[name] = _to_microbatches(out[name], axis)
    return {'x': out['x'], 'g_ffn1': out['g_ffn1'], 'w_gu1': out['w_gu1'], 'w_down1': out['w_down1'], 'g_mix': out['g_mix'], 'w_in': out['w_in'], 'conv_w': out['conv_w'], 'q_norm_g': out['q_norm_g'], 'k_norm_g': out['k_norm_g'], 'sinks': out['sinks'], 'w_out_conv': out['w_out_conv'], 'w_out_attn': out['w_out_attn'], 'w_o': out['w_o'], 'g_ffn2': out['g_ffn2'], 'w_gu2': out['w_gu2'], 'w_down2': out['w_down2'], 'loss_target': out['loss_target'], 'm_g_ffn1': out['m_g_ffn1'], 'm_w_gu1': out['m_w_gu1'], 'm_w_down1': out['m_w_down1'], 'm_g_mix': out['m_g_mix'], 'm_w_in': out['m_w_in'], 'm_conv_w': out['m_conv_w'], 'm_q_norm_g': out['m_q_norm_g'], 'm_k_norm_g': out['m_k_norm_g'], 'm_sinks': out['m_sinks'], 'm_w_out_conv': out['m_w_out_conv'], 'm_w_out_attn': out['m_w_out_attn'], 'm_w_o': out['m_w_o'], 'm_g_ffn2': out['m_g_ffn2'], 'm_w_gu2': out['m_w_gu2'], 'm_w_down2': out['m_w_down2'], 'v_g_ffn1': out['v_g_ffn1'], 'v_w_gu1': out['v_w_gu1'], 'v_w_down1': out['v_w_down1'], 'v_g_mix': out['v_g_mix'], 'v_w_in': out['v_w_in'], 'v_conv_w': out['v_conv_w'], 'v_q_norm_g': out['v_q_norm_g'], 'v_k_norm_g': out['v_k_norm_g'], 'v_sinks': out['v_sinks'], 'v_w_out_conv': out['v_w_out_conv'], 'v_w_out_attn': out['v_w_out_attn'], 'v_w_o': out['v_w_o'], 'v_g_ffn2': out['v_g_ffn2'], 'v_w_gu2': out['v_w_gu2'], 'v_w_down2': out['v_w_down2']}


def _loss(weights, diff, rest, loss_target):
    with _jax.named_scope("forward"):
        args = {**rest, TWIN_DIFF_INPUT: diff, **{k: w.astype(_WEIGHT_DTYPES[k]) for k, w in weights.items()}}
        y = _forward(args)
    with _jax.named_scope("loss_head"):
        err = _jnp.square(y.astype(_jnp.float32) - loss_target)
        return 0.5 * _jnp.sum(_jnp.mean(err, axis=-1)) if err.ndim else 0.5 * err


def _adamw(w, g, m, v):
    m = ADAM_B1 * m + (1.0 - ADAM_B1) * g
    v = ADAM_B2 * v + (1.0 - ADAM_B2) * _jnp.square(g)
    m_hat = m / (1.0 - ADAM_B1 ** ADAM_STEP)
    v_hat = v / (1.0 - ADAM_B2 ** ADAM_STEP)
    delta = -ADAM_LR * (m_hat / (_jnp.sqrt(v_hat) + ADAM_EPS) + ADAM_WD * w)
    return delta, m, v


def reference(x, g_ffn1, w_gu1, w_down1, g_mix, w_in, conv_w, q_norm_g, k_norm_g, sinks, w_out_conv, w_out_attn, w_o, g_ffn2, w_gu2, w_down2, loss_target, m_g_ffn1, m_w_gu1, m_w_down1, m_g_mix, m_w_in, m_conv_w, m_q_norm_g, m_k_norm_g, m_sinks, m_w_out_conv, m_w_out_attn, m_w_o, m_g_ffn2, m_w_gu2, m_w_down2, v_g_ffn1, v_w_gu1, v_w_down1, v_g_mix, v_w_in, v_conv_w, v_q_norm_g, v_k_norm_g, v_sinks, v_w_out_conv, v_w_out_attn, v_w_o, v_g_ffn2, v_w_gu2, v_w_down2):
    given = dict(x=x, g_ffn1=g_ffn1, w_gu1=w_gu1, w_down1=w_down1, g_mix=g_mix, w_in=w_in, conv_w=conv_w, q_norm_g=q_norm_g, k_norm_g=k_norm_g, sinks=sinks, w_out_conv=w_out_conv, w_out_attn=w_out_attn, w_o=w_o, g_ffn2=g_ffn2, w_gu2=w_gu2, w_down2=w_down2, loss_target=loss_target, m_g_ffn1=m_g_ffn1, m_w_gu1=m_w_gu1, m_w_down1=m_w_down1, m_g_mix=m_g_mix, m_w_in=m_w_in, m_conv_w=m_conv_w, m_q_norm_g=m_q_norm_g, m_k_norm_g=m_k_norm_g, m_sinks=m_sinks, m_w_out_conv=m_w_out_conv, m_w_out_attn=m_w_out_attn, m_w_o=m_w_o, m_g_ffn2=m_g_ffn2, m_w_gu2=m_w_gu2, m_w_down2=m_w_down2, v_g_ffn1=v_g_ffn1, v_w_gu1=v_w_gu1, v_w_down1=v_w_down1, v_g_mix=v_g_mix, v_w_in=v_w_in, v_conv_w=v_conv_w, v_q_norm_g=v_q_norm_g, v_k_norm_g=v_k_norm_g, v_sinks=v_sinks, v_w_out_conv=v_w_out_conv, v_w_out_attn=v_w_out_attn, v_w_o=v_w_o, v_g_ffn2=v_g_ffn2, v_w_gu2=v_w_gu2, v_w_down2=v_w_down2)
    weights = {n: given[n] for n in TWIN_WEIGHTS}
    shared = {n: given[n] for n in SHARED_INPUTS}
    per_example = {n: given[n] for n in ['x']}
    grad_fn = _jax.value_and_grad(_loss, argnums=(0, 1))

    def one_microbatch(ex, loss_target):
        ex = dict(ex)
        diff = ex.pop(TWIN_DIFF_INPUT)
        return grad_fn(weights, diff, {**shared, **ex}, loss_target)

    if N_MICROBATCH == 1:
        loss, (grad_w, grad_x) = one_microbatch(per_example, given["loss_target"])
    else:
        def body(carry, xs):
            loss_sum, grad_sum = carry
            l_k, (gw_k, gx_k) = one_microbatch(xs[0], xs[1])
            with _jax.named_scope("update"):
                return (loss_sum + l_k, _jax.tree.map(_jnp.add, grad_sum, gw_k)), gx_k

        init = (_jnp.zeros((), _jnp.float32), _jax.tree.map(_jnp.zeros_like, weights))
        (loss, grad_w), grad_x = _jax.lax.scan(body, init, (per_example, given["loss_target"]))
    with _jax.named_scope("update"):
        delta_w, new_m, new_v = {}, {}, {}
        for n in TWIN_WEIGHTS:
            delta_w[n], new_m[n], new_v[n] = _adamw(weights[n], grad_w[n], given["m_" + n], given["v_" + n])
    return (loss, grad_x, *[grad_w[n] for n in TWIN_WEIGHTS], *[delta_w[n] for n in TWIN_WEIGHTS],
            *[new_m[n] for n in TWIN_WEIGHTS], *[new_v[n] for n in TWIN_WEIGHTS])
```

```python
import functools
import math

import numpy as np
import jax
import jax.numpy as jnp
from jax import lax
from jax.experimental import pallas as pl
from jax.experimental.pallas import tpu as pltpu

F32 = jnp.float32
BF16 = jnp.bfloat16
MESH = pl.DeviceIdType.MESH

RMS_EPS = 1e-6
BLOCK = 128
ROPE_THETA = 500000.0
NEG_INF = -1e30
CONV_K = 3
ADAM_LR, ADAM_B1, ADAM_B2, ADAM_EPS, ADAM_WD, ADAM_STEP = 0.001, 0.9, 0.999, 1e-08, 0.01, 10

VMEM_LIMIT_V7X = 48 * 1024 * 1024
LANES = 128
N_CHIPS = 4
N_DEV = 8
PACK_ROWS = 16


def _tile(n, want, align=LANES):
    best = None
    t = align
    while t <= min(n, want):
        if n % t == 0:
            best = t
        t += align
    return best or n


def _cparams(sem):
    return pltpu.CompilerParams(dimension_semantics=sem, vmem_limit_bytes=VMEM_LIMIT_V7X)


def _sigmoid(x):
    return 1.0 / (1.0 + jnp.exp(-x))


def _mm(name, grid, ins, in_specs, pairs, dims, acc_shape, out_shape, out_specs, epilogue):
    nk = grid[-1]
    n_in = len(ins)
    n_acc = len(pairs)
    multi = isinstance(out_shape, (tuple, list))
    n_out = len(out_shape) if multi else 1

    def body(*refs):
        in_refs = refs[:n_in]
        out_refs = refs[n_in:n_in + n_out]
        accs = refs[n_in + n_out:]
        k = pl.program_id(len(grid) - 1)

        @pl.when(k == 0)
        def _():
            for acc in accs:
                acc[...] = jnp.zeros_like(acc)

        for acc, (ai, bi) in zip(accs, pairs):
            acc[...] += lax.dot_general(in_refs[ai][...], in_refs[bi][...], dims, preferred_element_type=F32)

        @pl.when(k == nk - 1)
        def _():
            epilogue([acc[...] for acc in accs], in_refs, out_refs)

    sem = ("parallel",) * (len(grid) - 1) + ("arbitrary",)
    return pl.pallas_call(
        body, name=name, grid=grid, in_specs=in_specs, out_specs=out_specs, out_shape=out_shape,
        scratch_shapes=[pltpu.VMEM(acc_shape, F32) for _ in range(n_acc)], compiler_params=_cparams(sem),
    )(*ins)


NN = (((1,), (0,)), ((), ()))
NT = (((1,), (1,)), ((), ()))
TN = (((0,), (0,)), ((), ()))


def _ffn_up(name, h, wgu3):
    S, D = h.shape
    Ns = wgu3.shape[2]
    F = 2 * Ns
    tm, tn, tk = _tile(S, 512), _tile(Ns, 1408), _tile(D, 512)
    nbs = Ns // tn

    def epi(accs, in_refs, out_refs):
        g, u = accs
        gu_ref, a_ref = out_refs
        gu_ref[0] = g.astype(BF16)
        gu_ref[1] = u.astype(BF16)
        a_ref[...] = (g * _sigmoid(g) * u).astype(BF16)

    return _mm(
        name, (S // tm, F // tn, D // tk), (h, wgu3, wgu3),
        [pl.BlockSpec((tm, tk), lambda i, j, k: (i, k)),
         pl.BlockSpec((None, tk, tn), lambda i, j, k: (j // nbs, k, j % nbs)),
         pl.BlockSpec((None, tk, tn), lambda i, j, k: (2 + j // nbs, k, j % nbs))],
        [(0, 1), (0, 2)], NN, (tm, tn),
        (jax.ShapeDtypeStruct((2, S, F), BF16), jax.ShapeDtypeStruct((S, F), BF16)),
        (pl.BlockSpec((2, tm, tn), lambda i, j, k: (0, i, j)), pl.BlockSpec((tm, tn), lambda i, j, k: (i, j))),
        epi)


def _mm_res(name, a, w, res, scale):
    S, K = a.shape
    N = w.shape[1]
    tm, tn, tk = _tile(S, 1024), _tile(N, 1024), _tile(K, 512)

    def epi(accs, in_refs, out_refs):
        out_refs[0][...] = in_refs[2][...] + scale * accs[0]

    return _mm(
        name, (S // tm, N // tn, K // tk), (a, w, res),
        [pl.BlockSpec((tm, tk), lambda i, j, k: (i, k)), pl.BlockSpec((tk, tn), lambda i, j, k: (k, j)),
         pl.BlockSpec((tm, tn), lambda i, j, k: (i, j))],
        [(0, 1)], NN, (tm, tn), jax.ShapeDtypeStruct((S, N), F32), pl.BlockSpec((tm, tn), lambda i, j, k: (i, j)), epi)


def _mm_cols(name, a, w3, out_dtype):
    S, K = a.shape
    Ns = w3.shape[2]
    tm, tn, tk = _tile(S, 512), _tile(Ns, 2304), _tile(K, 512)
    nbs = Ns // tn

    def epi(accs, in_refs, out_refs):
        out_refs[0][...] = accs[0].astype(out_dtype)

    return _mm(
        name, (S // tm, N_CHIPS * nbs, K // tk), (a, w3),
        [pl.BlockSpec((tm, tk), lambda i, j, k: (i, k)),
         pl.BlockSpec((None, tk, tn), lambda i, j, k: (j // nbs, k, j % nbs))],
        [(0, 1)], NN, (tm, tn), jax.ShapeDtypeStruct((S, N_CHIPS * Ns), out_dtype),
        pl.BlockSpec((tm, tn), lambda i, j, k: (i, j)), epi)


def _mm_nt(name, a, w, out_dtype, scale=1.0):
    S, N = a.shape
    K = w.shape[0]
    tm, tn, tk = _tile(S, 1024), _tile(K, 1024), _tile(N, 512)

    def epi(accs, in_refs, out_refs):
        out_refs[0][...] = (scale * accs[0]).astype(out_dtype)

    return _mm(
        name, (S // tm, K // tn, N // tk), (a, w),
        [pl.BlockSpec((tm, tk), lambda i, j, k: (i, k)), pl.BlockSpec((tn, tk), lambda i, j, k: (j, k))],
        [(0, 1)], NT, (tm, tn), jax.ShapeDtypeStruct((S, K), out_dtype), pl.BlockSpec((tm, tn), lambda i, j, k: (i, j)), epi)


def _ffn_down_bwd(name, dy, wd, gu, scale):
    S, D = dy.shape
    F = wd.shape[0]
    tm, tn, tk = _tile(S, 512), _tile(F, 1408), _tile(D, 512)

    def epi(accs, in_refs, out_refs):
        da = scale * accs[0]
        g = in_refs[2][0].astype(F32)
        u = in_refs[2][1].astype(F32)
        sg = _sigmoid(g)
        out_refs[0][0] = (da * u * (sg * (1.0 + g * (1.0 - sg)))).astype(BF16)
        out_refs[0][1] = (da * (g * sg)).astype(BF16)

    return _mm(
        name, (S // tm, F // tn, D // tk), (dy, wd, gu),
        [pl.BlockSpec((tm, tk), lambda i, j, k: (i, k)), pl.BlockSpec((tn, tk), lambda i, j, k: (j, k)),
         pl.BlockSpec((2, tm, tn), lambda i, j, k: (0, i, j))],
        [(0, 1)], NT, (tm, tn), jax.ShapeDtypeStruct((2, S, F), BF16),
        pl.BlockSpec((2, tm, tn), lambda i, j, k: (0, i, j)), epi)


def _mm_nt_cols(name, a, w3, a_is_gu=False):
    K, Ns = w3.shape[1], w3.shape[2]
    S = a.shape[1] if a_is_gu else a.shape[0]
    tm, tn, tk = _tile(S, 1024), _tile(K, 1024), _tile(Ns, 2304)
    nbs = Ns // tk
    if a_is_gu:
        a_spec = pl.BlockSpec((None, tm, tk), lambda i, j, k: (k // (2 * nbs), i, k % (2 * nbs)))
    else:
        a_spec = pl.BlockSpec((tm, tk), lambda i, j, k: (i, k))

    def epi(accs, in_refs, out_refs):
        out_refs[0][...] = accs[0]

    return _mm(
        name, (S // tm, K // tn, N_CHIPS * nbs), (a, w3),
        [a_spec, pl.BlockSpec((None, tn, tk), lambda i, j, k: (k // nbs, j, k % nbs))],
        [(0, 1)], NT, (tm, tn), jax.ShapeDtypeStruct((S, K), F32), pl.BlockSpec((tm, tn), lambda i, j, k: (i, j)), epi)


def _mm_tn(name, a, b, scale=1.0):
    S, K = a.shape
    N = b.shape[1]
    tm, tn, tk = _tile(K, 512), _tile(N, 1024), _tile(S, 512)

    def epi(accs, in_refs, out_refs):
        out_refs[0][...] = scale * accs[0]

    return _mm(
        name, (K // tm, N // tn, S // tk), (a, b),
        [pl.BlockSpec((tk, tm), lambda i, j, k: (k, i)), pl.BlockSpec((tk, tn), lambda i, j, k: (k, j))],
        [(0, 1)], TN, (tm, tn), jax.ShapeDtypeStruct((K, N), F32), pl.BlockSpec((tm, tn), lambda i, j, k: (i, j)), epi)


def _mm_tn_cols(name, a, b, Ns, b_is_gu=False):
    S, K = a.shape
    tm, tn, tk = _tile(K, 512), _tile(Ns, 1408), _tile(S, 512)
    nbs = Ns // tn
    if b_is_gu:
        b_spec = pl.BlockSpec((None, tk, tn), lambda i, j, k: (j // (2 * nbs), k, j % (2 * nbs)))
    else:
        b_spec = pl.BlockSpec((tk, tn), lambda i, j, k: (k, j))

    def epi(accs, in_refs, out_refs):
        out_refs[0][...] = accs[0]

    return _mm(
        name, (K // tm, N_CHIPS * nbs, S // tk), (a, b),
        [pl.BlockSpec((tk, tm), lambda i, j, k: (k, i)), b_spec],
        [(0, 1)], TN, (tm, tn), jax.ShapeDtypeStruct((N_CHIPS, K, Ns), F32),
        pl.BlockSpec((None, tm, tn), lambda i, j, k: (j // nbs, i, j % nbs)), epi)


def _rms_fwd(name, x, gain):
    S, D = x.shape
    tm = _tile(S, 256, 8)

    def body(x_ref, g_ref, h_ref):
        xv = x_ref[...]
        r = lax.rsqrt(jnp.mean(xv * xv, axis=-1, keepdims=True) + RMS_EPS)
        h_ref[...] = (xv * r * g_ref[...]).astype(BF16)

    return pl.pallas_call(
        body, name=name, grid=(S // tm,),
        in_specs=[pl.BlockSpec((tm, D), lambda i: (i, 0)), pl.BlockSpec((1, D), lambda i: (0, 0))],
        out_specs=pl.BlockSpec((tm, D), lambda i: (i, 0)), out_shape=jax.ShapeDtypeStruct((S, D), BF16),
        compiler_params=_cparams(("parallel",)),
    )(x, gain)


def _rms_bwd(name, x, gain, dh, dres):
    S, D = x.shape
    tm = _tile(S, 256, 8)

    def body(x_ref, g_ref, dh_ref, dres_ref, dx_ref, dg_ref):
        i = pl.program_id(0)
        xv = x_ref[...]
        r = lax.rsqrt(jnp.mean(xv * xv, axis=-1, keepdims=True) + RMS_EPS)
        xhat = xv * r
        dhv = dh_ref[...]
        dxhat = dhv * g_ref[...]
        dx_ref[...] = dres_ref[...] + r * (dxhat - xhat * jnp.mean(dxhat * xhat, axis=-1, keepdims=True))

        @pl.when(i == 0)
        def _():
            dg_ref[...] = jnp.zeros_like(dg_ref)

        dg_ref[...] += jnp.sum(dhv * xhat, axis=0, keepdims=True)

    row = pl.BlockSpec((tm, D), lambda i: (i, 0))
    vec = pl.BlockSpec((1, D), lambda i: (0, 0))
    return pl.pallas_call(
        body, name=name, grid=(S // tm,), in_specs=[row, vec, row, row], out_specs=(row, vec),
        out_shape=(jax.ShapeDtypeStruct((S, D), F32), jax.ShapeDtypeStruct((1, D), F32)),
        compiler_params=_cparams(("arbitrary",)),
    )(x, gain, dh, dres)


def _loss_grad(name, y, target):
    S, D = y.shape
    tm = _tile(S, 256, 8)

    def body(y_ref, t_ref, dy_ref, l_ref):
        i = pl.program_id(0)
        e = y_ref[...] - t_ref[...]
        dy_ref[...] = e * (1.0 / D)
        col = jnp.sum(e * e, axis=0, keepdims=True)
        part = col[:, 0:LANES]
        for k in range(1, D // LANES):
            part = part + col[:, k * LANES:(k + 1) * LANES]

        @pl.when(i == 0)
        def _():
            l_ref[...] = jnp.zeros_like(l_ref)

        l_ref[...] += part

    row = pl.BlockSpec((tm, D), lambda i: (i, 0))
    return pl.pallas_call(
        body, name=name, grid=(S // tm,), in_specs=[row, row],
        out_specs=(row, pl.BlockSpec((1, LANES), lambda i: (0, 0))),
        out_shape=(jax.ShapeDtypeStruct((S, D), F32), jax.ShapeDtypeStruct((1, LANES), F32)),
        compiler_params=_cparams(("arbitrary",)),
    )(y, target)


def _shift_down(u, k):
    rows = lax.broadcasted_iota(jnp.int32, u.shape, 0)
    return jnp.where(rows >= k, pltpu.roll(u, k, 0), 0.0)


def _shift_up(u, k):
    n = u.shape[0]
    rows = lax.broadcasted_iota(jnp.int32, u.shape, 0)
    return jnp.where(rows < n - k, pltpu.roll(u, n - k, 0), 0.0)


def _conv_specs(S, cw, conv_width):
    nb = conv_width // cw
    col = lambda off: pl.BlockSpec((S, cw), lambda j, off=off: (0, off * nb + j))
    return nb, col(0), col(1), col(2)


def _conv_fwd(name, proj, convw3, conv_width):
    S = proj.shape[0]
    cw = convw3.shape[2]
    nb, xc_s, bg_s, cg_s = _conv_specs(S, cw, conv_width)

    def body(xc_ref, bg_ref, cg_ref, w_ref, o_ref):
        u = cg_ref[...] * xc_ref[...]
        w = w_ref[...]
        cv = w[2:3, :] * u + w[1:2, :] * _shift_down(u, 1) + w[0:1, :] * _shift_down(u, 2)
        o_ref[...] = (bg_ref[...] * cv).astype(BF16)

    return pl.pallas_call(
        body, name=name, grid=(nb,),
        in_specs=[xc_s, bg_s, cg_s, pl.BlockSpec((None, CONV_K, cw), lambda j: (j, 0, 0))],
        out_specs=pl.BlockSpec((S, cw), lambda j: (0, j)), out_shape=jax.ShapeDtypeStruct((S, conv_width), BF16),
        compiler_params=_cparams(("parallel",)),
    )(proj, proj, proj, convw3)


def _conv_bwd(name, proj, convw3, da, conv_width):
    S = proj.shape[0]
    cw = convw3.shape[2]
    nb, xc_s, bg_s, cg_s = _conv_specs(S, cw, conv_width)

    def body(xc_ref, bg_ref, cg_ref, w_ref, da_ref, dxc_ref, dbg_ref, dcg_ref, dw_ref):
        xc, cg = xc_ref[...], cg_ref[...]
        u = cg * xc
        w = w_ref[...]
        u1, u2 = _shift_down(u, 1), _shift_down(u, 2)
        cv = w[2:3, :] * u + w[1:2, :] * u1 + w[0:1, :] * u2
        dav = da_ref[...]
        dbg_ref[...] = (dav * cv).astype(BF16)
        dcv = dav * bg_ref[...]
        du = w[2:3, :] * dcv + w[1:2, :] * _shift_up(dcv, 1) + w[0:1, :] * _shift_up(dcv, 2)
        dxc_ref[...] = (du * cg).astype(BF16)
        dcg_ref[...] = (du * xc).astype(BF16)
        dw_ref[0:1, :] = jnp.sum(dcv * u2, axis=0, keepdims=True)
        dw_ref[1:2, :] = jnp.sum(dcv * u1, axis=0, keepdims=True)
        dw_ref[2:3, :] = jnp.sum(dcv * u, axis=0, keepdims=True)

    wspec = pl.BlockSpec((None, CONV_K, cw), lambda j: (j, 0, 0))
    ospec = pl.BlockSpec((S, cw), lambda j: (0, j))
    act = jax.ShapeDtypeStruct((S, conv_width), BF16)
    return pl.pallas_call(
        body, name=name, grid=(nb,), in_specs=[xc_s, bg_s, cg_s, wspec, ospec],
        out_specs=(ospec, ospec, ospec, wspec),
        out_shape=(act, act, act, jax.ShapeDtypeStruct(convw3.shape, F32)),
        compiler_params=_cparams(("parallel",)),
    )(proj, proj, proj, convw3, da)


def _rope_consts(S, dh):
    rot = dh // 4
    half = rot // 2
    inv_freq = 1.0 / (ROPE_THETA ** (jnp.arange(0, rot, 2, dtype=F32) / rot))
    ang = jnp.arange(S, dtype=F32)[:, None] * inv_freq[None, :]
    cos = jnp.concatenate([jnp.cos(ang), jnp.cos(ang), jnp.ones((S, dh - rot), F32)], axis=1)
    sin = jnp.concatenate([jnp.sin(ang), jnp.sin(ang), jnp.zeros((S, dh - rot), F32)], axis=1)
    rm = np.zeros((dh, dh), np.float32)
    for j in range(half):
        rm[j + half, j] = -1.0
        rm[j, j + half] = 1.0
    return cos, sin, jnp.asarray(rm, BF16), jnp.asarray(rm.T, BF16)


def _exact_perm(y, rm):
    hi = y.astype(BF16)
    r1 = y - hi.astype(F32)
    mid = r1.astype(BF16)
    lo = (r1 - mid.astype(F32)).astype(BF16)
    dot = lambda a: jnp.dot(a, rm, preferred_element_type=F32)
    return dot(hi) + dot(mid) + dot(lo)


def _qk_prep(name, xh, gain, cos, sin, rm):
    H, S, dh = xh.shape
    tm = _tile(S, 1024, 8)

    def body(x_ref, g_ref, c_ref, s_ref, rm_ref, o_ref):
        xv = x_ref[...]
        y = xv * lax.rsqrt(jnp.mean(xv * xv, axis=-1, keepdims=True) + RMS_EPS) * g_ref[...]
        o_ref[...] = (y * c_ref[...] + _exact_perm(y, rm_ref[...]) * s_ref[...]).astype(BF16)

    blk = pl.BlockSpec((None, tm, dh), lambda h, i: (h, i, 0))
    tab = pl.BlockSpec((tm, dh), lambda h, i: (i, 0))
    return pl.pallas_call(
        body, name=name, grid=(H, S // tm),
        in_specs=[blk, pl.BlockSpec((1, dh), lambda h, i: (0, 0)), tab, tab, pl.BlockSpec((dh, dh), lambda h, i: (0, 0))],
        out_specs=blk, out_shape=jax.ShapeDtypeStruct((H, S, dh), BF16),
        compiler_params=_cparams(("parallel", "parallel")),
    )(xh, gain, cos, sin, rm)


def _qk_prep_bwd(name, xh, gain, cos, sin, rmt, dout):
    H, S, dh = xh.shape
    tm = _tile(S, 1024, 8)

    def body(x_ref, g_ref, c_ref, s_ref, rmt_ref, do_ref, dx_ref, dg_ref):
        first = (pl.program_id(0) == 0) & (pl.program_id(1) == 0)
        xv = x_ref[...]
        r = lax.rsqrt(jnp.mean(xv * xv, axis=-1, keepdims=True) + RMS_EPS)
        xhat = xv * r
        dov = do_ref[...]
        dy = dov * c_ref[...] + _exact_perm(dov * s_ref[...], rmt_ref[...])
        dxhat = dy * g_ref[...]
        dx_ref[...] = (r * (dxhat - xhat * jnp.mean(dxhat * xhat, axis=-1, keepdims=True))).astype(BF16)

        @pl.when(first)
        def _():
            dg_ref[...] = jnp.zeros_like(dg_ref)

        dg_ref[...] += jnp.sum(dy * xhat, axis=0, keepdims=True)

    blk = pl.BlockSpec((None, tm, dh), lambda h, i: (h, i, 0))
    tab = pl.BlockSpec((tm, dh), lambda h, i: (i, 0))
    vec = pl.BlockSpec((1, dh), lambda h, i: (0, 0))
    return pl.pallas_call(
        body, name=name, grid=(H, S // tm),
        in_specs=[blk, vec, tab, tab, pl.BlockSpec((dh, dh), lambda h, i: (0, 0)), blk],
        out_specs=(blk, vec), out_shape=(jax.ShapeDtypeStruct((H, S, dh), BF16), jax.ShapeDtypeStruct((1, dh), F32)),
        compiler_params=_cparams(("arbitrary", "arbitrary")),
    )(xh, gain, cos, sin, rmt, dout)


def _attn_probs(q, kp, kc, sink_col, n, scale):
    rows = q.shape[0]
    sp = lax.dot_general(q, kp, NT, preferred_element_type=F32) * scale
    sc = lax.dot_general(q, kc, NT, preferred_element_type=F32) * scale
    qi = lax.broadcasted_iota(jnp.int32, (rows, BLOCK), 0) % BLOCK
    kj = lax.broadcasted_iota(jnp.int32, (rows, BLOCK), 1)
    sp = jnp.where((kj > qi) & (n > 0), sp, NEG_INF)
    sc = jnp.where(kj <= qi, sc, NEG_INF)
    m = jnp.maximum(jnp.maximum(jnp.max(sp, axis=-1, keepdims=True), jnp.max(sc, axis=-1, keepdims=True)), sink_col)
    pp, pc, ps = jnp.exp(sp - m), jnp.exp(sc - m), jnp.exp(sink_col - m)
    inv = 1.0 / (jnp.sum(pp, axis=-1, keepdims=True) + jnp.sum(pc, axis=-1, keepdims=True) + ps)
    return pp * inv, pc * inv, ps * inv


def _sink_col(sink_ref, hk, group):
    rows = group * BLOCK
    g = lax.broadcasted_iota(jnp.int32, (rows, 1), 0) // BLOCK
    col = jnp.zeros((rows, 1), F32)
    for i in range(group):
        col = jnp.where(g == i, sink_ref[hk * group + i], col)
    return col


def _attn_specs(group, dh):
    qb = pl.BlockSpec((group, BLOCK, dh), lambda hk, n: (hk, n, 0))
    prev = pl.BlockSpec((None, BLOCK, dh), lambda hk, n: (hk, jnp.maximum(n - 1, 0), 0))
    cur = pl.BlockSpec((None, BLOCK, dh), lambda hk, n: (hk, n, 0))
    return qb, prev, cur, pl.BlockSpec(memory_space=pltpu.SMEM)


def _attn_fwd(name, q, k, v, sinks):
    HQ, S, dh = q.shape
    HKV = k.shape[0]
    group = HQ // HKV
    scale = dh ** -0.5
    qb, prev, cur, smem = _attn_specs(group, dh)

    def body(q_ref, kp_ref, kc_ref, vp_ref, vc_ref, sink_ref, o_ref):
        hk, n = pl.program_id(0), pl.program_id(1)
        qv = q_ref[...].reshape(group * BLOCK, dh)
        pp, pc, _ = _attn_probs(qv, kp_ref[...], kc_ref[...], _sink_col(sink_ref, hk, group), n, scale)
        o = jnp.dot(pp.astype(BF16), vp_ref[...], preferred_element_type=F32)
        o = o + jnp.dot(pc.astype(BF16), vc_ref[...], preferred_element_type=F32)
        o_ref[...] = o.reshape(group, BLOCK, dh).astype(BF16)

    return pl.pallas_call(
        body, name=name, grid=(HKV, S // BLOCK), in_specs=[qb, prev, cur, prev, cur, smem], out_specs=qb,
        out_shape=jax.ShapeDtypeStruct((HQ, S, dh), BF16), compiler_params=_cparams(("parallel", "parallel")),
    )(q, k, k, v, v, sinks)


def _attn_bwd(name, q, k, v, sinks, do):
    HQ, S, dh = q.shape
    HKV = k.shape[0]
    group = HQ // HKV
    scale = dh ** -0.5
    qb, prev, cur, smem = _attn_specs(group, dh)
    whole = pl.BlockSpec((None, S, dh), lambda hk, n: (hk, 0, 0))
    sk = pl.BlockSpec((None, group, LANES), lambda hk, n: (hk, 0, 0))

    def body(q_ref, kp_ref, kc_ref, vp_ref, vc_ref, sink_ref, do_ref, dq_ref, dk_ref, dv_ref, ds_ref):
        hk, n = pl.program_id(0), pl.program_id(1)
        rows = group * BLOCK
        qv = q_ref[...].reshape(rows, dh)
        dov = do_ref[...].reshape(rows, dh)
        kp, kc, vp, vc = kp_ref[...], kc_ref[...], vp_ref[...], vc_ref[...]
        pp, pc, ps = _attn_probs(qv, kp, kc, _sink_col(sink_ref, hk, group), n, scale)
        dpp = lax.dot_general(dov, vp, NT, preferred_element_type=F32)
        dpc = lax.dot_general(dov, vc, NT, preferred_element_type=F32)
        delta = jnp.sum(pp * dpp, axis=-1, keepdims=True) + jnp.sum(pc * dpc, axis=-1, keepdims=True)
        dsp = (pp * (dpp - delta) * scale).astype(BF16)
        dsc = (pc * (dpc - delta) * scale).astype(BF16)
        dq = jnp.dot(dsp, kp, preferred_element_type=F32) + jnp.dot(dsc, kc, preferred_element_type=F32)
        dq_ref[...] = dq.reshape(group, BLOCK, dh)

        @pl.when(n == 0)
        def _():
            dk_ref[...] = jnp.zeros_like(dk_ref)
            dv_ref[...] = jnp.zeros_like(dv_ref)
            ds_ref[...] = jnp.zeros_like(ds_ref)

        cur_rows = pl.ds(pl.multiple_of(n * BLOCK, BLOCK), BLOCK)
        prev_rows = pl.ds(pl.multiple_of(jnp.maximum(n - 1, 0) * BLOCK, BLOCK), BLOCK)
        tdot = lambda a, b: lax.dot_general(a, b, TN, preferred_element_type=F32)
        dk_ref[prev_rows, :] += tdot(dsp, qv)
        dv_ref[prev_rows, :] += tdot(pp.astype(BF16), dov)
        dk_ref[cur_rows, :] += tdot(dsc, qv)
        dv_ref[cur_rows, :] += tdot(pc.astype(BF16), dov)
        dsink = -jnp.sum((ps * delta).reshape(group, BLOCK, 1), axis=1)
        ds_ref[...] += jnp.broadcast_to(dsink, (group, LANES))

    return pl.pallas_call(
        body, name=name, grid=(HKV, S // BLOCK), in_specs=[qb, prev, cur, prev, cur, smem, qb],
        out_specs=(qb, whole, whole, sk),
        out_shape=(jax.ShapeDtypeStruct((HQ, S, dh), F32), jax.ShapeDtypeStruct((HKV, S, dh), F32),
                   jax.ShapeDtypeStruct((HKV, S, dh), F32), jax.ShapeDtypeStruct((HKV, group, LANES), F32)),
        compiler_params=_cparams(("arbitrary", "arbitrary")),
    )(q, k, k, v, v, sinks, do)


def _gate_specs(S, D, ga_off, gb_off):
    tg = LANES
    for t in range(LANES, 513, LANES):
        if D % t == 0 and ga_off % t == 0 and gb_off % t == 0:
            tg = t
    if D % LANES:
        tg = math.gcd(math.gcd(D, ga_off), gb_off)
    tm = _tile(S, 512, 8)
    act = pl.BlockSpec((tm, tg), lambda i, j: (i, j))
    ga = pl.BlockSpec((tm, tg), lambda i, j: (i, ga_off // tg + j))
    gb = pl.BlockSpec((tm, tg), lambda i, j: (i, gb_off // tg + j))
    return (S // tm, D // tg), act, ga, gb


def _gate_fwd(name, proj, ya, yb, ga_off, gb_off):
    S, D = ya.shape
    grid, act, ga, gb = _gate_specs(S, D, ga_off, gb_off)

    def body(ga_ref, gb_ref, ya_ref, yb_ref, o_ref):
        o_ref[...] = (_sigmoid(ga_ref[...]) * ya_ref[...] + _sigmoid(gb_ref[...]) * yb_ref[...]).astype(BF16)

    return pl.pallas_call(
        body, name=name, grid=grid, in_specs=[ga, gb, act, act], out_specs=act,
        out_shape=jax.ShapeDtypeStruct((S, D), BF16), compiler_params=_cparams(("parallel", "parallel")),
    )(proj, proj, ya, yb)


def _gate_bwd(name, proj, ya, yb, dm, ga_off, gb_off):
    S, D = ya.shape
    grid, act, ga, gb = _gate_specs(S, D, ga_off, gb_off)

    def body(ga_ref, gb_ref, ya_ref, yb_ref, dm_ref, dga_ref, dgb_ref, dya_ref, dyb_ref):
        dmv = dm_ref[...]
        sa, sb = _sigmoid(ga_ref[...]), _sigmoid(gb_ref[...])
        dga_ref[...] = (dmv * ya_ref[...] * sa * (1.0 - sa)).astype(BF16)
        dgb_ref[...] = (dmv * yb_ref[...] * sb * (1.0 - sb)).astype(BF16)
        dya_ref[...] = (dmv * sa).astype(BF16)
        dyb_ref[...] = (dmv * sb).astype(BF16)

    o = jax.ShapeDtypeStruct((S, D), BF16)
    return pl.pallas_call(
        body, name=name, grid=grid, in_specs=[ga, gb, act, act, act], out_specs=(act, act, act, act),
        out_shape=(o, o, o, o), compiler_params=_cparams(("parallel", "parallel")),
    )(proj, proj, ya, yb, dm)


def _row_tile(rows, cols, n_arrays):
    want = max(8, (VMEM_LIMIT_V7X // 2) // (2 * n_arrays * cols * 4))
    return _tile(rows, want, 8)


def _add_half(name, g3, r3, c_arr):
    n, h, C = r3.shape
    tr = _row_tile(h, C, 3)
    nb = h // tr

    def body(c_ref, g_ref, r_ref, o_ref):
        o_ref[...] = g_ref[...] + r_ref[...]

    blk = pl.BlockSpec((None, tr, C), lambda s, i, c_ref: (s, i, 0))
    return pl.pallas_call(
        body, name=name,
        grid_spec=pltpu.PrefetchScalarGridSpec(
            num_scalar_prefetch=1, grid=(n, nb),
            in_specs=[pl.BlockSpec((None, tr, C), lambda s, i, c_ref: (s, c_ref[0] * nb + i, 0)), blk], out_specs=blk),
        out_shape=jax.ShapeDtypeStruct(r3.shape, F32), compiler_params=_cparams(("parallel", "parallel")),
    )(c_arr, g3, r3)


def _add_chips(name, r3):
    n, h, C = r3.shape
    tr = _row_tile(h, C, 5)

    def body(a_ref, b_ref, c_ref, d_ref, o_ref):
        o_ref[...] = ((a_ref[...] + b_ref[...]) + c_ref[...]) + d_ref[...]

    part = lambda s: pl.BlockSpec((None, tr, C), lambda i, s=s: (s, i, 0))
    return pl.pallas_call(
        body, name=name, grid=(h // tr,), in_specs=[part(0), part(1), part(2), part(3)],
        out_specs=pl.BlockSpec((tr, C), lambda i: (i, 0)), out_shape=jax.ShapeDtypeStruct((h, C), F32),
        compiler_params=_cparams(("parallel",)),
    )(r3, r3, r3, r3)


def _adamw(name, w, g, m, v):
    R, C = w.shape
    tr = _row_tile(R, C, 7)
    c1 = 1.0 - ADAM_B1 ** ADAM_STEP
    c2 = 1.0 - ADAM_B2 ** ADAM_STEP

    def body(w_ref, g_ref, m_ref, v_ref, d_ref, nm_ref, nv_ref):
        gv = g_ref[...]
        nm = ADAM_B1 * m_ref[...] + (1.0 - ADAM_B1) * gv
        nv = ADAM_B2 * v_ref[...] + (1.0 - ADAM_B2) * (gv * gv)
        d_ref[...] = -ADAM_LR * ((nm / c1) / (jnp.sqrt(nv / c2) + ADAM_EPS) + ADAM_WD * w_ref[...])
        nm_ref[...] = nm
        nv_ref[...] = nv

    blk = pl.BlockSpec((tr, C), lambda i: (i, 0))
    o = jax.ShapeDtypeStruct((R, C), F32)
    return pl.pallas_call(
        body, name=name, grid=(R // tr,), in_specs=[blk, blk, blk, blk], out_specs=(blk, blk, blk),
        out_shape=(o, o, o), compiler_params=_cparams(("parallel",)),
    )(w, g, m, v)


def _place():
    x, y, c = lax.axis_index("x"), lax.axis_index("y"), lax.axis_index("c")
    chips = [(1 - x, y), (x, 1 - y), (1 - x, 1 - y)]
    return x, y, c, 2 * x + y, chips


ANY = pl.BlockSpec(memory_space=pl.ANY)


def _gather_weights(name, shards):
    n = len(shards)
    halved = [s.shape[0] % 16 == 0 for s in shards]

    def body(*refs):
        ins, outs = refs[:n], refs[n:2 * n]
        send, recv, lsem = refs[2 * n:]
        x, y, c, p, chips = _place()
        sib = (x, y, 1 - c)

        def rdma(k, src, dst, to):
            return pltpu.make_async_remote_copy(src_ref=src, dst_ref=dst, send_sem=send.at[k], recv_sem=recv.at[k],
                                                device_id=to, device_id_type=MESH)

        started, local = [], []
        for i in range(n):
            src, dst = ins[i], outs[i]
            cp = pltpu.make_async_copy(src, dst.at[p], lsem.at[i])
            cp.start()
            local.append(cp)
            if halved[i]:
                h = src.shape[0] // 2
                mine = pl.ds(pl.multiple_of(c * h, 16), h)
                for j, chip in enumerate(chips):
                    cp = rdma(6 * i + j, src.at[mine], dst.at[p, mine], (*chip, c))
                    cp.start()
                    started.append(cp)
            else:
                for j, chip in enumerate(chips):
                    cp = rdma(6 * i + j, src, dst.at[p], (*chip, c))
                    cp.start()
                    started.append(cp)
        for i in range(n):
            src, dst = ins[i], outs[i]
            for j, (cx, cy) in enumerate(chips):
                q = 2 * cx + cy
                if halved[i]:
                    h = src.shape[0] // 2
                    mine = pl.ds(pl.multiple_of(c * h, 16), h)
                    rdma(6 * i + j, dst.at[q, mine], dst.at[q, mine], (cx, cy, c)).wait_recv()
                    cp = rdma(6 * i + 3 + j, dst.at[q, mine], dst.at[q, mine], sib)
                    cp.start()
                    started.append(cp)
                else:
                    rdma(6 * i + j, dst.at[q], dst.at[q], (cx, cy, c)).wait_recv()
        for i in range(n):
            if halved[i]:
                dst = outs[i]
                h = dst.shape[1] // 2
                other = pl.ds(pl.multiple_of((1 - c) * h, 16), h)
                for j, (cx, cy) in enumerate(chips):
                    q = 2 * cx + cy
                    rdma(6 * i + 3 + j, dst.at[q, other], dst.at[q, other], sib).wait_recv()
        for cp in started:
            cp.wait_send()
        for cp in local:
            cp.wait()

    return pl.pallas_call(
        body, name=name, in_specs=[ANY] * n, out_specs=tuple([ANY] * n),
        out_shape=tuple(jax.ShapeDtypeStruct((N_CHIPS,) + s.shape, s.dtype) for s in shards),
        scratch_shapes=[pltpu.SemaphoreType.DMA((6 * n,)), pltpu.SemaphoreType.DMA((6 * n,)), pltpu.SemaphoreType.DMA((n,))],
        compiler_params=pltpu.CompilerParams(has_side_effects=True),
    )(*shards)


def _swap_halves(name, grads):
    n = len(grads)

    def body(*refs):
        ins, outs = refs[:n], refs[n:2 * n]
        send, recv = refs[2 * n:]
        x, y, c, p, chips = _place()
        cps = []
        for i in range(n):
            h = ins[i].shape[1] // 2
            other = pl.ds(pl.multiple_of((1 - c) * h, 8), h)
            cp = pltpu.make_async_remote_copy(src_ref=ins[i].at[:, other, :], dst_ref=outs[i], send_sem=send.at[i],
                                              recv_sem=recv.at[i], device_id=(x, y, 1 - c), device_id_type=MESH)
            cp.start()
            cps.append(cp)
        for cp in cps:
            cp.wait()

    return pl.pallas_call(
        body, name=name, in_specs=[ANY] * n, out_specs=tuple([ANY] * n),
        out_shape=tuple(jax.ShapeDtypeStruct((g.shape[0], g.shape[1] // 2, g.shape[2]), g.dtype) for g in grads),
        scratch_shapes=[pltpu.SemaphoreType.DMA((n,)), pltpu.SemaphoreType.DMA((n,))],
        compiler_params=pltpu.CompilerParams(has_side_effects=True),
    )(*grads)


def _exchange_chips(name, parts):
    n = len(parts)

    def body(*refs):
        ins, outs = refs[:n], refs[n:2 * n]
        send, recv, lsem = refs[2 * n:]
        x, y, c, p, chips = _place()
        cps, local = [], []
        for i in range(n):
            cp = pltpu.make_async_copy(ins[i].at[p], outs[i].at[p], lsem.at[i])
            cp.start()
            local.append(cp)
            for j, (cx, cy) in enumerate(chips):
                cp = pltpu.make_async_remote_copy(src_ref=ins[i].at[2 * cx + cy], dst_ref=outs[i].at[p], send_sem=send.at[3 * i + j],
                                                  recv_sem=recv.at[3 * i + j], device_id=(cx, cy, c), device_id_type=MESH)
                cp.start()
                cps.append(cp)
        for i in range(n):
            for j, (cx, cy) in enumerate(chips):
                q = 2 * cx + cy
                pltpu.make_async_remote_copy(src_ref=outs[i].at[q], dst_ref=outs[i].at[q], send_sem=send.at[3 * i + j],
                                             recv_sem=recv.at[3 * i + j], device_id=(cx, cy, c), device_id_type=MESH).wait_recv()
        for cp in cps:
            cp.wait_send()
        for cp in local:
            cp.wait()

    return pl.pallas_call(
        body, name=name, in_specs=[ANY] * n, out_specs=tuple([ANY] * n),
        out_shape=tuple(jax.ShapeDtypeStruct(t.shape, t.dtype) for t in parts),
        scratch_shapes=[pltpu.SemaphoreType.DMA((3 * n,)), pltpu.SemaphoreType.DMA((3 * n,)), pltpu.SemaphoreType.DMA((n,))],
        compiler_params=pltpu.CompilerParams(has_side_effects=True),
    )(*parts)


def _join_halves(name, halves):
    n = len(halves)

    def body(*refs):
        ins, outs = refs[:n], refs[n:2 * n]
        send, recv, lsem = refs[2 * n:]
        x, y, c, p, chips = _place()
        cps = []
        for i in range(n):
            h = ins[i].shape[0]
            mine = pl.ds(pl.multiple_of(c * h, 8), h)
            lc = pltpu.make_async_copy(ins[i], outs[i].at[mine], lsem.at[i])
            lc.start()
            cp = pltpu.make_async_remote_copy(src_ref=ins[i], dst_ref=outs[i].at[mine], send_sem=send.at[i],
                                              recv_sem=recv.at[i], device_id=(x, y, 1 - c), device_id_type=MESH)
            cp.start()
            cps.append((lc, cp))
        for i, (lc, cp) in enumerate(cps):
            h = ins[i].shape[0]
            other = pl.ds(pl.multiple_of((1 - c) * h, 8), h)
            pltpu.make_async_remote_copy(src_ref=ins[i], dst_ref=outs[i].at[other], send_sem=send.at[i],
                                         recv_sem=recv.at[i], device_id=(x, y, 1 - c), device_id_type=MESH).wait_recv()
            cp.wait_send()
            lc.wait()

    return pl.pallas_call(
        body, name=name, in_specs=[ANY] * n, out_specs=tuple([ANY] * n),
        out_shape=tuple(jax.ShapeDtypeStruct((2 * t.shape[0], t.shape[1]), t.dtype) for t in halves),
        scratch_shapes=[pltpu.SemaphoreType.DMA((n,)), pltpu.SemaphoreType.DMA((n,)), pltpu.SemaphoreType.DMA((n,))],
        compiler_params=pltpu.CompilerParams(has_side_effects=True),
    )(*halves)


def _allreduce_small(name, pack):
    R, W = pack.shape

    def body(in_ref, out_ref, slots, send, recv):
        x, y, c = lax.axis_index("x"), lax.axis_index("y"), lax.axis_index("c")
        me = 4 * x + 2 * y + c
        slots[0] = in_ref[...]
        cps = []
        for k in range(1, N_DEV):
            peer = (x ^ (k >> 2), y ^ ((k >> 1) & 1), c ^ (k & 1))
            cp = pltpu.make_async_remote_copy(src_ref=in_ref, dst_ref=slots.at[k], send_sem=send.at[k - 1],
                                              recv_sem=recv.at[k - 1], device_id=peer, device_id_type=MESH)
            cp.start()
            cps.append(cp)
        for cp in cps:
            cp.wait()
        total = slots[me]
        for a in range(1, N_DEV):
            total = total + slots[jnp.bitwise_xor(a, me)]
        out_ref[...] = total

    vmem = pl.BlockSpec(memory_space=pltpu.VMEM)
    return pl.pallas_call(
        body, name=name, in_specs=[vmem], out_specs=vmem, out_shape=jax.ShapeDtypeStruct((R, W), F32),
        scratch_shapes=[pltpu.VMEM((N_DEV, R, W), F32), pltpu.SemaphoreType.DMA((N_DEV - 1,)), pltpu.SemaphoreType.DMA((N_DEV - 1,))],
        compiler_params=pltpu.CompilerParams(has_side_effects=True),
    )(pack)


def _heads(a, n_heads):
    S = a.shape[0]
    return a.reshape(S, n_heads, a.shape[1] // n_heads).transpose(1, 0, 2)


def _unheads(a):
    H, S, dh = a.shape
    return a.transpose(1, 0, 2).reshape(S, H * dh)


def _ffn_fwd(tag, xin, gain, wgu3, wd):
    h = _rms_fwd(f"rms_fwd_{tag}", xin, gain)
    gu, act = _ffn_up(f"ffn_up_{tag}", h, wgu3)
    xout = _mm_res(f"ffn_down_{tag}", act, wd, xin, 0.5)
    return xout, (h, gu, act)


def _ffn_bwd(tag, xin, gain, wgu3, wd, saved, dxout):
    h, gu, act = saved
    Ns = wgu3.shape[2]
    dxo_b = dxout.astype(BF16)
    dgu = _ffn_down_bwd(f"ffn_down_bwd_{tag}", dxo_b, wd, gu, 0.5)
    dwd = _mm_tn(f"dw_down_{tag}", act, dxo_b, 0.5)
    dh = _mm_nt_cols(f"ffn_up_bwd_{tag}", dgu, wgu3, a_is_gu=True)
    dwgu = _mm_tn_cols(f"dw_gu_{tag}", h, dgu, Ns, b_is_gu=True)
    dxin, dgain = _rms_bwd(f"rms_bwd_{tag}", xin, gain, dh, dxout)
    return dxin, dgain, dwgu, dwd


def kernel(x, g_ffn1, w_gu1, w_down1, g_mix, w_in, conv_w, q_norm_g, k_norm_g, sinks, w_out_conv, w_out_attn, w_o, g_ffn2, w_gu2, w_down2, loss_target, m_g_ffn1, m_w_gu1, m_w_down1, m_g_mix, m_w_in, m_conv_w, m_q_norm_g, m_k_norm_g, m_sinks, m_w_out_conv, m_w_out_attn, m_w_o, m_g_ffn2, m_w_gu2, m_w_down2, v_g_ffn1, v_w_gu1, v_w_down1, v_g_mix, v_w_in, v_conv_w, v_q_norm_g, v_k_norm_g, v_sinks, v_w_out_conv, v_w_out_attn, v_w_o, v_g_ffn2, v_w_gu2, v_w_down2):
    S, D = x.shape[1], x.shape[2]
    dh = q_norm_g.shape[1]
    HQ = sinks.shape[1]
    HKV = HQ // 4
    AW, KVW, CW = HQ * dh, HKV * dh, D // 2
    off_q, off_k, off_v = 3 * CW, 3 * CW + AW, 3 * CW + AW + KVW
    off_ga, off_gb = off_v + KVW, off_v + KVW + D
    x0, target = x[0], loss_target[0]
    cx, cy, cc = lax.axis_index("x"), lax.axis_index("y"), lax.axis_index("c")
    chip = 2 * cx + cy

    big = dict(w_gu1=w_gu1, w_down1=w_down1, w_in=w_in, w_out_conv=w_out_conv, w_out_attn=w_out_attn, w_o=w_o,
               w_gu2=w_gu2, w_down2=w_down2)
    names = list(big)
    gathered = _gather_weights("gather_weights", [big[k][0].astype(BF16) for k in names] + [conv_w[0]])
    W = dict(zip(names + ["conv_w"], gathered))
    wd1 = W["w_down1"].reshape(-1, D)
    wd2 = W["w_down2"].reshape(-1, D)
    wo = W["w_o"].reshape(-1, D)
    wgu1, wgu2, win3, woc3, woa3, convw3 = W["w_gu1"], W["w_gu2"], W["w_in"], W["w_out_conv"], W["w_out_attn"], W["conv_w"]
    cos, sin, rm, rmt = _rope_consts(S, dh)
    sink_vec = sinks[0]

    x1, saved1 = _ffn_fwd("1", x0, g_ffn1, wgu1, wd1)
    h2 = _rms_fwd("rms_fwd_mix", x1, g_mix)
    proj = _mm_cols("in_proj", h2, win3, F32)
    aconv = _conv_fwd("conv_fwd", proj, convw3, CW)
    ya = _mm_cols("out_conv", aconv, woc3, F32)
    q_raw = _heads(proj[:, off_q:off_q + AW], HQ)
    k_raw = _heads(proj[:, off_k:off_k + KVW], HKV)
    vh = _heads(proj[:, off_v:off_v + KVW], HKV).astype(BF16)
    qn = _qk_prep("q_prep", q_raw, q_norm_g, cos, sin, rm)
    kn = _qk_prep("k_prep", k_raw, k_norm_g, cos, sin, rm)
    oh = _attn_fwd("attn_fwd", qn, kn, vh, sink_vec)
    o = _unheads(oh)
    yb = _mm_cols("out_attn", o, woa3, F32)
    merged = _gate_fwd("gate_fwd", proj, ya, yb, off_ga, off_gb)
    x2 = _mm_res("mix_out", merged, wo, x1, 1.0)
    x3, saved2 = _ffn_fwd("2", x2, g_ffn2, wgu2, wd2)

    dy, loss_lanes = _loss_grad("loss_grad", x3, target)
    dx2, dg_ffn2, dwgu2, dwd2 = _ffn_bwd("2", x2, g_ffn2, wgu2, wd2, saved2, dy)
    dx2_b = dx2.astype(BF16)
    dmerged = _mm_nt("mix_out_bwd", dx2_b, wo, F32)
    dwo = _mm_tn("dw_o", merged, dx2_b)
    dga, dgb, dya, dyb = _gate_bwd("gate_bwd", proj, ya, yb, dmerged, off_ga, off_gb)
    daconv = _mm_nt_cols("out_conv_bwd", dya, woc3)
    dwoc = _mm_tn_cols("dw_out_conv", aconv, dya, woc3.shape[2])
    do = _mm_nt_cols("out_attn_bwd", dyb, woa3)
    dwoa = _mm_tn_cols("dw_out_attn", o, dyb, woa3.shape[2])
    dxc, dbg, dcg, dconvw = _conv_bwd("conv_bwd", proj, convw3, daconv, CW)
    dqn, dkn, dvh, dsink3 = _attn_bwd("attn_bwd", qn, kn, vh, sink_vec, _heads(do, HQ).astype(BF16))
    dq_raw, dqg = _qk_prep_bwd("q_prep_bwd", q_raw, q_norm_g, cos, sin, rmt, dqn)
    dk_raw, dkg = _qk_prep_bwd("k_prep_bwd", k_raw, k_norm_g, cos, sin, rmt, dkn)
    dproj = jnp.concatenate([dxc, dbg, dcg, _unheads(dq_raw), _unheads(dk_raw), _unheads(dvh).astype(BF16), dga, dgb], axis=1)
    dh2 = _mm_nt_cols("in_proj_bwd", dproj, win3)
    dwin = _mm_tn_cols("dw_in", h2, dproj, win3.shape[2])
    dx1, dg_mix = _rms_bwd("rms_bwd_mix", x1, g_mix, dh2, dx2)
    dx0, dg_ffn1, dwgu1, dwd1 = _ffn_bwd("1", x0, g_ffn1, wgu1, wd1, saved1, dx1)

    full = dict(w_gu1=dwgu1, w_down1=dwd1.reshape(N_CHIPS, -1, D), w_in=dwin, w_out_conv=dwoc, w_out_attn=dwoa,
                w_o=dwo.reshape(N_CHIPS, -1, D), w_gu2=dwgu2, w_down2=dwd2.reshape(N_CHIPS, -1, D))
    grads = [full[k] for k in names]
    c_arr = jnp.reshape(cc, (1,)).astype(jnp.int32)
    sib_halves = _swap_halves("swap_halves", grads)
    chip_parts = [_add_half(f"add_half_{k}", g, r, c_arr) for k, g, r in zip(names, grads, sib_halves)]
    by_chip = _exchange_chips("exchange_chips", chip_parts)
    reduced_halves = [_add_chips(f"add_chips_{k}", r) for k, r in zip(names, by_chip)]
    reduced = dict(zip(names, _join_halves("join_halves", reduced_halves)))

    def row(a):
        a = a.reshape(-1, a.shape[-1])
        return jnp.pad(a, ((0, 0), (0, D - a.shape[1])))

    misc = jnp.concatenate([dqg, dkg, dsink3[:, :, 0].reshape(1, HQ), loss_lanes], axis=1)
    n_conv_rows = N_CHIPS * CONV_K
    pack = jnp.concatenate([row(dg_ffn1), row(dg_mix), row(dg_ffn2), row(dconvw), row(misc)], axis=0)
    pack = jnp.pad(pack, ((0, PACK_ROWS - pack.shape[0]), (0, 0)))
    tot = _allreduce_small("allreduce_small", pack)
    cw_s = conv_w.shape[2]
    g_conv = lax.dynamic_slice(tot, (3 + CONV_K * chip, 0), (CONV_K, cw_s))
    mrow = 3 + n_conv_rows
    small_g = dict(g_ffn1=tot[0:1], g_mix=tot[1:2], g_ffn2=tot[2:3], conv_w=g_conv,
                   q_norm_g=tot[mrow:mrow + 1, 0:dh], k_norm_g=tot[mrow:mrow + 1, dh:2 * dh],
                   sinks=tot[mrow:mrow + 1, 2 * dh:2 * dh + HQ])
    loss = (0.5 / D) * jnp.sum(tot[mrow, 2 * dh + HQ:2 * dh + HQ + LANES])

    wts = dict(g_ffn1=g_ffn1, w_gu1=w_gu1, w_down1=w_down1, g_mix=g_mix, w_in=w_in, conv_w=conv_w, q_norm_g=q_norm_g,
               k_norm_g=k_norm_g, sinks=sinks, w_out_conv=w_out_conv, w_out_attn=w_out_attn, w_o=w_o, g_ffn2=g_ffn2,
               w_gu2=w_gu2, w_down2=w_down2)
    ms = dict(g_ffn1=m_g_ffn1, w_gu1=m_w_gu1, w_down1=m_w_down1, g_mix=m_g_mix, w_in=m_w_in, conv_w=m_conv_w,
              q_norm_g=m_q_norm_g, k_norm_g=m_k_norm_g, sinks=m_sinks, w_out_conv=m_w_out_conv, w_out_attn=m_w_out_attn,
              w_o=m_w_o, g_ffn2=m_g_ffn2, w_gu2=m_w_gu2, w_down2=m_w_down2)
    vs = dict(g_ffn1=v_g_ffn1, w_gu1=v_w_gu1, w_down1=v_w_down1, g_mix=v_g_mix, w_in=v_w_in, conv_w=v_conv_w,
              q_norm_g=v_q_norm_g, k_norm_g=v_k_norm_g, sinks=v_sinks, w_out_conv=v_w_out_conv, w_out_attn=v_w_out_attn,
              w_o=v_w_o, g_ffn2=v_g_ffn2, w_gu2=v_w_gu2, w_down2=v_w_down2)
    order = list(wts)
    small_names = [k for k in order if k not in big]
    grad, delta, new_m, new_v = {}, {}, {}, {}
    for k in names:
        g2 = reduced[k]
        d, nm, nv = _adamw(f"adamw_{k}", wts[k][0], g2, ms[k][0], vs[k][0])
        grad[k], delta[k], new_m[k], new_v[k] = g2[None], d[None], nm[None], nv[None]
    def small_pack(src):
        rows = jnp.concatenate([row(src[k]) for k in small_names], axis=0)
        return jnp.pad(rows, ((0, PACK_ROWS - rows.shape[0]), (0, 0)))

    sd, sm, sv = _adamw("adamw_small", small_pack(wts), small_pack(small_g), small_pack(ms), small_pack(vs))
    r0 = 0
    for k in small_names:
        shape = wts[k].shape
        nr, ncol = math.prod(shape[:-1]), shape[-1]
        grad[k] = small_g[k].reshape(shape)
        delta[k], new_m[k], new_v[k] = (a[r0:r0 + nr, 0:ncol].reshape(shape) for a in (sd, sm, sv))
        r0 += nr
    return (loss, dx0[None], *[grad[k] for k in order], *[delta[k] for k in order],
            *[new_m[k] for k in order], *[new_v[k] for k in order])
```

```python
import math

import numpy as np
import jax
import jax.numpy as jnp
from jax import lax
from jax.experimental import pallas as pl
from jax.experimental.pallas import tpu as pltpu

F32 = jnp.float32
BF16 = jnp.bfloat16
MESH = pl.DeviceIdType.MESH

RMS_EPS = 1e-6
BLOCK = 128
ROPE_THETA = 500000.0
NEG_INF = -1e30
CONV_K = 3
ADAM_LR, ADAM_B1, ADAM_B2, ADAM_EPS, ADAM_WD, ADAM_STEP = 0.001, 0.9, 0.999, 1e-08, 0.01, 10

VMEM_LIMIT_V7X = 48 * 1024 * 1024
LANES = 128
N_CHIPS = 4
N_DEV = 8


def _tile(n, want, align=LANES):
    best = None
    t = align
    while t <= min(n, want):
        if n % t == 0:
            best = t
        t += align
    return best or n


def _cparams(sem):
    return pltpu.CompilerParams(dimension_semantics=sem, vmem_limit_bytes=VMEM_LIMIT_V7X)


def _sigmoid(x):
    return 1.0 / (1.0 + jnp.exp(-x))


def _mm(name, grid, ins, in_specs, pairs, dims, acc_shape, out_shape, out_specs, epilogue):
    nk = grid[-1]
    n_in = len(ins)
    n_acc = len(pairs)
    multi = isinstance(out_shape, (tuple, list))
    n_out = len(out_shape) if multi else 1

    def body(*refs):
        in_refs = refs[:n_in]
        out_refs = refs[n_in:n_in + n_out]
        accs = refs[n_in + n_out:]
        k = pl.program_id(len(grid) - 1)

        @pl.when(k == 0)
        def _():
            for acc in accs:
                acc[...] = jnp.zeros_like(acc)

        for acc, (ai, bi) in zip(accs, pairs):
            acc[...] += lax.dot_general(in_refs[ai][...], in_refs[bi][...], dims, preferred_element_type=F32)

        @pl.when(k == nk - 1)
        def _():
            epilogue([acc[...] for acc in accs], in_refs, out_refs)

    sem = ("parallel",) * (len(grid) - 1) + ("arbitrary",)
    return pl.pallas_call(
        body, name=name, grid=grid, in_specs=in_specs, out_specs=out_specs, out_shape=out_shape,
        scratch_shapes=[pltpu.VMEM(acc_shape, F32) for _ in range(n_acc)], compiler_params=_cparams(sem),
    )(*ins)


NN = (((1,), (0,)), ((), ()))
NT = (((1,), (1,)), ((), ()))
TN = (((0,), (0,)), ((), ()))


def _ffn_up(name, h, wgu3):
    S, D = h.shape
    Ns = wgu3.shape[2]
    F = 2 * Ns
    tm, tn, tk = _tile(S, 512), _tile(Ns, 1408), _tile(D, 512)
    nbs = Ns // tn

    def epi(accs, in_refs, out_refs):
        g, u = accs
        gu_ref, a_ref = out_refs
        gu_ref[0] = g.astype(BF16)
        gu_ref[1] = u.astype(BF16)
        a_ref[...] = (g * _sigmoid(g) * u).astype(BF16)

    return _mm(
        name, (S // tm, F // tn, D // tk), (h, wgu3, wgu3),
        [pl.BlockSpec((tm, tk), lambda i, j, k: (i, k)),
         pl.BlockSpec((None, tk, tn), lambda i, j, k: (j // nbs, k, j % nbs)),
         pl.BlockSpec((None, tk, tn), lambda i, j, k: (2 + j // nbs, k, j % nbs))],
        [(0, 1), (0, 2)], NN, (tm, tn),
        (jax.ShapeDtypeStruct((2, S, F), BF16), jax.ShapeDtypeStruct((S, F), BF16)),
        (pl.BlockSpec((2, tm, tn), lambda i, j, k: (0, i, j)), pl.BlockSpec((tm, tn), lambda i, j, k: (i, j))),
        epi)


def _mm_res(name, a, w, res, scale):
    S, K = a.shape
    N = w.shape[1]
    tm, tn, tk = _tile(S, 1024), _tile(N, 1024), _tile(K, 512)

    def epi(accs, in_refs, out_refs):
        out_refs[0][...] = in_refs[2][...] + scale * accs[0]

    return _mm(
        name, (S // tm, N // tn, K // tk), (a, w, res),
        [pl.BlockSpec((tm, tk), lambda i, j, k: (i, k)), pl.BlockSpec((tk, tn), lambda i, j, k: (k, j)),
         pl.BlockSpec((tm, tn), lambda i, j, k: (i, j))],
        [(0, 1)], NN, (tm, tn), jax.ShapeDtypeStruct((S, N), F32), pl.BlockSpec((tm, tn), lambda i, j, k: (i, j)), epi)


def _mm_cols(name, a, w3, out_dtype):
    S, K = a.shape
    Ns = w3.shape[2]
    tm, tn, tk = _tile(S, 512), _tile(Ns, 2304), _tile(K, 512)
    nbs = Ns // tn

    def epi(accs, in_refs, out_refs):
        out_refs[0][...] = accs[0].astype(out_dtype)

    return _mm(
        name, (S // tm, N_CHIPS * nbs, K // tk), (a, w3),
        [pl.BlockSpec((tm, tk), lambda i, j, k: (i, k)),
         pl.BlockSpec((None, tk, tn), lambda i, j, k: (j // nbs, k, j % nbs))],
        [(0, 1)], NN, (tm, tn), jax.ShapeDtypeStruct((S, N_CHIPS * Ns), out_dtype),
        pl.BlockSpec((tm, tn), lambda i, j, k: (i, j)), epi)


def _mm_nt(name, a, w, out_dtype, scale=1.0):
    S, N = a.shape
    K = w.shape[0]
    tm, tn, tk = _tile(S, 1024), _tile(K, 1024), _tile(N, 512)

    def epi(accs, in_refs, out_refs):
        out_refs[0][...] = (scale * accs[0]).astype(out_dtype)

    return _mm(
        name, (S // tm, K // tn, N // tk), (a, w),
        [pl.BlockSpec((tm, tk), lambda i, j, k: (i, k)), pl.BlockSpec((tn, tk), lambda i, j, k: (j, k))],
        [(0, 1)], NT, (tm, tn), jax.ShapeDtypeStruct((S, K), out_dtype), pl.BlockSpec((tm, tn), lambda i, j, k: (i, j)), epi)


def _ffn_down_bwd(name, dy, wd, gu, scale):
    S, D = dy.shape
    F = wd.shape[0]
    tm, tn, tk = _tile(S, 512), _tile(F, 1408), _tile(D, 512)

    def epi(accs, in_refs, out_refs):
        da = scale * accs[0]
        g = in_refs[2][0].astype(F32)
        u = in_refs[2][1].astype(F32)
        sg = _sigmoid(g)
        out_refs[0][0] = (da * u * (sg * (1.0 + g * (1.0 - sg)))).astype(BF16)
        out_refs[0][1] = (da * (g * sg)).astype(BF16)

    return _mm(
        name, (S // tm, F // tn, D // tk), (dy, wd, gu),
        [pl.BlockSpec((tm, tk), lambda i, j, k: (i, k)), pl.BlockSpec((tn, tk), lambda i, j, k: (j, k)),
         pl.BlockSpec((2, tm, tn), lambda i, j, k: (0, i, j))],
        [(0, 1)], NT, (tm, tn), jax.ShapeDtypeStruct((2, S, F), BF16),
        pl.BlockSpec((2, tm, tn), lambda i, j, k: (0, i, j)), epi)


def _mm_nt_cols(name, a, w3, a_is_gu=False):
    K, Ns = w3.shape[1], w3.shape[2]
    S = a.shape[1] if a_is_gu else a.shape[0]
    tm, tn, tk = _tile(S, 1024), _tile(K, 1024), _tile(Ns, 2304)
    nbs = Ns // tk
    if a_is_gu:
        a_spec = pl.BlockSpec((None, tm, tk), lambda i, j, k: (k // (2 * nbs), i, k % (2 * nbs)))
    else:
        a_spec = pl.BlockSpec((tm, tk), lambda i, j, k: (i, k))

    def epi(accs, in_refs, out_refs):
        out_refs[0][...] = accs[0]

    return _mm(
        name, (S // tm, K // tn, N_CHIPS * nbs), (a, w3),
        [a_spec, pl.BlockSpec((None, tn, tk), lambda i, j, k: (k // nbs, j, k % nbs))],
        [(0, 1)], NT, (tm, tn), jax.ShapeDtypeStruct((S, K), F32), pl.BlockSpec((tm, tn), lambda i, j, k: (i, j)), epi)


def _mm_tn(name, a, b, scale=1.0):
    S, K = a.shape
    N = b.shape[1]
    tm, tn, tk = _tile(K, 512), _tile(N, 1024), _tile(S, 512)

    def epi(accs, in_refs, out_refs):
        out_refs[0][...] = scale * accs[0]

    return _mm(
        name, (K // tm, N // tn, S // tk), (a, b),
        [pl.BlockSpec((tk, tm), lambda i, j, k: (k, i)), pl.BlockSpec((tk, tn), lambda i, j, k: (k, j))],
        [(0, 1)], TN, (tm, tn), jax.ShapeDtypeStruct((K, N), F32), pl.BlockSpec((tm, tn), lambda i, j, k: (i, j)), epi)


def _mm_tn_cols(name, a, b, Ns, b_is_gu=False):
    S, K = a.shape
    tm, tn, tk = _tile(K, 512), _tile(Ns, 2304), _tile(S, 512)
    nbs = Ns // tn
    if b_is_gu:
        b_spec = pl.BlockSpec((None, tk, tn), lambda i, j, k: (j // (2 * nbs), k, j % (2 * nbs)))
    else:
        b_spec = pl.BlockSpec((tk, tn), lambda i, j, k: (k, j))

    def epi(accs, in_refs, out_refs):
        out_refs[0][...] = accs[0]

    return _mm(
        name, (K // tm, N_CHIPS * nbs, S // tk), (a, b),
        [pl.BlockSpec((tk, tm), lambda i, j, k: (k, i)), b_spec],
        [(0, 1)], TN, (tm, tn), jax.ShapeDtypeStruct((N_CHIPS, K, Ns), F32),
        pl.BlockSpec((None, tm, tn), lambda i, j, k: (j // nbs, i, j % nbs)), epi)


def _rms_fwd(name, x, gain):
    S, D = x.shape
    tm = _tile(S, 256, 8)

    def body(x_ref, g_ref, h_ref):
        xv = x_ref[...]
        r = lax.rsqrt(jnp.mean(xv * xv, axis=-1, keepdims=True) + RMS_EPS)
        h_ref[...] = (xv * r * g_ref[...]).astype(BF16)

    return pl.pallas_call(
        body, name=name, grid=(S // tm,),
        in_specs=[pl.BlockSpec((tm, D), lambda i: (i, 0)), pl.BlockSpec((1, D), lambda i: (0, 0))],
        out_specs=pl.BlockSpec((tm, D), lambda i: (i, 0)), out_shape=jax.ShapeDtypeStruct((S, D), BF16),
        compiler_params=_cparams(("parallel",)),
    )(x, gain)


def _rms_bwd(name, x, gain, dh, dres):
    S, D = x.shape
    tm = _tile(S, 256, 8)

    def body(x_ref, g_ref, dh_ref, dres_ref, dx_ref, dg_ref):
        i = pl.program_id(0)
        xv = x_ref[...]
        r = lax.rsqrt(jnp.mean(xv * xv, axis=-1, keepdims=True) + RMS_EPS)
        xhat = xv * r
        dhv = dh_ref[...]
        dxhat = dhv * g_ref[...]
        dx_ref[...] = dres_ref[...] + r * (dxhat - xhat * jnp.mean(dxhat * xhat, axis=-1, keepdims=True))

        @pl.when(i == 0)
        def _():
            dg_ref[...] = jnp.zeros_like(dg_ref)

        dg_ref[...] += jnp.sum(dhv * xhat, axis=0, keepdims=True)

    row = pl.BlockSpec((tm, D), lambda i: (i, 0))
    vec = pl.BlockSpec((1, D), lambda i: (0, 0))
    return pl.pallas_call(
        body, name=name, grid=(S // tm,), in_specs=[row, vec, row, row], out_specs=(row, vec),
        out_shape=(jax.ShapeDtypeStruct((S, D), F32), jax.ShapeDtypeStruct((1, D), F32)),
        compiler_params=_cparams(("arbitrary",)),
    )(x, gain, dh, dres)


def _loss_grad(name, y, target):
    S, D = y.shape
    tm = _tile(S, 256, 8)

    def body(y_ref, t_ref, dy_ref, l_ref):
        i = pl.program_id(0)
        e = y_ref[...] - t_ref[...]
        dy_ref[...] = e * (1.0 / D)
        col = jnp.sum(e * e, axis=0, keepdims=True)
        part = col[:, 0:LANES]
        for k in range(1, D // LANES):
            part = part + col[:, k * LANES:(k + 1) * LANES]

        @pl.when(i == 0)
        def _():
            l_ref[...] = jnp.zeros_like(l_ref)

        l_ref[...] += part

    row = pl.BlockSpec((tm, D), lambda i: (i, 0))
    return pl.pallas_call(
        body, name=name, grid=(S // tm,), in_specs=[row, row],
        out_specs=(row, pl.BlockSpec((1, LANES), lambda i: (0, 0))),
        out_shape=(jax.ShapeDtypeStruct((S, D), F32), jax.ShapeDtypeStruct((1, LANES), F32)),
        compiler_params=_cparams(("arbitrary",)),
    )(y, target)


def _shift_down(u, k):
    rows = lax.broadcasted_iota(jnp.int32, u.shape, 0)
    return jnp.where(rows >= k, pltpu.roll(u, k, 0), 0.0)


def _shift_up(u, k):
    n = u.shape[0]
    rows = lax.broadcasted_iota(jnp.int32, u.shape, 0)
    return jnp.where(rows < n - k, pltpu.roll(u, n - k, 0), 0.0)


def _conv_specs(S, cw, conv_width):
    nb = conv_width // cw
    col = lambda off: pl.BlockSpec((S, cw), lambda j, off=off: (0, off * nb + j))
    return nb, col(0), col(1), col(2)


def _conv_fwd(name, proj, convw3, conv_width):
    S = proj.shape[0]
    cw = convw3.shape[2]
    nb, xc_s, bg_s, cg_s = _conv_specs(S, cw, conv_width)

    def body(xc_ref, bg_ref, cg_ref, w_ref, o_ref):
        u = cg_ref[...] * xc_ref[...]
        w = w_ref[...]
        cv = w[2:3, :] * u + w[1:2, :] * _shift_down(u, 1) + w[0:1, :] * _shift_down(u, 2)
        o_ref[...] = (bg_ref[...] * cv).astype(BF16)

    return pl.pallas_call(
        body, name=name, grid=(nb,),
        in_specs=[xc_s, bg_s, cg_s, pl.BlockSpec((None, CONV_K, cw), lambda j: (j, 0, 0))],
        out_specs=pl.BlockSpec((S, cw), lambda j: (0, j)), out_shape=jax.ShapeDtypeStruct((S, conv_width), BF16),
        compiler_params=_cparams(("parallel",)),
    )(proj, proj, proj, convw3)


def _conv_bwd(name, proj, convw3, da, conv_width):
    S = proj.shape[0]
    cw = convw3.shape[2]
    nb, xc_s, bg_s, cg_s = _conv_specs(S, cw, conv_width)

    def body(xc_ref, bg_ref, cg_ref, w_ref, da_ref, dxc_ref, dbg_ref, dcg_ref, dw_ref):
        xc, cg = xc_ref[...], cg_ref[...]
        u = cg * xc
        w = w_ref[...]
        u1, u2 = _shift_down(u, 1), _shift_down(u, 2)
        cv = w[2:3, :] * u + w[1:2, :] * u1 + w[0:1, :] * u2
        dav = da_ref[...]
        dbg_ref[...] = (dav * cv).astype(BF16)
        dcv = dav * bg_ref[...]
        du = w[2:3, :] * dcv + w[1:2, :] * _shift_up(dcv, 1) + w[0:1, :] * _shift_up(dcv, 2)
        dxc_ref[...] = (du * cg).astype(BF16)
        dcg_ref[...] = (du * xc).astype(BF16)
        dw_ref[0:1, :] = jnp.sum(dcv * u2, axis=0, keepdims=True)
        dw_ref[1:2, :] = jnp.sum(dcv * u1, axis=0, keepdims=True)
        dw_ref[2:3, :] = jnp.sum(dcv * u, axis=0, keepdims=True)

    wspec = pl.BlockSpec((None, CONV_K, cw), lambda j: (j, 0, 0))
    ospec = pl.BlockSpec((S, cw), lambda j: (0, j))
    act = jax.ShapeDtypeStruct((S, conv_width), BF16)
    return pl.pallas_call(
        body, name=name, grid=(nb,), in_specs=[xc_s, bg_s, cg_s, wspec, ospec],
        out_specs=(ospec, ospec, ospec, wspec),
        out_shape=(act, act, act, jax.ShapeDtypeStruct(convw3.shape, F32)),
        compiler_params=_cparams(("parallel",)),
    )(proj, proj, proj, convw3, da)


def _rope_consts(S, dh):
    rot = dh // 4
    half = rot // 2
    inv_freq = 1.0 / (ROPE_THETA ** (jnp.arange(0, rot, 2, dtype=F32) / rot))
    ang = jnp.arange(S, dtype=F32)[:, None] * inv_freq[None, :]
    cos = jnp.concatenate([jnp.cos(ang), jnp.cos(ang), jnp.ones((S, dh - rot), F32)], axis=1)
    sin = jnp.concatenate([jnp.sin(ang), jnp.sin(ang), jnp.zeros((S, dh - rot), F32)], axis=1)
    rm = np.zeros((dh, dh), np.float32)
    for j in range(half):
        rm[j + half, j] = -1.0
        rm[j, j + half] = 1.0
    return cos, sin, jnp.asarray(rm, BF16), jnp.asarray(rm.T, BF16)


def _exact_perm(y, rm):
    hi = y.astype(BF16)
    r1 = y - hi.astype(F32)
    mid = r1.astype(BF16)
    lo = (r1 - mid.astype(F32)).astype(BF16)
    dot = lambda a: jnp.dot(a, rm, preferred_element_type=F32)
    return dot(hi) + dot(mid) + dot(lo)


def _qk_prep(name, xh, gain, cos, sin, rm):
    H, S, dh = xh.shape
    tm = _tile(S, 1024, 8)

    def body(x_ref, g_ref, c_ref, s_ref, rm_ref, o_ref):
        xv = x_ref[...]
        y = xv * lax.rsqrt(jnp.mean(xv * xv, axis=-1, keepdims=True) + RMS_EPS) * g_ref[...]
        o_ref[...] = (y * c_ref[...] + _exact_perm(y, rm_ref[...]) * s_ref[...]).astype(BF16)

    blk = pl.BlockSpec((None, tm, dh), lambda h, i: (h, i, 0))
    tab = pl.BlockSpec((tm, dh), lambda h, i: (i, 0))
    return pl.pallas_call(
        body, name=name, grid=(H, S // tm),
        in_specs=[blk, pl.BlockSpec((1, dh), lambda h, i: (0, 0)), tab, tab, pl.BlockSpec((dh, dh), lambda h, i: (0, 0))],
        out_specs=blk, out_shape=jax.ShapeDtypeStruct((H, S, dh), BF16),
        compiler_params=_cparams(("parallel", "parallel")),
    )(xh, gain, cos, sin, rm)


def _qk_prep_bwd(name, xh, gain, cos, sin, rmt, dout):
    H, S, dh = xh.shape
    tm = _tile(S, 1024, 8)

    def body(x_ref, g_ref, c_ref, s_ref, rmt_ref, do_ref, dx_ref, dg_ref):
        first = (pl.program_id(0) == 0) & (pl.program_id(1) == 0)
        xv = x_ref[...]
        r = lax.rsqrt(jnp.mean(xv * xv, axis=-1, keepdims=True) + RMS_EPS)
        xhat = xv * r
        dov = do_ref[...]
        dy = dov * c_ref[...] + _exact_perm(dov * s_ref[...], rmt_ref[...])
        dxhat = dy * g_ref[...]
        dx_ref[...] = (r * (dxhat - xhat * jnp.mean(dxhat * xhat, axis=-1, keepdims=True))).astype(BF16)

        @pl.when(first)
        def _():
            dg_ref[...] = jnp.zeros_like(dg_ref)

        dg_ref[...] += jnp.sum(dy * xhat, axis=0, keepdims=True)

    blk = pl.BlockSpec((None, tm, dh), lambda h, i: (h, i, 0))
    tab = pl.BlockSpec((tm, dh), lambda h, i: (i, 0))
    vec = pl.BlockSpec((1, dh), lambda h, i: (0, 0))
    return pl.pallas_call(
        body, name=name, grid=(H, S // tm),
        in_specs=[blk, vec, tab, tab, pl.BlockSpec((dh, dh), lambda h, i: (0, 0)), blk],
        out_specs=(blk, vec), out_shape=(jax.ShapeDtypeStruct((H, S, dh), BF16), jax.ShapeDtypeStruct((1, dh), F32)),
        compiler_params=_cparams(("arbitrary", "arbitrary")),
    )(xh, gain, cos, sin, rmt, dout)


def _attn_probs(q, kp, kc, sink_col, n, scale):
    rows = q.shape[0]
    sp = lax.dot_general(q, kp, NT, preferred_element_type=F32) * scale
    sc = lax.dot_general(q, kc, NT, preferred_element_type=F32) * scale
    qi = lax.broadcasted_iota(jnp.int32, (rows, BLOCK), 0) % BLOCK
    kj = lax.broadcasted_iota(jnp.int32, (rows, BLOCK), 1)
    sp = jnp.where((kj > qi) & (n > 0), sp, NEG_INF)
    sc = jnp.where(kj <= qi, sc, NEG_INF)
    m = jnp.maximum(jnp.maximum(jnp.max(sp, axis=-1, keepdims=True), jnp.max(sc, axis=-1, keepdims=True)), sink_col)
    pp, pc, ps = jnp.exp(sp - m), jnp.exp(sc - m), jnp.exp(sink_col - m)
    inv = 1.0 / (jnp.sum(pp, axis=-1, keepdims=True) + jnp.sum(pc, axis=-1, keepdims=True) + ps)
    return pp * inv, pc * inv, ps * inv


def _sink_col(sink_ref, hk, group):
    rows = group * BLOCK
    g = lax.broadcasted_iota(jnp.int32, (rows, 1), 0) // BLOCK
    col = jnp.zeros((rows, 1), F32)
    for i in range(group):
        col = jnp.where(g == i, sink_ref[hk * group + i], col)
    return col


def _attn_specs(group, dh):
    qb = pl.BlockSpec((group, BLOCK, dh), lambda hk, n: (hk, n, 0))
    prev = pl.BlockSpec((None, BLOCK, dh), lambda hk, n: (hk, jnp.maximum(n - 1, 0), 0))
    cur = pl.BlockSpec((None, BLOCK, dh), lambda hk, n: (hk, n, 0))
    return qb, prev, cur, pl.BlockSpec(memory_space=pltpu.SMEM)


def _attn_fwd(name, q, k, v, sinks):
    HQ, S, dh = q.shape
    HKV = k.shape[0]
    group = HQ // HKV
    scale = dh ** -0.5
    qb, prev, cur, smem = _attn_specs(group, dh)

    def body(q_ref, kp_ref, kc_ref, vp_ref, vc_ref, sink_ref, o_ref):
        hk, n = pl.program_id(0), pl.program_id(1)
        qv = q_ref[...].reshape(group * BLOCK, dh)
        pp, pc, _ = _attn_probs(qv, kp_ref[...], kc_ref[...], _sink_col(sink_ref, hk, group), n, scale)
        o = jnp.dot(pp.astype(BF16), vp_ref[...], preferred_element_type=F32)
        o = o + jnp.dot(pc.astype(BF16), vc_ref[...], preferred_element_type=F32)
        o_ref[...] = o.reshape(group, BLOCK, dh).astype(BF16)

    return pl.pallas_call(
        body, name=name, grid=(HKV, S // BLOCK), in_specs=[qb, prev, cur, prev, cur, smem], out_specs=qb,
        out_shape=jax.ShapeDtypeStruct((HQ, S, dh), BF16), compiler_params=_cparams(("parallel", "parallel")),
    )(q, k, k, v, v, sinks)


def _attn_bwd(name, q, k, v, sinks, do):
    HQ, S, dh = q.shape
    HKV = k.shape[0]
    group = HQ // HKV
    scale = dh ** -0.5
    qb, prev, cur, smem = _attn_specs(group, dh)
    whole = pl.BlockSpec((None, S, dh), lambda hk, n: (hk, 0, 0))
    sk = pl.BlockSpec((None, group, LANES), lambda hk, n: (hk, 0, 0))

    def body(q_ref, kp_ref, kc_ref, vp_ref, vc_ref, sink_ref, do_ref, dq_ref, dk_ref, dv_ref, ds_ref):
        hk, n = pl.program_id(0), pl.program_id(1)
        rows = group * BLOCK
        qv = q_ref[...].reshape(rows, dh)
        dov = do_ref[...].reshape(rows, dh)
        kp, kc, vp, vc = kp_ref[...], kc_ref[...], vp_ref[...], vc_ref[...]
        pp, pc, ps = _attn_probs(qv, kp, kc, _sink_col(sink_ref, hk, group), n, scale)
        dpp = lax.dot_general(dov, vp, NT, preferred_element_type=F32)
        dpc = lax.dot_general(dov, vc, NT, preferred_element_type=F32)
        delta = jnp.sum(pp * dpp, axis=-1, keepdims=True) + jnp.sum(pc * dpc, axis=-1, keepdims=True)
        dsp = (pp * (dpp - delta) * scale).astype(BF16)
        dsc = (pc * (dpc - delta) * scale).astype(BF16)
        dq = jnp.dot(dsp, kp, preferred_element_type=F32) + jnp.dot(dsc, kc, preferred_element_type=F32)
        dq_ref[...] = dq.reshape(group, BLOCK, dh)

        @pl.when(n == 0)
        def _():
            dk_ref[...] = jnp.zeros_like(dk_ref)
            dv_ref[...] = jnp.zeros_like(dv_ref)
            ds_ref[...] = jnp.zeros_like(ds_ref)

        cur_rows = pl.ds(pl.multiple_of(n * BLOCK, BLOCK), BLOCK)
        prev_rows = pl.ds(pl.multiple_of(jnp.maximum(n - 1, 0) * BLOCK, BLOCK), BLOCK)
        tdot = lambda a, b: lax.dot_general(a, b, TN, preferred_element_type=F32)
        dk_ref[prev_rows, :] += tdot(dsp, qv)
        dv_ref[prev_rows, :] += tdot(pp.astype(BF16), dov)
        dk_ref[cur_rows, :] += tdot(dsc, qv)
        dv_ref[cur_rows, :] += tdot(pc.astype(BF16), dov)
        dsink = -jnp.sum((ps * delta).reshape(group, BLOCK, 1), axis=1)
        ds_ref[...] += jnp.broadcast_to(dsink, (group, LANES))

    return pl.pallas_call(
        body, name=name, grid=(HKV, S // BLOCK), in_specs=[qb, prev, cur, prev, cur, smem, qb],
        out_specs=(qb, whole, whole, sk),
        out_shape=(jax.ShapeDtypeStruct((HQ, S, dh), F32), jax.ShapeDtypeStruct((HKV, S, dh), F32),
                   jax.ShapeDtypeStruct((HKV, S, dh), F32), jax.ShapeDtypeStruct((HKV, group, LANES), F32)),
        compiler_params=_cparams(("arbitrary", "arbitrary")),
    )(q, k, k, v, v, sinks, do)


def _gate_specs(S, D, ga_off, gb_off):
    tg = LANES
    for t in range(LANES, 513, LANES):
        if D % t == 0 and ga_off % t == 0 and gb_off % t == 0:
            tg = t
    if D % LANES:
        tg = math.gcd(math.gcd(D, ga_off), gb_off)
    tm = _tile(S, 512, 8)
    act = pl.BlockSpec((tm, tg), lambda i, j: (i, j))
    ga = pl.BlockSpec((tm, tg), lambda i, j: (i, ga_off // tg + j))
    gb = pl.BlockSpec((tm, tg), lambda i, j: (i, gb_off // tg + j))
    return (S // tm, D // tg), act, ga, gb


def _gate_fwd(name, proj, ya, yb, ga_off, gb_off):
    S, D = ya.shape
    grid, act, ga, gb = _gate_specs(S, D, ga_off, gb_off)

    def body(ga_ref, gb_ref, ya_ref, yb_ref, o_ref):
        o_ref[...] = (_sigmoid(ga_ref[...]) * ya_ref[...] + _sigmoid(gb_ref[...]) * yb_ref[...]).astype(BF16)

    return pl.pallas_call(
        body, name=name, grid=grid, in_specs=[ga, gb, act, act], out_specs=act,
        out_shape=jax.ShapeDtypeStruct((S, D), BF16), compiler_params=_cparams(("parallel", "parallel")),
    )(proj, proj, ya, yb)


def _gate_bwd(name, proj, ya, yb, dm, ga_off, gb_off):
    S, D = ya.shape
    grid, act, ga, gb = _gate_specs(S, D, ga_off, gb_off)

    def body(ga_ref, gb_ref, ya_ref, yb_ref, dm_ref, dga_ref, dgb_ref, dya_ref, dyb_ref):
        dmv = dm_ref[...]
        sa, sb = _sigmoid(ga_ref[...]), _sigmoid(gb_ref[...])
        dga_ref[...] = (dmv * ya_ref[...] * sa * (1.0 - sa)).astype(BF16)
        dgb_ref[...] = (dmv * yb_ref[...] * sb * (1.0 - sb)).astype(BF16)
        dya_ref[...] = (dmv * sa).astype(BF16)
        dyb_ref[...] = (dmv * sb).astype(BF16)

    o = jax.ShapeDtypeStruct((S, D), BF16)
    return pl.pallas_call(
        body, name=name, grid=grid, in_specs=[ga, gb, act, act, act], out_specs=(act, act, act, act),
        out_shape=(o, o, o, o), compiler_params=_cparams(("parallel", "parallel")),
    )(proj, proj, ya, yb, dm)


def _row_tile(rows, cols, n_arrays):
    want = max(8, (VMEM_LIMIT_V7X // 2) // (2 * n_arrays * cols * 4))
    return _tile(rows, want, 8)


def _cast_to_slot(name, w, dtype, p_arr):
    R, C = w.shape
    tr = _row_tile(R, C, 2)

    def body(p_ref, w_ref, o_ref):
        o_ref[...] = w_ref[...].astype(dtype)

    return pl.pallas_call(
        body, name=name,
        grid_spec=pltpu.PrefetchScalarGridSpec(
            num_scalar_prefetch=1, grid=(R // tr,), in_specs=[pl.BlockSpec((tr, C), lambda i, p_ref: (i, 0))],
            out_specs=pl.BlockSpec((None, tr, C), lambda i, p_ref: (p_ref[0], i, 0))),
        out_shape=jax.ShapeDtypeStruct((N_CHIPS, R, C), dtype), compiler_params=_cparams(("parallel",)),
    )(p_arr, w)


def _add_half(name, g3, r3, c_arr):
    n, h, C = r3.shape
    tr = _row_tile(h, C, 3)
    nb = h // tr

    def body(c_ref, g_ref, r_ref, o_ref):
        o_ref[...] = (g_ref[...] + r_ref[...]).astype(BF16)

    blk = pl.BlockSpec((None, tr, C), lambda s, i, c_ref: (s, i, 0))
    return pl.pallas_call(
        body, name=name,
        grid_spec=pltpu.PrefetchScalarGridSpec(
            num_scalar_prefetch=1, grid=(n, nb),
            in_specs=[pl.BlockSpec((None, tr, C), lambda s, i, c_ref: (s, c_ref[0] * nb + i, 0)), blk], out_specs=blk),
        out_shape=jax.ShapeDtypeStruct(r3.shape, BF16), compiler_params=_cparams(("parallel", "parallel")),
    )(c_arr, g3, r3)


def _add_chips(name, t3, r3, cp_arr):
    n, h, C = r3.shape
    tr = _row_tile(h, C, 6)
    nb = h // tr

    def body(cp_ref, t_ref, r0_ref, r1_ref, r2_ref, r3_ref, o_ref):
        p = cp_ref[1]
        total = None
        for a, r_ref in enumerate((r0_ref, r1_ref, r2_ref, r3_ref)):
            part = jnp.where(p == a, t_ref[...], r_ref[...]).astype(F32)
            total = part if total is None else total + part
        o_ref[...] = total

    def part(a):
        return pl.BlockSpec((None, tr, C), lambda i, cp_ref: (jnp.where(cp_ref[1] == a, (a + 1) % N_CHIPS, a), i, 0))

    return pl.pallas_call(
        body, name=name,
        grid_spec=pltpu.PrefetchScalarGridSpec(
            num_scalar_prefetch=1, grid=(nb,),
            in_specs=[pl.BlockSpec((None, tr, C), lambda i, cp_ref: (cp_ref[1], i, 0)), part(0), part(1), part(2), part(3)],
            out_specs=pl.BlockSpec((tr, C), lambda i, cp_ref: (cp_ref[0] * nb + i, 0))),
        out_shape=jax.ShapeDtypeStruct((2 * h, C), F32), compiler_params=_cparams(("parallel",)),
    )(cp_arr, t3, r3, r3, r3, r3)


def _adamw(name, w, g, m, v):
    R, C = w.shape
    tr = _row_tile(R, C, 7)
    c1 = 1.0 - ADAM_B1 ** ADAM_STEP
    c2 = 1.0 - ADAM_B2 ** ADAM_STEP

    def body(w_ref, g_ref, m_ref, v_ref, d_ref, nm_ref, nv_ref):
        gv = g_ref[...]
        nm = ADAM_B1 * m_ref[...] + (1.0 - ADAM_B1) * gv
        nv = ADAM_B2 * v_ref[...] + (1.0 - ADAM_B2) * (gv * gv)
        d_ref[...] = -ADAM_LR * ((nm / c1) / (jnp.sqrt(nv / c2) + ADAM_EPS) + ADAM_WD * w_ref[...])
        nm_ref[...] = nm
        nv_ref[...] = nv

    blk = pl.BlockSpec((tr, C), lambda i: (i, 0))
    o = jax.ShapeDtypeStruct((R, C), F32)
    return pl.pallas_call(
        body, name=name, grid=(R // tr,), in_specs=[blk, blk, blk, blk], out_specs=(blk, blk, blk),
        out_shape=(o, o, o), compiler_params=_cparams(("parallel",)),
    )(w, g, m, v)


def _place():
    x, y, c = lax.axis_index("x"), lax.axis_index("y"), lax.axis_index("c")
    chips = [(1 - x, y), (x, 1 - y), (1 - x, 1 - y)]
    return x, y, c, 2 * x + y, chips


ANY = pl.BlockSpec(memory_space=pl.ANY)


def _gather_weights(name, slots):
    n = len(slots)
    halved = [s.shape[1] % 16 == 0 for s in slots]

    def body(*refs):
        bufs = refs[n:2 * n]
        send, recv = refs[2 * n:]
        x, y, c, p, chips = _place()
        sib = (x, y, 1 - c)

        def rdma(k, ref, to):
            return pltpu.make_async_remote_copy(src_ref=ref, dst_ref=ref, send_sem=send.at[k], recv_sem=recv.at[k],
                                                device_id=to, device_id_type=MESH)

        def half(i, which):
            h = bufs[i].shape[1] // 2
            return pl.ds(pl.multiple_of(which * h, 16), h)

        started = []
        for i in range(n):
            block = bufs[i].at[p, half(i, c)] if halved[i] else bufs[i].at[p]
            for j, chip in enumerate(chips):
                cp = rdma(6 * i + j, block, (*chip, c))
                cp.start()
                started.append(cp)
        for i in range(n):
            for j, (cx, cy) in enumerate(chips):
                q = 2 * cx + cy
                if halved[i]:
                    rdma(6 * i + j, bufs[i].at[q, half(i, c)], (cx, cy, c)).wait_recv()
                    cp = rdma(6 * i + 3 + j, bufs[i].at[q, half(i, c)], sib)
                    cp.start()
                    started.append(cp)
                else:
                    rdma(6 * i + j, bufs[i].at[q], (cx, cy, c)).wait_recv()
        for i in range(n):
            if halved[i]:
                for j, (cx, cy) in enumerate(chips):
                    rdma(6 * i + 3 + j, bufs[i].at[2 * cx + cy, half(i, 1 - c)], sib).wait_recv()
        for cp in started:
            cp.wait_send()

    return pl.pallas_call(
        body, name=name, in_specs=[ANY] * n, out_specs=tuple([ANY] * n),
        out_shape=tuple(jax.ShapeDtypeStruct(s.shape, s.dtype) for s in slots),
        scratch_shapes=[pltpu.SemaphoreType.DMA((6 * n,)), pltpu.SemaphoreType.DMA((6 * n,))],
        input_output_aliases={i: i for i in range(n)},
        compiler_params=pltpu.CompilerParams(has_side_effects=True),
    )(*slots)


def _swap_halves(name, grads):
    n = len(grads)

    def body(*refs):
        ins, outs = refs[:n], refs[n:2 * n]
        send, recv = refs[2 * n:]
        x, y, c, p, chips = _place()
        cps = []
        for i in range(n):
            h = ins[i].shape[1] // 2
            other = pl.ds(pl.multiple_of((1 - c) * h, 8), h)
            cp = pltpu.make_async_remote_copy(src_ref=ins[i].at[:, other, :], dst_ref=outs[i], send_sem=send.at[i],
                                              recv_sem=recv.at[i], device_id=(x, y, 1 - c), device_id_type=MESH)
            cp.start()
            cps.append(cp)
        for cp in cps:
            cp.wait()

    return pl.pallas_call(
        body, name=name, in_specs=[ANY] * n, out_specs=tuple([ANY] * n),
        out_shape=tuple(jax.ShapeDtypeStruct((g.shape[0], g.shape[1] // 2, g.shape[2]), g.dtype) for g in grads),
        scratch_shapes=[pltpu.SemaphoreType.DMA((n,)), pltpu.SemaphoreType.DMA((n,))],
        compiler_params=pltpu.CompilerParams(has_side_effects=True),
    )(*grads)


def _exchange_chips(name, parts):
    n = len(parts)

    def body(*refs):
        ins, outs = refs[:n], refs[n:2 * n]
        send, recv = refs[2 * n:]
        x, y, c, p, chips = _place()
        cps = []
        for i in range(n):
            for j, (cx, cy) in enumerate(chips):
                cp = pltpu.make_async_remote_copy(src_ref=ins[i].at[2 * cx + cy], dst_ref=outs[i].at[p], send_sem=send.at[3 * i + j],
                                                  recv_sem=recv.at[3 * i + j], device_id=(cx, cy, c), device_id_type=MESH)
                cp.start()
                cps.append(cp)
        for i in range(n):
            for j, (cx, cy) in enumerate(chips):
                q = 2 * cx + cy
                pltpu.make_async_remote_copy(src_ref=outs[i].at[q], dst_ref=outs[i].at[q], send_sem=send.at[3 * i + j],
                                             recv_sem=recv.at[3 * i + j], device_id=(cx, cy, c), device_id_type=MESH).wait_recv()
        for cp in cps:
            cp.wait_send()

    return pl.pallas_call(
        body, name=name, in_specs=[ANY] * n, out_specs=tuple([ANY] * n),
        out_shape=tuple(jax.ShapeDtypeStruct(t.shape, t.dtype) for t in parts),
        scratch_shapes=[pltpu.SemaphoreType.DMA((3 * n,)), pltpu.SemaphoreType.DMA((3 * n,))],
        compiler_params=pltpu.CompilerParams(has_side_effects=True),
    )(*parts)


def _join_halves(name, bufs):
    n = len(bufs)

    def body(*refs):
        outs = refs[n:2 * n]
        send, recv = refs[2 * n:]
        x, y, c, p, chips = _place()

        def rdma(i, which):
            h = outs[i].shape[0] // 2
            rows = outs[i].at[pl.ds(pl.multiple_of(which * h, 8), h)]
            return pltpu.make_async_remote_copy(src_ref=rows, dst_ref=rows, send_sem=send.at[i], recv_sem=recv.at[i],
                                                device_id=(x, y, 1 - c), device_id_type=MESH)

        cps = [rdma(i, c) for i in range(n)]
        for cp in cps:
            cp.start()
        for i, cp in enumerate(cps):
            rdma(i, 1 - c).wait_recv()
            cp.wait_send()

    return pl.pallas_call(
        body, name=name, in_specs=[ANY] * n, out_specs=tuple([ANY] * n),
        out_shape=tuple(jax.ShapeDtypeStruct(t.shape, t.dtype) for t in bufs),
        scratch_shapes=[pltpu.SemaphoreType.DMA((n,)), pltpu.SemaphoreType.DMA((n,))],
        input_output_aliases={i: i for i in range(n)},
        compiler_params=pltpu.CompilerParams(has_side_effects=True),
    )(*bufs)


def _allreduce_small(name, pack):
    R, W = pack.shape

    def body(in_ref, out_ref, slots, send, recv):
        x, y, c = lax.axis_index("x"), lax.axis_index("y"), lax.axis_index("c")
        me = 4 * x + 2 * y + c
        slots[0] = in_ref[...]
        cps = []
        for k in range(1, N_DEV):
            peer = (x ^ (k >> 2), y ^ ((k >> 1) & 1), c ^ (k & 1))
            cp = pltpu.make_async_remote_copy(src_ref=in_ref, dst_ref=slots.at[k], send_sem=send.at[k - 1],
                                              recv_sem=recv.at[k - 1], device_id=peer, device_id_type=MESH)
            cp.start()
            cps.append(cp)
        for cp in cps:
            cp.wait()
        total = slots[me]
        for a in range(1, N_DEV):
            total = total + slots[jnp.bitwise_xor(a, me)]
        out_ref[...] = total

    vmem = pl.BlockSpec(memory_space=pltpu.VMEM)
    return pl.pallas_call(
        body, name=name, in_specs=[vmem], out_specs=vmem, out_shape=jax.ShapeDtypeStruct((R, W), F32),
        scratch_shapes=[pltpu.VMEM((N_DEV, R, W), F32), pltpu.SemaphoreType.DMA((N_DEV - 1,)), pltpu.SemaphoreType.DMA((N_DEV - 1,))],
        compiler_params=pltpu.CompilerParams(has_side_effects=True),
    )(pack)


def _heads(a, n_heads):
    S = a.shape[0]
    return a.reshape(S, n_heads, a.shape[1] // n_heads).transpose(1, 0, 2)


def _unheads(a):
    H, S, dh = a.shape
    return a.transpose(1, 0, 2).reshape(S, H * dh)


def _ffn_fwd(tag, xin, gain, wgu3, wd):
    h = _rms_fwd(f"rms_fwd_{tag}", xin, gain)
    gu, act = _ffn_up(f"ffn_up_{tag}", h, wgu3)
    xout = _mm_res(f"ffn_down_{tag}", act, wd, xin, 0.5)
    return xout, (h, gu, act)


def _ffn_bwd(tag, xin, gain, wgu3, wd, saved, dxout):
    h, gu, act = saved
    Ns = wgu3.shape[2]
    dxo_b = dxout.astype(BF16)
    dgu = _ffn_down_bwd(f"ffn_down_bwd_{tag}", dxo_b, wd, gu, 0.5)
    dwd = _mm_tn(f"dw_down_{tag}", act, dxo_b, 0.5)
    dh = _mm_nt_cols(f"ffn_up_bwd_{tag}", dgu, wgu3, a_is_gu=True)
    dwgu = _mm_tn_cols(f"dw_gu_{tag}", h, dgu, Ns, b_is_gu=True)
    dxin, dgain = _rms_bwd(f"rms_bwd_{tag}", xin, gain, dh, dxout)
    return dxin, dgain, dwgu, dwd


def kernel(x, g_ffn1, w_gu1, w_down1, g_mix, w_in, conv_w, q_norm_g, k_norm_g, sinks, w_out_conv, w_out_attn, w_o, g_ffn2, w_gu2, w_down2, loss_target, m_g_ffn1, m_w_gu1, m_w_down1, m_g_mix, m_w_in, m_conv_w, m_q_norm_g, m_k_norm_g, m_sinks, m_w_out_conv, m_w_out_attn, m_w_o, m_g_ffn2, m_w_gu2, m_w_down2, v_g_ffn1, v_w_gu1, v_w_down1, v_g_mix, v_w_in, v_conv_w, v_q_norm_g, v_k_norm_g, v_sinks, v_w_out_conv, v_w_out_attn, v_w_o, v_g_ffn2, v_w_gu2, v_w_down2):
    S, D = x.shape[1], x.shape[2]
    dh = q_norm_g.shape[1]
    HQ = sinks.shape[1]
    HKV = HQ // 4
    AW, KVW, CW = HQ * dh, HKV * dh, D // 2
    off_q, off_k, off_v = 3 * CW, 3 * CW + AW, 3 * CW + AW + KVW
    off_ga, off_gb = off_v + KVW, off_v + KVW + D
    x0, target = x[0], loss_target[0]
    cx, cy, cc = lax.axis_index("x"), lax.axis_index("y"), lax.axis_index("c")
    chip = 2 * cx + cy
    p_arr = jnp.reshape(chip, (1,)).astype(jnp.int32)
    c_arr = jnp.reshape(cc, (1,)).astype(jnp.int32)
    cp_arr = jnp.stack([cc, chip]).astype(jnp.int32)
    wts = dict(g_ffn1=g_ffn1, w_gu1=w_gu1, w_down1=w_down1, g_mix=g_mix, w_in=w_in, conv_w=conv_w, q_norm_g=q_norm_g,
               k_norm_g=k_norm_g, sinks=sinks, w_out_conv=w_out_conv, w_out_attn=w_out_attn, w_o=w_o, g_ffn2=g_ffn2,
               w_gu2=w_gu2, w_down2=w_down2)
    ms = dict(g_ffn1=m_g_ffn1, w_gu1=m_w_gu1, w_down1=m_w_down1, g_mix=m_g_mix, w_in=m_w_in, conv_w=m_conv_w,
              q_norm_g=m_q_norm_g, k_norm_g=m_k_norm_g, sinks=m_sinks, w_out_conv=m_w_out_conv, w_out_attn=m_w_out_attn,
              w_o=m_w_o, g_ffn2=m_g_ffn2, w_gu2=m_w_gu2, w_down2=m_w_down2)
    vs = dict(g_ffn1=v_g_ffn1, w_gu1=v_w_gu1, w_down1=v_w_down1, g_mix=v_g_mix, w_in=v_w_in, conv_w=v_conv_w,
              q_norm_g=v_q_norm_g, k_norm_g=v_k_norm_g, sinks=v_sinks, w_out_conv=v_w_out_conv, w_out_attn=v_w_out_attn,
              w_o=v_w_o, g_ffn2=v_g_ffn2, w_gu2=v_w_gu2, w_down2=v_w_down2)
    order = list(wts)
    groups = {"1": ("w_gu1", "w_down1"), "mix": ("w_in", "w_out_conv", "w_out_attn", "w_o"), "2": ("w_gu2", "w_down2")}
    big = [k for keys in groups.values() for k in keys]
    small_names = [k for k in order if k not in big]
    grad, delta, new_m, new_v = {}, {}, {}, {}

    def gather(tag, keys):
        slots = [_cast_to_slot(f"cast_{k}", wts[k][0], F32 if k == "conv_w" else BF16, p_arr) for k in keys]
        return dict(zip(keys, _gather_weights(f"gather_{tag}", slots)))

    def reduce_and_update(tag, full):
        keys = list(full)
        gs = [full[k] for k in keys]
        sib = _swap_halves(f"swap_{tag}", gs)
        parts = [_add_half(f"add_half_{k}", g, r, c_arr) for k, g, r in zip(keys, gs, sib)]
        by_chip = _exchange_chips(f"exchange_{tag}", parts)
        halves = [_add_chips(f"add_chips_{k}", t, r, cp_arr) for k, t, r in zip(keys, parts, by_chip)]
        for k, g2 in zip(keys, _join_halves(f"join_{tag}", halves)):
            d, nm, nv = _adamw(f"adamw_{k}", wts[k][0], g2, ms[k][0], vs[k][0])
            grad[k], delta[k], new_m[k], new_v[k] = g2[None], d[None], nm[None], nv[None]

    W = {}
    W.update(gather("1", groups["1"]))
    W.update(gather("mix", groups["mix"] + ("conv_w",)))
    W.update(gather("2", groups["2"]))
    wd1 = W["w_down1"].reshape(-1, D)
    wd2 = W["w_down2"].reshape(-1, D)
    wo = W["w_o"].reshape(-1, D)
    wgu1, wgu2, win3, woc3, woa3, convw3 = W["w_gu1"], W["w_gu2"], W["w_in"], W["w_out_conv"], W["w_out_attn"], W["conv_w"]
    cos, sin, rm, rmt = _rope_consts(S, dh)
    sink_vec = sinks[0]

    x1, saved1 = _ffn_fwd("1", x0, g_ffn1, wgu1, wd1)
    h2 = _rms_fwd("rms_fwd_mix", x1, g_mix)
    proj = _mm_cols("in_proj", h2, win3, F32)
    aconv = _conv_fwd("conv_fwd", proj, convw3, CW)
    ya = _mm_cols("out_conv", aconv, woc3, F32)
    q_raw = _heads(proj[:, off_q:off_q + AW], HQ)
    k_raw = _heads(proj[:, off_k:off_k + KVW], HKV)
    vh = _heads(proj[:, off_v:off_v + KVW], HKV).astype(BF16)
    qn = _qk_prep("q_prep", q_raw, q_norm_g, cos, sin, rm)
    kn = _qk_prep("k_prep", k_raw, k_norm_g, cos, sin, rm)
    oh = _attn_fwd("attn_fwd", qn, kn, vh, sink_vec)
    o = _unheads(oh)
    yb = _mm_cols("out_attn", o, woa3, F32)
    merged = _gate_fwd("gate_fwd", proj, ya, yb, off_ga, off_gb)
    x2 = _mm_res("mix_out", merged, wo, x1, 1.0)
    x3, saved2 = _ffn_fwd("2", x2, g_ffn2, wgu2, wd2)

    dy, loss_lanes = _loss_grad("loss_grad", x3, target)
    dx2, dg_ffn2, dwgu2, dwd2 = _ffn_bwd("2", x2, g_ffn2, wgu2, wd2, saved2, dy)
    reduce_and_update("2", dict(w_gu2=dwgu2, w_down2=dwd2.reshape(N_CHIPS, -1, D)))
    dx2_b = dx2.astype(BF16)
    dmerged = _mm_nt("mix_out_bwd", dx2_b, wo, F32)
    dwo = _mm_tn("dw_o", merged, dx2_b)
    dga, dgb, dya, dyb = _gate_bwd("gate_bwd", proj, ya, yb, dmerged, off_ga, off_gb)
    daconv = _mm_nt_cols("out_conv_bwd", dya, woc3)
    dwoc = _mm_tn_cols("dw_out_conv", aconv, dya, woc3.shape[2])
    do = _mm_nt_cols("out_attn_bwd", dyb, woa3)
    dwoa = _mm_tn_cols("dw_out_attn", o, dyb, woa3.shape[2])
    dxc, dbg, dcg, dconvw = _conv_bwd("conv_bwd", proj, convw3, daconv, CW)
    dqn, dkn, dvh, dsink3 = _attn_bwd("attn_bwd", qn, kn, vh, sink_vec, _heads(do, HQ).astype(BF16))
    dq_raw, dqg = _qk_prep_bwd("q_prep_bwd", q_raw, q_norm_g, cos, sin, rmt, dqn)
    dk_raw, dkg = _qk_prep_bwd("k_prep_bwd", k_raw, k_norm_g, cos, sin, rmt, dkn)
    dproj = jnp.concatenate([dxc, dbg, dcg, _unheads(dq_raw), _unheads(dk_raw), _unheads(dvh).astype(BF16), dga, dgb], axis=1)
    dh2 = _mm_nt_cols("in_proj_bwd", dproj, win3)
    dwin = _mm_tn_cols("dw_in", h2, dproj, win3.shape[2])
    reduce_and_update("mix", dict(w_in=dwin, w_out_conv=dwoc, w_out_attn=dwoa, w_o=dwo.reshape(N_CHIPS, -1, D)))
    dx1, dg_mix = _rms_bwd("rms_bwd_mix", x1, g_mix, dh2, dx2)
    dx0, dg_ffn1, dwgu1, dwd1 = _ffn_bwd("1", x0, g_ffn1, wgu1, wd1, saved1, dx1)

    reduce_and_update("1", dict(w_gu1=dwgu1, w_down1=dwd1.reshape(N_CHIPS, -1, D)))

    def rows8(a):
        a = a.reshape(-1, a.shape[-1])
        return jnp.pad(a, ((0, -a.shape[0] % 8), (0, D - a.shape[1])))

    misc = jnp.concatenate([dqg, dkg, dsink3[:, :, 0].reshape(1, HQ), loss_lanes], axis=1)
    tot = _allreduce_small("allreduce_small", jnp.concatenate([rows8(a) for a in (dg_ffn1, dg_mix, dg_ffn2, dconvw, misc)], axis=0))
    cw_s = conv_w.shape[2]
    conv_row0, misc_row = 24, 24 + (-(-N_CHIPS * CONV_K // 8)) * 8
    small_g = dict(g_ffn1=tot[0:1], g_mix=tot[8:9], g_ffn2=tot[16:17],
                   conv_w=lax.dynamic_slice(tot, (conv_row0 + CONV_K * chip, 0), (CONV_K, cw_s)),
                   q_norm_g=tot[misc_row:misc_row + 1, 0:dh], k_norm_g=tot[misc_row:misc_row + 1, dh:2 * dh],
                   sinks=tot[misc_row:misc_row + 1, 2 * dh:2 * dh + HQ])
    loss = (0.5 / D) * jnp.sum(tot[misc_row, 2 * dh + HQ:2 * dh + HQ + LANES])

    def small_pack(src):
        return jnp.concatenate([rows8(src[k]) for k in small_names], axis=0)

    sd, sm, sv = _adamw("adamw_small", small_pack(wts), small_pack(small_g), small_pack(ms), small_pack(vs))
    for i, k in enumerate(small_names):
        shape = wts[k].shape
        nr, ncol = math.prod(shape[:-1]), shape[-1]
        grad[k] = small_g[k].reshape(shape)
        delta[k], new_m[k], new_v[k] = (a[8 * i:8 * i + nr, 0:ncol].reshape(shape) for a in (sd, sm, sv))
    return (loss, dx0[None], *[grad[k] for k in order], *[delta[k] for k in order],
            *[new_m[k] for k in order], *[new_v[k] for k in order])
```

```python
import math

import numpy as np
import jax
import jax.numpy as jnp
from jax import lax
from jax.experimental import pallas as pl
from jax.experimental.pallas import tpu as pltpu

F32 = jnp.float32
BF16 = jnp.bfloat16
MESH = pl.DeviceIdType.MESH

RMS_EPS = 1e-6
BLOCK = 128
ROPE_THETA = 500000.0
NEG_INF = -1e30
CONV_K = 3
ADAM_LR, ADAM_B1, ADAM_B2, ADAM_EPS, ADAM_WD, ADAM_STEP = 0.001, 0.9, 0.999, 1e-08, 0.01, 10

VMEM_LIMIT_V7X = 48 * 1024 * 1024
LANES = 128
N_CHIPS = 4
N_DEV = 8


def _tile(n, want, align=LANES):
    best = None
    t = align
    while t <= min(n, want):
        if n % t == 0:
            best = t
        t += align
    return best or n


def _cparams(sem):
    return pltpu.CompilerParams(dimension_semantics=sem, vmem_limit_bytes=VMEM_LIMIT_V7X)


def _sigmoid(x):
    return 1.0 / (1.0 + jnp.exp(-x))


def _mm(name, grid, ins, in_specs, pairs, dims, acc_shape, out_shape, out_specs, epilogue, dep=None):
    nk = grid[-1]
    if dep is not None:
        ins, in_specs = tuple(ins) + (dep,), list(in_specs) + [pl.BlockSpec(dep.shape, lambda *_: (0, 0))]
    n_in = len(ins)
    n_acc = len(pairs)
    multi = isinstance(out_shape, (tuple, list))
    n_out = len(out_shape) if multi else 1

    def body(*refs):
        in_refs = refs[:n_in]
        out_refs = refs[n_in:n_in + n_out]
        accs = refs[n_in + n_out:]
        k = pl.program_id(len(grid) - 1)

        @pl.when(k == 0)
        def _():
            for acc in accs:
                acc[...] = jnp.zeros_like(acc)

        for acc, (ai, bi) in zip(accs, pairs):
            acc[...] += lax.dot_general(in_refs[ai][...], in_refs[bi][...], dims, preferred_element_type=F32)

        @pl.when(k == nk - 1)
        def _():
            epilogue([acc[...] for acc in accs], in_refs, out_refs)

    sem = ("parallel",) * (len(grid) - 1) + ("arbitrary",)
    return pl.pallas_call(
        body, name=name, grid=grid, in_specs=in_specs, out_specs=out_specs, out_shape=out_shape,
        scratch_shapes=[pltpu.VMEM(acc_shape, F32) for _ in range(n_acc)], compiler_params=_cparams(sem),
    )(*ins)


NN = (((1,), (0,)), ((), ()))
NT = (((1,), (1,)), ((), ()))
TN = (((0,), (0,)), ((), ()))


def _ffn_up(name, h, wgu3):
    S, D = h.shape
    Ns = wgu3.shape[2]
    F = 2 * Ns
    tm, tn, tk = _tile(S, 512), _tile(Ns, 1408), _tile(D, 512)
    nbs = Ns // tn

    def epi(accs, in_refs, out_refs):
        g, u = accs
        gu_ref, a_ref = out_refs
        gu_ref[0] = g.astype(BF16)
        gu_ref[1] = u.astype(BF16)
        a_ref[...] = (g * _sigmoid(g) * u).astype(BF16)

    return _mm(
        name, (S // tm, F // tn, D // tk), (h, wgu3, wgu3),
        [pl.BlockSpec((tm, tk), lambda i, j, k: (i, k)),
         pl.BlockSpec((None, tk, tn), lambda i, j, k: (j // nbs, k, j % nbs)),
         pl.BlockSpec((None, tk, tn), lambda i, j, k: (2 + j // nbs, k, j % nbs))],
        [(0, 1), (0, 2)], NN, (tm, tn),
        (jax.ShapeDtypeStruct((2, S, F), BF16), jax.ShapeDtypeStruct((S, F), BF16)),
        (pl.BlockSpec((2, tm, tn), lambda i, j, k: (0, i, j)), pl.BlockSpec((tm, tn), lambda i, j, k: (i, j))),
        epi)


def _mm_res(name, a, w, res, scale):
    S, K = a.shape
    N = w.shape[1]
    tm, tn, tk = _tile(S, 1024), _tile(N, 1024), _tile(K, 512)

    def epi(accs, in_refs, out_refs):
        out_refs[0][...] = in_refs[2][...] + scale * accs[0]

    return _mm(
        name, (S // tm, N // tn, K // tk), (a, w, res),
        [pl.BlockSpec((tm, tk), lambda i, j, k: (i, k)), pl.BlockSpec((tk, tn), lambda i, j, k: (k, j)),
         pl.BlockSpec((tm, tn), lambda i, j, k: (i, j))],
        [(0, 1)], NN, (tm, tn), jax.ShapeDtypeStruct((S, N), F32), pl.BlockSpec((tm, tn), lambda i, j, k: (i, j)), epi)


def _mm_cols(name, a, w3, out_dtype):
    S, K = a.shape
    Ns = w3.shape[2]
    tm, tn, tk = _tile(S, 512), _tile(Ns, 2304), _tile(K, 512)
    nbs = Ns // tn

    def epi(accs, in_refs, out_refs):
        out_refs[0][...] = accs[0].astype(out_dtype)

    return _mm(
        name, (S // tm, N_CHIPS * nbs, K // tk), (a, w3),
        [pl.BlockSpec((tm, tk), lambda i, j, k: (i, k)),
         pl.BlockSpec((None, tk, tn), lambda i, j, k: (j // nbs, k, j % nbs))],
        [(0, 1)], NN, (tm, tn), jax.ShapeDtypeStruct((S, N_CHIPS * Ns), out_dtype),
        pl.BlockSpec((tm, tn), lambda i, j, k: (i, j)), epi)


def _mm_nt(name, a, w, out_dtype, scale=1.0):
    S, N = a.shape
    K = w.shape[0]
    tm, tn, tk = _tile(S, 1024), _tile(K, 1024), _tile(N, 512)

    def epi(accs, in_refs, out_refs):
        out_refs[0][...] = (scale * accs[0]).astype(out_dtype)

    return _mm(
        name, (S // tm, K // tn, N // tk), (a, w),
        [pl.BlockSpec((tm, tk), lambda i, j, k: (i, k)), pl.BlockSpec((tn, tk), lambda i, j, k: (j, k))],
        [(0, 1)], NT, (tm, tn), jax.ShapeDtypeStruct((S, K), out_dtype), pl.BlockSpec((tm, tn), lambda i, j, k: (i, j)), epi)


def _ffn_down_bwd(name, dy, wd, gu, scale, dep=None):
    S, D = dy.shape
    F = wd.shape[0]
    tm, tn, tk = _tile(S, 512), _tile(F, 1408), _tile(D, 512)

    def epi(accs, in_refs, out_refs):
        da = scale * accs[0]
        g = in_refs[2][0].astype(F32)
        u = in_refs[2][1].astype(F32)
        sg = _sigmoid(g)
        out_refs[0][0] = (da * u * (sg * (1.0 + g * (1.0 - sg)))).astype(BF16)
        out_refs[0][1] = (da * (g * sg)).astype(BF16)

    return _mm(
        name, (S // tm, F // tn, D // tk), (dy, wd, gu),
        [pl.BlockSpec((tm, tk), lambda i, j, k: (i, k)), pl.BlockSpec((tn, tk), lambda i, j, k: (j, k)),
         pl.BlockSpec((2, tm, tn), lambda i, j, k: (0, i, j))],
        [(0, 1)], NT, (tm, tn), jax.ShapeDtypeStruct((2, S, F), BF16),
        pl.BlockSpec((2, tm, tn), lambda i, j, k: (0, i, j)), epi, dep=dep)


def _mm_nt_cols(name, a, w3, a_is_gu=False, dep=None):
    K, Ns = w3.shape[1], w3.shape[2]
    S = a.shape[1] if a_is_gu else a.shape[0]
    tm, tn, tk = _tile(S, 1024), _tile(K, 1024), _tile(Ns, 2304)
    nbs = Ns // tk
    if a_is_gu:
        a_spec = pl.BlockSpec((None, tm, tk), lambda i, j, k: (k // (2 * nbs), i, k % (2 * nbs)))
    else:
        a_spec = pl.BlockSpec((tm, tk), lambda i, j, k: (i, k))

    def epi(accs, in_refs, out_refs):
        out_refs[0][...] = accs[0]

    return _mm(
        name, (S // tm, K // tn, N_CHIPS * nbs), (a, w3),
        [a_spec, pl.BlockSpec((None, tn, tk), lambda i, j, k: (k // nbs, j, k % nbs))],
        [(0, 1)], NT, (tm, tn), jax.ShapeDtypeStruct((S, K), F32), pl.BlockSpec((tm, tn), lambda i, j, k: (i, j)), epi, dep=dep)


def _mm_tn(name, a, b, scale=1.0, dep=None):
    S, K = a.shape
    N = b.shape[1]
    tm, tn, tk = _tile(K, 512), _tile(N, 1024), _tile(S, 512)

    def epi(accs, in_refs, out_refs):
        out_refs[0][...] = scale * accs[0]

    return _mm(
        name, (K // tm, N // tn, S // tk), (a, b),
        [pl.BlockSpec((tk, tm), lambda i, j, k: (k, i)), pl.BlockSpec((tk, tn), lambda i, j, k: (k, j))],
        [(0, 1)], TN, (tm, tn), jax.ShapeDtypeStruct((K, N), F32), pl.BlockSpec((tm, tn), lambda i, j, k: (i, j)), epi, dep=dep)


def _mm_tn_cols(name, a, b, Ns, b_is_gu=False, dep=None):
    S, K = a.shape
    tm, tn, tk = _tile(K, 512), _tile(Ns, 2304), _tile(S, 512)
    nbs = Ns // tn
    if b_is_gu:
        b_spec = pl.BlockSpec((None, tk, tn), lambda i, j, k: (j // (2 * nbs), k, j % (2 * nbs)))
    else:
        b_spec = pl.BlockSpec((tk, tn), lambda i, j, k: (k, j))

    def epi(accs, in_refs, out_refs):
        out_refs[0][...] = accs[0]

    return _mm(
        name, (K // tm, N_CHIPS * nbs, S // tk), (a, b),
        [pl.BlockSpec((tk, tm), lambda i, j, k: (k, i)), b_spec],
        [(0, 1)], TN, (tm, tn), jax.ShapeDtypeStruct((N_CHIPS, K, Ns), F32),
        pl.BlockSpec((None, tm, tn), lambda i, j, k: (j // nbs, i, j % nbs)), epi, dep=dep)


def _rms_fwd(name, x, gain, dep=None):
    S, D = x.shape
    tm = _tile(S, 256, 8)
    extra = () if dep is None else (dep,)

    def body(x_ref, g_ref, *rest):
        h_ref = rest[-1]
        xv = x_ref[...]
        r = lax.rsqrt(jnp.mean(xv * xv, axis=-1, keepdims=True) + RMS_EPS)
        h_ref[...] = (xv * r * g_ref[...]).astype(BF16)

    return pl.pallas_call(
        body, name=name, grid=(S // tm,),
        in_specs=[pl.BlockSpec((tm, D), lambda i: (i, 0)), pl.BlockSpec((1, D), lambda i: (0, 0))]
        + [pl.BlockSpec(d.shape, lambda i: (0, 0)) for d in extra],
        out_specs=pl.BlockSpec((tm, D), lambda i: (i, 0)), out_shape=jax.ShapeDtypeStruct((S, D), BF16),
        compiler_params=_cparams(("parallel",)),
    )(x, gain, *extra)


def _rms_bwd(name, x, gain, dh, dres):
    S, D = x.shape
    tm = _tile(S, 256, 8)

    def body(x_ref, g_ref, dh_ref, dres_ref, dx_ref, dg_ref):
        i = pl.program_id(0)
        xv = x_ref[...]
        r = lax.rsqrt(jnp.mean(xv * xv, axis=-1, keepdims=True) + RMS_EPS)
        xhat = xv * r
        dhv = dh_ref[...]
        dxhat = dhv * g_ref[...]
        dx_ref[...] = dres_ref[...] + r * (dxhat - xhat * jnp.mean(dxhat * xhat, axis=-1, keepdims=True))

        @pl.when(i == 0)
        def _():
            dg_ref[...] = jnp.zeros_like(dg_ref)

        dg_ref[...] += jnp.sum(dhv * xhat, axis=0, keepdims=True)

    row = pl.BlockSpec((tm, D), lambda i: (i, 0))
    vec = pl.BlockSpec((1, D), lambda i: (0, 0))
    return pl.pallas_call(
        body, name=name, grid=(S // tm,), in_specs=[row, vec, row, row], out_specs=(row, vec),
        out_shape=(jax.ShapeDtypeStruct((S, D), F32), jax.ShapeDtypeStruct((1, D), F32)),
        compiler_params=_cparams(("arbitrary",)),
    )(x, gain, dh, dres)


def _loss_grad(name, y, target):
    S, D = y.shape
    tm = _tile(S, 256, 8)

    def body(y_ref, t_ref, dy_ref, l_ref):
        i = pl.program_id(0)
        e = y_ref[...] - t_ref[...]
        dy_ref[...] = e * (1.0 / D)
        col = jnp.sum(e * e, axis=0, keepdims=True)
        part = col[:, 0:LANES]
        for k in range(1, D // LANES):
            part = part + col[:, k * LANES:(k + 1) * LANES]

        @pl.when(i == 0)
        def _():
            l_ref[...] = jnp.zeros_like(l_ref)

        l_ref[...] += part

    row = pl.BlockSpec((tm, D), lambda i: (i, 0))
    return pl.pallas_call(
        body, name=name, grid=(S // tm,), in_specs=[row, row],
        out_specs=(row, pl.BlockSpec((1, LANES), lambda i: (0, 0))),
        out_shape=(jax.ShapeDtypeStruct((S, D), F32), jax.ShapeDtypeStruct((1, LANES), F32)),
        compiler_params=_cparams(("arbitrary",)),
    )(y, target)


def _shift_down(u, k):
    rows = lax.broadcasted_iota(jnp.int32, u.shape, 0)
    return jnp.where(rows >= k, pltpu.roll(u, k, 0), 0.0)


def _shift_up(u, k):
    n = u.shape[0]
    rows = lax.broadcasted_iota(jnp.int32, u.shape, 0)
    return jnp.where(rows < n - k, pltpu.roll(u, n - k, 0), 0.0)


def _conv_specs(S, cw, conv_width):
    nb = conv_width // cw
    col = lambda off: pl.BlockSpec((S, cw), lambda j, off=off: (0, off * nb + j))
    return nb, col(0), col(1), col(2)


def _conv_fwd(name, proj, convw3, conv_width):
    S = proj.shape[0]
    cw = convw3.shape[2]
    nb, xc_s, bg_s, cg_s = _conv_specs(S, cw, conv_width)

    def body(xc_ref, bg_ref, cg_ref, w_ref, o_ref):
        u = cg_ref[...] * xc_ref[...]
        w = w_ref[...]
        cv = w[2:3, :] * u + w[1:2, :] * _shift_down(u, 1) + w[0:1, :] * _shift_down(u, 2)
        o_ref[...] = (bg_ref[...] * cv).astype(BF16)

    return pl.pallas_call(
        body, name=name, grid=(nb,),
        in_specs=[xc_s, bg_s, cg_s, pl.BlockSpec((None, CONV_K, cw), lambda j: (j, 0, 0))],
        out_specs=pl.BlockSpec((S, cw), lambda j: (0, j)), out_shape=jax.ShapeDtypeStruct((S, conv_width), BF16),
        compiler_params=_cparams(("parallel",)),
    )(proj, proj, proj, convw3)


def _conv_bwd(name, proj, convw3, da, conv_width):
    S = proj.shape[0]
    cw = convw3.shape[2]
    nb, xc_s, bg_s, cg_s = _conv_specs(S, cw, conv_width)

    def body(xc_ref, bg_ref, cg_ref, w_ref, da_ref, dxc_ref, dbg_ref, dcg_ref, dw_ref):
        xc, cg = xc_ref[...], cg_ref[...]
        u = cg * xc
        w = w_ref[...]
        u1, u2 = _shift_down(u, 1), _shift_down(u, 2)
        cv = w[2:3, :] * u + w[1:2, :] * u1 + w[0:1, :] * u2
        dav = da_ref[...]
        dbg_ref[...] = (dav * cv).astype(BF16)
        dcv = dav * bg_ref[...]
        du = w[2:3, :] * dcv + w[1:2, :] * _shift_up(dcv, 1) + w[0:1, :] * _shift_up(dcv, 2)
        dxc_ref[...] = (du * cg).astype(BF16)
        dcg_ref[...] = (du * xc).astype(BF16)
        dw_ref[0:1, :] = jnp.sum(dcv * u2, axis=0, keepdims=True)
        dw_ref[1:2, :] = jnp.sum(dcv * u1, axis=0, keepdims=True)
        dw_ref[2:3, :] = jnp.sum(dcv * u, axis=0, keepdims=True)

    wspec = pl.BlockSpec((None, CONV_K, cw), lambda j: (j, 0, 0))
    ospec = pl.BlockSpec((S, cw), lambda j: (0, j))
    act = jax.ShapeDtypeStruct((S, conv_width), BF16)
    return pl.pallas_call(
        body, name=name, grid=(nb,), in_specs=[xc_s, bg_s, cg_s, wspec, ospec],
        out_specs=(ospec, ospec, ospec, wspec),
        out_shape=(act, act, act, jax.ShapeDtypeStruct(convw3.shape, F32)),
        compiler_params=_cparams(("parallel",)),
    )(proj, proj, proj, convw3, da)


def _rope_consts(S, dh):
    rot = dh // 4
    half = rot // 2
    inv_freq = 1.0 / (ROPE_THETA ** (jnp.arange(0, rot, 2, dtype=F32) / rot))
    ang = jnp.arange(S, dtype=F32)[:, None] * inv_freq[None, :]
    cos = jnp.concatenate([jnp.cos(ang), jnp.cos(ang), jnp.ones((S, dh - rot), F32)], axis=1)
    sin = jnp.concatenate([jnp.sin(ang), jnp.sin(ang), jnp.zeros((S, dh - rot), F32)], axis=1)
    rm = np.zeros((dh, dh), np.float32)
    for j in range(half):
        rm[j + half, j] = -1.0
        rm[j, j + half] = 1.0
    return cos, sin, jnp.asarray(rm, BF16), jnp.asarray(rm.T, BF16)


def _exact_perm(y, rm):
    hi = y.astype(BF16)
    r1 = y - hi.astype(F32)
    mid = r1.astype(BF16)
    lo = (r1 - mid.astype(F32)).astype(BF16)
    dot = lambda a: jnp.dot(a, rm, preferred_element_type=F32)
    return dot(hi) + dot(mid) + dot(lo)


def _qk_prep(name, xh, gain, cos, sin, rm):
    H, S, dh = xh.shape
    tm = _tile(S, 1024, 8)

    def body(x_ref, g_ref, c_ref, s_ref, rm_ref, o_ref):
        xv = x_ref[...]
        y = xv * lax.rsqrt(jnp.mean(xv * xv, axis=-1, keepdims=True) + RMS_EPS) * g_ref[...]
        o_ref[...] = (y * c_ref[...] + _exact_perm(y, rm_ref[...]) * s_ref[...]).astype(BF16)

    blk = pl.BlockSpec((None, tm, dh), lambda h, i: (h, i, 0))
    tab = pl.BlockSpec((tm, dh), lambda h, i: (i, 0))
    return pl.pallas_call(
        body, name=name, grid=(H, S // tm),
        in_specs=[blk, pl.BlockSpec((1, dh), lambda h, i: (0, 0)), tab, tab, pl.BlockSpec((dh, dh), lambda h, i: (0, 0))],
        out_specs=blk, out_shape=jax.ShapeDtypeStruct((H, S, dh), BF16),
        compiler_params=_cparams(("parallel", "parallel")),
    )(xh, gain, cos, sin, rm)


def _qk_prep_bwd(name, xh, gain, cos, sin, rmt, dout):
    H, S, dh = xh.shape
    tm = _tile(S, 1024, 8)

    def body(x_ref, g_ref, c_ref, s_ref, rmt_ref, do_ref, dx_ref, dg_ref):
        first = (pl.program_id(0) == 0) & (pl.program_id(1) == 0)
        xv = x_ref[...]
        r = lax.rsqrt(jnp.mean(xv * xv, axis=-1, keepdims=True) + RMS_EPS)
        xhat = xv * r
        dov = do_ref[...]
        dy = dov * c_ref[...] + _exact_perm(dov * s_ref[...], rmt_ref[...])
        dxhat = dy * g_ref[...]
        dx_ref[...] = (r * (dxhat - xhat * jnp.mean(dxhat * xhat, axis=-1, keepdims=True))).astype(BF16)

        @pl.when(first)
        def _():
            dg_ref[...] = jnp.zeros_like(dg_ref)

        dg_ref[...] += jnp.sum(dy * xhat, axis=0, keepdims=True)

    blk = pl.BlockSpec((None, tm, dh), lambda h, i: (h, i, 0))
    tab = pl.BlockSpec((tm, dh), lambda h, i: (i, 0))
    vec = pl.BlockSpec((1, dh), lambda h, i: (0, 0))
    return pl.pallas_call(
        body, name=name, grid=(H, S // tm),
        in_specs=[blk, vec, tab, tab, pl.BlockSpec((dh, dh), lambda h, i: (0, 0)), blk],
        out_specs=(blk, vec), out_shape=(jax.ShapeDtypeStruct((H, S, dh), BF16), jax.ShapeDtypeStruct((1, dh), F32)),
        compiler_params=_cparams(("arbitrary", "arbitrary")),
    )(xh, gain, cos, sin, rmt, dout)


def _attn_probs(q, kp, kc, sink_col, n, scale):
    rows = q.shape[0]
    sp = lax.dot_general(q, kp, NT, preferred_element_type=F32) * scale
    sc = lax.dot_general(q, kc, NT, preferred_element_type=F32) * scale
    qi = lax.broadcasted_iota(jnp.int32, (rows, BLOCK), 0) % BLOCK
    kj = lax.broadcasted_iota(jnp.int32, (rows, BLOCK), 1)
    sp = jnp.where((kj > qi) & (n > 0), sp, NEG_INF)
    sc = jnp.where(kj <= qi, sc, NEG_INF)
    m = jnp.maximum(jnp.maximum(jnp.max(sp, axis=-1, keepdims=True), jnp.max(sc, axis=-1, keepdims=True)), sink_col)
    pp, pc, ps = jnp.exp(sp - m), jnp.exp(sc - m), jnp.exp(sink_col - m)
    inv = 1.0 / (jnp.sum(pp, axis=-1, keepdims=True) + jnp.sum(pc, axis=-1, keepdims=True) + ps)
    return pp * inv, pc * inv, ps * inv


def _sink_col(sink_ref, hk, group):
    rows = group * BLOCK
    g = lax.broadcasted_iota(jnp.int32, (rows, 1), 0) // BLOCK
    col = jnp.zeros((rows, 1), F32)
    for i in range(group):
        col = jnp.where(g == i, sink_ref[hk * group + i], col)
    return col


def _attn_specs(group, dh):
    qb = pl.BlockSpec((group, BLOCK, dh), lambda hk, n: (hk, n, 0))
    prev = pl.BlockSpec((None, BLOCK, dh), lambda hk, n: (hk, jnp.maximum(n - 1, 0), 0))
    cur = pl.BlockSpec((None, BLOCK, dh), lambda hk, n: (hk, n, 0))
    return qb, prev, cur, pl.BlockSpec(memory_space=pltpu.SMEM)


def _attn_fwd(name, q, k, v, sinks):
    HQ, S, dh = q.shape
    HKV = k.shape[0]
    group = HQ // HKV
    scale = dh ** -0.5
    qb, prev, cur, smem = _attn_specs(group, dh)

    def body(q_ref, kp_ref, kc_ref, vp_ref, vc_ref, sink_ref, o_ref):
        hk, n = pl.program_id(0), pl.program_id(1)
        qv = q_ref[...].reshape(group * BLOCK, dh)
        pp, pc, _ = _attn_probs(qv, kp_ref[...], kc_ref[...], _sink_col(sink_ref, hk, group), n, scale)
        o = jnp.dot(pp.astype(BF16), vp_ref[...], preferred_element_type=F32)
        o = o + jnp.dot(pc.astype(BF16), vc_ref[...], preferred_element_type=F32)
        o_ref[...] = o.reshape(group, BLOCK, dh).astype(BF16)

    return pl.pallas_call(
        body, name=name, grid=(HKV, S // BLOCK), in_specs=[qb, prev, cur, prev, cur, smem], out_specs=qb,
        out_shape=jax.ShapeDtypeStruct((HQ, S, dh), BF16), compiler_params=_cparams(("parallel", "parallel")),
    )(q, k, k, v, v, sinks)


def _attn_bwd(name, q, k, v, sinks, do):
    HQ, S, dh = q.shape
    HKV = k.shape[0]
    group = HQ // HKV
    scale = dh ** -0.5
    qb, prev, cur, smem = _attn_specs(group, dh)
    whole = pl.BlockSpec((None, S, dh), lambda hk, n: (hk, 0, 0))
    sk = pl.BlockSpec((None, group, LANES), lambda hk, n: (hk, 0, 0))

    def body(q_ref, kp_ref, kc_ref, vp_ref, vc_ref, sink_ref, do_ref, dq_ref, dk_ref, dv_ref, ds_ref):
        hk, n = pl.program_id(0), pl.program_id(1)
        rows = group * BLOCK
        qv = q_ref[...].reshape(rows, dh)
        dov = do_ref[...].reshape(rows, dh)
        kp, kc, vp, vc = kp_ref[...], kc_ref[...], vp_ref[...], vc_ref[...]
        pp, pc, ps = _attn_probs(qv, kp, kc, _sink_col(sink_ref, hk, group), n, scale)
        dpp = lax.dot_general(dov, vp, NT, preferred_element_type=F32)
        dpc = lax.dot_general(dov, vc, NT, preferred_element_type=F32)
        delta = jnp.sum(pp * dpp, axis=-1, keepdims=True) + jnp.sum(pc * dpc, axis=-1, keepdims=True)
        dsp = (pp * (dpp - delta) * scale).astype(BF16)
        dsc = (pc * (dpc - delta) * scale).astype(BF16)
        dq = jnp.dot(dsp, kp, preferred_element_type=F32) + jnp.dot(dsc, kc, preferred_element_type=F32)
        dq_ref[...] = dq.reshape(group, BLOCK, dh)

        @pl.when(n == 0)
        def _():
            dk_ref[...] = jnp.zeros_like(dk_ref)
            dv_ref[...] = jnp.zeros_like(dv_ref)
            ds_ref[...] = jnp.zeros_like(ds_ref)

        cur_rows = pl.ds(pl.multiple_of(n * BLOCK, BLOCK), BLOCK)
        prev_rows = pl.ds(pl.multiple_of(jnp.maximum(n - 1, 0) * BLOCK, BLOCK), BLOCK)
        tdot = lambda a, b: lax.dot_general(a, b, TN, preferred_element_type=F32)
        dk_ref[prev_rows, :] += tdot(dsp, qv)
        dv_ref[prev_rows, :] += tdot(pp.astype(BF16), dov)
        dk_ref[cur_rows, :] += tdot(dsc, qv)
        dv_ref[cur_rows, :] += tdot(pc.astype(BF16), dov)
        dsink = -jnp.sum((ps * delta).reshape(group, BLOCK, 1), axis=1)
        ds_ref[...] += jnp.broadcast_to(dsink, (group, LANES))

    return pl.pallas_call(
        body, name=name, grid=(HKV, S // BLOCK), in_specs=[qb, prev, cur, prev, cur, smem, qb],
        out_specs=(qb, whole, whole, sk),
        out_shape=(jax.ShapeDtypeStruct((HQ, S, dh), F32), jax.ShapeDtypeStruct((HKV, S, dh), F32),
                   jax.ShapeDtypeStruct((HKV, S, dh), F32), jax.ShapeDtypeStruct((HKV, group, LANES), F32)),
        compiler_params=_cparams(("arbitrary", "arbitrary")),
    )(q, k, k, v, v, sinks, do)


def _gate_specs(S, D, ga_off, gb_off):
    tg = LANES
    for t in range(LANES, 513, LANES):
        if D % t == 0 and ga_off % t == 0 and gb_off % t == 0:
            tg = t
    if D % LANES:
        tg = math.gcd(math.gcd(D, ga_off), gb_off)
    tm = _tile(S, 512, 8)
    act = pl.BlockSpec((tm, tg), lambda i, j: (i, j))
    ga = pl.BlockSpec((tm, tg), lambda i, j: (i, ga_off // tg + j))
    gb = pl.BlockSpec((tm, tg), lambda i, j: (i, gb_off // tg + j))
    return (S // tm, D // tg), act, ga, gb


def _gate_fwd(name, proj, ya, yb, ga_off, gb_off):
    S, D = ya.shape
    grid, act, ga, gb = _gate_specs(S, D, ga_off, gb_off)

    def body(ga_ref, gb_ref, ya_ref, yb_ref, o_ref):
        o_ref[...] = (_sigmoid(ga_ref[...]) * ya_ref[...] + _sigmoid(gb_ref[...]) * yb_ref[...]).astype(BF16)

    return pl.pallas_call(
        body, name=name, grid=grid, in_specs=[ga, gb, act, act], out_specs=act,
        out_shape=jax.ShapeDtypeStruct((S, D), BF16), compiler_params=_cparams(("parallel", "parallel")),
    )(proj, proj, ya, yb)


def _gate_bwd(name, proj, ya, yb, dm, ga_off, gb_off):
    S, D = ya.shape
    grid, act, ga, gb = _gate_specs(S, D, ga_off, gb_off)

    def body(ga_ref, gb_ref, ya_ref, yb_ref, dm_ref, dga_ref, dgb_ref, dya_ref, dyb_ref):
        dmv = dm_ref[...]
        sa, sb = _sigmoid(ga_ref[...]), _sigmoid(gb_ref[...])
        dga_ref[...] = (dmv * ya_ref[...] * sa * (1.0 - sa)).astype(BF16)
        dgb_ref[...] = (dmv * yb_ref[...] * sb * (1.0 - sb)).astype(BF16)
        dya_ref[...] = (dmv * sa).astype(BF16)
        dyb_ref[...] = (dmv * sb).astype(BF16)

    o = jax.ShapeDtypeStruct((S, D), BF16)
    return pl.pallas_call(
        body, name=name, grid=grid, in_specs=[ga, gb, act, act, act], out_specs=(act, act, act, act),
        out_shape=(o, o, o, o), compiler_params=_cparams(("parallel", "parallel")),
    )(proj, proj, ya, yb, dm)


def _row_tile(rows, cols, n_arrays):
    want = max(8, (VMEM_LIMIT_V7X // 2) // (2 * n_arrays * cols * 4))
    return _tile(rows, want, 8)


def _cast_to_slot(name, w, dtype, p_arr, dep=None):
    R, C = w.shape
    tr = _row_tile(R, C, 2)
    extra = () if dep is None else (dep,)

    def body(p_ref, w_ref, *rest):
        rest[-1][...] = w_ref[...].astype(dtype)

    return pl.pallas_call(
        body, name=name,
        grid_spec=pltpu.PrefetchScalarGridSpec(
            num_scalar_prefetch=1, grid=(R // tr,),
            in_specs=[pl.BlockSpec((tr, C), lambda i, p_ref: (i, 0))] + [pl.BlockSpec(d.shape, lambda i, p_ref: (0, 0)) for d in extra],
            out_specs=pl.BlockSpec((None, tr, C), lambda i, p_ref: (p_ref[0], i, 0))),
        out_shape=jax.ShapeDtypeStruct((N_CHIPS, R, C), dtype), compiler_params=_cparams(("parallel",)),
    )(p_arr, w, *extra)


def _add_half(name, g3, r3, c_arr):
    n, h, C = r3.shape
    tr = _row_tile(h, C, 3)
    nb = h // tr

    def body(c_ref, g_ref, r_ref, o_ref):
        o_ref[...] = (g_ref[...] + r_ref[...]).astype(BF16)

    blk = pl.BlockSpec((None, tr, C), lambda s, i, c_ref: (s, i, 0))
    return pl.pallas_call(
        body, name=name,
        grid_spec=pltpu.PrefetchScalarGridSpec(
            num_scalar_prefetch=1, grid=(n, nb),
            in_specs=[pl.BlockSpec((None, tr, C), lambda s, i, c_ref: (s, c_ref[0] * nb + i, 0)), blk], out_specs=blk),
        out_shape=jax.ShapeDtypeStruct(r3.shape, BF16), compiler_params=_cparams(("parallel", "parallel")),
    )(c_arr, g3, r3)


def _add_chips(name, t3, r3, cp_arr):
    n, h, C = r3.shape
    tr = _row_tile(h, C, 6)
    nb = h // tr

    def body(cp_ref, t_ref, r0_ref, r1_ref, r2_ref, r3_ref, o_ref):
        p = cp_ref[1]
        total = None
        for a, r_ref in enumerate((r0_ref, r1_ref, r2_ref, r3_ref)):
            part = jnp.where(p == a, t_ref[...], r_ref[...]).astype(F32)
            total = part if total is None else total + part
        o_ref[...] = total

    def part(a):
        return pl.BlockSpec((None, tr, C), lambda i, cp_ref: (jnp.where(cp_ref[1] == a, (a + 1) % N_CHIPS, a), i, 0))

    return pl.pallas_call(
        body, name=name,
        grid_spec=pltpu.PrefetchScalarGridSpec(
            num_scalar_prefetch=1, grid=(nb,),
            in_specs=[pl.BlockSpec((None, tr, C), lambda i, cp_ref: (cp_ref[1], i, 0)), part(0), part(1), part(2), part(3)],
            out_specs=pl.BlockSpec((tr, C), lambda i, cp_ref: (cp_ref[0] * nb + i, 0))),
        out_shape=jax.ShapeDtypeStruct((2 * h, C), F32), compiler_params=_cparams(("parallel",)),
    )(cp_arr, t3, r3, r3, r3, r3)


def _adamw(name, w, g, m, v):
    R, C = w.shape
    tr = _row_tile(R, C, 7)
    c1 = 1.0 - ADAM_B1 ** ADAM_STEP
    c2 = 1.0 - ADAM_B2 ** ADAM_STEP

    def body(w_ref, g_ref, m_ref, v_ref, d_ref, nm_ref, nv_ref):
        gv = g_ref[...]
        nm = ADAM_B1 * m_ref[...] + (1.0 - ADAM_B1) * gv
        nv = ADAM_B2 * v_ref[...] + (1.0 - ADAM_B2) * (gv * gv)
        d_ref[...] = -ADAM_LR * ((nm / c1) / (jnp.sqrt(nv / c2) + ADAM_EPS) + ADAM_WD * w_ref[...])
        nm_ref[...] = nm
        nv_ref[...] = nv

    blk = pl.BlockSpec((tr, C), lambda i: (i, 0))
    o = jax.ShapeDtypeStruct((R, C), F32)
    return pl.pallas_call(
        body, name=name, grid=(R // tr,), in_specs=[blk, blk, blk, blk], out_specs=(blk, blk, blk),
        out_shape=(o, o, o), compiler_params=_cparams(("parallel",)),
    )(w, g, m, v)


def _place():
    x, y, c = lax.axis_index("x"), lax.axis_index("y"), lax.axis_index("c")
    chips = [(1 - x, y), (x, 1 - y), (1 - x, 1 - y)]
    return x, y, c, 2 * x + y, chips


ANY = pl.BlockSpec(memory_space=pl.ANY)


HBM = pl.BlockSpec(memory_space=pltpu.HBM)
SEM = pl.BlockSpec(memory_space=pltpu.SEMAPHORE)
TOKEN = jax.ShapeDtypeStruct((8, LANES), F32)
DATAFLOW = pltpu.SideEffectType.DATAFLOW_SIDE_EFFECTING


def _hbm(a):
    return pltpu.with_memory_space_constraint(a, pltpu.HBM)


def _gather_blocks(bufs, i, c, p, chips):
    if bufs[i].shape[1] % 16:
        return bufs[i].at[p], [bufs[i].at[2 * cx + cy] for cx, cy in chips]
    h = bufs[i].shape[1] // 2
    rows = pl.ds(pl.multiple_of(c * h, 16), h)
    return bufs[i].at[p, rows], [bufs[i].at[2 * cx + cy, rows] for cx, cy in chips]


def _gather_start(name, slots, dep):
    n = len(slots)

    def body(*refs):
        bufs, send, recv, token = refs[:n], refs[n + 1], refs[n + 2], refs[-1]
        x, y, c, p, chips = _place()
        for i in range(n):
            mine, _ = _gather_blocks(bufs, i, c, p, chips)
            for j, chip in enumerate(chips):
                pltpu.make_async_remote_copy(src_ref=mine, dst_ref=mine, send_sem=send.at[3 * i + j], recv_sem=recv.at[3 * i + j],
                                             device_id=(*chip, c), device_id_type=MESH).start()
        token[...] = jnp.zeros_like(token)

    out = pl.pallas_call(
        body, name=name, in_specs=[HBM] * n + [ANY],
        out_specs=(SEM, SEM, *([HBM] * n), pl.BlockSpec(memory_space=pltpu.VMEM)),
        out_shape=(pltpu.SemaphoreType.DMA((3 * n,)), pltpu.SemaphoreType.DMA((3 * n,)),
                   *[pltpu.HBM(s.shape, s.dtype) for s in slots], TOKEN),
        input_output_aliases={i: 2 + i for i in range(n)},
        compiler_params=pltpu.CompilerParams(has_side_effects=DATAFLOW),
    )(*[_hbm(s) for s in slots], dep)
    return out[0], out[1], list(out[2:2 + n]), out[-1]


def _gather_wait(name, send, recv, slots, after):
    n = len(slots)

    def body(*refs):
        bufs, send, recv = refs[:n], refs[n], refs[n + 1]
        x, y, c, p, chips = _place()
        for i in range(n):
            mine, landed = _gather_blocks(bufs, i, c, p, chips)
            for j, chip in enumerate(chips):
                cp = pltpu.make_async_remote_copy(src_ref=mine, dst_ref=landed[j], send_sem=send.at[3 * i + j],
                                                  recv_sem=recv.at[3 * i + j], device_id=(*chip, c), device_id_type=MESH)
                cp.wait_send()
                cp.wait_recv()

    return list(pl.pallas_call(
        body, name=name, in_specs=[HBM] * n + [SEM, SEM, ANY], out_specs=tuple([HBM] * n),
        out_shape=tuple(pltpu.HBM(s.shape, s.dtype) for s in slots),
        input_output_aliases={i: i for i in range(n)},
        compiler_params=pltpu.CompilerParams(has_side_effects=DATAFLOW),
    )(*slots, send, recv, after))


def _gather_forward(name, slots):
    idx = [i for i, s in enumerate(slots) if s.shape[1] % 16 == 0]
    n = len(slots)

    def body(*refs):
        bufs = refs[n:2 * n]
        send, recv = refs[2 * n:]
        x, y, c, p, chips = _place()

        def rdma(k, ref):
            return pltpu.make_async_remote_copy(src_ref=ref, dst_ref=ref, send_sem=send.at[k], recv_sem=recv.at[k],
                                                device_id=(x, y, 1 - c), device_id_type=MESH)

        cps = []
        for k, i in enumerate(idx):
            for j, ref in enumerate(_gather_blocks(bufs, i, c, p, chips)[1]):
                cps.append(rdma(3 * k + j, ref))
                cps[-1].start()
        for k, i in enumerate(idx):
            for j, ref in enumerate(_gather_blocks(bufs, i, 1 - c, p, chips)[1]):
                rdma(3 * k + j, ref).wait_recv()
        for cp in cps:
            cp.wait_send()

    return list(pl.pallas_call(
        body, name=name, in_specs=[ANY] * n, out_specs=tuple([ANY] * n),
        out_shape=tuple(jax.ShapeDtypeStruct(s.shape, s.dtype) for s in slots),
        scratch_shapes=[pltpu.SemaphoreType.DMA((3 * len(idx),)), pltpu.SemaphoreType.DMA((3 * len(idx),))],
        input_output_aliases={i: i for i in range(n)},
        compiler_params=pltpu.CompilerParams(has_side_effects=True),
    )(*slots))


def _swap_halves(name, grads):
    n = len(grads)

    def body(*refs):
        ins, outs = refs[:n], refs[n:2 * n]
        send, recv = refs[2 * n:]
        x, y, c, p, chips = _place()
        cps = []
        for i in range(n):
            h = ins[i].shape[1] // 2
            other = pl.ds(pl.multiple_of((1 - c) * h, 8), h)
            cp = pltpu.make_async_remote_copy(src_ref=ins[i].at[:, other, :], dst_ref=outs[i], send_sem=send.at[i],
                                              recv_sem=recv.at[i], device_id=(x, y, 1 - c), device_id_type=MESH)
            cp.start()
            cps.append(cp)
        for cp in cps:
            cp.wait()

    return pl.pallas_call(
        body, name=name, in_specs=[ANY] * n, out_specs=tuple([ANY] * n),
        out_shape=tuple(jax.ShapeDtypeStruct((g.shape[0], g.shape[1] // 2, g.shape[2]), g.dtype) for g in grads),
        scratch_shapes=[pltpu.SemaphoreType.DMA((n,)), pltpu.SemaphoreType.DMA((n,))],
        compiler_params=pltpu.CompilerParams(has_side_effects=True),
    )(*grads)


def _exchange_start(name, parts):
    n = len(parts)

    def body(*refs):
        ins, lands, send, recv, token = refs[:n], refs[n:2 * n], refs[2 * n], refs[2 * n + 1], refs[-1]
        x, y, c, p, chips = _place()
        for i in range(n):
            for j, (cx, cy) in enumerate(chips):
                pltpu.make_async_remote_copy(src_ref=ins[i].at[2 * cx + cy], dst_ref=lands[i].at[p], send_sem=send.at[3 * i + j],
                                             recv_sem=recv.at[3 * i + j], device_id=(cx, cy, c), device_id_type=MESH).start()
        token[...] = jnp.zeros_like(token)

    shapes = [pltpu.HBM(t.shape, t.dtype) for t in parts]
    out = pl.pallas_call(
        body, name=name, in_specs=[HBM] * (2 * n),
        out_specs=(SEM, SEM, *([HBM] * (2 * n)), pl.BlockSpec(memory_space=pltpu.VMEM)),
        out_shape=(pltpu.SemaphoreType.DMA((3 * n,)), pltpu.SemaphoreType.DMA((3 * n,)), *shapes, *shapes, TOKEN),
        input_output_aliases={i: 2 + i for i in range(2 * n)},
        compiler_params=pltpu.CompilerParams(has_side_effects=DATAFLOW),
    )(*[_hbm(t) for t in parts], *[_hbm(lax.empty(t.shape, t.dtype)) for t in parts])
    return out[0], out[1], list(out[2:2 + n]), list(out[2 + n:2 + 2 * n]), out[-1]


def _exchange_wait(name, send, recv, parts, lands, after):
    n = len(parts)

    def body(*refs):
        ins, lands, send, recv = refs[:n], refs[n:2 * n], refs[2 * n], refs[2 * n + 1]
        x, y, c, p, chips = _place()
        for i in range(n):
            for j, (cx, cy) in enumerate(chips):
                q = 2 * cx + cy
                cp = pltpu.make_async_remote_copy(src_ref=ins[i].at[q], dst_ref=lands[i].at[q], send_sem=send.at[3 * i + j],
                                                  recv_sem=recv.at[3 * i + j], device_id=(cx, cy, c), device_id_type=MESH)
                cp.wait_send()
                cp.wait_recv()

    shapes = [pltpu.HBM(t.shape, t.dtype) for t in parts]
    out = pl.pallas_call(
        body, name=name, in_specs=[HBM] * (2 * n) + [SEM, SEM, ANY], out_specs=tuple([HBM] * (2 * n)),
        out_shape=(*shapes, *shapes), input_output_aliases={i: i for i in range(2 * n)},
        compiler_params=pltpu.CompilerParams(has_side_effects=DATAFLOW),
    )(*parts, *lands, send, recv, after)
    return list(out[:n]), list(out[n:])


def _join_halves(name, bufs):
    n = len(bufs)

    def body(*refs):
        outs = refs[n:2 * n]
        send, recv = refs[2 * n:]
        x, y, c, p, chips = _place()

        def rdma(i, which):
            h = outs[i].shape[0] // 2
            rows = outs[i].at[pl.ds(pl.multiple_of(which * h, 8), h)]
            return pltpu.make_async_remote_copy(src_ref=rows, dst_ref=rows, send_sem=send.at[i], recv_sem=recv.at[i],
                                                device_id=(x, y, 1 - c), device_id_type=MESH)

        cps = [rdma(i, c) for i in range(n)]
        for cp in cps:
            cp.start()
        for i, cp in enumerate(cps):
            rdma(i, 1 - c).wait_recv()
            cp.wait_send()

    return pl.pallas_call(
        body, name=name, in_specs=[ANY] * n, out_specs=tuple([ANY] * n),
        out_shape=tuple(jax.ShapeDtypeStruct(t.shape, t.dtype) for t in bufs),
        scratch_shapes=[pltpu.SemaphoreType.DMA((n,)), pltpu.SemaphoreType.DMA((n,))],
        input_output_aliases={i: i for i in range(n)},
        compiler_params=pltpu.CompilerParams(has_side_effects=True),
    )(*bufs)


def _allreduce_small(name, pack):
    R, W = pack.shape

    def body(in_ref, out_ref, slots, send, recv):
        x, y, c = lax.axis_index("x"), lax.axis_index("y"), lax.axis_index("c")
        me = 4 * x + 2 * y + c
        slots[0] = in_ref[...]
        cps = []
        for k in range(1, N_DEV):
            peer = (x ^ (k >> 2), y ^ ((k >> 1) & 1), c ^ (k & 1))
            cp = pltpu.make_async_remote_copy(src_ref=in_ref, dst_ref=slots.at[k], send_sem=send.at[k - 1],
                                              recv_sem=recv.at[k - 1], device_id=peer, device_id_type=MESH)
            cp.start()
            cps.append(cp)
        for cp in cps:
            cp.wait()
        total = slots[me]
        for a in range(1, N_DEV):
            total = total + slots[jnp.bitwise_xor(a, me)]
        out_ref[...] = total

    vmem = pl.BlockSpec(memory_space=pltpu.VMEM)
    return pl.pallas_call(
        body, name=name, in_specs=[vmem], out_specs=vmem, out_shape=jax.ShapeDtypeStruct((R, W), F32),
        scratch_shapes=[pltpu.VMEM((N_DEV, R, W), F32), pltpu.SemaphoreType.DMA((N_DEV - 1,)), pltpu.SemaphoreType.DMA((N_DEV - 1,))],
        compiler_params=pltpu.CompilerParams(has_side_effects=True),
    )(pack)


def _heads(a, n_heads):
    S = a.shape[0]
    return a.reshape(S, n_heads, a.shape[1] // n_heads).transpose(1, 0, 2)


def _unheads(a):
    H, S, dh = a.shape
    return a.transpose(1, 0, 2).reshape(S, H * dh)


def _ffn_bwd(tag, xin, gain, wgu3, wd, saved, dxout, reduce_start, dep):
    h, gu, act = saved
    D = xin.shape[1]
    dxo_b = dxout.astype(BF16)
    dgu = _ffn_down_bwd(f"ffn_down_bwd_{tag}", dxo_b, wd, gu, 0.5, dep=dep)
    tok = reduce_start({f"w_gu{tag}": _mm_tn_cols(f"dw_gu_{tag}", h, dgu, wgu3.shape[2], b_is_gu=True)})
    tok = reduce_start({f"w_down{tag}": _mm_tn(f"dw_down_{tag}", act, dxo_b, 0.5, dep=tok).reshape(N_CHIPS, -1, D)})
    dh = _mm_nt_cols(f"ffn_up_bwd_{tag}", dgu, wgu3, a_is_gu=True, dep=tok)
    dxin, dgain = _rms_bwd(f"rms_bwd_{tag}", xin, gain, dh, dxout)
    return dxin, dgain, tok


def kernel(x, g_ffn1, w_gu1, w_down1, g_mix, w_in, conv_w, q_norm_g, k_norm_g, sinks, w_out_conv, w_out_attn, w_o, g_ffn2, w_gu2, w_down2, loss_target, m_g_ffn1, m_w_gu1, m_w_down1, m_g_mix, m_w_in, m_conv_w, m_q_norm_g, m_k_norm_g, m_sinks, m_w_out_conv, m_w_out_attn, m_w_o, m_g_ffn2, m_w_gu2, m_w_down2, v_g_ffn1, v_w_gu1, v_w_down1, v_g_mix, v_w_in, v_conv_w, v_q_norm_g, v_k_norm_g, v_sinks, v_w_out_conv, v_w_out_attn, v_w_o, v_g_ffn2, v_w_gu2, v_w_down2):
    S, D = x.shape[1], x.shape[2]
    dh = q_norm_g.shape[1]
    HQ = sinks.shape[1]
    HKV = HQ // 4
    AW, KVW, CW = HQ * dh, HKV * dh, D // 2
    off_q, off_k, off_v = 3 * CW, 3 * CW + AW, 3 * CW + AW + KVW
    off_ga, off_gb = off_v + KVW, off_v + KVW + D
    x0, target = x[0], loss_target[0]
    cx, cy, cc = lax.axis_index("x"), lax.axis_index("y"), lax.axis_index("c")
    chip = 2 * cx + cy
    p_arr = jnp.reshape(chip, (1,)).astype(jnp.int32)
    c_arr = jnp.reshape(cc, (1,)).astype(jnp.int32)
    cp_arr = jnp.stack([cc, chip]).astype(jnp.int32)
    wts = dict(g_ffn1=g_ffn1, w_gu1=w_gu1, w_down1=w_down1, g_mix=g_mix, w_in=w_in, conv_w=conv_w, q_norm_g=q_norm_g,
               k_norm_g=k_norm_g, sinks=sinks, w_out_conv=w_out_conv, w_out_attn=w_out_attn, w_o=w_o, g_ffn2=g_ffn2,
               w_gu2=w_gu2, w_down2=w_down2)
    ms = dict(g_ffn1=m_g_ffn1, w_gu1=m_w_gu1, w_down1=m_w_down1, g_mix=m_g_mix, w_in=m_w_in, conv_w=m_conv_w,
              q_norm_g=m_q_norm_g, k_norm_g=m_k_norm_g, sinks=m_sinks, w_out_conv=m_w_out_conv, w_out_attn=m_w_out_attn,
              w_o=m_w_o, g_ffn2=m_g_ffn2, w_gu2=m_w_gu2, w_down2=m_w_down2)
    vs = dict(g_ffn1=v_g_ffn1, w_gu1=v_w_gu1, w_down1=v_w_down1, g_mix=v_g_mix, w_in=v_w_in, conv_w=v_conv_w,
              q_norm_g=v_q_norm_g, k_norm_g=v_k_norm_g, sinks=v_sinks, w_out_conv=v_w_out_conv, w_out_attn=v_w_out_attn,
              w_o=v_w_o, g_ffn2=v_g_ffn2, w_gu2=v_w_gu2, w_down2=v_w_down2)
    order = list(wts)
    small_names = [k for k in order if not k.startswith("w_")]
    grad, delta, new_m, new_v = {}, {}, {}, {}

    def cast(keys, dep=None):
        return [_cast_to_slot(f"cast_{k}", wts[k][0], F32 if k == "conv_w" else BF16, p_arr, dep) for k in keys]

    def gather_start(tag, slots, dep):
        send, recv, slots, tok = _gather_start(f"gather_start_{tag}", slots, dep)
        return (tag, send, recv, slots), tok

    def gather_finish(started, after):
        tag, send, recv, slots = started
        return _gather_forward(f"gather_forward_{tag}", _gather_wait(f"gather_wait_{tag}", send, recv, slots, after))

    pending = []

    def reduce_start(full):
        keys = list(full)
        gs = [full[k] for k in keys]
        sib = _swap_halves(f"swap_{keys[0]}", gs)
        parts = [_add_half(f"add_half_{k}", g, r, c_arr) for k, g, r in zip(keys, gs, sib)]
        send, recv, parts, lands, tok = _exchange_start(f"exchange_start_{keys[0]}", parts)
        pending.append((keys, send, recv, parts, lands))
        return tok

    def reduce_finish(entries, after):
        keys_all, halves = [], []
        for keys, send, recv, parts, lands in entries:
            parts, lands = _exchange_wait(f"exchange_wait_{keys[0]}", send, recv, parts, lands, after)
            halves += [_add_chips(f"add_chips_{k}", t, r, cp_arr) for k, t, r in zip(keys, parts, lands)]
            keys_all += keys
        for k, g2 in zip(keys_all, _join_halves(f"join_{keys_all[0]}", halves)):
            d, nm, nv = _adamw(f"adamw_{k}", wts[k][0], g2, ms[k][0], vs[k][0])
            grad[k], delta[k], new_m[k], new_v[k] = g2[None], d[None], nm[None], nv[None]

    keys_mix, keys_2 = ["w_in", "w_out_conv", "w_out_attn", "w_o", "conv_w"], ["w_gu2", "w_down2"]
    st_gu1, tok = gather_start("gu1", cast(["w_gu1"]), x0)
    st_d1, tok = gather_start("d1", cast(["w_down1"]), tok)
    slots_mix, slots_2 = cast(keys_mix, tok), cast(keys_2, tok)
    wgu1, = gather_finish(st_gu1, slots_2[-1])
    st_mix, tok = gather_start("mix", slots_mix, wgu1)
    st_2, tok = gather_start("2", slots_2, tok)
    cos, sin, rm, rmt = _rope_consts(S, dh)
    sink_vec = sinks[0]

    h1 = _rms_fwd("rms_fwd_1", x0, g_ffn1, tok)
    gu1, act1 = _ffn_up("ffn_up_1", h1, wgu1)
    wd1 = gather_finish(st_d1, act1)[0].reshape(-1, D)
    x1 = _mm_res("ffn_down_1", act1, wd1, x0, 0.5)
    win3, woc3, woa3, wo, convw3 = gather_finish(st_mix, x1)
    wo = wo.reshape(-1, D)
    h2 = _rms_fwd("rms_fwd_mix", x1, g_mix)
    proj = _mm_cols("in_proj", h2, win3, F32)
    aconv = _conv_fwd("conv_fwd", proj, convw3, CW)
    ya = _mm_cols("out_conv", aconv, woc3, F32)
    q_raw = _heads(proj[:, off_q:off_q + AW], HQ)
    k_raw = _heads(proj[:, off_k:off_k + KVW], HKV)
    vh = _heads(proj[:, off_v:off_v + KVW], HKV).astype(BF16)
    qn = _qk_prep("q_prep", q_raw, q_norm_g, cos, sin, rm)
    kn = _qk_prep("k_prep", k_raw, k_norm_g, cos, sin, rm)
    oh = _attn_fwd("attn_fwd", qn, kn, vh, sink_vec)
    o = _unheads(oh)
    yb = _mm_cols("out_attn", o, woa3, F32)
    merged = _gate_fwd("gate_fwd", proj, ya, yb, off_ga, off_gb)
    x2 = _mm_res("mix_out", merged, wo, x1, 1.0)
    wgu2, wd2 = gather_finish(st_2, x2)
    wd2 = wd2.reshape(-1, D)
    h3 = _rms_fwd("rms_fwd_2", x2, g_ffn2)
    gu2, act2 = _ffn_up("ffn_up_2", h3, wgu2)
    x3 = _mm_res("ffn_down_2", act2, wd2, x2, 0.5)

    dy, loss_lanes = _loss_grad("loss_grad", x3, target)
    dx2, dg_ffn2, tok = _ffn_bwd("2", x2, g_ffn2, wgu2, wd2, (h3, gu2, act2), dy, reduce_start, None)
    dx2_b = dx2.astype(BF16)
    dmerged = _mm_nt("mix_out_bwd", dx2_b, wo, F32)
    tok = reduce_start(dict(w_o=_mm_tn("dw_o", merged, dx2_b, dep=tok).reshape(N_CHIPS, -1, D)))
    dga, dgb, dya, dyb = _gate_bwd("gate_bwd", proj, ya, yb, dmerged, off_ga, off_gb)
    daconv = _mm_nt_cols("out_conv_bwd", dya, woc3, dep=tok)
    dwoc = _mm_tn_cols("dw_out_conv", aconv, dya, woc3.shape[2])
    do = _mm_nt_cols("out_attn_bwd", dyb, woa3)
    dwoa = _mm_tn_cols("dw_out_attn", o, dyb, woa3.shape[2])
    tok = reduce_start(dict(w_out_conv=dwoc, w_out_attn=dwoa))
    dxc, dbg, dcg, dconvw = _conv_bwd("conv_bwd", proj, convw3, daconv, CW)
    dqn, dkn, dvh, dsink3 = _attn_bwd("attn_bwd", qn, kn, vh, sink_vec, _heads(do, HQ).astype(BF16))
    dq_raw, dqg = _qk_prep_bwd("q_prep_bwd", q_raw, q_norm_g, cos, sin, rmt, dqn)
    dk_raw, dkg = _qk_prep_bwd("k_prep_bwd", k_raw, k_norm_g, cos, sin, rmt, dkn)
    dproj = jnp.concatenate([dxc, dbg, dcg, _unheads(dq_raw), _unheads(dk_raw), _unheads(dvh).astype(BF16), dga, dgb], axis=1)
    dh2 = _mm_nt_cols("in_proj_bwd", dproj, win3, dep=tok)
    tok = reduce_start(dict(w_in=_mm_tn_cols("dw_in", h2, dproj, win3.shape[2])))
    dx1, dg_mix = _rms_bwd("rms_bwd_mix", x1, g_mix, dh2, dx2)
    dx0, dg_ffn1, tok = _ffn_bwd("1", x0, g_ffn1, wgu1, wd1, (h1, gu1, act1), dx1, reduce_start, tok)

    def rows8(a):
        a = a.reshape(-1, a.shape[-1])
        return jnp.pad(a, ((0, -a.shape[0] % 8), (0, D - a.shape[1])))

    misc = jnp.concatenate([dqg, dkg, dsink3[:, :, 0].reshape(1, HQ), loss_lanes], axis=1)
    tot = _allreduce_small("allreduce_small", jnp.concatenate([rows8(a) for a in (dg_ffn1, dg_mix, dg_ffn2, dconvw, misc)], axis=0))

    reduce_finish(pending[:-2], dx0)
    reduce_finish(pending[-2:], dx0)

    cw_s = conv_w.shape[2]
    conv_row0, misc_row = 24, 24 + (-(-N_CHIPS * CONV_K // 8)) * 8
    small_g = dict(g_ffn1=tot[0:1], g_mix=tot[8:9], g_ffn2=tot[16:17],
                   conv_w=lax.dynamic_slice(tot, (conv_row0 + CONV_K * chip, 0), (CONV_K, cw_s)),
                   q_norm_g=tot[misc_row:misc_row + 1, 0:dh], k_norm_g=tot[misc_row:misc_row + 1, dh:2 * dh],
                   sinks=tot[misc_row:misc_row + 1, 2 * dh:2 * dh + HQ])
    loss = (0.5 / D) * jnp.sum(tot[misc_row, 2 * dh + HQ:2 * dh + HQ + LANES])

    def small_pack(src):
        return jnp.concatenate([rows8(src[k]) for k in small_names], axis=0)

    sd, sm, sv = _adamw("adamw_small", small_pack(wts), small_pack(small_g), small_pack(ms), small_pack(vs))
    for i, k in enumerate(small_names):
        shape = wts[k].shape
        nr, ncol = math.prod(shape[:-1]), shape[-1]
        grad[k] = small_g[k].reshape(shape)
        delta[k], new_m[k], new_v[k] = (a[8 * i:8 * i + nr, 0:ncol].reshape(shape) for a in (sd, sm, sv))
    return (loss, dx0[None], *[grad[k] for k in order], *[delta[k] for k in order],
            *[new_m[k] for k in order], *[new_v[k] for k in order])
```

```python
import math

import numpy as np
import jax
import jax.numpy as jnp
from jax import lax
from jax.experimental import pallas as pl
from jax.experimental.pallas import tpu as pltpu

F32 = jnp.float32
BF16 = jnp.bfloat16
MESH = pl.DeviceIdType.MESH

RMS_EPS = 1e-6
BLOCK = 128
ROPE_THETA = 500000.0
NEG_INF = -1e30
CONV_K = 3
ADAM_LR, ADAM_B1, ADAM_B2, ADAM_EPS, ADAM_WD, ADAM_STEP = 0.001, 0.9, 0.999, 1e-08, 0.01, 10

VMEM_LIMIT_V7X = 56 * 1024 * 1024
LANES = 128
N_CHIPS = 4
N_DEV = 8


def _tile(n, want, align=LANES):
    best = None
    t = align
    while t <= min(n, want):
        if n % t == 0:
            best = t
        t += align
    return best or n


def _cparams(sem):
    return pltpu.CompilerParams(dimension_semantics=sem, vmem_limit_bytes=VMEM_LIMIT_V7X)


def _sigmoid(x):
    return 1.0 / (1.0 + jnp.exp(-x))


NN = (((1,), (0,)), ((), ()))
NT = (((1,), (1,)), ((), ()))
TN = (((0,), (0,)), ((), ()))


ONE_BUFFER = pl.Buffered(1)


def _mm(name, grid, ins, in_specs, compute, out_shape, out_specs, epilogue, dep=None):
    if dep is not None:
        ins, in_specs = tuple(ins) + (dep,), list(in_specs) + [pl.BlockSpec(dep.shape, lambda *_: (0, 0))]
    n_in = len(ins)

    def body(*refs):
        epilogue(compute(refs[:n_in]), refs[:n_in], refs[n_in:])

    return pl.pallas_call(
        body, name=name, grid=grid, in_specs=in_specs, out_specs=out_specs, out_shape=out_shape,
        compiler_params=_cparams(("parallel", "arbitrary")),
    )(*ins)


def _dot(dims, a=0, b=1):
    return lambda refs: [lax.dot_general(refs[a][...], refs[b][...], dims, preferred_element_type=F32)]


def _ffn_up(name, h, wgu3):
    S, D = h.shape
    Ns = wgu3.shape[2]
    F = 2 * Ns
    tm, tn = _tile(S, 512), _tile(Ns, 1408)
    nbs = Ns // tn

    def compute(refs):
        hv = refs[0][...]
        return [jnp.dot(hv, refs[1][...], preferred_element_type=F32), jnp.dot(hv, refs[2][...], preferred_element_type=F32)]

    def epi(accs, in_refs, out_refs):
        g, u = accs
        gu_ref, a_ref = out_refs
        gu_ref[0] = g.astype(BF16)
        gu_ref[1] = u.astype(BF16)
        a_ref[...] = (g * _sigmoid(g) * u).astype(BF16)

    return _mm(
        name, (F // tn, S // tm), (h, wgu3, wgu3),
        [pl.BlockSpec((tm, D), lambda j, i: (i, 0)),
         pl.BlockSpec((None, D, tn), lambda j, i: (j // nbs, 0, j % nbs), pipeline_mode=ONE_BUFFER),
         pl.BlockSpec((None, D, tn), lambda j, i: (2 + j // nbs, 0, j % nbs), pipeline_mode=ONE_BUFFER)],
        compute, (jax.ShapeDtypeStruct((2, S, F), BF16), jax.ShapeDtypeStruct((S, F), BF16)),
        (pl.BlockSpec((2, tm, tn), lambda j, i: (0, i, j)), pl.BlockSpec((tm, tn), lambda j, i: (i, j))), epi)


def _mm_res(name, a, w, res, scale):
    S, K = a.shape
    N = w.shape[1]
    tm, tn = _tile(S, 512), _tile(N, 512 if K > 2816 else 1024)

    def epi(accs, in_refs, out_refs):
        out_refs[0][...] = in_refs[2][...] + scale * accs[0]

    return _mm(
        name, (N // tn, S // tm), (a, w, res),
        [pl.BlockSpec((tm, K), lambda j, i: (i, 0)), pl.BlockSpec((K, tn), lambda j, i: (0, j), pipeline_mode=ONE_BUFFER),
         pl.BlockSpec((tm, tn), lambda j, i: (i, j))],
        _dot(NN), jax.ShapeDtypeStruct((S, N), F32), pl.BlockSpec((tm, tn), lambda j, i: (i, j)), epi)


def _mm_cols(name, a, w3, out_dtype):
    S, K = a.shape
    Ns = w3.shape[2]
    tm, tn = _tile(S, 512), _tile(Ns, 2304)
    nbs = Ns // tn

    def epi(accs, in_refs, out_refs):
        out_refs[0][...] = accs[0].astype(out_dtype)

    return _mm(
        name, (N_CHIPS * nbs, S // tm), (a, w3),
        [pl.BlockSpec((tm, K), lambda j, i: (i, 0)),
         pl.BlockSpec((None, K, tn), lambda j, i: (j // nbs, 0, j % nbs), pipeline_mode=ONE_BUFFER)],
        _dot(NN), jax.ShapeDtypeStruct((S, N_CHIPS * Ns), out_dtype), pl.BlockSpec((tm, tn), lambda j, i: (i, j)), epi)


def _mm_nt(name, a, w, out_dtype, scale=1.0):
    S, N = a.shape
    K = w.shape[0]
    tm, tn = _tile(S, 512), _tile(K, 1024)

    def epi(accs, in_refs, out_refs):
        out_refs[0][...] = (scale * accs[0]).astype(out_dtype)

    return _mm(
        name, (K // tn, S // tm), (a, w),
        [pl.BlockSpec((tm, N), lambda j, i: (i, 0)), pl.BlockSpec((tn, N), lambda j, i: (j, 0), pipeline_mode=ONE_BUFFER)],
        _dot(NT), jax.ShapeDtypeStruct((S, K), out_dtype), pl.BlockSpec((tm, tn), lambda j, i: (i, j)), epi)


def _ffn_down_bwd(name, dy, wd, gu, scale, dep=None):
    S, D = dy.shape
    F = wd.shape[0]
    tm, tn = _tile(S, 512), _tile(F, 1408)

    def epi(accs, in_refs, out_refs):
        da = scale * accs[0]
        g = in_refs[2][0].astype(F32)
        u = in_refs[2][1].astype(F32)
        sg = _sigmoid(g)
        out_refs[0][0] = (da * u * (sg * (1.0 + g * (1.0 - sg)))).astype(BF16)
        out_refs[0][1] = (da * (g * sg)).astype(BF16)

    return _mm(
        name, (F // tn, S // tm), (dy, wd, gu),
        [pl.BlockSpec((tm, D), lambda j, i: (i, 0)), pl.BlockSpec((tn, D), lambda j, i: (j, 0), pipeline_mode=ONE_BUFFER),
         pl.BlockSpec((2, tm, tn), lambda j, i: (0, i, j))],
        _dot(NT), jax.ShapeDtypeStruct((2, S, F), BF16), pl.BlockSpec((2, tm, tn), lambda j, i: (0, i, j)), epi, dep=dep)


def _mm_nt_cols(name, a, w3, a_is_gu=False, dep=None):
    K, Ns = w3.shape[1], w3.shape[2]
    S = a.shape[1] if a_is_gu else a.shape[0]
    tm = _tile(S, 512)
    tn = _tile(K, max(LANES, (6 << 20) // (N_CHIPS * Ns * 2)))
    if a_is_gu:
        a_spec = pl.BlockSpec((2, tm, 2 * Ns), lambda i, j: (0, i, 0), pipeline_mode=ONE_BUFFER)
        part = lambda a_ref, s: a_ref[s // 2, :, (s % 2) * Ns:(s % 2 + 1) * Ns]
    else:
        a_spec = pl.BlockSpec((tm, N_CHIPS * Ns), lambda i, j: (i, 0), pipeline_mode=ONE_BUFFER)
        part = lambda a_ref, s: a_ref[:, s * Ns:(s + 1) * Ns]

    def compute(refs):
        total = None
        for s in range(N_CHIPS):
            prod = lax.dot_general(part(refs[0], s), refs[1][s], NT, preferred_element_type=F32)
            total = prod if total is None else total + prod
        return [total]

    def epi(accs, in_refs, out_refs):
        out_refs[0][...] = accs[0]

    return _mm(
        name, (S // tm, K // tn), (a, w3), [a_spec, pl.BlockSpec((N_CHIPS, tn, Ns), lambda i, j: (0, j, 0))],
        compute, jax.ShapeDtypeStruct((S, K), F32), pl.BlockSpec((tm, tn), lambda i, j: (i, j)), epi, dep=dep)


def _mm_tn(name, a, b, scale=1.0, dep=None):
    S, K = a.shape
    N = b.shape[1]
    tm, tn = _tile(K, 512), _tile(N, 1024)

    def epi(accs, in_refs, out_refs):
        out_refs[0][...] = scale * accs[0]

    return _mm(
        name, (N // tn, K // tm), (a, b),
        [pl.BlockSpec((S, tm), lambda j, i: (0, i)), pl.BlockSpec((S, tn), lambda j, i: (0, j), pipeline_mode=ONE_BUFFER)],
        _dot(TN), jax.ShapeDtypeStruct((K, N), F32), pl.BlockSpec((tm, tn), lambda j, i: (i, j)), epi, dep=dep)


def _mm_tn_cols(name, a, b, Ns, b_is_gu=False, dep=None):
    S, K = a.shape
    tm, tn = _tile(K, 512), _tile(Ns, 2304)
    nbs = Ns // tn
    if b_is_gu:
        b_spec = pl.BlockSpec((None, S, tn), lambda j, i: (j // (2 * nbs), 0, j % (2 * nbs)), pipeline_mode=ONE_BUFFER)
    else:
        b_spec = pl.BlockSpec((S, tn), lambda j, i: (0, j), pipeline_mode=ONE_BUFFER)

    def epi(accs, in_refs, out_refs):
        out_refs[0][...] = accs[0]

    return _mm(
        name, (N_CHIPS * nbs, K // tm), (a, b), [pl.BlockSpec((S, tm), lambda j, i: (0, i)), b_spec],
        _dot(TN), jax.ShapeDtypeStruct((N_CHIPS, K, Ns), F32),
        pl.BlockSpec((None, tm, tn), lambda j, i: (j // nbs, i, j % nbs)), epi, dep=dep)


def _rms_fwd(name, x, gain, dep=None):
    S, D = x.shape
    tm = _tile(S, 256, 8)
    extra = () if dep is None else (dep,)

    def body(x_ref, g_ref, *rest):
        h_ref = rest[-1]
        xv = x_ref[...]
        r = lax.rsqrt(jnp.mean(xv * xv, axis=-1, keepdims=True) + RMS_EPS)
        h_ref[...] = (xv * r * g_ref[...]).astype(BF16)

    return pl.pallas_call(
        body, name=name, grid=(S // tm,),
        in_specs=[pl.BlockSpec((tm, D), lambda i: (i, 0)), pl.BlockSpec((1, D), lambda i: (0, 0))]
        + [pl.BlockSpec(d.shape, lambda i: (0, 0)) for d in extra],
        out_specs=pl.BlockSpec((tm, D), lambda i: (i, 0)), out_shape=jax.ShapeDtypeStruct((S, D), BF16),
        compiler_params=_cparams(("parallel",)),
    )(x, gain, *extra)


def _rms_bwd(name, x, gain, dh, dres):
    S, D = x.shape
    tm = _tile(S, 256, 8)

    def body(x_ref, g_ref, dh_ref, dres_ref, dx_ref, dg_ref):
        i = pl.program_id(0)
        xv = x_ref[...]
        r = lax.rsqrt(jnp.mean(xv * xv, axis=-1, keepdims=True) + RMS_EPS)
        xhat = xv * r
        dhv = dh_ref[...]
        dxhat = dhv * g_ref[...]
        dx_ref[...] = dres_ref[...] + r * (dxhat - xhat * jnp.mean(dxhat * xhat, axis=-1, keepdims=True))

        @pl.when(i == 0)
        def _():
            dg_ref[...] = jnp.zeros_like(dg_ref)

        dg_ref[...] += jnp.sum(dhv * xhat, axis=0, keepdims=True)

    row = pl.BlockSpec((tm, D), lambda i: (i, 0))
    vec = pl.BlockSpec((1, D), lambda i: (0, 0))
    return pl.pallas_call(
        body, name=name, grid=(S // tm,), in_specs=[row, vec, row, row], out_specs=(row, vec),
        out_shape=(jax.ShapeDtypeStruct((S, D), F32), jax.ShapeDtypeStruct((1, D), F32)),
        compiler_params=_cparams(("arbitrary",)),
    )(x, gain, dh, dres)


def _loss_grad(name, y, target):
    S, D = y.shape
    tm = _tile(S, 256, 8)

    def body(y_ref, t_ref, dy_ref, l_ref):
        i = pl.program_id(0)
        e = y_ref[...] - t_ref[...]
        dy_ref[...] = e * (1.0 / D)
        col = jnp.sum(e * e, axis=0, keepdims=True)
        part = col[:, 0:LANES]
        for k in range(1, D // LANES):
            part = part + col[:, k * LANES:(k + 1) * LANES]

        @pl.when(i == 0)
        def _():
            l_ref[...] = jnp.zeros_like(l_ref)

        l_ref[...] += part

    row = pl.BlockSpec((tm, D), lambda i: (i, 0))
    return pl.pallas_call(
        body, name=name, grid=(S // tm,), in_specs=[row, row],
        out_specs=(row, pl.BlockSpec((1, LANES), lambda i: (0, 0))),
        out_shape=(jax.ShapeDtypeStruct((S, D), F32), jax.ShapeDtypeStruct((1, LANES), F32)),
        compiler_params=_cparams(("arbitrary",)),
    )(y, target)


def _shift_down(u, k):
    rows = lax.broadcasted_iota(jnp.int32, u.shape, 0)
    return jnp.where(rows >= k, pltpu.roll(u, k, 0), 0.0)


def _shift_up(u, k):
    n = u.shape[0]
    rows = lax.broadcasted_iota(jnp.int32, u.shape, 0)
    return jnp.where(rows < n - k, pltpu.roll(u, n - k, 0), 0.0)


def _conv_specs(S, cw, conv_width):
    nb = conv_width // cw
    col = lambda off: pl.BlockSpec((S, cw), lambda j, off=off: (0, off * nb + j))
    return nb, col(0), col(1), col(2)


def _conv_fwd(name, proj, convw3, conv_width):
    S = proj.shape[0]
    cw = convw3.shape[2]
    nb, xc_s, bg_s, cg_s = _conv_specs(S, cw, conv_width)

    def body(xc_ref, bg_ref, cg_ref, w_ref, o_ref):
        u = cg_ref[...] * xc_ref[...]
        w = w_ref[...]
        cv = w[2:3, :] * u + w[1:2, :] * _shift_down(u, 1) + w[0:1, :] * _shift_down(u, 2)
        o_ref[...] = (bg_ref[...] * cv).astype(BF16)

    return pl.pallas_call(
        body, name=name, grid=(nb,),
        in_specs=[xc_s, bg_s, cg_s, pl.BlockSpec((None, CONV_K, cw), lambda j: (j, 0, 0))],
        out_specs=pl.BlockSpec((S, cw), lambda j: (0, j)), out_shape=jax.ShapeDtypeStruct((S, conv_width), BF16),
        compiler_params=_cparams(("parallel",)),
    )(proj, proj, proj, convw3)


def _conv_bwd(name, proj, convw3, da, conv_width):
    S = proj.shape[0]
    cw = convw3.shape[2]
    nb, xc_s, bg_s, cg_s = _conv_specs(S, cw, conv_width)

    def body(xc_ref, bg_ref, cg_ref, w_ref, da_ref, dxc_ref, dbg_ref, dcg_ref, dw_ref):
        xc, cg = xc_ref[...], cg_ref[...]
        u = cg * xc
        w = w_ref[...]
        u1, u2 = _shift_down(u, 1), _shift_down(u, 2)
        cv = w[2:3, :] * u + w[1:2, :] * u1 + w[0:1, :] * u2
        dav = da_ref[...]
        dbg_ref[...] = (dav * cv).astype(BF16)
        dcv = dav * bg_ref[...]
        du = w[2:3, :] * dcv + w[1:2, :] * _shift_up(dcv, 1) + w[0:1, :] * _shift_up(dcv, 2)
        dxc_ref[...] = (du * cg).astype(BF16)
        dcg_ref[...] = (du * xc).astype(BF16)
        dw_ref[0:1, :] = jnp.sum(dcv * u2, axis=0, keepdims=True)
        dw_ref[1:2, :] = jnp.sum(dcv * u1, axis=0, keepdims=True)
        dw_ref[2:3, :] = jnp.sum(dcv * u, axis=0, keepdims=True)

    wspec = pl.BlockSpec((None, CONV_K, cw), lambda j: (j, 0, 0))
    ospec = pl.BlockSpec((S, cw), lambda j: (0, j))
    act = jax.ShapeDtypeStruct((S, conv_width), BF16)
    return pl.pallas_call(
        body, name=name, grid=(nb,), in_specs=[xc_s, bg_s, cg_s, wspec, ospec],
        out_specs=(ospec, ospec, ospec, wspec),
        out_shape=(act, act, act, jax.ShapeDtypeStruct(convw3.shape, F32)),
        compiler_params=_cparams(("parallel",)),
    )(proj, proj, proj, convw3, da)


def _rope_consts(S, dh):
    rot = dh // 4
    half = rot // 2
    inv_freq = 1.0 / (ROPE_THETA ** (jnp.arange(0, rot, 2, dtype=F32) / rot))
    ang = jnp.arange(S, dtype=F32)[:, None] * inv_freq[None, :]
    cos = jnp.concatenate([jnp.cos(ang), jnp.cos(ang), jnp.ones((S, dh - rot), F32)], axis=1)
    sin = jnp.concatenate([jnp.sin(ang), jnp.sin(ang), jnp.zeros((S, dh - rot), F32)], axis=1)
    rm = np.zeros((dh, dh), np.float32)
    for j in range(half):
        rm[j + half, j] = -1.0
        rm[j, j + half] = 1.0
    return cos, sin, jnp.asarray(rm, BF16), jnp.asarray(rm.T, BF16)


def _exact_perm(y, rm):
    hi = y.astype(BF16)
    r1 = y - hi.astype(F32)
    mid = r1.astype(BF16)
    lo = (r1 - mid.astype(F32)).astype(BF16)
    dot = lambda a: jnp.dot(a, rm, preferred_element_type=F32)
    return dot(hi) + dot(mid) + dot(lo)


def _qk_prep(name, xh, gain, cos, sin, rm):
    H, S, dh = xh.shape
    tm = _tile(S, 1024, 8)

    def body(x_ref, g_ref, c_ref, s_ref, rm_ref, o_ref):
        xv = x_ref[...]
        y = xv * lax.rsqrt(jnp.mean(xv * xv, axis=-1, keepdims=True) + RMS_EPS) * g_ref[...]
        o_ref[...] = (y * c_ref[...] + _exact_perm(y, rm_ref[...]) * s_ref[...]).astype(BF16)

    blk = pl.BlockSpec((None, tm, dh), lambda h, i: (h, i, 0))
    tab = pl.BlockSpec((tm, dh), lambda h, i: (i, 0))
    return pl.pallas_call(
        body, name=name, grid=(H, S // tm),
        in_specs=[blk, pl.BlockSpec((1, dh), lambda h, i: (0, 0)), tab, tab, pl.BlockSpec((dh, dh), lambda h, i: (0, 0))],
        out_specs=blk, out_shape=jax.ShapeDtypeStruct((H, S, dh), BF16),
        compiler_params=_cparams(("parallel", "parallel")),
    )(xh, gain, cos, sin, rm)


def _qk_prep_bwd(name, xh, gain, cos, sin, rmt, dout):
    H, S, dh = xh.shape
    tm = _tile(S, 1024, 8)

    def body(x_ref, g_ref, c_ref, s_ref, rmt_ref, do_ref, dx_ref, dg_ref):
        first = (pl.program_id(0) == 0) & (pl.program_id(1) == 0)
        xv = x_ref[...]
        r = lax.rsqrt(jnp.mean(xv * xv, axis=-1, keepdims=True) + RMS_EPS)
        xhat = xv * r
        dov = do_ref[...]
        dy = dov * c_ref[...] + _exact_perm(dov * s_ref[...], rmt_ref[...])
        dxhat = dy * g_ref[...]
        dx_ref[...] = (r * (dxhat - xhat * jnp.mean(dxhat * xhat, axis=-1, keepdims=True))).astype(BF16)

        @pl.when(first)
        def _():
            dg_ref[...] = jnp.zeros_like(dg_ref)

        dg_ref[...] += jnp.sum(dy * xhat, axis=0, keepdims=True)

    blk = pl.BlockSpec((None, tm, dh), lambda h, i: (h, i, 0))
    tab = pl.BlockSpec((tm, dh), lambda h, i: (i, 0))
    vec = pl.BlockSpec((1, dh), lambda h, i: (0, 0))
    return pl.pallas_call(
        body, name=name, grid=(H, S // tm),
        in_specs=[blk, vec, tab, tab, pl.BlockSpec((dh, dh), lambda h, i: (0, 0)), blk],
        out_specs=(blk, vec), out_shape=(jax.ShapeDtypeStruct((H, S, dh), BF16), jax.ShapeDtypeStruct((1, dh), F32)),
        compiler_params=_cparams(("arbitrary", "arbitrary")),
    )(xh, gain, cos, sin, rmt, dout)


def _attn_probs(q, kp, kc, sink_col, n, scale):
    rows = q.shape[0]
    sp = lax.dot_general(q, kp, NT, preferred_element_type=F32) * scale
    sc = lax.dot_general(q, kc, NT, preferred_element_type=F32) * scale
    qi = lax.broadcasted_iota(jnp.int32, (rows, BLOCK), 0) % BLOCK
    kj = lax.broadcasted_iota(jnp.int32, (rows, BLOCK), 1)
    sp = jnp.where((kj > qi) & (n > 0), sp, NEG_INF)
    sc = jnp.where(kj <= qi, sc, NEG_INF)
    m = jnp.maximum(jnp.maximum(jnp.max(sp, axis=-1, keepdims=True), jnp.max(sc, axis=-1, keepdims=True)), sink_col)
    pp, pc, ps = jnp.exp(sp - m), jnp.exp(sc - m), jnp.exp(sink_col - m)
    inv = 1.0 / (jnp.sum(pp, axis=-1, keepdims=True) + jnp.sum(pc, axis=-1, keepdims=True) + ps)
    return pp * inv, pc * inv, ps * inv


def _sink_col(sink_ref, hk, group):
    rows = group * BLOCK
    g = lax.broadcasted_iota(jnp.int32, (rows, 1), 0) // BLOCK
    col = jnp.zeros((rows, 1), F32)
    for i in range(group):
        col = jnp.where(g == i, sink_ref[hk * group + i], col)
    return col


def _attn_specs(group, dh):
    qb = pl.BlockSpec((group, BLOCK, dh), lambda hk, n: (hk, n, 0))
    prev = pl.BlockSpec((None, BLOCK, dh), lambda hk, n: (hk, jnp.maximum(n - 1, 0), 0))
    cur = pl.BlockSpec((None, BLOCK, dh), lambda hk, n: (hk, n, 0))
    return qb, prev, cur, pl.BlockSpec(memory_space=pltpu.SMEM)


def _attn_fwd(name, q, k, v, sinks):
    HQ, S, dh = q.shape
    HKV = k.shape[0]
    group = HQ // HKV
    scale = dh ** -0.5
    qb, prev, cur, smem = _attn_specs(group, dh)

    def body(q_ref, kp_ref, kc_ref, vp_ref, vc_ref, sink_ref, o_ref):
        hk, n = pl.program_id(0), pl.program_id(1)
        qv = q_ref[...].reshape(group * BLOCK, dh)
        pp, pc, _ = _attn_probs(qv, kp_ref[...], kc_ref[...], _sink_col(sink_ref, hk, group), n, scale)
        o = jnp.dot(pp.astype(BF16), vp_ref[...], preferred_element_type=F32)
        o = o + jnp.dot(pc.astype(BF16), vc_ref[...], preferred_element_type=F32)
        o_ref[...] = o.reshape(group, BLOCK, dh).astype(BF16)

    return pl.pallas_call(
        body, name=name, grid=(HKV, S // BLOCK), in_specs=[qb, prev, cur, prev, cur, smem], out_specs=qb,
        out_shape=jax.ShapeDtypeStruct((HQ, S, dh), BF16), compiler_params=_cparams(("parallel", "parallel")),
    )(q, k, k, v, v, sinks)


def _attn_bwd(name, q, k, v, sinks, do):
    HQ, S, dh = q.shape
    HKV = k.shape[0]
    group = HQ // HKV
    scale = dh ** -0.5
    qb, prev, cur, smem = _attn_specs(group, dh)
    whole = pl.BlockSpec((None, S, dh), lambda hk, n: (hk, 0, 0))
    sk = pl.BlockSpec((None, group, LANES), lambda hk, n: (hk, 0, 0))

    def body(q_ref, kp_ref, kc_ref, vp_ref, vc_ref, sink_ref, do_ref, dq_ref, dk_ref, dv_ref, ds_ref):
        hk, n = pl.program_id(0), pl.program_id(1)
        rows = group * BLOCK
        qv = q_ref[...].reshape(rows, dh)
        dov = do_ref[...].reshape(rows, dh)
        kp, kc, vp, vc = kp_ref[...], kc_ref[...], vp_ref[...], vc_ref[...]
        pp, pc, ps = _attn_probs(qv, kp, kc, _sink_col(sink_ref, hk, group), n, scale)
        dpp = lax.dot_general(dov, vp, NT, preferred_element_type=F32)
        dpc = lax.dot_general(dov, vc, NT, preferred_element_type=F32)
        delta = jnp.sum(pp * dpp, axis=-1, keepdims=True) + jnp.sum(pc * dpc, axis=-1, keepdims=True)
        dsp = (pp * (dpp - delta) * scale).astype(BF16)
        dsc = (pc * (dpc - delta) * scale).astype(BF16)
        dq = jnp.dot(dsp, kp, preferred_element_type=F32) + jnp.dot(dsc, kc, preferred_element_type=F32)
        dq_ref[...] = dq.reshape(group, BLOCK, dh)

        @pl.when(n == 0)
        def _():
            dk_ref[...] = jnp.zeros_like(dk_ref)
            dv_ref[...] = jnp.zeros_like(dv_ref)
            ds_ref[...] = jnp.zeros_like(ds_ref)

        cur_rows = pl.ds(pl.multiple_of(n * BLOCK, BLOCK), BLOCK)
        prev_rows = pl.ds(pl.multiple_of(jnp.maximum(n - 1, 0) * BLOCK, BLOCK), BLOCK)
        tdot = lambda a, b: lax.dot_general(a, b, TN, preferred_element_type=F32)
        dk_ref[prev_rows, :] += tdot(dsp, qv)
        dv_ref[prev_rows, :] += tdot(pp.astype(BF16), dov)
        dk_ref[cur_rows, :] += tdot(dsc, qv)
        dv_ref[cur_rows, :] += tdot(pc.astype(BF16), dov)
        dsink = -jnp.sum((ps * delta).reshape(group, BLOCK, 1), axis=1)
        ds_ref[...] += jnp.broadcast_to(dsink, (group, LANES))

    return pl.pallas_call(
        body, name=name, grid=(HKV, S // BLOCK), in_specs=[qb, prev, cur, prev, cur, smem, qb],
        out_specs=(qb, whole, whole, sk),
        out_shape=(jax.ShapeDtypeStruct((HQ, S, dh), F32), jax.ShapeDtypeStruct((HKV, S, dh), F32),
                   jax.ShapeDtypeStruct((HKV, S, dh), F32), jax.ShapeDtypeStruct((HKV, group, LANES), F32)),
        compiler_params=_cparams(("arbitrary", "arbitrary")),
    )(q, k, k, v, v, sinks, do)


def _gate_specs(S, D, ga_off, gb_off):
    tg = LANES
    for t in range(LANES, 513, LANES):
        if D % t == 0 and ga_off % t == 0 and gb_off % t == 0:
            tg = t
    if D % LANES:
        tg = math.gcd(math.gcd(D, ga_off), gb_off)
    tm = _tile(S, 512, 8)
    act = pl.BlockSpec((tm, tg), lambda i, j: (i, j))
    ga = pl.BlockSpec((tm, tg), lambda i, j: (i, ga_off // tg + j))
    gb = pl.BlockSpec((tm, tg), lambda i, j: (i, gb_off // tg + j))
    return (S // tm, D // tg), act, ga, gb


def _gate_fwd(name, proj, ya, yb, ga_off, gb_off):
    S, D = ya.shape
    grid, act, ga, gb = _gate_specs(S, D, ga_off, gb_off)

    def body(ga_ref, gb_ref, ya_ref, yb_ref, o_ref):
        o_ref[...] = (_sigmoid(ga_ref[...]) * ya_ref[...] + _sigmoid(gb_ref[...]) * yb_ref[...]).astype(BF16)

    return pl.pallas_call(
        body, name=name, grid=grid, in_specs=[ga, gb, act, act], out_specs=act,
        out_shape=jax.ShapeDtypeStruct((S, D), BF16), compiler_params=_cparams(("parallel", "parallel")),
    )(proj, proj, ya, yb)


def _gate_bwd(name, proj, ya, yb, dm, ga_off, gb_off):
    S, D = ya.shape
    grid, act, ga, gb = _gate_specs(S, D, ga_off, gb_off)

    def body(ga_ref, gb_ref, ya_ref, yb_ref, dm_ref, dga_ref, dgb_ref, dya_ref, dyb_ref):
        dmv = dm_ref[...]
        sa, sb = _sigmoid(ga_ref[...]), _sigmoid(gb_ref[...])
        dga_ref[...] = (dmv * ya_ref[...] * sa * (1.0 - sa)).astype(BF16)
        dgb_ref[...] = (dmv * yb_ref[...] * sb * (1.0 - sb)).astype(BF16)
        dya_ref[...] = (dmv * sa).astype(BF16)
        dyb_ref[...] = (dmv * sb).astype(BF16)

    o = jax.ShapeDtypeStruct((S, D), BF16)
    return pl.pallas_call(
        body, name=name, grid=grid, in_specs=[ga, gb, act, act, act], out_specs=(act, act, act, act),
        out_shape=(o, o, o, o), compiler_params=_cparams(("parallel", "parallel")),
    )(proj, proj, ya, yb, dm)


def _row_tile(rows, cols, n_arrays):
    want = max(8, (VMEM_LIMIT_V7X // 2) // (2 * n_arrays * cols * 4))
    return _tile(rows, want, 8)


def _cast_to_slot(name, w, dtype, p_arr, dep=None):
    R, C = w.shape
    tr = _row_tile(R, C, 2)
    extra = () if dep is None else (dep,)

    def body(p_ref, w_ref, *rest):
        rest[-1][...] = w_ref[...].astype(dtype)

    return pl.pallas_call(
        body, name=name,
        grid_spec=pltpu.PrefetchScalarGridSpec(
            num_scalar_prefetch=1, grid=(R // tr,),
            in_specs=[pl.BlockSpec((tr, C), lambda i, p_ref: (i, 0))] + [pl.BlockSpec(d.shape, lambda i, p_ref: (0, 0)) for d in extra],
            out_specs=pl.BlockSpec((None, tr, C), lambda i, p_ref: (p_ref[0], i, 0))),
        out_shape=jax.ShapeDtypeStruct((N_CHIPS, R, C), dtype), compiler_params=_cparams(("parallel",)),
    )(p_arr, w, *extra)


def _add_half(name, g3, r3, c_arr):
    n, h, C = r3.shape
    tr = _row_tile(h, C, 3)
    nb = h // tr

    def body(c_ref, g_ref, r_ref, o_ref):
        o_ref[...] = (g_ref[...] + r_ref[...]).astype(BF16)

    blk = pl.BlockSpec((None, tr, C), lambda s, i, c_ref: (s, i, 0))
    return pl.pallas_call(
        body, name=name,
        grid_spec=pltpu.PrefetchScalarGridSpec(
            num_scalar_prefetch=1, grid=(n, nb),
            in_specs=[pl.BlockSpec((None, tr, C), lambda s, i, c_ref: (s, c_ref[0] * nb + i, 0)), blk], out_specs=blk),
        out_shape=jax.ShapeDtypeStruct(r3.shape, BF16), compiler_params=_cparams(("parallel", "parallel")),
    )(c_arr, g3, r3)


def _add_chips(name, t3, r3, cp_arr):
    n, h, C = r3.shape
    tr = _row_tile(h, C, 6)
    nb = h // tr

    def body(cp_ref, t_ref, r0_ref, r1_ref, r2_ref, r3_ref, o_ref):
        p = cp_ref[1]
        total = None
        for a, r_ref in enumerate((r0_ref, r1_ref, r2_ref, r3_ref)):
            part = jnp.where(p == a, t_ref[...], r_ref[...]).astype(F32)
            total = part if total is None else total + part
        o_ref[...] = total

    def part(a):
        return pl.BlockSpec((None, tr, C), lambda i, cp_ref: (jnp.where(cp_ref[1] == a, (a + 1) % N_CHIPS, a), i, 0))

    return pl.pallas_call(
        body, name=name,
        grid_spec=pltpu.PrefetchScalarGridSpec(
            num_scalar_prefetch=1, grid=(nb,),
            in_specs=[pl.BlockSpec((None, tr, C), lambda i, cp_ref: (cp_ref[1], i, 0)), part(0), part(1), part(2), part(3)],
            out_specs=pl.BlockSpec((tr, C), lambda i, cp_ref: (cp_ref[0] * nb + i, 0))),
        out_shape=jax.ShapeDtypeStruct((2 * h, C), F32), compiler_params=_cparams(("parallel",)),
    )(cp_arr, t3, r3, r3, r3, r3)


def _adamw(name, w, g, m, v):
    R, C = w.shape
    tr = _row_tile(R, C, 7)
    c1 = 1.0 - ADAM_B1 ** ADAM_STEP
    c2 = 1.0 - ADAM_B2 ** ADAM_STEP

    def body(w_ref, g_ref, m_ref, v_ref, d_ref, nm_ref, nv_ref):
        gv = g_ref[...]
        nm = ADAM_B1 * m_ref[...] + (1.0 - ADAM_B1) * gv
        nv = ADAM_B2 * v_ref[...] + (1.0 - ADAM_B2) * (gv * gv)
        d_ref[...] = -ADAM_LR * ((nm / c1) / (jnp.sqrt(nv / c2) + ADAM_EPS) + ADAM_WD * w_ref[...])
        nm_ref[...] = nm
        nv_ref[...] = nv

    blk = pl.BlockSpec((tr, C), lambda i: (i, 0))
    o = jax.ShapeDtypeStruct((R, C), F32)
    return pl.pallas_call(
        body, name=name, grid=(R // tr,), in_specs=[blk, blk, blk, blk], out_specs=(blk, blk, blk),
        out_shape=(o, o, o), compiler_params=_cparams(("parallel",)),
    )(w, g, m, v)


def _place():
    x, y, c = lax.axis_index("x"), lax.axis_index("y"), lax.axis_index("c")
    chips = [(1 - x, y), (x, 1 - y), (1 - x, 1 - y)]
    return x, y, c, 2 * x + y, chips


ANY = pl.BlockSpec(memory_space=pl.ANY)


HBM = pl.BlockSpec(memory_space=pltpu.HBM)
SEM = pl.BlockSpec(memory_space=pltpu.SEMAPHORE)
TOKEN = jax.ShapeDtypeStruct((8, LANES), F32)
DATAFLOW = pltpu.SideEffectType.DATAFLOW_SIDE_EFFECTING


def _hbm(a):
    return pltpu.with_memory_space_constraint(a, pltpu.HBM)


def _gather_blocks(bufs, i, c, p, chips):
    if bufs[i].shape[1] % 16:
        return bufs[i].at[p], [bufs[i].at[2 * cx + cy] for cx, cy in chips]
    h = bufs[i].shape[1] // 2
    rows = pl.ds(pl.multiple_of(c * h, 16), h)
    return bufs[i].at[p, rows], [bufs[i].at[2 * cx + cy, rows] for cx, cy in chips]


def _gather_start(name, slots, dep):
    n = len(slots)

    def body(*refs):
        bufs, send, recv, token = refs[:n], refs[n + 1], refs[n + 2], refs[-1]
        x, y, c, p, chips = _place()
        for i in range(n):
            mine, _ = _gather_blocks(bufs, i, c, p, chips)
            for j, chip in enumerate(chips):
                pltpu.make_async_remote_copy(src_ref=mine, dst_ref=mine, send_sem=send.at[3 * i + j], recv_sem=recv.at[3 * i + j],
                                             device_id=(*chip, c), device_id_type=MESH).start()
        token[...] = jnp.zeros_like(token)

    out = pl.pallas_call(
        body, name=name, in_specs=[HBM] * n + [ANY],
        out_specs=(SEM, SEM, *([HBM] * n), pl.BlockSpec(memory_space=pltpu.VMEM)),
        out_shape=(pltpu.SemaphoreType.DMA((3 * n,)), pltpu.SemaphoreType.DMA((3 * n,)),
                   *[pltpu.HBM(s.shape, s.dtype) for s in slots], TOKEN),
        input_output_aliases={i: 2 + i for i in range(n)},
        compiler_params=pltpu.CompilerParams(has_side_effects=DATAFLOW),
    )(*[_hbm(s) for s in slots], dep)
    return out[0], out[1], list(out[2:2 + n]), out[-1]


def _gather_wait(name, send, recv, slots, after):
    n = len(slots)

    def body(*refs):
        bufs, send, recv = refs[:n], refs[n], refs[n + 1]
        x, y, c, p, chips = _place()
        for i in range(n):
            mine, landed = _gather_blocks(bufs, i, c, p, chips)
            for j, chip in enumerate(chips):
                cp = pltpu.make_async_remote_copy(src_ref=mine, dst_ref=landed[j], send_sem=send.at[3 * i + j],
                                                  recv_sem=recv.at[3 * i + j], device_id=(*chip, c), device_id_type=MESH)
                cp.wait_send()
                cp.wait_recv()

    return list(pl.pallas_call(
        body, name=name, in_specs=[HBM] * n + [SEM, SEM, ANY], out_specs=tuple([HBM] * n),
        out_shape=tuple(pltpu.HBM(s.shape, s.dtype) for s in slots),
        input_output_aliases={i: i for i in range(n)},
        compiler_params=pltpu.CompilerParams(has_side_effects=DATAFLOW),
    )(*slots, send, recv, after))


def _gather_forward(name, slots):
    idx = [i for i, s in enumerate(slots) if s.shape[1] % 16 == 0]
    n = len(slots)

    def body(*refs):
        bufs = refs[n:2 * n]
        send, recv = refs[2 * n:]
        x, y, c, p, chips = _place()

        def rdma(k, ref):
            return pltpu.make_async_remote_copy(src_ref=ref, dst_ref=ref, send_sem=send.at[k], recv_sem=recv.at[k],
                                                device_id=(x, y, 1 - c), device_id_type=MESH)

        cps = []
        for k, i in enumerate(idx):
            for j, ref in enumerate(_gather_blocks(bufs, i, c, p, chips)[1]):
                cps.append(rdma(3 * k + j, ref))
                cps[-1].start()
        for k, i in enumerate(idx):
            for j, ref in enumerate(_gather_blocks(bufs, i, 1 - c, p, chips)[1]):
                rdma(3 * k + j, ref).wait_recv()
        for cp in cps:
            cp.wait_send()

    return list(pl.pallas_call(
        body, name=name, in_specs=[ANY] * n, out_specs=tuple([ANY] * n),
        out_shape=tuple(jax.ShapeDtypeStruct(s.shape, s.dtype) for s in slots),
        scratch_shapes=[pltpu.SemaphoreType.DMA((3 * len(idx),)), pltpu.SemaphoreType.DMA((3 * len(idx),))],
        input_output_aliases={i: i for i in range(n)},
        compiler_params=pltpu.CompilerParams(has_side_effects=True),
    )(*slots))


def _swap_copy(grads, lands, send, recv, i, x, y, c):
    h = grads[i].shape[1] // 2
    other = pl.ds(pl.multiple_of((1 - c) * h, 8), h)
    return pltpu.make_async_remote_copy(src_ref=grads[i].at[:, other, :], dst_ref=lands[i], send_sem=send.at[i],
                                        recv_sem=recv.at[i], device_id=(x, y, 1 - c), device_id_type=MESH)


def _swap_start(name, grads):
    n = len(grads)

    def body(*refs):
        ins, lands, send, recv, token = refs[:n], refs[n:2 * n], refs[2 * n], refs[2 * n + 1], refs[-1]
        x, y, c, p, chips = _place()
        for i in range(n):
            _swap_copy(ins, lands, send, recv, i, x, y, c).start()
        token[...] = jnp.zeros_like(token)

    gshapes = [pltpu.HBM(g.shape, g.dtype) for g in grads]
    halves = [(g.shape[0], g.shape[1] // 2, g.shape[2]) for g in grads]
    lshapes = [pltpu.HBM(s, g.dtype) for s, g in zip(halves, grads)]
    out = pl.pallas_call(
        body, name=name, in_specs=[HBM] * (2 * n),
        out_specs=(SEM, SEM, *([HBM] * (2 * n)), pl.BlockSpec(memory_space=pltpu.VMEM)),
        out_shape=(pltpu.SemaphoreType.DMA((n,)), pltpu.SemaphoreType.DMA((n,)), *gshapes, *lshapes, TOKEN),
        input_output_aliases={i: 2 + i for i in range(2 * n)},
        compiler_params=pltpu.CompilerParams(has_side_effects=DATAFLOW),
    )(*[_hbm(g) for g in grads], *[_hbm(lax.empty(s, g.dtype)) for s, g in zip(halves, grads)])
    return out[0], out[1], list(out[2:2 + n]), list(out[2 + n:2 + 2 * n]), out[-1]


def _swap_wait(name, send, recv, grads, lands, after):
    n = len(grads)

    def body(*refs):
        ins, lands, send, recv = refs[:n], refs[n:2 * n], refs[2 * n], refs[2 * n + 1]
        x, y, c, p, chips = _place()
        for i in range(n):
            cp = _swap_copy(ins, lands, send, recv, i, x, y, c)
            cp.wait_send()
            cp.wait_recv()

    shapes = [pltpu.HBM(t.shape, t.dtype) for t in list(grads) + list(lands)]
    out = pl.pallas_call(
        body, name=name, in_specs=[HBM] * (2 * n) + [SEM, SEM, ANY], out_specs=tuple([HBM] * (2 * n)),
        out_shape=tuple(shapes), input_output_aliases={i: i for i in range(2 * n)},
        compiler_params=pltpu.CompilerParams(has_side_effects=DATAFLOW),
    )(*grads, *lands, send, recv, after)
    return list(out[:n]), list(out[n:])


def _exchange_start(name, parts):
    n = len(parts)

    def body(*refs):
        ins, lands, send, recv, token = refs[:n], refs[n:2 * n], refs[2 * n], refs[2 * n + 1], refs[-1]
        x, y, c, p, chips = _place()
        for i in range(n):
            for j, (cx, cy) in enumerate(chips):
                pltpu.make_async_remote_copy(src_ref=ins[i].at[2 * cx + cy], dst_ref=lands[i].at[p], send_sem=send.at[3 * i + j],
                                             recv_sem=recv.at[3 * i + j], device_id=(cx, cy, c), device_id_type=MESH).start()
        token[...] = jnp.zeros_like(token)

    shapes = [pltpu.HBM(t.shape, t.dtype) for t in parts]
    out = pl.pallas_call(
        body, name=name, in_specs=[HBM] * (2 * n),
        out_specs=(SEM, SEM, *([HBM] * (2 * n)), pl.BlockSpec(memory_space=pltpu.VMEM)),
        out_shape=(pltpu.SemaphoreType.DMA((3 * n,)), pltpu.SemaphoreType.DMA((3 * n,)), *shapes, *shapes, TOKEN),
        input_output_aliases={i: 2 + i for i in range(2 * n)},
        compiler_params=pltpu.CompilerParams(has_side_effects=DATAFLOW),
    )(*[_hbm(t) for t in parts], *[_hbm(lax.empty(t.shape, t.dtype)) for t in parts])
    return out[0], out[1], list(out[2:2 + n]), list(out[2 + n:2 + 2 * n]), out[-1]


def _exchange_wait(name, send, recv, parts, lands, after):
    n = len(parts)

    def body(*refs):
        ins, lands, send, recv = refs[:n], refs[n:2 * n], refs[2 * n], refs[2 * n + 1]
        x, y, c, p, chips = _place()
        for i in range(n):
            for j, (cx, cy) in enumerate(chips):
                q = 2 * cx + cy
                cp = pltpu.make_async_remote_copy(src_ref=ins[i].at[q], dst_ref=lands[i].at[q], send_sem=send.at[3 * i + j],
                                                  recv_sem=recv.at[3 * i + j], device_id=(cx, cy, c), device_id_type=MESH)
                cp.wait_send()
                cp.wait_recv()

    shapes = [pltpu.HBM(t.shape, t.dtype) for t in parts]
    out = pl.pallas_call(
        body, name=name, in_specs=[HBM] * (2 * n) + [SEM, SEM, ANY], out_specs=tuple([HBM] * (2 * n)),
        out_shape=(*shapes, *shapes), input_output_aliases={i: i for i in range(2 * n)},
        compiler_params=pltpu.CompilerParams(has_side_effects=DATAFLOW),
    )(*parts, *lands, send, recv, after)
    return list(out[:n]), list(out[n:])


def _join_halves(name, bufs):
    n = len(bufs)

    def body(*refs):
        outs = refs[n:2 * n]
        send, recv = refs[2 * n:]
        x, y, c, p, chips = _place()

        def rdma(i, which):
            h = outs[i].shape[0] // 2
            rows = outs[i].at[pl.ds(pl.multiple_of(which * h, 8), h)]
            return pltpu.make_async_remote_copy(src_ref=rows, dst_ref=rows, send_sem=send.at[i], recv_sem=recv.at[i],
                                                device_id=(x, y, 1 - c), device_id_type=MESH)

        cps = [rdma(i, c) for i in range(n)]
        for cp in cps:
            cp.start()
        for i, cp in enumerate(cps):
            rdma(i, 1 - c).wait_recv()
            cp.wait_send()

    return pl.pallas_call(
        body, name=name, in_specs=[ANY] * n, out_specs=tuple([ANY] * n),
        out_shape=tuple(jax.ShapeDtypeStruct(t.shape, t.dtype) for t in bufs),
        scratch_shapes=[pltpu.SemaphoreType.DMA((n,)), pltpu.SemaphoreType.DMA((n,))],
        input_output_aliases={i: i for i in range(n)},
        compiler_params=pltpu.CompilerParams(has_side_effects=True),
    )(*bufs)


def _allreduce_small(name, pack):
    R, W = pack.shape

    def body(in_ref, out_ref, slots, send, recv):
        x, y, c = lax.axis_index("x"), lax.axis_index("y"), lax.axis_index("c")
        me = 4 * x + 2 * y + c
        slots[0] = in_ref[...]
        cps = []
        for k in range(1, N_DEV):
            peer = (x ^ (k >> 2), y ^ ((k >> 1) & 1), c ^ (k & 1))
            cp = pltpu.make_async_remote_copy(src_ref=in_ref, dst_ref=slots.at[k], send_sem=send.at[k - 1],
                                              recv_sem=recv.at[k - 1], device_id=peer, device_id_type=MESH)
            cp.start()
            cps.append(cp)
        for cp in cps:
            cp.wait()
        total = slots[me]
        for a in range(1, N_DEV):
            total = total + slots[jnp.bitwise_xor(a, me)]
        out_ref[...] = total

    vmem = pl.BlockSpec(memory_space=pltpu.VMEM)
    return pl.pallas_call(
        body, name=name, in_specs=[vmem], out_specs=vmem, out_shape=jax.ShapeDtypeStruct((R, W), F32),
        scratch_shapes=[pltpu.VMEM((N_DEV, R, W), F32), pltpu.SemaphoreType.DMA((N_DEV - 1,)), pltpu.SemaphoreType.DMA((N_DEV - 1,))],
        compiler_params=pltpu.CompilerParams(has_side_effects=True),
    )(pack)


def _heads(a, n_heads):
    S = a.shape[0]
    return a.reshape(S, n_heads, a.shape[1] // n_heads).transpose(1, 0, 2)


def _unheads(a):
    H, S, dh = a.shape
    return a.transpose(1, 0, 2).reshape(S, H * dh)


def _ffn_bwd(tag, xin, gain, wgu3, wd, saved, dxout, reduce_start, dep, flush=None):
    h, gu, act = saved
    D = xin.shape[1]
    dxo_b = dxout.astype(BF16)
    dgu = _ffn_down_bwd(f"ffn_down_bwd_{tag}", dxo_b, wd, gu, 0.5, dep=dep)
    tok = reduce_start({f"w_gu{tag}": _mm_tn_cols(f"dw_gu_{tag}", h, dgu, wgu3.shape[2], b_is_gu=True)})
    tok = reduce_start({f"w_down{tag}": _mm_tn(f"dw_down_{tag}", act, dxo_b, 0.5, dep=tok).reshape(N_CHIPS, -1, D)})
    if flush is not None:
        tok = flush(tok)
    dh = _mm_nt_cols(f"ffn_up_bwd_{tag}", dgu, wgu3, a_is_gu=True, dep=tok)
    dxin, dgain = _rms_bwd(f"rms_bwd_{tag}", xin, gain, dh, dxout)
    return dxin, dgain, tok


def kernel(x, g_ffn1, w_gu1, w_down1, g_mix, w_in, conv_w, q_norm_g, k_norm_g, sinks, w_out_conv, w_out_attn, w_o, g_ffn2, w_gu2, w_down2, loss_target, m_g_ffn1, m_w_gu1, m_w_down1, m_g_mix, m_w_in, m_conv_w, m_q_norm_g, m_k_norm_g, m_sinks, m_w_out_conv, m_w_out_attn, m_w_o, m_g_ffn2, m_w_gu2, m_w_down2, v_g_ffn1, v_w_gu1, v_w_down1, v_g_mix, v_w_in, v_conv_w, v_q_norm_g, v_k_norm_g, v_sinks, v_w_out_conv, v_w_out_attn, v_w_o, v_g_ffn2, v_w_gu2, v_w_down2):
    S, D = x.shape[1], x.shape[2]
    dh = q_norm_g.shape[1]
    HQ = sinks.shape[1]
    HKV = HQ // 4
    AW, KVW, CW = HQ * dh, HKV * dh, D // 2
    off_q, off_k, off_v = 3 * CW, 3 * CW + AW, 3 * CW + AW + KVW
    off_ga, off_gb = off_v + KVW, off_v + KVW + D
    x0, target = x[0], loss_target[0]
    cx, cy, cc = lax.axis_index("x"), lax.axis_index("y"), lax.axis_index("c")
    chip = 2 * cx + cy
    p_arr = jnp.reshape(chip, (1,)).astype(jnp.int32)
    c_arr = jnp.reshape(cc, (1,)).astype(jnp.int32)
    cp_arr = jnp.stack([cc, chip]).astype(jnp.int32)
    wts = dict(g_ffn1=g_ffn1, w_gu1=w_gu1, w_down1=w_down1, g_mix=g_mix, w_in=w_in, conv_w=conv_w, q_norm_g=q_norm_g,
               k_norm_g=k_norm_g, sinks=sinks, w_out_conv=w_out_conv, w_out_attn=w_out_attn, w_o=w_o, g_ffn2=g_ffn2,
               w_gu2=w_gu2, w_down2=w_down2)
    ms = dict(g_ffn1=m_g_ffn1, w_gu1=m_w_gu1, w_down1=m_w_down1, g_mix=m_g_mix, w_in=m_w_in, conv_w=m_conv_w,
              q_norm_g=m_q_norm_g, k_norm_g=m_k_norm_g, sinks=m_sinks, w_out_conv=m_w_out_conv, w_out_attn=m_w_out_attn,
              w_o=m_w_o, g_ffn2=m_g_ffn2, w_gu2=m_w_gu2, w_down2=m_w_down2)
    vs = dict(g_ffn1=v_g_ffn1, w_gu1=v_w_gu1, w_down1=v_w_down1, g_mix=v_g_mix, w_in=v_w_in, conv_w=v_conv_w,
              q_norm_g=v_q_norm_g, k_norm_g=v_k_norm_g, sinks=v_sinks, w_out_conv=v_w_out_conv, w_out_attn=v_w_out_attn,
              w_o=v_w_o, g_ffn2=v_g_ffn2, w_gu2=v_w_gu2, w_down2=v_w_down2)
    order = list(wts)
    small_names = [k for k in order if not k.startswith("w_")]
    grad, delta, new_m, new_v = {}, {}, {}, {}

    def cast(keys, dep=None):
        return [_cast_to_slot(f"cast_{k}", wts[k][0], F32 if k == "conv_w" else BF16, p_arr, dep) for k in keys]

    def gather_start(tag, slots, dep):
        send, recv, slots, tok = _gather_start(f"gather_start_{tag}", slots, dep)
        return (tag, send, recv, slots), tok

    def gather_finish(started, after):
        tag, send, recv, slots = started
        return _gather_forward(f"gather_forward_{tag}", _gather_wait(f"gather_wait_{tag}", send, recv, slots, after))

    swapping, pending = [], []

    def reduce_start(full):
        keys = list(full)
        send, recv, gs, lands, tok = _swap_start(f"swap_start_{keys[0]}", [full[k] for k in keys])
        if swapping:
            tok = reduce_advance(tok)
        swapping.append((keys, send, recv, gs, lands))
        return tok

    def reduce_advance(after):
        keys, send, recv, gs, lands = swapping.pop(0)
        gs, sib = _swap_wait(f"swap_wait_{keys[0]}", send, recv, gs, lands, after)
        parts = [_add_half(f"add_half_{k}", g, r, c_arr) for k, g, r in zip(keys, gs, sib)]
        send, recv, parts, lands, tok = _exchange_start(f"exchange_start_{keys[0]}", parts)
        pending.append((keys, send, recv, parts, lands))
        return tok

    def reduce_finish(entries, after):
        keys_all, halves = [], []
        for keys, send, recv, parts, lands in entries:
            parts, lands = _exchange_wait(f"exchange_wait_{keys[0]}", send, recv, parts, lands, after)
            halves += [_add_chips(f"add_chips_{k}", t, r, cp_arr) for k, t, r in zip(keys, parts, lands)]
            keys_all += keys
        for k, g2 in zip(keys_all, _join_halves(f"join_{keys_all[0]}", halves)):
            d, nm, nv = _adamw(f"adamw_{k}", wts[k][0], g2, ms[k][0], vs[k][0])
            grad[k], delta[k], new_m[k], new_v[k] = g2[None], d[None], nm[None], nv[None]

    keys_mix, keys_2 = ["w_in", "w_out_conv", "w_out_attn", "w_o", "conv_w"], ["w_gu2", "w_down2"]
    st_gu1, tok = gather_start("gu1", cast(["w_gu1"]), x0)
    st_d1, tok = gather_start("d1", cast(["w_down1"]), tok)
    slots_mix, slots_2 = cast(keys_mix, tok), cast(keys_2, tok)
    wgu1, = gather_finish(st_gu1, slots_2[-1])
    st_mix, tok = gather_start("mix", slots_mix, wgu1)
    st_2, tok = gather_start("2", slots_2, tok)
    cos, sin, rm, rmt = _rope_consts(S, dh)
    sink_vec = sinks[0]

    h1 = _rms_fwd("rms_fwd_1", x0, g_ffn1, tok)
    gu1, act1 = _ffn_up("ffn_up_1", h1, wgu1)
    wd1 = gather_finish(st_d1, act1)[0].reshape(-1, D)
    x1 = _mm_res("ffn_down_1", act1, wd1, x0, 0.5)
    win3, woc3, woa3, wo, convw3 = gather_finish(st_mix, x1)
    wo = wo.reshape(-1, D)
    h2 = _rms_fwd("rms_fwd_mix", x1, g_mix)
    proj = _mm_cols("in_proj", h2, win3, F32)
    aconv = _conv_fwd("conv_fwd", proj, convw3, CW)
    ya = _mm_cols("out_conv", aconv, woc3, F32)
    q_raw = _heads(proj[:, off_q:off_q + AW], HQ)
    k_raw = _heads(proj[:, off_k:off_k + KVW], HKV)
    vh = _heads(proj[:, off_v:off_v + KVW], HKV).astype(BF16)
    qn = _qk_prep("q_prep", q_raw, q_norm_g, cos, sin, rm)
    kn = _qk_prep("k_prep", k_raw, k_norm_g, cos, sin, rm)
    oh = _attn_fwd("attn_fwd", qn, kn, vh, sink_vec)
    o = _unheads(oh)
    yb = _mm_cols("out_attn", o, woa3, F32)
    merged = _gate_fwd("gate_fwd", proj, ya, yb, off_ga, off_gb)
    x2 = _mm_res("mix_out", merged, wo, x1, 1.0)
    wgu2, wd2 = gather_finish(st_2, x2)
    wd2 = wd2.reshape(-1, D)
    h3 = _rms_fwd("rms_fwd_2", x2, g_ffn2)
    gu2, act2 = _ffn_up("ffn_up_2", h3, wgu2)
    x3 = _mm_res("ffn_down_2", act2, wd2, x2, 0.5)

    dy, loss_lanes = _loss_grad("loss_grad", x3, target)
    dx2, dg_ffn2, tok = _ffn_bwd("2", x2, g_ffn2, wgu2, wd2, (h3, gu2, act2), dy, reduce_start, None)
    dx2_b = dx2.astype(BF16)
    dmerged = _mm_nt("mix_out_bwd", dx2_b, wo, F32)
    tok = reduce_start(dict(w_o=_mm_tn("dw_o", merged, dx2_b, dep=tok).reshape(N_CHIPS, -1, D)))
    dga, dgb, dya, dyb = _gate_bwd("gate_bwd", proj, ya, yb, dmerged, off_ga, off_gb)
    daconv = _mm_nt_cols("out_conv_bwd", dya, woc3, dep=tok)
    dwoc = _mm_tn_cols("dw_out_conv", aconv, dya, woc3.shape[2])
    do = _mm_nt_cols("out_attn_bwd", dyb, woa3)
    dwoa = _mm_tn_cols("dw_out_attn", o, dyb, woa3.shape[2])
    tok = reduce_start(dict(w_out_conv=dwoc, w_out_attn=dwoa))
    dxc, dbg, dcg, dconvw = _conv_bwd("conv_bwd", proj, convw3, daconv, CW)
    dqn, dkn, dvh, dsink3 = _attn_bwd("attn_bwd", qn, kn, vh, sink_vec, _heads(do, HQ).astype(BF16))
    dq_raw, dqg = _qk_prep_bwd("q_prep_bwd", q_raw, q_norm_g, cos, sin, rmt, dqn)
    dk_raw, dkg = _qk_prep_bwd("k_prep_bwd", k_raw, k_norm_g, cos, sin, rmt, dkn)
    dproj = jnp.concatenate([dxc, dbg, dcg, _unheads(dq_raw), _unheads(dk_raw), _unheads(dvh).astype(BF16), dga, dgb], axis=1)
    dh2 = _mm_nt_cols("in_proj_bwd", dproj, win3, dep=tok)
    tok = reduce_start(dict(w_in=_mm_tn_cols("dw_in", h2, dproj, win3.shape[2])))
    dx1, dg_mix = _rms_bwd("rms_bwd_mix", x1, g_mix, dh2, dx2)
    dx0, dg_ffn1, tok = _ffn_bwd("1", x0, g_ffn1, wgu1, wd1, (h1, gu1, act1), dx1, reduce_start, tok, reduce_advance)

    def rows8(a):
        a = a.reshape(-1, a.shape[-1])
        return jnp.pad(a, ((0, -a.shape[0] % 8), (0, D - a.shape[1])))

    misc = jnp.concatenate([dqg, dkg, dsink3[:, :, 0].reshape(1, HQ), loss_lanes], axis=1)
    tot = _allreduce_small("allreduce_small", jnp.concatenate([rows8(a) for a in (dg_ffn1, dg_mix, dg_ffn2, dconvw, misc)], axis=0))

    reduce_finish(pending[:-2], dx0)
    reduce_finish(pending[-2:], dx0)

    cw_s = conv_w.shape[2]
    conv_row0, misc_row = 24, 24 + (-(-N_CHIPS * CONV_K // 8)) * 8
    small_g = dict(g_ffn1=tot[0:1], g_mix=tot[8:9], g_ffn2=tot[16:17],
                   conv_w=lax.dynamic_slice(tot, (conv_row0 + CONV_K * chip, 0), (CONV_K, cw_s)),
                   q_norm_g=tot[misc_row:misc_row + 1, 0:dh], k_norm_g=tot[misc_row:misc_row + 1, dh:2 * dh],
                   sinks=tot[misc_row:misc_row + 1, 2 * dh:2 * dh + HQ])
    loss = (0.5 / D) * jnp.sum(tot[misc_row, 2 * dh + HQ:2 * dh + HQ + LANES])

    def small_pack(src):
        return jnp.concatenate([rows8(src[k]) for k in small_names], axis=0)

    sd, sm, sv = _adamw("adamw_small", small_pack(wts), small_pack(small_g), small_pack(ms), small_pack(vs))
    for i, k in enumerate(small_names):
        shape = wts[k].shape
        nr, ncol = math.prod(shape[:-1]), shape[-1]
        grad[k] = small_g[k].reshape(shape)
        delta[k], new_m[k], new_v[k] = (a[8 * i:8 * i + nr, 0:ncol].reshape(shape) for a in (sd, sm, sv))
    return (loss, dx0[None], *[grad[k] for k in order], *[delta[k] for k in order],
            *[new_m[k] for k in order], *[new_v[k] for k in order])
```

```python
import math

import numpy as np
import jax
import jax.numpy as jnp
from jax import lax
from jax.experimental import pallas as pl
from jax.experimental.pallas import tpu as pltpu

F32 = jnp.float32
BF16 = jnp.bfloat16
MESH = pl.DeviceIdType.MESH

RMS_EPS = 1e-6
BLOCK = 128
ROPE_THETA = 500000.0
NEG_INF = -1e30
CONV_K = 3
ADAM_LR, ADAM_B1, ADAM_B2, ADAM_EPS, ADAM_WD, ADAM_STEP = 0.001, 0.9, 0.999, 1e-08, 0.01, 10

VMEM_LIMIT_V7X = 56 * 1024 * 1024
LANES = 128
N_CHIPS = 4
N_DEV = 8


def _tile(n, want, align=LANES):
    best = None
    t = align
    while t <= min(n, want):
        if n % t == 0:
            best = t
        t += align
    return best or n


def _cparams(sem):
    return pltpu.CompilerParams(dimension_semantics=sem, vmem_limit_bytes=VMEM_LIMIT_V7X)


def _sigmoid(x):
    return 1.0 / (1.0 + jnp.exp(-x))


NN = (((1,), (0,)), ((), ()))
NT = (((1,), (1,)), ((), ()))
TN = (((0,), (0,)), ((), ()))


ONE_BUFFER = pl.Buffered(1)


def _mm(name, grid, ins, in_specs, compute, out_shape, out_specs, epilogue, dep=None):
    if dep is not None:
        ins, in_specs = tuple(ins) + (dep,), list(in_specs) + [pl.BlockSpec(dep.shape, lambda *_: (0, 0))]
    n_in = len(ins)

    def body(*refs):
        epilogue(compute(refs[:n_in]), refs[:n_in], refs[n_in:])

    return pl.pallas_call(
        body, name=name, grid=grid, in_specs=in_specs, out_specs=out_specs, out_shape=out_shape,
        compiler_params=_cparams(("parallel", "arbitrary")),
    )(*ins)


def _dot(dims, a=0, b=1):
    return lambda refs: [lax.dot_general(refs[a][...], refs[b][...], dims, preferred_element_type=F32)]


def _ffn_up(name, h, wgu3):
    S, D = h.shape
    Ns = wgu3.shape[2]
    F = 2 * Ns
    tm, tn = _tile(S, 512), _tile(Ns, 1408)
    nbs = Ns // tn

    def compute(refs):
        hv = refs[0][...]
        return [jnp.dot(hv, refs[1][...], preferred_element_type=F32), jnp.dot(hv, refs[2][...], preferred_element_type=F32)]

    def epi(accs, in_refs, out_refs):
        g, u = accs
        gu_ref, a_ref = out_refs
        gu_ref[0] = g.astype(BF16)
        gu_ref[1] = u.astype(BF16)
        a_ref[...] = (g * _sigmoid(g) * u).astype(BF16)

    return _mm(
        name, (F // tn, S // tm), (h, wgu3, wgu3),
        [pl.BlockSpec((tm, D), lambda j, i: (i, 0)),
         pl.BlockSpec((None, D, tn), lambda j, i: (j // nbs, 0, j % nbs), pipeline_mode=ONE_BUFFER),
         pl.BlockSpec((None, D, tn), lambda j, i: (2 + j // nbs, 0, j % nbs), pipeline_mode=ONE_BUFFER)],
        compute, (jax.ShapeDtypeStruct((2, S, F), BF16), jax.ShapeDtypeStruct((S, F), BF16)),
        (pl.BlockSpec((2, tm, tn), lambda j, i: (0, i, j)), pl.BlockSpec((tm, tn), lambda j, i: (i, j))), epi)


def _mm_res(name, a, w, res, scale):
    S, K = a.shape
    N = w.shape[1]
    tm, tn = _tile(S, 512), _tile(N, 512 if K > 2816 else 1024)

    def epi(accs, in_refs, out_refs):
        out_refs[0][...] = in_refs[2][...] + scale * accs[0]

    return _mm(
        name, (N // tn, S // tm), (a, w, res),
        [pl.BlockSpec((tm, K), lambda j, i: (i, 0)), pl.BlockSpec((K, tn), lambda j, i: (0, j), pipeline_mode=ONE_BUFFER),
         pl.BlockSpec((tm, tn), lambda j, i: (i, j))],
        _dot(NN), jax.ShapeDtypeStruct((S, N), F32), pl.BlockSpec((tm, tn), lambda j, i: (i, j)), epi)


def _mm_cols(name, a, w3, out_dtype):
    S, K = a.shape
    Ns = w3.shape[2]
    tm, tn = _tile(S, 512), _tile(Ns, 2304)
    nbs = Ns // tn

    def epi(accs, in_refs, out_refs):
        out_refs[0][...] = accs[0].astype(out_dtype)

    return _mm(
        name, (N_CHIPS * nbs, S // tm), (a, w3),
        [pl.BlockSpec((tm, K), lambda j, i: (i, 0)),
         pl.BlockSpec((None, K, tn), lambda j, i: (j // nbs, 0, j % nbs), pipeline_mode=ONE_BUFFER)],
        _dot(NN), jax.ShapeDtypeStruct((S, N_CHIPS * Ns), out_dtype), pl.BlockSpec((tm, tn), lambda j, i: (i, j)), epi)


def _mm_nt(name, a, w, out_dtype, scale=1.0):
    S, N = a.shape
    K = w.shape[0]
    tm, tn = _tile(S, 512), _tile(K, 1024)

    def epi(accs, in_refs, out_refs):
        out_refs[0][...] = (scale * accs[0]).astype(out_dtype)

    return _mm(
        name, (K // tn, S // tm), (a, w),
        [pl.BlockSpec((tm, N), lambda j, i: (i, 0)), pl.BlockSpec((tn, N), lambda j, i: (j, 0), pipeline_mode=ONE_BUFFER)],
        _dot(NT), jax.ShapeDtypeStruct((S, K), out_dtype), pl.BlockSpec((tm, tn), lambda j, i: (i, j)), epi)


def _ffn_down_bwd(name, dy, wd, gu, scale, dep=None):
    S, D = dy.shape
    F = wd.shape[0]
    tm, tn = _tile(S, 512), _tile(F, 1408)

    def epi(accs, in_refs, out_refs):
        da = scale * accs[0]
        g = in_refs[2][0].astype(F32)
        u = in_refs[2][1].astype(F32)
        sg = _sigmoid(g)
        out_refs[0][0] = (da * u * (sg * (1.0 + g * (1.0 - sg)))).astype(BF16)
        out_refs[0][1] = (da * (g * sg)).astype(BF16)

    return _mm(
        name, (F // tn, S // tm), (dy, wd, gu),
        [pl.BlockSpec((tm, D), lambda j, i: (i, 0)), pl.BlockSpec((tn, D), lambda j, i: (j, 0), pipeline_mode=ONE_BUFFER),
         pl.BlockSpec((2, tm, tn), lambda j, i: (0, i, j))],
        _dot(NT), jax.ShapeDtypeStruct((2, S, F), BF16), pl.BlockSpec((2, tm, tn), lambda j, i: (0, i, j)), epi, dep=dep)


def _mm_nt_cols(name, a, w3, a_is_gu=False, dep=None):
    K, Ns = w3.shape[1], w3.shape[2]
    S = a.shape[1] if a_is_gu else a.shape[0]
    tm = _tile(S, 512)
    tn = _tile(K, max(LANES, (6 << 20) // (N_CHIPS * Ns * 2)))
    if a_is_gu:
        a_spec = pl.BlockSpec((2, tm, 2 * Ns), lambda i, j: (0, i, 0), pipeline_mode=ONE_BUFFER)
        part = lambda a_ref, s: a_ref[s // 2, :, (s % 2) * Ns:(s % 2 + 1) * Ns]
    else:
        a_spec = pl.BlockSpec((tm, N_CHIPS * Ns), lambda i, j: (i, 0), pipeline_mode=ONE_BUFFER)
        part = lambda a_ref, s: a_ref[:, s * Ns:(s + 1) * Ns]

    def compute(refs):
        total = None
        for s in range(N_CHIPS):
            prod = lax.dot_general(part(refs[0], s), refs[1][s], NT, preferred_element_type=F32)
            total = prod if total is None else total + prod
        return [total]

    def epi(accs, in_refs, out_refs):
        out_refs[0][...] = accs[0]

    return _mm(
        name, (S // tm, K // tn), (a, w3), [a_spec, pl.BlockSpec((N_CHIPS, tn, Ns), lambda i, j: (0, j, 0))],
        compute, jax.ShapeDtypeStruct((S, K), F32), pl.BlockSpec((tm, tn), lambda i, j: (i, j)), epi, dep=dep)


def _mm_tn(name, a, b, scale=1.0, dep=None):
    S, K = a.shape
    N = b.shape[1]
    tm, tn = _tile(K, 512), _tile(N, 1024)

    def epi(accs, in_refs, out_refs):
        out_refs[0][...] = (scale * accs[0]).astype(BF16)

    return _mm(
        name, (N // tn, K // tm), (a, b),
        [pl.BlockSpec((S, tm), lambda j, i: (0, i)), pl.BlockSpec((S, tn), lambda j, i: (0, j), pipeline_mode=ONE_BUFFER)],
        _dot(TN), jax.ShapeDtypeStruct((K, N), BF16), pl.BlockSpec((tm, tn), lambda j, i: (i, j)), epi, dep=dep)


def _mm_tn_cols(name, a, b, Ns, b_is_gu=False, dep=None):
    S, K = a.shape
    tm, tn = _tile(K, 512), _tile(Ns, 2304)
    nbs = Ns // tn
    if b_is_gu:
        b_spec = pl.BlockSpec((None, S, tn), lambda j, i: (j // (2 * nbs), 0, j % (2 * nbs)), pipeline_mode=ONE_BUFFER)
    else:
        b_spec = pl.BlockSpec((S, tn), lambda j, i: (0, j), pipeline_mode=ONE_BUFFER)

    def epi(accs, in_refs, out_refs):
        out_refs[0][...] = accs[0].astype(BF16)

    return _mm(
        name, (N_CHIPS * nbs, K // tm), (a, b), [pl.BlockSpec((S, tm), lambda j, i: (0, i)), b_spec],
        _dot(TN), jax.ShapeDtypeStruct((N_CHIPS, K, Ns), BF16),
        pl.BlockSpec((None, tm, tn), lambda j, i: (j // nbs, i, j % nbs)), epi, dep=dep)


def _rms_fwd(name, x, gain, dep=None):
    S, D = x.shape
    tm = _tile(S, 256, 8)
    extra = () if dep is None else (dep,)

    def body(x_ref, g_ref, *rest):
        h_ref = rest[-1]
        xv = x_ref[...]
        r = lax.rsqrt(jnp.mean(xv * xv, axis=-1, keepdims=True) + RMS_EPS)
        h_ref[...] = (xv * r * g_ref[...]).astype(BF16)

    return pl.pallas_call(
        body, name=name, grid=(S // tm,),
        in_specs=[pl.BlockSpec((tm, D), lambda i: (i, 0)), pl.BlockSpec((1, D), lambda i: (0, 0))]
        + [pl.BlockSpec(d.shape, lambda i: (0, 0)) for d in extra],
        out_specs=pl.BlockSpec((tm, D), lambda i: (i, 0)), out_shape=jax.ShapeDtypeStruct((S, D), BF16),
        compiler_params=_cparams(("parallel",)),
    )(x, gain, *extra)


def _rms_bwd(name, x, gain, dh, dres):
    S, D = x.shape
    tm = _tile(S, 256, 8)

    def body(x_ref, g_ref, dh_ref, dres_ref, dx_ref, dg_ref):
        i = pl.program_id(0)
        xv = x_ref[...]
        r = lax.rsqrt(jnp.mean(xv * xv, axis=-1, keepdims=True) + RMS_EPS)
        xhat = xv * r
        dhv = dh_ref[...]
        dxhat = dhv * g_ref[...]
        dx_ref[...] = dres_ref[...] + r * (dxhat - xhat * jnp.mean(dxhat * xhat, axis=-1, keepdims=True))

        @pl.when(i == 0)
        def _():
            dg_ref[...] = jnp.zeros_like(dg_ref)

        dg_ref[...] += jnp.sum(dhv * xhat, axis=0, keepdims=True)

    row = pl.BlockSpec((tm, D), lambda i: (i, 0))
    vec = pl.BlockSpec((1, D), lambda i: (0, 0))
    return pl.pallas_call(
        body, name=name, grid=(S // tm,), in_specs=[row, vec, row, row], out_specs=(row, vec),
        out_shape=(jax.ShapeDtypeStruct((S, D), F32), jax.ShapeDtypeStruct((1, D), F32)),
        compiler_params=_cparams(("arbitrary",)),
    )(x, gain, dh, dres)


def _loss_grad(name, y, target):
    S, D = y.shape
    tm = _tile(S, 256, 8)

    def body(y_ref, t_ref, dy_ref, l_ref):
        i = pl.program_id(0)
        e = y_ref[...] - t_ref[...]
        dy_ref[...] = e * (1.0 / D)
        col = jnp.sum(e * e, axis=0, keepdims=True)
        part = col[:, 0:LANES]
        for k in range(1, D // LANES):
            part = part + col[:, k * LANES:(k + 1) * LANES]

        @pl.when(i == 0)
        def _():
            l_ref[...] = jnp.zeros_like(l_ref)

        l_ref[...] += part

    row = pl.BlockSpec((tm, D), lambda i: (i, 0))
    return pl.pallas_call(
        body, name=name, grid=(S // tm,), in_specs=[row, row],
        out_specs=(row, pl.BlockSpec((1, LANES), lambda i: (0, 0))),
        out_shape=(jax.ShapeDtypeStruct((S, D), F32), jax.ShapeDtypeStruct((1, LANES), F32)),
        compiler_params=_cparams(("arbitrary",)),
    )(y, target)


def _shift_down(u, k):
    rows = lax.broadcasted_iota(jnp.int32, u.shape, 0)
    return jnp.where(rows >= k, pltpu.roll(u, k, 0), 0.0)


def _shift_up(u, k):
    n = u.shape[0]
    rows = lax.broadcasted_iota(jnp.int32, u.shape, 0)
    return jnp.where(rows < n - k, pltpu.roll(u, n - k, 0), 0.0)


def _conv_specs(S, cw, conv_width):
    nb = conv_width // cw
    col = lambda off: pl.BlockSpec((S, cw), lambda j, off=off: (0, off * nb + j))
    return nb, col(0), col(1), col(2)


def _conv_fwd(name, proj, convw3, conv_width):
    S = proj.shape[0]
    cw = convw3.shape[2]
    nb, xc_s, bg_s, cg_s = _conv_specs(S, cw, conv_width)

    def body(xc_ref, bg_ref, cg_ref, w_ref, o_ref):
        u = cg_ref[...] * xc_ref[...]
        w = w_ref[...]
        cv = w[2:3, :] * u + w[1:2, :] * _shift_down(u, 1) + w[0:1, :] * _shift_down(u, 2)
        o_ref[...] = (bg_ref[...] * cv).astype(BF16)

    return pl.pallas_call(
        body, name=name, grid=(nb,),
        in_specs=[xc_s, bg_s, cg_s, pl.BlockSpec((None, CONV_K, cw), lambda j: (j, 0, 0))],
        out_specs=pl.BlockSpec((S, cw), lambda j: (0, j)), out_shape=jax.ShapeDtypeStruct((S, conv_width), BF16),
        compiler_params=_cparams(("parallel",)),
    )(proj, proj, proj, convw3)


def _conv_bwd(name, proj, convw3, da, conv_width):
    S = proj.shape[0]
    cw = convw3.shape[2]
    nb, xc_s, bg_s, cg_s = _conv_specs(S, cw, conv_width)

    def body(xc_ref, bg_ref, cg_ref, w_ref, da_ref, dxc_ref, dbg_ref, dcg_ref, dw_ref):
        xc, cg = xc_ref[...], cg_ref[...]
        u = cg * xc
        w = w_ref[...]
        u1, u2 = _shift_down(u, 1), _shift_down(u, 2)
        cv = w[2:3, :] * u + w[1:2, :] * u1 + w[0:1, :] * u2
        dav = da_ref[...]
        dbg_ref[...] = (dav * cv).astype(BF16)
        dcv = dav * bg_ref[...]
        du = w[2:3, :] * dcv + w[1:2, :] * _shift_up(dcv, 1) + w[0:1, :] * _shift_up(dcv, 2)
        dxc_ref[...] = (du * cg).astype(BF16)
        dcg_ref[...] = (du * xc).astype(BF16)
        dw_ref[0:1, :] = jnp.sum(dcv * u2, axis=0, keepdims=True)
        dw_ref[1:2, :] = jnp.sum(dcv * u1, axis=0, keepdims=True)
        dw_ref[2:3, :] = jnp.sum(dcv * u, axis=0, keepdims=True)

    wspec = pl.BlockSpec((None, CONV_K, cw), lambda j: (j, 0, 0))
    ospec = pl.BlockSpec((S, cw), lambda j: (0, j))
    act = jax.ShapeDtypeStruct((S, conv_width), BF16)
    return pl.pallas_call(
        body, name=name, grid=(nb,), in_specs=[xc_s, bg_s, cg_s, wspec, ospec],
        out_specs=(ospec, ospec, ospec, wspec),
        out_shape=(act, act, act, jax.ShapeDtypeStruct(convw3.shape, F32)),
        compiler_params=_cparams(("parallel",)),
    )(proj, proj, proj, convw3, da)


def _rope_consts(S, dh):
    rot = dh // 4
    half = rot // 2
    inv_freq = 1.0 / (ROPE_THETA ** (jnp.arange(0, rot, 2, dtype=F32) / rot))
    ang = jnp.arange(S, dtype=F32)[:, None] * inv_freq[None, :]
    cos = jnp.concatenate([jnp.cos(ang), jnp.cos(ang), jnp.ones((S, dh - rot), F32)], axis=1)
    sin = jnp.concatenate([jnp.sin(ang), jnp.sin(ang), jnp.zeros((S, dh - rot), F32)], axis=1)
    rm = np.zeros((dh, dh), np.float32)
    for j in range(half):
        rm[j + half, j] = -1.0
        rm[j, j + half] = 1.0
    return cos, sin, jnp.asarray(rm, BF16), jnp.asarray(rm.T, BF16)


def _exact_perm(y, rm):
    hi = y.astype(BF16)
    r1 = y - hi.astype(F32)
    mid = r1.astype(BF16)
    lo = (r1 - mid.astype(F32)).astype(BF16)
    dot = lambda a: jnp.dot(a, rm, preferred_element_type=F32)
    return dot(hi) + dot(mid) + dot(lo)


def _qk_prep(name, xh, gain, cos, sin, rm):
    H, S, dh = xh.shape
    tm = _tile(S, 1024, 8)

    def body(x_ref, g_ref, c_ref, s_ref, rm_ref, o_ref):
        xv = x_ref[...]
        y = xv * lax.rsqrt(jnp.mean(xv * xv, axis=-1, keepdims=True) + RMS_EPS) * g_ref[...]
        o_ref[...] = (y * c_ref[...] + _exact_perm(y, rm_ref[...]) * s_ref[...]).astype(BF16)

    blk = pl.BlockSpec((None, tm, dh), lambda h, i: (h, i, 0))
    tab = pl.BlockSpec((tm, dh), lambda h, i: (i, 0))
    return pl.pallas_call(
        body, name=name, grid=(H, S // tm),
        in_specs=[blk, pl.BlockSpec((1, dh), lambda h, i: (0, 0)), tab, tab, pl.BlockSpec((dh, dh), lambda h, i: (0, 0))],
        out_specs=blk, out_shape=jax.ShapeDtypeStruct((H, S, dh), BF16),
        compiler_params=_cparams(("parallel", "parallel")),
    )(xh, gain, cos, sin, rm)


def _qk_prep_bwd(name, xh, gain, cos, sin, rmt, dout):
    H, S, dh = xh.shape
    tm = _tile(S, 1024, 8)

    def body(x_ref, g_ref, c_ref, s_ref, rmt_ref, do_ref, dx_ref, dg_ref):
        first = (pl.program_id(0) == 0) & (pl.program_id(1) == 0)
        xv = x_ref[...]
        r = lax.rsqrt(jnp.mean(xv * xv, axis=-1, keepdims=True) + RMS_EPS)
        xhat = xv * r
        dov = do_ref[...]
        dy = dov * c_ref[...] + _exact_perm(dov * s_ref[...], rmt_ref[...])
        dxhat = dy * g_ref[...]
        dx_ref[...] = (r * (dxhat - xhat * jnp.mean(dxhat * xhat, axis=-1, keepdims=True))).astype(BF16)

        @pl.when(first)
        def _():
            dg_ref[...] = jnp.zeros_like(dg_ref)

        dg_ref[...] += jnp.sum(dy * xhat, axis=0, keepdims=True)

    blk = pl.BlockSpec((None, tm, dh), lambda h, i: (h, i, 0))
    tab = pl.BlockSpec((tm, dh), lambda h, i: (i, 0))
    vec = pl.BlockSpec((1, dh), lambda h, i: (0, 0))
    return pl.pallas_call(
        body, name=name, grid=(H, S // tm),
        in_specs=[blk, vec, tab, tab, pl.BlockSpec((dh, dh), lambda h, i: (0, 0)), blk],
        out_specs=(blk, vec), out_shape=(jax.ShapeDtypeStruct((H, S, dh), BF16), jax.ShapeDtypeStruct((1, dh), F32)),
        compiler_params=_cparams(("arbitrary", "arbitrary")),
    )(xh, gain, cos, sin, rmt, dout)


def _attn_probs(q, kp, kc, sink_col, n, scale):
    rows = q.shape[0]
    sp = lax.dot_general(q, kp, NT, preferred_element_type=F32) * scale
    sc = lax.dot_general(q, kc, NT, preferred_element_type=F32) * scale
    qi = lax.broadcasted_iota(jnp.int32, (rows, BLOCK), 0) % BLOCK
    kj = lax.broadcasted_iota(jnp.int32, (rows, BLOCK), 1)
    sp = jnp.where((kj > qi) & (n > 0), sp, NEG_INF)
    sc = jnp.where(kj <= qi, sc, NEG_INF)
    m = jnp.maximum(jnp.maximum(jnp.max(sp, axis=-1, keepdims=True), jnp.max(sc, axis=-1, keepdims=True)), sink_col)
    pp, pc, ps = jnp.exp(sp - m), jnp.exp(sc - m), jnp.exp(sink_col - m)
    inv = 1.0 / (jnp.sum(pp, axis=-1, keepdims=True) + jnp.sum(pc, axis=-1, keepdims=True) + ps)
    return pp * inv, pc * inv, ps * inv


def _sink_col(sink_ref, hk, group):
    rows = group * BLOCK
    g = lax.broadcasted_iota(jnp.int32, (rows, 1), 0) // BLOCK
    col = jnp.zeros((rows, 1), F32)
    for i in range(group):
        col = jnp.where(g == i, sink_ref[hk * group + i], col)
    return col


def _attn_specs(group, dh):
    qb = pl.BlockSpec((group, BLOCK, dh), lambda hk, n: (hk, n, 0))
    prev = pl.BlockSpec((None, BLOCK, dh), lambda hk, n: (hk, jnp.maximum(n - 1, 0), 0))
    cur = pl.BlockSpec((None, BLOCK, dh), lambda hk, n: (hk, n, 0))
    return qb, prev, cur, pl.BlockSpec(memory_space=pltpu.SMEM)


def _attn_fwd(name, q, k, v, sinks):
    HQ, S, dh = q.shape
    HKV = k.shape[0]
    group = HQ // HKV
    scale = dh ** -0.5
    qb, prev, cur, smem = _attn_specs(group, dh)

    def body(q_ref, kp_ref, kc_ref, vp_ref, vc_ref, sink_ref, o_ref):
        hk, n = pl.program_id(0), pl.program_id(1)
        qv = q_ref[...].reshape(group * BLOCK, dh)
        pp, pc, _ = _attn_probs(qv, kp_ref[...], kc_ref[...], _sink_col(sink_ref, hk, group), n, scale)
        o = jnp.dot(pp.astype(BF16), vp_ref[...], preferred_element_type=F32)
        o = o + jnp.dot(pc.astype(BF16), vc_ref[...], preferred_element_type=F32)
        o_ref[...] = o.reshape(group, BLOCK, dh).astype(BF16)

    return pl.pallas_call(
        body, name=name, grid=(HKV, S // BLOCK), in_specs=[qb, prev, cur, prev, cur, smem], out_specs=qb,
        out_shape=jax.ShapeDtypeStruct((HQ, S, dh), BF16), compiler_params=_cparams(("parallel", "parallel")),
    )(q, k, k, v, v, sinks)


def _attn_bwd(name, q, k, v, sinks, do):
    HQ, S, dh = q.shape
    HKV = k.shape[0]
    group = HQ // HKV
    scale = dh ** -0.5
    qb, prev, cur, smem = _attn_specs(group, dh)
    whole = pl.BlockSpec((None, S, dh), lambda hk, n: (hk, 0, 0))
    sk = pl.BlockSpec((None, group, LANES), lambda hk, n: (hk, 0, 0))

    def body(q_ref, kp_ref, kc_ref, vp_ref, vc_ref, sink_ref, do_ref, dq_ref, dk_ref, dv_ref, ds_ref):
        hk, n = pl.program_id(0), pl.program_id(1)
        rows = group * BLOCK
        qv = q_ref[...].reshape(rows, dh)
        dov = do_ref[...].reshape(rows, dh)
        kp, kc, vp, vc = kp_ref[...], kc_ref[...], vp_ref[...], vc_ref[...]
        pp, pc, ps = _attn_probs(qv, kp, kc, _sink_col(sink_ref, hk, group), n, scale)
        dpp = lax.dot_general(dov, vp, NT, preferred_element_type=F32)
        dpc = lax.dot_general(dov, vc, NT, preferred_element_type=F32)
        delta = jnp.sum(pp * dpp, axis=-1, keepdims=True) + jnp.sum(pc * dpc, axis=-1, keepdims=True)
        dsp = (pp * (dpp - delta) * scale).astype(BF16)
        dsc = (pc * (dpc - delta) * scale).astype(BF16)
        dq = jnp.dot(dsp, kp, preferred_element_type=F32) + jnp.dot(dsc, kc, preferred_element_type=F32)
        dq_ref[...] = dq.reshape(group, BLOCK, dh)

        @pl.when(n == 0)
        def _():
            dk_ref[...] = jnp.zeros_like(dk_ref)
            dv_ref[...] = jnp.zeros_like(dv_ref)
            ds_ref[...] = jnp.zeros_like(ds_ref)

        cur_rows = pl.ds(pl.multiple_of(n * BLOCK, BLOCK), BLOCK)
        prev_rows = pl.ds(pl.multiple_of(jnp.maximum(n - 1, 0) * BLOCK, BLOCK), BLOCK)
        tdot = lambda a, b: lax.dot_general(a, b, TN, preferred_element_type=F32)
        dk_ref[prev_rows, :] += tdot(dsp, qv)
        dv_ref[prev_rows, :] += tdot(pp.astype(BF16), dov)
        dk_ref[cur_rows, :] += tdot(dsc, qv)
        dv_ref[cur_rows, :] += tdot(pc.astype(BF16), dov)
        dsink = -jnp.sum((ps * delta).reshape(group, BLOCK, 1), axis=1)
        ds_ref[...] += jnp.broadcast_to(dsink, (group, LANES))

    return pl.pallas_call(
        body, name=name, grid=(HKV, S // BLOCK), in_specs=[qb, prev, cur, prev, cur, smem, qb],
        out_specs=(qb, whole, whole, sk),
        out_shape=(jax.ShapeDtypeStruct((HQ, S, dh), F32), jax.ShapeDtypeStruct((HKV, S, dh), F32),
                   jax.ShapeDtypeStruct((HKV, S, dh), F32), jax.ShapeDtypeStruct((HKV, group, LANES), F32)),
        compiler_params=_cparams(("arbitrary", "arbitrary")),
    )(q, k, k, v, v, sinks, do)


def _gate_specs(S, D, ga_off, gb_off):
    tg = LANES
    for t in range(LANES, 513, LANES):
        if D % t == 0 and ga_off % t == 0 and gb_off % t == 0:
            tg = t
    if D % LANES:
        tg = math.gcd(math.gcd(D, ga_off), gb_off)
    tm = _tile(S, 512, 8)
    act = pl.BlockSpec((tm, tg), lambda i, j: (i, j))
    ga = pl.BlockSpec((tm, tg), lambda i, j: (i, ga_off // tg + j))
    gb = pl.BlockSpec((tm, tg), lambda i, j: (i, gb_off // tg + j))
    return (S // tm, D // tg), act, ga, gb


def _gate_fwd(name, proj, ya, yb, ga_off, gb_off):
    S, D = ya.shape
    grid, act, ga, gb = _gate_specs(S, D, ga_off, gb_off)

    def body(ga_ref, gb_ref, ya_ref, yb_ref, o_ref):
        o_ref[...] = (_sigmoid(ga_ref[...]) * ya_ref[...] + _sigmoid(gb_ref[...]) * yb_ref[...]).astype(BF16)

    return pl.pallas_call(
        body, name=name, grid=grid, in_specs=[ga, gb, act, act], out_specs=act,
        out_shape=jax.ShapeDtypeStruct((S, D), BF16), compiler_params=_cparams(("parallel", "parallel")),
    )(proj, proj, ya, yb)


def _gate_bwd(name, proj, ya, yb, dm, ga_off, gb_off):
    S, D = ya.shape
    grid, act, ga, gb = _gate_specs(S, D, ga_off, gb_off)

    def body(ga_ref, gb_ref, ya_ref, yb_ref, dm_ref, dga_ref, dgb_ref, dya_ref, dyb_ref):
        dmv = dm_ref[...]
        sa, sb = _sigmoid(ga_ref[...]), _sigmoid(gb_ref[...])
        dga_ref[...] = (dmv * ya_ref[...] * sa * (1.0 - sa)).astype(BF16)
        dgb_ref[...] = (dmv * yb_ref[...] * sb * (1.0 - sb)).astype(BF16)
        dya_ref[...] = (dmv * sa).astype(BF16)
        dyb_ref[...] = (dmv * sb).astype(BF16)

    o = jax.ShapeDtypeStruct((S, D), BF16)
    return pl.pallas_call(
        body, name=name, grid=grid, in_specs=[ga, gb, act, act, act], out_specs=(act, act, act, act),
        out_shape=(o, o, o, o), compiler_params=_cparams(("parallel", "parallel")),
    )(proj, proj, ya, yb, dm)


def _row_tile(rows, cols, n_arrays):
    want = max(16, (VMEM_LIMIT_V7X // 2) // (2 * n_arrays * cols * 4))
    return _tile(rows, want, 16)


def _cast_to_slot(name, w, dtype, p_arr, dep=None):
    R, C = w.shape
    tr = _row_tile(R, C, 2)
    extra = () if dep is None else (dep,)

    def body(p_ref, w_ref, *rest):
        rest[-1][...] = w_ref[...].astype(dtype)

    return pl.pallas_call(
        body, name=name,
        grid_spec=pltpu.PrefetchScalarGridSpec(
            num_scalar_prefetch=1, grid=(R // tr,),
            in_specs=[pl.BlockSpec((tr, C), lambda i, p_ref: (i, 0))] + [pl.BlockSpec(d.shape, lambda i, p_ref: (0, 0)) for d in extra],
            out_specs=pl.BlockSpec((None, tr, C), lambda i, p_ref: (p_ref[0], i, 0))),
        out_shape=jax.ShapeDtypeStruct((N_CHIPS, R, C), dtype), compiler_params=_cparams(("parallel",)),
    )(p_arr, w, *extra)


def _add_half(name, g3, r3, c_arr):
    n, h, C = r3.shape
    tr = _row_tile(h, C, 3)
    nb = h // tr

    def body(c_ref, g_ref, r_ref, o_ref):
        o_ref[...] = (g_ref[...].astype(F32) + r_ref[...].astype(F32)).astype(BF16)

    blk = pl.BlockSpec((None, tr, C), lambda s, i, c_ref: (s, i, 0))
    return pl.pallas_call(
        body, name=name,
        grid_spec=pltpu.PrefetchScalarGridSpec(
            num_scalar_prefetch=1, grid=(n, nb),
            in_specs=[pl.BlockSpec((None, tr, C), lambda s, i, c_ref: (s, c_ref[0] * nb + i, 0)), blk], out_specs=blk),
        out_shape=jax.ShapeDtypeStruct(r3.shape, BF16), compiler_params=_cparams(("parallel", "parallel")),
    )(c_arr, g3, r3)


def _add_chips(name, t3, r3, cp_arr):
    n, h, C = r3.shape
    tr = _row_tile(h, C, 6)
    nb = h // tr

    def body(cp_ref, t_ref, r0_ref, r1_ref, r2_ref, r3_ref, o_ref):
        p = cp_ref[1]
        total = None
        for a, r_ref in enumerate((r0_ref, r1_ref, r2_ref, r3_ref)):
            part = jnp.where(p == a, t_ref[...], r_ref[...]).astype(F32)
            total = part if total is None else total + part
        o_ref[...] = total

    def part(a):
        return pl.BlockSpec((None, tr, C), lambda i, cp_ref: (jnp.where(cp_ref[1] == a, (a + 1) % N_CHIPS, a), i, 0))

    return pl.pallas_call(
        body, name=name,
        grid_spec=pltpu.PrefetchScalarGridSpec(
            num_scalar_prefetch=1, grid=(nb,),
            in_specs=[pl.BlockSpec((None, tr, C), lambda i, cp_ref: (cp_ref[1], i, 0)), part(0), part(1), part(2), part(3)],
            out_specs=pl.BlockSpec((tr, C), lambda i, cp_ref: (cp_ref[0] * nb + i, 0))),
        out_shape=jax.ShapeDtypeStruct((2 * h, C), F32), compiler_params=_cparams(("parallel",)),
    )(cp_arr, t3, r3, r3, r3, r3)


def _adamw(name, w, g, m, v):
    R, C = w.shape
    tr = _row_tile(R, C, 7)
    c1 = 1.0 - ADAM_B1 ** ADAM_STEP
    c2 = 1.0 - ADAM_B2 ** ADAM_STEP

    def body(w_ref, g_ref, m_ref, v_ref, d_ref, nm_ref, nv_ref):
        gv = g_ref[...]
        nm = ADAM_B1 * m_ref[...] + (1.0 - ADAM_B1) * gv
        nv = ADAM_B2 * v_ref[...] + (1.0 - ADAM_B2) * (gv * gv)
        d_ref[...] = -ADAM_LR * ((nm / c1) / (jnp.sqrt(nv / c2) + ADAM_EPS) + ADAM_WD * w_ref[...])
        nm_ref[...] = nm
        nv_ref[...] = nv

    blk = pl.BlockSpec((tr, C), lambda i: (i, 0))
    o = jax.ShapeDtypeStruct((R, C), F32)
    return pl.pallas_call(
        body, name=name, grid=(R // tr,), in_specs=[blk, blk, blk, blk], out_specs=(blk, blk, blk),
        out_shape=(o, o, o), compiler_params=_cparams(("parallel",)),
    )(w, g, m, v)


def _place():
    x, y, c = lax.axis_index("x"), lax.axis_index("y"), lax.axis_index("c")
    chips = [(1 - x, y), (x, 1 - y), (1 - x, 1 - y)]
    return x, y, c, 2 * x + y, chips


ANY = pl.BlockSpec(memory_space=pl.ANY)


HBM = pl.BlockSpec(memory_space=pltpu.HBM)
SEM = pl.BlockSpec(memory_space=pltpu.SEMAPHORE)
TOKEN = jax.ShapeDtypeStruct((8, LANES), F32)
DATAFLOW = pltpu.SideEffectType.DATAFLOW_SIDE_EFFECTING


def _hbm(a):
    return pltpu.with_memory_space_constraint(a, pltpu.HBM)


def _gather_blocks(bufs, i, c, p, chips):
    if bufs[i].shape[1] % 16:
        return bufs[i].at[p], [bufs[i].at[2 * cx + cy] for cx, cy in chips]
    h = bufs[i].shape[1] // 2
    rows = pl.ds(pl.multiple_of(c * h, 16), h)
    return bufs[i].at[p, rows], [bufs[i].at[2 * cx + cy, rows] for cx, cy in chips]


def _gather_start(name, slots, dep):
    n = len(slots)

    def body(*refs):
        bufs, send, recv, token = refs[:n], refs[n + 1], refs[n + 2], refs[-1]
        x, y, c, p, chips = _place()
        for i in range(n):
            mine, _ = _gather_blocks(bufs, i, c, p, chips)
            for j, chip in enumerate(chips):
                pltpu.make_async_remote_copy(src_ref=mine, dst_ref=mine, send_sem=send.at[3 * i + j], recv_sem=recv.at[3 * i + j],
                                             device_id=(*chip, c), device_id_type=MESH).start()
        token[...] = jnp.zeros_like(token)

    out = pl.pallas_call(
        body, name=name, in_specs=[HBM] * n + [ANY],
        out_specs=(SEM, SEM, *([HBM] * n), pl.BlockSpec(memory_space=pltpu.VMEM)),
        out_shape=(pltpu.SemaphoreType.DMA((3 * n,)), pltpu.SemaphoreType.DMA((3 * n,)),
                   *[pltpu.HBM(s.shape, s.dtype) for s in slots], TOKEN),
        input_output_aliases={i: 2 + i for i in range(n)},
        compiler_params=pltpu.CompilerParams(has_side_effects=DATAFLOW),
    )(*[_hbm(s) for s in slots], dep)
    return out[0], out[1], list(out[2:2 + n]), out[-1]


def _gather_wait(name, send, recv, slots, after):
    n = len(slots)

    def body(*refs):
        bufs, send, recv = refs[:n], refs[n], refs[n + 1]
        x, y, c, p, chips = _place()
        for i in range(n):
            mine, landed = _gather_blocks(bufs, i, c, p, chips)
            for j, chip in enumerate(chips):
                cp = pltpu.make_async_remote_copy(src_ref=mine, dst_ref=landed[j], send_sem=send.at[3 * i + j],
                                                  recv_sem=recv.at[3 * i + j], device_id=(*chip, c), device_id_type=MESH)
                cp.wait_send()
                cp.wait_recv()

    return list(pl.pallas_call(
        body, name=name, in_specs=[HBM] * n + [SEM, SEM, ANY], out_specs=tuple([HBM] * n),
        out_shape=tuple(pltpu.HBM(s.shape, s.dtype) for s in slots),
        input_output_aliases={i: i for i in range(n)},
        compiler_params=pltpu.CompilerParams(has_side_effects=DATAFLOW),
    )(*slots, send, recv, after))


def _gather_forward(name, slots):
    idx = [i for i, s in enumerate(slots) if s.shape[1] % 16 == 0]
    n = len(slots)

    def body(*refs):
        bufs = refs[n:2 * n]
        send, recv = refs[2 * n:]
        x, y, c, p, chips = _place()

        def rdma(k, ref):
            return pltpu.make_async_remote_copy(src_ref=ref, dst_ref=ref, send_sem=send.at[k], recv_sem=recv.at[k],
                                                device_id=(x, y, 1 - c), device_id_type=MESH)

        cps = []
        for k, i in enumerate(idx):
            for j, ref in enumerate(_gather_blocks(bufs, i, c, p, chips)[1]):
                cps.append(rdma(3 * k + j, ref))
                cps[-1].start()
        for k, i in enumerate(idx):
            for j, ref in enumerate(_gather_blocks(bufs, i, 1 - c, p, chips)[1]):
                rdma(3 * k + j, ref).wait_recv()
        for cp in cps:
            cp.wait_send()

    return list(pl.pallas_call(
        body, name=name, in_specs=[ANY] * n, out_specs=tuple([ANY] * n),
        out_shape=tuple(jax.ShapeDtypeStruct(s.shape, s.dtype) for s in slots),
        scratch_shapes=[pltpu.SemaphoreType.DMA((3 * len(idx),)), pltpu.SemaphoreType.DMA((3 * len(idx),))],
        input_output_aliases={i: i for i in range(n)},
        compiler_params=pltpu.CompilerParams(has_side_effects=True),
    )(*slots))


def _swap_copy(grads, lands, send, recv, i, x, y, c):
    h = grads[i].shape[1] // 2
    other = pl.ds(pl.multiple_of((1 - c) * h, 16), h)
    return pltpu.make_async_remote_copy(src_ref=grads[i].at[:, other, :], dst_ref=lands[i], send_sem=send.at[i],
                                        recv_sem=recv.at[i], device_id=(x, y, 1 - c), device_id_type=MESH)


def _swap_start(name, grads):
    n = len(grads)

    def body(*refs):
        ins, lands, send, recv, token = refs[:n], refs[n:2 * n], refs[2 * n], refs[2 * n + 1], refs[-1]
        x, y, c, p, chips = _place()
        for i in range(n):
            _swap_copy(ins, lands, send, recv, i, x, y, c).start()
        token[...] = jnp.zeros_like(token)

    gshapes = [pltpu.HBM(g.shape, g.dtype) for g in grads]
    halves = [(g.shape[0], g.shape[1] // 2, g.shape[2]) for g in grads]
    lshapes = [pltpu.HBM(s, g.dtype) for s, g in zip(halves, grads)]
    out = pl.pallas_call(
        body, name=name, in_specs=[HBM] * (2 * n),
        out_specs=(SEM, SEM, *([HBM] * (2 * n)), pl.BlockSpec(memory_space=pltpu.VMEM)),
        out_shape=(pltpu.SemaphoreType.DMA((n,)), pltpu.SemaphoreType.DMA((n,)), *gshapes, *lshapes, TOKEN),
        input_output_aliases={i: 2 + i for i in range(2 * n)},
        compiler_params=pltpu.CompilerParams(has_side_effects=DATAFLOW),
    )(*[_hbm(g) for g in grads], *[_hbm(lax.empty(s, g.dtype)) for s, g in zip(halves, grads)])
    return out[0], out[1], list(out[2:2 + n]), list(out[2 + n:2 + 2 * n]), out[-1]


def _swap_wait(name, send, recv, grads, lands, after):
    n = len(grads)

    def body(*refs):
        ins, lands, send, recv = refs[:n], refs[n:2 * n], refs[2 * n], refs[2 * n + 1]
        x, y, c, p, chips = _place()
        for i in range(n):
            cp = _swap_copy(ins, lands, send, recv, i, x, y, c)
            cp.wait_send()
            cp.wait_recv()

    shapes = [pltpu.HBM(t.shape, t.dtype) for t in list(grads) + list(lands)]
    out = pl.pallas_call(
        body, name=name, in_specs=[HBM] * (2 * n) + [SEM, SEM, ANY], out_specs=tuple([HBM] * (2 * n)),
        out_shape=tuple(shapes), input_output_aliases={i: i for i in range(2 * n)},
        compiler_params=pltpu.CompilerParams(has_side_effects=DATAFLOW),
    )(*grads, *lands, send, recv, after)
    return list(out[:n]), list(out[n:])


def _exchange_start(name, parts):
    n = len(parts)

    def body(*refs):
        ins, lands, send, recv, token = refs[:n], refs[n:2 * n], refs[2 * n], refs[2 * n + 1], refs[-1]
        x, y, c, p, chips = _place()
        for i in range(n):
            for j, (cx, cy) in enumerate(chips):
                pltpu.make_async_remote_copy(src_ref=ins[i].at[2 * cx + cy], dst_ref=lands[i].at[p], send_sem=send.at[3 * i + j],
                                             recv_sem=recv.at[3 * i + j], device_id=(cx, cy, c), device_id_type=MESH).start()
        token[...] = jnp.zeros_like(token)

    shapes = [pltpu.HBM(t.shape, t.dtype) for t in parts]
    out = pl.pallas_call(
        body, name=name, in_specs=[HBM] * (2 * n),
        out_specs=(SEM, SEM, *([HBM] * (2 * n)), pl.BlockSpec(memory_space=pltpu.VMEM)),
        out_shape=(pltpu.SemaphoreType.DMA((3 * n,)), pltpu.SemaphoreType.DMA((3 * n,)), *shapes, *shapes, TOKEN),
        input_output_aliases={i: 2 + i for i in range(2 * n)},
        compiler_params=pltpu.CompilerParams(has_side_effects=DATAFLOW),
    )(*[_hbm(t) for t in parts], *[_hbm(lax.empty(t.shape, t.dtype)) for t in parts])
    return out[0], out[1], list(out[2:2 + n]), list(out[2 + n:2 + 2 * n]), out[-1]


def _exchange_wait(name, send, recv, parts, lands, after):
    n = len(parts)

    def body(*refs):
        ins, lands, send, recv = refs[:n], refs[n:2 * n], refs[2 * n], refs[2 * n + 1]
        x, y, c, p, chips = _place()
        for i in range(n):
            for j, (cx, cy) in enumerate(chips):
                q = 2 * cx + cy
                cp = pltpu.make_async_remote_copy(src_ref=ins[i].at[q], dst_ref=lands[i].at[q], send_sem=send.at[3 * i + j],
                                                  recv_sem=recv.at[3 * i + j], device_id=(cx, cy, c), device_id_type=MESH)
                cp.wait_send()
                cp.wait_recv()

    shapes = [pltpu.HBM(t.shape, t.dtype) for t in parts]
    out = pl.pallas_call(
        body, name=name, in_specs=[HBM] * (2 * n) + [SEM, SEM, ANY], out_specs=tuple([HBM] * (2 * n)),
        out_shape=(*shapes, *shapes), input_output_aliases={i: i for i in range(2 * n)},
        compiler_params=pltpu.CompilerParams(has_side_effects=DATAFLOW),
    )(*parts, *lands, send, recv, after)
    return list(out[:n]), list(out[n:])


def _join_halves(name, bufs):
    n = len(bufs)

    def body(*refs):
        outs = refs[n:2 * n]
        send, recv = refs[2 * n:]
        x, y, c, p, chips = _place()

        def rdma(i, which):
            h = outs[i].shape[0] // 2
            rows = outs[i].at[pl.ds(pl.multiple_of(which * h, 8), h)]
            return pltpu.make_async_remote_copy(src_ref=rows, dst_ref=rows, send_sem=send.at[i], recv_sem=recv.at[i],
                                                device_id=(x, y, 1 - c), device_id_type=MESH)

        cps = [rdma(i, c) for i in range(n)]
        for cp in cps:
            cp.start()
        for i, cp in enumerate(cps):
            rdma(i, 1 - c).wait_recv()
            cp.wait_send()

    return pl.pallas_call(
        body, name=name, in_specs=[ANY] * n, out_specs=tuple([ANY] * n),
        out_shape=tuple(jax.ShapeDtypeStruct(t.shape, t.dtype) for t in bufs),
        scratch_shapes=[pltpu.SemaphoreType.DMA((n,)), pltpu.SemaphoreType.DMA((n,))],
        input_output_aliases={i: i for i in range(n)},
        compiler_params=pltpu.CompilerParams(has_side_effects=True),
    )(*bufs)


def _allreduce_small(name, pack):
    R, W = pack.shape

    def body(in_ref, out_ref, slots, send, recv):
        x, y, c = lax.axis_index("x"), lax.axis_index("y"), lax.axis_index("c")
        me = 4 * x + 2 * y + c
        slots[0] = in_ref[...]
        cps = []
        for k in range(1, N_DEV):
            peer = (x ^ (k >> 2), y ^ ((k >> 1) & 1), c ^ (k & 1))
            cp = pltpu.make_async_remote_copy(src_ref=in_ref, dst_ref=slots.at[k], send_sem=send.at[k - 1],
                                              recv_sem=recv.at[k - 1], device_id=peer, device_id_type=MESH)
            cp.start()
            cps.append(cp)
        for cp in cps:
            cp.wait()
        total = slots[me]
        for a in range(1, N_DEV):
            total = total + slots[jnp.bitwise_xor(a, me)]
        out_ref[...] = total

    vmem = pl.BlockSpec(memory_space=pltpu.VMEM)
    return pl.pallas_call(
        body, name=name, in_specs=[vmem], out_specs=vmem, out_shape=jax.ShapeDtypeStruct((R, W), F32),
        scratch_shapes=[pltpu.VMEM((N_DEV, R, W), F32), pltpu.SemaphoreType.DMA((N_DEV - 1,)), pltpu.SemaphoreType.DMA((N_DEV - 1,))],
        compiler_params=pltpu.CompilerParams(has_side_effects=True),
    )(pack)


def _heads(a, n_heads):
    S = a.shape[0]
    return a.reshape(S, n_heads, a.shape[1] // n_heads).transpose(1, 0, 2)


def _unheads(a):
    H, S, dh = a.shape
    return a.transpose(1, 0, 2).reshape(S, H * dh)


def _ffn_bwd(tag, xin, gain, wgu3, wd, saved, dxout, reduce_start, dep, flush=None):
    h, gu, act = saved
    D = xin.shape[1]
    dxo_b = dxout.astype(BF16)
    tok = reduce_start({f"w_down{tag}": _mm_tn(f"dw_down_{tag}", act, dxo_b, 0.5, dep=dep).reshape(N_CHIPS, -1, D)})
    dgu = _ffn_down_bwd(f"ffn_down_bwd_{tag}", dxo_b, wd, gu, 0.5, dep=tok)
    tok = reduce_start({f"w_gu{tag}": _mm_tn_cols(f"dw_gu_{tag}", h, dgu, wgu3.shape[2], b_is_gu=True)})
    if flush is not None:
        tok = flush(tok)
    dh = _mm_nt_cols(f"ffn_up_bwd_{tag}", dgu, wgu3, a_is_gu=True, dep=tok)
    dxin, dgain = _rms_bwd(f"rms_bwd_{tag}", xin, gain, dh, dxout)
    return dxin, dgain, tok


def kernel(x, g_ffn1, w_gu1, w_down1, g_mix, w_in, conv_w, q_norm_g, k_norm_g, sinks, w_out_conv, w_out_attn, w_o, g_ffn2, w_gu2, w_down2, loss_target, m_g_ffn1, m_w_gu1, m_w_down1, m_g_mix, m_w_in, m_conv_w, m_q_norm_g, m_k_norm_g, m_sinks, m_w_out_conv, m_w_out_attn, m_w_o, m_g_ffn2, m_w_gu2, m_w_down2, v_g_ffn1, v_w_gu1, v_w_down1, v_g_mix, v_w_in, v_conv_w, v_q_norm_g, v_k_norm_g, v_sinks, v_w_out_conv, v_w_out_attn, v_w_o, v_g_ffn2, v_w_gu2, v_w_down2):
    S, D = x.shape[1], x.shape[2]
    dh = q_norm_g.shape[1]
    HQ = sinks.shape[1]
    HKV = HQ // 4
    AW, KVW, CW = HQ * dh, HKV * dh, D // 2
    off_q, off_k, off_v = 3 * CW, 3 * CW + AW, 3 * CW + AW + KVW
    off_ga, off_gb = off_v + KVW, off_v + KVW + D
    x0, target = x[0], loss_target[0]
    cx, cy, cc = lax.axis_index("x"), lax.axis_index("y"), lax.axis_index("c")
    chip = 2 * cx + cy
    p_arr = jnp.reshape(chip, (1,)).astype(jnp.int32)
    c_arr = jnp.reshape(cc, (1,)).astype(jnp.int32)
    cp_arr = jnp.stack([cc, chip]).astype(jnp.int32)
    wts = dict(g_ffn1=g_ffn1, w_gu1=w_gu1, w_down1=w_down1, g_mix=g_mix, w_in=w_in, conv_w=conv_w, q_norm_g=q_norm_g,
               k_norm_g=k_norm_g, sinks=sinks, w_out_conv=w_out_conv, w_out_attn=w_out_attn, w_o=w_o, g_ffn2=g_ffn2,
               w_gu2=w_gu2, w_down2=w_down2)
    ms = dict(g_ffn1=m_g_ffn1, w_gu1=m_w_gu1, w_down1=m_w_down1, g_mix=m_g_mix, w_in=m_w_in, conv_w=m_conv_w,
              q_norm_g=m_q_norm_g, k_norm_g=m_k_norm_g, sinks=m_sinks, w_out_conv=m_w_out_conv, w_out_attn=m_w_out_attn,
              w_o=m_w_o, g_ffn2=m_g_ffn2, w_gu2=m_w_gu2, w_down2=m_w_down2)
    vs = dict(g_ffn1=v_g_ffn1, w_gu1=v_w_gu1, w_down1=v_w_down1, g_mix=v_g_mix, w_in=v_w_in, conv_w=v_conv_w,
              q_norm_g=v_q_norm_g, k_norm_g=v_k_norm_g, sinks=v_sinks, w_out_conv=v_w_out_conv, w_out_attn=v_w_out_attn,
              w_o=v_w_o, g_ffn2=v_g_ffn2, w_gu2=v_w_gu2, w_down2=v_w_down2)
    order = list(wts)
    small_names = [k for k in order if not k.startswith("w_")]
    grad, delta, new_m, new_v = {}, {}, {}, {}

    def cast(keys, dep=None):
        return [_cast_to_slot(f"cast_{k}", wts[k][0], F32 if k == "conv_w" else BF16, p_arr, dep) for k in keys]

    def gather_start(tag, slots, dep):
        send, recv, slots, tok = _gather_start(f"gather_start_{tag}", slots, dep)
        return (tag, send, recv, slots), tok

    def gather_finish(started, after):
        tag, send, recv, slots = started
        return _gather_forward(f"gather_forward_{tag}", _gather_wait(f"gather_wait_{tag}", send, recv, slots, after))

    swapping, pending = [], []

    def reduce_start(full):
        keys = list(full)
        send, recv, gs, lands, tok = _swap_start(f"swap_start_{keys[0]}", [full[k] for k in keys])
        if swapping:
            tok = reduce_advance(tok)
        swapping.append((keys, send, recv, gs, lands))
        return tok

    def reduce_advance(after):
        keys, send, recv, gs, lands = swapping.pop(0)
        gs, sib = _swap_wait(f"swap_wait_{keys[0]}", send, recv, gs, lands, after)
        parts = [_add_half(f"add_half_{k}", g, r, c_arr) for k, g, r in zip(keys, gs, sib)]
        send, recv, parts, lands, tok = _exchange_start(f"exchange_start_{keys[0]}", parts)
        pending.append((keys, send, recv, parts, lands))
        return tok

    def reduce_finish(entries, after):
        keys_all, halves = [], []
        for keys, send, recv, parts, lands in entries:
            parts, lands = _exchange_wait(f"exchange_wait_{keys[0]}", send, recv, parts, lands, after)
            halves += [_add_chips(f"add_chips_{k}", t, r, cp_arr) for k, t, r in zip(keys, parts, lands)]
            keys_all += keys
        for k, g2 in zip(keys_all, _join_halves(f"join_{keys_all[0]}", halves)):
            d, nm, nv = _adamw(f"adamw_{k}", wts[k][0], g2, ms[k][0], vs[k][0])
            grad[k], delta[k], new_m[k], new_v[k] = g2[None], d[None], nm[None], nv[None]

    keys_mix, keys_2 = ["w_in", "w_out_conv", "w_out_attn", "w_o", "conv_w"], ["w_gu2", "w_down2"]
    st_gu1, tok = gather_start("gu1", cast(["w_gu1"]), x0)
    st_d1, tok = gather_start("d1", cast(["w_down1"]), tok)
    slots_mix, slots_2 = cast(keys_mix, tok), cast(keys_2, tok)
    wgu1, = gather_finish(st_gu1, slots_2[-1])
    st_mix, tok = gather_start("mix", slots_mix, wgu1)
    st_2, tok = gather_start("2", slots_2, tok)
    cos, sin, rm, rmt = _rope_consts(S, dh)
    sink_vec = sinks[0]

    h1 = _rms_fwd("rms_fwd_1", x0, g_ffn1, tok)
    gu1, act1 = _ffn_up("ffn_up_1", h1, wgu1)
    wd1 = gather_finish(st_d1, act1)[0].reshape(-1, D)
    x1 = _mm_res("ffn_down_1", act1, wd1, x0, 0.5)
    win3, woc3, woa3, wo, convw3 = gather_finish(st_mix, x1)
    wo = wo.reshape(-1, D)
    h2 = _rms_fwd("rms_fwd_mix", x1, g_mix)
    proj = _mm_cols("in_proj", h2, win3, F32)
    aconv = _conv_fwd("conv_fwd", proj, convw3, CW)
    ya = _mm_cols("out_conv", aconv, woc3, F32)
    q_raw = _heads(proj[:, off_q:off_q + AW], HQ)
    k_raw = _heads(proj[:, off_k:off_k + KVW], HKV)
    vh = _heads(proj[:, off_v:off_v + KVW], HKV).astype(BF16)
    qn = _qk_prep("q_prep", q_raw, q_norm_g, cos, sin, rm)
    kn = _qk_prep("k_prep", k_raw, k_norm_g, cos, sin, rm)
    oh = _attn_fwd("attn_fwd", qn, kn, vh, sink_vec)
    o = _unheads(oh)
    yb = _mm_cols("out_attn", o, woa3, F32)
    merged = _gate_fwd("gate_fwd", proj, ya, yb, off_ga, off_gb)
    x2 = _mm_res("mix_out", merged, wo, x1, 1.0)
    wgu2, wd2 = gather_finish(st_2, x2)
    wd2 = wd2.reshape(-1, D)
    h3 = _rms_fwd("rms_fwd_2", x2, g_ffn2)
    gu2, act2 = _ffn_up("ffn_up_2", h3, wgu2)
    x3 = _mm_res("ffn_down_2", act2, wd2, x2, 0.5)

    dy, loss_lanes = _loss_grad("loss_grad", x3, target)
    dx2, dg_ffn2, tok = _ffn_bwd("2", x2, g_ffn2, wgu2, wd2, (h3, gu2, act2), dy, reduce_start, None)
    dx2_b = dx2.astype(BF16)
    dmerged = _mm_nt("mix_out_bwd", dx2_b, wo, F32)
    tok = reduce_start(dict(w_o=_mm_tn("dw_o", merged, dx2_b, dep=tok).reshape(N_CHIPS, -1, D)))
    dga, dgb, dya, dyb = _gate_bwd("gate_bwd", proj, ya, yb, dmerged, off_ga, off_gb)
    daconv = _mm_nt_cols("out_conv_bwd", dya, woc3, dep=tok)
    dwoc = _mm_tn_cols("dw_out_conv", aconv, dya, woc3.shape[2])
    do = _mm_nt_cols("out_attn_bwd", dyb, woa3)
    dwoa = _mm_tn_cols("dw_out_attn", o, dyb, woa3.shape[2])
    tok = reduce_start(dict(w_out_conv=dwoc, w_out_attn=dwoa))
    dxc, dbg, dcg, dconvw = _conv_bwd("conv_bwd", proj, convw3, daconv, CW)
    dqn, dkn, dvh, dsink3 = _attn_bwd("attn_bwd", qn, kn, vh, sink_vec, _heads(do, HQ).astype(BF16))
    dq_raw, dqg = _qk_prep_bwd("q_prep_bwd", q_raw, q_norm_g, cos, sin, rmt, dqn)
    dk_raw, dkg = _qk_prep_bwd("k_prep_bwd", k_raw, k_norm_g, cos, sin, rmt, dkn)
    dproj = jnp.concatenate([dxc, dbg, dcg, _unheads(dq_raw), _unheads(dk_raw), _unheads(dvh).astype(BF16), dga, dgb], axis=1)
    dh2 = _mm_nt_cols("in_proj_bwd", dproj, win3, dep=tok)
    tok = reduce_start(dict(w_in=_mm_tn_cols("dw_in", h2, dproj, win3.shape[2])))
    dx1, dg_mix = _rms_bwd("rms_bwd_mix", x1, g_mix, dh2, dx2)
    dx0, dg_ffn1, tok = _ffn_bwd("1", x0, g_ffn1, wgu1, wd1, (h1, gu1, act1), dx1, reduce_start, tok, reduce_advance)

    def rows8(a):
        a = a.reshape(-1, a.shape[-1])
        return jnp.pad(a, ((0, -a.shape[0] % 8), (0, D - a.shape[1])))

    misc = jnp.concatenate([dqg, dkg, dsink3[:, :, 0].reshape(1, HQ), loss_lanes], axis=1)
    reduce_finish(pending[:-2], dx0)
    tot = _allreduce_small("allreduce_small", jnp.concatenate([rows8(a) for a in (dg_ffn1, dg_mix, dg_ffn2, dconvw, misc)], axis=0))
    reduce_finish(pending[-2:], new_v[pending[-3][0][-1]])

    cw_s = conv_w.shape[2]
    conv_row0, misc_row = 24, 24 + (-(-N_CHIPS * CONV_K // 8)) * 8
    small_g = dict(g_ffn1=tot[0:1], g_mix=tot[8:9], g_ffn2=tot[16:17],
                   conv_w=lax.dynamic_slice(tot, (conv_row0 + CONV_K * chip, 0), (CONV_K, cw_s)),
                   q_norm_g=tot[misc_row:misc_row + 1, 0:dh], k_norm_g=tot[misc_row:misc_row + 1, dh:2 * dh],
                   sinks=tot[misc_row:misc_row + 1, 2 * dh:2 * dh + HQ])
    loss = (0.5 / D) * jnp.sum(tot[misc_row, 2 * dh + HQ:2 * dh + HQ + LANES])

    def small_pack(src):
        return jnp.concatenate([rows8(src[k]) for k in small_names], axis=0)

    sd, sm, sv = _adamw("adamw_small", small_pack(wts), small_pack(small_g), small_pack(ms), small_pack(vs))
    for i, k in enumerate(small_names):
        shape = wts[k].shape
        nr, ncol = math.prod(shape[:-1]), shape[-1]
        grad[k] = small_g[k].reshape(shape)
        delta[k], new_m[k], new_v[k] = (a[8 * i:8 * i + nr, 0:ncol].reshape(shape) for a in (sd, sm, sv))
    return (loss, dx0[None], *[grad[k] for k in order], *[delta[k] for k in order],
            *[new_m[k] for k in order], *[new_v[k] for k in order])
```

```python
import math

import numpy as np
import jax
import jax.numpy as jnp
from jax import lax
from jax.experimental import pallas as pl
from jax.experimental.pallas import tpu as pltpu

F32 = jnp.float32
BF16 = jnp.bfloat16
MESH = pl.DeviceIdType.MESH

RMS_EPS = 1e-6
BLOCK = 128
ROPE_THETA = 500000.0
NEG_INF = -1e30
CONV_K = 3
ADAM_LR, ADAM_B1, ADAM_B2, ADAM_EPS, ADAM_WD, ADAM_STEP = 0.001, 0.9, 0.999, 1e-08, 0.01, 10

VMEM_LIMIT_V7X = 56 * 1024 * 1024
LANES = 128
N_CHIPS = 4
N_DEV = 8


def _tile(n, want, align=LANES):
    best = None
    t = align
    while t <= min(n, want):
        if n % t == 0:
            best = t
        t += align
    return best or n


def _cparams(sem):
    return pltpu.CompilerParams(dimension_semantics=sem, vmem_limit_bytes=VMEM_LIMIT_V7X)


def _sigmoid(x):
    return 1.0 / (1.0 + jnp.exp(-x))


NN = (((1,), (0,)), ((), ()))
NT = (((1,), (1,)), ((), ()))
TN = (((0,), (0,)), ((), ()))


def _mm(name, grid, ins, in_specs, compute, out_shape, out_specs, epilogue, dep=None):
    if dep is not None:
        ins, in_specs = tuple(ins) + (dep,), list(in_specs) + [pl.BlockSpec(dep.shape, lambda *_: (0, 0))]
    n_in = len(ins)

    def body(*refs):
        epilogue(compute(refs[:n_in]), refs[:n_in], refs[n_in:])

    return pl.pallas_call(
        body, name=name, grid=grid, in_specs=in_specs, out_specs=out_specs, out_shape=out_shape,
        compiler_params=_cparams(("parallel", "arbitrary")),
    )(*ins)


def _dot(dims, a=0, b=1):
    return lambda refs: [lax.dot_general(refs[a][...], refs[b][...], dims, preferred_element_type=F32)]


def _ffn_up(name, h, wgu3):
    S, D = h.shape
    Ns = wgu3.shape[2]
    F = 2 * Ns
    tm, tn = _tile(S, 512), _tile(Ns, 1408)
    nbs = Ns // tn

    def compute(refs):
        hv = refs[0][...]
        return [jnp.dot(hv, refs[1][...], preferred_element_type=F32), jnp.dot(hv, refs[2][...], preferred_element_type=F32)]

    def epi(accs, in_refs, out_refs):
        g, u = accs
        gu_ref, a_ref = out_refs
        gu_ref[0] = g.astype(BF16)
        gu_ref[1] = u.astype(BF16)
        a_ref[...] = (g * _sigmoid(g) * u).astype(BF16)

    return _mm(
        name, (F // tn, S // tm), (h, wgu3, wgu3),
        [pl.BlockSpec((tm, D), lambda j, i: (i, 0)),
         pl.BlockSpec((None, D, tn), lambda j, i: (j // nbs, 0, j % nbs)),
         pl.BlockSpec((None, D, tn), lambda j, i: (2 + j // nbs, 0, j % nbs))],
        compute, (jax.ShapeDtypeStruct((2, S, F), BF16), jax.ShapeDtypeStruct((S, F), BF16)),
        (pl.BlockSpec((2, tm, tn), lambda j, i: (0, i, j)), pl.BlockSpec((tm, tn), lambda j, i: (i, j))), epi)


def _mm_res(name, a, w, res, scale):
    S, K = a.shape
    N = w.shape[1]
    tm, tn = _tile(S, 512), _tile(N, 512 if K > 2816 else 1024)

    def epi(accs, in_refs, out_refs):
        out_refs[0][...] = in_refs[2][...] + scale * accs[0]

    return _mm(
        name, (N // tn, S // tm), (a, w, res),
        [pl.BlockSpec((tm, K), lambda j, i: (i, 0)), pl.BlockSpec((K, tn), lambda j, i: (0, j)),
         pl.BlockSpec((tm, tn), lambda j, i: (i, j))],
        _dot(NN), jax.ShapeDtypeStruct((S, N), F32), pl.BlockSpec((tm, tn), lambda j, i: (i, j)), epi)


def _mm_cols(name, a, w3, out_dtype):
    S, K = a.shape
    Ns = w3.shape[2]
    tm, tn = _tile(S, 512), _tile(Ns, 2304)
    nbs = Ns // tn

    def epi(accs, in_refs, out_refs):
        out_refs[0][...] = accs[0].astype(out_dtype)

    return _mm(
        name, (N_CHIPS * nbs, S // tm), (a, w3),
        [pl.BlockSpec((tm, K), lambda j, i: (i, 0)),
         pl.BlockSpec((None, K, tn), lambda j, i: (j // nbs, 0, j % nbs))],
        _dot(NN), jax.ShapeDtypeStruct((S, N_CHIPS * Ns), out_dtype), pl.BlockSpec((tm, tn), lambda j, i: (i, j)), epi)


def _mm_nt(name, a, w, out_dtype, scale=1.0):
    S, N = a.shape
    K = w.shape[0]
    tm, tn = _tile(S, 512), _tile(K, 1024)

    def epi(accs, in_refs, out_refs):
        out_refs[0][...] = (scale * accs[0]).astype(out_dtype)

    return _mm(
        name, (K // tn, S // tm), (a, w),
        [pl.BlockSpec((tm, N), lambda j, i: (i, 0)), pl.BlockSpec((tn, N), lambda j, i: (j, 0))],
        _dot(NT), jax.ShapeDtypeStruct((S, K), out_dtype), pl.BlockSpec((tm, tn), lambda j, i: (i, j)), epi)


def _ffn_down_bwd(name, dy, wd, gu, scale, dep=None):
    S, D = dy.shape
    F = wd.shape[0]
    tm, tn = _tile(S, 512), _tile(F, 1408)

    def epi(accs, in_refs, out_refs):
        da = scale * accs[0]
        g = in_refs[2][0].astype(F32)
        u = in_refs[2][1].astype(F32)
        sg = _sigmoid(g)
        out_refs[0][0] = (da * u * (sg * (1.0 + g * (1.0 - sg)))).astype(BF16)
        out_refs[0][1] = (da * (g * sg)).astype(BF16)

    return _mm(
        name, (F // tn, S // tm), (dy, wd, gu),
        [pl.BlockSpec((tm, D), lambda j, i: (i, 0)), pl.BlockSpec((tn, D), lambda j, i: (j, 0)),
         pl.BlockSpec((2, tm, tn), lambda j, i: (0, i, j))],
        _dot(NT), jax.ShapeDtypeStruct((2, S, F), BF16), pl.BlockSpec((2, tm, tn), lambda j, i: (0, i, j)), epi, dep=dep)


def _mm_nt_cols(name, a, w3, a_is_gu=False, dep=None):
    K, Ns = w3.shape[1], w3.shape[2]
    S = a.shape[1] if a_is_gu else a.shape[0]
    tm = _tile(S, 512)
    tn = _tile(K, max(LANES, (6 << 20) // (N_CHIPS * Ns * 2)))
    if a_is_gu:
        a_spec = pl.BlockSpec((2, tm, 2 * Ns), lambda i, j: (0, i, 0))
        part = lambda a_ref, s: a_ref[s // 2, :, (s % 2) * Ns:(s % 2 + 1) * Ns]
    else:
        a_spec = pl.BlockSpec((tm, N_CHIPS * Ns), lambda i, j: (i, 0))
        part = lambda a_ref, s: a_ref[:, s * Ns:(s + 1) * Ns]

    def compute(refs):
        total = None
        for s in range(N_CHIPS):
            prod = lax.dot_general(part(refs[0], s), refs[1][s], NT, preferred_element_type=F32)
            total = prod if total is None else total + prod
        return [total]

    def epi(accs, in_refs, out_refs):
        out_refs[0][...] = accs[0]

    return _mm(
        name, (S // tm, K // tn), (a, w3), [a_spec, pl.BlockSpec((N_CHIPS, tn, Ns), lambda i, j: (0, j, 0))],
        compute, jax.ShapeDtypeStruct((S, K), F32), pl.BlockSpec((tm, tn), lambda i, j: (i, j)), epi, dep=dep)


def _mm_tn(name, a, b, scale=1.0, dep=None):
    S, K = a.shape
    N = b.shape[1]
    tm, tn = _tile(K, 512), _tile(N, 1024)

    def epi(accs, in_refs, out_refs):
        out_refs[0][...] = (scale * accs[0]).astype(BF16)

    return _mm(
        name, (N // tn, K // tm), (a, b),
        [pl.BlockSpec((S, tm), lambda j, i: (0, i)), pl.BlockSpec((S, tn), lambda j, i: (0, j))],
        _dot(TN), jax.ShapeDtypeStruct((K, N), BF16), pl.BlockSpec((tm, tn), lambda j, i: (i, j)), epi, dep=dep)


def _mm_tn_cols(name, a, b, Ns, b_is_gu=False, dep=None):
    S, K = a.shape
    tm, tn = _tile(K, 512), _tile(Ns, 2304)
    nbs = Ns // tn
    if b_is_gu:
        b_spec = pl.BlockSpec((None, S, tn), lambda j, i: (j // (2 * nbs), 0, j % (2 * nbs)))
    else:
        b_spec = pl.BlockSpec((S, tn), lambda j, i: (0, j))

    def epi(accs, in_refs, out_refs):
        out_refs[0][...] = accs[0].astype(BF16)

    return _mm(
        name, (N_CHIPS * nbs, K // tm), (a, b), [pl.BlockSpec((S, tm), lambda j, i: (0, i)), b_spec],
        _dot(TN), jax.ShapeDtypeStruct((N_CHIPS, K, Ns), BF16),
        pl.BlockSpec((None, tm, tn), lambda j, i: (j // nbs, i, j % nbs)), epi, dep=dep)


def _rms_fwd(name, x, gain, dep=None):
    S, D = x.shape
    tm = _tile(S, 256, 8)
    extra = () if dep is None else (dep,)

    def body(x_ref, g_ref, *rest):
        h_ref = rest[-1]
        xv = x_ref[...]
        r = lax.rsqrt(jnp.mean(xv * xv, axis=-1, keepdims=True) + RMS_EPS)
        h_ref[...] = (xv * r * g_ref[...]).astype(BF16)

    return pl.pallas_call(
        body, name=name, grid=(S // tm,),
        in_specs=[pl.BlockSpec((tm, D), lambda i: (i, 0)), pl.BlockSpec((1, D), lambda i: (0, 0))]
        + [pl.BlockSpec(d.shape, lambda i: (0, 0)) for d in extra],
        out_specs=pl.BlockSpec((tm, D), lambda i: (i, 0)), out_shape=jax.ShapeDtypeStruct((S, D), BF16),
        compiler_params=_cparams(("parallel",)),
    )(x, gain, *extra)


def _rms_bwd(name, x, gain, dh, dres):
    S, D = x.shape
    tm = _tile(S, 256, 8)

    def body(x_ref, g_ref, dh_ref, dres_ref, dx_ref, dg_ref):
        i = pl.program_id(0)
        xv = x_ref[...]
        r = lax.rsqrt(jnp.mean(xv * xv, axis=-1, keepdims=True) + RMS_EPS)
        xhat = xv * r
        dhv = dh_ref[...]
        dxhat = dhv * g_ref[...]
        dx_ref[...] = dres_ref[...] + r * (dxhat - xhat * jnp.mean(dxhat * xhat, axis=-1, keepdims=True))

        @pl.when(i == 0)
        def _():
            dg_ref[...] = jnp.zeros_like(dg_ref)

        dg_ref[...] += jnp.sum(dhv * xhat, axis=0, keepdims=True)

    row = pl.BlockSpec((tm, D), lambda i: (i, 0))
    vec = pl.BlockSpec((1, D), lambda i: (0, 0))
    return pl.pallas_call(
        body, name=name, grid=(S // tm,), in_specs=[row, vec, row, row], out_specs=(row, vec),
        out_shape=(jax.ShapeDtypeStruct((S, D), F32), jax.ShapeDtypeStruct((1, D), F32)),
        compiler_params=_cparams(("arbitrary",)),
    )(x, gain, dh, dres)


def _loss_grad(name, y, target):
    S, D = y.shape
    tm = _tile(S, 256, 8)

    def body(y_ref, t_ref, dy_ref, l_ref):
        i = pl.program_id(0)
        e = y_ref[...] - t_ref[...]
        dy_ref[...] = e * (1.0 / D)
        col = jnp.sum(e * e, axis=0, keepdims=True)
        part = col[:, 0:LANES]
        for k in range(1, D // LANES):
            part = part + col[:, k * LANES:(k + 1) * LANES]

        @pl.when(i == 0)
        def _():
            l_ref[...] = jnp.zeros_like(l_ref)

        l_ref[...] += part

    row = pl.BlockSpec((tm, D), lambda i: (i, 0))
    return pl.pallas_call(
        body, name=name, grid=(S // tm,), in_specs=[row, row],
        out_specs=(row, pl.BlockSpec((1, LANES), lambda i: (0, 0))),
        out_shape=(jax.ShapeDtypeStruct((S, D), F32), jax.ShapeDtypeStruct((1, LANES), F32)),
        compiler_params=_cparams(("arbitrary",)),
    )(y, target)


def _shift_down(u, k):
    rows = lax.broadcasted_iota(jnp.int32, u.shape, 0)
    return jnp.where(rows >= k, pltpu.roll(u, k, 0), 0.0)


def _shift_up(u, k):
    n = u.shape[0]
    rows = lax.broadcasted_iota(jnp.int32, u.shape, 0)
    return jnp.where(rows < n - k, pltpu.roll(u, n - k, 0), 0.0)


def _conv_specs(S, cw, conv_width):
    nb = conv_width // cw
    col = lambda off: pl.BlockSpec((S, cw), lambda j, off=off: (0, off * nb + j))
    return nb, col(0), col(1), col(2)


def _conv_fwd(name, proj, convw3, conv_width):
    S = proj.shape[0]
    cw = convw3.shape[2]
    nb, xc_s, bg_s, cg_s = _conv_specs(S, cw, conv_width)

    def body(xc_ref, bg_ref, cg_ref, w_ref, o_ref):
        u = cg_ref[...] * xc_ref[...]
        w = w_ref[...]
        cv = w[2:3, :] * u + w[1:2, :] * _shift_down(u, 1) + w[0:1, :] * _shift_down(u, 2)
        o_ref[...] = (bg_ref[...] * cv).astype(BF16)

    return pl.pallas_call(
        body, name=name, grid=(nb,),
        in_specs=[xc_s, bg_s, cg_s, pl.BlockSpec((None, CONV_K, cw), lambda j: (j, 0, 0))],
        out_specs=pl.BlockSpec((S, cw), lambda j: (0, j)), out_shape=jax.ShapeDtypeStruct((S, conv_width), BF16),
        compiler_params=_cparams(("parallel",)),
    )(proj, proj, proj, convw3)


def _conv_bwd(name, proj, convw3, da, conv_width):
    S = proj.shape[0]
    cw = convw3.shape[2]
    nb, xc_s, bg_s, cg_s = _conv_specs(S, cw, conv_width)

    def body(xc_ref, bg_ref, cg_ref, w_ref, da_ref, dxc_ref, dbg_ref, dcg_ref, dw_ref):
        xc, cg = xc_ref[...], cg_ref[...]
        u = cg * xc
        w = w_ref[...]
        u1, u2 = _shift_down(u, 1), _shift_down(u, 2)
        cv = w[2:3, :] * u + w[1:2, :] * u1 + w[0:1, :] * u2
        dav = da_ref[...]
        dbg_ref[...] = (dav * cv).astype(BF16)
        dcv = dav * bg_ref[...]
        du = w[2:3, :] * dcv + w[1:2, :] * _shift_up(dcv, 1) + w[0:1, :] * _shift_up(dcv, 2)
        dxc_ref[...] = (du * cg).astype(BF16)
        dcg_ref[...] = (du * xc).astype(BF16)
        dw_ref[0:1, :] = jnp.sum(dcv * u2, axis=0, keepdims=True)
        dw_ref[1:2, :] = jnp.sum(dcv * u1, axis=0, keepdims=True)
        dw_ref[2:3, :] = jnp.sum(dcv * u, axis=0, keepdims=True)

    wspec = pl.BlockSpec((None, CONV_K, cw), lambda j: (j, 0, 0))
    ospec = pl.BlockSpec((S, cw), lambda j: (0, j))
    act = jax.ShapeDtypeStruct((S, conv_width), BF16)
    return pl.pallas_call(
        body, name=name, grid=(nb,), in_specs=[xc_s, bg_s, cg_s, wspec, ospec],
        out_specs=(ospec, ospec, ospec, wspec),
        out_shape=(act, act, act, jax.ShapeDtypeStruct(convw3.shape, F32)),
        compiler_params=_cparams(("parallel",)),
    )(proj, proj, proj, convw3, da)


def _rope_consts(S, dh):
    rot = dh // 4
    half = rot // 2
    inv_freq = 1.0 / (ROPE_THETA ** (jnp.arange(0, rot, 2, dtype=F32) / rot))
    ang = jnp.arange(S, dtype=F32)[:, None] * inv_freq[None, :]
    cos = jnp.concatenate([jnp.cos(ang), jnp.cos(ang), jnp.ones((S, dh - rot), F32)], axis=1)
    sin = jnp.concatenate([jnp.sin(ang), jnp.sin(ang), jnp.zeros((S, dh - rot), F32)], axis=1)
    rm = np.zeros((dh, dh), np.float32)
    for j in range(half):
        rm[j + half, j] = -1.0
        rm[j, j + half] = 1.0
    return cos, sin, jnp.asarray(rm, BF16), jnp.asarray(rm.T, BF16)


def _exact_perm(y, rm):
    hi = y.astype(BF16)
    r1 = y - hi.astype(F32)
    mid = r1.astype(BF16)
    lo = (r1 - mid.astype(F32)).astype(BF16)
    dot = lambda a: jnp.dot(a, rm, preferred_element_type=F32)
    return dot(hi) + dot(mid) + dot(lo)


def _qk_prep(name, xh, gain, cos, sin, rm):
    H, S, dh = xh.shape
    tm = _tile(S, 1024, 8)

    def body(x_ref, g_ref, c_ref, s_ref, rm_ref, o_ref):
        xv = x_ref[...]
        y = xv * lax.rsqrt(jnp.mean(xv * xv, axis=-1, keepdims=True) + RMS_EPS) * g_ref[...]
        o_ref[...] = (y * c_ref[...] + _exact_perm(y, rm_ref[...]) * s_ref[...]).astype(BF16)

    blk = pl.BlockSpec((None, tm, dh), lambda h, i: (h, i, 0))
    tab = pl.BlockSpec((tm, dh), lambda h, i: (i, 0))
    return pl.pallas_call(
        body, name=name, grid=(H, S // tm),
        in_specs=[blk, pl.BlockSpec((1, dh), lambda h, i: (0, 0)), tab, tab, pl.BlockSpec((dh, dh), lambda h, i: (0, 0))],
        out_specs=blk, out_shape=jax.ShapeDtypeStruct((H, S, dh), BF16),
        compiler_params=_cparams(("parallel", "parallel")),
    )(xh, gain, cos, sin, rm)


def _qk_prep_bwd(name, xh, gain, cos, sin, rmt, dout):
    H, S, dh = xh.shape
    tm = _tile(S, 1024, 8)

    def body(x_ref, g_ref, c_ref, s_ref, rmt_ref, do_ref, dx_ref, dg_ref):
        first = (pl.program_id(0) == 0) & (pl.program_id(1) == 0)
        xv = x_ref[...]
        r = lax.rsqrt(jnp.mean(xv * xv, axis=-1, keepdims=True) + RMS_EPS)
        xhat = xv * r
        dov = do_ref[...]
        dy = dov * c_ref[...] + _exact_perm(dov * s_ref[...], rmt_ref[...])
        dxhat = dy * g_ref[...]
        dx_ref[...] = (r * (dxhat - xhat * jnp.mean(dxhat * xhat, axis=-1, keepdims=True))).astype(BF16)

        @pl.when(first)
        def _():
            dg_ref[...] = jnp.zeros_like(dg_ref)

        dg_ref[...] += jnp.sum(dy * xhat, axis=0, keepdims=True)

    blk = pl.BlockSpec((None, tm, dh), lambda h, i: (h, i, 0))
    tab = pl.BlockSpec((tm, dh), lambda h, i: (i, 0))
    vec = pl.BlockSpec((1, dh), lambda h, i: (0, 0))
    return pl.pallas_call(
        body, name=name, grid=(H, S // tm),
        in_specs=[blk, vec, tab, tab, pl.BlockSpec((dh, dh), lambda h, i: (0, 0)), blk],
        out_specs=(blk, vec), out_shape=(jax.ShapeDtypeStruct((H, S, dh), BF16), jax.ShapeDtypeStruct((1, dh), F32)),
        compiler_params=_cparams(("arbitrary", "arbitrary")),
    )(xh, gain, cos, sin, rmt, dout)


def _attn_probs(q, kp, kc, sink_col, n, scale):
    rows = q.shape[0]
    sp = lax.dot_general(q, kp, NT, preferred_element_type=F32) * scale
    sc = lax.dot_general(q, kc, NT, preferred_element_type=F32) * scale
    qi = lax.broadcasted_iota(jnp.int32, (rows, BLOCK), 0) % BLOCK
    kj = lax.broadcasted_iota(jnp.int32, (rows, BLOCK), 1)
    sp = jnp.where((kj > qi) & (n > 0), sp, NEG_INF)
    sc = jnp.where(kj <= qi, sc, NEG_INF)
    m = jnp.maximum(jnp.maximum(jnp.max(sp, axis=-1, keepdims=True), jnp.max(sc, axis=-1, keepdims=True)), sink_col)
    pp, pc, ps = jnp.exp(sp - m), jnp.exp(sc - m), jnp.exp(sink_col - m)
    inv = 1.0 / (jnp.sum(pp, axis=-1, keepdims=True) + jnp.sum(pc, axis=-1, keepdims=True) + ps)
    return pp * inv, pc * inv, ps * inv


def _sink_col(sink_ref, hk, group):
    rows = group * BLOCK
    g = lax.broadcasted_iota(jnp.int32, (rows, 1), 0) // BLOCK
    col = jnp.zeros((rows, 1), F32)
    for i in range(group):
        col = jnp.where(g == i, sink_ref[hk * group + i], col)
    return col


def _attn_specs(group, dh):
    qb = pl.BlockSpec((group, BLOCK, dh), lambda hk, n: (hk, n, 0))
    prev = pl.BlockSpec((None, BLOCK, dh), lambda hk, n: (hk, jnp.maximum(n - 1, 0), 0))
    cur = pl.BlockSpec((None, BLOCK, dh), lambda hk, n: (hk, n, 0))
    return qb, prev, cur, pl.BlockSpec(memory_space=pltpu.SMEM)


def _attn_fwd(name, q, k, v, sinks):
    HQ, S, dh = q.shape
    HKV = k.shape[0]
    group = HQ // HKV
    scale = dh ** -0.5
    qb, prev, cur, smem = _attn_specs(group, dh)

    def body(q_ref, kp_ref, kc_ref, vp_ref, vc_ref, sink_ref, o_ref):
        hk, n = pl.program_id(0), pl.program_id(1)
        qv = q_ref[...].reshape(group * BLOCK, dh)
        pp, pc, _ = _attn_probs(qv, kp_ref[...], kc_ref[...], _sink_col(sink_ref, hk, group), n, scale)
        o = jnp.dot(pp.astype(BF16), vp_ref[...], preferred_element_type=F32)
        o = o + jnp.dot(pc.astype(BF16), vc_ref[...], preferred_element_type=F32)
        o_ref[...] = o.reshape(group, BLOCK, dh).astype(BF16)

    return pl.pallas_call(
        body, name=name, grid=(HKV, S // BLOCK), in_specs=[qb, prev, cur, prev, cur, smem], out_specs=qb,
        out_shape=jax.ShapeDtypeStruct((HQ, S, dh), BF16), compiler_params=_cparams(("parallel", "parallel")),
    )(q, k, k, v, v, sinks)


def _attn_bwd(name, q, k, v, sinks, do):
    HQ, S, dh = q.shape
    HKV = k.shape[0]
    group = HQ // HKV
    scale = dh ** -0.5
    qb, prev, cur, smem = _attn_specs(group, dh)
    whole = pl.BlockSpec((None, S, dh), lambda hk, n: (hk, 0, 0))
    sk = pl.BlockSpec((None, group, LANES), lambda hk, n: (hk, 0, 0))

    def body(q_ref, kp_ref, kc_ref, vp_ref, vc_ref, sink_ref, do_ref, dq_ref, dk_ref, dv_ref, ds_ref):
        hk, n = pl.program_id(0), pl.program_id(1)
        rows = group * BLOCK
        qv = q_ref[...].reshape(rows, dh)
        dov = do_ref[...].reshape(rows, dh)
        kp, kc, vp, vc = kp_ref[...], kc_ref[...], vp_ref[...], vc_ref[...]
        pp, pc, ps = _attn_probs(qv, kp, kc, _sink_col(sink_ref, hk, group), n, scale)
        dpp = lax.dot_general(dov, vp, NT, preferred_element_type=F32)
        dpc = lax.dot_general(dov, vc, NT, preferred_element_type=F32)
        delta = jnp.sum(pp * dpp, axis=-1, keepdims=True) + jnp.sum(pc * dpc, axis=-1, keepdims=True)
        dsp = (pp * (dpp - delta) * scale).astype(BF16)
        dsc = (pc * (dpc - delta) * scale).astype(BF16)
        dq = jnp.dot(dsp, kp, preferred_element_type=F32) + jnp.dot(dsc, kc, preferred_element_type=F32)
        dq_ref[...] = dq.reshape(group, BLOCK, dh)

        @pl.when(n == 0)
        def _():
            dk_ref[...] = jnp.zeros_like(dk_ref)
            dv_ref[...] = jnp.zeros_like(dv_ref)
            ds_ref[...] = jnp.zeros_like(ds_ref)

        cur_rows = pl.ds(pl.multiple_of(n * BLOCK, BLOCK), BLOCK)
        prev_rows = pl.ds(pl.multiple_of(jnp.maximum(n - 1, 0) * BLOCK, BLOCK), BLOCK)
        tdot = lambda a, b: lax.dot_general(a, b, TN, preferred_element_type=F32)
        dk_ref[prev_rows, :] += tdot(dsp, qv)
        dv_ref[prev_rows, :] += tdot(pp.astype(BF16), dov)
        dk_ref[cur_rows, :] += tdot(dsc, qv)
        dv_ref[cur_rows, :] += tdot(pc.astype(BF16), dov)
        dsink = -jnp.sum((ps * delta).reshape(group, BLOCK, 1), axis=1)
        ds_ref[...] += jnp.broadcast_to(dsink, (group, LANES))

    return pl.pallas_call(
        body, name=name, grid=(HKV, S // BLOCK), in_specs=[qb, prev, cur, prev, cur, smem, qb],
        out_specs=(qb, whole, whole, sk),
        out_shape=(jax.ShapeDtypeStruct((HQ, S, dh), F32), jax.ShapeDtypeStruct((HKV, S, dh), F32),
                   jax.ShapeDtypeStruct((HKV, S, dh), F32), jax.ShapeDtypeStruct((HKV, group, LANES), F32)),
        compiler_params=_cparams(("arbitrary", "arbitrary")),
    )(q, k, k, v, v, sinks, do)


def _gate_specs(S, D, ga_off, gb_off):
    tg = LANES
    for t in range(LANES, 513, LANES):
        if D % t == 0 and ga_off % t == 0 and gb_off % t == 0:
            tg = t
    if D % LANES:
        tg = math.gcd(math.gcd(D, ga_off), gb_off)
    tm = _tile(S, 512, 8)
    act = pl.BlockSpec((tm, tg), lambda i, j: (i, j))
    ga = pl.BlockSpec((tm, tg), lambda i, j: (i, ga_off // tg + j))
    gb = pl.BlockSpec((tm, tg), lambda i, j: (i, gb_off // tg + j))
    return (S // tm, D // tg), act, ga, gb


def _gate_fwd(name, proj, ya, yb, ga_off, gb_off):
    S, D = ya.shape
    grid, act, ga, gb = _gate_specs(S, D, ga_off, gb_off)

    def body(ga_ref, gb_ref, ya_ref, yb_ref, o_ref):
        o_ref[...] = (_sigmoid(ga_ref[...]) * ya_ref[...] + _sigmoid(gb_ref[...]) * yb_ref[...]).astype(BF16)

    return pl.pallas_call(
        body, name=name, grid=grid, in_specs=[ga, gb, act, act], out_specs=act,
        out_shape=jax.ShapeDtypeStruct((S, D), BF16), compiler_params=_cparams(("parallel", "parallel")),
    )(proj, proj, ya, yb)


def _gate_bwd(name, proj, ya, yb, dm, ga_off, gb_off):
    S, D = ya.shape
    grid, act, ga, gb = _gate_specs(S, D, ga_off, gb_off)

    def body(ga_ref, gb_ref, ya_ref, yb_ref, dm_ref, dga_ref, dgb_ref, dya_ref, dyb_ref):
        dmv = dm_ref[...]
        sa, sb = _sigmoid(ga_ref[...]), _sigmoid(gb_ref[...])
        dga_ref[...] = (dmv * ya_ref[...] * sa * (1.0 - sa)).astype(BF16)
        dgb_ref[...] = (dmv * yb_ref[...] * sb * (1.0 - sb)).astype(BF16)
        dya_ref[...] = (dmv * sa).astype(BF16)
        dyb_ref[...] = (dmv * sb).astype(BF16)

    o = jax.ShapeDtypeStruct((S, D), BF16)
    return pl.pallas_call(
        body, name=name, grid=grid, in_specs=[ga, gb, act, act, act], out_specs=(act, act, act, act),
        out_shape=(o, o, o, o), compiler_params=_cparams(("parallel", "parallel")),
    )(proj, proj, ya, yb, dm)


ANY = pl.BlockSpec(memory_space=pl.ANY)


def _row_tile(rows, cols, n_arrays):
    want = max(16, (VMEM_LIMIT_V7X // 2) // (2 * n_arrays * cols * 4))
    return _tile(rows, want, 16)


def _cast_to_slot(name, w, dtype, p_arr, dep=None):
    R, C = w.shape
    tr = _row_tile(R, C, 2)
    extra = () if dep is None else (dep,)

    def body(p_ref, w_ref, *rest):
        rest[-1][...] = w_ref[...].astype(dtype)

    return pl.pallas_call(
        body, name=name,
        grid_spec=pltpu.PrefetchScalarGridSpec(
            num_scalar_prefetch=1, grid=(R // tr,),
            in_specs=[pl.BlockSpec((tr, C), lambda i, p_ref: (i, 0))] + [pl.BlockSpec(d.shape, lambda i, p_ref: (0, 0)) for d in extra],
            out_specs=pl.BlockSpec((None, tr, C), lambda i, p_ref: (p_ref[0], i, 0))),
        out_shape=jax.ShapeDtypeStruct((N_CHIPS, R, C), dtype), compiler_params=_cparams(("parallel",)),
    )(p_arr, w, *extra)


def _add_half(name, g3, r3, c_arr):
    n, h, C = r3.shape
    tr = _row_tile(h, C, 3)
    nb = h // tr

    def body(c_ref, g_ref, r_ref, o_ref):
        o_ref[...] = (g_ref[...].astype(F32) + r_ref[...].astype(F32)).astype(BF16)

    blk = pl.BlockSpec((None, tr, C), lambda s, i, c_ref: (s, i, 0))
    return pl.pallas_call(
        body, name=name,
        grid_spec=pltpu.PrefetchScalarGridSpec(
            num_scalar_prefetch=1, grid=(n, nb),
            in_specs=[pl.BlockSpec((None, tr, C), lambda s, i, c_ref: (s, c_ref[0] * nb + i, 0)), blk], out_specs=blk),
        out_shape=jax.ShapeDtypeStruct(r3.shape, BF16), compiler_params=_cparams(("parallel", "parallel")),
    )(c_arr, g3, r3)


def _add_chips(name, t3, r3, cp_arr):
    n, h, C = r3.shape
    tr = _row_tile(h, C, 6)
    nb = h // tr

    def body(cp_ref, t_ref, r0_ref, r1_ref, r2_ref, r3_ref, o_ref):
        p = cp_ref[1]
        total = None
        for a, r_ref in enumerate((r0_ref, r1_ref, r2_ref, r3_ref)):
            part = jnp.where(p == a, t_ref[...], r_ref[...]).astype(F32)
            total = part if total is None else total + part
        o_ref[...] = total

    def part(a):
        return pl.BlockSpec((None, tr, C), lambda i, cp_ref: (jnp.where(cp_ref[1] == a, (a + 1) % N_CHIPS, a), i, 0))

    return pl.pallas_call(
        body, name=name,
        grid_spec=pltpu.PrefetchScalarGridSpec(
            num_scalar_prefetch=1, grid=(nb,),
            in_specs=[pl.BlockSpec((None, tr, C), lambda i, cp_ref: (cp_ref[1], i, 0)), part(0), part(1), part(2), part(3)],
            out_specs=pl.BlockSpec((tr, C), lambda i, cp_ref: (cp_ref[0] * nb + i, 0))),
        out_shape=jax.ShapeDtypeStruct((2 * h, C), F32), compiler_params=_cparams(("parallel",)),
    )(cp_arr, t3, r3, r3, r3, r3)


def _adamw(name, w, g, m, v, dep=None):
    R, C = w.shape
    extra = () if dep is None else (dep,)
    tr = _row_tile(R, C, 7)
    c1 = 1.0 - ADAM_B1 ** ADAM_STEP
    c2 = 1.0 - ADAM_B2 ** ADAM_STEP

    def body(w_ref, g_ref, m_ref, v_ref, *rest):
        d_ref, nm_ref, nv_ref = rest[-3:]
        gv = g_ref[...]
        nm = ADAM_B1 * m_ref[...] + (1.0 - ADAM_B1) * gv
        nv = ADAM_B2 * v_ref[...] + (1.0 - ADAM_B2) * (gv * gv)
        d_ref[...] = -ADAM_LR * ((nm / c1) / (jnp.sqrt(nv / c2) + ADAM_EPS) + ADAM_WD * w_ref[...])
        nm_ref[...] = nm
        nv_ref[...] = nv

    blk = pl.BlockSpec((tr, C), lambda i: (i, 0))
    o = jax.ShapeDtypeStruct((R, C), F32)
    return pl.pallas_call(
        body, name=name, grid=(R // tr,), in_specs=[blk, blk, blk, blk] + [ANY] * len(extra), out_specs=(blk, blk, blk),
        out_shape=(o, o, o), compiler_params=_cparams(("parallel",)),
    )(w, g, m, v, *extra)


def _place():
    x, y, c = lax.axis_index("x"), lax.axis_index("y"), lax.axis_index("c")
    chips = [(1 - x, y), (x, 1 - y), (1 - x, 1 - y)]
    return x, y, c, 2 * x + y, chips


HBM = pl.BlockSpec(memory_space=pltpu.HBM)
SEM = pl.BlockSpec(memory_space=pltpu.SEMAPHORE)
TOKEN = jax.ShapeDtypeStruct((8, LANES), F32)
DATAFLOW = pltpu.SideEffectType.DATAFLOW_SIDE_EFFECTING


def _hbm(a):
    return pltpu.with_memory_space_constraint(a, pltpu.HBM)


def _gather_blocks(bufs, i, c, p, chips):
    if bufs[i].shape[1] % 16:
        return bufs[i].at[p], [bufs[i].at[2 * cx + cy] for cx, cy in chips]
    h = bufs[i].shape[1] // 2
    rows = pl.ds(pl.multiple_of(c * h, 16), h)
    return bufs[i].at[p, rows], [bufs[i].at[2 * cx + cy, rows] for cx, cy in chips]


def _gather_start(name, slots, dep):
    n = len(slots)

    def body(*refs):
        bufs, send, recv, token = refs[:n], refs[n + 1], refs[n + 2], refs[-1]
        x, y, c, p, chips = _place()
        for i in range(n):
            mine, _ = _gather_blocks(bufs, i, c, p, chips)
            for j, chip in enumerate(chips):
                pltpu.make_async_remote_copy(src_ref=mine, dst_ref=mine, send_sem=send.at[3 * i + j], recv_sem=recv.at[3 * i + j],
                                             device_id=(*chip, c), device_id_type=MESH).start()
        token[...] = jnp.zeros_like(token)

    out = pl.pallas_call(
        body, name=name, in_specs=[HBM] * n + [ANY],
        out_specs=(SEM, SEM, *([HBM] * n), pl.BlockSpec(memory_space=pltpu.VMEM)),
        out_shape=(pltpu.SemaphoreType.DMA((3 * n,)), pltpu.SemaphoreType.DMA((3 * n,)),
                   *[pltpu.HBM(s.shape, s.dtype) for s in slots], TOKEN),
        input_output_aliases={i: 2 + i for i in range(n)},
        compiler_params=pltpu.CompilerParams(has_side_effects=DATAFLOW),
    )(*[_hbm(s) for s in slots], dep)
    return out[0], out[1], list(out[2:2 + n]), out[-1]


def _gather_wait(name, send, recv, slots, after):
    n = len(slots)

    def body(*refs):
        bufs, send, recv = refs[:n], refs[n], refs[n + 1]
        x, y, c, p, chips = _place()
        for i in range(n):
            mine, landed = _gather_blocks(bufs, i, c, p, chips)
            for j, chip in enumerate(chips):
                cp = pltpu.make_async_remote_copy(src_ref=mine, dst_ref=landed[j], send_sem=send.at[3 * i + j],
                                                  recv_sem=recv.at[3 * i + j], device_id=(*chip, c), device_id_type=MESH)
                cp.wait_send()
                cp.wait_recv()

    return list(pl.pallas_call(
        body, name=name, in_specs=[HBM] * n + [SEM, SEM, ANY], out_specs=tuple([HBM] * n),
        out_shape=tuple(pltpu.HBM(s.shape, s.dtype) for s in slots),
        input_output_aliases={i: i for i in range(n)},
        compiler_params=pltpu.CompilerParams(has_side_effects=DATAFLOW),
    )(*slots, send, recv, after))


def _gather_forward(name, slots):
    idx = [i for i, s in enumerate(slots) if s.shape[1] % 16 == 0]
    n = len(slots)

    def body(*refs):
        bufs = refs[n:2 * n]
        send, recv = refs[2 * n:]
        x, y, c, p, chips = _place()

        def rdma(k, ref):
            return pltpu.make_async_remote_copy(src_ref=ref, dst_ref=ref, send_sem=send.at[k], recv_sem=recv.at[k],
                                                device_id=(x, y, 1 - c), device_id_type=MESH)

        cps = []
        for k, i in enumerate(idx):
            for j, ref in enumerate(_gather_blocks(bufs, i, c, p, chips)[1]):
                cps.append(rdma(3 * k + j, ref))
                cps[-1].start()
        for k, i in enumerate(idx):
            for j, ref in enumerate(_gather_blocks(bufs, i, 1 - c, p, chips)[1]):
                rdma(3 * k + j, ref).wait_recv()
        for cp in cps:
            cp.wait_send()

    return list(pl.pallas_call(
        body, name=name, in_specs=[ANY] * n, out_specs=tuple([ANY] * n),
        out_shape=tuple(jax.ShapeDtypeStruct(s.shape, s.dtype) for s in slots),
        scratch_shapes=[pltpu.SemaphoreType.DMA((3 * len(idx),)), pltpu.SemaphoreType.DMA((3 * len(idx),))],
        input_output_aliases={i: i for i in range(n)},
        compiler_params=pltpu.CompilerParams(has_side_effects=True),
    )(*slots))


def _swap_copy(grads, lands, send, recv, i, x, y, c):
    h = grads[i].shape[1] // 2
    other = pl.ds(pl.multiple_of((1 - c) * h, 16), h)
    return pltpu.make_async_remote_copy(src_ref=grads[i].at[:, other, :], dst_ref=lands[i], send_sem=send.at[i],
                                        recv_sem=recv.at[i], device_id=(x, y, 1 - c), device_id_type=MESH)


def _swap_start(name, grads):
    n = len(grads)

    def body(*refs):
        ins, lands, send, recv, token = refs[:n], refs[n:2 * n], refs[2 * n], refs[2 * n + 1], refs[-1]
        x, y, c, p, chips = _place()
        for i in range(n):
            _swap_copy(ins, lands, send, recv, i, x, y, c).start()
        token[...] = jnp.zeros_like(token)

    gshapes = [pltpu.HBM(g.shape, g.dtype) for g in grads]
    halves = [(g.shape[0], g.shape[1] // 2, g.shape[2]) for g in grads]
    lshapes = [pltpu.HBM(s, g.dtype) for s, g in zip(halves, grads)]
    out = pl.pallas_call(
        body, name=name, in_specs=[HBM] * (2 * n),
        out_specs=(SEM, SEM, *([HBM] * (2 * n)), pl.BlockSpec(memory_space=pltpu.VMEM)),
        out_shape=(pltpu.SemaphoreType.DMA((n,)), pltpu.SemaphoreType.DMA((n,)), *gshapes, *lshapes, TOKEN),
        input_output_aliases={i: 2 + i for i in range(2 * n)},
        compiler_params=pltpu.CompilerParams(has_side_effects=DATAFLOW),
    )(*[_hbm(g) for g in grads], *[_hbm(lax.empty(s, g.dtype)) for s, g in zip(halves, grads)])
    return out[0], out[1], list(out[2:2 + n]), list(out[2 + n:2 + 2 * n]), out[-1]


def _swap_wait(name, send, recv, grads, lands, after):
    n = len(grads)

    def body(*refs):
        ins, lands, send, recv = refs[:n], refs[n:2 * n], refs[2 * n], refs[2 * n + 1]
        x, y, c, p, chips = _place()
        for i in range(n):
            cp = _swap_copy(ins, lands, send, recv, i, x, y, c)
            cp.wait_send()
            cp.wait_recv()

    shapes = [pltpu.HBM(t.shape, t.dtype) for t in list(grads) + list(lands)]
    out = pl.pallas_call(
        body, name=name, in_specs=[HBM] * (2 * n) + [SEM, SEM, ANY], out_specs=tuple([HBM] * (2 * n)),
        out_shape=tuple(shapes), input_output_aliases={i: i for i in range(2 * n)},
        compiler_params=pltpu.CompilerParams(has_side_effects=DATAFLOW),
    )(*grads, *lands, send, recv, after)
    return list(out[:n]), list(out[n:])


def _exchange_start(name, parts):
    n = len(parts)

    def body(*refs):
        ins, lands, send, recv, token = refs[:n], refs[n:2 * n], refs[2 * n], refs[2 * n + 1], refs[-1]
        x, y, c, p, chips = _place()
        for i in range(n):
            for j, (cx, cy) in enumerate(chips):
                pltpu.make_async_remote_copy(src_ref=ins[i].at[2 * cx + cy], dst_ref=lands[i].at[p], send_sem=send.at[3 * i + j],
                                             recv_sem=recv.at[3 * i + j], device_id=(cx, cy, c), device_id_type=MESH).start()
        token[...] = jnp.zeros_like(token)

    shapes = [pltpu.HBM(t.shape, t.dtype) for t in parts]
    out = pl.pallas_call(
        body, name=name, in_specs=[HBM] * (2 * n),
        out_specs=(SEM, SEM, *([HBM] * (2 * n)), pl.BlockSpec(memory_space=pltpu.VMEM)),
        out_shape=(pltpu.SemaphoreType.DMA((3 * n,)), pltpu.SemaphoreType.DMA((3 * n,)), *shapes, *shapes, TOKEN),
        input_output_aliases={i: 2 + i for i in range(2 * n)},
        compiler_params=pltpu.CompilerParams(has_side_effects=DATAFLOW),
    )(*[_hbm(t) for t in parts], *[_hbm(lax.empty(t.shape, t.dtype)) for t in parts])
    return out[0], out[1], list(out[2:2 + n]), list(out[2 + n:2 + 2 * n]), out[-1]


def _exchange_wait(name, send, recv, parts, lands, after):
    n = len(parts)

    def body(*refs):
        ins, lands, send, recv = refs[:n], refs[n:2 * n], refs[2 * n], refs[2 * n + 1]
        x, y, c, p, chips = _place()
        for i in range(n):
            for j, (cx, cy) in enumerate(chips):
                q = 2 * cx + cy
                cp = pltpu.make_async_remote_copy(src_ref=ins[i].at[q], dst_ref=lands[i].at[q], send_sem=send.at[3 * i + j],
                                                  recv_sem=recv.at[3 * i + j], device_id=(cx, cy, c), device_id_type=MESH)
                cp.wait_send()
                cp.wait_recv()

    shapes = [pltpu.HBM(t.shape, t.dtype) for t in parts]
    out = pl.pallas_call(
        body, name=name, in_specs=[HBM] * (2 * n) + [SEM, SEM, ANY], out_specs=tuple([HBM] * (2 * n)),
        out_shape=(*shapes, *shapes), input_output_aliases={i: i for i in range(2 * n)},
        compiler_params=pltpu.CompilerParams(has_side_effects=DATAFLOW),
    )(*parts, *lands, send, recv, after)
    return list(out[:n]), list(out[n:])


def _join_halves(name, bufs):
    n = len(bufs)

    def body(*refs):
        outs = refs[n:2 * n]
        send, recv = refs[2 * n:]
        x, y, c, p, chips = _place()

        def rdma(i, which):
            h = outs[i].shape[0] // 2
            rows = outs[i].at[pl.ds(pl.multiple_of(which * h, 8), h)]
            return pltpu.make_async_remote_copy(src_ref=rows, dst_ref=rows, send_sem=send.at[i], recv_sem=recv.at[i],
                                                device_id=(x, y, 1 - c), device_id_type=MESH)

        cps = [rdma(i, c) for i in range(n)]
        for cp in cps:
            cp.start()
        for i, cp in enumerate(cps):
            rdma(i, 1 - c).wait_recv()
            cp.wait_send()

    return pl.pallas_call(
        body, name=name, in_specs=[ANY] * n, out_specs=tuple([ANY] * n),
        out_shape=tuple(jax.ShapeDtypeStruct(t.shape, t.dtype) for t in bufs),
        scratch_shapes=[pltpu.SemaphoreType.DMA((n,)), pltpu.SemaphoreType.DMA((n,))],
        input_output_aliases={i: i for i in range(n)},
        compiler_params=pltpu.CompilerParams(has_side_effects=True),
    )(*bufs)


def _allreduce_small(name, pack, dep):
    R, W = pack.shape

    def body(in_ref, dep_ref, out_ref, slots, send, recv):
        x, y, c = lax.axis_index("x"), lax.axis_index("y"), lax.axis_index("c")
        me = 4 * x + 2 * y + c
        slots[0] = in_ref[...]
        cps = []
        for k in range(1, N_DEV):
            peer = (x ^ (k >> 2), y ^ ((k >> 1) & 1), c ^ (k & 1))
            cp = pltpu.make_async_remote_copy(src_ref=in_ref, dst_ref=slots.at[k], send_sem=send.at[k - 1],
                                              recv_sem=recv.at[k - 1], device_id=peer, device_id_type=MESH)
            cp.start()
            cps.append(cp)
        for cp in cps:
            cp.wait()
        total = slots[me]
        for a in range(1, N_DEV):
            total = total + slots[jnp.bitwise_xor(a, me)]
        out_ref[...] = total

    vmem = pl.BlockSpec(memory_space=pltpu.VMEM)
    return pl.pallas_call(
        body, name=name, in_specs=[vmem, ANY], out_specs=vmem, out_shape=jax.ShapeDtypeStruct((R, W), F32),
        scratch_shapes=[pltpu.VMEM((N_DEV, R, W), F32), pltpu.SemaphoreType.DMA((N_DEV - 1,)), pltpu.SemaphoreType.DMA((N_DEV - 1,))],
        compiler_params=pltpu.CompilerParams(has_side_effects=True),
    )(pack, dep)


def _heads(a, n_heads):
    S = a.shape[0]
    return a.reshape(S, n_heads, a.shape[1] // n_heads).transpose(1, 0, 2)


def _unheads(a):
    H, S, dh = a.shape
    return a.transpose(1, 0, 2).reshape(S, H * dh)


def _ffn_bwd(tag, xin, gain, wgu3, wd, saved, dxout, reduce_start, dep, flush=None):
    h, gu, act = saved
    D = xin.shape[1]
    dxo_b = dxout.astype(BF16)
    tok = reduce_start({f"w_down{tag}": _mm_tn(f"dw_down_{tag}", act, dxo_b, 0.5, dep=dep).reshape(N_CHIPS, -1, D)})
    dgu = _ffn_down_bwd(f"ffn_down_bwd_{tag}", dxo_b, wd, gu, 0.5, dep=tok)
    tok = reduce_start({f"w_gu{tag}": _mm_tn_cols(f"dw_gu_{tag}", h, dgu, wgu3.shape[2], b_is_gu=True)})
    if flush is not None:
        tok = flush(tok)
    dh = _mm_nt_cols(f"ffn_up_bwd_{tag}", dgu, wgu3, a_is_gu=True, dep=tok)
    dxin, dgain = _rms_bwd(f"rms_bwd_{tag}", xin, gain, dh, dxout)
    return dxin, dgain, tok


def kernel(x, g_ffn1, w_gu1, w_down1, g_mix, w_in, conv_w, q_norm_g, k_norm_g, sinks, w_out_conv, w_out_attn, w_o, g_ffn2, w_gu2, w_down2, loss_target, m_g_ffn1, m_w_gu1, m_w_down1, m_g_mix, m_w_in, m_conv_w, m_q_norm_g, m_k_norm_g, m_sinks, m_w_out_conv, m_w_out_attn, m_w_o, m_g_ffn2, m_w_gu2, m_w_down2, v_g_ffn1, v_w_gu1, v_w_down1, v_g_mix, v_w_in, v_conv_w, v_q_norm_g, v_k_norm_g, v_sinks, v_w_out_conv, v_w_out_attn, v_w_o, v_g_ffn2, v_w_gu2, v_w_down2):
    S, D = x.shape[1], x.shape[2]
    dh = q_norm_g.shape[1]
    HQ = sinks.shape[1]
    HKV = HQ // 4
    AW, KVW, CW = HQ * dh, HKV * dh, D // 2
    off_q, off_k, off_v = 3 * CW, 3 * CW + AW, 3 * CW + AW + KVW
    off_ga, off_gb = off_v + KVW, off_v + KVW + D
    x0, target = x[0], loss_target[0]
    cx, cy, cc = lax.axis_index("x"), lax.axis_index("y"), lax.axis_index("c")
    chip = 2 * cx + cy
    p_arr = jnp.reshape(chip, (1,)).astype(jnp.int32)
    c_arr = jnp.reshape(cc, (1,)).astype(jnp.int32)
    cp_arr = jnp.stack([cc, chip]).astype(jnp.int32)
    wts = dict(g_ffn1=g_ffn1, w_gu1=w_gu1, w_down1=w_down1, g_mix=g_mix, w_in=w_in, conv_w=conv_w, q_norm_g=q_norm_g,
               k_norm_g=k_norm_g, sinks=sinks, w_out_conv=w_out_conv, w_out_attn=w_out_attn, w_o=w_o, g_ffn2=g_ffn2,
               w_gu2=w_gu2, w_down2=w_down2)
    ms = dict(g_ffn1=m_g_ffn1, w_gu1=m_w_gu1, w_down1=m_w_down1, g_mix=m_g_mix, w_in=m_w_in, conv_w=m_conv_w,
              q_norm_g=m_q_norm_g, k_norm_g=m_k_norm_g, sinks=m_sinks, w_out_conv=m_w_out_conv, w_out_attn=m_w_out_attn,
              w_o=m_w_o, g_ffn2=m_g_ffn2, w_gu2=m_w_gu2, w_down2=m_w_down2)
    vs = dict(g_ffn1=v_g_ffn1, w_gu1=v_w_gu1, w_down1=v_w_down1, g_mix=v_g_mix, w_in=v_w_in, conv_w=v_conv_w,
              q_norm_g=v_q_norm_g, k_norm_g=v_k_norm_g, sinks=v_sinks, w_out_conv=v_w_out_conv, w_out_attn=v_w_out_attn,
              w_o=v_w_o, g_ffn2=v_g_ffn2, w_gu2=v_w_gu2, w_down2=v_w_down2)
    order = list(wts)
    small_names = [k for k in order if not k.startswith("w_")]
    grad, delta, new_m, new_v = {}, {}, {}, {}

    def cast(keys, dep=None):
        return [_cast_to_slot(f"cast_{k}", wts[k][0], F32 if k == "conv_w" else BF16, p_arr, dep) for k in keys]

    def gather_start(tag, slots, dep):
        send, recv, slots, tok = _gather_start(f"gather_start_{tag}", slots, dep)
        return (tag, send, recv, slots), tok

    def gather_finish(started, after):
        tag, send, recv, slots = started
        return _gather_forward(f"gather_forward_{tag}", _gather_wait(f"gather_wait_{tag}", send, recv, slots, after))

    swapping, pending = [], []

    def reduce_start(full):
        keys = list(full)
        send, recv, gs, lands, tok = _swap_start(f"swap_start_{keys[0]}", [full[k] for k in keys])
        if swapping:
            tok = reduce_advance(tok)
        swapping.append((keys, send, recv, gs, lands))
        return tok

    def reduce_advance(after):
        keys, send, recv, gs, lands = swapping.pop(0)
        gs, sib = _swap_wait(f"swap_wait_{keys[0]}", send, recv, gs, lands, after)
        parts = [_add_half(f"add_half_{k}", g, r, c_arr) for k, g, r in zip(keys, gs, sib)]
        send, recv, parts, lands, tok = _exchange_start(f"exchange_start_{keys[0]}", parts)
        pending.append((keys, send, recv, parts, lands))
        return tok

    def reduce_finish(entries, after):
        keys_all, halves = [], []
        for keys, send, recv, parts, lands in entries:
            parts, lands = _exchange_wait(f"exchange_wait_{keys[0]}", send, recv, parts, lands, after)
            halves += [_add_chips(f"add_chips_{k}", t, r, cp_arr) for k, t, r in zip(keys, parts, lands)]
            keys_all += keys
        prev = None
        for k, g2 in zip(keys_all, _join_halves(f"join_{keys_all[0]}", halves)):
            d, nm, nv = prev = _adamw(f"adamw_{k}", wts[k][0], g2, ms[k][0], vs[k][0], None if prev is None else prev[2])
            grad[k], delta[k], new_m[k], new_v[k] = g2[None], d[None], nm[None], nv[None]
        return nv

    keys_mix, keys_2 = ["w_in", "w_out_conv", "w_out_attn", "w_o", "conv_w"], ["w_gu2", "w_down2"]
    st_gu1, tok = gather_start("gu1", cast(["w_gu1"]), x0)
    st_d1, tok = gather_start("d1", cast(["w_down1"]), tok)
    slots_mix, slots_2 = cast(keys_mix, tok), cast(keys_2, tok)
    wgu1, = gather_finish(st_gu1, slots_2[-1])
    st_mix, tok = gather_start("mix", slots_mix, wgu1)
    st_2, tok = gather_start("2", slots_2, tok)
    cos, sin, rm, rmt = _rope_consts(S, dh)
    sink_vec = sinks[0]

    h1 = _rms_fwd("rms_fwd_1", x0, g_ffn1, tok)
    gu1, act1 = _ffn_up("ffn_up_1", h1, wgu1)
    wd1 = gather_finish(st_d1, act1)[0].reshape(-1, D)
    x1 = _mm_res("ffn_down_1", act1, wd1, x0, 0.5)
    win3, woc3, woa3, wo, convw3 = gather_finish(st_mix, x1)
    wo = wo.reshape(-1, D)
    h2 = _rms_fwd("rms_fwd_mix", x1, g_mix)
    proj = _mm_cols("in_proj", h2, win3, F32)
    aconv = _conv_fwd("conv_fwd", proj, convw3, CW)
    ya = _mm_cols("out_conv", aconv, woc3, F32)
    q_raw = _heads(proj[:, off_q:off_q + AW], HQ)
    k_raw = _heads(proj[:, off_k:off_k + KVW], HKV)
    vh = _heads(proj[:, off_v:off_v + KVW], HKV).astype(BF16)
    qn = _qk_prep("q_prep", q_raw, q_norm_g, cos, sin, rm)
    kn = _qk_prep("k_prep", k_raw, k_norm_g, cos, sin, rm)
    oh = _attn_fwd("attn_fwd", qn, kn, vh, sink_vec)
    o = _unheads(oh)
    yb = _mm_cols("out_attn", o, woa3, F32)
    merged = _gate_fwd("gate_fwd", proj, ya, yb, off_ga, off_gb)
    x2 = _mm_res("mix_out", merged, wo, x1, 1.0)
    wgu2, wd2 = gather_finish(st_2, x2)
    wd2 = wd2.reshape(-1, D)
    h3 = _rms_fwd("rms_fwd_2", x2, g_ffn2)
    gu2, act2 = _ffn_up("ffn_up_2", h3, wgu2)
    x3 = _mm_res("ffn_down_2", act2, wd2, x2, 0.5)

    dy, loss_lanes = _loss_grad("loss_grad", x3, target)
    dx2, dg_ffn2, tok = _ffn_bwd("2", x2, g_ffn2, wgu2, wd2, (h3, gu2, act2), dy, reduce_start, None)
    dx2_b = dx2.astype(BF16)
    dmerged = _mm_nt("mix_out_bwd", dx2_b, wo, F32)
    tok = reduce_start(dict(w_o=_mm_tn("dw_o", merged, dx2_b, dep=tok).reshape(N_CHIPS, -1, D)))
    dga, dgb, dya, dyb = _gate_bwd("gate_bwd", proj, ya, yb, dmerged, off_ga, off_gb)
    daconv = _mm_nt_cols("out_conv_bwd", dya, woc3, dep=tok)
    dwoc = _mm_tn_cols("dw_out_conv", aconv, dya, woc3.shape[2])
    do = _mm_nt_cols("out_attn_bwd", dyb, woa3)
    dwoa = _mm_tn_cols("dw_out_attn", o, dyb, woa3.shape[2])
    tok = reduce_start(dict(w_out_conv=dwoc, w_out_attn=dwoa))
    dxc, dbg, dcg, dconvw = _conv_bwd("conv_bwd", proj, convw3, daconv, CW)
    dqn, dkn, dvh, dsink3 = _attn_bwd("attn_bwd", qn, kn, vh, sink_vec, _heads(do, HQ).astype(BF16))
    dq_raw, dqg = _qk_prep_bwd("q_prep_bwd", q_raw, q_norm_g, cos, sin, rmt, dqn)
    dk_raw, dkg = _qk_prep_bwd("k_prep_bwd", k_raw, k_norm_g, cos, sin, rmt, dkn)
    dproj = jnp.concatenate([dxc, dbg, dcg, _unheads(dq_raw), _unheads(dk_raw), _unheads(dvh).astype(BF16), dga, dgb], axis=1)
    dh2 = _mm_nt_cols("in_proj_bwd", dproj, win3, dep=tok)
    tok = reduce_start(dict(w_in=_mm_tn_cols("dw_in", h2, dproj, win3.shape[2])))
    dx1, dg_mix = _rms_bwd("rms_bwd_mix", x1, g_mix, dh2, dx2)
    dx0, dg_ffn1, tok = _ffn_bwd("1", x0, g_ffn1, wgu1, wd1, (h1, gu1, act1), dx1, reduce_start, tok, reduce_advance)

    def rows8(a):
        a = a.reshape(-1, a.shape[-1])
        return jnp.pad(a, ((0, -a.shape[0] % 8), (0, D - a.shape[1])))

    misc = jnp.concatenate([dqg, dkg, dsink3[:, :, 0].reshape(1, HQ), loss_lanes], axis=1)
    done = reduce_finish(pending[:-2], dx0)
    tot = _allreduce_small("allreduce_small", jnp.concatenate([rows8(a) for a in (dg_ffn1, dg_mix, dg_ffn2, dconvw, misc)], axis=0), done)
    reduce_finish(pending[-2:], tot)

    cw_s = conv_w.shape[2]
    conv_row0, misc_row = 24, 24 + (-(-N_CHIPS * CONV_K // 8)) * 8
    small_g = dict(g_ffn1=tot[0:1], g_mix=tot[8:9], g_ffn2=tot[16:17],
                   conv_w=lax.dynamic_slice(tot, (conv_row0 + CONV_K * chip, 0), (CONV_K, cw_s)),
                   q_norm_g=tot[misc_row:misc_row + 1, 0:dh], k_norm_g=tot[misc_row:misc_row + 1, dh:2 * dh],
                   sinks=tot[misc_row:misc_row + 1, 2 * dh:2 * dh + HQ])
    loss = (0.5 / D) * jnp.sum(tot[misc_row, 2 * dh + HQ:2 * dh + HQ + LANES])

    def small_pack(src):
        return jnp.concatenate([rows8(src[k]) for k in small_names], axis=0)

    sd, sm, sv = _adamw("adamw_small", small_pack(wts), small_pack(small_g), small_pack(ms), small_pack(vs))
    for i, k in enumerate(small_names):
        shape = wts[k].shape
        nr, ncol = math.prod(shape[:-1]), shape[-1]
        grad[k] = small_g[k].reshape(shape)
        delta[k], new_m[k], new_v[k] = (a[8 * i:8 * i + nr, 0:ncol].reshape(shape) for a in (sd, sm, sv))
    return (loss, dx0[None], *[grad[k] for k in order], *[delta[k] for k in order],
            *[new_m[k] for k in order], *[new_v[k] for k in order])
```

```python
import math

import numpy as np
import jax
import jax.numpy as jnp
from jax import lax
from jax.experimental import pallas as pl
from jax.experimental.pallas import tpu as pltpu

F32 = jnp.float32
BF16 = jnp.bfloat16
MESH = pl.DeviceIdType.MESH

RMS_EPS = 1e-6
BLOCK = 128
ROPE_THETA = 500000.0
NEG_INF = -1e30
CONV_K = 3
ADAM_LR, ADAM_B1, ADAM_B2, ADAM_EPS, ADAM_WD, ADAM_STEP = 0.001, 0.9, 0.999, 1e-08, 0.01, 10

VMEM_LIMIT_V7X = 56 * 1024 * 1024
LANES = 128
N_CHIPS = 4
N_DEV = 8


def _tile(n, want, align=LANES):
    best = None
    t = align
    while t <= min(n, want):
        if n % t == 0:
            best = t
        t += align
    return best or n


def _cparams(sem):
    return pltpu.CompilerParams(dimension_semantics=sem, vmem_limit_bytes=VMEM_LIMIT_V7X)


def _sigmoid(x):
    return 1.0 / (1.0 + jnp.exp(-x))


NN = (((1,), (0,)), ((), ()))
NT = (((1,), (1,)), ((), ()))
TN = (((0,), (0,)), ((), ()))


def _mm(name, grid, ins, in_specs, compute, out_shape, out_specs, epilogue, dep=None):
    if dep is not None:
        ins, in_specs = tuple(ins) + (dep,), list(in_specs) + [pl.BlockSpec(dep.shape, lambda *_: (0, 0))]
    n_in = len(ins)

    def body(*refs):
        epilogue(compute(refs[:n_in]), refs[:n_in], refs[n_in:])

    return pl.pallas_call(
        body, name=name, grid=grid, in_specs=in_specs, out_specs=out_specs, out_shape=out_shape,
        compiler_params=_cparams(("parallel", "arbitrary")),
    )(*ins)


def _dot(dims, a=0, b=1):
    return lambda refs: [lax.dot_general(refs[a][...], refs[b][...], dims, preferred_element_type=F32)]


def _ffn_up(name, h, wgu3):
    S, D = h.shape
    Ns = wgu3.shape[2]
    F = 2 * Ns
    tm, tn = _tile(S, 512), _tile(Ns, 1408)
    nbs = Ns // tn

    def compute(refs):
        hv = refs[0][...]
        return [jnp.dot(hv, refs[1][...], preferred_element_type=F32), jnp.dot(hv, refs[2][...], preferred_element_type=F32)]

    def epi(accs, in_refs, out_refs):
        g, u = accs
        dgu_ref, a_ref = out_refs
        sg = _sigmoid(g)
        silu = g * sg
        dgu_ref[0] = (u * (sg * (1.0 + g * (1.0 - sg)))).astype(BF16)
        dgu_ref[1] = silu.astype(BF16)
        a_ref[...] = (silu * u).astype(BF16)

    return _mm(
        name, (F // tn, S // tm), (h, wgu3, wgu3),
        [pl.BlockSpec((tm, D), lambda j, i: (i, 0)),
         pl.BlockSpec((None, D, tn), lambda j, i: (j // nbs, 0, j % nbs)),
         pl.BlockSpec((None, D, tn), lambda j, i: (2 + j // nbs, 0, j % nbs))],
        compute, (jax.ShapeDtypeStruct((2, S, F), BF16), jax.ShapeDtypeStruct((S, F), BF16)),
        (pl.BlockSpec((2, tm, tn), lambda j, i: (0, i, j)), pl.BlockSpec((tm, tn), lambda j, i: (i, j))), epi)


def _mm_res(name, a, w, res, scale):
    S, K = a.shape
    N = w.shape[1]
    tm, tn = _tile(S, 512), _tile(N, 512 if K > 2816 else 1024)

    def epi(accs, in_refs, out_refs):
        out_refs[0][...] = in_refs[2][...] + scale * accs[0]

    return _mm(
        name, (N // tn, S // tm), (a, w, res),
        [pl.BlockSpec((tm, K), lambda j, i: (i, 0)), pl.BlockSpec((K, tn), lambda j, i: (0, j)),
         pl.BlockSpec((tm, tn), lambda j, i: (i, j))],
        _dot(NN), jax.ShapeDtypeStruct((S, N), F32), pl.BlockSpec((tm, tn), lambda j, i: (i, j)), epi)


def _mm_cols(name, a, w3, out_dtype):
    S, K = a.shape
    Ns = w3.shape[2]
    tm, tn = _tile(S, 512), _tile(Ns, 2304)
    nbs = Ns // tn

    def epi(accs, in_refs, out_refs):
        out_refs[0][...] = accs[0].astype(out_dtype)

    return _mm(
        name, (N_CHIPS * nbs, S // tm), (a, w3),
        [pl.BlockSpec((tm, K), lambda j, i: (i, 0)),
         pl.BlockSpec((None, K, tn), lambda j, i: (j // nbs, 0, j % nbs))],
        _dot(NN), jax.ShapeDtypeStruct((S, N_CHIPS * Ns), out_dtype), pl.BlockSpec((tm, tn), lambda j, i: (i, j)), epi)


def _mm_nt(name, a, w, out_dtype, scale=1.0):
    S, N = a.shape
    K = w.shape[0]
    tm, tn = _tile(S, 512), _tile(K, 1024)

    def epi(accs, in_refs, out_refs):
        out_refs[0][...] = (scale * accs[0]).astype(out_dtype)

    return _mm(
        name, (K // tn, S // tm), (a, w),
        [pl.BlockSpec((tm, N), lambda j, i: (i, 0)), pl.BlockSpec((tn, N), lambda j, i: (j, 0))],
        _dot(NT), jax.ShapeDtypeStruct((S, K), out_dtype), pl.BlockSpec((tm, tn), lambda j, i: (i, j)), epi)


def _ffn_down_bwd(name, dy, wd, gu, scale, dep=None):
    S, D = dy.shape
    F = wd.shape[0]
    tm, tn = _tile(S, 512), _tile(F, 1408)

    def epi(accs, in_refs, out_refs):
        da = scale * accs[0]
        out_refs[0][0] = (da * in_refs[2][0].astype(F32)).astype(BF16)
        out_refs[0][1] = (da * in_refs[2][1].astype(F32)).astype(BF16)

    return _mm(
        name, (F // tn, S // tm), (dy, wd, gu),
        [pl.BlockSpec((tm, D), lambda j, i: (i, 0)), pl.BlockSpec((tn, D), lambda j, i: (j, 0)),
         pl.BlockSpec((2, tm, tn), lambda j, i: (0, i, j))],
        _dot(NT), jax.ShapeDtypeStruct((2, S, F), BF16), pl.BlockSpec((2, tm, tn), lambda j, i: (0, i, j)), epi, dep=dep)


def _mm_nt_cols(name, a, w3, a_is_gu=False, dep=None):
    K, Ns = w3.shape[1], w3.shape[2]
    S = a.shape[1] if a_is_gu else a.shape[0]
    tm = _tile(S, 512)
    tn = _tile(K, max(LANES, (6 << 20) // (N_CHIPS * Ns * 2)))
    if a_is_gu:
        a_spec = pl.BlockSpec((2, tm, 2 * Ns), lambda i, j: (0, i, 0))
        part = lambda a_ref, s: a_ref[s // 2, :, (s % 2) * Ns:(s % 2 + 1) * Ns]
    else:
        a_spec = pl.BlockSpec((tm, N_CHIPS * Ns), lambda i, j: (i, 0))
        part = lambda a_ref, s: a_ref[:, s * Ns:(s + 1) * Ns]

    def compute(refs):
        total = None
        for s in range(N_CHIPS):
            prod = lax.dot_general(part(refs[0], s), refs[1][s], NT, preferred_element_type=F32)
            total = prod if total is None else total + prod
        return [total]

    def epi(accs, in_refs, out_refs):
        out_refs[0][...] = accs[0]

    return _mm(
        name, (S // tm, K // tn), (a, w3), [a_spec, pl.BlockSpec((N_CHIPS, tn, Ns), lambda i, j: (0, j, 0))],
        compute, jax.ShapeDtypeStruct((S, K), F32), pl.BlockSpec((tm, tn), lambda i, j: (i, j)), epi, dep=dep)


def _mm_tn(name, a, b, scale=1.0, dep=None):
    S, K = a.shape
    N = b.shape[1]
    tm, tn = _tile(K, 512), _tile(N, 1024)

    def epi(accs, in_refs, out_refs):
        out_refs[0][...] = (scale * accs[0]).astype(BF16)

    return _mm(
        name, (N // tn, K // tm), (a, b),
        [pl.BlockSpec((S, tm), lambda j, i: (0, i)), pl.BlockSpec((S, tn), lambda j, i: (0, j))],
        _dot(TN), jax.ShapeDtypeStruct((K, N), BF16), pl.BlockSpec((tm, tn), lambda j, i: (i, j)), epi, dep=dep)


def _mm_tn_cols(name, a, b, Ns, b_is_gu=False, dep=None):
    S, K = a.shape
    tm, tn = _tile(K, 512), _tile(Ns, 2304)
    nbs = Ns // tn
    if b_is_gu:
        b_spec = pl.BlockSpec((None, S, tn), lambda j, i: (j // (2 * nbs), 0, j % (2 * nbs)))
    else:
        b_spec = pl.BlockSpec((S, tn), lambda j, i: (0, j))

    def epi(accs, in_refs, out_refs):
        out_refs[0][...] = accs[0].astype(BF16)

    return _mm(
        name, (N_CHIPS * nbs, K // tm), (a, b), [pl.BlockSpec((S, tm), lambda j, i: (0, i)), b_spec],
        _dot(TN), jax.ShapeDtypeStruct((N_CHIPS, K, Ns), BF16),
        pl.BlockSpec((None, tm, tn), lambda j, i: (j // nbs, i, j % nbs)), epi, dep=dep)


def _rms_fwd(name, x, gain, dep=None):
    S, D = x.shape
    tm = _tile(S, 256, 8)
    extra = () if dep is None else (dep,)

    def body(x_ref, g_ref, *rest):
        h_ref = rest[-1]
        xv = x_ref[...]
        r = lax.rsqrt(jnp.mean(xv * xv, axis=-1, keepdims=True) + RMS_EPS)
        h_ref[...] = (xv * r * g_ref[...]).astype(BF16)

    return pl.pallas_call(
        body, name=name, grid=(S // tm,),
        in_specs=[pl.BlockSpec((tm, D), lambda i: (i, 0)), pl.BlockSpec((1, D), lambda i: (0, 0))]
        + [pl.BlockSpec(d.shape, lambda i: (0, 0)) for d in extra],
        out_specs=pl.BlockSpec((tm, D), lambda i: (i, 0)), out_shape=jax.ShapeDtypeStruct((S, D), BF16),
        compiler_params=_cparams(("parallel",)),
    )(x, gain, *extra)


def _rms_bwd(name, x, gain, dh, dres):
    S, D = x.shape
    tm = _tile(S, 256, 8)

    def body(x_ref, g_ref, dh_ref, dres_ref, dx_ref, dxb_ref, dg_ref):
        i = pl.program_id(0)
        xv = x_ref[...]
        r = lax.rsqrt(jnp.mean(xv * xv, axis=-1, keepdims=True) + RMS_EPS)
        xhat = xv * r
        dhv = dh_ref[...]
        dxhat = dhv * g_ref[...]
        dx = dres_ref[...] + r * (dxhat - xhat * jnp.mean(dxhat * xhat, axis=-1, keepdims=True))
        dx_ref[...] = dx
        dxb_ref[...] = dx.astype(BF16)

        @pl.when(i == 0)
        def _():
            dg_ref[...] = jnp.zeros_like(dg_ref)

        dg_ref[...] += jnp.sum(dhv * xhat, axis=0, keepdims=True)

    row = pl.BlockSpec((tm, D), lambda i: (i, 0))
    vec = pl.BlockSpec((1, D), lambda i: (0, 0))
    return pl.pallas_call(
        body, name=name, grid=(S // tm,), in_specs=[row, vec, row, row], out_specs=(row, row, vec),
        out_shape=(jax.ShapeDtypeStruct((S, D), F32), jax.ShapeDtypeStruct((S, D), BF16), jax.ShapeDtypeStruct((1, D), F32)),
        compiler_params=_cparams(("arbitrary",)),
    )(x, gain, dh, dres)


def _loss_grad(name, y, target):
    S, D = y.shape
    tm = _tile(S, 256, 8)

    def body(y_ref, t_ref, dy_ref, dyb_ref, l_ref):
        i = pl.program_id(0)
        e = y_ref[...] - t_ref[...]
        dy_ref[...] = e * (1.0 / D)
        dyb_ref[...] = (e * (1.0 / D)).astype(BF16)
        col = jnp.sum(e * e, axis=0, keepdims=True)
        part = col[:, 0:LANES]
        for k in range(1, D // LANES):
            part = part + col[:, k * LANES:(k + 1) * LANES]

        @pl.when(i == 0)
        def _():
            l_ref[...] = jnp.zeros_like(l_ref)

        l_ref[...] += part

    row = pl.BlockSpec((tm, D), lambda i: (i, 0))
    return pl.pallas_call(
        body, name=name, grid=(S // tm,), in_specs=[row, row],
        out_specs=(row, row, pl.BlockSpec((1, LANES), lambda i: (0, 0))),
        out_shape=(jax.ShapeDtypeStruct((S, D), F32), jax.ShapeDtypeStruct((S, D), BF16), jax.ShapeDtypeStruct((1, LANES), F32)),
        compiler_params=_cparams(("arbitrary",)),
    )(y, target)


def _shift_down(u, k):
    rows = lax.broadcasted_iota(jnp.int32, u.shape, 0)
    return jnp.where(rows >= k, pltpu.roll(u, k, 0), 0.0)


def _shift_up(u, k):
    n = u.shape[0]
    rows = lax.broadcasted_iota(jnp.int32, u.shape, 0)
    return jnp.where(rows < n - k, pltpu.roll(u, n - k, 0), 0.0)


def _conv_specs(S, cw, conv_width):
    nb = conv_width // cw
    col = lambda off: pl.BlockSpec((S, cw), lambda j, off=off: (0, off * nb + j))
    return nb, col(0), col(1), col(2)


def _conv_fwd(name, proj, convw3, conv_width):
    S = proj.shape[0]
    cw = convw3.shape[2]
    nb, xc_s, bg_s, cg_s = _conv_specs(S, cw, conv_width)

    def body(xc_ref, bg_ref, cg_ref, w_ref, o_ref):
        u = cg_ref[...] * xc_ref[...]
        w = w_ref[...]
        cv = w[2:3, :] * u + w[1:2, :] * _shift_down(u, 1) + w[0:1, :] * _shift_down(u, 2)
        o_ref[...] = (bg_ref[...] * cv).astype(BF16)

    return pl.pallas_call(
        body, name=name, grid=(nb,),
        in_specs=[xc_s, bg_s, cg_s, pl.BlockSpec((None, CONV_K, cw), lambda j: (j, 0, 0))],
        out_specs=pl.BlockSpec((S, cw), lambda j: (0, j)), out_shape=jax.ShapeDtypeStruct((S, conv_width), BF16),
        compiler_params=_cparams(("parallel",)),
    )(proj, proj, proj, convw3)


def _conv_bwd(name, proj, convw3, da, conv_width):
    S = proj.shape[0]
    cw = convw3.shape[2]
    nb, xc_s, bg_s, cg_s = _conv_specs(S, cw, conv_width)

    def body(xc_ref, bg_ref, cg_ref, w_ref, da_ref, dxc_ref, dbg_ref, dcg_ref, dw_ref):
        xc, cg = xc_ref[...], cg_ref[...]
        u = cg * xc
        w = w_ref[...]
        u1, u2 = _shift_down(u, 1), _shift_down(u, 2)
        cv = w[2:3, :] * u + w[1:2, :] * u1 + w[0:1, :] * u2
        dav = da_ref[...]
        dbg_ref[...] = (dav * cv).astype(BF16)
        dcv = dav * bg_ref[...]
        du = w[2:3, :] * dcv + w[1:2, :] * _shift_up(dcv, 1) + w[0:1, :] * _shift_up(dcv, 2)
        dxc_ref[...] = (du * cg).astype(BF16)
        dcg_ref[...] = (du * xc).astype(BF16)
        dw_ref[0:1, :] = jnp.sum(dcv * u2, axis=0, keepdims=True)
        dw_ref[1:2, :] = jnp.sum(dcv * u1, axis=0, keepdims=True)
        dw_ref[2:3, :] = jnp.sum(dcv * u, axis=0, keepdims=True)

    wspec = pl.BlockSpec((None, CONV_K, cw), lambda j: (j, 0, 0))
    ospec = pl.BlockSpec((S, cw), lambda j: (0, j))
    act = jax.ShapeDtypeStruct((S, conv_width), BF16)
    return pl.pallas_call(
        body, name=name, grid=(nb,), in_specs=[xc_s, bg_s, cg_s, wspec, ospec],
        out_specs=(ospec, ospec, ospec, wspec),
        out_shape=(act, act, act, jax.ShapeDtypeStruct(convw3.shape, F32)),
        compiler_params=_cparams(("parallel",)),
    )(proj, proj, proj, convw3, da)


def _rope_consts(S, dh):
    rot = dh // 4
    half = rot // 2
    inv_freq = 1.0 / (ROPE_THETA ** (jnp.arange(0, rot, 2, dtype=F32) / rot))
    ang = jnp.arange(S, dtype=F32)[:, None] * inv_freq[None, :]
    cos = jnp.concatenate([jnp.cos(ang), jnp.cos(ang), jnp.ones((S, dh - rot), F32)], axis=1)
    sin = jnp.concatenate([jnp.sin(ang), jnp.sin(ang), jnp.zeros((S, dh - rot), F32)], axis=1)
    rm = np.zeros((dh, dh), np.float32)
    for j in range(half):
        rm[j + half, j] = -1.0
        rm[j, j + half] = 1.0
    return cos, sin, jnp.asarray(rm, BF16), jnp.asarray(rm.T, BF16)


def _exact_perm(y, rm):
    hi = y.astype(BF16)
    r1 = y - hi.astype(F32)
    mid = r1.astype(BF16)
    lo = (r1 - mid.astype(F32)).astype(BF16)
    dot = lambda a: jnp.dot(a, rm, preferred_element_type=F32)
    return dot(hi) + dot(mid) + dot(lo)


def _qk_prep(name, xh, gain, cos, sin, rm):
    H, S, dh = xh.shape
    tm = _tile(S, 1024, 8)

    def body(x_ref, g_ref, c_ref, s_ref, rm_ref, o_ref):
        xv = x_ref[...]
        y = xv * lax.rsqrt(jnp.mean(xv * xv, axis=-1, keepdims=True) + RMS_EPS) * g_ref[...]
        o_ref[...] = (y * c_ref[...] + _exact_perm(y, rm_ref[...]) * s_ref[...]).astype(BF16)

    blk = pl.BlockSpec((None, tm, dh), lambda h, i: (h, i, 0))
    tab = pl.BlockSpec((tm, dh), lambda h, i: (i, 0))
    return pl.pallas_call(
        body, name=name, grid=(H, S // tm),
        in_specs=[blk, pl.BlockSpec((1, dh), lambda h, i: (0, 0)), tab, tab, pl.BlockSpec((dh, dh), lambda h, i: (0, 0))],
        out_specs=blk, out_shape=jax.ShapeDtypeStruct((H, S, dh), BF16),
        compiler_params=_cparams(("parallel", "parallel")),
    )(xh, gain, cos, sin, rm)


def _qk_prep_bwd(name, xh, gain, cos, sin, rmt, dout):
    H, S, dh = xh.shape
    tm = _tile(S, 1024, 8)

    def body(x_ref, g_ref, c_ref, s_ref, rmt_ref, do_ref, dx_ref, dg_ref):
        first = (pl.program_id(0) == 0) & (pl.program_id(1) == 0)
        xv = x_ref[...]
        r = lax.rsqrt(jnp.mean(xv * xv, axis=-1, keepdims=True) + RMS_EPS)
        xhat = xv * r
        dov = do_ref[...]
        dy = dov * c_ref[...] + _exact_perm(dov * s_ref[...], rmt_ref[...])
        dxhat = dy * g_ref[...]
        dx_ref[...] = (r * (dxhat - xhat * jnp.mean(dxhat * xhat, axis=-1, keepdims=True))).astype(BF16)

        @pl.when(first)
        def _():
            dg_ref[...] = jnp.zeros_like(dg_ref)

        dg_ref[...] += jnp.sum(dy * xhat, axis=0, keepdims=True)

    blk = pl.BlockSpec((None, tm, dh), lambda h, i: (h, i, 0))
    tab = pl.BlockSpec((tm, dh), lambda h, i: (i, 0))
    vec = pl.BlockSpec((1, dh), lambda h, i: (0, 0))
    return pl.pallas_call(
        body, name=name, grid=(H, S // tm),
        in_specs=[blk, vec, tab, tab, pl.BlockSpec((dh, dh), lambda h, i: (0, 0)), blk],
        out_specs=(blk, vec), out_shape=(jax.ShapeDtypeStruct((H, S, dh), BF16), jax.ShapeDtypeStruct((1, dh), F32)),
        compiler_params=_cparams(("arbitrary", "arbitrary")),
    )(xh, gain, cos, sin, rmt, dout)


def _attn_probs(q, kp, kc, sink_col, n, scale):
    rows = q.shape[0]
    sp = lax.dot_general(q, kp, NT, preferred_element_type=F32) * scale
    sc = lax.dot_general(q, kc, NT, preferred_element_type=F32) * scale
    qi = lax.broadcasted_iota(jnp.int32, (rows, BLOCK), 0) % BLOCK
    kj = lax.broadcasted_iota(jnp.int32, (rows, BLOCK), 1)
    sp = jnp.where((kj > qi) & (n > 0), sp, NEG_INF)
    sc = jnp.where(kj <= qi, sc, NEG_INF)
    m = jnp.maximum(jnp.maximum(jnp.max(sp, axis=-1, keepdims=True), jnp.max(sc, axis=-1, keepdims=True)), sink_col)
    pp, pc, ps = jnp.exp(sp - m), jnp.exp(sc - m), jnp.exp(sink_col - m)
    inv = 1.0 / (jnp.sum(pp, axis=-1, keepdims=True) + jnp.sum(pc, axis=-1, keepdims=True) + ps)
    return pp * inv, pc * inv, ps * inv


def _sink_col(sink_ref, hk, group):
    rows = group * BLOCK
    g = lax.broadcasted_iota(jnp.int32, (rows, 1), 0) // BLOCK
    col = jnp.zeros((rows, 1), F32)
    for i in range(group):
        col = jnp.where(g == i, sink_ref[hk * group + i], col)
    return col


def _attn_specs(group, dh):
    qb = pl.BlockSpec((group, BLOCK, dh), lambda hk, n: (hk, n, 0))
    prev = pl.BlockSpec((None, BLOCK, dh), lambda hk, n: (hk, jnp.maximum(n - 1, 0), 0))
    cur = pl.BlockSpec((None, BLOCK, dh), lambda hk, n: (hk, n, 0))
    return qb, prev, cur, pl.BlockSpec(memory_space=pltpu.SMEM)


def _attn_fwd(name, q, k, v, sinks):
    HQ, S, dh = q.shape
    HKV = k.shape[0]
    group = HQ // HKV
    scale = dh ** -0.5
    qb, prev, cur, smem = _attn_specs(group, dh)

    def body(q_ref, kp_ref, kc_ref, vp_ref, vc_ref, sink_ref, o_ref):
        hk, n = pl.program_id(0), pl.program_id(1)
        qv = q_ref[...].reshape(group * BLOCK, dh)
        pp, pc, _ = _attn_probs(qv, kp_ref[...], kc_ref[...], _sink_col(sink_ref, hk, group), n, scale)
        o = jnp.dot(pp.astype(BF16), vp_ref[...], preferred_element_type=F32)
        o = o + jnp.dot(pc.astype(BF16), vc_ref[...], preferred_element_type=F32)
        o_ref[...] = o.reshape(group, BLOCK, dh).astype(BF16)

    return pl.pallas_call(
        body, name=name, grid=(HKV, S // BLOCK), in_specs=[qb, prev, cur, prev, cur, smem], out_specs=qb,
        out_shape=jax.ShapeDtypeStruct((HQ, S, dh), BF16), compiler_params=_cparams(("parallel", "parallel")),
    )(q, k, k, v, v, sinks)


def _attn_bwd(name, q, k, v, sinks, do):
    HQ, S, dh = q.shape
    HKV = k.shape[0]
    group = HQ // HKV
    scale = dh ** -0.5
    qb, prev, cur, smem = _attn_specs(group, dh)
    whole = pl.BlockSpec((None, S, dh), lambda hk, n: (hk, 0, 0))
    sk = pl.BlockSpec((None, group, LANES), lambda hk, n: (hk, 0, 0))

    def body(q_ref, kp_ref, kc_ref, vp_ref, vc_ref, sink_ref, do_ref, dq_ref, dk_ref, dv_ref, ds_ref):
        hk, n = pl.program_id(0), pl.program_id(1)
        rows = group * BLOCK
        qv = q_ref[...].reshape(rows, dh)
        dov = do_ref[...].reshape(rows, dh)
        kp, kc, vp, vc = kp_ref[...], kc_ref[...], vp_ref[...], vc_ref[...]
        pp, pc, ps = _attn_probs(qv, kp, kc, _sink_col(sink_ref, hk, group), n, scale)
        dpp = lax.dot_general(dov, vp, NT, preferred_element_type=F32)
        dpc = lax.dot_general(dov, vc, NT, preferred_element_type=F32)
        delta = jnp.sum(pp * dpp, axis=-1, keepdims=True) + jnp.sum(pc * dpc, axis=-1, keepdims=True)
        dsp = (pp * (dpp - delta) * scale).astype(BF16)
        dsc = (pc * (dpc - delta) * scale).astype(BF16)
        dq = jnp.dot(dsp, kp, preferred_element_type=F32) + jnp.dot(dsc, kc, preferred_element_type=F32)
        dq_ref[...] = dq.reshape(group, BLOCK, dh)

        @pl.when(n == 0)
        def _():
            dk_ref[...] = jnp.zeros_like(dk_ref)
            dv_ref[...] = jnp.zeros_like(dv_ref)
            ds_ref[...] = jnp.zeros_like(ds_ref)

        cur_rows = pl.ds(pl.multiple_of(n * BLOCK, BLOCK), BLOCK)
        prev_rows = pl.ds(pl.multiple_of(jnp.maximum(n - 1, 0) * BLOCK, BLOCK), BLOCK)
        tdot = lambda a, b: lax.dot_general(a, b, TN, preferred_element_type=F32)
        dk_ref[prev_rows, :] += tdot(dsp, qv)
        dv_ref[prev_rows, :] += tdot(pp.astype(BF16), dov)
        dk_ref[cur_rows, :] += tdot(dsc, qv)
        dv_ref[cur_rows, :] += tdot(pc.astype(BF16), dov)
        dsink = -jnp.sum((ps * delta).reshape(group, BLOCK, 1), axis=1)
        ds_ref[...] += jnp.broadcast_to(dsink, (group, LANES))

    return pl.pallas_call(
        body, name=name, grid=(HKV, S // BLOCK), in_specs=[qb, prev, cur, prev, cur, smem, qb],
        out_specs=(qb, whole, whole, sk),
        out_shape=(jax.ShapeDtypeStruct((HQ, S, dh), F32), jax.ShapeDtypeStruct((HKV, S, dh), F32),
                   jax.ShapeDtypeStruct((HKV, S, dh), F32), jax.ShapeDtypeStruct((HKV, group, LANES), F32)),
        compiler_params=_cparams(("arbitrary", "arbitrary")),
    )(q, k, k, v, v, sinks, do)


def _gate_specs(S, D, ga_off, gb_off):
    tg = LANES
    for t in range(LANES, 513, LANES):
        if D % t == 0 and ga_off % t == 0 and gb_off % t == 0:
            tg = t
    if D % LANES:
        tg = math.gcd(math.gcd(D, ga_off), gb_off)
    tm = _tile(S, 512, 8)
    act = pl.BlockSpec((tm, tg), lambda i, j: (i, j))
    ga = pl.BlockSpec((tm, tg), lambda i, j: (i, ga_off // tg + j))
    gb = pl.BlockSpec((tm, tg), lambda i, j: (i, gb_off // tg + j))
    return (S // tm, D // tg), act, ga, gb


def _gate_fwd(name, proj, ya, yb, ga_off, gb_off):
    S, D = ya.shape
    grid, act, ga, gb = _gate_specs(S, D, ga_off, gb_off)

    def body(ga_ref, gb_ref, ya_ref, yb_ref, o_ref):
        o_ref[...] = (_sigmoid(ga_ref[...]) * ya_ref[...] + _sigmoid(gb_ref[...]) * yb_ref[...]).astype(BF16)

    return pl.pallas_call(
        body, name=name, grid=grid, in_specs=[ga, gb, act, act], out_specs=act,
        out_shape=jax.ShapeDtypeStruct((S, D), BF16), compiler_params=_cparams(("parallel", "parallel")),
    )(proj, proj, ya, yb)


def _gate_bwd(name, proj, ya, yb, dm, ga_off, gb_off):
    S, D = ya.shape
    grid, act, ga, gb = _gate_specs(S, D, ga_off, gb_off)

    def body(ga_ref, gb_ref, ya_ref, yb_ref, dm_ref, dga_ref, dgb_ref, dya_ref, dyb_ref):
        dmv = dm_ref[...]
        sa, sb = _sigmoid(ga_ref[...]), _sigmoid(gb_ref[...])
        dga_ref[...] = (dmv * ya_ref[...] * sa * (1.0 - sa)).astype(BF16)
        dgb_ref[...] = (dmv * yb_ref[...] * sb * (1.0 - sb)).astype(BF16)
        dya_ref[...] = (dmv * sa).astype(BF16)
        dyb_ref[...] = (dmv * sb).astype(BF16)

    o = jax.ShapeDtypeStruct((S, D), BF16)
    return pl.pallas_call(
        body, name=name, grid=grid, in_specs=[ga, gb, act, act, act], out_specs=(act, act, act, act),
        out_shape=(o, o, o, o), compiler_params=_cparams(("parallel", "parallel")),
    )(proj, proj, ya, yb, dm)


ANY = pl.BlockSpec(memory_space=pl.ANY)


def _row_tile(rows, cols, n_arrays):
    want = max(16, (VMEM_LIMIT_V7X // 2) // (2 * n_arrays * cols * 4))
    return _tile(rows, want, 16)


def _cast_to_slot(name, w, dtype, p_arr, dep=None):
    R, C = w.shape
    tr = _row_tile(R, C, 2)
    extra = () if dep is None else (dep,)

    def body(p_ref, w_ref, *rest):
        rest[-1][...] = w_ref[...].astype(dtype)

    return pl.pallas_call(
        body, name=name,
        grid_spec=pltpu.PrefetchScalarGridSpec(
            num_scalar_prefetch=1, grid=(R // tr,),
            in_specs=[pl.BlockSpec((tr, C), lambda i, p_ref: (i, 0))] + [pl.BlockSpec(d.shape, lambda i, p_ref: (0, 0)) for d in extra],
            out_specs=pl.BlockSpec((None, tr, C), lambda i, p_ref: (p_ref[0], i, 0))),
        out_shape=jax.ShapeDtypeStruct((N_CHIPS, R, C), dtype), compiler_params=_cparams(("parallel",)),
    )(p_arr, w, *extra)


def _add_half(name, g3, r3, c_arr):
    n, h, C = r3.shape
    tr = _row_tile(h, C, 3)
    nb = h // tr

    def body(c_ref, g_ref, r_ref, o_ref):
        o_ref[...] = (g_ref[...].astype(F32) + r_ref[...].astype(F32)).astype(BF16)

    blk = pl.BlockSpec((None, tr, C), lambda s, i, c_ref: (s, i, 0))
    return pl.pallas_call(
        body, name=name,
        grid_spec=pltpu.PrefetchScalarGridSpec(
            num_scalar_prefetch=1, grid=(n, nb),
            in_specs=[pl.BlockSpec((None, tr, C), lambda s, i, c_ref: (s, c_ref[0] * nb + i, 0)), blk], out_specs=blk),
        out_shape=jax.ShapeDtypeStruct(r3.shape, BF16), compiler_params=_cparams(("parallel", "parallel")),
    )(c_arr, g3, r3)


def _add_chips(name, t3, r3, cp_arr):
    n, h, C = r3.shape
    tr = _row_tile(h, C, 6)
    nb = h // tr

    def body(cp_ref, t_ref, r0_ref, r1_ref, r2_ref, r3_ref, o_ref):
        p = cp_ref[1]
        total = None
        for a, r_ref in enumerate((r0_ref, r1_ref, r2_ref, r3_ref)):
            part = jnp.where(p == a, t_ref[...], r_ref[...]).astype(F32)
            total = part if total is None else total + part
        o_ref[...] = total

    def part(a):
        return pl.BlockSpec((None, tr, C), lambda i, cp_ref: (jnp.where(cp_ref[1] == a, (a + 1) % N_CHIPS, a), i, 0))

    return pl.pallas_call(
        body, name=name,
        grid_spec=pltpu.PrefetchScalarGridSpec(
            num_scalar_prefetch=1, grid=(nb,),
            in_specs=[pl.BlockSpec((None, tr, C), lambda i, cp_ref: (cp_ref[1], i, 0)), part(0), part(1), part(2), part(3)],
            out_specs=pl.BlockSpec((tr, C), lambda i, cp_ref: (cp_ref[0] * nb + i, 0))),
        out_shape=jax.ShapeDtypeStruct((2 * h, C), F32), compiler_params=_cparams(("parallel",)),
    )(cp_arr, t3, r3, r3, r3, r3)


def _adamw(name, w, g, m, v, dep=None):
    R, C = w.shape
    extra = () if dep is None else (dep,)
    tr = _row_tile(R, C, 8)
    c1 = 1.0 - ADAM_B1 ** ADAM_STEP
    c2 = 1.0 - ADAM_B2 ** ADAM_STEP

    def body(w_ref, g_ref, m_ref, v_ref, *rest):
        go_ref, d_ref, nm_ref, nv_ref = rest[-4:]
        gv = g_ref[...]
        go_ref[...] = gv
        nm = ADAM_B1 * m_ref[...] + (1.0 - ADAM_B1) * gv
        nv = ADAM_B2 * v_ref[...] + (1.0 - ADAM_B2) * (gv * gv)
        d_ref[...] = -ADAM_LR * ((nm / c1) / (jnp.sqrt(nv / c2) + ADAM_EPS) + ADAM_WD * w_ref[...])
        nm_ref[...] = nm
        nv_ref[...] = nv

    blk = pl.BlockSpec((tr, C), lambda i: (i, 0))
    o = jax.ShapeDtypeStruct((R, C), F32)
    return pl.pallas_call(
        body, name=name, grid=(R // tr,), in_specs=[blk, blk, blk, blk] + [ANY] * len(extra), out_specs=(blk, blk, blk, blk),
        out_shape=(o, o, o, o), compiler_params=_cparams(("parallel",)),
    )(w, g, m, v, *extra)


def _place():
    x, y, c = lax.axis_index("x"), lax.axis_index("y"), lax.axis_index("c")
    chips = [(1 - x, y), (x, 1 - y), (1 - x, 1 - y)]
    return x, y, c, 2 * x + y, chips


HBM = pl.BlockSpec(memory_space=pltpu.HBM)
SEM = pl.BlockSpec(memory_space=pltpu.SEMAPHORE)
TOKEN = jax.ShapeDtypeStruct((8, LANES), F32)
DATAFLOW = pltpu.SideEffectType.DATAFLOW_SIDE_EFFECTING


def _hbm(a):
    return pltpu.with_memory_space_constraint(a, pltpu.HBM)


def _gather_blocks(bufs, i, c, p, chips):
    if bufs[i].shape[1] % 16:
        return bufs[i].at[p], [bufs[i].at[2 * cx + cy] for cx, cy in chips]
    h = bufs[i].shape[1] // 2
    rows = pl.ds(pl.multiple_of(c * h, 16), h)
    return bufs[i].at[p, rows], [bufs[i].at[2 * cx + cy, rows] for cx, cy in chips]


def _gather_start(name, slots, dep):
    n = len(slots)

    def body(*refs):
        bufs, send, recv, token = refs[:n], refs[n + 1], refs[n + 2], refs[-1]
        x, y, c, p, chips = _place()
        for i in range(n):
            mine, _ = _gather_blocks(bufs, i, c, p, chips)
            for j, chip in enumerate(chips):
                pltpu.make_async_remote_copy(src_ref=mine, dst_ref=mine, send_sem=send.at[3 * i + j], recv_sem=recv.at[3 * i + j],
                                             device_id=(*chip, c), device_id_type=MESH).start()
        token[...] = jnp.zeros_like(token)

    out = pl.pallas_call(
        body, name=name, in_specs=[HBM] * n + [ANY],
        out_specs=(SEM, SEM, *([HBM] * n), pl.BlockSpec(memory_space=pltpu.VMEM)),
        out_shape=(pltpu.SemaphoreType.DMA((3 * n,)), pltpu.SemaphoreType.DMA((3 * n,)),
                   *[pltpu.HBM(s.shape, s.dtype) for s in slots], TOKEN),
        input_output_aliases={i: 2 + i for i in range(n)},
        compiler_params=pltpu.CompilerParams(has_side_effects=DATAFLOW),
    )(*[_hbm(s) for s in slots], dep)
    return out[0], out[1], list(out[2:2 + n]), out[-1]


def _gather_wait(name, send, recv, slots, after):
    n = len(slots)

    def body(*refs):
        bufs, send, recv = refs[:n], refs[n], refs[n + 1]
        x, y, c, p, chips = _place()
        for i in range(n):
            mine, landed = _gather_blocks(bufs, i, c, p, chips)
            for j, chip in enumerate(chips):
                cp = pltpu.make_async_remote_copy(src_ref=mine, dst_ref=landed[j], send_sem=send.at[3 * i + j],
                                                  recv_sem=recv.at[3 * i + j], device_id=(*chip, c), device_id_type=MESH)
                cp.wait_send()
                cp.wait_recv()

    return list(pl.pallas_call(
        body, name=name, in_specs=[HBM] * n + [SEM, SEM, ANY], out_specs=tuple([HBM] * n),
        out_shape=tuple(pltpu.HBM(s.shape, s.dtype) for s in slots),
        input_output_aliases={i: i for i in range(n)},
        compiler_params=pltpu.CompilerParams(has_side_effects=DATAFLOW),
    )(*slots, send, recv, after))


def _gather_forward(name, slots):
    idx = [i for i, s in enumerate(slots) if s.shape[1] % 16 == 0]
    n = len(slots)

    def body(*refs):
        bufs = refs[n:2 * n]
        send, recv = refs[2 * n:]
        x, y, c, p, chips = _place()

        def rdma(k, ref):
            return pltpu.make_async_remote_copy(src_ref=ref, dst_ref=ref, send_sem=send.at[k], recv_sem=recv.at[k],
                                                device_id=(x, y, 1 - c), device_id_type=MESH)

        cps = []
        for k, i in enumerate(idx):
            for j, ref in enumerate(_gather_blocks(bufs, i, c, p, chips)[1]):
                cps.append(rdma(3 * k + j, ref))
                cps[-1].start()
        for k, i in enumerate(idx):
            for j, ref in enumerate(_gather_blocks(bufs, i, 1 - c, p, chips)[1]):
                rdma(3 * k + j, ref).wait_recv()
        for cp in cps:
            cp.wait_send()

    return list(pl.pallas_call(
        body, name=name, in_specs=[ANY] * n, out_specs=tuple([ANY] * n),
        out_shape=tuple(jax.ShapeDtypeStruct(s.shape, s.dtype) for s in slots),
        scratch_shapes=[pltpu.SemaphoreType.DMA((3 * len(idx),)), pltpu.SemaphoreType.DMA((3 * len(idx),))],
        input_output_aliases={i: i for i in range(n)},
        compiler_params=pltpu.CompilerParams(has_side_effects=True),
    )(*slots))


def _swap_copy(grads, lands, send, recv, i, x, y, c):
    h = grads[i].shape[1] // 2
    other = pl.ds(pl.multiple_of((1 - c) * h, 16), h)
    return pltpu.make_async_remote_copy(src_ref=grads[i].at[:, other, :], dst_ref=lands[i], send_sem=send.at[i],
                                        recv_sem=recv.at[i], device_id=(x, y, 1 - c), device_id_type=MESH)


def _swap_start(name, grads):
    n = len(grads)

    def body(*refs):
        ins, lands, send, recv, token = refs[:n], refs[n:2 * n], refs[2 * n], refs[2 * n + 1], refs[-1]
        x, y, c, p, chips = _place()
        for i in range(n):
            _swap_copy(ins, lands, send, recv, i, x, y, c).start()
        token[...] = jnp.zeros_like(token)

    gshapes = [pltpu.HBM(g.shape, g.dtype) for g in grads]
    halves = [(g.shape[0], g.shape[1] // 2, g.shape[2]) for g in grads]
    lshapes = [pltpu.HBM(s, g.dtype) for s, g in zip(halves, grads)]
    out = pl.pallas_call(
        body, name=name, in_specs=[HBM] * (2 * n),
        out_specs=(SEM, SEM, *([HBM] * (2 * n)), pl.BlockSpec(memory_space=pltpu.VMEM)),
        out_shape=(pltpu.SemaphoreType.DMA((n,)), pltpu.SemaphoreType.DMA((n,)), *gshapes, *lshapes, TOKEN),
        input_output_aliases={i: 2 + i for i in range(2 * n)},
        compiler_params=pltpu.CompilerParams(has_side_effects=DATAFLOW),
    )(*[_hbm(g) for g in grads], *[_hbm(lax.empty(s, g.dtype)) for s, g in zip(halves, grads)])
    return out[0], out[1], list(out[2:2 + n]), list(out[2 + n:2 + 2 * n]), out[-1]


def _swap_wait(name, send, recv, grads, lands, after):
    n = len(grads)

    def body(*refs):
        ins, lands, send, recv = refs[:n], refs[n:2 * n], refs[2 * n], refs[2 * n + 1]
        x, y, c, p, chips = _place()
        for i in range(n):
            cp = _swap_copy(ins, lands, send, recv, i, x, y, c)
            cp.wait_send()
            cp.wait_recv()

    shapes = [pltpu.HBM(t.shape, t.dtype) for t in list(grads) + list(lands)]
    out = pl.pallas_call(
        body, name=name, in_specs=[HBM] * (2 * n) + [SEM, SEM, ANY], out_specs=tuple([HBM] * (2 * n)),
        out_shape=tuple(shapes), input_output_aliases={i: i for i in range(2 * n)},
        compiler_params=pltpu.CompilerParams(has_side_effects=DATAFLOW),
    )(*grads, *lands, send, recv, after)
    return list(out[:n]), list(out[n:])


def _exchange_start(name, parts):
    n = len(parts)

    def body(*refs):
        ins, lands, send, recv, token = refs[:n], refs[n:2 * n], refs[2 * n], refs[2 * n + 1], refs[-1]
        x, y, c, p, chips = _place()
        for i in range(n):
            for j, (cx, cy) in enumerate(chips):
                pltpu.make_async_remote_copy(src_ref=ins[i].at[2 * cx + cy], dst_ref=lands[i].at[p], send_sem=send.at[3 * i + j],
                                             recv_sem=recv.at[3 * i + j], device_id=(cx, cy, c), device_id_type=MESH).start()
        token[...] = jnp.zeros_like(token)

    shapes = [pltpu.HBM(t.shape, t.dtype) for t in parts]
    out = pl.pallas_call(
        body, name=name, in_specs=[HBM] * (2 * n),
        out_specs=(SEM, SEM, *([HBM] * (2 * n)), pl.BlockSpec(memory_space=pltpu.VMEM)),
        out_shape=(pltpu.SemaphoreType.DMA((3 * n,)), pltpu.SemaphoreType.DMA((3 * n,)), *shapes, *shapes, TOKEN),
        input_output_aliases={i: 2 + i for i in range(2 * n)},
        compiler_params=pltpu.CompilerParams(has_side_effects=DATAFLOW),
    )(*[_hbm(t) for t in parts], *[_hbm(lax.empty(t.shape, t.dtype)) for t in parts])
    return out[0], out[1], list(out[2:2 + n]), list(out[2 + n:2 + 2 * n]), out[-1]


def _exchange_wait(name, send, recv, parts, lands, after):
    n = len(parts)

    def body(*refs):
        ins, lands, send, recv = refs[:n], refs[n:2 * n], refs[2 * n], refs[2 * n + 1]
        x, y, c, p, chips = _place()
        for i in range(n):
            for j, (cx, cy) in enumerate(chips):
                q = 2 * cx + cy
                cp = pltpu.make_async_remote_copy(src_ref=ins[i].at[q], dst_ref=lands[i].at[q], send_sem=send.at[3 * i + j],
                                                  recv_sem=recv.at[3 * i + j], device_id=(cx, cy, c), device_id_type=MESH)
                cp.wait_send()
                cp.wait_recv()

    shapes = [pltpu.HBM(t.shape, t.dtype) for t in parts]
    out = pl.pallas_call(
        body, name=name, in_specs=[HBM] * (2 * n) + [SEM, SEM, ANY], out_specs=tuple([HBM] * (2 * n)),
        out_shape=(*shapes, *shapes), input_output_aliases={i: i for i in range(2 * n)},
        compiler_params=pltpu.CompilerParams(has_side_effects=DATAFLOW),
    )(*parts, *lands, send, recv, after)
    return list(out[:n]), list(out[n:])


def _join_halves(name, bufs):
    n = len(bufs)

    def body(*refs):
        outs = refs[n:2 * n]
        send, recv = refs[2 * n:]
        x, y, c, p, chips = _place()

        def rdma(i, which):
            h = outs[i].shape[0] // 2
            rows = outs[i].at[pl.ds(pl.multiple_of(which * h, 8), h)]
            return pltpu.make_async_remote_copy(src_ref=rows, dst_ref=rows, send_sem=send.at[i], recv_sem=recv.at[i],
                                                device_id=(x, y, 1 - c), device_id_type=MESH)

        cps = [rdma(i, c) for i in range(n)]
        for cp in cps:
            cp.start()
        for i, cp in enumerate(cps):
            rdma(i, 1 - c).wait_recv()
            cp.wait_send()

    return pl.pallas_call(
        body, name=name, in_specs=[ANY] * n, out_specs=tuple([ANY] * n),
        out_shape=tuple(jax.ShapeDtypeStruct(t.shape, t.dtype) for t in bufs),
        scratch_shapes=[pltpu.SemaphoreType.DMA((n,)), pltpu.SemaphoreType.DMA((n,))],
        input_output_aliases={i: i for i in range(n)},
        compiler_params=pltpu.CompilerParams(has_side_effects=True),
    )(*bufs)


def _allreduce_small(name, pack, dep):
    R, W = pack.shape

    def body(in_ref, dep_ref, out_ref, slots, send, recv):
        x, y, c = lax.axis_index("x"), lax.axis_index("y"), lax.axis_index("c")
        me = 4 * x + 2 * y + c
        slots[0] = in_ref[...]
        cps = []
        for k in range(1, N_DEV):
            peer = (x ^ (k >> 2), y ^ ((k >> 1) & 1), c ^ (k & 1))
            cp = pltpu.make_async_remote_copy(src_ref=in_ref, dst_ref=slots.at[k], send_sem=send.at[k - 1],
                                              recv_sem=recv.at[k - 1], device_id=peer, device_id_type=MESH)
            cp.start()
            cps.append(cp)
        for cp in cps:
            cp.wait()
        total = slots[me]
        for a in range(1, N_DEV):
            total = total + slots[jnp.bitwise_xor(a, me)]
        out_ref[...] = total

    vmem = pl.BlockSpec(memory_space=pltpu.VMEM)
    return pl.pallas_call(
        body, name=name, in_specs=[vmem, ANY], out_specs=vmem, out_shape=jax.ShapeDtypeStruct((R, W), F32),
        scratch_shapes=[pltpu.VMEM((N_DEV, R, W), F32), pltpu.SemaphoreType.DMA((N_DEV - 1,)), pltpu.SemaphoreType.DMA((N_DEV - 1,))],
        compiler_params=pltpu.CompilerParams(has_side_effects=True),
    )(pack, dep)


def _heads(a, n_heads):
    S = a.shape[0]
    return a.reshape(S, n_heads, a.shape[1] // n_heads).transpose(1, 0, 2)


def _unheads(a):
    H, S, dh = a.shape
    return a.transpose(1, 0, 2).reshape(S, H * dh)


def _ffn_bwd(tag, xin, gain, wgu3, wd, saved, dxout, dxo_b, reduce_start, dep, flush=None):
    h, gu, act = saved
    D = xin.shape[1]
    tok = reduce_start({f"w_down{tag}": _mm_tn(f"dw_down_{tag}", act, dxo_b, 0.5, dep=dep).reshape(N_CHIPS, -1, D)})
    dgu = _ffn_down_bwd(f"ffn_down_bwd_{tag}", dxo_b, wd, gu, 0.5, dep=tok)
    tok = reduce_start({f"w_gu{tag}": _mm_tn_cols(f"dw_gu_{tag}", h, dgu, wgu3.shape[2], b_is_gu=True)})
    if flush is not None:
        tok = flush(tok)
    dh = _mm_nt_cols(f"ffn_up_bwd_{tag}", dgu, wgu3, a_is_gu=True, dep=tok)
    dxin, dxin_b, dgain = _rms_bwd(f"rms_bwd_{tag}", xin, gain, dh, dxout)
    return dxin, dxin_b, dgain, tok


def kernel(x, g_ffn1, w_gu1, w_down1, g_mix, w_in, conv_w, q_norm_g, k_norm_g, sinks, w_out_conv, w_out_attn, w_o, g_ffn2, w_gu2, w_down2, loss_target, m_g_ffn1, m_w_gu1, m_w_down1, m_g_mix, m_w_in, m_conv_w, m_q_norm_g, m_k_norm_g, m_sinks, m_w_out_conv, m_w_out_attn, m_w_o, m_g_ffn2, m_w_gu2, m_w_down2, v_g_ffn1, v_w_gu1, v_w_down1, v_g_mix, v_w_in, v_conv_w, v_q_norm_g, v_k_norm_g, v_sinks, v_w_out_conv, v_w_out_attn, v_w_o, v_g_ffn2, v_w_gu2, v_w_down2):
    S, D = x.shape[1], x.shape[2]
    dh = q_norm_g.shape[1]
    HQ = sinks.shape[1]
    HKV = HQ // 4
    AW, KVW, CW = HQ * dh, HKV * dh, D // 2
    off_q, off_k, off_v = 3 * CW, 3 * CW + AW, 3 * CW + AW + KVW
    off_ga, off_gb = off_v + KVW, off_v + KVW + D
    x0, target = x[0], loss_target[0]
    cx, cy, cc = lax.axis_index("x"), lax.axis_index("y"), lax.axis_index("c")
    chip = 2 * cx + cy
    p_arr = jnp.reshape(chip, (1,)).astype(jnp.int32)
    c_arr = jnp.reshape(cc, (1,)).astype(jnp.int32)
    cp_arr = jnp.stack([cc, chip]).astype(jnp.int32)
    wts = dict(g_ffn1=g_ffn1, w_gu1=w_gu1, w_down1=w_down1, g_mix=g_mix, w_in=w_in, conv_w=conv_w, q_norm_g=q_norm_g,
               k_norm_g=k_norm_g, sinks=sinks, w_out_conv=w_out_conv, w_out_attn=w_out_attn, w_o=w_o, g_ffn2=g_ffn2,
               w_gu2=w_gu2, w_down2=w_down2)
    ms = dict(g_ffn1=m_g_ffn1, w_gu1=m_w_gu1, w_down1=m_w_down1, g_mix=m_g_mix, w_in=m_w_in, conv_w=m_conv_w,
              q_norm_g=m_q_norm_g, k_norm_g=m_k_norm_g, sinks=m_sinks, w_out_conv=m_w_out_conv, w_out_attn=m_w_out_attn,
              w_o=m_w_o, g_ffn2=m_g_ffn2, w_gu2=m_w_gu2, w_down2=m_w_down2)
    vs = dict(g_ffn1=v_g_ffn1, w_gu1=v_w_gu1, w_down1=v_w_down1, g_mix=v_g_mix, w_in=v_w_in, conv_w=v_conv_w,
              q_norm_g=v_q_norm_g, k_norm_g=v_k_norm_g, sinks=v_sinks, w_out_conv=v_w_out_conv, w_out_attn=v_w_out_attn,
              w_o=v_w_o, g_ffn2=v_g_ffn2, w_gu2=v_w_gu2, w_down2=v_w_down2)
    order = list(wts)
    small_names = [k for k in order if not k.startswith("w_")]
    grad, delta, new_m, new_v = {}, {}, {}, {}

    def cast(keys, dep=None):
        return [_cast_to_slot(f"cast_{k}", wts[k][0], F32 if k == "conv_w" else BF16, p_arr, dep) for k in keys]

    def gather_start(tag, slots, dep):
        send, recv, slots, tok = _gather_start(f"gather_start_{tag}", slots, dep)
        return (tag, send, recv, slots), tok

    def gather_finish(started, after):
        tag, send, recv, slots = started
        return _gather_forward(f"gather_forward_{tag}", _gather_wait(f"gather_wait_{tag}", send, recv, slots, after))

    swapping, pending = [], []

    def reduce_start(full):
        keys = list(full)
        send, recv, gs, lands, tok = _swap_start(f"swap_start_{keys[0]}", [full[k] for k in keys])
        if swapping:
            tok = reduce_advance(tok)
        swapping.append((keys, send, recv, gs, lands))
        return tok

    def reduce_advance(after):
        keys, send, recv, gs, lands = swapping.pop(0)
        gs, sib = _swap_wait(f"swap_wait_{keys[0]}", send, recv, gs, lands, after)
        parts = [_add_half(f"add_half_{k}", g, r, c_arr) for k, g, r in zip(keys, gs, sib)]
        send, recv, parts, lands, tok = _exchange_start(f"exchange_start_{keys[0]}", parts)
        pending.append((keys, send, recv, parts, lands))
        return tok

    def reduce_finish(entries, after):
        keys_all, halves = [], []
        for keys, send, recv, parts, lands in entries:
            parts, lands = _exchange_wait(f"exchange_wait_{keys[0]}", send, recv, parts, lands, after)
            halves += [_add_chips(f"add_chips_{k}", t, r, cp_arr) for k, t, r in zip(keys, parts, lands)]
            keys_all += keys
        prev = None
        for k, g2 in zip(keys_all, _join_halves(f"join_{keys_all[0]}", halves)):
            g2, d, nm, nv = prev = _adamw(f"adamw_{k}", wts[k][0], g2, ms[k][0], vs[k][0], None if prev is None else prev[3])
            grad[k], delta[k], new_m[k], new_v[k] = g2[None], d[None], nm[None], nv[None]
        return nv

    st_gu1, tok = gather_start("gu1", cast(["w_gu1"]), x0)
    st_d1, tok = gather_start("d1", cast(["w_down1"]), tok)
    later = ["w_in", "conv_w", "w_out_conv", "w_out_attn", "w_o", "w_gu2", "w_down2"]
    slot = dict(zip(later, cast(later, tok)))
    wgu1, = gather_finish(st_gu1, slot["w_down2"])
    st_in, tok = gather_start("in", [slot["w_in"], slot["conv_w"]], wgu1)
    st_out, tok = gather_start("out", [slot["w_out_conv"], slot["w_out_attn"], slot["w_o"]], tok)
    st_gu2, tok = gather_start("gu2", [slot["w_gu2"]], tok)
    st_d2, tok = gather_start("d2", [slot["w_down2"]], tok)
    cos, sin, rm, rmt = _rope_consts(S, dh)
    sink_vec = sinks[0]

    h1 = _rms_fwd("rms_fwd_1", x0, g_ffn1, tok)
    gu1, act1 = _ffn_up("ffn_up_1", h1, wgu1)
    wd1 = gather_finish(st_d1, act1)[0].reshape(-1, D)
    x1 = _mm_res("ffn_down_1", act1, wd1, x0, 0.5)
    win3, convw3 = gather_finish(st_in, x1)
    h2 = _rms_fwd("rms_fwd_mix", x1, g_mix)
    proj = _mm_cols("in_proj", h2, win3, F32)
    aconv = _conv_fwd("conv_fwd", proj, convw3, CW)
    woc3, woa3, wo = gather_finish(st_out, aconv)
    wo = wo.reshape(-1, D)
    ya = _mm_cols("out_conv", aconv, woc3, F32)
    q_raw = _heads(proj[:, off_q:off_q + AW], HQ)
    k_raw = _heads(proj[:, off_k:off_k + KVW], HKV)
    vh = _heads(proj[:, off_v:off_v + KVW], HKV).astype(BF16)
    qn = _qk_prep("q_prep", q_raw, q_norm_g, cos, sin, rm)
    kn = _qk_prep("k_prep", k_raw, k_norm_g, cos, sin, rm)
    oh = _attn_fwd("attn_fwd", qn, kn, vh, sink_vec)
    o = _unheads(oh)
    yb = _mm_cols("out_attn", o, woa3, F32)
    merged = _gate_fwd("gate_fwd", proj, ya, yb, off_ga, off_gb)
    x2 = _mm_res("mix_out", merged, wo, x1, 1.0)
    wgu2, = gather_finish(st_gu2, x2)
    h3 = _rms_fwd("rms_fwd_2", x2, g_ffn2)
    gu2, act2 = _ffn_up("ffn_up_2", h3, wgu2)
    wd2 = gather_finish(st_d2, act2)[0].reshape(-1, D)
    x3 = _mm_res("ffn_down_2", act2, wd2, x2, 0.5)

    dy, dy_b, loss_lanes = _loss_grad("loss_grad", x3, target)
    dx2, dx2_b, dg_ffn2, tok = _ffn_bwd("2", x2, g_ffn2, wgu2, wd2, (h3, gu2, act2), dy, dy_b, reduce_start, None)
    dmerged = _mm_nt("mix_out_bwd", dx2_b, wo, F32)
    tok = reduce_start(dict(w_o=_mm_tn("dw_o", merged, dx2_b, dep=tok).reshape(N_CHIPS, -1, D)))
    dga, dgb, dya, dyb = _gate_bwd("gate_bwd", proj, ya, yb, dmerged, off_ga, off_gb)
    daconv = _mm_nt_cols("out_conv_bwd", dya, woc3, dep=tok)
    dwoc = _mm_tn_cols("dw_out_conv", aconv, dya, woc3.shape[2])
    do = _mm_nt_cols("out_attn_bwd", dyb, woa3)
    dwoa = _mm_tn_cols("dw_out_attn", o, dyb, woa3.shape[2])
    tok = reduce_start(dict(w_out_conv=dwoc, w_out_attn=dwoa))
    dxc, dbg, dcg, dconvw = _conv_bwd("conv_bwd", proj, convw3, daconv, CW)
    dqn, dkn, dvh, dsink3 = _attn_bwd("attn_bwd", qn, kn, vh, sink_vec, _heads(do, HQ).astype(BF16))
    dq_raw, dqg = _qk_prep_bwd("q_prep_bwd", q_raw, q_norm_g, cos, sin, rmt, dqn)
    dk_raw, dkg = _qk_prep_bwd("k_prep_bwd", k_raw, k_norm_g, cos, sin, rmt, dkn)
    dproj = jnp.concatenate([dxc, dbg, dcg, _unheads(dq_raw), _unheads(dk_raw), _unheads(dvh).astype(BF16), dga, dgb], axis=1)
    dh2 = _mm_nt_cols("in_proj_bwd", dproj, win3, dep=tok)
    tok = reduce_start(dict(w_in=_mm_tn_cols("dw_in", h2, dproj, win3.shape[2])))
    dx1, dx1_b, dg_mix = _rms_bwd("rms_bwd_mix", x1, g_mix, dh2, dx2)
    dx0, _, dg_ffn1, tok = _ffn_bwd("1", x0, g_ffn1, wgu1, wd1, (h1, gu1, act1), dx1, dx1_b, reduce_start, tok, reduce_advance)

    def rows8(a):
        a = a.reshape(-1, a.shape[-1])
        return jnp.pad(a, ((0, -a.shape[0] % 8), (0, D - a.shape[1])))

    misc = jnp.concatenate([dqg, dkg, dsink3[:, :, 0].reshape(1, HQ), loss_lanes], axis=1)
    done = reduce_finish(pending[:-2], dx0)
    tot = _allreduce_small("allreduce_small", jnp.concatenate([rows8(a) for a in (dg_ffn1, dg_mix, dg_ffn2, dconvw, misc)], axis=0), done)
    reduce_finish(pending[-2:], tot)

    cw_s = conv_w.shape[2]
    conv_row0, misc_row = 24, 24 + (-(-N_CHIPS * CONV_K // 8)) * 8
    small_g = dict(g_ffn1=tot[0:1], g_mix=tot[8:9], g_ffn2=tot[16:17],
                   conv_w=lax.dynamic_slice(tot, (conv_row0 + CONV_K * chip, 0), (CONV_K, cw_s)),
                   q_norm_g=tot[misc_row:misc_row + 1, 0:dh], k_norm_g=tot[misc_row:misc_row + 1, dh:2 * dh],
                   sinks=tot[misc_row:misc_row + 1, 2 * dh:2 * dh + HQ])
    loss = (0.5 / D) * jnp.sum(tot[misc_row, 2 * dh + HQ:2 * dh + HQ + LANES])

    def small_pack(src):
        return jnp.concatenate([rows8(src[k]) for k in small_names], axis=0)

    _, sd, sm, sv = _adamw("adamw_small", small_pack(wts), small_pack(small_g), small_pack(ms), small_pack(vs))
    for i, k in enumerate(small_names):
        shape = wts[k].shape
        nr, ncol = math.prod(shape[:-1]), shape[-1]
        grad[k] = small_g[k].reshape(shape)
        delta[k], new_m[k], new_v[k] = (a[8 * i:8 * i + nr, 0:ncol].reshape(shape) for a in (sd, sm, sv))
    return (loss, dx0[None], *[grad[k] for k in order], *[delta[k] for k in order],
            *[new_m[k] for k in order], *[new_v[k] for k in order])
```

```python
import math

import numpy as np
import jax
import jax.numpy as jnp
from jax import lax
from jax.experimental import pallas as pl
from jax.experimental.pallas import tpu as pltpu

F32 = jnp.float32
BF16 = jnp.bfloat16
MESH = pl.DeviceIdType.MESH

RMS_EPS = 1e-6
BLOCK = 128
ROPE_THETA = 500000.0
NEG_INF = -1e30
CONV_K = 3
ADAM_LR, ADAM_B1, ADAM_B2, ADAM_EPS, ADAM_WD, ADAM_STEP = 0.001, 0.9, 0.999, 1e-08, 0.01, 10

VMEM_LIMIT_V7X = 56 * 1024 * 1024
LANES = 128
N_CHIPS = 4
N_DEV = 8


def _tile(n, want, align=LANES):
    best = None
    t = align
    while t <= min(n, want):
        if n % t == 0:
            best = t
        t += align
    return best or n


def _cparams(sem):
    return pltpu.CompilerParams(dimension_semantics=sem, vmem_limit_bytes=VMEM_LIMIT_V7X)


def _sigmoid(x):
    return 1.0 / (1.0 + jnp.exp(-x))


NN = (((1,), (0,)), ((), ()))
NT = (((1,), (1,)), ((), ()))
TN = (((0,), (0,)), ((), ()))


def _mm(name, grid, ins, in_specs, compute, out_shape, out_specs, epilogue, dep=None):
    if dep is not None:
        ins, in_specs = tuple(ins) + (dep,), list(in_specs) + [pl.BlockSpec(dep.shape, lambda *_: (0, 0))]
    n_in = len(ins)

    def body(*refs):
        epilogue(compute(refs[:n_in]), refs[:n_in], refs[n_in:])

    return pl.pallas_call(
        body, name=name, grid=grid, in_specs=in_specs, out_specs=out_specs, out_shape=out_shape,
        compiler_params=_cparams(("parallel", "arbitrary")),
    )(*ins)


def _dot(dims, a=0, b=1):
    return lambda refs: [lax.dot_general(refs[a][...], refs[b][...], dims, preferred_element_type=F32)]


def _ffn_up(name, h, wgu3):
    S, D = h.shape
    Ns = wgu3.shape[2]
    F = 2 * Ns
    tm, tn = _tile(S, 512), _tile(Ns, 1408)
    nbs = Ns // tn

    def compute(refs):
        hv = refs[0][...]
        return [jnp.dot(hv, refs[1][...], preferred_element_type=F32), jnp.dot(hv, refs[2][...], preferred_element_type=F32)]

    def epi(accs, in_refs, out_refs):
        g, u = accs
        dgu_ref, a_ref = out_refs
        sg = _sigmoid(g)
        silu = g * sg
        dgu_ref[0] = (u * (sg * (1.0 + g * (1.0 - sg)))).astype(BF16)
        dgu_ref[1] = silu.astype(BF16)
        a_ref[...] = (silu * u).astype(BF16)

    return _mm(
        name, (F // tn, S // tm), (h, wgu3, wgu3),
        [pl.BlockSpec((tm, D), lambda j, i: (i, 0)),
         pl.BlockSpec((None, D, tn), lambda j, i: (j // nbs, 0, j % nbs)),
         pl.BlockSpec((None, D, tn), lambda j, i: (2 + j // nbs, 0, j % nbs))],
        compute, (jax.ShapeDtypeStruct((2, S, F), BF16), jax.ShapeDtypeStruct((S, F), BF16)),
        (pl.BlockSpec((2, tm, tn), lambda j, i: (0, i, j)), pl.BlockSpec((tm, tn), lambda j, i: (i, j))), epi)


def _mm_res(name, a, w, res, scale):
    S, K = a.shape
    N = w.shape[1]
    tm, tn = _tile(S, 512), _tile(N, 512 if K > 2816 else 1024)

    def epi(accs, in_refs, out_refs):
        out_refs[0][...] = in_refs[2][...] + scale * accs[0]

    return _mm(
        name, (N // tn, S // tm), (a, w, res),
        [pl.BlockSpec((tm, K), lambda j, i: (i, 0)), pl.BlockSpec((K, tn), lambda j, i: (0, j)),
         pl.BlockSpec((tm, tn), lambda j, i: (i, j))],
        _dot(NN), jax.ShapeDtypeStruct((S, N), F32), pl.BlockSpec((tm, tn), lambda j, i: (i, j)), epi)


def _mm_cols(name, a, w3, out_dtype):
    S, K = a.shape
    Ns = w3.shape[2]
    tm, tn = _tile(S, 512), _tile(Ns, 2304)
    nbs = Ns // tn

    def epi(accs, in_refs, out_refs):
        out_refs[0][...] = accs[0].astype(out_dtype)

    return _mm(
        name, (N_CHIPS * nbs, S // tm), (a, w3),
        [pl.BlockSpec((tm, K), lambda j, i: (i, 0)),
         pl.BlockSpec((None, K, tn), lambda j, i: (j // nbs, 0, j % nbs))],
        _dot(NN), jax.ShapeDtypeStruct((S, N_CHIPS * Ns), out_dtype), pl.BlockSpec((tm, tn), lambda j, i: (i, j)), epi)


def _mm_nt(name, a, w, out_dtype, scale=1.0):
    S, N = a.shape
    K = w.shape[0]
    tm, tn = _tile(S, 512), _tile(K, 1024)

    def epi(accs, in_refs, out_refs):
        out_refs[0][...] = (scale * accs[0]).astype(out_dtype)

    return _mm(
        name, (K // tn, S // tm), (a, w),
        [pl.BlockSpec((tm, N), lambda j, i: (i, 0)), pl.BlockSpec((tn, N), lambda j, i: (j, 0))],
        _dot(NT), jax.ShapeDtypeStruct((S, K), out_dtype), pl.BlockSpec((tm, tn), lambda j, i: (i, j)), epi)


def _ffn_down_bwd(name, dy, wd, gu, scale, dep=None):
    S, D = dy.shape
    F = wd.shape[0]
    tm, tn = _tile(S, 512), _tile(F, 1408)

    def epi(accs, in_refs, out_refs):
        da = scale * accs[0]
        out_refs[0][0] = (da * in_refs[2][0].astype(F32)).astype(BF16)
        out_refs[0][1] = (da * in_refs[2][1].astype(F32)).astype(BF16)

    return _mm(
        name, (F // tn, S // tm), (dy, wd, gu),
        [pl.BlockSpec((tm, D), lambda j, i: (i, 0)), pl.BlockSpec((tn, D), lambda j, i: (j, 0)),
         pl.BlockSpec((2, tm, tn), lambda j, i: (0, i, j))],
        _dot(NT), jax.ShapeDtypeStruct((2, S, F), BF16), pl.BlockSpec((2, tm, tn), lambda j, i: (0, i, j)), epi, dep=dep)


def _mm_nt_cols(name, a, w3, a_is_gu=False, dep=None):
    K, Ns = w3.shape[1], w3.shape[2]
    S = a.shape[1] if a_is_gu else a.shape[0]
    tm = _tile(S, 512)
    tn = _tile(K, max(LANES, (6 << 20) // (N_CHIPS * Ns * 2)))
    if a_is_gu:
        a_spec = pl.BlockSpec((2, tm, 2 * Ns), lambda i, j: (0, i, 0))
        part = lambda a_ref, s: a_ref[s // 2, :, (s % 2) * Ns:(s % 2 + 1) * Ns]
    else:
        a_spec = pl.BlockSpec((tm, N_CHIPS * Ns), lambda i, j: (i, 0))
        part = lambda a_ref, s: a_ref[:, s * Ns:(s + 1) * Ns]

    def compute(refs):
        total = None
        for s in range(N_CHIPS):
            prod = lax.dot_general(part(refs[0], s), refs[1][s], NT, preferred_element_type=F32)
            total = prod if total is None else total + prod
        return [total]

    def epi(accs, in_refs, out_refs):
        out_refs[0][...] = accs[0]

    return _mm(
        name, (S // tm, K // tn), (a, w3), [a_spec, pl.BlockSpec((N_CHIPS, tn, Ns), lambda i, j: (0, j, 0))],
        compute, jax.ShapeDtypeStruct((S, K), F32), pl.BlockSpec((tm, tn), lambda i, j: (i, j)), epi, dep=dep)


def _mm_tn(name, a, b, scale=1.0, dep=None):
    S, K = a.shape
    N = b.shape[1]
    tm, tn = _tile(K, 512), _tile(N, 1024)

    def epi(accs, in_refs, out_refs):
        out_refs[0][...] = (scale * accs[0]).astype(BF16)

    return _mm(
        name, (N // tn, K // tm), (a, b),
        [pl.BlockSpec((S, tm), lambda j, i: (0, i)), pl.BlockSpec((S, tn), lambda j, i: (0, j))],
        _dot(TN), jax.ShapeDtypeStruct((K, N), BF16), pl.BlockSpec((tm, tn), lambda j, i: (i, j)), epi, dep=dep)


def _mm_tn_cols(name, a, b, Ns, b_is_gu=False, dep=None):
    S, K = a.shape
    tm, tn = _tile(K, 512), _tile(Ns, 2304)
    nbs = Ns // tn
    if b_is_gu:
        b_spec = pl.BlockSpec((None, S, tn), lambda j, i: (j // (2 * nbs), 0, j % (2 * nbs)))
    else:
        b_spec = pl.BlockSpec((S, tn), lambda j, i: (0, j))

    def epi(accs, in_refs, out_refs):
        out_refs[0][...] = accs[0].astype(BF16)

    return _mm(
        name, (N_CHIPS * nbs, K // tm), (a, b), [pl.BlockSpec((S, tm), lambda j, i: (0, i)), b_spec],
        _dot(TN), jax.ShapeDtypeStruct((N_CHIPS, K, Ns), BF16),
        pl.BlockSpec((None, tm, tn), lambda j, i: (j // nbs, i, j % nbs)), epi, dep=dep)


def _rms_fwd(name, x, gain, dep=None):
    S, D = x.shape
    tm = _tile(S, 256, 8)
    extra = () if dep is None else (dep,)

    def body(x_ref, g_ref, *rest):
        h_ref = rest[-1]
        xv = x_ref[...]
        r = lax.rsqrt(jnp.mean(xv * xv, axis=-1, keepdims=True) + RMS_EPS)
        h_ref[...] = (xv * r * g_ref[...]).astype(BF16)

    return pl.pallas_call(
        body, name=name, grid=(S // tm,),
        in_specs=[pl.BlockSpec((tm, D), lambda i: (i, 0)), pl.BlockSpec((1, D), lambda i: (0, 0))]
        + [pl.BlockSpec(d.shape, lambda i: (0, 0)) for d in extra],
        out_specs=pl.BlockSpec((tm, D), lambda i: (i, 0)), out_shape=jax.ShapeDtypeStruct((S, D), BF16),
        compiler_params=_cparams(("parallel",)),
    )(x, gain, *extra)


def _rms_bwd(name, x, gain, dh, dres):
    S, D = x.shape
    tm = _tile(S, 256, 8)

    def body(x_ref, g_ref, dh_ref, dres_ref, dx_ref, dxb_ref, dg_ref):
        i = pl.program_id(0)
        xv = x_ref[...]
        r = lax.rsqrt(jnp.mean(xv * xv, axis=-1, keepdims=True) + RMS_EPS)
        xhat = xv * r
        dhv = dh_ref[...]
        dxhat = dhv * g_ref[...]
        dx = dres_ref[...] + r * (dxhat - xhat * jnp.mean(dxhat * xhat, axis=-1, keepdims=True))
        dx_ref[...] = dx
        dxb_ref[...] = dx.astype(BF16)

        @pl.when(i == 0)
        def _():
            dg_ref[...] = jnp.zeros_like(dg_ref)

        dg_ref[...] += jnp.sum(dhv * xhat, axis=0, keepdims=True)

    row = pl.BlockSpec((tm, D), lambda i: (i, 0))
    vec = pl.BlockSpec((1, D), lambda i: (0, 0))
    return pl.pallas_call(
        body, name=name, grid=(S // tm,), in_specs=[row, vec, row, row], out_specs=(row, row, vec),
        out_shape=(jax.ShapeDtypeStruct((S, D), F32), jax.ShapeDtypeStruct((S, D), BF16), jax.ShapeDtypeStruct((1, D), F32)),
        compiler_params=_cparams(("arbitrary",)),
    )(x, gain, dh, dres)


def _loss_grad(name, y, target):
    S, D = y.shape
    tm = _tile(S, 256, 8)

    def body(y_ref, t_ref, dy_ref, dyb_ref, l_ref):
        i = pl.program_id(0)
        e = y_ref[...] - t_ref[...]
        dy_ref[...] = e * (1.0 / D)
        dyb_ref[...] = (e * (1.0 / D)).astype(BF16)
        col = jnp.sum(e * e, axis=0, keepdims=True)
        part = col[:, 0:LANES]
        for k in range(1, D // LANES):
            part = part + col[:, k * LANES:(k + 1) * LANES]

        @pl.when(i == 0)
        def _():
            l_ref[...] = jnp.zeros_like(l_ref)

        l_ref[...] += part

    row = pl.BlockSpec((tm, D), lambda i: (i, 0))
    return pl.pallas_call(
        body, name=name, grid=(S // tm,), in_specs=[row, row],
        out_specs=(row, row, pl.BlockSpec((1, LANES), lambda i: (0, 0))),
        out_shape=(jax.ShapeDtypeStruct((S, D), F32), jax.ShapeDtypeStruct((S, D), BF16), jax.ShapeDtypeStruct((1, LANES), F32)),
        compiler_params=_cparams(("arbitrary",)),
    )(y, target)


def _shift_down(u, k):
    rows = lax.broadcasted_iota(jnp.int32, u.shape, 0)
    return jnp.where(rows >= k, pltpu.roll(u, k, 0), 0.0)


def _shift_up(u, k):
    n = u.shape[0]
    rows = lax.broadcasted_iota(jnp.int32, u.shape, 0)
    return jnp.where(rows < n - k, pltpu.roll(u, n - k, 0), 0.0)


def _conv_specs(S, cw, conv_width):
    nb = conv_width // cw
    col = lambda off: pl.BlockSpec((S, cw), lambda j, off=off: (0, off * nb + j))
    return nb, col(0), col(1), col(2)


def _conv_fwd(name, proj, convw3, conv_width):
    S = proj.shape[0]
    cw = convw3.shape[2]
    nb, xc_s, bg_s, cg_s = _conv_specs(S, cw, conv_width)

    def body(xc_ref, bg_ref, cg_ref, w_ref, o_ref):
        u = cg_ref[...] * xc_ref[...]
        w = w_ref[...]
        cv = w[2:3, :] * u + w[1:2, :] * _shift_down(u, 1) + w[0:1, :] * _shift_down(u, 2)
        o_ref[...] = (bg_ref[...] * cv).astype(BF16)

    return pl.pallas_call(
        body, name=name, grid=(nb,),
        in_specs=[xc_s, bg_s, cg_s, pl.BlockSpec((None, CONV_K, cw), lambda j: (j, 0, 0))],
        out_specs=pl.BlockSpec((S, cw), lambda j: (0, j)), out_shape=jax.ShapeDtypeStruct((S, conv_width), BF16),
        compiler_params=_cparams(("parallel",)),
    )(proj, proj, proj, convw3)


def _conv_bwd(name, proj, convw3, da, conv_width):
    S = proj.shape[0]
    cw = convw3.shape[2]
    nb, xc_s, bg_s, cg_s = _conv_specs(S, cw, conv_width)

    def body(xc_ref, bg_ref, cg_ref, w_ref, da_ref, dxc_ref, dbg_ref, dcg_ref, dw_ref):
        xc, cg = xc_ref[...], cg_ref[...]
        u = cg * xc
        w = w_ref[...]
        u1, u2 = _shift_down(u, 1), _shift_down(u, 2)
        cv = w[2:3, :] * u + w[1:2, :] * u1 + w[0:1, :] * u2
        dav = da_ref[...]
        dbg_ref[...] = (dav * cv).astype(BF16)
        dcv = dav * bg_ref[...]
        du = w[2:3, :] * dcv + w[1:2, :] * _shift_up(dcv, 1) + w[0:1, :] * _shift_up(dcv, 2)
        dxc_ref[...] = (du * cg).astype(BF16)
        dcg_ref[...] = (du * xc).astype(BF16)
        dw_ref[0:1, :] = jnp.sum(dcv * u2, axis=0, keepdims=True)
        dw_ref[1:2, :] = jnp.sum(dcv * u1, axis=0, keepdims=True)
        dw_ref[2:3, :] = jnp.sum(dcv * u, axis=0, keepdims=True)

    wspec = pl.BlockSpec((None, CONV_K, cw), lambda j: (j, 0, 0))
    ospec = pl.BlockSpec((S, cw), lambda j: (0, j))
    act = jax.ShapeDtypeStruct((S, conv_width), BF16)
    return pl.pallas_call(
        body, name=name, grid=(nb,), in_specs=[xc_s, bg_s, cg_s, wspec, ospec],
        out_specs=(ospec, ospec, ospec, wspec),
        out_shape=(act, act, act, jax.ShapeDtypeStruct(convw3.shape, F32)),
        compiler_params=_cparams(("parallel",)),
    )(proj, proj, proj, convw3, da)


def _rope_consts(S, dh):
    rot = dh // 4
    half = rot // 2
    inv_freq = 1.0 / (ROPE_THETA ** (jnp.arange(0, rot, 2, dtype=F32) / rot))
    ang = jnp.arange(S, dtype=F32)[:, None] * inv_freq[None, :]
    cos = jnp.concatenate([jnp.cos(ang), jnp.cos(ang), jnp.ones((S, dh - rot), F32)], axis=1)
    sin = jnp.concatenate([jnp.sin(ang), jnp.sin(ang), jnp.zeros((S, dh - rot), F32)], axis=1)
    rm = np.zeros((dh, dh), np.float32)
    for j in range(half):
        rm[j + half, j] = -1.0
        rm[j, j + half] = 1.0
    return cos, sin, jnp.asarray(rm, BF16), jnp.asarray(rm.T, BF16)


def _exact_perm(y, rm):
    hi = y.astype(BF16)
    r1 = y - hi.astype(F32)
    mid = r1.astype(BF16)
    lo = (r1 - mid.astype(F32)).astype(BF16)
    dot = lambda a: jnp.dot(a, rm, preferred_element_type=F32)
    return dot(hi) + dot(mid) + dot(lo)


def _qk_prep(name, xh, gain, cos, sin, rm):
    H, S, dh = xh.shape
    tm = _tile(S, 1024, 8)

    def body(x_ref, g_ref, c_ref, s_ref, rm_ref, o_ref):
        xv = x_ref[...]
        y = xv * lax.rsqrt(jnp.mean(xv * xv, axis=-1, keepdims=True) + RMS_EPS) * g_ref[...]
        o_ref[...] = (y * c_ref[...] + _exact_perm(y, rm_ref[...]) * s_ref[...]).astype(BF16)

    blk = pl.BlockSpec((None, tm, dh), lambda h, i: (h, i, 0))
    tab = pl.BlockSpec((tm, dh), lambda h, i: (i, 0))
    return pl.pallas_call(
        body, name=name, grid=(H, S // tm),
        in_specs=[blk, pl.BlockSpec((1, dh), lambda h, i: (0, 0)), tab, tab, pl.BlockSpec((dh, dh), lambda h, i: (0, 0))],
        out_specs=blk, out_shape=jax.ShapeDtypeStruct((H, S, dh), BF16),
        compiler_params=_cparams(("parallel", "parallel")),
    )(xh, gain, cos, sin, rm)


def _qk_prep_bwd(name, xh, gain, cos, sin, rmt, dout):
    H, S, dh = xh.shape
    tm = _tile(S, 1024, 8)

    def body(x_ref, g_ref, c_ref, s_ref, rmt_ref, do_ref, dx_ref, dg_ref):
        first = (pl.program_id(0) == 0) & (pl.program_id(1) == 0)
        xv = x_ref[...]
        r = lax.rsqrt(jnp.mean(xv * xv, axis=-1, keepdims=True) + RMS_EPS)
        xhat = xv * r
        dov = do_ref[...]
        dy = dov * c_ref[...] + _exact_perm(dov * s_ref[...], rmt_ref[...])
        dxhat = dy * g_ref[...]
        dx_ref[...] = (r * (dxhat - xhat * jnp.mean(dxhat * xhat, axis=-1, keepdims=True))).astype(BF16)

        @pl.when(first)
        def _():
            dg_ref[...] = jnp.zeros_like(dg_ref)

        dg_ref[...] += jnp.sum(dy * xhat, axis=0, keepdims=True)

    blk = pl.BlockSpec((None, tm, dh), lambda h, i: (h, i, 0))
    tab = pl.BlockSpec((tm, dh), lambda h, i: (i, 0))
    vec = pl.BlockSpec((1, dh), lambda h, i: (0, 0))
    return pl.pallas_call(
        body, name=name, grid=(H, S // tm),
        in_specs=[blk, vec, tab, tab, pl.BlockSpec((dh, dh), lambda h, i: (0, 0)), blk],
        out_specs=(blk, vec), out_shape=(jax.ShapeDtypeStruct((H, S, dh), BF16), jax.ShapeDtypeStruct((1, dh), F32)),
        compiler_params=_cparams(("arbitrary", "arbitrary")),
    )(xh, gain, cos, sin, rmt, dout)


def _attn_probs(q, kp, kc, sink_col, n, scale):
    rows = q.shape[0]
    sp = lax.dot_general(q, kp, NT, preferred_element_type=F32) * scale
    sc = lax.dot_general(q, kc, NT, preferred_element_type=F32) * scale
    qi = lax.broadcasted_iota(jnp.int32, (rows, BLOCK), 0) % BLOCK
    kj = lax.broadcasted_iota(jnp.int32, (rows, BLOCK), 1)
    sp = jnp.where((kj > qi) & (n > 0), sp, NEG_INF)
    sc = jnp.where(kj <= qi, sc, NEG_INF)
    m = jnp.maximum(jnp.maximum(jnp.max(sp, axis=-1, keepdims=True), jnp.max(sc, axis=-1, keepdims=True)), sink_col)
    pp, pc, ps = jnp.exp(sp - m), jnp.exp(sc - m), jnp.exp(sink_col - m)
    inv = 1.0 / (jnp.sum(pp, axis=-1, keepdims=True) + jnp.sum(pc, axis=-1, keepdims=True) + ps)
    return pp * inv, pc * inv, ps * inv


def _sink_col(sink_ref, hk, group):
    rows = group * BLOCK
    g = lax.broadcasted_iota(jnp.int32, (rows, 1), 0) // BLOCK
    col = jnp.zeros((rows, 1), F32)
    for i in range(group):
        col = jnp.where(g == i, sink_ref[hk * group + i], col)
    return col


def _attn_specs(group, dh):
    qb = pl.BlockSpec((group, BLOCK, dh), lambda hk, n: (hk, n, 0))
    prev = pl.BlockSpec((None, BLOCK, dh), lambda hk, n: (hk, jnp.maximum(n - 1, 0), 0))
    cur = pl.BlockSpec((None, BLOCK, dh), lambda hk, n: (hk, n, 0))
    return qb, prev, cur, pl.BlockSpec(memory_space=pltpu.SMEM)


def _attn_fwd(name, q, k, v, sinks):
    HQ, S, dh = q.shape
    HKV = k.shape[0]
    group = HQ // HKV
    scale = dh ** -0.5
    qb, prev, cur, smem = _attn_specs(group, dh)

    def body(q_ref, kp_ref, kc_ref, vp_ref, vc_ref, sink_ref, o_ref):
        hk, n = pl.program_id(0), pl.program_id(1)
        qv = q_ref[...].reshape(group * BLOCK, dh)
        pp, pc, _ = _attn_probs(qv, kp_ref[...], kc_ref[...], _sink_col(sink_ref, hk, group), n, scale)
        o = jnp.dot(pp.astype(BF16), vp_ref[...], preferred_element_type=F32)
        o = o + jnp.dot(pc.astype(BF16), vc_ref[...], preferred_element_type=F32)
        o_ref[...] = o.reshape(group, BLOCK, dh).astype(BF16)

    return pl.pallas_call(
        body, name=name, grid=(HKV, S // BLOCK), in_specs=[qb, prev, cur, prev, cur, smem], out_specs=qb,
        out_shape=jax.ShapeDtypeStruct((HQ, S, dh), BF16), compiler_params=_cparams(("parallel", "parallel")),
    )(q, k, k, v, v, sinks)


def _attn_bwd(name, q, k, v, sinks, do):
    HQ, S, dh = q.shape
    HKV = k.shape[0]
    group = HQ // HKV
    scale = dh ** -0.5
    qb, prev, cur, smem = _attn_specs(group, dh)
    whole = pl.BlockSpec((None, S, dh), lambda hk, n: (hk, 0, 0))
    sk = pl.BlockSpec((None, group, LANES), lambda hk, n: (hk, 0, 0))

    def body(q_ref, kp_ref, kc_ref, vp_ref, vc_ref, sink_ref, do_ref, dq_ref, dk_ref, dv_ref, ds_ref):
        hk, n = pl.program_id(0), pl.program_id(1)
        rows = group * BLOCK
        qv = q_ref[...].reshape(rows, dh)
        dov = do_ref[...].reshape(rows, dh)
        kp, kc, vp, vc = kp_ref[...], kc_ref[...], vp_ref[...], vc_ref[...]
        pp, pc, ps = _attn_probs(qv, kp, kc, _sink_col(sink_ref, hk, group), n, scale)
        dpp = lax.dot_general(dov, vp, NT, preferred_element_type=F32)
        dpc = lax.dot_general(dov, vc, NT, preferred_element_type=F32)
        delta = jnp.sum(pp * dpp, axis=-1, keepdims=True) + jnp.sum(pc * dpc, axis=-1, keepdims=True)
        dsp = (pp * (dpp - delta) * scale).astype(BF16)
        dsc = (pc * (dpc - delta) * scale).astype(BF16)
        dq = jnp.dot(dsp, kp, preferred_element_type=F32) + jnp.dot(dsc, kc, preferred_element_type=F32)
        dq_ref[...] = dq.reshape(group, BLOCK, dh)

        @pl.when(n == 0)
        def _():
            dk_ref[...] = jnp.zeros_like(dk_ref)
            dv_ref[...] = jnp.zeros_like(dv_ref)
            ds_ref[...] = jnp.zeros_like(ds_ref)

        cur_rows = pl.ds(pl.multiple_of(n * BLOCK, BLOCK), BLOCK)
        prev_rows = pl.ds(pl.multiple_of(jnp.maximum(n - 1, 0) * BLOCK, BLOCK), BLOCK)
        tdot = lambda a, b: lax.dot_general(a, b, TN, preferred_element_type=F32)
        dk_ref[prev_rows, :] += tdot(dsp, qv)
        dv_ref[prev_rows, :] += tdot(pp.astype(BF16), dov)
        dk_ref[cur_rows, :] += tdot(dsc, qv)
        dv_ref[cur_rows, :] += tdot(pc.astype(BF16), dov)
        dsink = -jnp.sum((ps * delta).reshape(group, BLOCK, 1), axis=1)
        ds_ref[...] += jnp.broadcast_to(dsink, (group, LANES))

    return pl.pallas_call(
        body, name=name, grid=(HKV, S // BLOCK), in_specs=[qb, prev, cur, prev, cur, smem, qb],
        out_specs=(qb, whole, whole, sk),
        out_shape=(jax.ShapeDtypeStruct((HQ, S, dh), F32), jax.ShapeDtypeStruct((HKV, S, dh), F32),
                   jax.ShapeDtypeStruct((HKV, S, dh), F32), jax.ShapeDtypeStruct((HKV, group, LANES), F32)),
        compiler_params=_cparams(("arbitrary", "arbitrary")),
    )(q, k, k, v, v, sinks, do)


def _gate_specs(S, D, ga_off, gb_off):
    tg = LANES
    for t in range(LANES, 513, LANES):
        if D % t == 0 and ga_off % t == 0 and gb_off % t == 0:
            tg = t
    if D % LANES:
        tg = math.gcd(math.gcd(D, ga_off), gb_off)
    tm = _tile(S, 512, 8)
    act = pl.BlockSpec((tm, tg), lambda i, j: (i, j))
    ga = pl.BlockSpec((tm, tg), lambda i, j: (i, ga_off // tg + j))
    gb = pl.BlockSpec((tm, tg), lambda i, j: (i, gb_off // tg + j))
    return (S // tm, D // tg), act, ga, gb


def _gate_fwd(name, proj, ya, yb, ga_off, gb_off):
    S, D = ya.shape
    grid, act, ga, gb = _gate_specs(S, D, ga_off, gb_off)

    def body(ga_ref, gb_ref, ya_ref, yb_ref, o_ref):
        o_ref[...] = (_sigmoid(ga_ref[...]) * ya_ref[...] + _sigmoid(gb_ref[...]) * yb_ref[...]).astype(BF16)

    return pl.pallas_call(
        body, name=name, grid=grid, in_specs=[ga, gb, act, act], out_specs=act,
        out_shape=jax.ShapeDtypeStruct((S, D), BF16), compiler_params=_cparams(("parallel", "parallel")),
    )(proj, proj, ya, yb)


def _gate_bwd(name, proj, ya, yb, dm, ga_off, gb_off):
    S, D = ya.shape
    grid, act, ga, gb = _gate_specs(S, D, ga_off, gb_off)

    def body(ga_ref, gb_ref, ya_ref, yb_ref, dm_ref, dga_ref, dgb_ref, dya_ref, dyb_ref):
        dmv = dm_ref[...]
        sa, sb = _sigmoid(ga_ref[...]), _sigmoid(gb_ref[...])
        dga_ref[...] = (dmv * ya_ref[...] * sa * (1.0 - sa)).astype(BF16)
        dgb_ref[...] = (dmv * yb_ref[...] * sb * (1.0 - sb)).astype(BF16)
        dya_ref[...] = (dmv * sa).astype(BF16)
        dyb_ref[...] = (dmv * sb).astype(BF16)

    o = jax.ShapeDtypeStruct((S, D), BF16)
    return pl.pallas_call(
        body, name=name, grid=grid, in_specs=[ga, gb, act, act, act], out_specs=(act, act, act, act),
        out_shape=(o, o, o, o), compiler_params=_cparams(("parallel", "parallel")),
    )(proj, proj, ya, yb, dm)


ANY = pl.BlockSpec(memory_space=pl.ANY)


def _row_tile(rows, cols, n_arrays):
    want = max(16, (VMEM_LIMIT_V7X // 2) // (2 * n_arrays * cols * 4))
    return _tile(rows, want, 16)


def _cast_to_slot(name, w, dtype, p_arr, dep=None):
    R, C = w.shape
    tr = _row_tile(R, C, 2)
    extra = () if dep is None else (dep,)

    def body(p_ref, w_ref, *rest):
        rest[-1][...] = w_ref[...].astype(dtype)

    return pl.pallas_call(
        body, name=name,
        grid_spec=pltpu.PrefetchScalarGridSpec(
            num_scalar_prefetch=1, grid=(R // tr,),
            in_specs=[pl.BlockSpec((tr, C), lambda i, p_ref: (i, 0))] + [pl.BlockSpec(d.shape, lambda i, p_ref: (0, 0)) for d in extra],
            out_specs=pl.BlockSpec((None, tr, C), lambda i, p_ref: (p_ref[0], i, 0))),
        out_shape=jax.ShapeDtypeStruct((N_CHIPS, R, C), dtype), compiler_params=_cparams(("parallel",)),
    )(p_arr, w, *extra)


def _add_half(name, g3, r3, c_arr):
    n, h, C = r3.shape
    tr = _row_tile(h, C, 3)
    nb = h // tr

    def body(c_ref, g_ref, r_ref, o_ref):
        o_ref[...] = (g_ref[...].astype(F32) + r_ref[...].astype(F32)).astype(BF16)

    blk = pl.BlockSpec((None, tr, C), lambda s, i, c_ref: (s, i, 0))
    return pl.pallas_call(
        body, name=name,
        grid_spec=pltpu.PrefetchScalarGridSpec(
            num_scalar_prefetch=1, grid=(n, nb),
            in_specs=[pl.BlockSpec((None, tr, C), lambda s, i, c_ref: (s, c_ref[0] * nb + i, 0)), blk], out_specs=blk),
        out_shape=jax.ShapeDtypeStruct(r3.shape, BF16), compiler_params=_cparams(("parallel", "parallel")),
    )(c_arr, g3, r3)


def _add_chips(name, t3, r3, cp_arr):
    n, h, C = r3.shape
    tr = _row_tile(h, C, 6)
    nb = h // tr

    def body(cp_ref, t_ref, r0_ref, r1_ref, r2_ref, r3_ref, o_ref):
        p = cp_ref[1]
        total = None
        for a, r_ref in enumerate((r0_ref, r1_ref, r2_ref, r3_ref)):
            part = jnp.where(p == a, t_ref[...], r_ref[...]).astype(F32)
            total = part if total is None else total + part
        o_ref[...] = total

    def part(a):
        return pl.BlockSpec((None, tr, C), lambda i, cp_ref: (jnp.where(cp_ref[1] == a, (a + 1) % N_CHIPS, a), i, 0))

    return pl.pallas_call(
        body, name=name,
        grid_spec=pltpu.PrefetchScalarGridSpec(
            num_scalar_prefetch=1, grid=(nb,),
            in_specs=[pl.BlockSpec((None, tr, C), lambda i, cp_ref: (cp_ref[1], i, 0)), part(0), part(1), part(2), part(3)],
            out_specs=pl.BlockSpec((tr, C), lambda i, cp_ref: (cp_ref[0] * nb + i, 0))),
        out_shape=jax.ShapeDtypeStruct((2 * h, C), F32), compiler_params=_cparams(("parallel",)),
    )(cp_arr, t3, r3, r3, r3, r3)


def _adamw(name, w, g, m, v, deps=()):
    R, C = w.shape
    extra = tuple(deps)
    tr = _row_tile(R, C, 8)
    c1 = 1.0 - ADAM_B1 ** ADAM_STEP
    c2 = 1.0 - ADAM_B2 ** ADAM_STEP

    def body(w_ref, g_ref, m_ref, v_ref, *rest):
        go_ref, d_ref, nm_ref, nv_ref = rest[-4:]
        gv = g_ref[...]
        go_ref[...] = gv
        nm = ADAM_B1 * m_ref[...] + (1.0 - ADAM_B1) * gv
        nv = ADAM_B2 * v_ref[...] + (1.0 - ADAM_B2) * (gv * gv)
        d_ref[...] = -ADAM_LR * ((nm / c1) / (jnp.sqrt(nv / c2) + ADAM_EPS) + ADAM_WD * w_ref[...])
        nm_ref[...] = nm
        nv_ref[...] = nv

    blk = pl.BlockSpec((tr, C), lambda i: (i, 0))
    o = jax.ShapeDtypeStruct((R, C), F32)
    return pl.pallas_call(
        body, name=name, grid=(R // tr,), in_specs=[blk, blk, blk, blk] + [ANY] * len(extra), out_specs=(blk, blk, blk, blk),
        out_shape=(o, o, o, o), compiler_params=_cparams(("parallel",)),
    )(w, g, m, v, *extra)


def _place():
    x, y, c = lax.axis_index("x"), lax.axis_index("y"), lax.axis_index("c")
    chips = [(1 - x, y), (x, 1 - y), (1 - x, 1 - y)]
    return x, y, c, 2 * x + y, chips


HBM = pl.BlockSpec(memory_space=pltpu.HBM)
SEM = pl.BlockSpec(memory_space=pltpu.SEMAPHORE)
TOKEN = jax.ShapeDtypeStruct((8, LANES), F32)
DATAFLOW = pltpu.SideEffectType.DATAFLOW_SIDE_EFFECTING


def _hbm(a):
    return pltpu.with_memory_space_constraint(a, pltpu.HBM)


def _gather_blocks(bufs, i, c, p, chips):
    if bufs[i].shape[1] % 16:
        return bufs[i].at[p], [bufs[i].at[2 * cx + cy] for cx, cy in chips]
    h = bufs[i].shape[1] // 2
    rows = pl.ds(pl.multiple_of(c * h, 16), h)
    return bufs[i].at[p, rows], [bufs[i].at[2 * cx + cy, rows] for cx, cy in chips]


def _gather_start(name, slots, dep):
    n = len(slots)

    def body(*refs):
        bufs, send, recv, token = refs[:n], refs[n + 1], refs[n + 2], refs[-1]
        x, y, c, p, chips = _place()
        for i in range(n):
            mine, _ = _gather_blocks(bufs, i, c, p, chips)
            for j, chip in enumerate(chips):
                pltpu.make_async_remote_copy(src_ref=mine, dst_ref=mine, send_sem=send.at[3 * i + j], recv_sem=recv.at[3 * i + j],
                                             device_id=(*chip, c), device_id_type=MESH).start()
        token[...] = jnp.zeros_like(token)

    out = pl.pallas_call(
        body, name=name, in_specs=[HBM] * n + [ANY],
        out_specs=(SEM, SEM, *([HBM] * n), pl.BlockSpec(memory_space=pltpu.VMEM)),
        out_shape=(pltpu.SemaphoreType.DMA((3 * n,)), pltpu.SemaphoreType.DMA((3 * n,)),
                   *[pltpu.HBM(s.shape, s.dtype) for s in slots], TOKEN),
        input_output_aliases={i: 2 + i for i in range(n)},
        compiler_params=pltpu.CompilerParams(has_side_effects=DATAFLOW),
    )(*[_hbm(s) for s in slots], dep)
    return out[0], out[1], list(out[2:2 + n]), out[-1]


def _gather_wait(name, send, recv, slots, after):
    n = len(slots)

    def body(*refs):
        bufs, send, recv = refs[:n], refs[n], refs[n + 1]
        x, y, c, p, chips = _place()
        for i in range(n):
            mine, landed = _gather_blocks(bufs, i, c, p, chips)
            for j, chip in enumerate(chips):
                cp = pltpu.make_async_remote_copy(src_ref=mine, dst_ref=landed[j], send_sem=send.at[3 * i + j],
                                                  recv_sem=recv.at[3 * i + j], device_id=(*chip, c), device_id_type=MESH)
                cp.wait_send()
                cp.wait_recv()

    return list(pl.pallas_call(
        body, name=name, in_specs=[HBM] * n + [SEM, SEM, ANY], out_specs=tuple([HBM] * n),
        out_shape=tuple(pltpu.HBM(s.shape, s.dtype) for s in slots),
        input_output_aliases={i: i for i in range(n)},
        compiler_params=pltpu.CompilerParams(has_side_effects=DATAFLOW),
    )(*slots, send, recv, after))


def _gather_forward(name, slots):
    idx = [i for i, s in enumerate(slots) if s.shape[1] % 16 == 0]
    n = len(slots)

    def body(*refs):
        bufs = refs[n:2 * n]
        send, recv = refs[2 * n:]
        x, y, c, p, chips = _place()

        def rdma(k, ref):
            return pltpu.make_async_remote_copy(src_ref=ref, dst_ref=ref, send_sem=send.at[k], recv_sem=recv.at[k],
                                                device_id=(x, y, 1 - c), device_id_type=MESH)

        cps = []
        for k, i in enumerate(idx):
            for j, ref in enumerate(_gather_blocks(bufs, i, c, p, chips)[1]):
                cps.append(rdma(3 * k + j, ref))
                cps[-1].start()
        for k, i in enumerate(idx):
            for j, ref in enumerate(_gather_blocks(bufs, i, 1 - c, p, chips)[1]):
                rdma(3 * k + j, ref).wait_recv()
        for cp in cps:
            cp.wait_send()

    return list(pl.pallas_call(
        body, name=name, in_specs=[ANY] * n, out_specs=tuple([ANY] * n),
        out_shape=tuple(jax.ShapeDtypeStruct(s.shape, s.dtype) for s in slots),
        scratch_shapes=[pltpu.SemaphoreType.DMA((3 * len(idx),)), pltpu.SemaphoreType.DMA((3 * len(idx),))],
        input_output_aliases={i: i for i in range(n)},
        compiler_params=pltpu.CompilerParams(has_side_effects=True),
    )(*slots))


def _swap_copy(grads, lands, send, recv, i, x, y, c):
    h = grads[i].shape[1] // 2
    other = pl.ds(pl.multiple_of((1 - c) * h, 16), h)
    return pltpu.make_async_remote_copy(src_ref=grads[i].at[:, other, :], dst_ref=lands[i], send_sem=send.at[i],
                                        recv_sem=recv.at[i], device_id=(x, y, 1 - c), device_id_type=MESH)


def _swap_start(name, grads):
    n = len(grads)

    def body(*refs):
        ins, lands, send, recv, token = refs[:n], refs[n:2 * n], refs[2 * n], refs[2 * n + 1], refs[-1]
        x, y, c, p, chips = _place()
        for i in range(n):
            _swap_copy(ins, lands, send, recv, i, x, y, c).start()
        token[...] = jnp.zeros_like(token)

    gshapes = [pltpu.HBM(g.shape, g.dtype) for g in grads]
    halves = [(g.shape[0], g.shape[1] // 2, g.shape[2]) for g in grads]
    lshapes = [pltpu.HBM(s, g.dtype) for s, g in zip(halves, grads)]
    out = pl.pallas_call(
        body, name=name, in_specs=[HBM] * (2 * n),
        out_specs=(SEM, SEM, *([HBM] * (2 * n)), pl.BlockSpec(memory_space=pltpu.VMEM)),
        out_shape=(pltpu.SemaphoreType.DMA((n,)), pltpu.SemaphoreType.DMA((n,)), *gshapes, *lshapes, TOKEN),
        input_output_aliases={i: 2 + i for i in range(2 * n)},
        compiler_params=pltpu.CompilerParams(has_side_effects=DATAFLOW),
    )(*[_hbm(g) for g in grads], *[_hbm(lax.empty(s, g.dtype)) for s, g in zip(halves, grads)])
    return out[0], out[1], list(out[2:2 + n]), list(out[2 + n:2 + 2 * n]), out[-1]


def _swap_wait(name, send, recv, grads, lands, after):
    n = len(grads)

    def body(*refs):
        ins, lands, send, recv = refs[:n], refs[n:2 * n], refs[2 * n], refs[2 * n + 1]
        x, y, c, p, chips = _place()
        for i in range(n):
            cp = _swap_copy(ins, lands, send, recv, i, x, y, c)
            cp.wait_send()
            cp.wait_recv()

    shapes = [pltpu.HBM(t.shape, t.dtype) for t in list(grads) + list(lands)]
    out = pl.pallas_call(
        body, name=name, in_specs=[HBM] * (2 * n) + [SEM, SEM, ANY], out_specs=tuple([HBM] * (2 * n)),
        out_shape=tuple(shapes), input_output_aliases={i: i for i in range(2 * n)},
        compiler_params=pltpu.CompilerParams(has_side_effects=DATAFLOW),
    )(*grads, *lands, send, recv, after)
    return list(out[:n]), list(out[n:])


def _exchange_start(name, parts):
    n = len(parts)

    def body(*refs):
        ins, lands, send, recv, token = refs[:n], refs[n:2 * n], refs[2 * n], refs[2 * n + 1], refs[-1]
        x, y, c, p, chips = _place()
        for i in range(n):
            for j, (cx, cy) in enumerate(chips):
                pltpu.make_async_remote_copy(src_ref=ins[i].at[2 * cx + cy], dst_ref=lands[i].at[p], send_sem=send.at[3 * i + j],
                                             recv_sem=recv.at[3 * i + j], device_id=(cx, cy, c), device_id_type=MESH).start()
        token[...] = jnp.zeros_like(token)

    shapes = [pltpu.HBM(t.shape, t.dtype) for t in parts]
    out = pl.pallas_call(
        body, name=name, in_specs=[HBM] * (2 * n),
        out_specs=(SEM, SEM, *([HBM] * (2 * n)), pl.BlockSpec(memory_space=pltpu.VMEM)),
        out_shape=(pltpu.SemaphoreType.DMA((3 * n,)), pltpu.SemaphoreType.DMA((3 * n,)), *shapes, *shapes, TOKEN),
        input_output_aliases={i: 2 + i for i in range(2 * n)},
        compiler_params=pltpu.CompilerParams(has_side_effects=DATAFLOW),
    )(*[_hbm(t) for t in parts], *[_hbm(lax.empty(t.shape, t.dtype)) for t in parts])
    return out[0], out[1], list(out[2:2 + n]), list(out[2 + n:2 + 2 * n]), out[-1]


def _exchange_wait(name, send, recv, parts, lands, after):
    n = len(parts)

    def body(*refs):
        ins, lands, send, recv = refs[:n], refs[n:2 * n], refs[2 * n], refs[2 * n + 1]
        x, y, c, p, chips = _place()
        for i in range(n):
            for j, (cx, cy) in enumerate(chips):
                q = 2 * cx + cy
                cp = pltpu.make_async_remote_copy(src_ref=ins[i].at[q], dst_ref=lands[i].at[q], send_sem=send.at[3 * i + j],
                                                  recv_sem=recv.at[3 * i + j], device_id=(cx, cy, c), device_id_type=MESH)
                cp.wait_send()
                cp.wait_recv()

    shapes = [pltpu.HBM(t.shape, t.dtype) for t in parts]
    out = pl.pallas_call(
        body, name=name, in_specs=[HBM] * (2 * n) + [SEM, SEM, ANY], out_specs=tuple([HBM] * (2 * n)),
        out_shape=(*shapes, *shapes), input_output_aliases={i: i for i in range(2 * n)},
        compiler_params=pltpu.CompilerParams(has_side_effects=DATAFLOW),
    )(*parts, *lands, send, recv, after)
    return list(out[:n]), list(out[n:])


def _join_copy(bufs, send, recv, i, which, x, y, c):
    h = bufs[i].shape[0] // 2
    rows = bufs[i].at[pl.ds(pl.multiple_of(which * h, 8), h)]
    return pltpu.make_async_remote_copy(src_ref=rows, dst_ref=rows, send_sem=send.at[i], recv_sem=recv.at[i],
                                        device_id=(x, y, 1 - c), device_id_type=MESH)


def _join_start(name, bufs):
    n = len(bufs)

    def body(*refs):
        ins, send, recv, token = refs[:n], refs[n], refs[n + 1], refs[-1]
        x, y, c, p, chips = _place()
        for i in range(n):
            _join_copy(ins, send, recv, i, c, x, y, c).start()
        token[...] = jnp.zeros_like(token)

    out = pl.pallas_call(
        body, name=name, in_specs=[HBM] * n,
        out_specs=(SEM, SEM, *([HBM] * n), pl.BlockSpec(memory_space=pltpu.VMEM)),
        out_shape=(pltpu.SemaphoreType.DMA((n,)), pltpu.SemaphoreType.DMA((n,)), *[pltpu.HBM(t.shape, t.dtype) for t in bufs], TOKEN),
        input_output_aliases={i: 2 + i for i in range(n)},
        compiler_params=pltpu.CompilerParams(has_side_effects=DATAFLOW),
    )(*[_hbm(t) for t in bufs])
    return out[0], out[1], list(out[2:2 + n]), out[-1]


def _join_wait(name, send, recv, bufs, after):
    n = len(bufs)

    def body(*refs):
        ins, send, recv = refs[:n], refs[n], refs[n + 1]
        x, y, c, p, chips = _place()
        for i in range(n):
            _join_copy(ins, send, recv, i, c, x, y, c).wait_send()
            _join_copy(ins, send, recv, i, 1 - c, x, y, c).wait_recv()

    return list(pl.pallas_call(
        body, name=name, in_specs=[HBM] * n + [SEM, SEM, ANY], out_specs=tuple([HBM] * n),
        out_shape=tuple(pltpu.HBM(t.shape, t.dtype) for t in bufs), input_output_aliases={i: i for i in range(n)},
        compiler_params=pltpu.CompilerParams(has_side_effects=DATAFLOW),
    )(*bufs, send, recv, after))


def _allreduce_small(name, pack, dep):
    R, W = pack.shape

    def body(in_ref, dep_ref, out_ref, slots, send, recv):
        x, y, c = lax.axis_index("x"), lax.axis_index("y"), lax.axis_index("c")
        me = 4 * x + 2 * y + c
        slots[0] = in_ref[...]
        cps = []
        for k in range(1, N_DEV):
            peer = (x ^ (k >> 2), y ^ ((k >> 1) & 1), c ^ (k & 1))
            cp = pltpu.make_async_remote_copy(src_ref=in_ref, dst_ref=slots.at[k], send_sem=send.at[k - 1],
                                              recv_sem=recv.at[k - 1], device_id=peer, device_id_type=MESH)
            cp.start()
            cps.append(cp)
        for cp in cps:
            cp.wait()
        total = slots[me]
        for a in range(1, N_DEV):
            total = total + slots[jnp.bitwise_xor(a, me)]
        out_ref[...] = total

    vmem = pl.BlockSpec(memory_space=pltpu.VMEM)
    return pl.pallas_call(
        body, name=name, in_specs=[vmem, ANY], out_specs=vmem, out_shape=jax.ShapeDtypeStruct((R, W), F32),
        scratch_shapes=[pltpu.VMEM((N_DEV, R, W), F32), pltpu.SemaphoreType.DMA((N_DEV - 1,)), pltpu.SemaphoreType.DMA((N_DEV - 1,))],
        compiler_params=pltpu.CompilerParams(has_side_effects=True),
    )(pack, dep)


def _heads(a, n_heads):
    S = a.shape[0]
    return a.reshape(S, n_heads, a.shape[1] // n_heads).transpose(1, 0, 2)


def _unheads(a):
    H, S, dh = a.shape
    return a.transpose(1, 0, 2).reshape(S, H * dh)


def _ffn_bwd(tag, xin, gain, wgu3, wd, saved, dxout, dxo_b, reduce_start, dep, flush=None):
    h, gu, act = saved
    D = xin.shape[1]
    tok = reduce_start({f"w_down{tag}": _mm_tn(f"dw_down_{tag}", act, dxo_b, 0.5, dep=dep).reshape(N_CHIPS, -1, D)})
    dgu = _ffn_down_bwd(f"ffn_down_bwd_{tag}", dxo_b, wd, gu, 0.5, dep=tok)
    tok = reduce_start({f"w_gu{tag}": _mm_tn_cols(f"dw_gu_{tag}", h, dgu, wgu3.shape[2], b_is_gu=True)})
    if flush is not None:
        tok = flush(tok)
    dh = _mm_nt_cols(f"ffn_up_bwd_{tag}", dgu, wgu3, a_is_gu=True, dep=tok)
    dxin, dxin_b, dgain = _rms_bwd(f"rms_bwd_{tag}", xin, gain, dh, dxout)
    return dxin, dxin_b, dgain, tok


def kernel(x, g_ffn1, w_gu1, w_down1, g_mix, w_in, conv_w, q_norm_g, k_norm_g, sinks, w_out_conv, w_out_attn, w_o, g_ffn2, w_gu2, w_down2, loss_target, m_g_ffn1, m_w_gu1, m_w_down1, m_g_mix, m_w_in, m_conv_w, m_q_norm_g, m_k_norm_g, m_sinks, m_w_out_conv, m_w_out_attn, m_w_o, m_g_ffn2, m_w_gu2, m_w_down2, v_g_ffn1, v_w_gu1, v_w_down1, v_g_mix, v_w_in, v_conv_w, v_q_norm_g, v_k_norm_g, v_sinks, v_w_out_conv, v_w_out_attn, v_w_o, v_g_ffn2, v_w_gu2, v_w_down2):
    S, D = x.shape[1], x.shape[2]
    dh = q_norm_g.shape[1]
    HQ = sinks.shape[1]
    HKV = HQ // 4
    AW, KVW, CW = HQ * dh, HKV * dh, D // 2
    off_q, off_k, off_v = 3 * CW, 3 * CW + AW, 3 * CW + AW + KVW
    off_ga, off_gb = off_v + KVW, off_v + KVW + D
    x0, target = x[0], loss_target[0]
    cx, cy, cc = lax.axis_index("x"), lax.axis_index("y"), lax.axis_index("c")
    chip = 2 * cx + cy
    p_arr = jnp.reshape(chip, (1,)).astype(jnp.int32)
    c_arr = jnp.reshape(cc, (1,)).astype(jnp.int32)
    cp_arr = jnp.stack([cc, chip]).astype(jnp.int32)
    wts = dict(g_ffn1=g_ffn1, w_gu1=w_gu1, w_down1=w_down1, g_mix=g_mix, w_in=w_in, conv_w=conv_w, q_norm_g=q_norm_g,
               k_norm_g=k_norm_g, sinks=sinks, w_out_conv=w_out_conv, w_out_attn=w_out_attn, w_o=w_o, g_ffn2=g_ffn2,
               w_gu2=w_gu2, w_down2=w_down2)
    ms = dict(g_ffn1=m_g_ffn1, w_gu1=m_w_gu1, w_down1=m_w_down1, g_mix=m_g_mix, w_in=m_w_in, conv_w=m_conv_w,
              q_norm_g=m_q_norm_g, k_norm_g=m_k_norm_g, sinks=m_sinks, w_out_conv=m_w_out_conv, w_out_attn=m_w_out_attn,
              w_o=m_w_o, g_ffn2=m_g_ffn2, w_gu2=m_w_gu2, w_down2=m_w_down2)
    vs = dict(g_ffn1=v_g_ffn1, w_gu1=v_w_gu1, w_down1=v_w_down1, g_mix=v_g_mix, w_in=v_w_in, conv_w=v_conv_w,
              q_norm_g=v_q_norm_g, k_norm_g=v_k_norm_g, sinks=v_sinks, w_out_conv=v_w_out_conv, w_out_attn=v_w_out_attn,
              w_o=v_w_o, g_ffn2=v_g_ffn2, w_gu2=v_w_gu2, w_down2=v_w_down2)
    order = list(wts)
    small_names = [k for k in order if not k.startswith("w_")]
    grad, delta, new_m, new_v = {}, {}, {}, {}

    def cast(keys, dep=None):
        return [_cast_to_slot(f"cast_{k}", wts[k][0], F32 if k == "conv_w" else BF16, p_arr, dep) for k in keys]

    def gather_start(tag, slots, dep):
        send, recv, slots, tok = _gather_start(f"gather_start_{tag}", slots, dep)
        return (tag, send, recv, slots), tok

    def gather_finish(started, after):
        tag, send, recv, slots = started
        return _gather_forward(f"gather_forward_{tag}", _gather_wait(f"gather_wait_{tag}", send, recv, slots, after))

    swapping, pending = [], []

    def reduce_start(full):
        keys = list(full)
        send, recv, gs, lands, tok = _swap_start(f"swap_start_{keys[0]}", [full[k] for k in keys])
        if swapping:
            tok = reduce_advance(tok)
        swapping.append((keys, send, recv, gs, lands))
        return tok

    def reduce_advance(after):
        keys, send, recv, gs, lands = swapping.pop(0)
        gs, sib = _swap_wait(f"swap_wait_{keys[0]}", send, recv, gs, lands, after)
        parts = [_add_half(f"add_half_{k}", g, r, c_arr) for k, g, r in zip(keys, gs, sib)]
        send, recv, parts, lands, tok = _exchange_start(f"exchange_start_{keys[0]}", parts)
        pending.append((keys, send, recv, parts, lands))
        return tok

    def reduce_finish(entries, after):
        joining, last = None, after

        def update(joining, tok):
            keys, send, recv, halves = joining
            out = last
            for k, g2 in zip(keys, _join_wait(f"join_wait_{keys[0]}", send, recv, halves, tok)):
                g2, d, nm, nv = _adamw(f"adamw_{k}", wts[k][0], g2, ms[k][0], vs[k][0], (out,))
                grad[k], delta[k], new_m[k], new_v[k] = g2[None], d[None], nm[None], nv[None]
                out = nv
            return out

        for keys, send, recv, parts, lands in entries:
            parts, lands = _exchange_wait(f"exchange_wait_{keys[0]}", send, recv, parts, lands, after)
            halves = [_add_chips(f"add_chips_{k}", t, r, cp_arr) for k, t, r in zip(keys, parts, lands)]
            send, recv, halves, tok = _join_start(f"join_start_{keys[0]}", halves)
            if joining is not None:
                last = update(joining, tok)
            joining = (keys, send, recv, halves)
        return update(joining, last)

    st_gu1, tok = gather_start("gu1", cast(["w_gu1"]), x0)
    st_d1, tok = gather_start("d1", cast(["w_down1"]), tok)
    later = ["w_in", "conv_w", "w_out_conv", "w_out_attn", "w_o", "w_gu2", "w_down2"]
    slot = dict(zip(later, cast(later, tok)))
    wgu1, = gather_finish(st_gu1, slot["w_down2"])
    st_in, tok = gather_start("in", [slot["w_in"], slot["conv_w"]], wgu1)
    st_out, tok = gather_start("out", [slot["w_out_conv"], slot["w_out_attn"], slot["w_o"]], tok)
    st_gu2, tok = gather_start("gu2", [slot["w_gu2"]], tok)
    st_d2, tok = gather_start("d2", [slot["w_down2"]], tok)
    cos, sin, rm, rmt = _rope_consts(S, dh)
    sink_vec = sinks[0]

    h1 = _rms_fwd("rms_fwd_1", x0, g_ffn1, tok)
    gu1, act1 = _ffn_up("ffn_up_1", h1, wgu1)
    wd1 = gather_finish(st_d1, act1)[0].reshape(-1, D)
    x1 = _mm_res("ffn_down_1", act1, wd1, x0, 0.5)
    win3, convw3 = gather_finish(st_in, x1)
    h2 = _rms_fwd("rms_fwd_mix", x1, g_mix)
    proj = _mm_cols("in_proj", h2, win3, F32)
    aconv = _conv_fwd("conv_fwd", proj, convw3, CW)
    woc3, woa3, wo = gather_finish(st_out, aconv)
    wo = wo.reshape(-1, D)
    ya = _mm_cols("out_conv", aconv, woc3, F32)
    q_raw = _heads(proj[:, off_q:off_q + AW], HQ)
    k_raw = _heads(proj[:, off_k:off_k + KVW], HKV)
    vh = _heads(proj[:, off_v:off_v + KVW], HKV).astype(BF16)
    qn = _qk_prep("q_prep", q_raw, q_norm_g, cos, sin, rm)
    kn = _qk_prep("k_prep", k_raw, k_norm_g, cos, sin, rm)
    oh = _attn_fwd("attn_fwd", qn, kn, vh, sink_vec)
    o = _unheads(oh)
    yb = _mm_cols("out_attn", o, woa3, F32)
    merged = _gate_fwd("gate_fwd", proj, ya, yb, off_ga, off_gb)
    x2 = _mm_res("mix_out", merged, wo, x1, 1.0)
    wgu2, = gather_finish(st_gu2, x2)
    h3 = _rms_fwd("rms_fwd_2", x2, g_ffn2)
    gu2, act2 = _ffn_up("ffn_up_2", h3, wgu2)
    wd2 = gather_finish(st_d2, act2)[0].reshape(-1, D)
    x3 = _mm_res("ffn_down_2", act2, wd2, x2, 0.5)

    dy, dy_b, loss_lanes = _loss_grad("loss_grad", x3, target)
    dx2, dx2_b, dg_ffn2, tok = _ffn_bwd("2", x2, g_ffn2, wgu2, wd2, (h3, gu2, act2), dy, dy_b, reduce_start, None)
    dmerged = _mm_nt("mix_out_bwd", dx2_b, wo, F32)
    tok = reduce_start(dict(w_o=_mm_tn("dw_o", merged, dx2_b, dep=tok).reshape(N_CHIPS, -1, D)))
    dga, dgb, dya, dyb = _gate_bwd("gate_bwd", proj, ya, yb, dmerged, off_ga, off_gb)
    daconv = _mm_nt_cols("out_conv_bwd", dya, woc3, dep=tok)
    dwoc = _mm_tn_cols("dw_out_conv", aconv, dya, woc3.shape[2])
    do = _mm_nt_cols("out_attn_bwd", dyb, woa3)
    dwoa = _mm_tn_cols("dw_out_attn", o, dyb, woa3.shape[2])
    tok = reduce_start(dict(w_out_conv=dwoc, w_out_attn=dwoa))
    dxc, dbg, dcg, dconvw = _conv_bwd("conv_bwd", proj, convw3, daconv, CW)
    dqn, dkn, dvh, dsink3 = _attn_bwd("attn_bwd", qn, kn, vh, sink_vec, _heads(do, HQ).astype(BF16))
    dq_raw, dqg = _qk_prep_bwd("q_prep_bwd", q_raw, q_norm_g, cos, sin, rmt, dqn)
    dk_raw, dkg = _qk_prep_bwd("k_prep_bwd", k_raw, k_norm_g, cos, sin, rmt, dkn)
    dproj = jnp.concatenate([dxc, dbg, dcg, _unheads(dq_raw), _unheads(dk_raw), _unheads(dvh).astype(BF16), dga, dgb], axis=1)
    dh2 = _mm_nt_cols("in_proj_bwd", dproj, win3, dep=tok)
    tok = reduce_start(dict(w_in=_mm_tn_cols("dw_in", h2, dproj, win3.shape[2])))
    dx1, dx1_b, dg_mix = _rms_bwd("rms_bwd_mix", x1, g_mix, dh2, dx2)
    dx0, _, dg_ffn1, tok = _ffn_bwd("1", x0, g_ffn1, wgu1, wd1, (h1, gu1, act1), dx1, dx1_b, reduce_start, tok, reduce_advance)

    def rows8(a):
        a = a.reshape(-1, a.shape[-1])
        return jnp.pad(a, ((0, -a.shape[0] % 8), (0, D - a.shape[1])))

    misc = jnp.concatenate([dqg, dkg, dsink3[:, :, 0].reshape(1, HQ), loss_lanes], axis=1)
    done = reduce_finish(pending[:-2], dx0)
    tot = _allreduce_small("allreduce_small", jnp.concatenate([rows8(a) for a in (dg_ffn1, dg_mix, dg_ffn2, dconvw, misc)], axis=0), done)
    reduce_finish(pending[-2:], tot)

    cw_s = conv_w.shape[2]
    conv_row0, misc_row = 24, 24 + (-(-N_CHIPS * CONV_K // 8)) * 8
    small_g = dict(g_ffn1=tot[0:1], g_mix=tot[8:9], g_ffn2=tot[16:17],
                   conv_w=lax.dynamic_slice(tot, (conv_row0 + CONV_K * chip, 0), (CONV_K, cw_s)),
                   q_norm_g=tot[misc_row:misc_row + 1, 0:dh], k_norm_g=tot[misc_row:misc_row + 1, dh:2 * dh],
                   sinks=tot[misc_row:misc_row + 1, 2 * dh:2 * dh + HQ])
    loss = (0.5 / D) * jnp.sum(tot[misc_row, 2 * dh + HQ:2 * dh + HQ + LANES])

    def small_pack(src):
        return jnp.concatenate([rows8(src[k]) for k in small_names], axis=0)

    _, sd, sm, sv = _adamw("adamw_small", small_pack(wts), small_pack(small_g), small_pack(ms), small_pack(vs))
    for i, k in enumerate(small_names):
        shape = wts[k].shape
        nr, ncol = math.prod(shape[:-1]), shape[-1]
        grad[k] = small_g[k].reshape(shape)
        delta[k], new_m[k], new_v[k] = (a[8 * i:8 * i + nr, 0:ncol].reshape(shape) for a in (sd, sm, sv))
    return (loss, dx0[None], *[grad[k] for k in order], *[delta[k] for k in order],
            *[new_m[k] for k in order], *[new_v[k] for k in order])
```

```python
import math

import numpy as np
import jax
import jax.numpy as jnp
from jax import lax
from jax.experimental import pallas as pl
from jax.experimental.pallas import tpu as pltpu

F32 = jnp.float32
BF16 = jnp.bfloat16
MESH = pl.DeviceIdType.MESH

RMS_EPS = 1e-6
BLOCK = 128
ROPE_THETA = 500000.0
NEG_INF = -1e30
CONV_K = 3
ADAM_LR, ADAM_B1, ADAM_B2, ADAM_EPS, ADAM_WD, ADAM_STEP = 0.001, 0.9, 0.999, 1e-08, 0.01, 10

VMEM_LIMIT_V7X = 56 * 1024 * 1024
LANES = 128
N_CHIPS = 4
N_DEV = 8


def _tile(n, want, align=LANES):
    best = None
    t = align
    while t <= min(n, want):
        if n % t == 0:
            best = t
        t += align
    return best or n


def _cparams(sem):
    return pltpu.CompilerParams(dimension_semantics=sem, vmem_limit_bytes=VMEM_LIMIT_V7X)


def _sigmoid(x):
    return 1.0 / (1.0 + jnp.exp(-x))


NN = (((1,), (0,)), ((), ()))
NT = (((1,), (1,)), ((), ()))
TN = (((0,), (0,)), ((), ()))


def _mm(name, grid, ins, in_specs, compute, out_shape, out_specs, epilogue, dep=None):
    if dep is not None:
        ins, in_specs = tuple(ins) + (dep,), list(in_specs) + [pl.BlockSpec(dep.shape, lambda *_: (0, 0))]
    n_in = len(ins)

    def body(*refs):
        epilogue(compute(refs[:n_in]), refs[:n_in], refs[n_in:])

    return pl.pallas_call(
        body, name=name, grid=grid, in_specs=in_specs, out_specs=out_specs, out_shape=out_shape,
        compiler_params=_cparams(("parallel", "arbitrary")),
    )(*ins)


def _dot(dims, a=0, b=1):
    return lambda refs: [lax.dot_general(refs[a][...], refs[b][...], dims, preferred_element_type=F32)]


def _ffn_up(name, h, wgu3):
    S, D = h.shape
    Ns = wgu3.shape[2]
    F = 2 * Ns
    tm, tn = _tile(S, 512), _tile(Ns, 1408)
    nbs = Ns // tn

    def compute(refs):
        hv = refs[0][...]
        return [jnp.dot(hv, refs[1][...], preferred_element_type=F32), jnp.dot(hv, refs[2][...], preferred_element_type=F32)]

    def epi(accs, in_refs, out_refs):
        g, u = accs
        dgu_ref, a_ref = out_refs
        sg = _sigmoid(g)
        silu = g * sg
        dgu_ref[0] = (u * (sg * (1.0 + g * (1.0 - sg)))).astype(BF16)
        dgu_ref[1] = silu.astype(BF16)
        a_ref[...] = (silu * u).astype(BF16)

    return _mm(
        name, (F // tn, S // tm), (h, wgu3, wgu3),
        [pl.BlockSpec((tm, D), lambda j, i: (i, 0)),
         pl.BlockSpec((None, D, tn), lambda j, i: (j // nbs, 0, j % nbs)),
         pl.BlockSpec((None, D, tn), lambda j, i: (2 + j // nbs, 0, j % nbs))],
        compute, (jax.ShapeDtypeStruct((2, S, F), BF16), jax.ShapeDtypeStruct((S, F), BF16)),
        (pl.BlockSpec((2, tm, tn), lambda j, i: (0, i, j)), pl.BlockSpec((tm, tn), lambda j, i: (i, j))), epi)


def _mm_res(name, a, w, res, scale):
    S, K = a.shape
    N = w.shape[1]
    tm, tn = _tile(S, 512), _tile(N, 512 if K > 2816 else 1024)

    def epi(accs, in_refs, out_refs):
        out_refs[0][...] = in_refs[2][...] + scale * accs[0]

    return _mm(
        name, (N // tn, S // tm), (a, w, res),
        [pl.BlockSpec((tm, K), lambda j, i: (i, 0)), pl.BlockSpec((K, tn), lambda j, i: (0, j)),
         pl.BlockSpec((tm, tn), lambda j, i: (i, j))],
        _dot(NN), jax.ShapeDtypeStruct((S, N), F32), pl.BlockSpec((tm, tn), lambda j, i: (i, j)), epi)


def _mm_cols(name, a, w3, out_dtype):
    S, K = a.shape
    Ns = w3.shape[2]
    tm, tn = _tile(S, 512), _tile(Ns, 2304)
    nbs = Ns // tn

    def epi(accs, in_refs, out_refs):
        out_refs[0][...] = accs[0].astype(out_dtype)

    return _mm(
        name, (N_CHIPS * nbs, S // tm), (a, w3),
        [pl.BlockSpec((tm, K), lambda j, i: (i, 0)),
         pl.BlockSpec((None, K, tn), lambda j, i: (j // nbs, 0, j % nbs))],
        _dot(NN), jax.ShapeDtypeStruct((S, N_CHIPS * Ns), out_dtype), pl.BlockSpec((tm, tn), lambda j, i: (i, j)), epi)


def _mm_nt(name, a, w, out_dtype, scale=1.0):
    S, N = a.shape
    K = w.shape[0]
    tm, tn = _tile(S, 512), _tile(K, 1024)

    def epi(accs, in_refs, out_refs):
        out_refs[0][...] = (scale * accs[0]).astype(out_dtype)

    return _mm(
        name, (K // tn, S // tm), (a, w),
        [pl.BlockSpec((tm, N), lambda j, i: (i, 0)), pl.BlockSpec((tn, N), lambda j, i: (j, 0))],
        _dot(NT), jax.ShapeDtypeStruct((S, K), out_dtype), pl.BlockSpec((tm, tn), lambda j, i: (i, j)), epi)


def _ffn_down_bwd(name, dy, wd, gu, scale, dep=None):
    S, D = dy.shape
    F = wd.shape[0]
    tm, tn = _tile(S, 512), _tile(F, 1408)

    def epi(accs, in_refs, out_refs):
        da = scale * accs[0]
        out_refs[0][0] = (da * in_refs[2][0].astype(F32)).astype(BF16)
        out_refs[0][1] = (da * in_refs[2][1].astype(F32)).astype(BF16)

    return _mm(
        name, (F // tn, S // tm), (dy, wd, gu),
        [pl.BlockSpec((tm, D), lambda j, i: (i, 0)), pl.BlockSpec((tn, D), lambda j, i: (j, 0)),
         pl.BlockSpec((2, tm, tn), lambda j, i: (0, i, j))],
        _dot(NT), jax.ShapeDtypeStruct((2, S, F), BF16), pl.BlockSpec((2, tm, tn), lambda j, i: (0, i, j)), epi, dep=dep)


def _mm_nt_cols(name, a, w3, a_is_gu=False, dep=None):
    K, Ns = w3.shape[1], w3.shape[2]
    S = a.shape[1] if a_is_gu else a.shape[0]
    tm = _tile(S, 512)
    tn = _tile(K, max(LANES, (6 << 20) // (N_CHIPS * Ns * 2)))
    if a_is_gu:
        a_spec = pl.BlockSpec((2, tm, 2 * Ns), lambda i, j: (0, i, 0))
        part = lambda a_ref, s: a_ref[s // 2, :, (s % 2) * Ns:(s % 2 + 1) * Ns]
    else:
        a_spec = pl.BlockSpec((tm, N_CHIPS * Ns), lambda i, j: (i, 0))
        part = lambda a_ref, s: a_ref[:, s * Ns:(s + 1) * Ns]

    def compute(refs):
        total = None
        for s in range(N_CHIPS):
            prod = lax.dot_general(part(refs[0], s), refs[1][s], NT, preferred_element_type=F32)
            total = prod if total is None else total + prod
        return [total]

    def epi(accs, in_refs, out_refs):
        out_refs[0][...] = accs[0]

    return _mm(
        name, (S // tm, K // tn), (a, w3), [a_spec, pl.BlockSpec((N_CHIPS, tn, Ns), lambda i, j: (0, j, 0))],
        compute, jax.ShapeDtypeStruct((S, K), F32), pl.BlockSpec((tm, tn), lambda i, j: (i, j)), epi, dep=dep)


def _mm_tn(name, a, b, scale=1.0, dep=None):
    S, K = a.shape
    N = b.shape[1]
    tm, tn = _tile(K, 512), _tile(N, 1024)

    def epi(accs, in_refs, out_refs):
        out_refs[0][...] = (scale * accs[0]).astype(BF16)

    return _mm(
        name, (N // tn, K // tm), (a, b),
        [pl.BlockSpec((S, tm), lambda j, i: (0, i)), pl.BlockSpec((S, tn), lambda j, i: (0, j))],
        _dot(TN), jax.ShapeDtypeStruct((K, N), BF16), pl.BlockSpec((tm, tn), lambda j, i: (i, j)), epi, dep=dep)


def _mm_tn_cols(name, a, b, Ns, b_is_gu=False, dep=None):
    S, K = a.shape
    tm, tn = _tile(K, 512), _tile(Ns, 2304)
    nbs = Ns // tn
    if b_is_gu:
        b_spec = pl.BlockSpec((None, S, tn), lambda j, i: (j // (2 * nbs), 0, j % (2 * nbs)))
    else:
        b_spec = pl.BlockSpec((S, tn), lambda j, i: (0, j))

    def epi(accs, in_refs, out_refs):
        out_refs[0][...] = accs[0].astype(BF16)

    return _mm(
        name, (N_CHIPS * nbs, K // tm), (a, b), [pl.BlockSpec((S, tm), lambda j, i: (0, i)), b_spec],
        _dot(TN), jax.ShapeDtypeStruct((N_CHIPS, K, Ns), BF16),
        pl.BlockSpec((None, tm, tn), lambda j, i: (j // nbs, i, j % nbs)), epi, dep=dep)


def _rms_fwd(name, x, gain, dep=None):
    S, D = x.shape
    tm = _tile(S, 256, 8)
    extra = () if dep is None else (dep,)

    def body(x_ref, g_ref, *rest):
        h_ref = rest[-1]
        xv = x_ref[...]
        r = lax.rsqrt(jnp.mean(xv * xv, axis=-1, keepdims=True) + RMS_EPS)
        h_ref[...] = (xv * r * g_ref[...]).astype(BF16)

    return pl.pallas_call(
        body, name=name, grid=(S // tm,),
        in_specs=[pl.BlockSpec((tm, D), lambda i: (i, 0)), pl.BlockSpec((1, D), lambda i: (0, 0))]
        + [pl.BlockSpec(memory_space=pl.ANY) for d in extra],
        out_specs=pl.BlockSpec((tm, D), lambda i: (i, 0)), out_shape=jax.ShapeDtypeStruct((S, D), BF16),
        compiler_params=_cparams(("parallel",)),
    )(x, gain, *extra)


def _rms_bwd(name, x, gain, dh, dres):
    S, D = x.shape
    tm = _tile(S, 256, 8)

    def body(x_ref, g_ref, dh_ref, dres_ref, dx_ref, dxb_ref, dg_ref):
        i = pl.program_id(0)
        xv = x_ref[...]
        r = lax.rsqrt(jnp.mean(xv * xv, axis=-1, keepdims=True) + RMS_EPS)
        xhat = xv * r
        dhv = dh_ref[...]
        dxhat = dhv * g_ref[...]
        dx = dres_ref[...] + r * (dxhat - xhat * jnp.mean(dxhat * xhat, axis=-1, keepdims=True))
        dx_ref[...] = dx
        dxb_ref[...] = dx.astype(BF16)

        @pl.when(i == 0)
        def _():
            dg_ref[...] = jnp.zeros_like(dg_ref)

        dg_ref[...] += jnp.sum(dhv * xhat, axis=0, keepdims=True)

    row = pl.BlockSpec((tm, D), lambda i: (i, 0))
    vec = pl.BlockSpec((1, D), lambda i: (0, 0))
    return pl.pallas_call(
        body, name=name, grid=(S // tm,), in_specs=[row, vec, row, row], out_specs=(row, row, vec),
        out_shape=(jax.ShapeDtypeStruct((S, D), F32), jax.ShapeDtypeStruct((S, D), BF16), jax.ShapeDtypeStruct((1, D), F32)),
        compiler_params=_cparams(("arbitrary",)),
    )(x, gain, dh, dres)


def _loss_grad(name, y, target):
    S, D = y.shape
    tm = _tile(S, 256, 8)

    def body(y_ref, t_ref, dy_ref, dyb_ref, l_ref):
        i = pl.program_id(0)
        e = y_ref[...] - t_ref[...]
        dy_ref[...] = e * (1.0 / D)
        dyb_ref[...] = (e * (1.0 / D)).astype(BF16)
        col = jnp.sum(e * e, axis=0, keepdims=True)
        part = col[:, 0:LANES]
        for k in range(1, D // LANES):
            part = part + col[:, k * LANES:(k + 1) * LANES]

        @pl.when(i == 0)
        def _():
            l_ref[...] = jnp.zeros_like(l_ref)

        l_ref[...] += part

    row = pl.BlockSpec((tm, D), lambda i: (i, 0))
    return pl.pallas_call(
        body, name=name, grid=(S // tm,), in_specs=[row, row],
        out_specs=(row, row, pl.BlockSpec((1, LANES), lambda i: (0, 0))),
        out_shape=(jax.ShapeDtypeStruct((S, D), F32), jax.ShapeDtypeStruct((S, D), BF16), jax.ShapeDtypeStruct((1, LANES), F32)),
        compiler_params=_cparams(("arbitrary",)),
    )(y, target)


def _shift_down(u, k):
    rows = lax.broadcasted_iota(jnp.int32, u.shape, 0)
    return jnp.where(rows >= k, pltpu.roll(u, k, 0), 0.0)


def _shift_up(u, k):
    n = u.shape[0]
    rows = lax.broadcasted_iota(jnp.int32, u.shape, 0)
    return jnp.where(rows < n - k, pltpu.roll(u, n - k, 0), 0.0)


def _conv_specs(S, cw, conv_width):
    nb = conv_width // cw
    col = lambda off: pl.BlockSpec((S, cw), lambda j, off=off: (0, off * nb + j))
    return nb, col(0), col(1), col(2)


def _conv_fwd(name, proj, convw3, conv_width):
    S = proj.shape[0]
    cw = convw3.shape[2]
    nb, xc_s, bg_s, cg_s = _conv_specs(S, cw, conv_width)

    def body(xc_ref, bg_ref, cg_ref, w_ref, o_ref):
        u = cg_ref[...] * xc_ref[...]
        w = w_ref[...]
        cv = w[2:3, :] * u + w[1:2, :] * _shift_down(u, 1) + w[0:1, :] * _shift_down(u, 2)
        o_ref[...] = (bg_ref[...] * cv).astype(BF16)

    return pl.pallas_call(
        body, name=name, grid=(nb,),
        in_specs=[xc_s, bg_s, cg_s, pl.BlockSpec((None, CONV_K, cw), lambda j: (j, 0, 0))],
        out_specs=pl.BlockSpec((S, cw), lambda j: (0, j)), out_shape=jax.ShapeDtypeStruct((S, conv_width), BF16),
        compiler_params=_cparams(("parallel",)),
    )(proj, proj, proj, convw3)


def _conv_bwd(name, proj, convw3, da, conv_width):
    S = proj.shape[0]
    cw = convw3.shape[2]
    nb, xc_s, bg_s, cg_s = _conv_specs(S, cw, conv_width)

    def body(xc_ref, bg_ref, cg_ref, w_ref, da_ref, dxc_ref, dbg_ref, dcg_ref, dw_ref):
        xc, cg = xc_ref[...], cg_ref[...]
        u = cg * xc
        w = w_ref[...]
        u1, u2 = _shift_down(u, 1), _shift_down(u, 2)
        cv = w[2:3, :] * u + w[1:2, :] * u1 + w[0:1, :] * u2
        dav = da_ref[...]
        dbg_ref[...] = (dav * cv).astype(BF16)
        dcv = dav * bg_ref[...]
        du = w[2:3, :] * dcv + w[1:2, :] * _shift_up(dcv, 1) + w[0:1, :] * _shift_up(dcv, 2)
        dxc_ref[...] = (du * cg).astype(BF16)
        dcg_ref[...] = (du * xc).astype(BF16)
        dw_ref[0:1, :] = jnp.sum(dcv * u2, axis=0, keepdims=True)
        dw_ref[1:2, :] = jnp.sum(dcv * u1, axis=0, keepdims=True)
        dw_ref[2:3, :] = jnp.sum(dcv * u, axis=0, keepdims=True)

    wspec = pl.BlockSpec((None, CONV_K, cw), lambda j: (j, 0, 0))
    ospec = pl.BlockSpec((S, cw), lambda j: (0, j))
    act = jax.ShapeDtypeStruct((S, conv_width), BF16)
    return pl.pallas_call(
        body, name=name, grid=(nb,), in_specs=[xc_s, bg_s, cg_s, wspec, ospec],
        out_specs=(ospec, ospec, ospec, wspec),
        out_shape=(act, act, act, jax.ShapeDtypeStruct(convw3.shape, F32)),
        compiler_params=_cparams(("parallel",)),
    )(proj, proj, proj, convw3, da)


def _rope_consts(S, dh):
    rot = dh // 4
    half = rot // 2
    inv_freq = 1.0 / (ROPE_THETA ** (jnp.arange(0, rot, 2, dtype=F32) / rot))
    ang = jnp.arange(S, dtype=F32)[:, None] * inv_freq[None, :]
    cos = jnp.concatenate([jnp.cos(ang), jnp.cos(ang), jnp.ones((S, dh - rot), F32)], axis=1)
    sin = jnp.concatenate([jnp.sin(ang), jnp.sin(ang), jnp.zeros((S, dh - rot), F32)], axis=1)
    rm = np.zeros((dh, dh), np.float32)
    for j in range(half):
        rm[j + half, j] = -1.0
        rm[j, j + half] = 1.0
    return cos, sin, jnp.asarray(rm, BF16), jnp.asarray(rm.T, BF16)


def _exact_perm(y, rm):
    hi = y.astype(BF16)
    r1 = y - hi.astype(F32)
    mid = r1.astype(BF16)
    lo = (r1 - mid.astype(F32)).astype(BF16)
    dot = lambda a: jnp.dot(a, rm, preferred_element_type=F32)
    return dot(hi) + dot(mid) + dot(lo)


def _qk_prep(name, xh, gain, cos, sin, rm):
    H, S, dh = xh.shape
    tm = _tile(S, 1024, 8)

    def body(x_ref, g_ref, c_ref, s_ref, rm_ref, o_ref):
        xv = x_ref[...]
        y = xv * lax.rsqrt(jnp.mean(xv * xv, axis=-1, keepdims=True) + RMS_EPS) * g_ref[...]
        o_ref[...] = (y * c_ref[...] + _exact_perm(y, rm_ref[...]) * s_ref[...]).astype(BF16)

    blk = pl.BlockSpec((None, tm, dh), lambda h, i: (h, i, 0))
    tab = pl.BlockSpec((tm, dh), lambda h, i: (i, 0))
    return pl.pallas_call(
        body, name=name, grid=(H, S // tm),
        in_specs=[blk, pl.BlockSpec((1, dh), lambda h, i: (0, 0)), tab, tab, pl.BlockSpec((dh, dh), lambda h, i: (0, 0))],
        out_specs=blk, out_shape=jax.ShapeDtypeStruct((H, S, dh), BF16),
        compiler_params=_cparams(("parallel", "parallel")),
    )(xh, gain, cos, sin, rm)


def _qk_prep_bwd(name, xh, gain, cos, sin, rmt, dout):
    H, S, dh = xh.shape
    tm = _tile(S, 1024, 8)

    def body(x_ref, g_ref, c_ref, s_ref, rmt_ref, do_ref, dx_ref, dg_ref):
        first = (pl.program_id(0) == 0) & (pl.program_id(1) == 0)
        xv = x_ref[...]
        r = lax.rsqrt(jnp.mean(xv * xv, axis=-1, keepdims=True) + RMS_EPS)
        xhat = xv * r
        dov = do_ref[...]
        dy = dov * c_ref[...] + _exact_perm(dov * s_ref[...], rmt_ref[...])
        dxhat = dy * g_ref[...]
        dx_ref[...] = (r * (dxhat - xhat * jnp.mean(dxhat * xhat, axis=-1, keepdims=True))).astype(BF16)

        @pl.when(first)
        def _():
            dg_ref[...] = jnp.zeros_like(dg_ref)

        dg_ref[...] += jnp.sum(dy * xhat, axis=0, keepdims=True)

    blk = pl.BlockSpec((None, tm, dh), lambda h, i: (h, i, 0))
    tab = pl.BlockSpec((tm, dh), lambda h, i: (i, 0))
    vec = pl.BlockSpec((1, dh), lambda h, i: (0, 0))
    return pl.pallas_call(
        body, name=name, grid=(H, S // tm),
        in_specs=[blk, vec, tab, tab, pl.BlockSpec((dh, dh), lambda h, i: (0, 0)), blk],
        out_specs=(blk, vec), out_shape=(jax.ShapeDtypeStruct((H, S, dh), BF16), jax.ShapeDtypeStruct((1, dh), F32)),
        compiler_params=_cparams(("arbitrary", "arbitrary")),
    )(xh, gain, cos, sin, rmt, dout)


def _attn_probs(q, kp, kc, sink_col, n, scale):
    rows = q.shape[0]
    sp = lax.dot_general(q, kp, NT, preferred_element_type=F32) * scale
    sc = lax.dot_general(q, kc, NT, preferred_element_type=F32) * scale
    qi = lax.broadcasted_iota(jnp.int32, (rows, BLOCK), 0) % BLOCK
    kj = lax.broadcasted_iota(jnp.int32, (rows, BLOCK), 1)
    sp = jnp.where((kj > qi) & (n > 0), sp, NEG_INF)
    sc = jnp.where(kj <= qi, sc, NEG_INF)
    m = jnp.maximum(jnp.maximum(jnp.max(sp, axis=-1, keepdims=True), jnp.max(sc, axis=-1, keepdims=True)), sink_col)
    pp, pc, ps = jnp.exp(sp - m), jnp.exp(sc - m), jnp.exp(sink_col - m)
    inv = 1.0 / (jnp.sum(pp, axis=-1, keepdims=True) + jnp.sum(pc, axis=-1, keepdims=True) + ps)
    return pp * inv, pc * inv, ps * inv


def _sink_col(sink_ref, hk, group):
    rows = group * BLOCK
    g = lax.broadcasted_iota(jnp.int32, (rows, 1), 0) // BLOCK
    col = jnp.zeros((rows, 1), F32)
    for i in range(group):
        col = jnp.where(g == i, sink_ref[hk * group + i], col)
    return col


def _attn_specs(group, S, dh):
    heads = pl.BlockSpec((group, S, dh), lambda hk: (hk, 0, 0))
    kv = pl.BlockSpec((None, S, dh), lambda hk: (hk, 0, 0))
    return heads, kv, pl.BlockSpec(memory_space=pltpu.SMEM)


def _block_rows(n):
    cur = pl.ds(pl.multiple_of(n * BLOCK, BLOCK), BLOCK)
    prev = pl.ds(pl.multiple_of(jnp.maximum(n - 1, 0) * BLOCK, BLOCK), BLOCK)
    return cur, prev


def _attn_fwd(name, q, k, v, sinks):
    HQ, S, dh = q.shape
    HKV = k.shape[0]
    group = HQ // HKV
    scale = dh ** -0.5
    heads, kv, smem = _attn_specs(group, S, dh)

    def body(q_ref, k_ref, v_ref, sink_ref, o_ref):
        sink = _sink_col(sink_ref, pl.program_id(0), group)

        def block(n, carry):
            cur, prev = _block_rows(n)
            qv = q_ref[:, cur, :].reshape(group * BLOCK, dh)
            pp, pc, _ = _attn_probs(qv, k_ref[prev, :], k_ref[cur, :], sink, n, scale)
            o = jnp.dot(pp.astype(BF16), v_ref[prev, :], preferred_element_type=F32)
            o = o + jnp.dot(pc.astype(BF16), v_ref[cur, :], preferred_element_type=F32)
            o_ref[:, cur, :] = o.reshape(group, BLOCK, dh).astype(BF16)
            return carry

        lax.fori_loop(0, S // BLOCK, block, 0)

    return pl.pallas_call(
        body, name=name, grid=(HKV,), in_specs=[heads, kv, kv, smem], out_specs=heads,
        out_shape=jax.ShapeDtypeStruct((HQ, S, dh), BF16), compiler_params=_cparams(("parallel",)),
    )(q, k, v, sinks)


def _attn_bwd(name, q, k, v, sinks, do):
    HQ, S, dh = q.shape
    HKV = k.shape[0]
    group = HQ // HKV
    scale = dh ** -0.5
    heads, kv, smem = _attn_specs(group, S, dh)
    sk = pl.BlockSpec((None, group, LANES), lambda hk: (hk, 0, 0))

    def body(q_ref, k_ref, v_ref, sink_ref, do_ref, dq_ref, dk_ref, dv_ref, ds_ref):
        rows = group * BLOCK
        sink = _sink_col(sink_ref, pl.program_id(0), group)
        dk_ref[...] = jnp.zeros_like(dk_ref)
        dv_ref[...] = jnp.zeros_like(dv_ref)
        tdot = lambda a, b: lax.dot_general(a, b, TN, preferred_element_type=F32)

        def block(n, dsink):
            cur, prev = _block_rows(n)
            qv = q_ref[:, cur, :].reshape(rows, dh)
            dov = do_ref[:, cur, :].reshape(rows, dh)
            kp, kc, vp, vc = k_ref[prev, :], k_ref[cur, :], v_ref[prev, :], v_ref[cur, :]
            pp, pc, ps = _attn_probs(qv, kp, kc, sink, n, scale)
            dpp = lax.dot_general(dov, vp, NT, preferred_element_type=F32)
            dpc = lax.dot_general(dov, vc, NT, preferred_element_type=F32)
            delta = jnp.sum(pp * dpp, axis=-1, keepdims=True) + jnp.sum(pc * dpc, axis=-1, keepdims=True)
            dsp = (pp * (dpp - delta) * scale).astype(BF16)
            dsc = (pc * (dpc - delta) * scale).astype(BF16)
            dq = jnp.dot(dsp, kp, preferred_element_type=F32) + jnp.dot(dsc, kc, preferred_element_type=F32)
            dq_ref[:, cur, :] = dq.reshape(group, BLOCK, dh)
            dk_ref[prev, :] += tdot(dsp, qv)
            dv_ref[prev, :] += tdot(pp.astype(BF16), dov)
            dk_ref[cur, :] += tdot(dsc, qv)
            dv_ref[cur, :] += tdot(pc.astype(BF16), dov)
            return dsink - jnp.sum((ps * delta).reshape(group, BLOCK, 1), axis=1)

        dsink = lax.fori_loop(0, S // BLOCK, block, jnp.zeros((group, 1), F32))
        ds_ref[...] = jnp.broadcast_to(dsink, (group, LANES))

    return pl.pallas_call(
        body, name=name, grid=(HKV,), in_specs=[heads, kv, kv, smem, heads], out_specs=(heads, kv, kv, sk),
        out_shape=(jax.ShapeDtypeStruct((HQ, S, dh), F32), jax.ShapeDtypeStruct((HKV, S, dh), F32),
                   jax.ShapeDtypeStruct((HKV, S, dh), F32), jax.ShapeDtypeStruct((HKV, group, LANES), F32)),
        compiler_params=_cparams(("parallel",)),
    )(q, k, v, sinks, do)


def _gate_specs(S, D, ga_off, gb_off):
    tg = LANES
    for t in range(LANES, 513, LANES):
        if D % t == 0 and ga_off % t == 0 and gb_off % t == 0:
            tg = t
    if D % LANES:
        tg = math.gcd(math.gcd(D, ga_off), gb_off)
    tm = _tile(S, 512, 8)
    act = pl.BlockSpec((tm, tg), lambda i, j: (i, j))
    ga = pl.BlockSpec((tm, tg), lambda i, j: (i, ga_off // tg + j))
    gb = pl.BlockSpec((tm, tg), lambda i, j: (i, gb_off // tg + j))
    return (S // tm, D // tg), act, ga, gb


def _gate_fwd(name, proj, ya, yb, ga_off, gb_off):
    S, D = ya.shape
    grid, act, ga, gb = _gate_specs(S, D, ga_off, gb_off)

    def body(ga_ref, gb_ref, ya_ref, yb_ref, o_ref):
        o_ref[...] = (_sigmoid(ga_ref[...]) * ya_ref[...] + _sigmoid(gb_ref[...]) * yb_ref[...]).astype(BF16)

    return pl.pallas_call(
        body, name=name, grid=grid, in_specs=[ga, gb, act, act], out_specs=act,
        out_shape=jax.ShapeDtypeStruct((S, D), BF16), compiler_params=_cparams(("parallel", "parallel")),
    )(proj, proj, ya, yb)


def _gate_bwd(name, proj, ya, yb, dm, ga_off, gb_off):
    S, D = ya.shape
    grid, act, ga, gb = _gate_specs(S, D, ga_off, gb_off)

    def body(ga_ref, gb_ref, ya_ref, yb_ref, dm_ref, dga_ref, dgb_ref, dya_ref, dyb_ref):
        dmv = dm_ref[...]
        sa, sb = _sigmoid(ga_ref[...]), _sigmoid(gb_ref[...])
        dga_ref[...] = (dmv * ya_ref[...] * sa * (1.0 - sa)).astype(BF16)
        dgb_ref[...] = (dmv * yb_ref[...] * sb * (1.0 - sb)).astype(BF16)
        dya_ref[...] = (dmv * sa).astype(BF16)
        dyb_ref[...] = (dmv * sb).astype(BF16)

    o = jax.ShapeDtypeStruct((S, D), BF16)
    return pl.pallas_call(
        body, name=name, grid=grid, in_specs=[ga, gb, act, act, act], out_specs=(act, act, act, act),
        out_shape=(o, o, o, o), compiler_params=_cparams(("parallel", "parallel")),
    )(proj, proj, ya, yb, dm)


ANY = pl.BlockSpec(memory_space=pl.ANY)


def _row_tile(rows, cols, n_arrays):
    want = max(16, (VMEM_LIMIT_V7X // 2) // (2 * n_arrays * cols * 4))
    return _tile(rows, want, 16)


def _cast_to_slot(name, w, dtype, p_arr, dep=None):
    R, C = w.shape
    tr = _row_tile(R, C, 2)
    extra = () if dep is None else (dep,)

    def body(p_ref, w_ref, *rest):
        rest[-1][...] = w_ref[...].astype(dtype)

    return pl.pallas_call(
        body, name=name,
        grid_spec=pltpu.PrefetchScalarGridSpec(
            num_scalar_prefetch=1, grid=(R // tr,),
            in_specs=[pl.BlockSpec((tr, C), lambda i, p_ref: (i, 0))] + [pl.BlockSpec(d.shape, lambda i, p_ref: (0, 0)) for d in extra],
            out_specs=pl.BlockSpec((None, tr, C), lambda i, p_ref: (p_ref[0], i, 0))),
        out_shape=jax.ShapeDtypeStruct((N_CHIPS, R, C), dtype), compiler_params=_cparams(("parallel",)),
    )(p_arr, w, *extra)


def _add_half(name, g3, r3, c_arr):
    n, h, C = r3.shape
    tr = _row_tile(h, C, 3)
    nb = h // tr

    def body(c_ref, g_ref, r_ref, o_ref):
        o_ref[...] = (g_ref[...].astype(F32) + r_ref[...].astype(F32)).astype(BF16)

    blk = pl.BlockSpec((None, tr, C), lambda s, i, c_ref: (s, i, 0))
    return pl.pallas_call(
        body, name=name,
        grid_spec=pltpu.PrefetchScalarGridSpec(
            num_scalar_prefetch=1, grid=(n, nb),
            in_specs=[pl.BlockSpec((None, tr, C), lambda s, i, c_ref: (s, c_ref[0] * nb + i, 0)), blk], out_specs=blk),
        out_shape=jax.ShapeDtypeStruct(r3.shape, BF16), compiler_params=_cparams(("parallel", "parallel")),
    )(c_arr, g3, r3)


def _add_chips(name, t3, r3, cp_arr):
    n, h, C = r3.shape
    tr = _row_tile(h, C, 6)
    nb = h // tr

    def body(cp_ref, t_ref, r0_ref, r1_ref, r2_ref, r3_ref, o_ref):
        p = cp_ref[1]
        total = None
        for a, r_ref in enumerate((r0_ref, r1_ref, r2_ref, r3_ref)):
            part = jnp.where(p == a, t_ref[...], r_ref[...]).astype(F32)
            total = part if total is None else total + part
        o_ref[...] = total

    def part(a):
        return pl.BlockSpec((None, tr, C), lambda i, cp_ref: (jnp.where(cp_ref[1] == a, (a + 1) % N_CHIPS, a), i, 0))

    return pl.pallas_call(
        body, name=name,
        grid_spec=pltpu.PrefetchScalarGridSpec(
            num_scalar_prefetch=1, grid=(nb,),
            in_specs=[pl.BlockSpec((None, tr, C), lambda i, cp_ref: (cp_ref[1], i, 0)), part(0), part(1), part(2), part(3)],
            out_specs=pl.BlockSpec((tr, C), lambda i, cp_ref: (cp_ref[0] * nb + i, 0))),
        out_shape=jax.ShapeDtypeStruct((2 * h, C), F32), compiler_params=_cparams(("parallel",)),
    )(cp_arr, t3, r3, r3, r3, r3)


def _adamw(name, w, g, m, v, deps=()):
    R, C = w.shape
    extra = tuple(deps)
    tr = _row_tile(R, C, 8)
    c1 = 1.0 - ADAM_B1 ** ADAM_STEP
    c2 = 1.0 - ADAM_B2 ** ADAM_STEP

    def body(w_ref, g_ref, m_ref, v_ref, *rest):
        go_ref, d_ref, nm_ref, nv_ref = rest[-4:]
        gv = g_ref[...]
        go_ref[...] = gv
        nm = ADAM_B1 * m_ref[...] + (1.0 - ADAM_B1) * gv
        nv = ADAM_B2 * v_ref[...] + (1.0 - ADAM_B2) * (gv * gv)
        d_ref[...] = -ADAM_LR * ((nm / c1) / (jnp.sqrt(nv / c2) + ADAM_EPS) + ADAM_WD * w_ref[...])
        nm_ref[...] = nm
        nv_ref[...] = nv

    blk = pl.BlockSpec((tr, C), lambda i: (i, 0))
    o = jax.ShapeDtypeStruct((R, C), F32)
    return pl.pallas_call(
        body, name=name, grid=(R // tr,), in_specs=[blk, blk, blk, blk] + [ANY] * len(extra), out_specs=(blk, blk, blk, blk),
        out_shape=(o, o, o, o), compiler_params=_cparams(("parallel",)),
    )(w, g, m, v, *extra)


def _place():
    x, y, c = lax.axis_index("x"), lax.axis_index("y"), lax.axis_index("c")
    chips = [(1 - x, y), (x, 1 - y), (1 - x, 1 - y)]
    return x, y, c, 2 * x + y, chips


HBM = pl.BlockSpec(memory_space=pltpu.HBM)
SEM = pl.BlockSpec(memory_space=pltpu.SEMAPHORE)
TOKEN = jax.ShapeDtypeStruct((8, LANES), F32)
DATAFLOW = pltpu.SideEffectType.DATAFLOW_SIDE_EFFECTING


def _hbm(a):
    return pltpu.with_memory_space_constraint(a, pltpu.HBM)


def _gather_blocks(bufs, i, c, p, chips):
    if bufs[i].shape[1] % 16:
        return bufs[i].at[p], [bufs[i].at[2 * cx + cy] for cx, cy in chips]
    h = bufs[i].shape[1] // 2
    rows = pl.ds(pl.multiple_of(c * h, 16), h)
    return bufs[i].at[p, rows], [bufs[i].at[2 * cx + cy, rows] for cx, cy in chips]


def _gather_start(name, slots, dep):
    n = len(slots)

    def body(*refs):
        bufs, send, recv, token = refs[:n], refs[n + 1], refs[n + 2], refs[-1]
        x, y, c, p, chips = _place()
        for i in range(n):
            mine, _ = _gather_blocks(bufs, i, c, p, chips)
            for j, chip in enumerate(chips):
                pltpu.make_async_remote_copy(src_ref=mine, dst_ref=mine, send_sem=send.at[3 * i + j], recv_sem=recv.at[3 * i + j],
                                             device_id=(*chip, c), device_id_type=MESH).start()
        token[...] = jnp.zeros_like(token)

    out = pl.pallas_call(
        body, name=name, in_specs=[HBM] * n + [ANY],
        out_specs=(SEM, SEM, *([HBM] * n), pl.BlockSpec(memory_space=pltpu.VMEM)),
        out_shape=(pltpu.SemaphoreType.DMA((3 * n,)), pltpu.SemaphoreType.DMA((3 * n,)),
                   *[pltpu.HBM(s.shape, s.dtype) for s in slots], TOKEN),
        input_output_aliases={i: 2 + i for i in range(n)},
        compiler_params=pltpu.CompilerParams(has_side_effects=DATAFLOW),
    )(*[_hbm(s) for s in slots], dep)
    return out[0], out[1], list(out[2:2 + n]), out[-1]


def _gather_wait(name, send, recv, slots, after):
    n = len(slots)

    def body(*refs):
        bufs, send, recv = refs[:n], refs[n], refs[n + 1]
        x, y, c, p, chips = _place()
        for i in range(n):
            mine, landed = _gather_blocks(bufs, i, c, p, chips)
            for j, chip in enumerate(chips):
                cp = pltpu.make_async_remote_copy(src_ref=mine, dst_ref=landed[j], send_sem=send.at[3 * i + j],
                                                  recv_sem=recv.at[3 * i + j], device_id=(*chip, c), device_id_type=MESH)
                cp.wait_send()
                cp.wait_recv()

    return list(pl.pallas_call(
        body, name=name, in_specs=[HBM] * n + [SEM, SEM, ANY], out_specs=tuple([HBM] * n),
        out_shape=tuple(pltpu.HBM(s.shape, s.dtype) for s in slots),
        input_output_aliases={i: i for i in range(n)},
        compiler_params=pltpu.CompilerParams(has_side_effects=DATAFLOW),
    )(*slots, send, recv, after))


def _gather_forward(name, slots):
    idx = [i for i, s in enumerate(slots) if s.shape[1] % 16 == 0]
    n = len(slots)

    def body(*refs):
        bufs = refs[n:2 * n]
        send, recv = refs[2 * n:]
        x, y, c, p, chips = _place()

        def rdma(k, ref):
            return pltpu.make_async_remote_copy(src_ref=ref, dst_ref=ref, send_sem=send.at[k], recv_sem=recv.at[k],
                                                device_id=(x, y, 1 - c), device_id_type=MESH)

        cps = []
        for k, i in enumerate(idx):
            for j, ref in enumerate(_gather_blocks(bufs, i, c, p, chips)[1]):
                cps.append(rdma(3 * k + j, ref))
                cps[-1].start()
        for k, i in enumerate(idx):
            for j, ref in enumerate(_gather_blocks(bufs, i, 1 - c, p, chips)[1]):
                rdma(3 * k + j, ref).wait_recv()
        for cp in cps:
            cp.wait_send()

    return list(pl.pallas_call(
        body, name=name, in_specs=[ANY] * n, out_specs=tuple([ANY] * n),
        out_shape=tuple(jax.ShapeDtypeStruct(s.shape, s.dtype) for s in slots),
        scratch_shapes=[pltpu.SemaphoreType.DMA((3 * len(idx),)), pltpu.SemaphoreType.DMA((3 * len(idx),))],
        input_output_aliases={i: i for i in range(n)},
        compiler_params=pltpu.CompilerParams(has_side_effects=True),
    )(*slots))


def _swap_copy(grads, lands, send, recv, i, x, y, c):
    h = grads[i].shape[1] // 2
    other = pl.ds(pl.multiple_of((1 - c) * h, 16), h)
    return pltpu.make_async_remote_copy(src_ref=grads[i].at[:, other, :], dst_ref=lands[i], send_sem=send.at[i],
                                        recv_sem=recv.at[i], device_id=(x, y, 1 - c), device_id_type=MESH)


def _swap_start(name, grads):
    n = len(grads)

    def body(*refs):
        ins, lands, send, recv, token = refs[:n], refs[n:2 * n], refs[2 * n], refs[2 * n + 1], refs[-1]
        x, y, c, p, chips = _place()
        for i in range(n):
            _swap_copy(ins, lands, send, recv, i, x, y, c).start()
        token[...] = jnp.zeros_like(token)

    gshapes = [pltpu.HBM(g.shape, g.dtype) for g in grads]
    halves = [(g.shape[0], g.shape[1] // 2, g.shape[2]) for g in grads]
    lshapes = [pltpu.HBM(s, g.dtype) for s, g in zip(halves, grads)]
    out = pl.pallas_call(
        body, name=name, in_specs=[HBM] * (2 * n),
        out_specs=(SEM, SEM, *([HBM] * (2 * n)), pl.BlockSpec(memory_space=pltpu.VMEM)),
        out_shape=(pltpu.SemaphoreType.DMA((n,)), pltpu.SemaphoreType.DMA((n,)), *gshapes, *lshapes, TOKEN),
        input_output_aliases={i: 2 + i for i in range(2 * n)},
        compiler_params=pltpu.CompilerParams(has_side_effects=DATAFLOW),
    )(*[_hbm(g) for g in grads], *[_hbm(lax.empty(s, g.dtype)) for s, g in zip(halves, grads)])
    return out[0], out[1], list(out[2:2 + n]), list(out[2 + n:2 + 2 * n]), out[-1]


def _swap_wait(name, send, recv, grads, lands, after):
    n = len(grads)

    def body(*refs):
        ins, lands, send, recv = refs[:n], refs[n:2 * n], refs[2 * n], refs[2 * n + 1]
        x, y, c, p, chips = _place()
        for i in range(n):
            cp = _swap_copy(ins, lands, send, recv, i, x, y, c)
            cp.wait_send()
            cp.wait_recv()

    shapes = [pltpu.HBM(t.shape, t.dtype) for t in list(grads) + list(lands)]
    out = pl.pallas_call(
        body, name=name, in_specs=[HBM] * (2 * n) + [SEM, SEM, ANY], out_specs=tuple([HBM] * (2 * n)),
        out_shape=tuple(shapes), input_output_aliases={i: i for i in range(2 * n)},
        compiler_params=pltpu.CompilerParams(has_side_effects=DATAFLOW),
    )(*grads, *lands, send, recv, after)
    return list(out[:n]), list(out[n:])


def _exchange_start(name, parts):
    n = len(parts)

    def body(*refs):
        ins, lands, send, recv, token = refs[:n], refs[n:2 * n], refs[2 * n], refs[2 * n + 1], refs[-1]
        x, y, c, p, chips = _place()
        for i in range(n):
            for j, (cx, cy) in enumerate(chips):
                pltpu.make_async_remote_copy(src_ref=ins[i].at[2 * cx + cy], dst_ref=lands[i].at[p], send_sem=send.at[3 * i + j],
                                             recv_sem=recv.at[3 * i + j], device_id=(cx, cy, c), device_id_type=MESH).start()
        token[...] = jnp.zeros_like(token)

    shapes = [pltpu.HBM(t.shape, t.dtype) for t in parts]
    out = pl.pallas_call(
        body, name=name, in_specs=[HBM] * (2 * n),
        out_specs=(SEM, SEM, *([HBM] * (2 * n)), pl.BlockSpec(memory_space=pltpu.VMEM)),
        out_shape=(pltpu.SemaphoreType.DMA((3 * n,)), pltpu.SemaphoreType.DMA((3 * n,)), *shapes, *shapes, TOKEN),
        input_output_aliases={i: 2 + i for i in range(2 * n)},
        compiler_params=pltpu.CompilerParams(has_side_effects=DATAFLOW),
    )(*[_hbm(t) for t in parts], *[_hbm(lax.empty(t.shape, t.dtype)) for t in parts])
    return out[0], out[1], list(out[2:2 + n]), list(out[2 + n:2 + 2 * n]), out[-1]


def _exchange_wait(name, send, recv, parts, lands, after):
    n = len(parts)

    def body(*refs):
        ins, lands, send, recv = refs[:n], refs[n:2 * n], refs[2 * n], refs[2 * n + 1]
        x, y, c, p, chips = _place()
        for i in range(n):
            for j, (cx, cy) in enumerate(chips):
                q = 2 * cx + cy
                cp = pltpu.make_async_remote_copy(src_ref=ins[i].at[q], dst_ref=lands[i].at[q], send_sem=send.at[3 * i + j],
                                                  recv_sem=recv.at[3 * i + j], device_id=(cx, cy, c), device_id_type=MESH)
                cp.wait_send()
                cp.wait_recv()

    shapes = [pltpu.HBM(t.shape, t.dtype) for t in parts]
    out = pl.pallas_call(
        body, name=name, in_specs=[HBM] * (2 * n) + [SEM, SEM, ANY], out_specs=tuple([HBM] * (2 * n)),
        out_shape=(*shapes, *shapes), input_output_aliases={i: i for i in range(2 * n)},
        compiler_params=pltpu.CompilerParams(has_side_effects=DATAFLOW),
    )(*parts, *lands, send, recv, after)
    return list(out[:n]), list(out[n:])


def _join_copy(bufs, send, recv, i, which, x, y, c):
    h = bufs[i].shape[0] // 2
    rows = bufs[i].at[pl.ds(pl.multiple_of(which * h, 8), h)]
    return pltpu.make_async_remote_copy(src_ref=rows, dst_ref=rows, send_sem=send.at[i], recv_sem=recv.at[i],
                                        device_id=(x, y, 1 - c), device_id_type=MESH)


def _join_start(name, bufs):
    n = len(bufs)

    def body(*refs):
        ins, send, recv, token = refs[:n], refs[n], refs[n + 1], refs[-1]
        x, y, c, p, chips = _place()
        for i in range(n):
            _join_copy(ins, send, recv, i, c, x, y, c).start()
        token[...] = jnp.zeros_like(token)

    out = pl.pallas_call(
        body, name=name, in_specs=[HBM] * n,
        out_specs=(SEM, SEM, *([HBM] * n), pl.BlockSpec(memory_space=pltpu.VMEM)),
        out_shape=(pltpu.SemaphoreType.DMA((n,)), pltpu.SemaphoreType.DMA((n,)), *[pltpu.HBM(t.shape, t.dtype) for t in bufs], TOKEN),
        input_output_aliases={i: 2 + i for i in range(n)},
        compiler_params=pltpu.CompilerParams(has_side_effects=DATAFLOW),
    )(*[_hbm(t) for t in bufs])
    return out[0], out[1], list(out[2:2 + n]), out[-1]


def _join_wait(name, send, recv, bufs, after):
    n = len(bufs)

    def body(*refs):
        ins, send, recv = refs[:n], refs[n], refs[n + 1]
        x, y, c, p, chips = _place()
        for i in range(n):
            _join_copy(ins, send, recv, i, c, x, y, c).wait_send()
            _join_copy(ins, send, recv, i, 1 - c, x, y, c).wait_recv()

    return list(pl.pallas_call(
        body, name=name, in_specs=[HBM] * n + [SEM, SEM, ANY], out_specs=tuple([HBM] * n),
        out_shape=tuple(pltpu.HBM(t.shape, t.dtype) for t in bufs), input_output_aliases={i: i for i in range(n)},
        compiler_params=pltpu.CompilerParams(has_side_effects=DATAFLOW),
    )(*bufs, send, recv, after))


def _allreduce_small(name, pack, dep):
    R, W = pack.shape

    def body(in_ref, dep_ref, out_ref, slots, send, recv):
        x, y, c = lax.axis_index("x"), lax.axis_index("y"), lax.axis_index("c")
        me = 4 * x + 2 * y + c
        slots[0] = in_ref[...]
        cps = []
        for k in range(1, N_DEV):
            peer = (x ^ (k >> 2), y ^ ((k >> 1) & 1), c ^ (k & 1))
            cp = pltpu.make_async_remote_copy(src_ref=in_ref, dst_ref=slots.at[k], send_sem=send.at[k - 1],
                                              recv_sem=recv.at[k - 1], device_id=peer, device_id_type=MESH)
            cp.start()
            cps.append(cp)
        for cp in cps:
            cp.wait()
        total = slots[me]
        for a in range(1, N_DEV):
            total = total + slots[jnp.bitwise_xor(a, me)]
        out_ref[...] = total

    vmem = pl.BlockSpec(memory_space=pltpu.VMEM)
    return pl.pallas_call(
        body, name=name, in_specs=[vmem, ANY], out_specs=vmem, out_shape=jax.ShapeDtypeStruct((R, W), F32),
        scratch_shapes=[pltpu.VMEM((N_DEV, R, W), F32), pltpu.SemaphoreType.DMA((N_DEV - 1,)), pltpu.SemaphoreType.DMA((N_DEV - 1,))],
        compiler_params=pltpu.CompilerParams(has_side_effects=True),
    )(pack, dep)


def _heads(a, n_heads):
    S = a.shape[0]
    return a.reshape(S, n_heads, a.shape[1] // n_heads).transpose(1, 0, 2)


def _unheads(a):
    H, S, dh = a.shape
    return a.transpose(1, 0, 2).reshape(S, H * dh)


def _ffn_bwd(tag, xin, gain, wgu3, wd, saved, dxout, dxo_b, reduce_start, dep, flush=None):
    h, gu, act = saved
    D = xin.shape[1]
    tok = reduce_start({f"w_down{tag}": _mm_tn(f"dw_down_{tag}", act, dxo_b, 0.5, dep=dep).reshape(N_CHIPS, -1, D)})
    dgu = _ffn_down_bwd(f"ffn_down_bwd_{tag}", dxo_b, wd, gu, 0.5, dep=tok)
    tok = reduce_start({f"w_gu{tag}": _mm_tn_cols(f"dw_gu_{tag}", h, dgu, wgu3.shape[2], b_is_gu=True)})
    if flush is not None:
        tok = flush(tok)
    dh = _mm_nt_cols(f"ffn_up_bwd_{tag}", dgu, wgu3, a_is_gu=True, dep=tok)
    dxin, dxin_b, dgain = _rms_bwd(f"rms_bwd_{tag}", xin, gain, dh, dxout)
    return dxin, dxin_b, dgain, tok


def kernel(x, g_ffn1, w_gu1, w_down1, g_mix, w_in, conv_w, q_norm_g, k_norm_g, sinks, w_out_conv, w_out_attn, w_o, g_ffn2, w_gu2, w_down2, loss_target, m_g_ffn1, m_w_gu1, m_w_down1, m_g_mix, m_w_in, m_conv_w, m_q_norm_g, m_k_norm_g, m_sinks, m_w_out_conv, m_w_out_attn, m_w_o, m_g_ffn2, m_w_gu2, m_w_down2, v_g_ffn1, v_w_gu1, v_w_down1, v_g_mix, v_w_in, v_conv_w, v_q_norm_g, v_k_norm_g, v_sinks, v_w_out_conv, v_w_out_attn, v_w_o, v_g_ffn2, v_w_gu2, v_w_down2):
    S, D = x.shape[1], x.shape[2]
    dh = q_norm_g.shape[1]
    HQ = sinks.shape[1]
    HKV = HQ // 4
    AW, KVW, CW = HQ * dh, HKV * dh, D // 2
    off_q, off_k, off_v = 3 * CW, 3 * CW + AW, 3 * CW + AW + KVW
    off_ga, off_gb = off_v + KVW, off_v + KVW + D
    x0, target = x[0], loss_target[0]
    cx, cy, cc = lax.axis_index("x"), lax.axis_index("y"), lax.axis_index("c")
    chip = 2 * cx + cy
    p_arr = jnp.reshape(chip, (1,)).astype(jnp.int32)
    c_arr = jnp.reshape(cc, (1,)).astype(jnp.int32)
    cp_arr = jnp.stack([cc, chip]).astype(jnp.int32)
    wts = dict(g_ffn1=g_ffn1, w_gu1=w_gu1, w_down1=w_down1, g_mix=g_mix, w_in=w_in, conv_w=conv_w, q_norm_g=q_norm_g,
               k_norm_g=k_norm_g, sinks=sinks, w_out_conv=w_out_conv, w_out_attn=w_out_attn, w_o=w_o, g_ffn2=g_ffn2,
               w_gu2=w_gu2, w_down2=w_down2)
    ms = dict(g_ffn1=m_g_ffn1, w_gu1=m_w_gu1, w_down1=m_w_down1, g_mix=m_g_mix, w_in=m_w_in, conv_w=m_conv_w,
              q_norm_g=m_q_norm_g, k_norm_g=m_k_norm_g, sinks=m_sinks, w_out_conv=m_w_out_conv, w_out_attn=m_w_out_attn,
              w_o=m_w_o, g_ffn2=m_g_ffn2, w_gu2=m_w_gu2, w_down2=m_w_down2)
    vs = dict(g_ffn1=v_g_ffn1, w_gu1=v_w_gu1, w_down1=v_w_down1, g_mix=v_g_mix, w_in=v_w_in, conv_w=v_conv_w,
              q_norm_g=v_q_norm_g, k_norm_g=v_k_norm_g, sinks=v_sinks, w_out_conv=v_w_out_conv, w_out_attn=v_w_out_attn,
              w_o=v_w_o, g_ffn2=v_g_ffn2, w_gu2=v_w_gu2, w_down2=v_w_down2)
    order = list(wts)
    small_names = [k for k in order if not k.startswith("w_")]
    grad, delta, new_m, new_v = {}, {}, {}, {}

    def cast(keys, dep=None):
        return [_cast_to_slot(f"cast_{k}", wts[k][0], F32 if k == "conv_w" else BF16, p_arr, dep) for k in keys]

    def gather_start(tag, slots, dep):
        send, recv, slots, tok = _gather_start(f"gather_start_{tag}", slots, dep)
        return (tag, send, recv, slots), tok

    def gather_finish(started, after):
        tag, send, recv, slots = started
        return _gather_forward(f"gather_forward_{tag}", _gather_wait(f"gather_wait_{tag}", send, recv, slots, after))

    swapping, pending = [], []

    def reduce_start(full):
        keys = list(full)
        send, recv, gs, lands, tok = _swap_start(f"swap_start_{keys[0]}", [full[k] for k in keys])
        if swapping:
            tok = reduce_advance(tok)
        swapping.append((keys, send, recv, gs, lands))
        return tok

    def reduce_advance(after):
        keys, send, recv, gs, lands = swapping.pop(0)
        gs, sib = _swap_wait(f"swap_wait_{keys[0]}", send, recv, gs, lands, after)
        parts = [_add_half(f"add_half_{k}", g, r, c_arr) for k, g, r in zip(keys, gs, sib)]
        send, recv, parts, lands, tok = _exchange_start(f"exchange_start_{keys[0]}", parts)
        pending.append((keys, send, recv, parts, lands))
        return tok

    def reduce_finish(entries, after):
        joining, last = None, after

        def update(joining, tok):
            keys, send, recv, halves = joining
            out = last
            for k, g2 in zip(keys, _join_wait(f"join_wait_{keys[0]}", send, recv, halves, tok)):
                g2, d, nm, nv = _adamw(f"adamw_{k}", wts[k][0], g2, ms[k][0], vs[k][0], (out,))
                grad[k], delta[k], new_m[k], new_v[k] = g2[None], d[None], nm[None], nv[None]
                out = nv
            return out

        for keys, send, recv, parts, lands in entries:
            parts, lands = _exchange_wait(f"exchange_wait_{keys[0]}", send, recv, parts, lands, after)
            halves = [_add_chips(f"add_chips_{k}", t, r, cp_arr) for k, t, r in zip(keys, parts, lands)]
            send, recv, halves, tok = _join_start(f"join_start_{keys[0]}", halves)
            if joining is not None:
                last = update(joining, tok)
            joining = (keys, send, recv, halves)
        return update(joining, last)

    st_gu1, tok = gather_start("gu1", cast(["w_gu1"]), x0)
    st_d1, tok = gather_start("d1", cast(["w_down1"]), tok)
    later = ["w_in", "conv_w", "w_out_conv", "w_out_attn", "w_o", "w_gu2", "w_down2"]
    slot = dict(zip(later, cast(later, tok)))
    h1 = _rms_fwd("rms_fwd_1", x0, g_ffn1, slot["w_down2"])
    wgu1, = gather_finish(st_gu1, h1)
    st_in, tok = gather_start("in", [slot["w_in"], slot["conv_w"]], wgu1)
    st_out, tok = gather_start("out", [slot["w_out_conv"], slot["w_out_attn"], slot["w_o"]], tok)
    st_gu2, tok = gather_start("gu2", [slot["w_gu2"]], tok)
    st_d2, tok = gather_start("d2", [slot["w_down2"]], tok)
    cos, sin, rm, rmt = _rope_consts(S, dh)
    sink_vec = sinks[0]

    gu1, act1 = _ffn_up("ffn_up_1", h1, wgu1)
    wd1 = gather_finish(st_d1, act1)[0].reshape(-1, D)
    x1 = _mm_res("ffn_down_1", act1, wd1, x0, 0.5)
    win3, convw3 = gather_finish(st_in, x1)
    h2 = _rms_fwd("rms_fwd_mix", x1, g_mix)
    proj = _mm_cols("in_proj", h2, win3, F32)
    aconv = _conv_fwd("conv_fwd", proj, convw3, CW)
    woc3, woa3, wo = gather_finish(st_out, aconv)
    wo = wo.reshape(-1, D)
    ya = _mm_cols("out_conv", aconv, woc3, F32)
    q_raw = _heads(proj[:, off_q:off_q + AW], HQ)
    k_raw = _heads(proj[:, off_k:off_k + KVW], HKV)
    vh = _heads(proj[:, off_v:off_v + KVW], HKV).astype(BF16)
    qn = _qk_prep("q_prep", q_raw, q_norm_g, cos, sin, rm)
    kn = _qk_prep("k_prep", k_raw, k_norm_g, cos, sin, rm)
    oh = _attn_fwd("attn_fwd", qn, kn, vh, sink_vec)
    o = _unheads(oh)
    yb = _mm_cols("out_attn", o, woa3, F32)
    merged = _gate_fwd("gate_fwd", proj, ya, yb, off_ga, off_gb)
    x2 = _mm_res("mix_out", merged, wo, x1, 1.0)
    wgu2, = gather_finish(st_gu2, x2)
    h3 = _rms_fwd("rms_fwd_2", x2, g_ffn2)
    gu2, act2 = _ffn_up("ffn_up_2", h3, wgu2)
    wd2 = gather_finish(st_d2, act2)[0].reshape(-1, D)
    x3 = _mm_res("ffn_down_2", act2, wd2, x2, 0.5)

    dy, dy_b, loss_lanes = _loss_grad("loss_grad", x3, target)
    dx2, dx2_b, dg_ffn2, tok = _ffn_bwd("2", x2, g_ffn2, wgu2, wd2, (h3, gu2, act2), dy, dy_b, reduce_start, None)
    dmerged = _mm_nt("mix_out_bwd", dx2_b, wo, F32)
    tok = reduce_start(dict(w_o=_mm_tn("dw_o", merged, dx2_b, dep=tok).reshape(N_CHIPS, -1, D)))
    dga, dgb, dya, dyb = _gate_bwd("gate_bwd", proj, ya, yb, dmerged, off_ga, off_gb)
    daconv = _mm_nt_cols("out_conv_bwd", dya, woc3, dep=tok)
    dwoc = _mm_tn_cols("dw_out_conv", aconv, dya, woc3.shape[2])
    do = _mm_nt_cols("out_attn_bwd", dyb, woa3)
    dwoa = _mm_tn_cols("dw_out_attn", o, dyb, woa3.shape[2])
    tok = reduce_start(dict(w_out_conv=dwoc, w_out_attn=dwoa))
    dxc, dbg, dcg, dconvw = _conv_bwd("conv_bwd", proj, convw3, daconv, CW)
    dqn, dkn, dvh, dsink3 = _attn_bwd("attn_bwd", qn, kn, vh, sink_vec, _heads(do, HQ).astype(BF16))
    dq_raw, dqg = _qk_prep_bwd("q_prep_bwd", q_raw, q_norm_g, cos, sin, rmt, dqn)
    dk_raw, dkg = _qk_prep_bwd("k_prep_bwd", k_raw, k_norm_g, cos, sin, rmt, dkn)
    dproj = jnp.concatenate([dxc, dbg, dcg, _unheads(dq_raw), _unheads(dk_raw), _unheads(dvh).astype(BF16), dga, dgb], axis=1)
    dh2 = _mm_nt_cols("in_proj_bwd", dproj, win3, dep=tok)
    tok = reduce_start(dict(w_in=_mm_tn_cols("dw_in", h2, dproj, win3.shape[2])))
    dx1, dx1_b, dg_mix = _rms_bwd("rms_bwd_mix", x1, g_mix, dh2, dx2)
    dx0, _, dg_ffn1, tok = _ffn_bwd("1", x0, g_ffn1, wgu1, wd1, (h1, gu1, act1), dx1, dx1_b, reduce_start, tok, reduce_advance)

    def rows8(a):
        a = a.reshape(-1, a.shape[-1])
        return jnp.pad(a, ((0, -a.shape[0] % 8), (0, D - a.shape[1])))

    misc = jnp.concatenate([dqg, dkg, dsink3[:, :, 0].reshape(1, HQ), loss_lanes], axis=1)
    done = reduce_finish(pending[:-2], dx0)
    tot = _allreduce_small("allreduce_small", jnp.concatenate([rows8(a) for a in (dg_ffn1, dg_mix, dg_ffn2, dconvw, misc)], axis=0), done)
    reduce_finish(pending[-2:], tot)

    cw_s = conv_w.shape[2]
    conv_row0, misc_row = 24, 24 + (-(-N_CHIPS * CONV_K // 8)) * 8
    small_g = dict(g_ffn1=tot[0:1], g_mix=tot[8:9], g_ffn2=tot[16:17],
                   conv_w=lax.dynamic_slice(tot, (conv_row0 + CONV_K * chip, 0), (CONV_K, cw_s)),
                   q_norm_g=tot[misc_row:misc_row + 1, 0:dh], k_norm_g=tot[misc_row:misc_row + 1, dh:2 * dh],
                   sinks=tot[misc_row:misc_row + 1, 2 * dh:2 * dh + HQ])
    loss = (0.5 / D) * jnp.sum(tot[misc_row, 2 * dh + HQ:2 * dh + HQ + LANES])

    def small_pack(src):
        return jnp.concatenate([rows8(src[k]) for k in small_names], axis=0)

    _, sd, sm, sv = _adamw("adamw_small", small_pack(wts), small_pack(small_g), small_pack(ms), small_pack(vs))
    for i, k in enumerate(small_names):
        shape = wts[k].shape
        nr, ncol = math.prod(shape[:-1]), shape[-1]
        grad[k] = small_g[k].reshape(shape)
        delta[k], new_m[k], new_v[k] = (a[8 * i:8 * i + nr, 0:ncol].reshape(shape) for a in (sd, sm, sv))
    return (loss, dx0[None], *[grad[k] for k in order], *[delta[k] for k in order],
            *[new_m[k] for k in order], *[new_v[k] for k in order])
```

```python
import math

import numpy as np
import jax
import jax.numpy as jnp
from jax import lax
from jax.experimental import pallas as pl
from jax.experimental.pallas import tpu as pltpu

F32 = jnp.float32
BF16 = jnp.bfloat16
MESH = pl.DeviceIdType.MESH

RMS_EPS = 1e-6
BLOCK = 128
ROPE_THETA = 500000.0
NEG_INF = -1e30
CONV_K = 3
ADAM_LR, ADAM_B1, ADAM_B2, ADAM_EPS, ADAM_WD, ADAM_STEP = 0.001, 0.9, 0.999, 1e-08, 0.01, 10

VMEM_LIMIT_V7X = 56 * 1024 * 1024
LANES = 128
N_CHIPS = 4
N_DEV = 8


def _tile(n, want, align=LANES):
    best = None
    t = align
    while t <= min(n, want):
        if n % t == 0:
            best = t
        t += align
    return best or n


def _cparams(sem):
    return pltpu.CompilerParams(dimension_semantics=sem, vmem_limit_bytes=VMEM_LIMIT_V7X)


def _sigmoid(x):
    return 1.0 / (1.0 + jnp.exp(-x))


NN = (((1,), (0,)), ((), ()))
NT = (((1,), (1,)), ((), ()))
TN = (((0,), (0,)), ((), ()))


def _mm(name, grid, ins, in_specs, compute, out_shape, out_specs, epilogue, dep=None):
    if dep is not None:
        ins, in_specs = tuple(ins) + (dep,), list(in_specs) + [pl.BlockSpec(dep.shape, lambda *_: (0, 0))]
    n_in = len(ins)

    def body(*refs):
        epilogue(compute(refs[:n_in]), refs[:n_in], refs[n_in:])

    return pl.pallas_call(
        body, name=name, grid=grid, in_specs=in_specs, out_specs=out_specs, out_shape=out_shape,
        compiler_params=_cparams(("parallel", "arbitrary")),
    )(*ins)


def _dot(dims, a=0, b=1):
    return lambda refs: [lax.dot_general(refs[a][...], refs[b][...], dims, preferred_element_type=F32)]


def _ffn_up(name, h, wgu3, dep=None):
    S, D = h.shape
    Ns = wgu3.shape[2]
    F = 2 * Ns
    tm, tn = _tile(S, 512), _tile(Ns, 1408)
    nbs = Ns // tn

    def compute(refs):
        hv = refs[0][...]
        return [jnp.dot(hv, refs[1][...], preferred_element_type=F32), jnp.dot(hv, refs[2][...], preferred_element_type=F32)]

    def epi(accs, in_refs, out_refs):
        g, u = accs
        dgu_ref, a_ref = out_refs
        sg = _sigmoid(g)
        silu = g * sg
        dgu_ref[0] = (u * (sg * (1.0 + g * (1.0 - sg)))).astype(BF16)
        dgu_ref[1] = silu.astype(BF16)
        a_ref[...] = (silu * u).astype(BF16)

    return _mm(
        name, (F // tn, S // tm), (h, wgu3, wgu3),
        [pl.BlockSpec((tm, D), lambda j, i: (i, 0)),
         pl.BlockSpec((None, D, tn), lambda j, i: (j // nbs, 0, j % nbs)),
         pl.BlockSpec((None, D, tn), lambda j, i: (2 + j // nbs, 0, j % nbs))],
        compute, (jax.ShapeDtypeStruct((2, S, F), BF16), jax.ShapeDtypeStruct((S, F), BF16)),
        (pl.BlockSpec((2, tm, tn), lambda j, i: (0, i, j)), pl.BlockSpec((tm, tn), lambda j, i: (i, j))), epi, dep=dep)


def _mm_res(name, a, w, res, scale):
    S, K = a.shape
    N = w.shape[1]
    tm, tn = _tile(S, 512), _tile(N, 512 if K > 2816 else 1024)

    def epi(accs, in_refs, out_refs):
        out_refs[0][...] = in_refs[2][...] + scale * accs[0]

    return _mm(
        name, (N // tn, S // tm), (a, w, res),
        [pl.BlockSpec((tm, K), lambda j, i: (i, 0)), pl.BlockSpec((K, tn), lambda j, i: (0, j)),
         pl.BlockSpec((tm, tn), lambda j, i: (i, j))],
        _dot(NN), jax.ShapeDtypeStruct((S, N), F32), pl.BlockSpec((tm, tn), lambda j, i: (i, j)), epi)


def _mm_cols(name, a, w3, out_dtype):
    S, K = a.shape
    Ns = w3.shape[2]
    tm, tn = _tile(S, 512), _tile(Ns, 2304)
    nbs = Ns // tn

    def epi(accs, in_refs, out_refs):
        out_refs[0][...] = accs[0].astype(out_dtype)

    return _mm(
        name, (N_CHIPS * nbs, S // tm), (a, w3),
        [pl.BlockSpec((tm, K), lambda j, i: (i, 0)),
         pl.BlockSpec((None, K, tn), lambda j, i: (j // nbs, 0, j % nbs))],
        _dot(NN), jax.ShapeDtypeStruct((S, N_CHIPS * Ns), out_dtype), pl.BlockSpec((tm, tn), lambda j, i: (i, j)), epi)


def _mm_nt(name, a, w, out_dtype, scale=1.0):
    S, N = a.shape
    K = w.shape[0]
    tm, tn = _tile(S, 512), _tile(K, 1024)

    def epi(accs, in_refs, out_refs):
        out_refs[0][...] = (scale * accs[0]).astype(out_dtype)

    return _mm(
        name, (K // tn, S // tm), (a, w),
        [pl.BlockSpec((tm, N), lambda j, i: (i, 0)), pl.BlockSpec((tn, N), lambda j, i: (j, 0))],
        _dot(NT), jax.ShapeDtypeStruct((S, K), out_dtype), pl.BlockSpec((tm, tn), lambda j, i: (i, j)), epi)


def _ffn_down_bwd(name, dy, wd, gu, scale, dep=None):
    S, D = dy.shape
    F = wd.shape[0]
    tm, tn = _tile(S, 512), _tile(F, 1408)

    def epi(accs, in_refs, out_refs):
        da = scale * accs[0]
        out_refs[0][0] = (da * in_refs[2][0].astype(F32)).astype(BF16)
        out_refs[0][1] = (da * in_refs[2][1].astype(F32)).astype(BF16)

    return _mm(
        name, (F // tn, S // tm), (dy, wd, gu),
        [pl.BlockSpec((tm, D), lambda j, i: (i, 0)), pl.BlockSpec((tn, D), lambda j, i: (j, 0)),
         pl.BlockSpec((2, tm, tn), lambda j, i: (0, i, j))],
        _dot(NT), jax.ShapeDtypeStruct((2, S, F), BF16), pl.BlockSpec((2, tm, tn), lambda j, i: (0, i, j)), epi, dep=dep)


def _mm_nt_cols(name, a, w3, a_is_gu=False, dep=None):
    K, Ns = w3.shape[1], w3.shape[2]
    S = a.shape[1] if a_is_gu else a.shape[0]
    tm = _tile(S, 512)
    tn = _tile(K, max(LANES, (6 << 20) // (N_CHIPS * Ns * 2)))
    if a_is_gu:
        a_spec = pl.BlockSpec((2, tm, 2 * Ns), lambda i, j: (0, i, 0))
        part = lambda a_ref, s: a_ref[s // 2, :, (s % 2) * Ns:(s % 2 + 1) * Ns]
    else:
        a_spec = pl.BlockSpec((tm, N_CHIPS * Ns), lambda i, j: (i, 0))
        part = lambda a_ref, s: a_ref[:, s * Ns:(s + 1) * Ns]

    def compute(refs):
        total = None
        for s in range(N_CHIPS):
            prod = lax.dot_general(part(refs[0], s), refs[1][s], NT, preferred_element_type=F32)
            total = prod if total is None else total + prod
        return [total]

    def epi(accs, in_refs, out_refs):
        out_refs[0][...] = accs[0]

    return _mm(
        name, (S // tm, K // tn), (a, w3), [a_spec, pl.BlockSpec((N_CHIPS, tn, Ns), lambda i, j: (0, j, 0))],
        compute, jax.ShapeDtypeStruct((S, K), F32), pl.BlockSpec((tm, tn), lambda i, j: (i, j)), epi, dep=dep)


def _mm_tn(name, a, b, scale=1.0, dep=None):
    S, K = a.shape
    N = b.shape[1]
    tm, tn = _tile(K, 512), _tile(N, 1024)

    def epi(accs, in_refs, out_refs):
        out_refs[0][...] = (scale * accs[0]).astype(BF16)

    return _mm(
        name, (N // tn, K // tm), (a, b),
        [pl.BlockSpec((S, tm), lambda j, i: (0, i)), pl.BlockSpec((S, tn), lambda j, i: (0, j))],
        _dot(TN), jax.ShapeDtypeStruct((K, N), BF16), pl.BlockSpec((tm, tn), lambda j, i: (i, j)), epi, dep=dep)


def _mm_tn_cols(name, a, b, Ns, b_is_gu=False, dep=None):
    S, K = a.shape
    tm, tn = _tile(K, 512), _tile(Ns, 2304)
    nbs = Ns // tn
    if b_is_gu:
        b_spec = pl.BlockSpec((None, S, tn), lambda j, i: (j // (2 * nbs), 0, j % (2 * nbs)))
    else:
        b_spec = pl.BlockSpec((S, tn), lambda j, i: (0, j))

    def epi(accs, in_refs, out_refs):
        out_refs[0][...] = accs[0].astype(BF16)

    return _mm(
        name, (N_CHIPS * nbs, K // tm), (a, b), [pl.BlockSpec((S, tm), lambda j, i: (0, i)), b_spec],
        _dot(TN), jax.ShapeDtypeStruct((N_CHIPS, K, Ns), BF16),
        pl.BlockSpec((None, tm, tn), lambda j, i: (j // nbs, i, j % nbs)), epi, dep=dep)


def _rms_fwd(name, x, gain, dep=None):
    S, D = x.shape
    tm = _tile(S, 256, 8)
    extra = () if dep is None else (dep,)

    def body(x_ref, g_ref, *rest):
        h_ref = rest[-1]
        xv = x_ref[...]
        r = lax.rsqrt(jnp.mean(xv * xv, axis=-1, keepdims=True) + RMS_EPS)
        h_ref[...] = (xv * r * g_ref[...]).astype(BF16)

    return pl.pallas_call(
        body, name=name, grid=(S // tm,),
        in_specs=[pl.BlockSpec((tm, D), lambda i: (i, 0)), pl.BlockSpec((1, D), lambda i: (0, 0))]
        + [pl.BlockSpec(memory_space=pl.ANY) for d in extra],
        out_specs=pl.BlockSpec((tm, D), lambda i: (i, 0)), out_shape=jax.ShapeDtypeStruct((S, D), BF16),
        compiler_params=_cparams(("parallel",)),
    )(x, gain, *extra)


def _rms_bwd(name, x, gain, dh, dres):
    S, D = x.shape
    tm = _tile(S, 256, 8)

    def body(x_ref, g_ref, dh_ref, dres_ref, dx_ref, dxb_ref, dg_ref):
        i = pl.program_id(0)
        xv = x_ref[...]
        r = lax.rsqrt(jnp.mean(xv * xv, axis=-1, keepdims=True) + RMS_EPS)
        xhat = xv * r
        dhv = dh_ref[...]
        dxhat = dhv * g_ref[...]
        dx = dres_ref[...] + r * (dxhat - xhat * jnp.mean(dxhat * xhat, axis=-1, keepdims=True))
        dx_ref[...] = dx
        dxb_ref[...] = dx.astype(BF16)

        @pl.when(i == 0)
        def _():
            dg_ref[...] = jnp.zeros_like(dg_ref)

        dg_ref[...] += jnp.sum(dhv * xhat, axis=0, keepdims=True)

    row = pl.BlockSpec((tm, D), lambda i: (i, 0))
    vec = pl.BlockSpec((1, D), lambda i: (0, 0))
    return pl.pallas_call(
        body, name=name, grid=(S // tm,), in_specs=[row, vec, row, row], out_specs=(row, row, vec),
        out_shape=(jax.ShapeDtypeStruct((S, D), F32), jax.ShapeDtypeStruct((S, D), BF16), jax.ShapeDtypeStruct((1, D), F32)),
        compiler_params=_cparams(("arbitrary",)),
    )(x, gain, dh, dres)


def _loss_grad(name, y, target):
    S, D = y.shape
    tm = _tile(S, 256, 8)

    def body(y_ref, t_ref, dy_ref, dyb_ref, l_ref):
        i = pl.program_id(0)
        e = y_ref[...] - t_ref[...]
        dy_ref[...] = e * (1.0 / D)
        dyb_ref[...] = (e * (1.0 / D)).astype(BF16)
        col = jnp.sum(e * e, axis=0, keepdims=True)
        part = col[:, 0:LANES]
        for k in range(1, D // LANES):
            part = part + col[:, k * LANES:(k + 1) * LANES]

        @pl.when(i == 0)
        def _():
            l_ref[...] = jnp.zeros_like(l_ref)

        l_ref[...] += part

    row = pl.BlockSpec((tm, D), lambda i: (i, 0))
    return pl.pallas_call(
        body, name=name, grid=(S // tm,), in_specs=[row, row],
        out_specs=(row, row, pl.BlockSpec((1, LANES), lambda i: (0, 0))),
        out_shape=(jax.ShapeDtypeStruct((S, D), F32), jax.ShapeDtypeStruct((S, D), BF16), jax.ShapeDtypeStruct((1, LANES), F32)),
        compiler_params=_cparams(("arbitrary",)),
    )(y, target)


def _shift_down(u, k):
    rows = lax.broadcasted_iota(jnp.int32, u.shape, 0)
    return jnp.where(rows >= k, pltpu.roll(u, k, 0), 0.0)


def _shift_up(u, k):
    n = u.shape[0]
    rows = lax.broadcasted_iota(jnp.int32, u.shape, 0)
    return jnp.where(rows < n - k, pltpu.roll(u, n - k, 0), 0.0)


def _conv_specs(S, cw, conv_width):
    nb = conv_width // cw
    col = lambda off: pl.BlockSpec((S, cw), lambda j, off=off: (0, off * nb + j))
    return nb, col(0), col(1), col(2)


def _conv_fwd(name, proj, convw3, conv_width):
    S = proj.shape[0]
    cw = convw3.shape[2]
    nb, xc_s, bg_s, cg_s = _conv_specs(S, cw, conv_width)

    def body(xc_ref, bg_ref, cg_ref, w_ref, o_ref):
        u = cg_ref[...] * xc_ref[...]
        w = w_ref[...]
        cv = w[2:3, :] * u + w[1:2, :] * _shift_down(u, 1) + w[0:1, :] * _shift_down(u, 2)
        o_ref[...] = (bg_ref[...] * cv).astype(BF16)

    return pl.pallas_call(
        body, name=name, grid=(nb,),
        in_specs=[xc_s, bg_s, cg_s, pl.BlockSpec((None, CONV_K, cw), lambda j: (j, 0, 0))],
        out_specs=pl.BlockSpec((S, cw), lambda j: (0, j)), out_shape=jax.ShapeDtypeStruct((S, conv_width), BF16),
        compiler_params=_cparams(("parallel",)),
    )(proj, proj, proj, convw3)


def _conv_bwd(name, proj, convw3, da, conv_width):
    S = proj.shape[0]
    cw = convw3.shape[2]
    nb, xc_s, bg_s, cg_s = _conv_specs(S, cw, conv_width)

    def body(xc_ref, bg_ref, cg_ref, w_ref, da_ref, dxc_ref, dbg_ref, dcg_ref, dw_ref):
        xc, cg = xc_ref[...], cg_ref[...]
        u = cg * xc
        w = w_ref[...]
        u1, u2 = _shift_down(u, 1), _shift_down(u, 2)
        cv = w[2:3, :] * u + w[1:2, :] * u1 + w[0:1, :] * u2
        dav = da_ref[...]
        dbg_ref[...] = (dav * cv).astype(BF16)
        dcv = dav * bg_ref[...]
        du = w[2:3, :] * dcv + w[1:2, :] * _shift_up(dcv, 1) + w[0:1, :] * _shift_up(dcv, 2)
        dxc_ref[...] = (du * cg).astype(BF16)
        dcg_ref[...] = (du * xc).astype(BF16)
        dw_ref[0:1, :] = jnp.sum(dcv * u2, axis=0, keepdims=True)
        dw_ref[1:2, :] = jnp.sum(dcv * u1, axis=0, keepdims=True)
        dw_ref[2:3, :] = jnp.sum(dcv * u, axis=0, keepdims=True)

    wspec = pl.BlockSpec((None, CONV_K, cw), lambda j: (j, 0, 0))
    ospec = pl.BlockSpec((S, cw), lambda j: (0, j))
    act = jax.ShapeDtypeStruct((S, conv_width), BF16)
    return pl.pallas_call(
        body, name=name, grid=(nb,), in_specs=[xc_s, bg_s, cg_s, wspec, ospec],
        out_specs=(ospec, ospec, ospec, wspec),
        out_shape=(act, act, act, jax.ShapeDtypeStruct(convw3.shape, F32)),
        compiler_params=_cparams(("parallel",)),
    )(proj, proj, proj, convw3, da)


def _rope_consts(S, dh):
    rot = dh // 4
    half = rot // 2
    inv_freq = 1.0 / (ROPE_THETA ** (jnp.arange(0, rot, 2, dtype=F32) / rot))
    ang = jnp.arange(S, dtype=F32)[:, None] * inv_freq[None, :]
    cos = jnp.concatenate([jnp.cos(ang), jnp.cos(ang), jnp.ones((S, dh - rot), F32)], axis=1)
    sin = jnp.concatenate([jnp.sin(ang), jnp.sin(ang), jnp.zeros((S, dh - rot), F32)], axis=1)
    rm = np.zeros((dh, dh), np.float32)
    for j in range(half):
        rm[j + half, j] = -1.0
        rm[j, j + half] = 1.0
    return cos, sin, jnp.asarray(rm, BF16), jnp.asarray(rm.T, BF16)


def _exact_perm(y, rm):
    hi = y.astype(BF16)
    r1 = y - hi.astype(F32)
    mid = r1.astype(BF16)
    lo = (r1 - mid.astype(F32)).astype(BF16)
    dot = lambda a: jnp.dot(a, rm, preferred_element_type=F32)
    return dot(hi) + dot(mid) + dot(lo)


def _qk_prep(name, xh, gain, cos, sin, rm):
    H, S, dh = xh.shape
    tm = _tile(S, 1024, 8)

    def body(x_ref, g_ref, c_ref, s_ref, rm_ref, o_ref):
        xv = x_ref[...]
        y = xv * lax.rsqrt(jnp.mean(xv * xv, axis=-1, keepdims=True) + RMS_EPS) * g_ref[...]
        o_ref[...] = (y * c_ref[...] + _exact_perm(y, rm_ref[...]) * s_ref[...]).astype(BF16)

    blk = pl.BlockSpec((None, tm, dh), lambda h, i: (h, i, 0))
    tab = pl.BlockSpec((tm, dh), lambda h, i: (i, 0))
    return pl.pallas_call(
        body, name=name, grid=(H, S // tm),
        in_specs=[blk, pl.BlockSpec((1, dh), lambda h, i: (0, 0)), tab, tab, pl.BlockSpec((dh, dh), lambda h, i: (0, 0))],
        out_specs=blk, out_shape=jax.ShapeDtypeStruct((H, S, dh), BF16),
        compiler_params=_cparams(("parallel", "parallel")),
    )(xh, gain, cos, sin, rm)


def _qk_prep_bwd(name, xh, gain, cos, sin, rmt, dout):
    H, S, dh = xh.shape
    tm = _tile(S, 1024, 8)

    def body(x_ref, g_ref, c_ref, s_ref, rmt_ref, do_ref, dx_ref, dg_ref):
        first = (pl.program_id(0) == 0) & (pl.program_id(1) == 0)
        xv = x_ref[...]
        r = lax.rsqrt(jnp.mean(xv * xv, axis=-1, keepdims=True) + RMS_EPS)
        xhat = xv * r
        dov = do_ref[...]
        dy = dov * c_ref[...] + _exact_perm(dov * s_ref[...], rmt_ref[...])
        dxhat = dy * g_ref[...]
        dx_ref[...] = (r * (dxhat - xhat * jnp.mean(dxhat * xhat, axis=-1, keepdims=True))).astype(BF16)

        @pl.when(first)
        def _():
            dg_ref[...] = jnp.zeros_like(dg_ref)

        dg_ref[...] += jnp.sum(dy * xhat, axis=0, keepdims=True)

    blk = pl.BlockSpec((None, tm, dh), lambda h, i: (h, i, 0))
    tab = pl.BlockSpec((tm, dh), lambda h, i: (i, 0))
    vec = pl.BlockSpec((1, dh), lambda h, i: (0, 0))
    return pl.pallas_call(
        body, name=name, grid=(H, S // tm),
        in_specs=[blk, vec, tab, tab, pl.BlockSpec((dh, dh), lambda h, i: (0, 0)), blk],
        out_specs=(blk, vec), out_shape=(jax.ShapeDtypeStruct((H, S, dh), BF16), jax.ShapeDtypeStruct((1, dh), F32)),
        compiler_params=_cparams(("arbitrary", "arbitrary")),
    )(xh, gain, cos, sin, rmt, dout)


def _attn_probs(q, kp, kc, sink_col, n, scale):
    rows = q.shape[0]
    sp = lax.dot_general(q, kp, NT, preferred_element_type=F32) * scale
    sc = lax.dot_general(q, kc, NT, preferred_element_type=F32) * scale
    qi = lax.broadcasted_iota(jnp.int32, (rows, BLOCK), 0) % BLOCK
    kj = lax.broadcasted_iota(jnp.int32, (rows, BLOCK), 1)
    sp = jnp.where((kj > qi) & (n > 0), sp, NEG_INF)
    sc = jnp.where(kj <= qi, sc, NEG_INF)
    m = jnp.maximum(jnp.maximum(jnp.max(sp, axis=-1, keepdims=True), jnp.max(sc, axis=-1, keepdims=True)), sink_col)
    pp, pc, ps = jnp.exp(sp - m), jnp.exp(sc - m), jnp.exp(sink_col - m)
    inv = 1.0 / (jnp.sum(pp, axis=-1, keepdims=True) + jnp.sum(pc, axis=-1, keepdims=True) + ps)
    return pp * inv, pc * inv, ps * inv


def _sink_col(sink_ref, hk, group):
    rows = group * BLOCK
    g = lax.broadcasted_iota(jnp.int32, (rows, 1), 0) // BLOCK
    col = jnp.zeros((rows, 1), F32)
    for i in range(group):
        col = jnp.where(g == i, sink_ref[hk * group + i], col)
    return col


def _attn_specs(group, S, dh):
    heads = pl.BlockSpec((group, S, dh), lambda hk: (hk, 0, 0))
    kv = pl.BlockSpec((None, S, dh), lambda hk: (hk, 0, 0))
    return heads, kv, pl.BlockSpec(memory_space=pltpu.SMEM)


def _block_rows(n):
    cur = pl.ds(pl.multiple_of(n * BLOCK, BLOCK), BLOCK)
    prev = pl.ds(pl.multiple_of(jnp.maximum(n - 1, 0) * BLOCK, BLOCK), BLOCK)
    return cur, prev


def _attn_fwd(name, q, k, v, sinks):
    HQ, S, dh = q.shape
    HKV = k.shape[0]
    group = HQ // HKV
    scale = dh ** -0.5
    heads, kv, smem = _attn_specs(group, S, dh)

    def body(q_ref, k_ref, v_ref, sink_ref, o_ref):
        sink = _sink_col(sink_ref, pl.program_id(0), group)

        def block(n, carry):
            cur, prev = _block_rows(n)
            qv = q_ref[:, cur, :].reshape(group * BLOCK, dh)
            pp, pc, _ = _attn_probs(qv, k_ref[prev, :], k_ref[cur, :], sink, n, scale)
            o = jnp.dot(pp.astype(BF16), v_ref[prev, :], preferred_element_type=F32)
            o = o + jnp.dot(pc.astype(BF16), v_ref[cur, :], preferred_element_type=F32)
            o_ref[:, cur, :] = o.reshape(group, BLOCK, dh).astype(BF16)
            return carry

        lax.fori_loop(0, S // BLOCK, block, 0)

    return pl.pallas_call(
        body, name=name, grid=(HKV,), in_specs=[heads, kv, kv, smem], out_specs=heads,
        out_shape=jax.ShapeDtypeStruct((HQ, S, dh), BF16), compiler_params=_cparams(("parallel",)),
    )(q, k, v, sinks)


def _attn_bwd(name, q, k, v, sinks, do):
    HQ, S, dh = q.shape
    HKV = k.shape[0]
    group = HQ // HKV
    scale = dh ** -0.5
    heads, kv, smem = _attn_specs(group, S, dh)
    sk = pl.BlockSpec((None, group, LANES), lambda hk: (hk, 0, 0))

    def body(q_ref, k_ref, v_ref, sink_ref, do_ref, dq_ref, dk_ref, dv_ref, ds_ref):
        rows = group * BLOCK
        sink = _sink_col(sink_ref, pl.program_id(0), group)
        dk_ref[...] = jnp.zeros_like(dk_ref)
        dv_ref[...] = jnp.zeros_like(dv_ref)
        tdot = lambda a, b: lax.dot_general(a, b, TN, preferred_element_type=F32)

        def block(n, dsink):
            cur, prev = _block_rows(n)
            qv = q_ref[:, cur, :].reshape(rows, dh)
            dov = do_ref[:, cur, :].reshape(rows, dh)
            kp, kc, vp, vc = k_ref[prev, :], k_ref[cur, :], v_ref[prev, :], v_ref[cur, :]
            pp, pc, ps = _attn_probs(qv, kp, kc, sink, n, scale)
            dpp = lax.dot_general(dov, vp, NT, preferred_element_type=F32)
            dpc = lax.dot_general(dov, vc, NT, preferred_element_type=F32)
            delta = jnp.sum(pp * dpp, axis=-1, keepdims=True) + jnp.sum(pc * dpc, axis=-1, keepdims=True)
            dsp = (pp * (dpp - delta) * scale).astype(BF16)
            dsc = (pc * (dpc - delta) * scale).astype(BF16)
            dq = jnp.dot(dsp, kp, preferred_element_type=F32) + jnp.dot(dsc, kc, preferred_element_type=F32)
            dq_ref[:, cur, :] = dq.reshape(group, BLOCK, dh)
            dk_ref[prev, :] += tdot(dsp, qv)
            dv_ref[prev, :] += tdot(pp.astype(BF16), dov)
            dk_ref[cur, :] += tdot(dsc, qv)
            dv_ref[cur, :] += tdot(pc.astype(BF16), dov)
            return dsink - jnp.sum((ps * delta).reshape(group, BLOCK, 1), axis=1)

        dsink = lax.fori_loop(0, S // BLOCK, block, jnp.zeros((group, 1), F32))
        ds_ref[...] = jnp.broadcast_to(dsink, (group, LANES))

    return pl.pallas_call(
        body, name=name, grid=(HKV,), in_specs=[heads, kv, kv, smem, heads], out_specs=(heads, kv, kv, sk),
        out_shape=(jax.ShapeDtypeStruct((HQ, S, dh), F32), jax.ShapeDtypeStruct((HKV, S, dh), F32),
                   jax.ShapeDtypeStruct((HKV, S, dh), F32), jax.ShapeDtypeStruct((HKV, group, LANES), F32)),
        compiler_params=_cparams(("parallel",)),
    )(q, k, v, sinks, do)


def _gate_specs(S, D, ga_off, gb_off):
    tg = LANES
    for t in range(LANES, 513, LANES):
        if D % t == 0 and ga_off % t == 0 and gb_off % t == 0:
            tg = t
    if D % LANES:
        tg = math.gcd(math.gcd(D, ga_off), gb_off)
    tm = _tile(S, 512, 8)
    act = pl.BlockSpec((tm, tg), lambda i, j: (i, j))
    ga = pl.BlockSpec((tm, tg), lambda i, j: (i, ga_off // tg + j))
    gb = pl.BlockSpec((tm, tg), lambda i, j: (i, gb_off // tg + j))
    return (S // tm, D // tg), act, ga, gb


def _gate_fwd(name, proj, ya, yb, ga_off, gb_off):
    S, D = ya.shape
    grid, act, ga, gb = _gate_specs(S, D, ga_off, gb_off)

    def body(ga_ref, gb_ref, ya_ref, yb_ref, o_ref):
        o_ref[...] = (_sigmoid(ga_ref[...]) * ya_ref[...] + _sigmoid(gb_ref[...]) * yb_ref[...]).astype(BF16)

    return pl.pallas_call(
        body, name=name, grid=grid, in_specs=[ga, gb, act, act], out_specs=act,
        out_shape=jax.ShapeDtypeStruct((S, D), BF16), compiler_params=_cparams(("parallel", "parallel")),
    )(proj, proj, ya, yb)


def _gate_bwd(name, proj, ya, yb, dm, ga_off, gb_off):
    S, D = ya.shape
    grid, act, ga, gb = _gate_specs(S, D, ga_off, gb_off)

    def body(ga_ref, gb_ref, ya_ref, yb_ref, dm_ref, dga_ref, dgb_ref, dya_ref, dyb_ref):
        dmv = dm_ref[...]
        sa, sb = _sigmoid(ga_ref[...]), _sigmoid(gb_ref[...])
        dga_ref[...] = (dmv * ya_ref[...] * sa * (1.0 - sa)).astype(BF16)
        dgb_ref[...] = (dmv * yb_ref[...] * sb * (1.0 - sb)).astype(BF16)
        dya_ref[...] = (dmv * sa).astype(BF16)
        dyb_ref[...] = (dmv * sb).astype(BF16)

    o = jax.ShapeDtypeStruct((S, D), BF16)
    return pl.pallas_call(
        body, name=name, grid=grid, in_specs=[ga, gb, act, act, act], out_specs=(act, act, act, act),
        out_shape=(o, o, o, o), compiler_params=_cparams(("parallel", "parallel")),
    )(proj, proj, ya, yb, dm)


ANY = pl.BlockSpec(memory_space=pl.ANY)


def _row_tile(rows, cols, n_arrays):
    want = max(16, (VMEM_LIMIT_V7X // 2) // (2 * n_arrays * cols * 4))
    return _tile(rows, want, 16)


def _cast_to_slot(name, w, dtype, p_arr, dep=None):
    R, C = w.shape
    tr = _row_tile(R, C, 2)
    extra = () if dep is None else (dep,)

    def body(p_ref, w_ref, *rest):
        rest[-1][...] = w_ref[...].astype(dtype)

    return pl.pallas_call(
        body, name=name,
        grid_spec=pltpu.PrefetchScalarGridSpec(
            num_scalar_prefetch=1, grid=(R // tr,),
            in_specs=[pl.BlockSpec((tr, C), lambda i, p_ref: (i, 0))] + [pl.BlockSpec(d.shape, lambda i, p_ref: (0, 0)) for d in extra],
            out_specs=pl.BlockSpec((None, tr, C), lambda i, p_ref: (p_ref[0], i, 0))),
        out_shape=jax.ShapeDtypeStruct((N_CHIPS, R, C), dtype), compiler_params=_cparams(("parallel",)),
    )(p_arr, w, *extra)


def _add_half(name, g3, r3, c_arr):
    n, h, C = r3.shape
    tr = _row_tile(h, C, 3)
    nb = h // tr

    def body(c_ref, g_ref, r_ref, o_ref):
        o_ref[...] = (g_ref[...].astype(F32) + r_ref[...].astype(F32)).astype(BF16)

    blk = pl.BlockSpec((None, tr, C), lambda s, i, c_ref: (s, i, 0))
    return pl.pallas_call(
        body, name=name,
        grid_spec=pltpu.PrefetchScalarGridSpec(
            num_scalar_prefetch=1, grid=(n, nb),
            in_specs=[pl.BlockSpec((None, tr, C), lambda s, i, c_ref: (s, c_ref[0] * nb + i, 0)), blk], out_specs=blk),
        out_shape=jax.ShapeDtypeStruct(r3.shape, BF16), compiler_params=_cparams(("parallel", "parallel")),
    )(c_arr, g3, r3)


def _add_chips(name, t3, r3, cp_arr):
    n, h, C = r3.shape
    tr = _row_tile(h, C, 6)
    nb = h // tr

    def body(cp_ref, t_ref, r0_ref, r1_ref, r2_ref, r3_ref, o_ref):
        p = cp_ref[1]
        total = None
        for a, r_ref in enumerate((r0_ref, r1_ref, r2_ref, r3_ref)):
            part = jnp.where(p == a, t_ref[...], r_ref[...]).astype(F32)
            total = part if total is None else total + part
        o_ref[...] = total

    def part(a):
        return pl.BlockSpec((None, tr, C), lambda i, cp_ref: (jnp.where(cp_ref[1] == a, (a + 1) % N_CHIPS, a), i, 0))

    return pl.pallas_call(
        body, name=name,
        grid_spec=pltpu.PrefetchScalarGridSpec(
            num_scalar_prefetch=1, grid=(nb,),
            in_specs=[pl.BlockSpec((None, tr, C), lambda i, cp_ref: (cp_ref[1], i, 0)), part(0), part(1), part(2), part(3)],
            out_specs=pl.BlockSpec((tr, C), lambda i, cp_ref: (cp_ref[0] * nb + i, 0))),
        out_shape=jax.ShapeDtypeStruct((2 * h, C), F32), compiler_params=_cparams(("parallel",)),
    )(cp_arr, t3, r3, r3, r3, r3)


def _adamw(name, w, g, m, v, deps=()):
    R, C = w.shape
    extra = tuple(deps)
    tr = _row_tile(R, C, 8)
    c1 = 1.0 - ADAM_B1 ** ADAM_STEP
    c2 = 1.0 - ADAM_B2 ** ADAM_STEP

    def body(w_ref, g_ref, m_ref, v_ref, *rest):
        go_ref, d_ref, nm_ref, nv_ref = rest[-4:]
        gv = g_ref[...]
        go_ref[...] = gv
        nm = ADAM_B1 * m_ref[...] + (1.0 - ADAM_B1) * gv
        nv = ADAM_B2 * v_ref[...] + (1.0 - ADAM_B2) * (gv * gv)
        d_ref[...] = -ADAM_LR * ((nm / c1) / (jnp.sqrt(nv / c2) + ADAM_EPS) + ADAM_WD * w_ref[...])
        nm_ref[...] = nm
        nv_ref[...] = nv

    blk = pl.BlockSpec((tr, C), lambda i: (i, 0))
    o = jax.ShapeDtypeStruct((R, C), F32)
    return pl.pallas_call(
        body, name=name, grid=(R // tr,), in_specs=[blk, blk, blk, blk] + [ANY] * len(extra), out_specs=(blk, blk, blk, blk),
        out_shape=(o, o, o, o), compiler_params=_cparams(("parallel",)),
    )(w, g, m, v, *extra)


def _place():
    x, y, c = lax.axis_index("x"), lax.axis_index("y"), lax.axis_index("c")
    chips = [(1 - x, y), (x, 1 - y), (1 - x, 1 - y)]
    return x, y, c, 2 * x + y, chips


HBM = pl.BlockSpec(memory_space=pltpu.HBM)
SEM = pl.BlockSpec(memory_space=pltpu.SEMAPHORE)
TOKEN = jax.ShapeDtypeStruct((8, LANES), F32)
DATAFLOW = pltpu.SideEffectType.DATAFLOW_SIDE_EFFECTING


def _hbm(a):
    return pltpu.with_memory_space_constraint(a, pltpu.HBM)


def _gather_blocks(bufs, i, c, p, chips):
    if bufs[i].shape[1] % 16:
        return bufs[i].at[p], [bufs[i].at[2 * cx + cy] for cx, cy in chips]
    h = bufs[i].shape[1] // 2
    rows = pl.ds(pl.multiple_of(c * h, 16), h)
    return bufs[i].at[p, rows], [bufs[i].at[2 * cx + cy, rows] for cx, cy in chips]


def _gather_start(name, slots, dep):
    n = len(slots)

    def body(*refs):
        bufs, send, recv, token = refs[:n], refs[n + 1], refs[n + 2], refs[-1]
        x, y, c, p, chips = _place()
        for i in range(n):
            mine, _ = _gather_blocks(bufs, i, c, p, chips)
            for j, chip in enumerate(chips):
                pltpu.make_async_remote_copy(src_ref=mine, dst_ref=mine, send_sem=send.at[3 * i + j], recv_sem=recv.at[3 * i + j],
                                             device_id=(*chip, c), device_id_type=MESH).start()
        token[...] = jnp.zeros_like(token)

    out = pl.pallas_call(
        body, name=name, in_specs=[HBM] * n + [ANY],
        out_specs=(SEM, SEM, *([HBM] * n), pl.BlockSpec(memory_space=pltpu.VMEM)),
        out_shape=(pltpu.SemaphoreType.DMA((3 * n,)), pltpu.SemaphoreType.DMA((3 * n,)),
                   *[pltpu.HBM(s.shape, s.dtype) for s in slots], TOKEN),
        input_output_aliases={i: 2 + i for i in range(n)},
        compiler_params=pltpu.CompilerParams(has_side_effects=DATAFLOW),
    )(*[_hbm(s) for s in slots], dep)
    return out[0], out[1], list(out[2:2 + n]), out[-1]


def _gather_wait(name, send, recv, slots, after):
    n = len(slots)

    def body(*refs):
        bufs, send, recv = refs[:n], refs[n], refs[n + 1]
        x, y, c, p, chips = _place()
        for i in range(n):
            mine, landed = _gather_blocks(bufs, i, c, p, chips)
            for j, chip in enumerate(chips):
                cp = pltpu.make_async_remote_copy(src_ref=mine, dst_ref=landed[j], send_sem=send.at[3 * i + j],
                                                  recv_sem=recv.at[3 * i + j], device_id=(*chip, c), device_id_type=MESH)
                cp.wait_send()
                cp.wait_recv()

    return list(pl.pallas_call(
        body, name=name, in_specs=[HBM] * n + [SEM, SEM, ANY], out_specs=tuple([HBM] * n),
        out_shape=tuple(pltpu.HBM(s.shape, s.dtype) for s in slots),
        input_output_aliases={i: i for i in range(n)},
        compiler_params=pltpu.CompilerParams(has_side_effects=DATAFLOW),
    )(*slots, send, recv, after))


def _gather_forward(name, slots):
    idx = [i for i, s in enumerate(slots) if s.shape[1] % 16 == 0]
    n = len(slots)

    def body(*refs):
        bufs = refs[n:2 * n]
        send, recv = refs[2 * n:]
        x, y, c, p, chips = _place()

        def rdma(k, ref):
            return pltpu.make_async_remote_copy(src_ref=ref, dst_ref=ref, send_sem=send.at[k], recv_sem=recv.at[k],
                                                device_id=(x, y, 1 - c), device_id_type=MESH)

        cps = []
        for k, i in enumerate(idx):
            for j, ref in enumerate(_gather_blocks(bufs, i, c, p, chips)[1]):
                cps.append(rdma(3 * k + j, ref))
                cps[-1].start()
        for k, i in enumerate(idx):
            for j, ref in enumerate(_gather_blocks(bufs, i, 1 - c, p, chips)[1]):
                rdma(3 * k + j, ref).wait_recv()
        for cp in cps:
            cp.wait_send()

    return list(pl.pallas_call(
        body, name=name, in_specs=[ANY] * n, out_specs=tuple([ANY] * n),
        out_shape=tuple(jax.ShapeDtypeStruct(s.shape, s.dtype) for s in slots),
        scratch_shapes=[pltpu.SemaphoreType.DMA((3 * len(idx),)), pltpu.SemaphoreType.DMA((3 * len(idx),))],
        input_output_aliases={i: i for i in range(n)},
        compiler_params=pltpu.CompilerParams(has_side_effects=True),
    )(*slots))


def _swap_copy(grads, lands, send, recv, i, x, y, c):
    h = grads[i].shape[1] // 2
    other = pl.ds(pl.multiple_of((1 - c) * h, 16), h)
    return pltpu.make_async_remote_copy(src_ref=grads[i].at[:, other, :], dst_ref=lands[i], send_sem=send.at[i],
                                        recv_sem=recv.at[i], device_id=(x, y, 1 - c), device_id_type=MESH)


def _swap_start(name, grads):
    n = len(grads)

    def body(*refs):
        ins, lands, send, recv, token = refs[:n], refs[n:2 * n], refs[2 * n], refs[2 * n + 1], refs[-1]
        x, y, c, p, chips = _place()
        for i in range(n):
            _swap_copy(ins, lands, send, recv, i, x, y, c).start()
        token[...] = jnp.zeros_like(token)

    gshapes = [pltpu.HBM(g.shape, g.dtype) for g in grads]
    halves = [(g.shape[0], g.shape[1] // 2, g.shape[2]) for g in grads]
    lshapes = [pltpu.HBM(s, g.dtype) for s, g in zip(halves, grads)]
    out = pl.pallas_call(
        body, name=name, in_specs=[HBM] * (2 * n),
        out_specs=(SEM, SEM, *([HBM] * (2 * n)), pl.BlockSpec(memory_space=pltpu.VMEM)),
        out_shape=(pltpu.SemaphoreType.DMA((n,)), pltpu.SemaphoreType.DMA((n,)), *gshapes, *lshapes, TOKEN),
        input_output_aliases={i: 2 + i for i in range(2 * n)},
        compiler_params=pltpu.CompilerParams(has_side_effects=DATAFLOW),
    )(*[_hbm(g) for g in grads], *[_hbm(lax.empty(s, g.dtype)) for s, g in zip(halves, grads)])
    return out[0], out[1], list(out[2:2 + n]), list(out[2 + n:2 + 2 * n]), out[-1]


def _swap_wait(name, send, recv, grads, lands, after):
    n = len(grads)

    def body(*refs):
        ins, lands, send, recv = refs[:n], refs[n:2 * n], refs[2 * n], refs[2 * n + 1]
        x, y, c, p, chips = _place()
        for i in range(n):
            cp = _swap_copy(ins, lands, send, recv, i, x, y, c)
            cp.wait_send()
            cp.wait_recv()

    shapes = [pltpu.HBM(t.shape, t.dtype) for t in list(grads) + list(lands)]
    out = pl.pallas_call(
        body, name=name, in_specs=[HBM] * (2 * n) + [SEM, SEM, ANY], out_specs=tuple([HBM] * (2 * n)),
        out_shape=tuple(shapes), input_output_aliases={i: i for i in range(2 * n)},
        compiler_params=pltpu.CompilerParams(has_side_effects=DATAFLOW),
    )(*grads, *lands, send, recv, after)
    return list(out[:n]), list(out[n:])


def _exchange_start(name, parts):
    n = len(parts)

    def body(*refs):
        ins, lands, send, recv, token = refs[:n], refs[n:2 * n], refs[2 * n], refs[2 * n + 1], refs[-1]
        x, y, c, p, chips = _place()
        for i in range(n):
            for j, (cx, cy) in enumerate(chips):
                pltpu.make_async_remote_copy(src_ref=ins[i].at[2 * cx + cy], dst_ref=lands[i].at[p], send_sem=send.at[3 * i + j],
                                             recv_sem=recv.at[3 * i + j], device_id=(cx, cy, c), device_id_type=MESH).start()
        token[...] = jnp.zeros_like(token)

    shapes = [pltpu.HBM(t.shape, t.dtype) for t in parts]
    out = pl.pallas_call(
        body, name=name, in_specs=[HBM] * (2 * n),
        out_specs=(SEM, SEM, *([HBM] * (2 * n)), pl.BlockSpec(memory_space=pltpu.VMEM)),
        out_shape=(pltpu.SemaphoreType.DMA((3 * n,)), pltpu.SemaphoreType.DMA((3 * n,)), *shapes, *shapes, TOKEN),
        input_output_aliases={i: 2 + i for i in range(2 * n)},
        compiler_params=pltpu.CompilerParams(has_side_effects=DATAFLOW),
    )(*[_hbm(t) for t in parts], *[_hbm(lax.empty(t.shape, t.dtype)) for t in parts])
    return out[0], out[1], list(out[2:2 + n]), list(out[2 + n:2 + 2 * n]), out[-1]


def _exchange_wait(name, send, recv, parts, lands, after):
    n = len(parts)

    def body(*refs):
        ins, lands, send, recv = refs[:n], refs[n:2 * n], refs[2 * n], refs[2 * n + 1]
        x, y, c, p, chips = _place()
        for i in range(n):
            for j, (cx, cy) in enumerate(chips):
                q = 2 * cx + cy
                cp = pltpu.make_async_remote_copy(src_ref=ins[i].at[q], dst_ref=lands[i].at[q], send_sem=send.at[3 * i + j],
                                                  recv_sem=recv.at[3 * i + j], device_id=(cx, cy, c), device_id_type=MESH)
                cp.wait_send()
                cp.wait_recv()

    shapes = [pltpu.HBM(t.shape, t.dtype) for t in parts]
    out = pl.pallas_call(
        body, name=name, in_specs=[HBM] * (2 * n) + [SEM, SEM, ANY], out_specs=tuple([HBM] * (2 * n)),
        out_shape=(*shapes, *shapes), input_output_aliases={i: i for i in range(2 * n)},
        compiler_params=pltpu.CompilerParams(has_side_effects=DATAFLOW),
    )(*parts, *lands, send, recv, after)
    return list(out[:n]), list(out[n:])


def _join_copy(bufs, send, recv, i, which, x, y, c):
    h = bufs[i].shape[0] // 2
    rows = bufs[i].at[pl.ds(pl.multiple_of(which * h, 8), h)]
    return pltpu.make_async_remote_copy(src_ref=rows, dst_ref=rows, send_sem=send.at[i], recv_sem=recv.at[i],
                                        device_id=(x, y, 1 - c), device_id_type=MESH)


def _join_start(name, bufs):
    n = len(bufs)

    def body(*refs):
        ins, send, recv, token = refs[:n], refs[n], refs[n + 1], refs[-1]
        x, y, c, p, chips = _place()
        for i in range(n):
            _join_copy(ins, send, recv, i, c, x, y, c).start()
        token[...] = jnp.zeros_like(token)

    out = pl.pallas_call(
        body, name=name, in_specs=[HBM] * n,
        out_specs=(SEM, SEM, *([HBM] * n), pl.BlockSpec(memory_space=pltpu.VMEM)),
        out_shape=(pltpu.SemaphoreType.DMA((n,)), pltpu.SemaphoreType.DMA((n,)), *[pltpu.HBM(t.shape, t.dtype) for t in bufs], TOKEN),
        input_output_aliases={i: 2 + i for i in range(n)},
        compiler_params=pltpu.CompilerParams(has_side_effects=DATAFLOW),
    )(*[_hbm(t) for t in bufs])
    return out[0], out[1], list(out[2:2 + n]), out[-1]


def _join_wait(name, send, recv, bufs, after):
    n = len(bufs)

    def body(*refs):
        ins, send, recv = refs[:n], refs[n], refs[n + 1]
        x, y, c, p, chips = _place()
        for i in range(n):
            _join_copy(ins, send, recv, i, c, x, y, c).wait_send()
            _join_copy(ins, send, recv, i, 1 - c, x, y, c).wait_recv()

    return list(pl.pallas_call(
        body, name=name, in_specs=[HBM] * n + [SEM, SEM, ANY], out_specs=tuple([HBM] * n),
        out_shape=tuple(pltpu.HBM(t.shape, t.dtype) for t in bufs), input_output_aliases={i: i for i in range(n)},
        compiler_params=pltpu.CompilerParams(has_side_effects=DATAFLOW),
    )(*bufs, send, recv, after))


def _allreduce_small(name, pack, dep):
    R, W = pack.shape

    def body(in_ref, dep_ref, out_ref, slots, send, recv):
        x, y, c = lax.axis_index("x"), lax.axis_index("y"), lax.axis_index("c")
        me = 4 * x + 2 * y + c
        slots[0] = in_ref[...]
        cps = []
        for k in range(1, N_DEV):
            peer = (x ^ (k >> 2), y ^ ((k >> 1) & 1), c ^ (k & 1))
            cp = pltpu.make_async_remote_copy(src_ref=in_ref, dst_ref=slots.at[k], send_sem=send.at[k - 1],
                                              recv_sem=recv.at[k - 1], device_id=peer, device_id_type=MESH)
            cp.start()
            cps.append(cp)
        for cp in cps:
            cp.wait()
        total = slots[me]
        for a in range(1, N_DEV):
            total = total + slots[jnp.bitwise_xor(a, me)]
        out_ref[...] = total

    vmem = pl.BlockSpec(memory_space=pltpu.VMEM)
    return pl.pallas_call(
        body, name=name, in_specs=[vmem, ANY], out_specs=vmem, out_shape=jax.ShapeDtypeStruct((R, W), F32),
        scratch_shapes=[pltpu.VMEM((N_DEV, R, W), F32), pltpu.SemaphoreType.DMA((N_DEV - 1,)), pltpu.SemaphoreType.DMA((N_DEV - 1,))],
        compiler_params=pltpu.CompilerParams(has_side_effects=True),
    )(pack, dep)


def _heads(a, n_heads):
    S = a.shape[0]
    return a.reshape(S, n_heads, a.shape[1] // n_heads).transpose(1, 0, 2)


def _unheads(a):
    H, S, dh = a.shape
    return a.transpose(1, 0, 2).reshape(S, H * dh)


def _ffn_bwd(tag, xin, gain, wgu3, wd, saved, dxout, dxo_b, reduce_start, dep, flush=None):
    h, gu, act = saved
    D = xin.shape[1]
    tok = reduce_start({f"w_down{tag}": _mm_tn(f"dw_down_{tag}", act, dxo_b, 0.5, dep=dep).reshape(N_CHIPS, -1, D)})
    dgu = _ffn_down_bwd(f"ffn_down_bwd_{tag}", dxo_b, wd, gu, 0.5, dep=tok)
    tok = reduce_start({f"w_gu{tag}": _mm_tn_cols(f"dw_gu_{tag}", h, dgu, wgu3.shape[2], b_is_gu=True)})
    if flush is not None:
        tok = flush(tok)
    dh = _mm_nt_cols(f"ffn_up_bwd_{tag}", dgu, wgu3, a_is_gu=True, dep=tok)
    dxin, dxin_b, dgain = _rms_bwd(f"rms_bwd_{tag}", xin, gain, dh, dxout)
    return dxin, dxin_b, dgain, tok


def kernel(x, g_ffn1, w_gu1, w_down1, g_mix, w_in, conv_w, q_norm_g, k_norm_g, sinks, w_out_conv, w_out_attn, w_o, g_ffn2, w_gu2, w_down2, loss_target, m_g_ffn1, m_w_gu1, m_w_down1, m_g_mix, m_w_in, m_conv_w, m_q_norm_g, m_k_norm_g, m_sinks, m_w_out_conv, m_w_out_attn, m_w_o, m_g_ffn2, m_w_gu2, m_w_down2, v_g_ffn1, v_w_gu1, v_w_down1, v_g_mix, v_w_in, v_conv_w, v_q_norm_g, v_k_norm_g, v_sinks, v_w_out_conv, v_w_out_attn, v_w_o, v_g_ffn2, v_w_gu2, v_w_down2):
    S, D = x.shape[1], x.shape[2]
    dh = q_norm_g.shape[1]
    HQ = sinks.shape[1]
    HKV = HQ // 4
    AW, KVW, CW = HQ * dh, HKV * dh, D // 2
    off_q, off_k, off_v = 3 * CW, 3 * CW + AW, 3 * CW + AW + KVW
    off_ga, off_gb = off_v + KVW, off_v + KVW + D
    x0, target = x[0], loss_target[0]
    cx, cy, cc = lax.axis_index("x"), lax.axis_index("y"), lax.axis_index("c")
    chip = 2 * cx + cy
    p_arr = jnp.reshape(chip, (1,)).astype(jnp.int32)
    c_arr = jnp.reshape(cc, (1,)).astype(jnp.int32)
    cp_arr = jnp.stack([cc, chip]).astype(jnp.int32)
    wts = dict(g_ffn1=g_ffn1, w_gu1=w_gu1, w_down1=w_down1, g_mix=g_mix, w_in=w_in, conv_w=conv_w, q_norm_g=q_norm_g,
               k_norm_g=k_norm_g, sinks=sinks, w_out_conv=w_out_conv, w_out_attn=w_out_attn, w_o=w_o, g_ffn2=g_ffn2,
               w_gu2=w_gu2, w_down2=w_down2)
    ms = dict(g_ffn1=m_g_ffn1, w_gu1=m_w_gu1, w_down1=m_w_down1, g_mix=m_g_mix, w_in=m_w_in, conv_w=m_conv_w,
              q_norm_g=m_q_norm_g, k_norm_g=m_k_norm_g, sinks=m_sinks, w_out_conv=m_w_out_conv, w_out_attn=m_w_out_attn,
              w_o=m_w_o, g_ffn2=m_g_ffn2, w_gu2=m_w_gu2, w_down2=m_w_down2)
    vs = dict(g_ffn1=v_g_ffn1, w_gu1=v_w_gu1, w_down1=v_w_down1, g_mix=v_g_mix, w_in=v_w_in, conv_w=v_conv_w,
              q_norm_g=v_q_norm_g, k_norm_g=v_k_norm_g, sinks=v_sinks, w_out_conv=v_w_out_conv, w_out_attn=v_w_out_attn,
              w_o=v_w_o, g_ffn2=v_g_ffn2, w_gu2=v_w_gu2, w_down2=v_w_down2)
    order = list(wts)
    small_names = [k for k in order if not k.startswith("w_")]
    grad, delta, new_m, new_v = {}, {}, {}, {}

    def cast(keys, dep=None):
        return [_cast_to_slot(f"cast_{k}", wts[k][0], F32 if k == "conv_w" else BF16, p_arr, dep) for k in keys]

    def gather_start(tag, slots, dep):
        send, recv, slots, tok = _gather_start(f"gather_start_{tag}", slots, dep)
        return (tag, send, recv, slots), tok

    def gather_finish(started, after):
        tag, send, recv, slots = started
        return _gather_forward(f"gather_forward_{tag}", _gather_wait(f"gather_wait_{tag}", send, recv, slots, after))

    swapping, pending = [], []

    def reduce_start(full):
        keys = list(full)
        send, recv, gs, lands, tok = _swap_start(f"swap_start_{keys[0]}", [full[k] for k in keys])
        if swapping:
            tok = reduce_advance(tok)
        swapping.append((keys, send, recv, gs, lands))
        return tok

    def reduce_advance(after):
        keys, send, recv, gs, lands = swapping.pop(0)
        gs, sib = _swap_wait(f"swap_wait_{keys[0]}", send, recv, gs, lands, after)
        parts = [_add_half(f"add_half_{k}", g, r, c_arr) for k, g, r in zip(keys, gs, sib)]
        send, recv, parts, lands, tok = _exchange_start(f"exchange_start_{keys[0]}", parts)
        pending.append((keys, send, recv, parts, lands))
        return tok

    def reduce_finish(entries, after):
        joining, last = None, after

        def update(joining, tok):
            keys, send, recv, halves = joining
            out = last
            for k, g2 in zip(keys, _join_wait(f"join_wait_{keys[0]}", send, recv, halves, tok)):
                g2, d, nm, nv = _adamw(f"adamw_{k}", wts[k][0], g2, ms[k][0], vs[k][0], (out,))
                grad[k], delta[k], new_m[k], new_v[k] = g2[None], d[None], nm[None], nv[None]
                out = nv
            return out

        for keys, send, recv, parts, lands in entries:
            parts, lands = _exchange_wait(f"exchange_wait_{keys[0]}", send, recv, parts, lands, after)
            halves = [_add_chips(f"add_chips_{k}", t, r, cp_arr) for k, t, r in zip(keys, parts, lands)]
            send, recv, halves, tok = _join_start(f"join_start_{keys[0]}", halves)
            if joining is not None:
                last = update(joining, tok)
            joining = (keys, send, recv, halves)
        return update(joining, last)

    st_gu1, tok = gather_start("gu1", cast(["w_gu1"]), x0)
    st_d1, tok = gather_start("d1", cast(["w_down1"]), tok)
    later = ["w_in", "conv_w", "w_out_conv", "w_out_attn", "w_o", "w_gu2", "w_down2"]
    slot = dict(zip(later, cast(later, tok)))
    h1 = _rms_fwd("rms_fwd_1", x0, g_ffn1, slot["w_down2"])
    wgu1, = gather_finish(st_gu1, h1)
    st_in, tok = gather_start("in", [slot["w_in"], slot["conv_w"]], wgu1)
    st_out, tok = gather_start("out", [slot["w_out_conv"], slot["w_out_attn"], slot["w_o"]], tok)
    st_gu2, tok = gather_start("gu2", [slot["w_gu2"]], tok)
    st_d2, tok = gather_start("d2", [slot["w_down2"]], tok)
    cos, sin, rm, rmt = _rope_consts(S, dh)
    sink_vec = sinks[0]

    gu1, act1 = _ffn_up("ffn_up_1", h1, wgu1, tok)
    wd1 = gather_finish(st_d1, act1)[0].reshape(-1, D)
    x1 = _mm_res("ffn_down_1", act1, wd1, x0, 0.5)
    win3, convw3 = gather_finish(st_in, x1)
    h2 = _rms_fwd("rms_fwd_mix", x1, g_mix)
    proj = _mm_cols("in_proj", h2, win3, F32)
    aconv = _conv_fwd("conv_fwd", proj, convw3, CW)
    woc3, woa3, wo = gather_finish(st_out, aconv)
    wo = wo.reshape(-1, D)
    ya = _mm_cols("out_conv", aconv, woc3, F32)
    q_raw = _heads(proj[:, off_q:off_q + AW], HQ)
    k_raw = _heads(proj[:, off_k:off_k + KVW], HKV)
    vh = _heads(proj[:, off_v:off_v + KVW], HKV).astype(BF16)
    qn = _qk_prep("q_prep", q_raw, q_norm_g, cos, sin, rm)
    kn = _qk_prep("k_prep", k_raw, k_norm_g, cos, sin, rm)
    oh = _attn_fwd("attn_fwd", qn, kn, vh, sink_vec)
    o = _unheads(oh)
    yb = _mm_cols("out_attn", o, woa3, F32)
    merged = _gate_fwd("gate_fwd", proj, ya, yb, off_ga, off_gb)
    x2 = _mm_res("mix_out", merged, wo, x1, 1.0)
    wgu2, = gather_finish(st_gu2, x2)
    h3 = _rms_fwd("rms_fwd_2", x2, g_ffn2)
    gu2, act2 = _ffn_up("ffn_up_2", h3, wgu2)
    wd2 = gather_finish(st_d2, act2)[0].reshape(-1, D)
    x3 = _mm_res("ffn_down_2", act2, wd2, x2, 0.5)

    dy, dy_b, loss_lanes = _loss_grad("loss_grad", x3, target)
    dx2, dx2_b, dg_ffn2, tok = _ffn_bwd("2", x2, g_ffn2, wgu2, wd2, (h3, gu2, act2), dy, dy_b, reduce_start, None)
    dmerged = _mm_nt("mix_out_bwd", dx2_b, wo, F32)
    tok = reduce_start(dict(w_o=_mm_tn("dw_o", merged, dx2_b, dep=tok).reshape(N_CHIPS, -1, D)))
    dga, dgb, dya, dyb = _gate_bwd("gate_bwd", proj, ya, yb, dmerged, off_ga, off_gb)
    daconv = _mm_nt_cols("out_conv_bwd", dya, woc3, dep=tok)
    dwoc = _mm_tn_cols("dw_out_conv", aconv, dya, woc3.shape[2])
    do = _mm_nt_cols("out_attn_bwd", dyb, woa3)
    dwoa = _mm_tn_cols("dw_out_attn", o, dyb, woa3.shape[2])
    tok = reduce_start(dict(w_out_conv=dwoc, w_out_attn=dwoa))
    dxc, dbg, dcg, dconvw = _conv_bwd("conv_bwd", proj, convw3, daconv, CW)
    dqn, dkn, dvh, dsink3 = _attn_bwd("attn_bwd", qn, kn, vh, sink_vec, _heads(do, HQ).astype(BF16))
    dq_raw, dqg = _qk_prep_bwd("q_prep_bwd", q_raw, q_norm_g, cos, sin, rmt, dqn)
    dk_raw, dkg = _qk_prep_bwd("k_prep_bwd", k_raw, k_norm_g, cos, sin, rmt, dkn)
    dproj = jnp.concatenate([dxc, dbg, dcg, _unheads(dq_raw), _unheads(dk_raw), _unheads(dvh).astype(BF16), dga, dgb], axis=1)
    dh2 = _mm_nt_cols("in_proj_bwd", dproj, win3, dep=tok)
    tok = reduce_start(dict(w_in=_mm_tn_cols("dw_in", h2, dproj, win3.shape[2])))
    dx1, dx1_b, dg_mix = _rms_bwd("rms_bwd_mix", x1, g_mix, dh2, dx2)
    dx0, _, dg_ffn1, tok = _ffn_bwd("1", x0, g_ffn1, wgu1, wd1, (h1, gu1, act1), dx1, dx1_b, reduce_start, tok, reduce_advance)

    def rows8(a):
        a = a.reshape(-1, a.shape[-1])
        return jnp.pad(a, ((0, -a.shape[0] % 8), (0, D - a.shape[1])))

    misc = jnp.concatenate([dqg, dkg, dsink3[:, :, 0].reshape(1, HQ), loss_lanes], axis=1)
    done = reduce_finish(pending[:-2], dx0)
    tot = _allreduce_small("allreduce_small", jnp.concatenate([rows8(a) for a in (dg_ffn1, dg_mix, dg_ffn2, dconvw, misc)], axis=0), done)
    reduce_finish(pending[-2:], tot)

    cw_s = conv_w.shape[2]
    conv_row0, misc_row = 24, 24 + (-(-N_CHIPS * CONV_K // 8)) * 8
    small_g = dict(g_ffn1=tot[0:1], g_mix=tot[8:9], g_ffn2=tot[16:17],
                   conv_w=lax.dynamic_slice(tot, (conv_row0 + CONV_K * chip, 0), (CONV_K, cw_s)),
                   q_norm_g=tot[misc_row:misc_row + 1, 0:dh], k_norm_g=tot[misc_row:misc_row + 1, dh:2 * dh],
                   sinks=tot[misc_row:misc_row + 1, 2 * dh:2 * dh + HQ])
    loss = (0.5 / D) * jnp.sum(tot[misc_row, 2 * dh + HQ:2 * dh + HQ + LANES])

    def small_pack(src):
        return jnp.concatenate([rows8(src[k]) for k in small_names], axis=0)

    _, sd, sm, sv = _adamw("adamw_small", small_pack(wts), small_pack(small_g), small_pack(ms), small_pack(vs))
    for i, k in enumerate(small_names):
        shape = wts[k].shape
        nr, ncol = math.prod(shape[:-1]), shape[-1]
        grad[k] = small_g[k].reshape(shape)
        delta[k], new_m[k], new_v[k] = (a[8 * i:8 * i + nr, 0:ncol].reshape(shape) for a in (sd, sm, sv))
    return (loss, dx0[None], *[grad[k] for k in order], *[delta[k] for k in order],
            *[new_m[k] for k in order], *[new_v[k] for k in order])
```

```python
import math

import numpy as np
import jax
import jax.numpy as jnp
from jax import lax
from jax.experimental import pallas as pl
from jax.experimental.pallas import tpu as pltpu

F32 = jnp.float32
BF16 = jnp.bfloat16
MESH = pl.DeviceIdType.MESH

RMS_EPS = 1e-6
BLOCK = 128
ROPE_THETA = 500000.0
NEG_INF = -1e30
CONV_K = 3
ADAM_LR, ADAM_B1, ADAM_B2, ADAM_EPS, ADAM_WD, ADAM_STEP = 0.001, 0.9, 0.999, 1e-08, 0.01, 10

VMEM_LIMIT_V7X = 56 * 1024 * 1024
LANES = 128
N_CHIPS = 4
N_DEV = 8


def _tile(n, want, align=LANES):
    best = None
    t = align
    while t <= min(n, want):
        if n % t == 0:
            best = t
        t += align
    return best or n


def _cparams(sem):
    return pltpu.CompilerParams(dimension_semantics=sem, vmem_limit_bytes=VMEM_LIMIT_V7X)


def _sigmoid(x):
    return 1.0 / (1.0 + jnp.exp(-x))


NN = (((1,), (0,)), ((), ()))
NT = (((1,), (1,)), ((), ()))
TN = (((0,), (0,)), ((), ()))


def _mm(name, grid, ins, in_specs, compute, out_shape, out_specs, epilogue, dep=None):
    if dep is not None:
        ins, in_specs = tuple(ins) + (dep,), list(in_specs) + [pl.BlockSpec(dep.shape, lambda *_: (0, 0))]
    n_in = len(ins)

    def body(*refs):
        epilogue(compute(refs[:n_in]), refs[:n_in], refs[n_in:])

    return pl.pallas_call(
        body, name=name, grid=grid, in_specs=in_specs, out_specs=out_specs, out_shape=out_shape,
        compiler_params=_cparams(("parallel", "arbitrary")),
    )(*ins)


def _dot(dims, a=0, b=1):
    return lambda refs: [lax.dot_general(refs[a][...], refs[b][...], dims, preferred_element_type=F32)]


def _ffn_up(name, h, wgu3, dep=None):
    S, D = h.shape
    Ns = wgu3.shape[2]
    F = 2 * Ns
    tm, tn = _tile(S, 512), _tile(Ns, 1408)
    nbs = Ns // tn

    def compute(refs):
        hv = refs[0][...]
        return [jnp.dot(hv, refs[1][...], preferred_element_type=F32), jnp.dot(hv, refs[2][...], preferred_element_type=F32)]

    def epi(accs, in_refs, out_refs):
        g, u = accs
        dgu_ref, a_ref = out_refs
        sg = _sigmoid(g)
        silu = g * sg
        dgu_ref[0] = (u * (sg * (1.0 + g * (1.0 - sg)))).astype(BF16)
        dgu_ref[1] = silu.astype(BF16)
        a_ref[...] = (silu * u).astype(BF16)

    return _mm(
        name, (F // tn, S // tm), (h, wgu3, wgu3),
        [pl.BlockSpec((tm, D), lambda j, i: (i, 0)),
         pl.BlockSpec((None, D, tn), lambda j, i: (j // nbs, 0, j % nbs)),
         pl.BlockSpec((None, D, tn), lambda j, i: (2 + j // nbs, 0, j % nbs))],
        compute, (jax.ShapeDtypeStruct((2, S, F), BF16), jax.ShapeDtypeStruct((S, F), BF16)),
        (pl.BlockSpec((2, tm, tn), lambda j, i: (0, i, j)), pl.BlockSpec((tm, tn), lambda j, i: (i, j))), epi, dep=dep)


def _mm_res(name, a, w, res, scale):
    S, K = a.shape
    N = w.shape[1]
    tm, tn = _tile(S, 512), _tile(N, 512 if K > 2816 else 1024)

    def epi(accs, in_refs, out_refs):
        out_refs[0][...] = in_refs[2][...] + scale * accs[0]

    return _mm(
        name, (N // tn, S // tm), (a, w, res),
        [pl.BlockSpec((tm, K), lambda j, i: (i, 0)), pl.BlockSpec((K, tn), lambda j, i: (0, j)),
         pl.BlockSpec((tm, tn), lambda j, i: (i, j))],
        _dot(NN), jax.ShapeDtypeStruct((S, N), F32), pl.BlockSpec((tm, tn), lambda j, i: (i, j)), epi)


def _mm_cols(name, a, w3, out_dtype):
    S, K = a.shape
    Ns = w3.shape[2]
    tm, tn = _tile(S, 512), _tile(Ns, 2304)
    nbs = Ns // tn

    def epi(accs, in_refs, out_refs):
        out_refs[0][...] = accs[0].astype(out_dtype)

    return _mm(
        name, (N_CHIPS * nbs, S // tm), (a, w3),
        [pl.BlockSpec((tm, K), lambda j, i: (i, 0)),
         pl.BlockSpec((None, K, tn), lambda j, i: (j // nbs, 0, j % nbs))],
        _dot(NN), jax.ShapeDtypeStruct((S, N_CHIPS * Ns), out_dtype), pl.BlockSpec((tm, tn), lambda j, i: (i, j)), epi)


def _mm_nt(name, a, w, out_dtype, scale=1.0):
    S, N = a.shape
    K = w.shape[0]
    tm, tn = _tile(S, 512), _tile(K, 1024)

    def epi(accs, in_refs, out_refs):
        out_refs[0][...] = (scale * accs[0]).astype(out_dtype)

    return _mm(
        name, (K // tn, S // tm), (a, w),
        [pl.BlockSpec((tm, N), lambda j, i: (i, 0)), pl.BlockSpec((tn, N), lambda j, i: (j, 0))],
        _dot(NT), jax.ShapeDtypeStruct((S, K), out_dtype), pl.BlockSpec((tm, tn), lambda j, i: (i, j)), epi)


def _ffn_down_bwd(name, dy, wd, gu, scale, dep=None):
    S, D = dy.shape
    F = wd.shape[0]
    tm, tn = _tile(S, 512), _tile(F, 1408)

    def epi(accs, in_refs, out_refs):
        da = scale * accs[0]
        out_refs[0][0] = (da * in_refs[2][0].astype(F32)).astype(BF16)
        out_refs[0][1] = (da * in_refs[2][1].astype(F32)).astype(BF16)

    return _mm(
        name, (F // tn, S // tm), (dy, wd, gu),
        [pl.BlockSpec((tm, D), lambda j, i: (i, 0)), pl.BlockSpec((tn, D), lambda j, i: (j, 0)),
         pl.BlockSpec((2, tm, tn), lambda j, i: (0, i, j))],
        _dot(NT), jax.ShapeDtypeStruct((2, S, F), BF16), pl.BlockSpec((2, tm, tn), lambda j, i: (0, i, j)), epi, dep=dep)


def _mm_nt_cols(name, a, w3, a_is_gu=False, dep=None):
    K, Ns = w3.shape[1], w3.shape[2]
    S = a.shape[1] if a_is_gu else a.shape[0]
    tm = _tile(S, 512)
    tn = _tile(K, max(LANES, (6 << 20) // (N_CHIPS * Ns * 2)))
    if a_is_gu:
        a_spec = pl.BlockSpec((2, tm, 2 * Ns), lambda i, j: (0, i, 0))
        part = lambda a_ref, s: a_ref[s // 2, :, (s % 2) * Ns:(s % 2 + 1) * Ns]
    else:
        a_spec = pl.BlockSpec((tm, N_CHIPS * Ns), lambda i, j: (i, 0))
        part = lambda a_ref, s: a_ref[:, s * Ns:(s + 1) * Ns]

    def compute(refs):
        total = None
        for s in range(N_CHIPS):
            prod = lax.dot_general(part(refs[0], s), refs[1][s], NT, preferred_element_type=F32)
            total = prod if total is None else total + prod
        return [total]

    def epi(accs, in_refs, out_refs):
        out_refs[0][...] = accs[0]

    return _mm(
        name, (S // tm, K // tn), (a, w3), [a_spec, pl.BlockSpec((N_CHIPS, tn, Ns), lambda i, j: (0, j, 0))],
        compute, jax.ShapeDtypeStruct((S, K), F32), pl.BlockSpec((tm, tn), lambda i, j: (i, j)), epi, dep=dep)


def _mm_tn(name, a, b, scale=1.0, dep=None):
    S, K = a.shape
    N = b.shape[1]
    tm, tn = _tile(K, 512), _tile(N, 1024)

    def epi(accs, in_refs, out_refs):
        out_refs[0][...] = (scale * accs[0]).astype(BF16)

    return _mm(
        name, (N // tn, K // tm), (a, b),
        [pl.BlockSpec((S, tm), lambda j, i: (0, i)), pl.BlockSpec((S, tn), lambda j, i: (0, j))],
        _dot(TN), jax.ShapeDtypeStruct((K, N), BF16), pl.BlockSpec((tm, tn), lambda j, i: (i, j)), epi, dep=dep)


def _mm_tn_cols(name, a, b, Ns, b_is_gu=False, dep=None):
    S, K = a.shape
    tm, tn = _tile(K, 512), _tile(Ns, 2304)
    nbs = Ns // tn
    if b_is_gu:
        b_spec = pl.BlockSpec((None, S, tn), lambda j, i: (j // (2 * nbs), 0, j % (2 * nbs)))
    else:
        b_spec = pl.BlockSpec((S, tn), lambda j, i: (0, j))

    def epi(accs, in_refs, out_refs):
        out_refs[0][...] = accs[0].astype(BF16)

    return _mm(
        name, (N_CHIPS * nbs, K // tm), (a, b), [pl.BlockSpec((S, tm), lambda j, i: (0, i)), b_spec],
        _dot(TN), jax.ShapeDtypeStruct((N_CHIPS, K, Ns), BF16),
        pl.BlockSpec((None, tm, tn), lambda j, i: (j // nbs, i, j % nbs)), epi, dep=dep)


def _rms_fwd(name, x, gain, dep=None):
    S, D = x.shape
    tm = _tile(S, 256, 8)
    extra = () if dep is None else (dep,)

    def body(x_ref, g_ref, *rest):
        h_ref = rest[-1]
        xv = x_ref[...]
        r = lax.rsqrt(jnp.mean(xv * xv, axis=-1, keepdims=True) + RMS_EPS)
        h_ref[...] = (xv * r * g_ref[...]).astype(BF16)

    return pl.pallas_call(
        body, name=name, grid=(S // tm,),
        in_specs=[pl.BlockSpec((tm, D), lambda i: (i, 0)), pl.BlockSpec((1, D), lambda i: (0, 0))]
        + [pl.BlockSpec(memory_space=pl.ANY) for d in extra],
        out_specs=pl.BlockSpec((tm, D), lambda i: (i, 0)), out_shape=jax.ShapeDtypeStruct((S, D), BF16),
        compiler_params=_cparams(("parallel",)),
    )(x, gain, *extra)


def _rms_bwd(name, x, gain, dh, dres):
    S, D = x.shape
    tm = _tile(S, 256, 8)

    def body(x_ref, g_ref, dh_ref, dres_ref, dx_ref, dxb_ref, dg_ref):
        i = pl.program_id(0)
        xv = x_ref[...]
        r = lax.rsqrt(jnp.mean(xv * xv, axis=-1, keepdims=True) + RMS_EPS)
        xhat = xv * r
        dhv = dh_ref[...]
        dxhat = dhv * g_ref[...]
        dx = dres_ref[...] + r * (dxhat - xhat * jnp.mean(dxhat * xhat, axis=-1, keepdims=True))
        dx_ref[...] = dx
        dxb_ref[...] = dx.astype(BF16)

        @pl.when(i == 0)
        def _():
            dg_ref[...] = jnp.zeros_like(dg_ref)

        dg_ref[...] += jnp.sum(dhv * xhat, axis=0, keepdims=True)

    row = pl.BlockSpec((tm, D), lambda i: (i, 0))
    vec = pl.BlockSpec((1, D), lambda i: (0, 0))
    return pl.pallas_call(
        body, name=name, grid=(S // tm,), in_specs=[row, vec, row, row], out_specs=(row, row, vec),
        out_shape=(jax.ShapeDtypeStruct((S, D), F32), jax.ShapeDtypeStruct((S, D), BF16), jax.ShapeDtypeStruct((1, D), F32)),
        compiler_params=_cparams(("arbitrary",)),
    )(x, gain, dh, dres)


def _loss_grad(name, y, target):
    S, D = y.shape
    tm = _tile(S, 256, 8)

    def body(y_ref, t_ref, dy_ref, dyb_ref, l_ref):
        i = pl.program_id(0)
        e = y_ref[...] - t_ref[...]
        dy_ref[...] = e * (1.0 / D)
        dyb_ref[...] = (e * (1.0 / D)).astype(BF16)
        col = jnp.sum(e * e, axis=0, keepdims=True)
        part = col[:, 0:LANES]
        for k in range(1, D // LANES):
            part = part + col[:, k * LANES:(k + 1) * LANES]

        @pl.when(i == 0)
        def _():
            l_ref[...] = jnp.zeros_like(l_ref)

        l_ref[...] += part

    row = pl.BlockSpec((tm, D), lambda i: (i, 0))
    return pl.pallas_call(
        body, name=name, grid=(S // tm,), in_specs=[row, row],
        out_specs=(row, row, pl.BlockSpec((1, LANES), lambda i: (0, 0))),
        out_shape=(jax.ShapeDtypeStruct((S, D), F32), jax.ShapeDtypeStruct((S, D), BF16), jax.ShapeDtypeStruct((1, LANES), F32)),
        compiler_params=_cparams(("arbitrary",)),
    )(y, target)


def _shift_down(u, k):
    rows = lax.broadcasted_iota(jnp.int32, u.shape, 0)
    return jnp.where(rows >= k, pltpu.roll(u, k, 0), 0.0)


def _shift_up(u, k):
    n = u.shape[0]
    rows = lax.broadcasted_iota(jnp.int32, u.shape, 0)
    return jnp.where(rows < n - k, pltpu.roll(u, n - k, 0), 0.0)


def _conv_specs(S, cw, conv_width):
    nb = conv_width // cw
    col = lambda off: pl.BlockSpec((S, cw), lambda j, off=off: (0, off * nb + j))
    return nb, col(0), col(1), col(2)


def _conv_fwd(name, proj, convw3, conv_width):
    S = proj.shape[0]
    cw = convw3.shape[2]
    nb, xc_s, bg_s, cg_s = _conv_specs(S, cw, conv_width)

    def body(xc_ref, bg_ref, cg_ref, w_ref, o_ref):
        u = cg_ref[...] * xc_ref[...]
        w = w_ref[...]
        cv = w[2:3, :] * u + w[1:2, :] * _shift_down(u, 1) + w[0:1, :] * _shift_down(u, 2)
        o_ref[...] = (bg_ref[...] * cv).astype(BF16)

    return pl.pallas_call(
        body, name=name, grid=(nb,),
        in_specs=[xc_s, bg_s, cg_s, pl.BlockSpec((None, CONV_K, cw), lambda j: (j, 0, 0))],
        out_specs=pl.BlockSpec((S, cw), lambda j: (0, j)), out_shape=jax.ShapeDtypeStruct((S, conv_width), BF16),
        compiler_params=_cparams(("parallel",)),
    )(proj, proj, proj, convw3)


def _conv_bwd(name, proj, convw3, da, conv_width):
    S = proj.shape[0]
    cw = convw3.shape[2]
    nb, xc_s, bg_s, cg_s = _conv_specs(S, cw, conv_width)

    def body(xc_ref, bg_ref, cg_ref, w_ref, da_ref, dxc_ref, dbg_ref, dcg_ref, dw_ref):
        xc, cg = xc_ref[...], cg_ref[...]
        u = cg * xc
        w = w_ref[...]
        u1, u2 = _shift_down(u, 1), _shift_down(u, 2)
        cv = w[2:3, :] * u + w[1:2, :] * u1 + w[0:1, :] * u2
        dav = da_ref[...]
        dbg_ref[...] = (dav * cv).astype(BF16)
        dcv = dav * bg_ref[...]
        du = w[2:3, :] * dcv + w[1:2, :] * _shift_up(dcv, 1) + w[0:1, :] * _shift_up(dcv, 2)
        dxc_ref[...] = (du * cg).astype(BF16)
        dcg_ref[...] = (du * xc).astype(BF16)
        dw_ref[0:1, :] = jnp.sum(dcv * u2, axis=0, keepdims=True)
        dw_ref[1:2, :] = jnp.sum(dcv * u1, axis=0, keepdims=True)
        dw_ref[2:3, :] = jnp.sum(dcv * u, axis=0, keepdims=True)

    wspec = pl.BlockSpec((None, CONV_K, cw), lambda j: (j, 0, 0))
    ospec = pl.BlockSpec((S, cw), lambda j: (0, j))
    act = jax.ShapeDtypeStruct((S, conv_width), BF16)
    return pl.pallas_call(
        body, name=name, grid=(nb,), in_specs=[xc_s, bg_s, cg_s, wspec, ospec],
        out_specs=(ospec, ospec, ospec, wspec),
        out_shape=(act, act, act, jax.ShapeDtypeStruct(convw3.shape, F32)),
        compiler_params=_cparams(("parallel",)),
    )(proj, proj, proj, convw3, da)


def _rope_consts(S, dh):
    rot = dh // 4
    half = rot // 2
    inv_freq = 1.0 / (ROPE_THETA ** (jnp.arange(0, rot, 2, dtype=F32) / rot))
    ang = jnp.arange(S, dtype=F32)[:, None] * inv_freq[None, :]
    cos = jnp.concatenate([jnp.cos(ang), jnp.cos(ang), jnp.ones((S, dh - rot), F32)], axis=1)
    sin = jnp.concatenate([jnp.sin(ang), jnp.sin(ang), jnp.zeros((S, dh - rot), F32)], axis=1)
    rm = np.zeros((dh, dh), np.float32)
    for j in range(half):
        rm[j + half, j] = -1.0
        rm[j, j + half] = 1.0
    return cos, sin, jnp.asarray(rm, BF16), jnp.asarray(rm.T, BF16)


def _exact_perm(y, rm):
    hi = y.astype(BF16)
    r1 = y - hi.astype(F32)
    mid = r1.astype(BF16)
    lo = (r1 - mid.astype(F32)).astype(BF16)
    dot = lambda a: jnp.dot(a, rm, preferred_element_type=F32)
    return dot(hi) + dot(mid) + dot(lo)


def _qk_prep(name, xh, gain, cos, sin, rm):
    H, S, dh = xh.shape
    tm = _tile(S, 1024, 8)

    def body(x_ref, g_ref, c_ref, s_ref, rm_ref, o_ref):
        xv = x_ref[...]
        y = xv * lax.rsqrt(jnp.mean(xv * xv, axis=-1, keepdims=True) + RMS_EPS) * g_ref[...]
        o_ref[...] = (y * c_ref[...] + _exact_perm(y, rm_ref[...]) * s_ref[...]).astype(BF16)

    blk = pl.BlockSpec((None, tm, dh), lambda h, i: (h, i, 0))
    tab = pl.BlockSpec((tm, dh), lambda h, i: (i, 0))
    return pl.pallas_call(
        body, name=name, grid=(H, S // tm),
        in_specs=[blk, pl.BlockSpec((1, dh), lambda h, i: (0, 0)), tab, tab, pl.BlockSpec((dh, dh), lambda h, i: (0, 0))],
        out_specs=blk, out_shape=jax.ShapeDtypeStruct((H, S, dh), BF16),
        compiler_params=_cparams(("parallel", "parallel")),
    )(xh, gain, cos, sin, rm)


def _qk_prep_bwd(name, xh, gain, cos, sin, rmt, dout):
    H, S, dh = xh.shape
    tm = _tile(S, 1024, 8)

    def body(x_ref, g_ref, c_ref, s_ref, rmt_ref, do_ref, dx_ref, dg_ref):
        first = (pl.program_id(0) == 0) & (pl.program_id(1) == 0)
        xv = x_ref[...]
        r = lax.rsqrt(jnp.mean(xv * xv, axis=-1, keepdims=True) + RMS_EPS)
        xhat = xv * r
        dov = do_ref[...]
        dy = dov * c_ref[...] + _exact_perm(dov * s_ref[...], rmt_ref[...])
        dxhat = dy * g_ref[...]
        dx_ref[...] = (r * (dxhat - xhat * jnp.mean(dxhat * xhat, axis=-1, keepdims=True))).astype(BF16)

        @pl.when(first)
        def _():
            dg_ref[...] = jnp.zeros_like(dg_ref)

        dg_ref[...] += jnp.sum(dy * xhat, axis=0, keepdims=True)

    blk = pl.BlockSpec((None, tm, dh), lambda h, i: (h, i, 0))
    tab = pl.BlockSpec((tm, dh), lambda h, i: (i, 0))
    vec = pl.BlockSpec((1, dh), lambda h, i: (0, 0))
    return pl.pallas_call(
        body, name=name, grid=(H, S // tm),
        in_specs=[blk, vec, tab, tab, pl.BlockSpec((dh, dh), lambda h, i: (0, 0)), blk],
        out_specs=(blk, vec), out_shape=(jax.ShapeDtypeStruct((H, S, dh), BF16), jax.ShapeDtypeStruct((1, dh), F32)),
        compiler_params=_cparams(("arbitrary", "arbitrary")),
    )(xh, gain, cos, sin, rmt, dout)


def _attn_probs(q, kp, kc, sink_col, n, scale):
    rows = q.shape[0]
    sp = lax.dot_general(q, kp, NT, preferred_element_type=F32) * scale
    sc = lax.dot_general(q, kc, NT, preferred_element_type=F32) * scale
    qi = lax.broadcasted_iota(jnp.int32, (rows, BLOCK), 0) % BLOCK
    kj = lax.broadcasted_iota(jnp.int32, (rows, BLOCK), 1)
    sp = jnp.where((kj > qi) & (n > 0), sp, NEG_INF)
    sc = jnp.where(kj <= qi, sc, NEG_INF)
    m = jnp.maximum(jnp.maximum(jnp.max(sp, axis=-1, keepdims=True), jnp.max(sc, axis=-1, keepdims=True)), sink_col)
    pp, pc, ps = jnp.exp(sp - m), jnp.exp(sc - m), jnp.exp(sink_col - m)
    inv = 1.0 / (jnp.sum(pp, axis=-1, keepdims=True) + jnp.sum(pc, axis=-1, keepdims=True) + ps)
    return pp * inv, pc * inv, ps * inv


def _sink_col(sink_ref, hk, group):
    rows = group * BLOCK
    g = lax.broadcasted_iota(jnp.int32, (rows, 1), 0) // BLOCK
    col = jnp.zeros((rows, 1), F32)
    for i in range(group):
        col = jnp.where(g == i, sink_ref[hk * group + i], col)
    return col


def _attn_specs(group, S, dh):
    heads = pl.BlockSpec((group, S, dh), lambda hk: (hk, 0, 0))
    kv = pl.BlockSpec((None, S, dh), lambda hk: (hk, 0, 0))
    return heads, kv, pl.BlockSpec(memory_space=pltpu.SMEM)


def _block_rows(n):
    cur = pl.ds(pl.multiple_of(n * BLOCK, BLOCK), BLOCK)
    prev = pl.ds(pl.multiple_of(jnp.maximum(n - 1, 0) * BLOCK, BLOCK), BLOCK)
    return cur, prev


def _attn_fwd(name, q, k, v, sinks):
    HQ, S, dh = q.shape
    HKV = k.shape[0]
    group = HQ // HKV
    scale = dh ** -0.5
    heads, kv, smem = _attn_specs(group, S, dh)

    def body(q_ref, k_ref, v_ref, sink_ref, o_ref):
        sink = _sink_col(sink_ref, pl.program_id(0), group)

        def block(n, carry):
            cur, prev = _block_rows(n)
            qv = q_ref[:, cur, :].reshape(group * BLOCK, dh)
            pp, pc, _ = _attn_probs(qv, k_ref[prev, :], k_ref[cur, :], sink, n, scale)
            o = jnp.dot(pp.astype(BF16), v_ref[prev, :], preferred_element_type=F32)
            o = o + jnp.dot(pc.astype(BF16), v_ref[cur, :], preferred_element_type=F32)
            o_ref[:, cur, :] = o.reshape(group, BLOCK, dh).astype(BF16)
            return carry

        lax.fori_loop(0, S // BLOCK, block, 0)

    return pl.pallas_call(
        body, name=name, grid=(HKV,), in_specs=[heads, kv, kv, smem], out_specs=heads,
        out_shape=jax.ShapeDtypeStruct((HQ, S, dh), BF16), compiler_params=_cparams(("parallel",)),
    )(q, k, v, sinks)


def _attn_bwd(name, q, k, v, sinks, do):
    HQ, S, dh = q.shape
    HKV = k.shape[0]
    group = HQ // HKV
    scale = dh ** -0.5
    heads, kv, smem = _attn_specs(group, S, dh)
    sk = pl.BlockSpec((None, group, LANES), lambda hk: (hk, 0, 0))

    def body(q_ref, k_ref, v_ref, sink_ref, do_ref, dq_ref, dk_ref, dv_ref, ds_ref):
        rows = group * BLOCK
        sink = _sink_col(sink_ref, pl.program_id(0), group)
        dk_ref[...] = jnp.zeros_like(dk_ref)
        dv_ref[...] = jnp.zeros_like(dv_ref)
        tdot = lambda a, b: lax.dot_general(a, b, TN, preferred_element_type=F32)

        def block(n, dsink):
            cur, prev = _block_rows(n)
            qv = q_ref[:, cur, :].reshape(rows, dh)
            dov = do_ref[:, cur, :].reshape(rows, dh)
            kp, kc, vp, vc = k_ref[prev, :], k_ref[cur, :], v_ref[prev, :], v_ref[cur, :]
            pp, pc, ps = _attn_probs(qv, kp, kc, sink, n, scale)
            dpp = lax.dot_general(dov, vp, NT, preferred_element_type=F32)
            dpc = lax.dot_general(dov, vc, NT, preferred_element_type=F32)
            delta = jnp.sum(pp * dpp, axis=-1, keepdims=True) + jnp.sum(pc * dpc, axis=-1, keepdims=True)
            dsp = (pp * (dpp - delta) * scale).astype(BF16)
            dsc = (pc * (dpc - delta) * scale).astype(BF16)
            dq = jnp.dot(dsp, kp, preferred_element_type=F32) + jnp.dot(dsc, kc, preferred_element_type=F32)
            dq_ref[:, cur, :] = dq.reshape(group, BLOCK, dh)
            dk_ref[prev, :] += tdot(dsp, qv)
            dv_ref[prev, :] += tdot(pp.astype(BF16), dov)
            dk_ref[cur, :] += tdot(dsc, qv)
            dv_ref[cur, :] += tdot(pc.astype(BF16), dov)
            return dsink - jnp.sum((ps * delta).reshape(group, BLOCK, 1), axis=1)

        dsink = lax.fori_loop(0, S // BLOCK, block, jnp.zeros((group, 1), F32))
        ds_ref[...] = jnp.broadcast_to(dsink, (group, LANES))

    return pl.pallas_call(
        body, name=name, grid=(HKV,), in_specs=[heads, kv, kv, smem, heads], out_specs=(heads, kv, kv, sk),
        out_shape=(jax.ShapeDtypeStruct((HQ, S, dh), F32), jax.ShapeDtypeStruct((HKV, S, dh), F32),
                   jax.ShapeDtypeStruct((HKV, S, dh), F32), jax.ShapeDtypeStruct((HKV, group, LANES), F32)),
        compiler_params=_cparams(("parallel",)),
    )(q, k, v, sinks, do)


def _gate_specs(S, D, ga_off, gb_off):
    tg = LANES
    for t in range(LANES, 513, LANES):
        if D % t == 0 and ga_off % t == 0 and gb_off % t == 0:
            tg = t
    if D % LANES:
        tg = math.gcd(math.gcd(D, ga_off), gb_off)
    tm = _tile(S, 512, 8)
    act = pl.BlockSpec((tm, tg), lambda i, j: (i, j))
    ga = pl.BlockSpec((tm, tg), lambda i, j: (i, ga_off // tg + j))
    gb = pl.BlockSpec((tm, tg), lambda i, j: (i, gb_off // tg + j))
    return (S // tm, D // tg), act, ga, gb


def _gate_fwd(name, proj, ya, yb, ga_off, gb_off):
    S, D = ya.shape
    grid, act, ga, gb = _gate_specs(S, D, ga_off, gb_off)

    def body(ga_ref, gb_ref, ya_ref, yb_ref, o_ref):
        o_ref[...] = (_sigmoid(ga_ref[...]) * ya_ref[...] + _sigmoid(gb_ref[...]) * yb_ref[...]).astype(BF16)

    return pl.pallas_call(
        body, name=name, grid=grid, in_specs=[ga, gb, act, act], out_specs=act,
        out_shape=jax.ShapeDtypeStruct((S, D), BF16), compiler_params=_cparams(("parallel", "parallel")),
    )(proj, proj, ya, yb)


def _gate_bwd(name, proj, ya, yb, dm, ga_off, gb_off):
    S, D = ya.shape
    grid, act, ga, gb = _gate_specs(S, D, ga_off, gb_off)

    def body(ga_ref, gb_ref, ya_ref, yb_ref, dm_ref, dga_ref, dgb_ref, dya_ref, dyb_ref):
        dmv = dm_ref[...]
        sa, sb = _sigmoid(ga_ref[...]), _sigmoid(gb_ref[...])
        dga_ref[...] = (dmv * ya_ref[...] * sa * (1.0 - sa)).astype(BF16)
        dgb_ref[...] = (dmv * yb_ref[...] * sb * (1.0 - sb)).astype(BF16)
        dya_ref[...] = (dmv * sa).astype(BF16)
        dyb_ref[...] = (dmv * sb).astype(BF16)

    o = jax.ShapeDtypeStruct((S, D), BF16)
    return pl.pallas_call(
        body, name=name, grid=grid, in_specs=[ga, gb, act, act, act], out_specs=(act, act, act, act),
        out_shape=(o, o, o, o), compiler_params=_cparams(("parallel", "parallel")),
    )(proj, proj, ya, yb, dm)


ANY = pl.BlockSpec(memory_space=pl.ANY)


def _row_tile(rows, cols, n_arrays):
    want = max(16, (VMEM_LIMIT_V7X // 2) // (2 * n_arrays * cols * 4))
    return _tile(rows, want, 16)


def _cast_to_slot(name, w, dtype, p_arr, dep=None):
    R, C = w.shape
    tr = _row_tile(R, C, 2)
    extra = () if dep is None else (dep,)

    def body(p_ref, w_ref, *rest):
        rest[-1][...] = w_ref[...].astype(dtype)

    return pl.pallas_call(
        body, name=name,
        grid_spec=pltpu.PrefetchScalarGridSpec(
            num_scalar_prefetch=1, grid=(R // tr,),
            in_specs=[pl.BlockSpec((tr, C), lambda i, p_ref: (i, 0))] + [pl.BlockSpec(d.shape, lambda i, p_ref: (0, 0)) for d in extra],
            out_specs=pl.BlockSpec((None, tr, C), lambda i, p_ref: (p_ref[0], i, 0))),
        out_shape=jax.ShapeDtypeStruct((N_CHIPS, R, C), dtype), compiler_params=_cparams(("parallel",)),
    )(p_arr, w, *extra)


def _add_half(name, g3, r3, c_arr):
    n, h, C = r3.shape
    tr = _row_tile(h, C, 3)
    nb = h // tr

    def body(c_ref, g_ref, r_ref, o_ref):
        o_ref[...] = (g_ref[...].astype(F32) + r_ref[...].astype(F32)).astype(BF16)

    blk = pl.BlockSpec((None, tr, C), lambda s, i, c_ref: (s, i, 0))
    return pl.pallas_call(
        body, name=name,
        grid_spec=pltpu.PrefetchScalarGridSpec(
            num_scalar_prefetch=1, grid=(n, nb),
            in_specs=[pl.BlockSpec((None, tr, C), lambda s, i, c_ref: (s, c_ref[0] * nb + i, 0)), blk], out_specs=blk),
        out_shape=jax.ShapeDtypeStruct(r3.shape, BF16), compiler_params=_cparams(("parallel", "parallel")),
    )(c_arr, g3, r3)


def _add_chips(name, t3, r3, cp_arr):
    n, h, C = r3.shape
    tr = _row_tile(h, C, 6)
    nb = h // tr

    def body(cp_ref, t_ref, r0_ref, r1_ref, r2_ref, r3_ref, o_ref):
        p = cp_ref[1]
        total = None
        for a, r_ref in enumerate((r0_ref, r1_ref, r2_ref, r3_ref)):
            part = jnp.where(p == a, t_ref[...], r_ref[...]).astype(F32)
            total = part if total is None else total + part
        o_ref[...] = total

    def part(a):
        return pl.BlockSpec((None, tr, C), lambda i, cp_ref: (jnp.where(cp_ref[1] == a, (a + 1) % N_CHIPS, a), i, 0))

    return pl.pallas_call(
        body, name=name,
        grid_spec=pltpu.PrefetchScalarGridSpec(
            num_scalar_prefetch=1, grid=(nb,),
            in_specs=[pl.BlockSpec((None, tr, C), lambda i, cp_ref: (cp_ref[1], i, 0)), part(0), part(1), part(2), part(3)],
            out_specs=pl.BlockSpec((tr, C), lambda i, cp_ref: (cp_ref[0] * nb + i, 0))),
        out_shape=jax.ShapeDtypeStruct((2 * h, C), F32), compiler_params=_cparams(("parallel",)),
    )(cp_arr, t3, r3, r3, r3, r3)


def _adamw(name, w, g, m, v, deps=()):
    R, C = w.shape
    extra = tuple(deps)
    tr = _row_tile(R, C, 8)
    c1 = 1.0 - ADAM_B1 ** ADAM_STEP
    c2 = 1.0 - ADAM_B2 ** ADAM_STEP

    def body(w_ref, g_ref, m_ref, v_ref, *rest):
        go_ref, d_ref, nm_ref, nv_ref = rest[-4:]
        gv = g_ref[...]
        go_ref[...] = gv
        nm = ADAM_B1 * m_ref[...] + (1.0 - ADAM_B1) * gv
        nv = ADAM_B2 * v_ref[...] + (1.0 - ADAM_B2) * (gv * gv)
        d_ref[...] = -ADAM_LR * ((nm / c1) / (jnp.sqrt(nv / c2) + ADAM_EPS) + ADAM_WD * w_ref[...])
        nm_ref[...] = nm
        nv_ref[...] = nv

    blk = pl.BlockSpec((tr, C), lambda i: (i, 0))
    o = jax.ShapeDtypeStruct((R, C), F32)
    return pl.pallas_call(
        body, name=name, grid=(R // tr,), in_specs=[blk, blk, blk, blk] + [ANY] * len(extra), out_specs=(blk, blk, blk, blk),
        out_shape=(o, o, o, o), compiler_params=_cparams(("parallel",)),
    )(w, g, m, v, *extra)


def _place():
    x, y, c = lax.axis_index("x"), lax.axis_index("y"), lax.axis_index("c")
    chips = [(1 - x, y), (x, 1 - y), (1 - x, 1 - y)]
    return x, y, c, 2 * x + y, chips


HBM = pl.BlockSpec(memory_space=pltpu.HBM)
SEM = pl.BlockSpec(memory_space=pltpu.SEMAPHORE)
TOKEN = jax.ShapeDtypeStruct((8, LANES), F32)
DATAFLOW = pltpu.SideEffectType.DATAFLOW_SIDE_EFFECTING


def _hbm(a):
    return pltpu.with_memory_space_constraint(a, pltpu.HBM)


def _gather_blocks(bufs, i, c, p, chips):
    if bufs[i].shape[1] % 16:
        return bufs[i].at[p], [bufs[i].at[2 * cx + cy] for cx, cy in chips]
    h = bufs[i].shape[1] // 2
    rows = pl.ds(pl.multiple_of(c * h, 16), h)
    return bufs[i].at[p, rows], [bufs[i].at[2 * cx + cy, rows] for cx, cy in chips]


def _gather_start(name, groups, dep):
    slots = [s for g in groups for s in g]
    n, ng = len(slots), len(groups)

    def body(*refs):
        bufs, sems, token = refs[:n], refs[n + 1:n + 1 + 2 * ng], refs[-1]
        x, y, c, p, chips = _place()
        i = 0
        for gi, g in enumerate(groups):
            send, recv = sems[2 * gi], sems[2 * gi + 1]
            for k in range(len(g)):
                mine, _ = _gather_blocks(bufs, i, c, p, chips)
                for j, chip in enumerate(chips):
                    pltpu.make_async_remote_copy(src_ref=mine, dst_ref=mine, send_sem=send.at[3 * k + j], recv_sem=recv.at[3 * k + j],
                                                 device_id=(*chip, c), device_id_type=MESH).start()
                i += 1
        token[...] = jnp.zeros_like(token)

    sem_shapes = [pltpu.SemaphoreType.DMA((3 * len(g),)) for g in groups for _ in range(2)]
    out = pl.pallas_call(
        body, name=name, in_specs=[HBM] * n + [ANY],
        out_specs=(*([SEM] * (2 * ng)), *([HBM] * n), pl.BlockSpec(memory_space=pltpu.VMEM)),
        out_shape=(*sem_shapes, *[pltpu.HBM(s.shape, s.dtype) for s in slots], TOKEN),
        input_output_aliases={i: 2 * ng + i for i in range(n)},
        compiler_params=pltpu.CompilerParams(has_side_effects=DATAFLOW),
    )(*[_hbm(s) for s in slots], dep)
    started, i = [], 2 * ng
    for gi, g in enumerate(groups):
        started.append((out[2 * gi], out[2 * gi + 1], list(out[i:i + len(g)])))
        i += len(g)
    return started, out[-1]


def _gather_wait(name, send, recv, slots, after):
    n = len(slots)

    def body(*refs):
        bufs, send, recv = refs[:n], refs[n], refs[n + 1]
        x, y, c, p, chips = _place()
        for i in range(n):
            mine, landed = _gather_blocks(bufs, i, c, p, chips)
            for j, chip in enumerate(chips):
                cp = pltpu.make_async_remote_copy(src_ref=mine, dst_ref=landed[j], send_sem=send.at[3 * i + j],
                                                  recv_sem=recv.at[3 * i + j], device_id=(*chip, c), device_id_type=MESH)
                cp.wait_send()
                cp.wait_recv()

    return list(pl.pallas_call(
        body, name=name, in_specs=[HBM] * n + [SEM, SEM, ANY], out_specs=tuple([HBM] * n),
        out_shape=tuple(pltpu.HBM(s.shape, s.dtype) for s in slots),
        input_output_aliases={i: i for i in range(n)},
        compiler_params=pltpu.CompilerParams(has_side_effects=DATAFLOW),
    )(*slots, send, recv, after))


def _gather_forward(name, slots):
    idx = [i for i, s in enumerate(slots) if s.shape[1] % 16 == 0]
    n = len(slots)

    def body(*refs):
        bufs = refs[n:2 * n]
        send, recv = refs[2 * n:]
        x, y, c, p, chips = _place()

        def rdma(k, ref):
            return pltpu.make_async_remote_copy(src_ref=ref, dst_ref=ref, send_sem=send.at[k], recv_sem=recv.at[k],
                                                device_id=(x, y, 1 - c), device_id_type=MESH)

        cps = []
        for k, i in enumerate(idx):
            for j, ref in enumerate(_gather_blocks(bufs, i, c, p, chips)[1]):
                cps.append(rdma(3 * k + j, ref))
                cps[-1].start()
        for k, i in enumerate(idx):
            for j, ref in enumerate(_gather_blocks(bufs, i, 1 - c, p, chips)[1]):
                rdma(3 * k + j, ref).wait_recv()
        for cp in cps:
            cp.wait_send()

    return list(pl.pallas_call(
        body, name=name, in_specs=[ANY] * n, out_specs=tuple([ANY] * n),
        out_shape=tuple(jax.ShapeDtypeStruct(s.shape, s.dtype) for s in slots),
        scratch_shapes=[pltpu.SemaphoreType.DMA((3 * len(idx),)), pltpu.SemaphoreType.DMA((3 * len(idx),))],
        input_output_aliases={i: i for i in range(n)},
        compiler_params=pltpu.CompilerParams(has_side_effects=True),
    )(*slots))


def _swap_copy(grads, lands, send, recv, i, x, y, c):
    h = grads[i].shape[1] // 2
    other = pl.ds(pl.multiple_of((1 - c) * h, 16), h)
    return pltpu.make_async_remote_copy(src_ref=grads[i].at[:, other, :], dst_ref=lands[i], send_sem=send.at[i],
                                        recv_sem=recv.at[i], device_id=(x, y, 1 - c), device_id_type=MESH)


def _swap_wait(name, send, recv, grads, lands, after):
    n = len(grads)

    def body(*refs):
        ins, lands, send, recv = refs[:n], refs[n:2 * n], refs[2 * n], refs[2 * n + 1]
        x, y, c, p, chips = _place()
        for i in range(n):
            cp = _swap_copy(ins, lands, send, recv, i, x, y, c)
            cp.wait_send()
            cp.wait_recv()

    shapes = [pltpu.HBM(t.shape, t.dtype) for t in list(grads) + list(lands)]
    out = pl.pallas_call(
        body, name=name, in_specs=[HBM] * (2 * n) + [SEM, SEM, ANY], out_specs=tuple([HBM] * (2 * n)),
        out_shape=tuple(shapes), input_output_aliases={i: i for i in range(2 * n)},
        compiler_params=pltpu.CompilerParams(has_side_effects=DATAFLOW),
    )(*grads, *lands, send, recv, after)
    return list(out[:n]), list(out[n:])


def _reduce_starts(name, grads, parts):
    ng, npt = len(grads), len(parts)
    halves = [(g.shape[0], g.shape[1] // 2, g.shape[2]) for g in grads]
    arrays = list(grads) + [lax.empty(s, g.dtype) for s, g in zip(halves, grads)] + list(parts) + [lax.empty(t.shape, t.dtype) for t in parts]
    na = len(arrays)
    sems = ([pltpu.SemaphoreType.DMA((ng,))] * 2 if ng else []) + ([pltpu.SemaphoreType.DMA((3 * npt,))] * 2 if npt else [])
    ns = len(sems)

    def body(*refs):
        ins, sem, token = refs[:na], list(refs[na:na + ns]), refs[-1]
        x, y, c, p, chips = _place()
        if ng:
            for i in range(ng):
                _swap_copy(ins[:ng], ins[ng:2 * ng], sem[0], sem[1], i, x, y, c).start()
        if npt:
            src, land, send, recv = ins[2 * ng:2 * ng + npt], ins[2 * ng + npt:], sem[-2], sem[-1]
            for i in range(npt):
                for j, (cx, cy) in enumerate(chips):
                    pltpu.make_async_remote_copy(src_ref=src[i].at[2 * cx + cy], dst_ref=land[i].at[p], send_sem=send.at[3 * i + j],
                                                 recv_sem=recv.at[3 * i + j], device_id=(cx, cy, c), device_id_type=MESH).start()
        token[...] = jnp.zeros_like(token)

    out = pl.pallas_call(
        body, name=name, in_specs=[HBM] * na,
        out_specs=(*([SEM] * ns), *([HBM] * na), pl.BlockSpec(memory_space=pltpu.VMEM)),
        out_shape=(*sems, *[pltpu.HBM(a.shape, a.dtype) for a in arrays], TOKEN),
        input_output_aliases={i: ns + i for i in range(na)},
        compiler_params=pltpu.CompilerParams(has_side_effects=DATAFLOW),
    )(*[_hbm(a) for a in arrays])
    bufs = list(out[ns:ns + na])
    swap = (out[0], out[1], bufs[:ng], bufs[ng:2 * ng]) if ng else None
    exch = (out[ns - 2], out[ns - 1], bufs[2 * ng:2 * ng + npt], bufs[2 * ng + npt:]) if npt else None
    return swap, exch, out[-1]


def _exchange_wait(name, send, recv, parts, lands, after):
    n = len(parts)

    def body(*refs):
        ins, lands, send, recv = refs[:n], refs[n:2 * n], refs[2 * n], refs[2 * n + 1]
        x, y, c, p, chips = _place()
        for i in range(n):
            for j, (cx, cy) in enumerate(chips):
                q = 2 * cx + cy
                cp = pltpu.make_async_remote_copy(src_ref=ins[i].at[q], dst_ref=lands[i].at[q], send_sem=send.at[3 * i + j],
                                                  recv_sem=recv.at[3 * i + j], device_id=(cx, cy, c), device_id_type=MESH)
                cp.wait_send()
                cp.wait_recv()

    shapes = [pltpu.HBM(t.shape, t.dtype) for t in parts]
    out = pl.pallas_call(
        body, name=name, in_specs=[HBM] * (2 * n) + [SEM, SEM, ANY], out_specs=tuple([HBM] * (2 * n)),
        out_shape=(*shapes, *shapes), input_output_aliases={i: i for i in range(2 * n)},
        compiler_params=pltpu.CompilerParams(has_side_effects=DATAFLOW),
    )(*parts, *lands, send, recv, after)
    return list(out[:n]), list(out[n:])


def _join_copy(buf, send_sem, recv_sem, which, x, y, c):
    h = buf.shape[0] // 2
    rows = buf.at[pl.ds(pl.multiple_of(which * h, 8), h)]
    return pltpu.make_async_remote_copy(src_ref=rows, dst_ref=rows, send_sem=send_sem, recv_sem=recv_sem,
                                        device_id=(x, y, 1 - c), device_id_type=MESH)


def _join_start(name, groups):
    bufs = [b for g in groups for b in g]
    n, ng = len(bufs), len(groups)

    def body(*refs):
        ins, sems, token = refs[:n], refs[n:n + 2 * ng], refs[-1]
        x, y, c, p, chips = _place()
        i = 0
        for gi, g in enumerate(groups):
            for k in range(len(g)):
                _join_copy(ins[i], sems[2 * gi].at[k], sems[2 * gi + 1].at[k], c, x, y, c).start()
                i += 1
        token[...] = jnp.zeros_like(token)

    sem_shapes = [pltpu.SemaphoreType.DMA((len(g),)) for g in groups for _ in range(2)]
    out = pl.pallas_call(
        body, name=name, in_specs=[HBM] * n,
        out_specs=(*([SEM] * (2 * ng)), *([HBM] * n), pl.BlockSpec(memory_space=pltpu.VMEM)),
        out_shape=(*sem_shapes, *[pltpu.HBM(t.shape, t.dtype) for t in bufs], TOKEN),
        input_output_aliases={i: 2 * ng + i for i in range(n)},
        compiler_params=pltpu.CompilerParams(has_side_effects=DATAFLOW),
    )(*[_hbm(t) for t in bufs])
    started, i = [], 2 * ng
    for gi, g in enumerate(groups):
        started.append((out[2 * gi], out[2 * gi + 1], list(out[i:i + len(g)])))
        i += len(g)
    return started, out[-1]


def _join_wait(name, send, recv, bufs, after):
    n = len(bufs)

    def body(*refs):
        ins, send, recv = refs[:n], refs[n], refs[n + 1]
        x, y, c, p, chips = _place()
        for i in range(n):
            _join_copy(ins[i], send.at[i], recv.at[i], c, x, y, c).wait_send()
            _join_copy(ins[i], send.at[i], recv.at[i], 1 - c, x, y, c).wait_recv()

    return list(pl.pallas_call(
        body, name=name, in_specs=[HBM] * n + [SEM, SEM, ANY], out_specs=tuple([HBM] * n),
        out_shape=tuple(pltpu.HBM(t.shape, t.dtype) for t in bufs), input_output_aliases={i: i for i in range(n)},
        compiler_params=pltpu.CompilerParams(has_side_effects=DATAFLOW),
    )(*bufs, send, recv, after))


def _allreduce_small(name, pack, dep):
    R, W = pack.shape

    def body(in_ref, dep_ref, out_ref, slots, send, recv):
        x, y, c = lax.axis_index("x"), lax.axis_index("y"), lax.axis_index("c")
        me = 4 * x + 2 * y + c
        slots[0] = in_ref[...]
        cps = []
        for k in range(1, N_DEV):
            peer = (x ^ (k >> 2), y ^ ((k >> 1) & 1), c ^ (k & 1))
            cp = pltpu.make_async_remote_copy(src_ref=in_ref, dst_ref=slots.at[k], send_sem=send.at[k - 1],
                                              recv_sem=recv.at[k - 1], device_id=peer, device_id_type=MESH)
            cp.start()
            cps.append(cp)
        for cp in cps:
            cp.wait()
        total = slots[me]
        for a in range(1, N_DEV):
            total = total + slots[jnp.bitwise_xor(a, me)]
        out_ref[...] = total

    vmem = pl.BlockSpec(memory_space=pltpu.VMEM)
    return pl.pallas_call(
        body, name=name, in_specs=[vmem, ANY], out_specs=vmem, out_shape=jax.ShapeDtypeStruct((R, W), F32),
        scratch_shapes=[pltpu.VMEM((N_DEV, R, W), F32), pltpu.SemaphoreType.DMA((N_DEV - 1,)), pltpu.SemaphoreType.DMA((N_DEV - 1,))],
        compiler_params=pltpu.CompilerParams(has_side_effects=True),
    )(pack, dep)


def _heads(a, n_heads):
    S = a.shape[0]
    return a.reshape(S, n_heads, a.shape[1] // n_heads).transpose(1, 0, 2)


def _unheads(a):
    H, S, dh = a.shape
    return a.transpose(1, 0, 2).reshape(S, H * dh)


def _ffn_bwd(tag, xin, gain, wgu3, wd, saved, dxout, dxo_b, reduce_start, dep, flush=None):
    h, gu, act = saved
    D = xin.shape[1]
    tok = reduce_start({f"w_down{tag}": _mm_tn(f"dw_down_{tag}", act, dxo_b, 0.5, dep=dep).reshape(N_CHIPS, -1, D)})
    dgu = _ffn_down_bwd(f"ffn_down_bwd_{tag}", dxo_b, wd, gu, 0.5, dep=tok)
    tok = reduce_start({f"w_gu{tag}": _mm_tn_cols(f"dw_gu_{tag}", h, dgu, wgu3.shape[2], b_is_gu=True)})
    if flush is not None:
        tok = flush(tok)
    dh = _mm_nt_cols(f"ffn_up_bwd_{tag}", dgu, wgu3, a_is_gu=True, dep=tok)
    dxin, dxin_b, dgain = _rms_bwd(f"rms_bwd_{tag}", xin, gain, dh, dxout)
    return dxin, dxin_b, dgain, tok


def kernel(x, g_ffn1, w_gu1, w_down1, g_mix, w_in, conv_w, q_norm_g, k_norm_g, sinks, w_out_conv, w_out_attn, w_o, g_ffn2, w_gu2, w_down2, loss_target, m_g_ffn1, m_w_gu1, m_w_down1, m_g_mix, m_w_in, m_conv_w, m_q_norm_g, m_k_norm_g, m_sinks, m_w_out_conv, m_w_out_attn, m_w_o, m_g_ffn2, m_w_gu2, m_w_down2, v_g_ffn1, v_w_gu1, v_w_down1, v_g_mix, v_w_in, v_conv_w, v_q_norm_g, v_k_norm_g, v_sinks, v_w_out_conv, v_w_out_attn, v_w_o, v_g_ffn2, v_w_gu2, v_w_down2):
    S, D = x.shape[1], x.shape[2]
    dh = q_norm_g.shape[1]
    HQ = sinks.shape[1]
    HKV = HQ // 4
    AW, KVW, CW = HQ * dh, HKV * dh, D // 2
    off_q, off_k, off_v = 3 * CW, 3 * CW + AW, 3 * CW + AW + KVW
    off_ga, off_gb = off_v + KVW, off_v + KVW + D
    x0, target = x[0], loss_target[0]
    cx, cy, cc = lax.axis_index("x"), lax.axis_index("y"), lax.axis_index("c")
    chip = 2 * cx + cy
    p_arr = jnp.reshape(chip, (1,)).astype(jnp.int32)
    c_arr = jnp.reshape(cc, (1,)).astype(jnp.int32)
    cp_arr = jnp.stack([cc, chip]).astype(jnp.int32)
    wts = dict(g_ffn1=g_ffn1, w_gu1=w_gu1, w_down1=w_down1, g_mix=g_mix, w_in=w_in, conv_w=conv_w, q_norm_g=q_norm_g,
               k_norm_g=k_norm_g, sinks=sinks, w_out_conv=w_out_conv, w_out_attn=w_out_attn, w_o=w_o, g_ffn2=g_ffn2,
               w_gu2=w_gu2, w_down2=w_down2)
    ms = dict(g_ffn1=m_g_ffn1, w_gu1=m_w_gu1, w_down1=m_w_down1, g_mix=m_g_mix, w_in=m_w_in, conv_w=m_conv_w,
              q_norm_g=m_q_norm_g, k_norm_g=m_k_norm_g, sinks=m_sinks, w_out_conv=m_w_out_conv, w_out_attn=m_w_out_attn,
              w_o=m_w_o, g_ffn2=m_g_ffn2, w_gu2=m_w_gu2, w_down2=m_w_down2)
    vs = dict(g_ffn1=v_g_ffn1, w_gu1=v_w_gu1, w_down1=v_w_down1, g_mix=v_g_mix, w_in=v_w_in, conv_w=v_conv_w,
              q_norm_g=v_q_norm_g, k_norm_g=v_k_norm_g, sinks=v_sinks, w_out_conv=v_w_out_conv, w_out_attn=v_w_out_attn,
              w_o=v_w_o, g_ffn2=v_g_ffn2, w_gu2=v_w_gu2, w_down2=v_w_down2)
    order = list(wts)
    small_names = [k for k in order if not k.startswith("w_")]
    grad, delta, new_m, new_v = {}, {}, {}, {}

    def cast(keys, dep=None):
        return [_cast_to_slot(f"cast_{k}", wts[k][0], F32 if k == "conv_w" else BF16, p_arr, dep) for k in keys]

    def gather_finish(tag, started, after):
        send, recv, slots = started
        return _gather_forward(f"gather_forward_{tag}", _gather_wait(f"gather_wait_{tag}", send, recv, slots, after))

    swapping, pending = [], []

    def reduce_start(full, after=None):
        keys = [] if full is None else list(full)
        pkeys, parts = [], []
        if swapping:
            pkeys, send, recv, gs, lands = swapping.pop(0)
            gs, sib = _swap_wait(f"swap_wait_{pkeys[0]}", send, recv, gs, lands, after if full is None else full[keys[0]])
            parts = [_add_half(f"add_half_{k}", g, r, c_arr) for k, g, r in zip(pkeys, gs, sib)]
        swap, exch, tok = _reduce_starts(f"reduce_starts_{keys[0] if keys else 'last'}", [full[k] for k in keys], parts)
        if exch:
            pending.append((pkeys, *exch))
        if swap:
            swapping.append((keys, *swap))
        return tok

    def reduce_finish(entries, after):
        ready = []
        for keys, send, recv, parts, lands in entries:
            parts, lands = _exchange_wait(f"exchange_wait_{keys[0]}", send, recv, parts, lands, after)
            ready.append((keys, [_add_chips(f"add_chips_{k}", t, r, cp_arr) for k, t, r in zip(keys, parts, lands)]))
        started, last = _join_start(f"join_start_{ready[0][0][0]}", [halves for _, halves in ready])
        for (keys, _), (send, recv, halves) in zip(ready, started):
            for k, g2 in zip(keys, _join_wait(f"join_wait_{keys[0]}", send, recv, halves, last)):
                g2, d, nm, nv = _adamw(f"adamw_{k}", wts[k][0], g2, ms[k][0], vs[k][0], (last,))
                grad[k], delta[k], new_m[k], new_v[k] = g2[None], d[None], nm[None], nv[None]
                last = nv
        return last

    (st_gu1, st_d1), tok = _gather_start("gather_start_1", [cast(["w_gu1"]), cast(["w_down1"])], x0)
    later = ["w_in", "conv_w", "w_out_conv", "w_out_attn", "w_o", "w_gu2", "w_down2"]
    slot = dict(zip(later, cast(later, tok)))
    h1 = _rms_fwd("rms_fwd_1", x0, g_ffn1, slot["w_down2"])
    wgu1, = gather_finish("gu1", st_gu1, h1)
    (st_in, st_out, st_gu2, st_d2), tok = _gather_start(
        "gather_start_2", [[slot["w_in"], slot["conv_w"]], [slot["w_out_conv"], slot["w_out_attn"], slot["w_o"]],
                           [slot["w_gu2"]], [slot["w_down2"]]], wgu1)
    cos, sin, rm, rmt = _rope_consts(S, dh)
    sink_vec = sinks[0]

    gu1, act1 = _ffn_up("ffn_up_1", h1, wgu1, tok)
    wd1 = gather_finish("d1", st_d1, act1)[0].reshape(-1, D)
    x1 = _mm_res("ffn_down_1", act1, wd1, x0, 0.5)
    win3, convw3 = gather_finish("in", st_in, x1)
    h2 = _rms_fwd("rms_fwd_mix", x1, g_mix)
    proj = _mm_cols("in_proj", h2, win3, F32)
    aconv = _conv_fwd("conv_fwd", proj, convw3, CW)
    woc3, woa3, wo = gather_finish("out", st_out, aconv)
    wo = wo.reshape(-1, D)
    ya = _mm_cols("out_conv", aconv, woc3, F32)
    q_raw = _heads(proj[:, off_q:off_q + AW], HQ)
    k_raw = _heads(proj[:, off_k:off_k + KVW], HKV)
    vh = _heads(proj[:, off_v:off_v + KVW], HKV).astype(BF16)
    qn = _qk_prep("q_prep", q_raw, q_norm_g, cos, sin, rm)
    kn = _qk_prep("k_prep", k_raw, k_norm_g, cos, sin, rm)
    oh = _attn_fwd("attn_fwd", qn, kn, vh, sink_vec)
    o = _unheads(oh)
    yb = _mm_cols("out_attn", o, woa3, F32)
    merged = _gate_fwd("gate_fwd", proj, ya, yb, off_ga, off_gb)
    x2 = _mm_res("mix_out", merged, wo, x1, 1.0)
    wgu2, = gather_finish("gu2", st_gu2, x2)
    h3 = _rms_fwd("rms_fwd_2", x2, g_ffn2)
    gu2, act2 = _ffn_up("ffn_up_2", h3, wgu2)
    wd2 = gather_finish("d2", st_d2, act2)[0].reshape(-1, D)
    x3 = _mm_res("ffn_down_2", act2, wd2, x2, 0.5)

    dy, dy_b, loss_lanes = _loss_grad("loss_grad", x3, target)
    dx2, dx2_b, dg_ffn2, tok = _ffn_bwd("2", x2, g_ffn2, wgu2, wd2, (h3, gu2, act2), dy, dy_b, reduce_start, None)
    dmerged = _mm_nt("mix_out_bwd", dx2_b, wo, F32)
    tok = reduce_start(dict(w_o=_mm_tn("dw_o", merged, dx2_b, dep=tok).reshape(N_CHIPS, -1, D)))
    dga, dgb, dya, dyb = _gate_bwd("gate_bwd", proj, ya, yb, dmerged, off_ga, off_gb)
    daconv = _mm_nt_cols("out_conv_bwd", dya, woc3, dep=tok)
    dwoc = _mm_tn_cols("dw_out_conv", aconv, dya, woc3.shape[2])
    do = _mm_nt_cols("out_attn_bwd", dyb, woa3)
    dwoa = _mm_tn_cols("dw_out_attn", o, dyb, woa3.shape[2])
    tok = reduce_start(dict(w_out_conv=dwoc, w_out_attn=dwoa))
    dxc, dbg, dcg, dconvw = _conv_bwd("conv_bwd", proj, convw3, daconv, CW)
    dqn, dkn, dvh, dsink3 = _attn_bwd("attn_bwd", qn, kn, vh, sink_vec, _heads(do, HQ).astype(BF16))
    dq_raw, dqg = _qk_prep_bwd("q_prep_bwd", q_raw, q_norm_g, cos, sin, rmt, dqn)
    dk_raw, dkg = _qk_prep_bwd("k_prep_bwd", k_raw, k_norm_g, cos, sin, rmt, dkn)
    dproj = jnp.concatenate([dxc, dbg, dcg, _unheads(dq_raw), _unheads(dk_raw), _unheads(dvh).astype(BF16), dga, dgb], axis=1)
    dh2 = _mm_nt_cols("in_proj_bwd", dproj, win3, dep=tok)
    tok = reduce_start(dict(w_in=_mm_tn_cols("dw_in", h2, dproj, win3.shape[2])))
    dx1, dx1_b, dg_mix = _rms_bwd("rms_bwd_mix", x1, g_mix, dh2, dx2)
    dx0, _, dg_ffn1, tok = _ffn_bwd("1", x0, g_ffn1, wgu1, wd1, (h1, gu1, act1), dx1, dx1_b, reduce_start, tok, lambda after: reduce_start(None, after))

    def rows8(a):
        a = a.reshape(-1, a.shape[-1])
        return jnp.pad(a, ((0, -a.shape[0] % 8), (0, D - a.shape[1])))

    misc = jnp.concatenate([dqg, dkg, dsink3[:, :, 0].reshape(1, HQ), loss_lanes], axis=1)
    done = reduce_finish(pending[:-2], dx0)
    tot = _allreduce_small("allreduce_small", jnp.concatenate([rows8(a) for a in (dg_ffn1, dg_mix, dg_ffn2, dconvw, misc)], axis=0), done)
    reduce_finish(pending[-2:], tot)

    cw_s = conv_w.shape[2]
    conv_row0, misc_row = 24, 24 + (-(-N_CHIPS * CONV_K // 8)) * 8
    small_g = dict(g_ffn1=tot[0:1], g_mix=tot[8:9], g_ffn2=tot[16:17],
                   conv_w=lax.dynamic_slice(tot, (conv_row0 + CONV_K * chip, 0), (CONV_K, cw_s)),
                   q_norm_g=tot[misc_row:misc_row + 1, 0:dh], k_norm_g=tot[misc_row:misc_row + 1, dh:2 * dh],
                   sinks=tot[misc_row:misc_row + 1, 2 * dh:2 * dh + HQ])
    loss = (0.5 / D) * jnp.sum(tot[misc_row, 2 * dh + HQ:2 * dh + HQ + LANES])

    def small_pack(src):
        return jnp.concatenate([rows8(src[k]) for k in small_names], axis=0)

    _, sd, sm, sv = _adamw("adamw_small", small_pack(wts), small_pack(small_g), small_pack(ms), small_pack(vs))
    for i, k in enumerate(small_names):
        shape = wts[k].shape
        nr, ncol = math.prod(shape[:-1]), shape[-1]
        grad[k] = small_g[k].reshape(shape)
        delta[k], new_m[k], new_v[k] = (a[8 * i:8 * i + nr, 0:ncol].reshape(shape) for a in (sd, sm, sv))
    return (loss, dx0[None], *[grad[k] for k in order], *[delta[k] for k in order],
            *[new_m[k] for k in order], *[new_v[k] for k in order])
```

```python
import math

import numpy as np
import jax
import jax.numpy as jnp
from jax import lax
from jax.experimental import pallas as pl
from jax.experimental.pallas import tpu as pltpu

F32 = jnp.float32
BF16 = jnp.bfloat16
MESH = pl.DeviceIdType.MESH

RMS_EPS = 1e-6
BLOCK = 128
ROPE_THETA = 500000.0
NEG_INF = -1e30
CONV_K = 3
ADAM_LR, ADAM_B1, ADAM_B2, ADAM_EPS, ADAM_WD, ADAM_STEP = 0.001, 0.9, 0.999, 1e-08, 0.01, 10

VMEM_LIMIT_V7X = 56 * 1024 * 1024
LANES = 128
N_CHIPS = 4
N_DEV = 8


def _tile(n, want, align=LANES):
    best = None
    t = align
    while t <= min(n, want):
        if n % t == 0:
            best = t
        t += align
    return best or n


def _cparams(sem):
    return pltpu.CompilerParams(dimension_semantics=sem, vmem_limit_bytes=VMEM_LIMIT_V7X)


def _sigmoid(x):
    return 1.0 / (1.0 + jnp.exp(-x))


NN = (((1,), (0,)), ((), ()))
NT = (((1,), (1,)), ((), ()))
TN = (((0,), (0,)), ((), ()))


def _mm(name, grid, ins, in_specs, compute, out_shape, out_specs, epilogue, dep=None):
    if dep is not None:
        ins, in_specs = tuple(ins) + (dep,), list(in_specs) + [pl.BlockSpec(dep.shape, lambda *_: (0, 0))]
    n_in = len(ins)

    def body(*refs):
        epilogue(compute(refs[:n_in]), refs[:n_in], refs[n_in:])

    return pl.pallas_call(
        body, name=name, grid=grid, in_specs=in_specs, out_specs=out_specs, out_shape=out_shape,
        compiler_params=_cparams(("parallel", "arbitrary")),
    )(*ins)


def _dot(dims, a=0, b=1):
    return lambda refs: [lax.dot_general(refs[a][...], refs[b][...], dims, preferred_element_type=F32)]


def _ffn_up(name, h, wgu3, dep=None):
    S, D = h.shape
    Ns = wgu3.shape[2]
    F = 2 * Ns
    tm, tn = _tile(S, 512), _tile(Ns, 1408)
    nbs = Ns // tn

    def compute(refs):
        hv = refs[0][...]
        return [jnp.dot(hv, refs[1][...], preferred_element_type=F32), jnp.dot(hv, refs[2][...], preferred_element_type=F32)]

    def epi(accs, in_refs, out_refs):
        g, u = accs
        dgu_ref, a_ref = out_refs
        sg = _sigmoid(g)
        silu = g * sg
        dgu_ref[0] = (u * (sg * (1.0 + g * (1.0 - sg)))).astype(BF16)
        dgu_ref[1] = silu.astype(BF16)
        a_ref[...] = (silu * u).astype(BF16)

    return _mm(
        name, (F // tn, S // tm), (h, wgu3, wgu3),
        [pl.BlockSpec((tm, D), lambda j, i: (i, 0)),
         pl.BlockSpec((None, D, tn), lambda j, i: (j // nbs, 0, j % nbs)),
         pl.BlockSpec((None, D, tn), lambda j, i: (2 + j // nbs, 0, j % nbs))],
        compute, (jax.ShapeDtypeStruct((2, S, F), BF16), jax.ShapeDtypeStruct((S, F), BF16)),
        (pl.BlockSpec((2, tm, tn), lambda j, i: (0, i, j)), pl.BlockSpec((tm, tn), lambda j, i: (i, j))), epi, dep=dep)


def _mm_res(name, a, w, res, scale):
    S, K = a.shape
    N = w.shape[1]
    tm, tn = _tile(S, 512), _tile(N, 512 if K > 2816 else 1024)

    def epi(accs, in_refs, out_refs):
        out_refs[0][...] = in_refs[2][...] + scale * accs[0]

    return _mm(
        name, (N // tn, S // tm), (a, w, res),
        [pl.BlockSpec((tm, K), lambda j, i: (i, 0)), pl.BlockSpec((K, tn), lambda j, i: (0, j)),
         pl.BlockSpec((tm, tn), lambda j, i: (i, j))],
        _dot(NN), jax.ShapeDtypeStruct((S, N), F32), pl.BlockSpec((tm, tn), lambda j, i: (i, j)), epi)


def _mm_cols(name, a, w3, out_dtype):
    S, K = a.shape
    Ns = w3.shape[2]
    tm, tn = _tile(S, 512), _tile(Ns, 2304)
    nbs = Ns // tn

    def epi(accs, in_refs, out_refs):
        out_refs[0][...] = accs[0].astype(out_dtype)

    return _mm(
        name, (N_CHIPS * nbs, S // tm), (a, w3),
        [pl.BlockSpec((tm, K), lambda j, i: (i, 0)),
         pl.BlockSpec((None, K, tn), lambda j, i: (j // nbs, 0, j % nbs))],
        _dot(NN), jax.ShapeDtypeStruct((S, N_CHIPS * Ns), out_dtype), pl.BlockSpec((tm, tn), lambda j, i: (i, j)), epi)


def _mm_nt(name, a, w, out_dtype, scale=1.0):
    S, N = a.shape
    K = w.shape[0]
    tm, tn = _tile(S, 512), _tile(K, 1024)

    def epi(accs, in_refs, out_refs):
        out_refs[0][...] = (scale * accs[0]).astype(out_dtype)

    return _mm(
        name, (K // tn, S // tm), (a, w),
        [pl.BlockSpec((tm, N), lambda j, i: (i, 0)), pl.BlockSpec((tn, N), lambda j, i: (j, 0))],
        _dot(NT), jax.ShapeDtypeStruct((S, K), out_dtype), pl.BlockSpec((tm, tn), lambda j, i: (i, j)), epi)


def _ffn_down_bwd(name, dy, wd, gu, scale, dep=None):
    S, D = dy.shape
    F = wd.shape[0]
    tm, tn = _tile(S, 512), _tile(F, 1408)

    def epi(accs, in_refs, out_refs):
        da = scale * accs[0]
        out_refs[0][0] = (da * in_refs[2][0].astype(F32)).astype(BF16)
        out_refs[0][1] = (da * in_refs[2][1].astype(F32)).astype(BF16)

    return _mm(
        name, (F // tn, S // tm), (dy, wd, gu),
        [pl.BlockSpec((tm, D), lambda j, i: (i, 0)), pl.BlockSpec((tn, D), lambda j, i: (j, 0)),
         pl.BlockSpec((2, tm, tn), lambda j, i: (0, i, j))],
        _dot(NT), jax.ShapeDtypeStruct((2, S, F), BF16), pl.BlockSpec((2, tm, tn), lambda j, i: (0, i, j)), epi, dep=dep)


def _mm_nt_cols(name, a, w3, a_is_gu=False, dep=None):
    K, Ns = w3.shape[1], w3.shape[2]
    S = a.shape[1] if a_is_gu else a.shape[0]
    tm = _tile(S, 512)
    tn = _tile(K, max(LANES, (6 << 20) // (N_CHIPS * Ns * 2)))
    if a_is_gu:
        a_spec = pl.BlockSpec((2, tm, 2 * Ns), lambda i, j: (0, i, 0))
        part = lambda a_ref, s: a_ref[s // 2, :, (s % 2) * Ns:(s % 2 + 1) * Ns]
    else:
        a_spec = pl.BlockSpec((tm, N_CHIPS * Ns), lambda i, j: (i, 0))
        part = lambda a_ref, s: a_ref[:, s * Ns:(s + 1) * Ns]

    def compute(refs):
        total = None
        for s in range(N_CHIPS):
            prod = lax.dot_general(part(refs[0], s), refs[1][s], NT, preferred_element_type=F32)
            total = prod if total is None else total + prod
        return [total]

    def epi(accs, in_refs, out_refs):
        out_refs[0][...] = accs[0]

    return _mm(
        name, (S // tm, K // tn), (a, w3), [a_spec, pl.BlockSpec((N_CHIPS, tn, Ns), lambda i, j: (0, j, 0))],
        compute, jax.ShapeDtypeStruct((S, K), F32), pl.BlockSpec((tm, tn), lambda i, j: (i, j)), epi, dep=dep)


def _mm_tn(name, a, b, scale=1.0, dep=None):
    S, K = a.shape
    N = b.shape[1]
    tm, tn = _tile(K, 512), _tile(N, 1024)

    def epi(accs, in_refs, out_refs):
        out_refs[0][...] = (scale * accs[0]).astype(BF16)

    return _mm(
        name, (N // tn, K // tm), (a, b),
        [pl.BlockSpec((S, tm), lambda j, i: (0, i)), pl.BlockSpec((S, tn), lambda j, i: (0, j))],
        _dot(TN), jax.ShapeDtypeStruct((K, N), BF16), pl.BlockSpec((tm, tn), lambda j, i: (i, j)), epi, dep=dep)


def _mm_tn_cols(name, a, b, Ns, b_is_gu=False, dep=None):
    S, K = a.shape
    tm, tn = _tile(K, 512), _tile(Ns, 2304)
    nbs = Ns // tn
    if b_is_gu:
        b_spec = pl.BlockSpec((None, S, tn), lambda j, i: (j // (2 * nbs), 0, j % (2 * nbs)))
    else:
        b_spec = pl.BlockSpec((S, tn), lambda j, i: (0, j))

    def epi(accs, in_refs, out_refs):
        out_refs[0][...] = accs[0].astype(BF16)

    return _mm(
        name, (N_CHIPS * nbs, K // tm), (a, b), [pl.BlockSpec((S, tm), lambda j, i: (0, i)), b_spec],
        _dot(TN), jax.ShapeDtypeStruct((N_CHIPS, K, Ns), BF16),
        pl.BlockSpec((None, tm, tn), lambda j, i: (j // nbs, i, j % nbs)), epi, dep=dep)


def _rms_fwd(name, x, gain, dep=None):
    S, D = x.shape
    tm = _tile(S, 256, 8)
    extra = () if dep is None else (dep,)

    def body(x_ref, g_ref, *rest):
        h_ref = rest[-1]
        xv = x_ref[...]
        r = lax.rsqrt(jnp.mean(xv * xv, axis=-1, keepdims=True) + RMS_EPS)
        h_ref[...] = (xv * r * g_ref[...]).astype(BF16)

    return pl.pallas_call(
        body, name=name, grid=(S // tm,),
        in_specs=[pl.BlockSpec((tm, D), lambda i: (i, 0)), pl.BlockSpec((1, D), lambda i: (0, 0))]
        + [pl.BlockSpec(memory_space=pl.ANY) for d in extra],
        out_specs=pl.BlockSpec((tm, D), lambda i: (i, 0)), out_shape=jax.ShapeDtypeStruct((S, D), BF16),
        compiler_params=_cparams(("parallel",)),
    )(x, gain, *extra)


def _rms_bwd(name, x, gain, dh, dres):
    S, D = x.shape
    tm = _tile(S, 256, 8)

    def body(x_ref, g_ref, dh_ref, dres_ref, dx_ref, dxb_ref, dg_ref):
        i = pl.program_id(0)
        xv = x_ref[...]
        r = lax.rsqrt(jnp.mean(xv * xv, axis=-1, keepdims=True) + RMS_EPS)
        xhat = xv * r
        dhv = dh_ref[...]
        dxhat = dhv * g_ref[...]
        dx = dres_ref[...] + r * (dxhat - xhat * jnp.mean(dxhat * xhat, axis=-1, keepdims=True))
        dx_ref[...] = dx
        dxb_ref[...] = dx.astype(BF16)

        @pl.when(i == 0)
        def _():
            dg_ref[...] = jnp.zeros_like(dg_ref)

        dg_ref[...] += jnp.sum(dhv * xhat, axis=0, keepdims=True)

    row = pl.BlockSpec((tm, D), lambda i: (i, 0))
    vec = pl.BlockSpec((1, D), lambda i: (0, 0))
    return pl.pallas_call(
        body, name=name, grid=(S // tm,), in_specs=[row, vec, row, row], out_specs=(row, row, vec),
        out_shape=(jax.ShapeDtypeStruct((S, D), F32), jax.ShapeDtypeStruct((S, D), BF16), jax.ShapeDtypeStruct((1, D), F32)),
        compiler_params=_cparams(("arbitrary",)),
    )(x, gain, dh, dres)


def _loss_grad(name, y, target):
    S, D = y.shape
    tm = _tile(S, 256, 8)

    def body(y_ref, t_ref, dy_ref, dyb_ref, l_ref):
        i = pl.program_id(0)
        e = y_ref[...] - t_ref[...]
        dy_ref[...] = e * (1.0 / D)
        dyb_ref[...] = (e * (1.0 / D)).astype(BF16)
        col = jnp.sum(e * e, axis=0, keepdims=True)
        part = col[:, 0:LANES]
        for k in range(1, D // LANES):
            part = part + col[:, k * LANES:(k + 1) * LANES]

        @pl.when(i == 0)
        def _():
            l_ref[...] = jnp.zeros_like(l_ref)

        l_ref[...] += part

    row = pl.BlockSpec((tm, D), lambda i: (i, 0))
    return pl.pallas_call(
        body, name=name, grid=(S // tm,), in_specs=[row, row],
        out_specs=(row, row, pl.BlockSpec((1, LANES), lambda i: (0, 0))),
        out_shape=(jax.ShapeDtypeStruct((S, D), F32), jax.ShapeDtypeStruct((S, D), BF16), jax.ShapeDtypeStruct((1, LANES), F32)),
        compiler_params=_cparams(("arbitrary",)),
    )(y, target)


def _shift_down(u, k):
    rows = lax.broadcasted_iota(jnp.int32, u.shape, 0)
    return jnp.where(rows >= k, pltpu.roll(u, k, 0), 0.0)


def _shift_up(u, k):
    n = u.shape[0]
    rows = lax.broadcasted_iota(jnp.int32, u.shape, 0)
    return jnp.where(rows < n - k, pltpu.roll(u, n - k, 0), 0.0)


def _conv_specs(S, cw, conv_width):
    nb = conv_width // cw
    col = lambda off: pl.BlockSpec((S, cw), lambda j, off=off: (0, off * nb + j))
    return nb, col(0), col(1), col(2)


def _conv_fwd(name, proj, convw3, conv_width):
    S = proj.shape[0]
    cw = convw3.shape[2]
    nb, xc_s, bg_s, cg_s = _conv_specs(S, cw, conv_width)

    def body(xc_ref, bg_ref, cg_ref, w_ref, o_ref):
        u = cg_ref[...].astype(F32) * xc_ref[...].astype(F32)
        w = w_ref[...]
        cv = w[2:3, :] * u + w[1:2, :] * _shift_down(u, 1) + w[0:1, :] * _shift_down(u, 2)
        o_ref[...] = (bg_ref[...].astype(F32) * cv).astype(BF16)

    return pl.pallas_call(
        body, name=name, grid=(nb,),
        in_specs=[xc_s, bg_s, cg_s, pl.BlockSpec((None, CONV_K, cw), lambda j: (j, 0, 0))],
        out_specs=pl.BlockSpec((S, cw), lambda j: (0, j)), out_shape=jax.ShapeDtypeStruct((S, conv_width), BF16),
        compiler_params=_cparams(("parallel",)),
    )(proj, proj, proj, convw3)


def _conv_bwd(name, proj, convw3, da, conv_width):
    S = proj.shape[0]
    cw = convw3.shape[2]
    nb, xc_s, bg_s, cg_s = _conv_specs(S, cw, conv_width)

    def body(xc_ref, bg_ref, cg_ref, w_ref, da_ref, dxc_ref, dbg_ref, dcg_ref, dw_ref):
        xc, cg = xc_ref[...].astype(F32), cg_ref[...].astype(F32)
        u = cg * xc
        w = w_ref[...]
        u1, u2 = _shift_down(u, 1), _shift_down(u, 2)
        cv = w[2:3, :] * u + w[1:2, :] * u1 + w[0:1, :] * u2
        dav = da_ref[...]
        dbg_ref[...] = (dav * cv).astype(BF16)
        dcv = dav * bg_ref[...].astype(F32)
        du = w[2:3, :] * dcv + w[1:2, :] * _shift_up(dcv, 1) + w[0:1, :] * _shift_up(dcv, 2)
        dxc_ref[...] = (du * cg).astype(BF16)
        dcg_ref[...] = (du * xc).astype(BF16)
        dw_ref[0:1, :] = jnp.sum(dcv * u2, axis=0, keepdims=True)
        dw_ref[1:2, :] = jnp.sum(dcv * u1, axis=0, keepdims=True)
        dw_ref[2:3, :] = jnp.sum(dcv * u, axis=0, keepdims=True)

    wspec = pl.BlockSpec((None, CONV_K, cw), lambda j: (j, 0, 0))
    ospec = pl.BlockSpec((S, cw), lambda j: (0, j))
    act = jax.ShapeDtypeStruct((S, conv_width), BF16)
    return pl.pallas_call(
        body, name=name, grid=(nb,), in_specs=[xc_s, bg_s, cg_s, wspec, ospec],
        out_specs=(ospec, ospec, ospec, wspec),
        out_shape=(act, act, act, jax.ShapeDtypeStruct(convw3.shape, F32)),
        compiler_params=_cparams(("parallel",)),
    )(proj, proj, proj, convw3, da)


def _rope_consts(S, dh):
    rot = dh // 4
    half = rot // 2
    inv_freq = 1.0 / (ROPE_THETA ** (jnp.arange(0, rot, 2, dtype=F32) / rot))
    ang = jnp.arange(S, dtype=F32)[:, None] * inv_freq[None, :]
    cos = jnp.concatenate([jnp.cos(ang), jnp.cos(ang), jnp.ones((S, dh - rot), F32)], axis=1)
    sin = jnp.concatenate([jnp.sin(ang), jnp.sin(ang), jnp.zeros((S, dh - rot), F32)], axis=1)
    rm = np.zeros((dh, dh), np.float32)
    for j in range(half):
        rm[j + half, j] = -1.0
        rm[j, j + half] = 1.0
    return cos, sin, jnp.asarray(rm, BF16), jnp.asarray(rm.T, BF16)


def _exact_perm(y, rm):
    hi = y.astype(BF16)
    r1 = y - hi.astype(F32)
    mid = r1.astype(BF16)
    lo = (r1 - mid.astype(F32)).astype(BF16)
    dot = lambda a: jnp.dot(a, rm, preferred_element_type=F32)
    return dot(hi) + dot(mid) + dot(lo)


def _qk_prep(name, xh, gain, cos, sin, rm):
    H, S, dh = xh.shape
    tm = _tile(S, 1024, 8)

    def body(x_ref, g_ref, c_ref, s_ref, rm_ref, o_ref):
        xv = x_ref[...].astype(F32)
        y = xv * lax.rsqrt(jnp.mean(xv * xv, axis=-1, keepdims=True) + RMS_EPS) * g_ref[...]
        o_ref[...] = (y * c_ref[...] + _exact_perm(y, rm_ref[...]) * s_ref[...]).astype(BF16)

    blk = pl.BlockSpec((None, tm, dh), lambda h, i: (h, i, 0))
    tab = pl.BlockSpec((tm, dh), lambda h, i: (i, 0))
    return pl.pallas_call(
        body, name=name, grid=(H, S // tm),
        in_specs=[blk, pl.BlockSpec((1, dh), lambda h, i: (0, 0)), tab, tab, pl.BlockSpec((dh, dh), lambda h, i: (0, 0))],
        out_specs=blk, out_shape=jax.ShapeDtypeStruct((H, S, dh), BF16),
        compiler_params=_cparams(("parallel", "parallel")),
    )(xh, gain, cos, sin, rm)


def _qk_prep_bwd(name, xh, gain, cos, sin, rmt, dout):
    H, S, dh = xh.shape
    tm = _tile(S, 1024, 8)

    def body(x_ref, g_ref, c_ref, s_ref, rmt_ref, do_ref, dx_ref, dg_ref):
        first = (pl.program_id(0) == 0) & (pl.program_id(1) == 0)
        xv = x_ref[...].astype(F32)
        r = lax.rsqrt(jnp.mean(xv * xv, axis=-1, keepdims=True) + RMS_EPS)
        xhat = xv * r
        dov = do_ref[...]
        dy = dov * c_ref[...] + _exact_perm(dov * s_ref[...], rmt_ref[...])
        dxhat = dy * g_ref[...]
        dx_ref[...] = (r * (dxhat - xhat * jnp.mean(dxhat * xhat, axis=-1, keepdims=True))).astype(BF16)

        @pl.when(first)
        def _():
            dg_ref[...] = jnp.zeros_like(dg_ref)

        dg_ref[...] += jnp.sum(dy * xhat, axis=0, keepdims=True)

    blk = pl.BlockSpec((None, tm, dh), lambda h, i: (h, i, 0))
    tab = pl.BlockSpec((tm, dh), lambda h, i: (i, 0))
    vec = pl.BlockSpec((1, dh), lambda h, i: (0, 0))
    return pl.pallas_call(
        body, name=name, grid=(H, S // tm),
        in_specs=[blk, vec, tab, tab, pl.BlockSpec((dh, dh), lambda h, i: (0, 0)), blk],
        out_specs=(blk, vec), out_shape=(jax.ShapeDtypeStruct((H, S, dh), BF16), jax.ShapeDtypeStruct((1, dh), F32)),
        compiler_params=_cparams(("arbitrary", "arbitrary")),
    )(xh, gain, cos, sin, rmt, dout)


def _attn_probs(q, kp, kc, sink_col, n, scale):
    rows = q.shape[0]
    sp = lax.dot_general(q, kp, NT, preferred_element_type=F32) * scale
    sc = lax.dot_general(q, kc, NT, preferred_element_type=F32) * scale
    qi = lax.broadcasted_iota(jnp.int32, (rows, BLOCK), 0) % BLOCK
    kj = lax.broadcasted_iota(jnp.int32, (rows, BLOCK), 1)
    sp = jnp.where((kj > qi) & (n > 0), sp, NEG_INF)
    sc = jnp.where(kj <= qi, sc, NEG_INF)
    m = jnp.maximum(jnp.maximum(jnp.max(sp, axis=-1, keepdims=True), jnp.max(sc, axis=-1, keepdims=True)), sink_col)
    pp, pc, ps = jnp.exp(sp - m), jnp.exp(sc - m), jnp.exp(sink_col - m)
    inv = 1.0 / (jnp.sum(pp, axis=-1, keepdims=True) + jnp.sum(pc, axis=-1, keepdims=True) + ps)
    return pp * inv, pc * inv, ps * inv


def _sink_col(sink_ref, hk, group):
    rows = group * BLOCK
    g = lax.broadcasted_iota(jnp.int32, (rows, 1), 0) // BLOCK
    col = jnp.zeros((rows, 1), F32)
    for i in range(group):
        col = jnp.where(g == i, sink_ref[hk * group + i], col)
    return col


def _attn_specs(group, S, dh):
    heads = pl.BlockSpec((group, S, dh), lambda hk: (hk, 0, 0))
    kv = pl.BlockSpec((None, S, dh), lambda hk: (hk, 0, 0))
    return heads, kv, pl.BlockSpec(memory_space=pltpu.SMEM)


def _block_rows(n):
    cur = pl.ds(pl.multiple_of(n * BLOCK, BLOCK), BLOCK)
    prev = pl.ds(pl.multiple_of(jnp.maximum(n - 1, 0) * BLOCK, BLOCK), BLOCK)
    return cur, prev


def _attn_fwd(name, q, k, v, sinks):
    HQ, S, dh = q.shape
    HKV = k.shape[0]
    group = HQ // HKV
    scale = dh ** -0.5
    heads, kv, smem = _attn_specs(group, S, dh)

    def body(q_ref, k_ref, v_ref, sink_ref, o_ref):
        sink = _sink_col(sink_ref, pl.program_id(0), group)

        def block(n, carry):
            cur, prev = _block_rows(n)
            qv = q_ref[:, cur, :].reshape(group * BLOCK, dh)
            pp, pc, _ = _attn_probs(qv, k_ref[prev, :], k_ref[cur, :], sink, n, scale)
            o = jnp.dot(pp.astype(BF16), v_ref[prev, :], preferred_element_type=F32)
            o = o + jnp.dot(pc.astype(BF16), v_ref[cur, :], preferred_element_type=F32)
            o_ref[:, cur, :] = o.reshape(group, BLOCK, dh).astype(BF16)
            return carry

        lax.fori_loop(0, S // BLOCK, block, 0)

    return pl.pallas_call(
        body, name=name, grid=(HKV,), in_specs=[heads, kv, kv, smem], out_specs=heads,
        out_shape=jax.ShapeDtypeStruct((HQ, S, dh), BF16), compiler_params=_cparams(("parallel",)),
    )(q, k, v, sinks)


def _attn_bwd(name, q, k, v, sinks, do):
    HQ, S, dh = q.shape
    HKV = k.shape[0]
    group = HQ // HKV
    scale = dh ** -0.5
    heads, kv, smem = _attn_specs(group, S, dh)
    sk = pl.BlockSpec((None, group, LANES), lambda hk: (hk, 0, 0))

    def body(q_ref, k_ref, v_ref, sink_ref, do_ref, dq_ref, dk_ref, dv_ref, ds_ref):
        rows = group * BLOCK
        sink = _sink_col(sink_ref, pl.program_id(0), group)
        dk_ref[...] = jnp.zeros_like(dk_ref)
        dv_ref[...] = jnp.zeros_like(dv_ref)
        tdot = lambda a, b: lax.dot_general(a, b, TN, preferred_element_type=F32)

        def block(n, dsink):
            cur, prev = _block_rows(n)
            qv = q_ref[:, cur, :].reshape(rows, dh)
            dov = do_ref[:, cur, :].reshape(rows, dh)
            kp, kc, vp, vc = k_ref[prev, :], k_ref[cur, :], v_ref[prev, :], v_ref[cur, :]
            pp, pc, ps = _attn_probs(qv, kp, kc, sink, n, scale)
            dpp = lax.dot_general(dov, vp, NT, preferred_element_type=F32)
            dpc = lax.dot_general(dov, vc, NT, preferred_element_type=F32)
            delta = jnp.sum(pp * dpp, axis=-1, keepdims=True) + jnp.sum(pc * dpc, axis=-1, keepdims=True)
            dsp = (pp * (dpp - delta) * scale).astype(BF16)
            dsc = (pc * (dpc - delta) * scale).astype(BF16)
            dq = jnp.dot(dsp, kp, preferred_element_type=F32) + jnp.dot(dsc, kc, preferred_element_type=F32)
            dq_ref[:, cur, :] = dq.reshape(group, BLOCK, dh)
            dk_ref[prev, :] += tdot(dsp, qv)
            dv_ref[prev, :] += tdot(pp.astype(BF16), dov)
            dk_ref[cur, :] += tdot(dsc, qv)
            dv_ref[cur, :] += tdot(pc.astype(BF16), dov)
            return dsink - jnp.sum((ps * delta).reshape(group, BLOCK, 1), axis=1)

        dsink = lax.fori_loop(0, S // BLOCK, block, jnp.zeros((group, 1), F32))
        ds_ref[...] = jnp.broadcast_to(dsink, (group, LANES))

    return pl.pallas_call(
        body, name=name, grid=(HKV,), in_specs=[heads, kv, kv, smem, heads], out_specs=(heads, kv, kv, sk),
        out_shape=(jax.ShapeDtypeStruct((HQ, S, dh), F32), jax.ShapeDtypeStruct((HKV, S, dh), F32),
                   jax.ShapeDtypeStruct((HKV, S, dh), F32), jax.ShapeDtypeStruct((HKV, group, LANES), F32)),
        compiler_params=_cparams(("parallel",)),
    )(q, k, v, sinks, do)


def _gate_specs(S, D, ga_off, gb_off):
    tg = LANES
    for t in range(LANES, 513, LANES):
        if D % t == 0 and ga_off % t == 0 and gb_off % t == 0:
            tg = t
    if D % LANES:
        tg = math.gcd(math.gcd(D, ga_off), gb_off)
    tm = _tile(S, 512, 8)
    act = pl.BlockSpec((tm, tg), lambda i, j: (i, j))
    ga = pl.BlockSpec((tm, tg), lambda i, j: (i, ga_off // tg + j))
    gb = pl.BlockSpec((tm, tg), lambda i, j: (i, gb_off // tg + j))
    return (S // tm, D // tg), act, ga, gb


def _gate_fwd(name, proj, ya, yb, ga_off, gb_off):
    S, D = ya.shape
    grid, act, ga, gb = _gate_specs(S, D, ga_off, gb_off)

    def body(ga_ref, gb_ref, ya_ref, yb_ref, o_ref):
        f = lambda r: r[...].astype(F32)
        o_ref[...] = (_sigmoid(f(ga_ref)) * f(ya_ref) + _sigmoid(f(gb_ref)) * f(yb_ref)).astype(BF16)

    return pl.pallas_call(
        body, name=name, grid=grid, in_specs=[ga, gb, act, act], out_specs=act,
        out_shape=jax.ShapeDtypeStruct((S, D), BF16), compiler_params=_cparams(("parallel", "parallel")),
    )(proj, proj, ya, yb)


def _gate_bwd(name, proj, ya, yb, dm, ga_off, gb_off):
    S, D = ya.shape
    grid, act, ga, gb = _gate_specs(S, D, ga_off, gb_off)

    def body(ga_ref, gb_ref, ya_ref, yb_ref, dm_ref, dga_ref, dgb_ref, dya_ref, dyb_ref):
        dmv = dm_ref[...]
        f = lambda r: r[...].astype(F32)
        sa, sb = _sigmoid(f(ga_ref)), _sigmoid(f(gb_ref))
        dga_ref[...] = (dmv * f(ya_ref) * sa * (1.0 - sa)).astype(BF16)
        dgb_ref[...] = (dmv * f(yb_ref) * sb * (1.0 - sb)).astype(BF16)
        dya_ref[...] = (dmv * sa).astype(BF16)
        dyb_ref[...] = (dmv * sb).astype(BF16)

    o = jax.ShapeDtypeStruct((S, D), BF16)
    return pl.pallas_call(
        body, name=name, grid=grid, in_specs=[ga, gb, act, act, act], out_specs=(act, act, act, act),
        out_shape=(o, o, o, o), compiler_params=_cparams(("parallel", "parallel")),
    )(proj, proj, ya, yb, dm)


ANY = pl.BlockSpec(memory_space=pl.ANY)


def _row_tile(rows, cols, n_arrays):
    want = max(16, (VMEM_LIMIT_V7X // 2) // (2 * n_arrays * cols * 4))
    return _tile(rows, want, 16)


def _cast_to_slot(name, w, dtype, p_arr, dep=None):
    R, C = w.shape
    tr = _row_tile(R, C, 2)
    extra = () if dep is None else (dep,)

    def body(p_ref, w_ref, *rest):
        rest[-1][...] = w_ref[...].astype(dtype)

    return pl.pallas_call(
        body, name=name,
        grid_spec=pltpu.PrefetchScalarGridSpec(
            num_scalar_prefetch=1, grid=(R // tr,),
            in_specs=[pl.BlockSpec((tr, C), lambda i, p_ref: (i, 0))] + [pl.BlockSpec(d.shape, lambda i, p_ref: (0, 0)) for d in extra],
            out_specs=pl.BlockSpec((None, tr, C), lambda i, p_ref: (p_ref[0], i, 0))),
        out_shape=jax.ShapeDtypeStruct((N_CHIPS, R, C), dtype), compiler_params=_cparams(("parallel",)),
    )(p_arr, w, *extra)


def _add_half(name, g3, r3, c_arr):
    n, h, C = r3.shape
    tr = _row_tile(h, C, 3)
    nb = h // tr

    def body(c_ref, g_ref, r_ref, o_ref):
        o_ref[...] = (g_ref[...].astype(F32) + r_ref[...].astype(F32)).astype(BF16)

    blk = pl.BlockSpec((None, tr, C), lambda s, i, c_ref: (s, i, 0))
    return pl.pallas_call(
        body, name=name,
        grid_spec=pltpu.PrefetchScalarGridSpec(
            num_scalar_prefetch=1, grid=(n, nb),
            in_specs=[pl.BlockSpec((None, tr, C), lambda s, i, c_ref: (s, c_ref[0] * nb + i, 0)), blk], out_specs=blk),
        out_shape=jax.ShapeDtypeStruct(r3.shape, BF16), compiler_params=_cparams(("parallel", "parallel")),
    )(c_arr, g3, r3)


def _add_chips(name, t3, r3, cp_arr):
    n, h, C = r3.shape
    tr = _row_tile(h, C, 6)
    nb = h // tr

    def body(cp_ref, t_ref, r0_ref, r1_ref, r2_ref, r3_ref, o_ref):
        p = cp_ref[1]
        total = None
        for a, r_ref in enumerate((r0_ref, r1_ref, r2_ref, r3_ref)):
            part = jnp.where(p == a, t_ref[...], r_ref[...]).astype(F32)
            total = part if total is None else total + part
        o_ref[...] = total

    def part(a):
        return pl.BlockSpec((None, tr, C), lambda i, cp_ref: (jnp.where(cp_ref[1] == a, (a + 1) % N_CHIPS, a), i, 0))

    return pl.pallas_call(
        body, name=name,
        grid_spec=pltpu.PrefetchScalarGridSpec(
            num_scalar_prefetch=1, grid=(nb,),
            in_specs=[pl.BlockSpec((None, tr, C), lambda i, cp_ref: (cp_ref[1], i, 0)), part(0), part(1), part(2), part(3)],
            out_specs=pl.BlockSpec((tr, C), lambda i, cp_ref: (cp_ref[0] * nb + i, 0))),
        out_shape=jax.ShapeDtypeStruct((2 * h, C), F32), compiler_params=_cparams(("parallel",)),
    )(cp_arr, t3, r3, r3, r3, r3)


def _adamw(name, w, g, m, v, deps=()):
    R, C = w.shape
    extra = tuple(deps)
    tr = _row_tile(R, C, 8)
    c1 = 1.0 - ADAM_B1 ** ADAM_STEP
    c2 = 1.0 - ADAM_B2 ** ADAM_STEP

    def body(w_ref, g_ref, m_ref, v_ref, *rest):
        go_ref, d_ref, nm_ref, nv_ref = rest[-4:]
        gv = g_ref[...]
        go_ref[...] = gv
        nm = ADAM_B1 * m_ref[...] + (1.0 - ADAM_B1) * gv
        nv = ADAM_B2 * v_ref[...] + (1.0 - ADAM_B2) * (gv * gv)
        d_ref[...] = -ADAM_LR * ((nm / c1) / (jnp.sqrt(nv / c2) + ADAM_EPS) + ADAM_WD * w_ref[...])
        nm_ref[...] = nm
        nv_ref[...] = nv

    blk = pl.BlockSpec((tr, C), lambda i: (i, 0))
    o = jax.ShapeDtypeStruct((R, C), F32)
    return pl.pallas_call(
        body, name=name, grid=(R // tr,), in_specs=[blk, blk, blk, blk] + [ANY] * len(extra), out_specs=(blk, blk, blk, blk),
        out_shape=(o, o, o, o), compiler_params=_cparams(("parallel",)),
    )(w, g, m, v, *extra)


def _place():
    x, y, c = lax.axis_index("x"), lax.axis_index("y"), lax.axis_index("c")
    chips = [(1 - x, y), (x, 1 - y), (1 - x, 1 - y)]
    return x, y, c, 2 * x + y, chips


HBM = pl.BlockSpec(memory_space=pltpu.HBM)
SEM = pl.BlockSpec(memory_space=pltpu.SEMAPHORE)
TOKEN = jax.ShapeDtypeStruct((8, LANES), F32)
DATAFLOW = pltpu.SideEffectType.DATAFLOW_SIDE_EFFECTING


def _hbm(a):
    return pltpu.with_memory_space_constraint(a, pltpu.HBM)


def _gather_blocks(bufs, i, c, p, chips):
    if bufs[i].shape[1] % 16:
        return bufs[i].at[p], [bufs[i].at[2 * cx + cy] for cx, cy in chips]
    h = bufs[i].shape[1] // 2
    rows = pl.ds(pl.multiple_of(c * h, 16), h)
    return bufs[i].at[p, rows], [bufs[i].at[2 * cx + cy, rows] for cx, cy in chips]


def _gather_start(name, groups, dep):
    slots = [s for g in groups for s in g]
    n, ng = len(slots), len(groups)

    def body(*refs):
        bufs, sems, token = refs[:n], refs[n + 1:n + 1 + 2 * ng], refs[-1]
        x, y, c, p, chips = _place()
        i = 0
        for gi, g in enumerate(groups):
            send, recv = sems[2 * gi], sems[2 * gi + 1]
            for k in range(len(g)):
                mine, _ = _gather_blocks(bufs, i, c, p, chips)
                for j, chip in enumerate(chips):
                    pltpu.make_async_remote_copy(src_ref=mine, dst_ref=mine, send_sem=send.at[3 * k + j], recv_sem=recv.at[3 * k + j],
                                                 device_id=(*chip, c), device_id_type=MESH).start()
                i += 1
        token[...] = jnp.zeros_like(token)

    sem_shapes = [pltpu.SemaphoreType.DMA((3 * len(g),)) for g in groups for _ in range(2)]
    out = pl.pallas_call(
        body, name=name, in_specs=[HBM] * n + [ANY],
        out_specs=(*([SEM] * (2 * ng)), *([HBM] * n), pl.BlockSpec(memory_space=pltpu.VMEM)),
        out_shape=(*sem_shapes, *[pltpu.HBM(s.shape, s.dtype) for s in slots], TOKEN),
        input_output_aliases={i: 2 * ng + i for i in range(n)},
        compiler_params=pltpu.CompilerParams(has_side_effects=DATAFLOW),
    )(*[_hbm(s) for s in slots], dep)
    started, i = [], 2 * ng
    for gi, g in enumerate(groups):
        started.append((out[2 * gi], out[2 * gi + 1], list(out[i:i + len(g)])))
        i += len(g)
    return started, out[-1]


def _gather_wait(name, send, recv, slots, after):
    n = len(slots)

    def body(*refs):
        bufs, send, recv = refs[:n], refs[n], refs[n + 1]
        x, y, c, p, chips = _place()
        for i in range(n):
            mine, landed = _gather_blocks(bufs, i, c, p, chips)
            for j, chip in enumerate(chips):
                cp = pltpu.make_async_remote_copy(src_ref=mine, dst_ref=landed[j], send_sem=send.at[3 * i + j],
                                                  recv_sem=recv.at[3 * i + j], device_id=(*chip, c), device_id_type=MESH)
                cp.wait_send()
                cp.wait_recv()

    return list(pl.pallas_call(
        body, name=name, in_specs=[HBM] * n + [SEM, SEM, ANY], out_specs=tuple([HBM] * n),
        out_shape=tuple(pltpu.HBM(s.shape, s.dtype) for s in slots),
        input_output_aliases={i: i for i in range(n)},
        compiler_params=pltpu.CompilerParams(has_side_effects=DATAFLOW),
    )(*slots, send, recv, after))


def _gather_forward(name, slots):
    idx = [i for i, s in enumerate(slots) if s.shape[1] % 16 == 0]
    n = len(slots)

    def body(*refs):
        bufs = refs[n:2 * n]
        send, recv = refs[2 * n:]
        x, y, c, p, chips = _place()

        def rdma(k, ref):
            return pltpu.make_async_remote_copy(src_ref=ref, dst_ref=ref, send_sem=send.at[k], recv_sem=recv.at[k],
                                                device_id=(x, y, 1 - c), device_id_type=MESH)

        cps = []
        for k, i in enumerate(idx):
            for j, ref in enumerate(_gather_blocks(bufs, i, c, p, chips)[1]):
                cps.append(rdma(3 * k + j, ref))
                cps[-1].start()
        for k, i in enumerate(idx):
            for j, ref in enumerate(_gather_blocks(bufs, i, 1 - c, p, chips)[1]):
                rdma(3 * k + j, ref).wait_recv()
        for cp in cps:
            cp.wait_send()

    return list(pl.pallas_call(
        body, name=name, in_specs=[ANY] * n, out_specs=tuple([ANY] * n),
        out_shape=tuple(jax.ShapeDtypeStruct(s.shape, s.dtype) for s in slots),
        scratch_shapes=[pltpu.SemaphoreType.DMA((3 * len(idx),)), pltpu.SemaphoreType.DMA((3 * len(idx),))],
        input_output_aliases={i: i for i in range(n)},
        compiler_params=pltpu.CompilerParams(has_side_effects=True),
    )(*slots))


def _swap_copy(grads, lands, send, recv, i, x, y, c):
    h = grads[i].shape[1] // 2
    other = pl.ds(pl.multiple_of((1 - c) * h, 16), h)
    return pltpu.make_async_remote_copy(src_ref=grads[i].at[:, other, :], dst_ref=lands[i], send_sem=send.at[i],
                                        recv_sem=recv.at[i], device_id=(x, y, 1 - c), device_id_type=MESH)


def _swap_wait(name, send, recv, grads, lands, after):
    n = len(grads)

    def body(*refs):
        ins, lands, send, recv = refs[:n], refs[n:2 * n], refs[2 * n], refs[2 * n + 1]
        x, y, c, p, chips = _place()
        for i in range(n):
            cp = _swap_copy(ins, lands, send, recv, i, x, y, c)
            cp.wait_send()
            cp.wait_recv()

    shapes = [pltpu.HBM(t.shape, t.dtype) for t in list(grads) + list(lands)]
    out = pl.pallas_call(
        body, name=name, in_specs=[HBM] * (2 * n) + [SEM, SEM, ANY], out_specs=tuple([HBM] * (2 * n)),
        out_shape=tuple(shapes), input_output_aliases={i: i for i in range(2 * n)},
        compiler_params=pltpu.CompilerParams(has_side_effects=DATAFLOW),
    )(*grads, *lands, send, recv, after)
    return list(out[:n]), list(out[n:])


def _reduce_starts(name, grads, parts):
    ng, npt = len(grads), len(parts)
    halves = [(g.shape[0], g.shape[1] // 2, g.shape[2]) for g in grads]
    arrays = list(grads) + [lax.empty(s, g.dtype) for s, g in zip(halves, grads)] + list(parts) + [lax.empty(t.shape, t.dtype) for t in parts]
    na = len(arrays)
    sems = ([pltpu.SemaphoreType.DMA((ng,))] * 2 if ng else []) + ([pltpu.SemaphoreType.DMA((3 * npt,))] * 2 if npt else [])
    ns = len(sems)

    def body(*refs):
        ins, sem, token = refs[:na], list(refs[na:na + ns]), refs[-1]
        x, y, c, p, chips = _place()
        if ng:
            for i in range(ng):
                _swap_copy(ins[:ng], ins[ng:2 * ng], sem[0], sem[1], i, x, y, c).start()
        if npt:
            src, land, send, recv = ins[2 * ng:2 * ng + npt], ins[2 * ng + npt:], sem[-2], sem[-1]
            for i in range(npt):
                for j, (cx, cy) in enumerate(chips):
                    pltpu.make_async_remote_copy(src_ref=src[i].at[2 * cx + cy], dst_ref=land[i].at[p], send_sem=send.at[3 * i + j],
                                                 recv_sem=recv.at[3 * i + j], device_id=(cx, cy, c), device_id_type=MESH).start()
        token[...] = jnp.zeros_like(token)

    out = pl.pallas_call(
        body, name=name, in_specs=[HBM] * na,
        out_specs=(*([SEM] * ns), *([HBM] * na), pl.BlockSpec(memory_space=pltpu.VMEM)),
        out_shape=(*sems, *[pltpu.HBM(a.shape, a.dtype) for a in arrays], TOKEN),
        input_output_aliases={i: ns + i for i in range(na)},
        compiler_params=pltpu.CompilerParams(has_side_effects=DATAFLOW),
    )(*[_hbm(a) for a in arrays])
    bufs = list(out[ns:ns + na])
    swap = (out[0], out[1], bufs[:ng], bufs[ng:2 * ng]) if ng else None
    exch = (out[ns - 2], out[ns - 1], bufs[2 * ng:2 * ng + npt], bufs[2 * ng + npt:]) if npt else None
    return swap, exch, out[-1]


def _exchange_wait(name, send, recv, parts, lands, after):
    n = len(parts)

    def body(*refs):
        ins, lands, send, recv = refs[:n], refs[n:2 * n], refs[2 * n], refs[2 * n + 1]
        x, y, c, p, chips = _place()
        for i in range(n):
            for j, (cx, cy) in enumerate(chips):
                q = 2 * cx + cy
                cp = pltpu.make_async_remote_copy(src_ref=ins[i].at[q], dst_ref=lands[i].at[q], send_sem=send.at[3 * i + j],
                                                  recv_sem=recv.at[3 * i + j], device_id=(cx, cy, c), device_id_type=MESH)
                cp.wait_send()
                cp.wait_recv()

    shapes = [pltpu.HBM(t.shape, t.dtype) for t in parts]
    out = pl.pallas_call(
        body, name=name, in_specs=[HBM] * (2 * n) + [SEM, SEM, ANY], out_specs=tuple([HBM] * (2 * n)),
        out_shape=(*shapes, *shapes), input_output_aliases={i: i for i in range(2 * n)},
        compiler_params=pltpu.CompilerParams(has_side_effects=DATAFLOW),
    )(*parts, *lands, send, recv, after)
    return list(out[:n]), list(out[n:])


def _join_copy(buf, send_sem, recv_sem, which, x, y, c):
    h = buf.shape[0] // 2
    rows = buf.at[pl.ds(pl.multiple_of(which * h, 8), h)]
    return pltpu.make_async_remote_copy(src_ref=rows, dst_ref=rows, send_sem=send_sem, recv_sem=recv_sem,
                                        device_id=(x, y, 1 - c), device_id_type=MESH)


def _join_start(name, groups):
    bufs = [b for g in groups for b in g]
    n, ng = len(bufs), len(groups)

    def body(*refs):
        ins, sems, token = refs[:n], refs[n:n + 2 * ng], refs[-1]
        x, y, c, p, chips = _place()
        i = 0
        for gi, g in enumerate(groups):
            for k in range(len(g)):
                _join_copy(ins[i], sems[2 * gi].at[k], sems[2 * gi + 1].at[k], c, x, y, c).start()
                i += 1
        token[...] = jnp.zeros_like(token)

    sem_shapes = [pltpu.SemaphoreType.DMA((len(g),)) for g in groups for _ in range(2)]
    out = pl.pallas_call(
        body, name=name, in_specs=[HBM] * n,
        out_specs=(*([SEM] * (2 * ng)), *([HBM] * n), pl.BlockSpec(memory_space=pltpu.VMEM)),
        out_shape=(*sem_shapes, *[pltpu.HBM(t.shape, t.dtype) for t in bufs], TOKEN),
        input_output_aliases={i: 2 * ng + i for i in range(n)},
        compiler_params=pltpu.CompilerParams(has_side_effects=DATAFLOW),
    )(*[_hbm(t) for t in bufs])
    started, i = [], 2 * ng
    for gi, g in enumerate(groups):
        started.append((out[2 * gi], out[2 * gi + 1], list(out[i:i + len(g)])))
        i += len(g)
    return started, out[-1]


def _join_wait(name, send, recv, bufs, after):
    n = len(bufs)

    def body(*refs):
        ins, send, recv = refs[:n], refs[n], refs[n + 1]
        x, y, c, p, chips = _place()
        for i in range(n):
            _join_copy(ins[i], send.at[i], recv.at[i], c, x, y, c).wait_send()
            _join_copy(ins[i], send.at[i], recv.at[i], 1 - c, x, y, c).wait_recv()

    return list(pl.pallas_call(
        body, name=name, in_specs=[HBM] * n + [SEM, SEM, ANY], out_specs=tuple([HBM] * n),
        out_shape=tuple(pltpu.HBM(t.shape, t.dtype) for t in bufs), input_output_aliases={i: i for i in range(n)},
        compiler_params=pltpu.CompilerParams(has_side_effects=DATAFLOW),
    )(*bufs, send, recv, after))


def _allreduce_small(name, pack, dep):
    R, W = pack.shape

    def body(in_ref, dep_ref, out_ref, slots, send, recv):
        x, y, c = lax.axis_index("x"), lax.axis_index("y"), lax.axis_index("c")
        me = 4 * x + 2 * y + c
        slots[0] = in_ref[...]
        cps = []
        for k in range(1, N_DEV):
            peer = (x ^ (k >> 2), y ^ ((k >> 1) & 1), c ^ (k & 1))
            cp = pltpu.make_async_remote_copy(src_ref=in_ref, dst_ref=slots.at[k], send_sem=send.at[k - 1],
                                              recv_sem=recv.at[k - 1], device_id=peer, device_id_type=MESH)
            cp.start()
            cps.append(cp)
        for cp in cps:
            cp.wait()
        total = slots[me]
        for a in range(1, N_DEV):
            total = total + slots[jnp.bitwise_xor(a, me)]
        out_ref[...] = total

    vmem = pl.BlockSpec(memory_space=pltpu.VMEM)
    return pl.pallas_call(
        body, name=name, in_specs=[vmem, ANY], out_specs=vmem, out_shape=jax.ShapeDtypeStruct((R, W), F32),
        scratch_shapes=[pltpu.VMEM((N_DEV, R, W), F32), pltpu.SemaphoreType.DMA((N_DEV - 1,)), pltpu.SemaphoreType.DMA((N_DEV - 1,))],
        compiler_params=pltpu.CompilerParams(has_side_effects=True),
    )(pack, dep)


def _heads(a, n_heads):
    S = a.shape[0]
    return a.reshape(S, n_heads, a.shape[1] // n_heads).transpose(1, 0, 2)


def _unheads(a):
    H, S, dh = a.shape
    return a.transpose(1, 0, 2).reshape(S, H * dh)


def _ffn_bwd(tag, xin, gain, wgu3, wd, saved, dxout, dxo_b, reduce_start, dep, flush=None):
    h, gu, act = saved
    D = xin.shape[1]
    tok = reduce_start({f"w_down{tag}": _mm_tn(f"dw_down_{tag}", act, dxo_b, 0.5, dep=dep).reshape(N_CHIPS, -1, D)})
    dgu = _ffn_down_bwd(f"ffn_down_bwd_{tag}", dxo_b, wd, gu, 0.5, dep=tok)
    tok = reduce_start({f"w_gu{tag}": _mm_tn_cols(f"dw_gu_{tag}", h, dgu, wgu3.shape[2], b_is_gu=True)})
    if flush is not None:
        tok = flush(tok)
    dh = _mm_nt_cols(f"ffn_up_bwd_{tag}", dgu, wgu3, a_is_gu=True, dep=tok)
    dxin, dxin_b, dgain = _rms_bwd(f"rms_bwd_{tag}", xin, gain, dh, dxout)
    return dxin, dxin_b, dgain, tok


def kernel(x, g_ffn1, w_gu1, w_down1, g_mix, w_in, conv_w, q_norm_g, k_norm_g, sinks, w_out_conv, w_out_attn, w_o, g_ffn2, w_gu2, w_down2, loss_target, m_g_ffn1, m_w_gu1, m_w_down1, m_g_mix, m_w_in, m_conv_w, m_q_norm_g, m_k_norm_g, m_sinks, m_w_out_conv, m_w_out_attn, m_w_o, m_g_ffn2, m_w_gu2, m_w_down2, v_g_ffn1, v_w_gu1, v_w_down1, v_g_mix, v_w_in, v_conv_w, v_q_norm_g, v_k_norm_g, v_sinks, v_w_out_conv, v_w_out_attn, v_w_o, v_g_ffn2, v_w_gu2, v_w_down2):
    S, D = x.shape[1], x.shape[2]
    dh = q_norm_g.shape[1]
    HQ = sinks.shape[1]
    HKV = HQ // 4
    AW, KVW, CW = HQ * dh, HKV * dh, D // 2
    off_q, off_k, off_v = 3 * CW, 3 * CW + AW, 3 * CW + AW + KVW
    off_ga, off_gb = off_v + KVW, off_v + KVW + D
    x0, target = x[0], loss_target[0]
    cx, cy, cc = lax.axis_index("x"), lax.axis_index("y"), lax.axis_index("c")
    chip = 2 * cx + cy
    p_arr = jnp.reshape(chip, (1,)).astype(jnp.int32)
    c_arr = jnp.reshape(cc, (1,)).astype(jnp.int32)
    cp_arr = jnp.stack([cc, chip]).astype(jnp.int32)
    wts = dict(g_ffn1=g_ffn1, w_gu1=w_gu1, w_down1=w_down1, g_mix=g_mix, w_in=w_in, conv_w=conv_w, q_norm_g=q_norm_g,
               k_norm_g=k_norm_g, sinks=sinks, w_out_conv=w_out_conv, w_out_attn=w_out_attn, w_o=w_o, g_ffn2=g_ffn2,
               w_gu2=w_gu2, w_down2=w_down2)
    ms = dict(g_ffn1=m_g_ffn1, w_gu1=m_w_gu1, w_down1=m_w_down1, g_mix=m_g_mix, w_in=m_w_in, conv_w=m_conv_w,
              q_norm_g=m_q_norm_g, k_norm_g=m_k_norm_g, sinks=m_sinks, w_out_conv=m_w_out_conv, w_out_attn=m_w_out_attn,
              w_o=m_w_o, g_ffn2=m_g_ffn2, w_gu2=m_w_gu2, w_down2=m_w_down2)
    vs = dict(g_ffn1=v_g_ffn1, w_gu1=v_w_gu1, w_down1=v_w_down1, g_mix=v_g_mix, w_in=v_w_in, conv_w=v_conv_w,
              q_norm_g=v_q_norm_g, k_norm_g=v_k_norm_g, sinks=v_sinks, w_out_conv=v_w_out_conv, w_out_attn=v_w_out_attn,
              w_o=v_w_o, g_ffn2=v_g_ffn2, w_gu2=v_w_gu2, w_down2=v_w_down2)
    order = list(wts)
    small_names = [k for k in order if not k.startswith("w_")]
    grad, delta, new_m, new_v = {}, {}, {}, {}

    def cast(keys, dep=None):
        return [_cast_to_slot(f"cast_{k}", wts[k][0], F32 if k == "conv_w" else BF16, p_arr, dep) for k in keys]

    def gather_finish(tag, started, after):
        send, recv, slots = started
        return _gather_forward(f"gather_forward_{tag}", _gather_wait(f"gather_wait_{tag}", send, recv, slots, after))

    swapping, pending = [], []

    def reduce_start(full, after=None):
        keys = [] if full is None else list(full)
        pkeys, parts = [], []
        if swapping:
            pkeys, send, recv, gs, lands = swapping.pop(0)
            gs, sib = _swap_wait(f"swap_wait_{pkeys[0]}", send, recv, gs, lands, after if full is None else full[keys[0]])
            parts = [_add_half(f"add_half_{k}", g, r, c_arr) for k, g, r in zip(pkeys, gs, sib)]
        swap, exch, tok = _reduce_starts(f"reduce_starts_{keys[0] if keys else 'last'}", [full[k] for k in keys], parts)
        if exch:
            pending.append((pkeys, *exch))
        if swap:
            swapping.append((keys, *swap))
        return tok

    def reduce_finish(entries, after):
        ready = []
        for keys, send, recv, parts, lands in entries:
            parts, lands = _exchange_wait(f"exchange_wait_{keys[0]}", send, recv, parts, lands, after)
            ready.append((keys, [_add_chips(f"add_chips_{k}", t, r, cp_arr) for k, t, r in zip(keys, parts, lands)]))
        started, last = _join_start(f"join_start_{ready[0][0][0]}", [halves for _, halves in ready])
        for (keys, _), (send, recv, halves) in zip(ready, started):
            for k, g2 in zip(keys, _join_wait(f"join_wait_{keys[0]}", send, recv, halves, last)):
                g2, d, nm, nv = _adamw(f"adamw_{k}", wts[k][0], g2, ms[k][0], vs[k][0], (last,))
                grad[k], delta[k], new_m[k], new_v[k] = g2[None], d[None], nm[None], nv[None]
                last = nv
        return last

    (st_gu1, st_d1), tok = _gather_start("gather_start_1", [cast(["w_gu1"]), cast(["w_down1"])], x0)
    later = ["w_in", "conv_w", "w_out_conv", "w_out_attn", "w_o", "w_gu2", "w_down2"]
    slot = dict(zip(later, cast(later, tok)))
    h1 = _rms_fwd("rms_fwd_1", x0, g_ffn1, slot["w_down2"])
    wgu1, = gather_finish("gu1", st_gu1, h1)
    (st_in, st_out, st_gu2, st_d2), tok = _gather_start(
        "gather_start_2", [[slot["w_in"], slot["conv_w"]], [slot["w_out_conv"], slot["w_out_attn"], slot["w_o"]],
                           [slot["w_gu2"]], [slot["w_down2"]]], wgu1)
    cos, sin, rm, rmt = _rope_consts(S, dh)
    sink_vec = sinks[0]

    gu1, act1 = _ffn_up("ffn_up_1", h1, wgu1, tok)
    wd1 = gather_finish("d1", st_d1, act1)[0].reshape(-1, D)
    x1 = _mm_res("ffn_down_1", act1, wd1, x0, 0.5)
    win3, convw3 = gather_finish("in", st_in, x1)
    h2 = _rms_fwd("rms_fwd_mix", x1, g_mix)
    proj = _mm_cols("in_proj", h2, win3, BF16)
    aconv = _conv_fwd("conv_fwd", proj, convw3, CW)
    woc3, woa3, wo = gather_finish("out", st_out, aconv)
    wo = wo.reshape(-1, D)
    ya = _mm_cols("out_conv", aconv, woc3, BF16)
    q_raw = _heads(proj[:, off_q:off_q + AW], HQ)
    k_raw = _heads(proj[:, off_k:off_k + KVW], HKV)
    vh = _heads(proj[:, off_v:off_v + KVW], HKV)
    qn = _qk_prep("q_prep", q_raw, q_norm_g, cos, sin, rm)
    kn = _qk_prep("k_prep", k_raw, k_norm_g, cos, sin, rm)
    oh = _attn_fwd("attn_fwd", qn, kn, vh, sink_vec)
    o = _unheads(oh)
    yb = _mm_cols("out_attn", o, woa3, BF16)
    merged = _gate_fwd("gate_fwd", proj, ya, yb, off_ga, off_gb)
    x2 = _mm_res("mix_out", merged, wo, x1, 1.0)
    wgu2, = gather_finish("gu2", st_gu2, x2)
    h3 = _rms_fwd("rms_fwd_2", x2, g_ffn2)
    gu2, act2 = _ffn_up("ffn_up_2", h3, wgu2)
    wd2 = gather_finish("d2", st_d2, act2)[0].reshape(-1, D)
    x3 = _mm_res("ffn_down_2", act2, wd2, x2, 0.5)

    dy, dy_b, loss_lanes = _loss_grad("loss_grad", x3, target)
    dx2, dx2_b, dg_ffn2, tok = _ffn_bwd("2", x2, g_ffn2, wgu2, wd2, (h3, gu2, act2), dy, dy_b, reduce_start, None)
    dmerged = _mm_nt("mix_out_bwd", dx2_b, wo, F32)
    tok = reduce_start(dict(w_o=_mm_tn("dw_o", merged, dx2_b, dep=tok).reshape(N_CHIPS, -1, D)))
    dga, dgb, dya, dyb = _gate_bwd("gate_bwd", proj, ya, yb, dmerged, off_ga, off_gb)
    daconv = _mm_nt_cols("out_conv_bwd", dya, woc3, dep=tok)
    dwoc = _mm_tn_cols("dw_out_conv", aconv, dya, woc3.shape[2])
    do = _mm_nt_cols("out_attn_bwd", dyb, woa3)
    dwoa = _mm_tn_cols("dw_out_attn", o, dyb, woa3.shape[2])
    tok = reduce_start(dict(w_out_conv=dwoc, w_out_attn=dwoa))
    dxc, dbg, dcg, dconvw = _conv_bwd("conv_bwd", proj, convw3, daconv, CW)
    dqn, dkn, dvh, dsink3 = _attn_bwd("attn_bwd", qn, kn, vh, sink_vec, _heads(do, HQ).astype(BF16))
    dq_raw, dqg = _qk_prep_bwd("q_prep_bwd", q_raw, q_norm_g, cos, sin, rmt, dqn)
    dk_raw, dkg = _qk_prep_bwd("k_prep_bwd", k_raw, k_norm_g, cos, sin, rmt, dkn)
    dproj = jnp.concatenate([dxc, dbg, dcg, _unheads(dq_raw), _unheads(dk_raw), _unheads(dvh).astype(BF16), dga, dgb], axis=1)
    dh2 = _mm_nt_cols("in_proj_bwd", dproj, win3, dep=tok)
    tok = reduce_start(dict(w_in=_mm_tn_cols("dw_in", h2, dproj, win3.shape[2])))
    dx1, dx1_b, dg_mix = _rms_bwd("rms_bwd_mix", x1, g_mix, dh2, dx2)
    dx0, _, dg_ffn1, tok = _ffn_bwd("1", x0, g_ffn1, wgu1, wd1, (h1, gu1, act1), dx1, dx1_b, reduce_start, tok, lambda after: reduce_start(None, after))

    def rows8(a):
        a = a.reshape(-1, a.shape[-1])
        return jnp.pad(a, ((0, -a.shape[0] % 8), (0, D - a.shape[1])))

    misc = jnp.concatenate([dqg, dkg, dsink3[:, :, 0].reshape(1, HQ), loss_lanes], axis=1)
    done = reduce_finish(pending[:-2], dx0)
    tot = _allreduce_small("allreduce_small", jnp.concatenate([rows8(a) for a in (dg_ffn1, dg_mix, dg_ffn2, dconvw, misc)], axis=0), done)
    reduce_finish(pending[-2:], tot)

    cw_s = conv_w.shape[2]
    conv_row0, misc_row = 24, 24 + (-(-N_CHIPS * CONV_K // 8)) * 8
    small_g = dict(g_ffn1=tot[0:1], g_mix=tot[8:9], g_ffn2=tot[16:17],
                   conv_w=lax.dynamic_slice(tot, (conv_row0 + CONV_K * chip, 0), (CONV_K, cw_s)),
                   q_norm_g=tot[misc_row:misc_row + 1, 0:dh], k_norm_g=tot[misc_row:misc_row + 1, dh:2 * dh],
                   sinks=tot[misc_row:misc_row + 1, 2 * dh:2 * dh + HQ])
    loss = (0.5 / D) * jnp.sum(tot[misc_row, 2 * dh + HQ:2 * dh + HQ + LANES])

    def small_pack(src):
        return jnp.concatenate([rows8(src[k]) for k in small_names], axis=0)

    _, sd, sm, sv = _adamw("adamw_small", small_pack(wts), small_pack(small_g), small_pack(ms), small_pack(vs))
    for i, k in enumerate(small_names):
        shape = wts[k].shape
        nr, ncol = math.prod(shape[:-1]), shape[-1]
        grad[k] = small_g[k].reshape(shape)
        delta[k], new_m[k], new_v[k] = (a[8 * i:8 * i + nr, 0:ncol].reshape(shape) for a in (sd, sm, sv))
    return (loss, dx0[None], *[grad[k] for k in order], *[delta[k] for k in order],
            *[new_m[k] for k in order], *[new_v[k] for k in order])
```

```python
import math

import numpy as np
import jax
import jax.numpy as jnp
from jax import lax
from jax.experimental import pallas as pl
from jax.experimental.pallas import tpu as pltpu

F32 = jnp.float32
BF16 = jnp.bfloat16
MESH = pl.DeviceIdType.MESH

RMS_EPS = 1e-6
BLOCK = 128
ROPE_THETA = 500000.0
NEG_INF = -1e30
CONV_K = 3
ADAM_LR, ADAM_B1, ADAM_B2, ADAM_EPS, ADAM_WD, ADAM_STEP = 0.001, 0.9, 0.999, 1e-08, 0.01, 10

VMEM_LIMIT_V7X = 56 * 1024 * 1024
LANES = 128
N_CHIPS = 4
N_DEV = 8


def _tile(n, want, align=LANES):
    best = None
    t = align
    while t <= min(n, want):
        if n % t == 0:
            best = t
        t += align
    return best or n


def _cparams(sem):
    return pltpu.CompilerParams(dimension_semantics=sem, vmem_limit_bytes=VMEM_LIMIT_V7X)


def _sigmoid(x):
    return 1.0 / (1.0 + jnp.exp(-x))


NN = (((1,), (0,)), ((), ()))
NT = (((1,), (1,)), ((), ()))
TN = (((0,), (0,)), ((), ()))


def _mm(name, grid, ins, in_specs, compute, out_shape, out_specs, epilogue, dep=None):
    if dep is not None:
        ins, in_specs = tuple(ins) + (dep,), list(in_specs) + [pl.BlockSpec(dep.shape, lambda *_: (0, 0))]
    n_in = len(ins)

    def body(*refs):
        epilogue(compute(refs[:n_in]), refs[:n_in], refs[n_in:])

    return pl.pallas_call(
        body, name=name, grid=grid, in_specs=in_specs, out_specs=out_specs, out_shape=out_shape,
        compiler_params=_cparams(("parallel", "arbitrary")),
    )(*ins)


def _dot(dims, a=0, b=1):
    return lambda refs: [lax.dot_general(refs[a][...], refs[b][...], dims, preferred_element_type=F32)]


def _ffn_up(name, h, wgu3, dep=None):
    S, D = h.shape
    Ns = wgu3.shape[2]
    F = 2 * Ns
    tm, tn = _tile(S, 512), _tile(Ns, 1408)
    nbs = Ns // tn

    def compute(refs):
        hv = refs[0][...]
        return [jnp.dot(hv, refs[1][...], preferred_element_type=F32), jnp.dot(hv, refs[2][...], preferred_element_type=F32)]

    def epi(accs, in_refs, out_refs):
        g, u = accs
        dgu_ref, a_ref = out_refs
        sg = _sigmoid(g)
        silu = g * sg
        dgu_ref[0] = (u * (sg * (1.0 + g * (1.0 - sg)))).astype(BF16)
        dgu_ref[1] = silu.astype(BF16)
        a_ref[...] = (silu * u).astype(BF16)

    return _mm(
        name, (F // tn, S // tm), (h, wgu3, wgu3),
        [pl.BlockSpec((tm, D), lambda j, i: (i, 0)),
         pl.BlockSpec((None, D, tn), lambda j, i: (j // nbs, 0, j % nbs)),
         pl.BlockSpec((None, D, tn), lambda j, i: (2 + j // nbs, 0, j % nbs))],
        compute, (jax.ShapeDtypeStruct((2, S, F), BF16), jax.ShapeDtypeStruct((S, F), BF16)),
        (pl.BlockSpec((2, tm, tn), lambda j, i: (0, i, j)), pl.BlockSpec((tm, tn), lambda j, i: (i, j))), epi, dep=dep)


def _mm_res(name, a, w, res, scale):
    S, K = a.shape
    N = w.shape[1]
    tm, tn = _tile(S, 512), _tile(N, 512 if K > 2816 else 1024)

    def epi(accs, in_refs, out_refs):
        out_refs[0][...] = in_refs[2][...] + scale * accs[0]

    return _mm(
        name, (N // tn, S // tm), (a, w, res),
        [pl.BlockSpec((tm, K), lambda j, i: (i, 0)), pl.BlockSpec((K, tn), lambda j, i: (0, j)),
         pl.BlockSpec((tm, tn), lambda j, i: (i, j))],
        _dot(NN), jax.ShapeDtypeStruct((S, N), F32), pl.BlockSpec((tm, tn), lambda j, i: (i, j)), epi)


def _mm_cols(name, a, w3, out_dtype):
    S, K = a.shape
    Ns = w3.shape[2]
    tm, tn = _tile(S, 512), _tile(Ns, 2304)
    nbs = Ns // tn

    def epi(accs, in_refs, out_refs):
        out_refs[0][...] = accs[0].astype(out_dtype)

    return _mm(
        name, (N_CHIPS * nbs, S // tm), (a, w3),
        [pl.BlockSpec((tm, K), lambda j, i: (i, 0)),
         pl.BlockSpec((None, K, tn), lambda j, i: (j // nbs, 0, j % nbs))],
        _dot(NN), jax.ShapeDtypeStruct((S, N_CHIPS * Ns), out_dtype), pl.BlockSpec((tm, tn), lambda j, i: (i, j)), epi)


def _ffn_down_bwd(name, dy, wd, gu, scale, dep=None):
    S, D = dy.shape
    F = wd.shape[0]
    tm, tn = _tile(S, 512), _tile(F, 1408)

    def epi(accs, in_refs, out_refs):
        da = scale * accs[0]
        out_refs[0][0] = (da * in_refs[2][0].astype(F32)).astype(BF16)
        out_refs[0][1] = (da * in_refs[2][1].astype(F32)).astype(BF16)

    return _mm(
        name, (F // tn, S // tm), (dy, wd, gu),
        [pl.BlockSpec((tm, D), lambda j, i: (i, 0)), pl.BlockSpec((tn, D), lambda j, i: (j, 0)),
         pl.BlockSpec((2, tm, tn), lambda j, i: (0, i, j))],
        _dot(NT), jax.ShapeDtypeStruct((2, S, F), BF16), pl.BlockSpec((2, tm, tn), lambda j, i: (0, i, j)), epi, dep=dep)


def _mm_nt_cols(name, a, w3, a_is_gu=False, dep=None):
    K, Ns = w3.shape[1], w3.shape[2]
    S = a.shape[1] if a_is_gu else a.shape[0]
    tm = _tile(S, 512)
    tn = _tile(K, max(LANES, (6 << 20) // (N_CHIPS * Ns * 2)))
    if a_is_gu:
        a_spec = pl.BlockSpec((2, tm, 2 * Ns), lambda i, j: (0, i, 0))
        part = lambda a_ref, s: a_ref[s // 2, :, (s % 2) * Ns:(s % 2 + 1) * Ns]
    else:
        a_spec = pl.BlockSpec((tm, N_CHIPS * Ns), lambda i, j: (i, 0))
        part = lambda a_ref, s: a_ref[:, s * Ns:(s + 1) * Ns]

    def compute(refs):
        total = None
        for s in range(N_CHIPS):
            prod = lax.dot_general(part(refs[0], s), refs[1][s], NT, preferred_element_type=F32)
            total = prod if total is None else total + prod
        return [total]

    def epi(accs, in_refs, out_refs):
        out_refs[0][...] = accs[0]

    return _mm(
        name, (S // tm, K // tn), (a, w3), [a_spec, pl.BlockSpec((N_CHIPS, tn, Ns), lambda i, j: (0, j, 0))],
        compute, jax.ShapeDtypeStruct((S, K), F32), pl.BlockSpec((tm, tn), lambda i, j: (i, j)), epi, dep=dep)


def _mm_tn(name, a, b, scale=1.0, dep=None):
    S, K = a.shape
    N = b.shape[1]
    tm, tn = _tile(K, 512), _tile(N, 1024)

    def epi(accs, in_refs, out_refs):
        out_refs[0][...] = (scale * accs[0]).astype(BF16)

    return _mm(
        name, (N // tn, K // tm), (a, b),
        [pl.BlockSpec((S, tm), lambda j, i: (0, i)), pl.BlockSpec((S, tn), lambda j, i: (0, j))],
        _dot(TN), jax.ShapeDtypeStruct((K, N), BF16), pl.BlockSpec((tm, tn), lambda j, i: (i, j)), epi, dep=dep)


def _mm_tn_cols(name, a, b, Ns, b_is_gu=False, dep=None):
    S, K = a.shape
    tm, tn = _tile(K, 512), _tile(Ns, 2304)
    nbs = Ns // tn
    if b_is_gu:
        b_spec = pl.BlockSpec((None, S, tn), lambda j, i: (j // (2 * nbs), 0, j % (2 * nbs)))
    else:
        b_spec = pl.BlockSpec((S, tn), lambda j, i: (0, j))

    def epi(accs, in_refs, out_refs):
        out_refs[0][...] = accs[0].astype(BF16)

    return _mm(
        name, (N_CHIPS * nbs, K // tm), (a, b), [pl.BlockSpec((S, tm), lambda j, i: (0, i)), b_spec],
        _dot(TN), jax.ShapeDtypeStruct((N_CHIPS, K, Ns), BF16),
        pl.BlockSpec((None, tm, tn), lambda j, i: (j // nbs, i, j % nbs)), epi, dep=dep)


def _rms_fwd(name, x, gain, dep=None):
    S, D = x.shape
    tm = _tile(S, 256, 8)
    extra = () if dep is None else (dep,)

    def body(x_ref, g_ref, *rest):
        h_ref = rest[-1]
        xv = x_ref[...]
        r = lax.rsqrt(jnp.mean(xv * xv, axis=-1, keepdims=True) + RMS_EPS)
        h_ref[...] = (xv * r * g_ref[...]).astype(BF16)

    return pl.pallas_call(
        body, name=name, grid=(S // tm,),
        in_specs=[pl.BlockSpec((tm, D), lambda i: (i, 0)), pl.BlockSpec((1, D), lambda i: (0, 0))]
        + [pl.BlockSpec(memory_space=pl.ANY) for d in extra],
        out_specs=pl.BlockSpec((tm, D), lambda i: (i, 0)), out_shape=jax.ShapeDtypeStruct((S, D), BF16),
        compiler_params=_cparams(("parallel",)),
    )(x, gain, *extra)


def _rms_bwd(name, x, gain, dh, dres):
    S, D = x.shape
    tm = _tile(S, 256, 8)

    def body(x_ref, g_ref, dh_ref, dres_ref, dx_ref, dxb_ref, dg_ref):
        i = pl.program_id(0)
        xv = x_ref[...]
        r = lax.rsqrt(jnp.mean(xv * xv, axis=-1, keepdims=True) + RMS_EPS)
        xhat = xv * r
        dhv = dh_ref[...]
        dxhat = dhv * g_ref[...]
        dx = dres_ref[...] + r * (dxhat - xhat * jnp.mean(dxhat * xhat, axis=-1, keepdims=True))
        dx_ref[...] = dx
        dxb_ref[...] = dx.astype(BF16)

        @pl.when(i == 0)
        def _():
            dg_ref[...] = jnp.zeros_like(dg_ref)

        dg_ref[...] += jnp.sum(dhv * xhat, axis=0, keepdims=True)

    row = pl.BlockSpec((tm, D), lambda i: (i, 0))
    vec = pl.BlockSpec((1, D), lambda i: (0, 0))
    return pl.pallas_call(
        body, name=name, grid=(S // tm,), in_specs=[row, vec, row, row], out_specs=(row, row, vec),
        out_shape=(jax.ShapeDtypeStruct((S, D), F32), jax.ShapeDtypeStruct((S, D), BF16), jax.ShapeDtypeStruct((1, D), F32)),
        compiler_params=_cparams(("arbitrary",)),
    )(x, gain, dh, dres)


def _loss_grad(name, y, target):
    S, D = y.shape
    tm = _tile(S, 256, 8)

    def body(y_ref, t_ref, dy_ref, dyb_ref, l_ref):
        i = pl.program_id(0)
        e = y_ref[...] - t_ref[...]
        dy_ref[...] = e * (1.0 / D)
        dyb_ref[...] = (e * (1.0 / D)).astype(BF16)
        col = jnp.sum(e * e, axis=0, keepdims=True)
        part = col[:, 0:LANES]
        for k in range(1, D // LANES):
            part = part + col[:, k * LANES:(k + 1) * LANES]

        @pl.when(i == 0)
        def _():
            l_ref[...] = jnp.zeros_like(l_ref)

        l_ref[...] += part

    row = pl.BlockSpec((tm, D), lambda i: (i, 0))
    return pl.pallas_call(
        body, name=name, grid=(S // tm,), in_specs=[row, row],
        out_specs=(row, row, pl.BlockSpec((1, LANES), lambda i: (0, 0))),
        out_shape=(jax.ShapeDtypeStruct((S, D), F32), jax.ShapeDtypeStruct((S, D), BF16), jax.ShapeDtypeStruct((1, LANES), F32)),
        compiler_params=_cparams(("arbitrary",)),
    )(y, target)


def _shift_down(u, k):
    rows = lax.broadcasted_iota(jnp.int32, u.shape, 0)
    return jnp.where(rows >= k, pltpu.roll(u, k, 0), 0.0)


def _shift_up(u, k):
    n = u.shape[0]
    rows = lax.broadcasted_iota(jnp.int32, u.shape, 0)
    return jnp.where(rows < n - k, pltpu.roll(u, n - k, 0), 0.0)


def _conv_specs(S, cw, conv_width):
    nb = conv_width // cw
    col = lambda off: pl.BlockSpec((S, cw), lambda j, off=off: (0, off * nb + j))
    return nb, col(0), col(1), col(2)


def _conv_fwd(name, proj, convw3, conv_width):
    S = proj.shape[0]
    cw = convw3.shape[2]
    nb, xc_s, bg_s, cg_s = _conv_specs(S, cw, conv_width)

    def body(xc_ref, bg_ref, cg_ref, w_ref, o_ref):
        u = cg_ref[...].astype(F32) * xc_ref[...].astype(F32)
        w = w_ref[...]
        cv = w[2:3, :] * u + w[1:2, :] * _shift_down(u, 1) + w[0:1, :] * _shift_down(u, 2)
        o_ref[...] = (bg_ref[...].astype(F32) * cv).astype(BF16)

    return pl.pallas_call(
        body, name=name, grid=(nb,),
        in_specs=[xc_s, bg_s, cg_s, pl.BlockSpec((None, CONV_K, cw), lambda j: (j, 0, 0))],
        out_specs=pl.BlockSpec((S, cw), lambda j: (0, j)), out_shape=jax.ShapeDtypeStruct((S, conv_width), BF16),
        compiler_params=_cparams(("parallel",)),
    )(proj, proj, proj, convw3)


def _conv_bwd(name, proj, convw3, da, conv_width):
    S = proj.shape[0]
    cw = convw3.shape[2]
    nb, xc_s, bg_s, cg_s = _conv_specs(S, cw, conv_width)

    def body(xc_ref, bg_ref, cg_ref, w_ref, da_ref, dxc_ref, dbg_ref, dcg_ref, dw_ref):
        xc, cg = xc_ref[...].astype(F32), cg_ref[...].astype(F32)
        u = cg * xc
        w = w_ref[...]
        u1, u2 = _shift_down(u, 1), _shift_down(u, 2)
        cv = w[2:3, :] * u + w[1:2, :] * u1 + w[0:1, :] * u2
        dav = da_ref[...]
        dbg_ref[...] = (dav * cv).astype(BF16)
        dcv = dav * bg_ref[...].astype(F32)
        du = w[2:3, :] * dcv + w[1:2, :] * _shift_up(dcv, 1) + w[0:1, :] * _shift_up(dcv, 2)
        dxc_ref[...] = (du * cg).astype(BF16)
        dcg_ref[...] = (du * xc).astype(BF16)
        dw_ref[0:1, :] = jnp.sum(dcv * u2, axis=0, keepdims=True)
        dw_ref[1:2, :] = jnp.sum(dcv * u1, axis=0, keepdims=True)
        dw_ref[2:3, :] = jnp.sum(dcv * u, axis=0, keepdims=True)

    wspec = pl.BlockSpec((None, CONV_K, cw), lambda j: (j, 0, 0))
    ospec = pl.BlockSpec((S, cw), lambda j: (0, j))
    act = jax.ShapeDtypeStruct((S, conv_width), BF16)
    return pl.pallas_call(
        body, name=name, grid=(nb,), in_specs=[xc_s, bg_s, cg_s, wspec, ospec],
        out_specs=(ospec, ospec, ospec, wspec),
        out_shape=(act, act, act, jax.ShapeDtypeStruct(convw3.shape, F32)),
        compiler_params=_cparams(("parallel",)),
    )(proj, proj, proj, convw3, da)


def _rope_consts(S, dh):
    rot = dh // 4
    half = rot // 2
    inv_freq = 1.0 / (ROPE_THETA ** (jnp.arange(0, rot, 2, dtype=F32) / rot))
    ang = jnp.arange(S, dtype=F32)[:, None] * inv_freq[None, :]
    cos = jnp.concatenate([jnp.cos(ang), jnp.cos(ang), jnp.ones((S, dh - rot), F32)], axis=1)
    sin = jnp.concatenate([jnp.sin(ang), jnp.sin(ang), jnp.zeros((S, dh - rot), F32)], axis=1)
    rm = np.zeros((dh, dh), np.float32)
    for j in range(half):
        rm[j + half, j] = -1.0
        rm[j, j + half] = 1.0
    return cos, sin, jnp.asarray(rm, BF16), jnp.asarray(rm.T, BF16)


def _exact_perm(y, rm):
    hi = y.astype(BF16)
    r1 = y - hi.astype(F32)
    mid = r1.astype(BF16)
    lo = (r1 - mid.astype(F32)).astype(BF16)
    dot = lambda a: jnp.dot(a, rm, preferred_element_type=F32)
    return dot(hi) + dot(mid) + dot(lo)


def _qk_prep(name, xh, gain, cos, sin, rm):
    H, S, dh = xh.shape
    tm = _tile(S, 1024, 8)

    def body(x_ref, g_ref, c_ref, s_ref, rm_ref, o_ref):
        xv = x_ref[...].astype(F32)
        y = xv * lax.rsqrt(jnp.mean(xv * xv, axis=-1, keepdims=True) + RMS_EPS) * g_ref[...]
        o_ref[...] = (y * c_ref[...] + _exact_perm(y, rm_ref[...]) * s_ref[...]).astype(BF16)

    blk = pl.BlockSpec((None, tm, dh), lambda h, i: (h, i, 0))
    tab = pl.BlockSpec((tm, dh), lambda h, i: (i, 0))
    return pl.pallas_call(
        body, name=name, grid=(H, S // tm),
        in_specs=[blk, pl.BlockSpec((1, dh), lambda h, i: (0, 0)), tab, tab, pl.BlockSpec((dh, dh), lambda h, i: (0, 0))],
        out_specs=blk, out_shape=jax.ShapeDtypeStruct((H, S, dh), BF16),
        compiler_params=_cparams(("parallel", "parallel")),
    )(xh, gain, cos, sin, rm)


def _qk_prep_bwd(name, xh, gain, cos, sin, rmt, dout):
    H, S, dh = xh.shape
    tm = _tile(S, 1024, 8)

    def body(x_ref, g_ref, c_ref, s_ref, rmt_ref, do_ref, dx_ref, dg_ref):
        first = (pl.program_id(0) == 0) & (pl.program_id(1) == 0)
        xv = x_ref[...].astype(F32)
        r = lax.rsqrt(jnp.mean(xv * xv, axis=-1, keepdims=True) + RMS_EPS)
        xhat = xv * r
        dov = do_ref[...]
        dy = dov * c_ref[...] + _exact_perm(dov * s_ref[...], rmt_ref[...])
        dxhat = dy * g_ref[...]
        dx_ref[...] = (r * (dxhat - xhat * jnp.mean(dxhat * xhat, axis=-1, keepdims=True))).astype(BF16)

        @pl.when(first)
        def _():
            dg_ref[...] = jnp.zeros_like(dg_ref)

        dg_ref[...] += jnp.sum(dy * xhat, axis=0, keepdims=True)

    blk = pl.BlockSpec((None, tm, dh), lambda h, i: (h, i, 0))
    tab = pl.BlockSpec((tm, dh), lambda h, i: (i, 0))
    vec = pl.BlockSpec((1, dh), lambda h, i: (0, 0))
    return pl.pallas_call(
        body, name=name, grid=(H, S // tm),
        in_specs=[blk, vec, tab, tab, pl.BlockSpec((dh, dh), lambda h, i: (0, 0)), blk],
        out_specs=(blk, vec), out_shape=(jax.ShapeDtypeStruct((H, S, dh), BF16), jax.ShapeDtypeStruct((1, dh), F32)),
        compiler_params=_cparams(("arbitrary", "arbitrary")),
    )(xh, gain, cos, sin, rmt, dout)


def _attn_probs(q, kp, kc, sink_col, n, scale):
    rows = q.shape[0]
    sp = lax.dot_general(q, kp, NT, preferred_element_type=F32) * scale
    sc = lax.dot_general(q, kc, NT, preferred_element_type=F32) * scale
    qi = lax.broadcasted_iota(jnp.int32, (rows, BLOCK), 0) % BLOCK
    kj = lax.broadcasted_iota(jnp.int32, (rows, BLOCK), 1)
    sp = jnp.where((kj > qi) & (n > 0), sp, NEG_INF)
    sc = jnp.where(kj <= qi, sc, NEG_INF)
    m = jnp.maximum(jnp.maximum(jnp.max(sp, axis=-1, keepdims=True), jnp.max(sc, axis=-1, keepdims=True)), sink_col)
    pp, pc, ps = jnp.exp(sp - m), jnp.exp(sc - m), jnp.exp(sink_col - m)
    inv = 1.0 / (jnp.sum(pp, axis=-1, keepdims=True) + jnp.sum(pc, axis=-1, keepdims=True) + ps)
    return pp * inv, pc * inv, ps * inv


def _sink_col(sink_ref, hk, group):
    rows = group * BLOCK
    g = lax.broadcasted_iota(jnp.int32, (rows, 1), 0) // BLOCK
    col = jnp.zeros((rows, 1), F32)
    for i in range(group):
        col = jnp.where(g == i, sink_ref[hk * group + i], col)
    return col


def _attn_specs(group, S, dh):
    heads = pl.BlockSpec((group, S, dh), lambda hk: (hk, 0, 0))
    kv = pl.BlockSpec((None, S, dh), lambda hk: (hk, 0, 0))
    return heads, kv, pl.BlockSpec(memory_space=pltpu.SMEM)


def _block_rows(n):
    cur = pl.ds(pl.multiple_of(n * BLOCK, BLOCK), BLOCK)
    prev = pl.ds(pl.multiple_of(jnp.maximum(n - 1, 0) * BLOCK, BLOCK), BLOCK)
    return cur, prev


def _attn_fwd(name, q, k, v, sinks):
    HQ, S, dh = q.shape
    HKV = k.shape[0]
    group = HQ // HKV
    scale = dh ** -0.5
    heads, kv, smem = _attn_specs(group, S, dh)

    def body(q_ref, k_ref, v_ref, sink_ref, o_ref):
        sink = _sink_col(sink_ref, pl.program_id(0), group)

        def block(n, carry):
            cur, prev = _block_rows(n)
            qv = q_ref[:, cur, :].reshape(group * BLOCK, dh)
            pp, pc, _ = _attn_probs(qv, k_ref[prev, :], k_ref[cur, :], sink, n, scale)
            o = jnp.dot(pp.astype(BF16), v_ref[prev, :], preferred_element_type=F32)
            o = o + jnp.dot(pc.astype(BF16), v_ref[cur, :], preferred_element_type=F32)
            o_ref[:, cur, :] = o.reshape(group, BLOCK, dh).astype(BF16)
            return carry

        lax.fori_loop(0, S // BLOCK, block, 0)

    return pl.pallas_call(
        body, name=name, grid=(HKV,), in_specs=[heads, kv, kv, smem], out_specs=heads,
        out_shape=jax.ShapeDtypeStruct((HQ, S, dh), BF16), compiler_params=_cparams(("parallel",)),
    )(q, k, v, sinks)


def _attn_bwd(name, q, k, v, sinks, do):
    HQ, S, dh = q.shape
    HKV = k.shape[0]
    group = HQ // HKV
    scale = dh ** -0.5
    heads, kv, smem = _attn_specs(group, S, dh)
    sk = pl.BlockSpec((None, group, LANES), lambda hk: (hk, 0, 0))

    def body(q_ref, k_ref, v_ref, sink_ref, do_ref, dq_ref, dk_ref, dv_ref, ds_ref):
        rows = group * BLOCK
        sink = _sink_col(sink_ref, pl.program_id(0), group)
        dk_ref[...] = jnp.zeros_like(dk_ref)
        dv_ref[...] = jnp.zeros_like(dv_ref)
        tdot = lambda a, b: lax.dot_general(a, b, TN, preferred_element_type=F32)

        def block(n, dsink):
            cur, prev = _block_rows(n)
            qv = q_ref[:, cur, :].reshape(rows, dh)
            dov = do_ref[:, cur, :].reshape(rows, dh)
            kp, kc, vp, vc = k_ref[prev, :], k_ref[cur, :], v_ref[prev, :], v_ref[cur, :]
            pp, pc, ps = _attn_probs(qv, kp, kc, sink, n, scale)
            dpp = lax.dot_general(dov, vp, NT, preferred_element_type=F32)
            dpc = lax.dot_general(dov, vc, NT, preferred_element_type=F32)
            delta = jnp.sum(pp * dpp, axis=-1, keepdims=True) + jnp.sum(pc * dpc, axis=-1, keepdims=True)
            dsp = (pp * (dpp - delta) * scale).astype(BF16)
            dsc = (pc * (dpc - delta) * scale).astype(BF16)
            dq = jnp.dot(dsp, kp, preferred_element_type=F32) + jnp.dot(dsc, kc, preferred_element_type=F32)
            dq_ref[:, cur, :] = dq.reshape(group, BLOCK, dh)
            dk_ref[prev, :] += tdot(dsp, qv)
            dv_ref[prev, :] += tdot(pp.astype(BF16), dov)
            dk_ref[cur, :] += tdot(dsc, qv)
            dv_ref[cur, :] += tdot(pc.astype(BF16), dov)
            return dsink - jnp.sum((ps * delta).reshape(group, BLOCK, 1), axis=1)

        dsink = lax.fori_loop(0, S // BLOCK, block, jnp.zeros((group, 1), F32))
        ds_ref[...] = jnp.broadcast_to(dsink, (group, LANES))

    return pl.pallas_call(
        body, name=name, grid=(HKV,), in_specs=[heads, kv, kv, smem, heads], out_specs=(heads, kv, kv, sk),
        out_shape=(jax.ShapeDtypeStruct((HQ, S, dh), F32), jax.ShapeDtypeStruct((HKV, S, dh), F32),
                   jax.ShapeDtypeStruct((HKV, S, dh), F32), jax.ShapeDtypeStruct((HKV, group, LANES), F32)),
        compiler_params=_cparams(("parallel",)),
    )(q, k, v, sinks, do)


RESIDENT = pl.Buffered(1)


def _gate_blocks(tm, Ns, off):
    assert off % Ns == 0
    return [pl.BlockSpec((tm, Ns), lambda i, k=k: (i, off // Ns + k)) for k in range(N_CHIPS)]


def _mixer_out_fwd(name, aconv, o, woc3, woa3, wo, proj, x1, ga_off, gb_off):
    S, D = x1.shape
    Ns = woc3.shape[2]
    tm = _tile(S, 256, 16)

    def body(a_ref, o_ref, woc_ref, woa_ref, wo_ref, x1_ref, *rest):
        ga_refs, gb_refs = rest[:N_CHIPS], rest[N_CHIPS:2 * N_CHIPS]
        ya_ref, yb_ref, m_ref, x2_ref = rest[2 * N_CHIPS:]
        av, ov = a_ref[...], o_ref[...]
        for s in range(N_CHIPS):
            cols = slice(s * Ns, (s + 1) * Ns)
            ya = jnp.dot(av, woc_ref[s], preferred_element_type=F32)
            yb = jnp.dot(ov, woa_ref[s], preferred_element_type=F32)
            ya_ref[:, cols] = ya.astype(BF16)
            yb_ref[:, cols] = yb.astype(BF16)
            ga, gb = ga_refs[s][...].astype(F32), gb_refs[s][...].astype(F32)
            m_ref[:, cols] = (_sigmoid(ga) * ya + _sigmoid(gb) * yb).astype(BF16)
        x2_ref[...] = x1_ref[...] + jnp.dot(m_ref[...], wo_ref[...], preferred_element_type=F32)

    row = lambda w: pl.BlockSpec((tm, w), lambda i: (i, 0))
    whole3 = lambda a: pl.BlockSpec(a.shape, lambda i: (0, 0, 0), pipeline_mode=RESIDENT)
    act = jax.ShapeDtypeStruct((S, D), BF16)
    return pl.pallas_call(
        body, name=name, grid=(S // tm,),
        in_specs=[row(aconv.shape[1]), row(o.shape[1]), whole3(woc3), whole3(woa3),
                  pl.BlockSpec(wo.shape, lambda i: (0, 0), pipeline_mode=RESIDENT), row(D)]
        + _gate_blocks(tm, Ns, ga_off) + _gate_blocks(tm, Ns, gb_off),
        out_specs=(row(D), row(D), row(D), row(D)), out_shape=(act, act, act, jax.ShapeDtypeStruct((S, D), F32)),
        compiler_params=_cparams(("parallel",)),
    )(aconv, o, woc3, woa3, wo, x1, *([proj] * (2 * N_CHIPS)))


def _mixer_out_bwd(name, dx2_b, wo, ya, yb, proj, woc3, woa3, ga_off, gb_off, dep):
    S, D = dx2_b.shape
    K, Ns = woc3.shape[1], woc3.shape[2]
    tm = _tile(S, 256, 16)

    def body(dx_ref, wo_ref, ya_ref, yb_ref, woc_ref, woa_ref, *rest):
        ga_refs, gb_refs = rest[:N_CHIPS], rest[N_CHIPS:2 * N_CHIPS]
        dga_ref, dgb_ref, dya_ref, dyb_ref, da_ref, do_ref = rest[-6:]
        dm = lax.dot_general(dx_ref[...], wo_ref[...], NT, preferred_element_type=F32)
        da = do = None
        for s in range(N_CHIPS):
            cols = slice(s * Ns, (s + 1) * Ns)
            dms = dm[:, cols]
            sa, sb = _sigmoid(ga_refs[s][...].astype(F32)), _sigmoid(gb_refs[s][...].astype(F32))
            dga_ref[:, cols] = (dms * ya_ref[:, cols].astype(F32) * sa * (1.0 - sa)).astype(BF16)
            dgb_ref[:, cols] = (dms * yb_ref[:, cols].astype(F32) * sb * (1.0 - sb)).astype(BF16)
            dya, dyb = (dms * sa).astype(BF16), (dms * sb).astype(BF16)
            dya_ref[:, cols] = dya
            dyb_ref[:, cols] = dyb
            pa = lax.dot_general(dya, woc_ref[s], NT, preferred_element_type=F32)
            pb = lax.dot_general(dyb, woa_ref[s], NT, preferred_element_type=F32)
            da, do = (pa, pb) if da is None else (da + pa, do + pb)
        da_ref[...] = da
        do_ref[...] = do.astype(BF16)

    row = lambda w: pl.BlockSpec((tm, w), lambda i: (i, 0))
    whole3 = lambda a: pl.BlockSpec(a.shape, lambda i: (0, 0, 0), pipeline_mode=RESIDENT)
    act = jax.ShapeDtypeStruct((S, D), BF16)
    return pl.pallas_call(
        body, name=name, grid=(S // tm,),
        in_specs=[row(D), pl.BlockSpec(wo.shape, lambda i: (0, 0), pipeline_mode=RESIDENT), row(D), row(D), whole3(woc3), whole3(woa3)]
        + _gate_blocks(tm, Ns, ga_off) + _gate_blocks(tm, Ns, gb_off) + [pl.BlockSpec(dep.shape, lambda i: (0, 0))],
        out_specs=(row(D), row(D), row(D), row(D), row(K), row(K)),
        out_shape=(act, act, act, act, jax.ShapeDtypeStruct((S, K), F32), jax.ShapeDtypeStruct((S, K), BF16)),
        compiler_params=_cparams(("parallel",)),
    )(dx2_b, wo, ya, yb, woc3, woa3, *([proj] * (2 * N_CHIPS)), dep)


ANY = pl.BlockSpec(memory_space=pl.ANY)


def _row_tile(rows, cols, n_arrays):
    want = max(16, (VMEM_LIMIT_V7X // 2) // (2 * n_arrays * cols * 4))
    return _tile(rows, want, 16)


def _cast_to_slot(name, w, dtype, p_arr, dep=None):
    R, C = w.shape
    tr = _row_tile(R, C, 2)
    extra = () if dep is None else (dep,)

    def body(p_ref, w_ref, *rest):
        rest[-1][...] = w_ref[...].astype(dtype)

    return pl.pallas_call(
        body, name=name,
        grid_spec=pltpu.PrefetchScalarGridSpec(
            num_scalar_prefetch=1, grid=(R // tr,),
            in_specs=[pl.BlockSpec((tr, C), lambda i, p_ref: (i, 0))] + [pl.BlockSpec(d.shape, lambda i, p_ref: (0, 0)) for d in extra],
            out_specs=pl.BlockSpec((None, tr, C), lambda i, p_ref: (p_ref[0], i, 0))),
        out_shape=jax.ShapeDtypeStruct((N_CHIPS, R, C), dtype), compiler_params=_cparams(("parallel",)),
    )(p_arr, w, *extra)


def _add_half(name, g3, r3, c_arr):
    n, h, C = r3.shape
    tr = _row_tile(h, C, 3)
    nb = h // tr

    def body(c_ref, g_ref, r_ref, o_ref):
        o_ref[...] = (g_ref[...].astype(F32) + r_ref[...].astype(F32)).astype(BF16)

    blk = pl.BlockSpec((None, tr, C), lambda s, i, c_ref: (s, i, 0))
    return pl.pallas_call(
        body, name=name,
        grid_spec=pltpu.PrefetchScalarGridSpec(
            num_scalar_prefetch=1, grid=(n, nb),
            in_specs=[pl.BlockSpec((None, tr, C), lambda s, i, c_ref: (s, c_ref[0] * nb + i, 0)), blk], out_specs=blk),
        out_shape=jax.ShapeDtypeStruct(r3.shape, BF16), compiler_params=_cparams(("parallel", "parallel")),
    )(c_arr, g3, r3)


def _add_chips(name, t3, r3, cp_arr):
    n, h, C = r3.shape
    tr = _row_tile(h, C, 6)
    nb = h // tr

    def body(cp_ref, t_ref, r0_ref, r1_ref, r2_ref, r3_ref, o_ref):
        p = cp_ref[1]
        total = None
        for a, r_ref in enumerate((r0_ref, r1_ref, r2_ref, r3_ref)):
            part = jnp.where(p == a, t_ref[...], r_ref[...]).astype(F32)
            total = part if total is None else total + part
        o_ref[...] = total

    def part(a):
        return pl.BlockSpec((None, tr, C), lambda i, cp_ref: (jnp.where(cp_ref[1] == a, (a + 1) % N_CHIPS, a), i, 0))

    return pl.pallas_call(
        body, name=name,
        grid_spec=pltpu.PrefetchScalarGridSpec(
            num_scalar_prefetch=1, grid=(nb,),
            in_specs=[pl.BlockSpec((None, tr, C), lambda i, cp_ref: (cp_ref[1], i, 0)), part(0), part(1), part(2), part(3)],
            out_specs=pl.BlockSpec((tr, C), lambda i, cp_ref: (cp_ref[0] * nb + i, 0))),
        out_shape=jax.ShapeDtypeStruct((2 * h, C), F32), compiler_params=_cparams(("parallel",)),
    )(cp_arr, t3, r3, r3, r3, r3)


def _adamw(name, w, g, m, v, deps=()):
    R, C = w.shape
    extra = tuple(deps)
    tr = _row_tile(R, C, 8)
    c1 = 1.0 - ADAM_B1 ** ADAM_STEP
    c2 = 1.0 - ADAM_B2 ** ADAM_STEP

    def body(w_ref, g_ref, m_ref, v_ref, *rest):
        go_ref, d_ref, nm_ref, nv_ref = rest[-4:]
        gv = g_ref[...]
        go_ref[...] = gv
        nm = ADAM_B1 * m_ref[...] + (1.0 - ADAM_B1) * gv
        nv = ADAM_B2 * v_ref[...] + (1.0 - ADAM_B2) * (gv * gv)
        d_ref[...] = -ADAM_LR * ((nm / c1) / (jnp.sqrt(nv / c2) + ADAM_EPS) + ADAM_WD * w_ref[...])
        nm_ref[...] = nm
        nv_ref[...] = nv

    blk = pl.BlockSpec((tr, C), lambda i: (i, 0))
    o = jax.ShapeDtypeStruct((R, C), F32)
    return pl.pallas_call(
        body, name=name, grid=(R // tr,), in_specs=[blk, blk, blk, blk] + [ANY] * len(extra), out_specs=(blk, blk, blk, blk),
        out_shape=(o, o, o, o), compiler_params=_cparams(("parallel",)),
    )(w, g, m, v, *extra)


def _place():
    x, y, c = lax.axis_index("x"), lax.axis_index("y"), lax.axis_index("c")
    chips = [(1 - x, y), (x, 1 - y), (1 - x, 1 - y)]
    return x, y, c, 2 * x + y, chips


HBM = pl.BlockSpec(memory_space=pltpu.HBM)
SEM = pl.BlockSpec(memory_space=pltpu.SEMAPHORE)
TOKEN = jax.ShapeDtypeStruct((8, LANES), F32)
DATAFLOW = pltpu.SideEffectType.DATAFLOW_SIDE_EFFECTING


def _hbm(a):
    return pltpu.with_memory_space_constraint(a, pltpu.HBM)


def _gather_blocks(bufs, i, c, p, chips):
    if bufs[i].shape[1] % 16:
        return bufs[i].at[p], [bufs[i].at[2 * cx + cy] for cx, cy in chips]
    h = bufs[i].shape[1] // 2
    rows = pl.ds(pl.multiple_of(c * h, 16), h)
    return bufs[i].at[p, rows], [bufs[i].at[2 * cx + cy, rows] for cx, cy in chips]


def _gather_start(name, groups, dep):
    slots = [s for g in groups for s in g]
    n, ng = len(slots), len(groups)

    def body(*refs):
        bufs, sems, token = refs[:n], refs[n + 1:n + 1 + 2 * ng], refs[-1]
        x, y, c, p, chips = _place()
        i = 0
        for gi, g in enumerate(groups):
            send, recv = sems[2 * gi], sems[2 * gi + 1]
            for k in range(len(g)):
                mine, _ = _gather_blocks(bufs, i, c, p, chips)
                for j, chip in enumerate(chips):
                    pltpu.make_async_remote_copy(src_ref=mine, dst_ref=mine, send_sem=send.at[3 * k + j], recv_sem=recv.at[3 * k + j],
                                                 device_id=(*chip, c), device_id_type=MESH).start()
                i += 1
        token[...] = jnp.zeros_like(token)

    sem_shapes = [pltpu.SemaphoreType.DMA((3 * len(g),)) for g in groups for _ in range(2)]
    out = pl.pallas_call(
        body, name=name, in_specs=[HBM] * n + [ANY],
        out_specs=(*([SEM] * (2 * ng)), *([HBM] * n), pl.BlockSpec(memory_space=pltpu.VMEM)),
        out_shape=(*sem_shapes, *[pltpu.HBM(s.shape, s.dtype) for s in slots], TOKEN),
        input_output_aliases={i: 2 * ng + i for i in range(n)},
        compiler_params=pltpu.CompilerParams(has_side_effects=DATAFLOW),
    )(*[_hbm(s) for s in slots], dep)
    started, i = [], 2 * ng
    for gi, g in enumerate(groups):
        started.append((out[2 * gi], out[2 * gi + 1], list(out[i:i + len(g)])))
        i += len(g)
    return started, out[-1]


def _gather_wait(name, send, recv, slots, after):
    n = len(slots)

    def body(*refs):
        bufs, send, recv = refs[:n], refs[n], refs[n + 1]
        x, y, c, p, chips = _place()
        for i in range(n):
            mine, landed = _gather_blocks(bufs, i, c, p, chips)
            for j, chip in enumerate(chips):
                cp = pltpu.make_async_remote_copy(src_ref=mine, dst_ref=landed[j], send_sem=send.at[3 * i + j],
                                                  recv_sem=recv.at[3 * i + j], device_id=(*chip, c), device_id_type=MESH)
                cp.wait_send()
                cp.wait_recv()

    return list(pl.pallas_call(
        body, name=name, in_specs=[HBM] * n + [SEM, SEM, ANY], out_specs=tuple([HBM] * n),
        out_shape=tuple(pltpu.HBM(s.shape, s.dtype) for s in slots),
        input_output_aliases={i: i for i in range(n)},
        compiler_params=pltpu.CompilerParams(has_side_effects=DATAFLOW),
    )(*slots, send, recv, after))


def _gather_forward(name, slots):
    idx = [i for i, s in enumerate(slots) if s.shape[1] % 16 == 0]
    n = len(slots)

    def body(*refs):
        bufs = refs[n:2 * n]
        send, recv = refs[2 * n:]
        x, y, c, p, chips = _place()

        def rdma(k, ref):
            return pltpu.make_async_remote_copy(src_ref=ref, dst_ref=ref, send_sem=send.at[k], recv_sem=recv.at[k],
                                                device_id=(x, y, 1 - c), device_id_type=MESH)

        cps = []
        for k, i in enumerate(idx):
            for j, ref in enumerate(_gather_blocks(bufs, i, c, p, chips)[1]):
                cps.append(rdma(3 * k + j, ref))
                cps[-1].start()
        for k, i in enumerate(idx):
            for j, ref in enumerate(_gather_blocks(bufs, i, 1 - c, p, chips)[1]):
                rdma(3 * k + j, ref).wait_recv()
        for cp in cps:
            cp.wait_send()

    return list(pl.pallas_call(
        body, name=name, in_specs=[ANY] * n, out_specs=tuple([ANY] * n),
        out_shape=tuple(jax.ShapeDtypeStruct(s.shape, s.dtype) for s in slots),
        scratch_shapes=[pltpu.SemaphoreType.DMA((3 * len(idx),)), pltpu.SemaphoreType.DMA((3 * len(idx),))],
        input_output_aliases={i: i for i in range(n)},
        compiler_params=pltpu.CompilerParams(has_side_effects=True),
    )(*slots))


def _swap_copy(grads, lands, send, recv, i, x, y, c):
    h = grads[i].shape[1] // 2
    other = pl.ds(pl.multiple_of((1 - c) * h, 16), h)
    return pltpu.make_async_remote_copy(src_ref=grads[i].at[:, other, :], dst_ref=lands[i], send_sem=send.at[i],
                                        recv_sem=recv.at[i], device_id=(x, y, 1 - c), device_id_type=MESH)


def _swap_wait(name, send, recv, grads, lands, after):
    n = len(grads)

    def body(*refs):
        ins, lands, send, recv = refs[:n], refs[n:2 * n], refs[2 * n], refs[2 * n + 1]
        x, y, c, p, chips = _place()
        for i in range(n):
            cp = _swap_copy(ins, lands, send, recv, i, x, y, c)
            cp.wait_send()
            cp.wait_recv()

    shapes = [pltpu.HBM(t.shape, t.dtype) for t in list(grads) + list(lands)]
    out = pl.pallas_call(
        body, name=name, in_specs=[HBM] * (2 * n) + [SEM, SEM, ANY], out_specs=tuple([HBM] * (2 * n)),
        out_shape=tuple(shapes), input_output_aliases={i: i for i in range(2 * n)},
        compiler_params=pltpu.CompilerParams(has_side_effects=DATAFLOW),
    )(*grads, *lands, send, recv, after)
    return list(out[:n]), list(out[n:])


def _reduce_starts(name, grads, parts):
    ng, npt = len(grads), len(parts)
    halves = [(g.shape[0], g.shape[1] // 2, g.shape[2]) for g in grads]
    arrays = list(grads) + [lax.empty(s, g.dtype) for s, g in zip(halves, grads)] + list(parts) + [lax.empty(t.shape, t.dtype) for t in parts]
    na = len(arrays)
    sems = ([pltpu.SemaphoreType.DMA((ng,))] * 2 if ng else []) + ([pltpu.SemaphoreType.DMA((3 * npt,))] * 2 if npt else [])
    ns = len(sems)

    def body(*refs):
        ins, sem, token = refs[:na], list(refs[na:na + ns]), refs[-1]
        x, y, c, p, chips = _place()
        if ng:
            for i in range(ng):
                _swap_copy(ins[:ng], ins[ng:2 * ng], sem[0], sem[1], i, x, y, c).start()
        if npt:
            src, land, send, recv = ins[2 * ng:2 * ng + npt], ins[2 * ng + npt:], sem[-2], sem[-1]
            for i in range(npt):
                for j, (cx, cy) in enumerate(chips):
                    pltpu.make_async_remote_copy(src_ref=src[i].at[2 * cx + cy], dst_ref=land[i].at[p], send_sem=send.at[3 * i + j],
                                                 recv_sem=recv.at[3 * i + j], device_id=(cx, cy, c), device_id_type=MESH).start()
        token[...] = jnp.zeros_like(token)

    out = pl.pallas_call(
        body, name=name, in_specs=[HBM] * na,
        out_specs=(*([SEM] * ns), *([HBM] * na), pl.BlockSpec(memory_space=pltpu.VMEM)),
        out_shape=(*sems, *[pltpu.HBM(a.shape, a.dtype) for a in arrays], TOKEN),
        input_output_aliases={i: ns + i for i in range(na)},
        compiler_params=pltpu.CompilerParams(has_side_effects=DATAFLOW),
    )(*[_hbm(a) for a in arrays])
    bufs = list(out[ns:ns + na])
    swap = (out[0], out[1], bufs[:ng], bufs[ng:2 * ng]) if ng else None
    exch = (out[ns - 2], out[ns - 1], bufs[2 * ng:2 * ng + npt], bufs[2 * ng + npt:]) if npt else None
    return swap, exch, out[-1]


def _exchange_wait(name, send, recv, parts, lands, after):
    n = len(parts)

    def body(*refs):
        ins, lands, send, recv = refs[:n], refs[n:2 * n], refs[2 * n], refs[2 * n + 1]
        x, y, c, p, chips = _place()
        for i in range(n):
            for j, (cx, cy) in enumerate(chips):
                q = 2 * cx + cy
                cp = pltpu.make_async_remote_copy(src_ref=ins[i].at[q], dst_ref=lands[i].at[q], send_sem=send.at[3 * i + j],
                                                  recv_sem=recv.at[3 * i + j], device_id=(cx, cy, c), device_id_type=MESH)
                cp.wait_send()
                cp.wait_recv()

    shapes = [pltpu.HBM(t.shape, t.dtype) for t in parts]
    out = pl.pallas_call(
        body, name=name, in_specs=[HBM] * (2 * n) + [SEM, SEM, ANY], out_specs=tuple([HBM] * (2 * n)),
        out_shape=(*shapes, *shapes), input_output_aliases={i: i for i in range(2 * n)},
        compiler_params=pltpu.CompilerParams(has_side_effects=DATAFLOW),
    )(*parts, *lands, send, recv, after)
    return list(out[:n]), list(out[n:])


def _join_copy(buf, send_sem, recv_sem, which, x, y, c):
    h = buf.shape[0] // 2
    rows = buf.at[pl.ds(pl.multiple_of(which * h, 8), h)]
    return pltpu.make_async_remote_copy(src_ref=rows, dst_ref=rows, send_sem=send_sem, recv_sem=recv_sem,
                                        device_id=(x, y, 1 - c), device_id_type=MESH)


def _join_start(name, groups):
    bufs = [b for g in groups for b in g]
    n, ng = len(bufs), len(groups)

    def body(*refs):
        ins, sems, token = refs[:n], refs[n:n + 2 * ng], refs[-1]
        x, y, c, p, chips = _place()
        i = 0
        for gi, g in enumerate(groups):
            for k in range(len(g)):
                _join_copy(ins[i], sems[2 * gi].at[k], sems[2 * gi + 1].at[k], c, x, y, c).start()
                i += 1
        token[...] = jnp.zeros_like(token)

    sem_shapes = [pltpu.SemaphoreType.DMA((len(g),)) for g in groups for _ in range(2)]
    out = pl.pallas_call(
        body, name=name, in_specs=[HBM] * n,
        out_specs=(*([SEM] * (2 * ng)), *([HBM] * n), pl.BlockSpec(memory_space=pltpu.VMEM)),
        out_shape=(*sem_shapes, *[pltpu.HBM(t.shape, t.dtype) for t in bufs], TOKEN),
        input_output_aliases={i: 2 * ng + i for i in range(n)},
        compiler_params=pltpu.CompilerParams(has_side_effects=DATAFLOW),
    )(*[_hbm(t) for t in bufs])
    started, i = [], 2 * ng
    for gi, g in enumerate(groups):
        started.append((out[2 * gi], out[2 * gi + 1], list(out[i:i + len(g)])))
        i += len(g)
    return started, out[-1]


def _join_wait(name, send, recv, bufs, after):
    n = len(bufs)

    def body(*refs):
        ins, send, recv = refs[:n], refs[n], refs[n + 1]
        x, y, c, p, chips = _place()
        for i in range(n):
            _join_copy(ins[i], send.at[i], recv.at[i], c, x, y, c).wait_send()
            _join_copy(ins[i], send.at[i], recv.at[i], 1 - c, x, y, c).wait_recv()

    return list(pl.pallas_call(
        body, name=name, in_specs=[HBM] * n + [SEM, SEM, ANY], out_specs=tuple([HBM] * n),
        out_shape=tuple(pltpu.HBM(t.shape, t.dtype) for t in bufs), input_output_aliases={i: i for i in range(n)},
        compiler_params=pltpu.CompilerParams(has_side_effects=DATAFLOW),
    )(*bufs, send, recv, after))


def _allreduce_small(name, pack, dep):
    R, W = pack.shape

    def body(in_ref, dep_ref, out_ref, slots, send, recv):
        x, y, c = lax.axis_index("x"), lax.axis_index("y"), lax.axis_index("c")
        me = 4 * x + 2 * y + c
        slots[0] = in_ref[...]
        cps = []
        for k in range(1, N_DEV):
            peer = (x ^ (k >> 2), y ^ ((k >> 1) & 1), c ^ (k & 1))
            cp = pltpu.make_async_remote_copy(src_ref=in_ref, dst_ref=slots.at[k], send_sem=send.at[k - 1],
                                              recv_sem=recv.at[k - 1], device_id=peer, device_id_type=MESH)
            cp.start()
            cps.append(cp)
        for cp in cps:
            cp.wait()
        total = slots[me]
        for a in range(1, N_DEV):
            total = total + slots[jnp.bitwise_xor(a, me)]
        out_ref[...] = total

    vmem = pl.BlockSpec(memory_space=pltpu.VMEM)
    return pl.pallas_call(
        body, name=name, in_specs=[vmem, ANY], out_specs=vmem, out_shape=jax.ShapeDtypeStruct((R, W), F32),
        scratch_shapes=[pltpu.VMEM((N_DEV, R, W), F32), pltpu.SemaphoreType.DMA((N_DEV - 1,)), pltpu.SemaphoreType.DMA((N_DEV - 1,))],
        compiler_params=pltpu.CompilerParams(has_side_effects=True),
    )(pack, dep)


def _heads(a, n_heads):
    S = a.shape[0]
    return a.reshape(S, n_heads, a.shape[1] // n_heads).transpose(1, 0, 2)


def _unheads(a):
    H, S, dh = a.shape
    return a.transpose(1, 0, 2).reshape(S, H * dh)


def _ffn_bwd(tag, xin, gain, wgu3, wd, saved, dxout, dxo_b, reduce_start, dep, flush=None):
    h, gu, act = saved
    D = xin.shape[1]
    tok = reduce_start({f"w_down{tag}": _mm_tn(f"dw_down_{tag}", act, dxo_b, 0.5, dep=dep).reshape(N_CHIPS, -1, D)})
    dgu = _ffn_down_bwd(f"ffn_down_bwd_{tag}", dxo_b, wd, gu, 0.5, dep=tok)
    tok = reduce_start({f"w_gu{tag}": _mm_tn_cols(f"dw_gu_{tag}", h, dgu, wgu3.shape[2], b_is_gu=True)})
    if flush is not None:
        tok = flush(tok)
    dh = _mm_nt_cols(f"ffn_up_bwd_{tag}", dgu, wgu3, a_is_gu=True, dep=tok)
    dxin, dxin_b, dgain = _rms_bwd(f"rms_bwd_{tag}", xin, gain, dh, dxout)
    return dxin, dxin_b, dgain, tok


def kernel(x, g_ffn1, w_gu1, w_down1, g_mix, w_in, conv_w, q_norm_g, k_norm_g, sinks, w_out_conv, w_out_attn, w_o, g_ffn2, w_gu2, w_down2, loss_target, m_g_ffn1, m_w_gu1, m_w_down1, m_g_mix, m_w_in, m_conv_w, m_q_norm_g, m_k_norm_g, m_sinks, m_w_out_conv, m_w_out_attn, m_w_o, m_g_ffn2, m_w_gu2, m_w_down2, v_g_ffn1, v_w_gu1, v_w_down1, v_g_mix, v_w_in, v_conv_w, v_q_norm_g, v_k_norm_g, v_sinks, v_w_out_conv, v_w_out_attn, v_w_o, v_g_ffn2, v_w_gu2, v_w_down2):
    S, D = x.shape[1], x.shape[2]
    dh = q_norm_g.shape[1]
    HQ = sinks.shape[1]
    HKV = HQ // 4
    AW, KVW, CW = HQ * dh, HKV * dh, D // 2
    off_q, off_k, off_v = 3 * CW, 3 * CW + AW, 3 * CW + AW + KVW
    off_ga, off_gb = off_v + KVW, off_v + KVW + D
    x0, target = x[0], loss_target[0]
    cx, cy, cc = lax.axis_index("x"), lax.axis_index("y"), lax.axis_index("c")
    chip = 2 * cx + cy
    p_arr = jnp.reshape(chip, (1,)).astype(jnp.int32)
    c_arr = jnp.reshape(cc, (1,)).astype(jnp.int32)
    cp_arr = jnp.stack([cc, chip]).astype(jnp.int32)
    wts = dict(g_ffn1=g_ffn1, w_gu1=w_gu1, w_down1=w_down1, g_mix=g_mix, w_in=w_in, conv_w=conv_w, q_norm_g=q_norm_g,
               k_norm_g=k_norm_g, sinks=sinks, w_out_conv=w_out_conv, w_out_attn=w_out_attn, w_o=w_o, g_ffn2=g_ffn2,
               w_gu2=w_gu2, w_down2=w_down2)
    ms = dict(g_ffn1=m_g_ffn1, w_gu1=m_w_gu1, w_down1=m_w_down1, g_mix=m_g_mix, w_in=m_w_in, conv_w=m_conv_w,
              q_norm_g=m_q_norm_g, k_norm_g=m_k_norm_g, sinks=m_sinks, w_out_conv=m_w_out_conv, w_out_attn=m_w_out_attn,
              w_o=m_w_o, g_ffn2=m_g_ffn2, w_gu2=m_w_gu2, w_down2=m_w_down2)
    vs = dict(g_ffn1=v_g_ffn1, w_gu1=v_w_gu1, w_down1=v_w_down1, g_mix=v_g_mix, w_in=v_w_in, conv_w=v_conv_w,
              q_norm_g=v_q_norm_g, k_norm_g=v_k_norm_g, sinks=v_sinks, w_out_conv=v_w_out_conv, w_out_attn=v_w_out_attn,
              w_o=v_w_o, g_ffn2=v_g_ffn2, w_gu2=v_w_gu2, w_down2=v_w_down2)
    order = list(wts)
    small_names = [k for k in order if not k.startswith("w_")]
    grad, delta, new_m, new_v = {}, {}, {}, {}

    def cast(keys, dep=None):
        return [_cast_to_slot(f"cast_{k}", wts[k][0], F32 if k == "conv_w" else BF16, p_arr, dep) for k in keys]

    def gather_finish(tag, started, after):
        send, recv, slots = started
        return _gather_forward(f"gather_forward_{tag}", _gather_wait(f"gather_wait_{tag}", send, recv, slots, after))

    swapping, pending = [], []

    def reduce_start(full, after=None):
        keys = [] if full is None else list(full)
        pkeys, parts = [], []
        if swapping:
            pkeys, send, recv, gs, lands = swapping.pop(0)
            gs, sib = _swap_wait(f"swap_wait_{pkeys[0]}", send, recv, gs, lands, after if full is None else full[keys[0]])
            parts = [_add_half(f"add_half_{k}", g, r, c_arr) for k, g, r in zip(pkeys, gs, sib)]
        swap, exch, tok = _reduce_starts(f"reduce_starts_{keys[0] if keys else 'last'}", [full[k] for k in keys], parts)
        if exch:
            pending.append((pkeys, *exch))
        if swap:
            swapping.append((keys, *swap))
        return tok

    def reduce_finish(entries, after):
        ready = []
        for keys, send, recv, parts, lands in entries:
            parts, lands = _exchange_wait(f"exchange_wait_{keys[0]}", send, recv, parts, lands, after)
            ready.append((keys, [_add_chips(f"add_chips_{k}", t, r, cp_arr) for k, t, r in zip(keys, parts, lands)]))
        started, last = _join_start(f"join_start_{ready[0][0][0]}", [halves for _, halves in ready])
        for (keys, _), (send, recv, halves) in zip(ready, started):
            for k, g2 in zip(keys, _join_wait(f"join_wait_{keys[0]}", send, recv, halves, last)):
                g2, d, nm, nv = _adamw(f"adamw_{k}", wts[k][0], g2, ms[k][0], vs[k][0], (last,))
                grad[k], delta[k], new_m[k], new_v[k] = g2[None], d[None], nm[None], nv[None]
                last = nv
        return last

    (st_gu1, st_d1), tok = _gather_start("gather_start_1", [cast(["w_gu1"]), cast(["w_down1"])], x0)
    later = ["w_in", "conv_w", "w_out_conv", "w_out_attn", "w_o", "w_gu2", "w_down2"]
    slot = dict(zip(later, cast(later, tok)))
    h1 = _rms_fwd("rms_fwd_1", x0, g_ffn1, slot["w_down2"])
    wgu1, = gather_finish("gu1", st_gu1, h1)
    (st_in, st_out, st_gu2, st_d2), tok = _gather_start(
        "gather_start_2", [[slot["w_in"], slot["conv_w"]], [slot["w_out_conv"], slot["w_out_attn"], slot["w_o"]],
                           [slot["w_gu2"]], [slot["w_down2"]]], wgu1)
    cos, sin, rm, rmt = _rope_consts(S, dh)
    sink_vec = sinks[0]

    gu1, act1 = _ffn_up("ffn_up_1", h1, wgu1, tok)
    wd1 = gather_finish("d1", st_d1, act1)[0].reshape(-1, D)
    x1 = _mm_res("ffn_down_1", act1, wd1, x0, 0.5)
    win3, convw3 = gather_finish("in", st_in, x1)
    h2 = _rms_fwd("rms_fwd_mix", x1, g_mix)
    proj = _mm_cols("in_proj", h2, win3, BF16)
    aconv = _conv_fwd("conv_fwd", proj, convw3, CW)
    woc3, woa3, wo = gather_finish("out", st_out, aconv)
    wo = wo.reshape(-1, D)
    q_raw = _heads(proj[:, off_q:off_q + AW], HQ)
    k_raw = _heads(proj[:, off_k:off_k + KVW], HKV)
    vh = _heads(proj[:, off_v:off_v + KVW], HKV)
    qn = _qk_prep("q_prep", q_raw, q_norm_g, cos, sin, rm)
    kn = _qk_prep("k_prep", k_raw, k_norm_g, cos, sin, rm)
    oh = _attn_fwd("attn_fwd", qn, kn, vh, sink_vec)
    o = _unheads(oh)
    ya, yb, merged, x2 = _mixer_out_fwd("mixer_out", aconv, o, woc3, woa3, wo, proj, x1, off_ga, off_gb)
    wgu2, = gather_finish("gu2", st_gu2, x2)
    h3 = _rms_fwd("rms_fwd_2", x2, g_ffn2)
    gu2, act2 = _ffn_up("ffn_up_2", h3, wgu2)
    wd2 = gather_finish("d2", st_d2, act2)[0].reshape(-1, D)
    x3 = _mm_res("ffn_down_2", act2, wd2, x2, 0.5)

    dy, dy_b, loss_lanes = _loss_grad("loss_grad", x3, target)
    dx2, dx2_b, dg_ffn2, tok = _ffn_bwd("2", x2, g_ffn2, wgu2, wd2, (h3, gu2, act2), dy, dy_b, reduce_start, None)
    tok = reduce_start(dict(w_o=_mm_tn("dw_o", merged, dx2_b, dep=tok).reshape(N_CHIPS, -1, D)))
    dga, dgb, dya, dyb, daconv, do = _mixer_out_bwd("mixer_out_bwd", dx2_b, wo, ya, yb, proj, woc3, woa3, off_ga, off_gb, tok)
    dwoc = _mm_tn_cols("dw_out_conv", aconv, dya, woc3.shape[2])
    dwoa = _mm_tn_cols("dw_out_attn", o, dyb, woa3.shape[2])
    tok = reduce_start(dict(w_out_conv=dwoc, w_out_attn=dwoa))
    dxc, dbg, dcg, dconvw = _conv_bwd("conv_bwd", proj, convw3, daconv, CW)
    dqn, dkn, dvh, dsink3 = _attn_bwd("attn_bwd", qn, kn, vh, sink_vec, _heads(do, HQ))
    dq_raw, dqg = _qk_prep_bwd("q_prep_bwd", q_raw, q_norm_g, cos, sin, rmt, dqn)
    dk_raw, dkg = _qk_prep_bwd("k_prep_bwd", k_raw, k_norm_g, cos, sin, rmt, dkn)
    dproj = jnp.concatenate([dxc, dbg, dcg, _unheads(dq_raw), _unheads(dk_raw), _unheads(dvh).astype(BF16), dga, dgb], axis=1)
    dh2 = _mm_nt_cols("in_proj_bwd", dproj, win3, dep=tok)
    tok = reduce_start(dict(w_in=_mm_tn_cols("dw_in", h2, dproj, win3.shape[2])))
    dx1, dx1_b, dg_mix = _rms_bwd("rms_bwd_mix", x1, g_mix, dh2, dx2)
    dx0, _, dg_ffn1, tok = _ffn_bwd("1", x0, g_ffn1, wgu1, wd1, (h1, gu1, act1), dx1, dx1_b, reduce_start, tok, lambda after: reduce_start(None, after))

    def rows8(a):
        a = a.reshape(-1, a.shape[-1])
        return jnp.pad(a, ((0, -a.shape[0] % 8), (0, D - a.shape[1])))

    misc = jnp.concatenate([dqg, dkg, dsink3[:, :, 0].reshape(1, HQ), loss_lanes], axis=1)
    done = reduce_finish(pending[:-2], dx0)
    tot = _allreduce_small("allreduce_small", jnp.concatenate([rows8(a) for a in (dg_ffn1, dg_mix, dg_ffn2, dconvw, misc)], axis=0), done)
    reduce_finish(pending[-2:], tot)

    cw_s = conv_w.shape[2]
    conv_row0, misc_row = 24, 24 + (-(-N_CHIPS * CONV_K // 8)) * 8
    small_g = dict(g_ffn1=tot[0:1], g_mix=tot[8:9], g_ffn2=tot[16:17],
                   conv_w=lax.dynamic_slice(tot, (conv_row0 + CONV_K * chip, 0), (CONV_K, cw_s)),
                   q_norm_g=tot[misc_row:misc_row + 1, 0:dh], k_norm_g=tot[misc_row:misc_row + 1, dh:2 * dh],
                   sinks=tot[misc_row:misc_row + 1, 2 * dh:2 * dh + HQ])
    loss = (0.5 / D) * jnp.sum(tot[misc_row, 2 * dh + HQ:2 * dh + HQ + LANES])

    def small_pack(src):
        return jnp.concatenate([rows8(src[k]) for k in small_names], axis=0)

    _, sd, sm, sv = _adamw("adamw_small", small_pack(wts), small_pack(small_g), small_pack(ms), small_pack(vs))
    for i, k in enumerate(small_names):
        shape = wts[k].shape
        nr, ncol = math.prod(shape[:-1]), shape[-1]
        grad[k] = small_g[k].reshape(shape)
        delta[k], new_m[k], new_v[k] = (a[8 * i:8 * i + nr, 0:ncol].reshape(shape) for a in (sd, sm, sv))
    return (loss, dx0[None], *[grad[k] for k in order], *[delta[k] for k in order],
            *[new_m[k] for k in order], *[new_v[k] for k in order])
```

```python
import math

import numpy as np
import jax
import jax.numpy as jnp
from jax import lax
from jax.experimental import pallas as pl
from jax.experimental.pallas import tpu as pltpu

F32 = jnp.float32
BF16 = jnp.bfloat16
MESH = pl.DeviceIdType.MESH

RMS_EPS = 1e-6
BLOCK = 128
ROPE_THETA = 500000.0
NEG_INF = -1e30
CONV_K = 3
ADAM_LR, ADAM_B1, ADAM_B2, ADAM_EPS, ADAM_WD, ADAM_STEP = 0.001, 0.9, 0.999, 1e-08, 0.01, 10

VMEM_LIMIT_V7X = 56 * 1024 * 1024
LANES = 128
N_CHIPS = 4
N_DEV = 8


def _tile(n, want, align=LANES):
    best = None
    t = align
    while t <= min(n, want):
        if n % t == 0:
            best = t
        t += align
    return best or n


def _cparams(sem):
    return pltpu.CompilerParams(dimension_semantics=sem, vmem_limit_bytes=VMEM_LIMIT_V7X)


def _sigmoid(x):
    return 1.0 / (1.0 + jnp.exp(-x))


NN = (((1,), (0,)), ((), ()))
NT = (((1,), (1,)), ((), ()))
TN = (((0,), (0,)), ((), ()))


def _mm(name, grid, ins, in_specs, compute, out_shape, out_specs, epilogue, dep=None):
    if dep is not None:
        ins, in_specs = tuple(ins) + (dep,), list(in_specs) + [pl.BlockSpec(dep.shape, lambda *_: (0, 0))]
    n_in = len(ins)

    def body(*refs):
        epilogue(compute(refs[:n_in]), refs[:n_in], refs[n_in:])

    return pl.pallas_call(
        body, name=name, grid=grid, in_specs=in_specs, out_specs=out_specs, out_shape=out_shape,
        compiler_params=_cparams(("parallel", "arbitrary")),
    )(*ins)


def _dot(dims, a=0, b=1):
    return lambda refs: [lax.dot_general(refs[a][...], refs[b][...], dims, preferred_element_type=F32)]


def _ffn_up(name, h, wgu3, dep=None):
    S, D = h.shape
    Ns = wgu3.shape[2]
    F = 2 * Ns
    tm, tn = _tile(S, 512), _tile(Ns, 1408)
    nbs = Ns // tn

    def compute(refs):
        hv = refs[0][...]
        return [jnp.dot(hv, refs[1][...], preferred_element_type=F32), jnp.dot(hv, refs[2][...], preferred_element_type=F32)]

    def epi(accs, in_refs, out_refs):
        g, u = accs
        dgu_ref, a_ref = out_refs
        sg = _sigmoid(g)
        silu = g * sg
        dgu_ref[0] = (u * (sg * (1.0 + g * (1.0 - sg)))).astype(BF16)
        dgu_ref[1] = silu.astype(BF16)
        a_ref[...] = (silu * u).astype(BF16)

    return _mm(
        name, (F // tn, S // tm), (h, wgu3, wgu3),
        [pl.BlockSpec((tm, D), lambda j, i: (i, 0)),
         pl.BlockSpec((None, D, tn), lambda j, i: (j // nbs, 0, j % nbs)),
         pl.BlockSpec((None, D, tn), lambda j, i: (2 + j // nbs, 0, j % nbs))],
        compute, (jax.ShapeDtypeStruct((2, S, F), BF16), jax.ShapeDtypeStruct((S, F), BF16)),
        (pl.BlockSpec((2, tm, tn), lambda j, i: (0, i, j)), pl.BlockSpec((tm, tn), lambda j, i: (i, j))), epi, dep=dep)


def _mm_res(name, a, w, res, scale):
    S, K = a.shape
    N = w.shape[1]
    tm, tn = _tile(S, 512), _tile(N, 512 if K > 2816 else 1024)

    def epi(accs, in_refs, out_refs):
        out_refs[0][...] = in_refs[2][...] + scale * accs[0]

    return _mm(
        name, (N // tn, S // tm), (a, w, res),
        [pl.BlockSpec((tm, K), lambda j, i: (i, 0)), pl.BlockSpec((K, tn), lambda j, i: (0, j)),
         pl.BlockSpec((tm, tn), lambda j, i: (i, j))],
        _dot(NN), jax.ShapeDtypeStruct((S, N), F32), pl.BlockSpec((tm, tn), lambda j, i: (i, j)), epi)


def _mm_cols(name, a, w3, out_dtype):
    S, K = a.shape
    Ns = w3.shape[2]
    tm, tn = _tile(S, 512), _tile(Ns, 2304)
    nbs = Ns // tn

    def epi(accs, in_refs, out_refs):
        out_refs[0][...] = accs[0].astype(out_dtype)

    return _mm(
        name, (N_CHIPS * nbs, S // tm), (a, w3),
        [pl.BlockSpec((tm, K), lambda j, i: (i, 0)),
         pl.BlockSpec((None, K, tn), lambda j, i: (j // nbs, 0, j % nbs))],
        _dot(NN), jax.ShapeDtypeStruct((S, N_CHIPS * Ns), out_dtype), pl.BlockSpec((tm, tn), lambda j, i: (i, j)), epi)


def _ffn_down_bwd(name, dy, wd, gu, scale, dep=None):
    S, D = dy.shape
    F = wd.shape[0]
    tm, tn = _tile(S, 512), _tile(F, 1408)

    def epi(accs, in_refs, out_refs):
        da = scale * accs[0]
        out_refs[0][0] = (da * in_refs[2][0].astype(F32)).astype(BF16)
        out_refs[0][1] = (da * in_refs[2][1].astype(F32)).astype(BF16)

    return _mm(
        name, (F // tn, S // tm), (dy, wd, gu),
        [pl.BlockSpec((tm, D), lambda j, i: (i, 0)), pl.BlockSpec((tn, D), lambda j, i: (j, 0)),
         pl.BlockSpec((2, tm, tn), lambda j, i: (0, i, j))],
        _dot(NT), jax.ShapeDtypeStruct((2, S, F), BF16), pl.BlockSpec((2, tm, tn), lambda j, i: (0, i, j)), epi, dep=dep)


def _mm_nt_cols(name, a, w3, a_is_gu=False, dep=None):
    K, Ns = w3.shape[1], w3.shape[2]
    S = a.shape[1] if a_is_gu else a.shape[0]
    tm = _tile(S, 512)
    tn = _tile(K, max(LANES, (6 << 20) // (N_CHIPS * Ns * 2)))
    if a_is_gu:
        a_spec = pl.BlockSpec((2, tm, 2 * Ns), lambda i, j: (0, i, 0))
        part = lambda a_ref, s: a_ref[s // 2, :, (s % 2) * Ns:(s % 2 + 1) * Ns]
    else:
        a_spec = pl.BlockSpec((tm, N_CHIPS * Ns), lambda i, j: (i, 0))
        part = lambda a_ref, s: a_ref[:, s * Ns:(s + 1) * Ns]

    def compute(refs):
        total = None
        for s in range(N_CHIPS):
            prod = lax.dot_general(part(refs[0], s), refs[1][s], NT, preferred_element_type=F32)
            total = prod if total is None else total + prod
        return [total]

    def epi(accs, in_refs, out_refs):
        out_refs[0][...] = accs[0]

    return _mm(
        name, (S // tm, K // tn), (a, w3), [a_spec, pl.BlockSpec((N_CHIPS, tn, Ns), lambda i, j: (0, j, 0))],
        compute, jax.ShapeDtypeStruct((S, K), F32), pl.BlockSpec((tm, tn), lambda i, j: (i, j)), epi, dep=dep)


def _mm_tn(name, a, b, scale=1.0, dep=None):
    S, K = a.shape
    N = b.shape[1]
    tm, tn = _tile(K, 512), _tile(N, 1024)

    def epi(accs, in_refs, out_refs):
        out_refs[0][...] = (scale * accs[0]).astype(BF16)

    return _mm(
        name, (N // tn, K // tm), (a, b),
        [pl.BlockSpec((S, tm), lambda j, i: (0, i)), pl.BlockSpec((S, tn), lambda j, i: (0, j))],
        _dot(TN), jax.ShapeDtypeStruct((K, N), BF16), pl.BlockSpec((tm, tn), lambda j, i: (i, j)), epi, dep=dep)


def _mm_tn_cols(name, a, b, Ns, b_is_gu=False, dep=None):
    S, K = a.shape
    tm, tn = _tile(K, 512), _tile(Ns, 2304)
    nbs = Ns // tn
    if b_is_gu:
        b_spec = pl.BlockSpec((None, S, tn), lambda j, i: (j // (2 * nbs), 0, j % (2 * nbs)))
    else:
        b_spec = pl.BlockSpec((S, tn), lambda j, i: (0, j))

    def epi(accs, in_refs, out_refs):
        out_refs[0][...] = accs[0].astype(BF16)

    return _mm(
        name, (N_CHIPS * nbs, K // tm), (a, b), [pl.BlockSpec((S, tm), lambda j, i: (0, i)), b_spec],
        _dot(TN), jax.ShapeDtypeStruct((N_CHIPS, K, Ns), BF16),
        pl.BlockSpec((None, tm, tn), lambda j, i: (j // nbs, i, j % nbs)), epi, dep=dep)


def _rms_fwd(name, x, gain, dep=None):
    S, D = x.shape
    tm = _tile(S, 256, 8)
    extra = () if dep is None else (dep,)

    def body(x_ref, g_ref, *rest):
        h_ref = rest[-1]
        xv = x_ref[...]
        r = lax.rsqrt(jnp.mean(xv * xv, axis=-1, keepdims=True) + RMS_EPS)
        h_ref[...] = (xv * r * g_ref[...]).astype(BF16)

    return pl.pallas_call(
        body, name=name, grid=(S // tm,),
        in_specs=[pl.BlockSpec((tm, D), lambda i: (i, 0)), pl.BlockSpec((1, D), lambda i: (0, 0))]
        + [pl.BlockSpec(memory_space=pl.ANY) for d in extra],
        out_specs=pl.BlockSpec((tm, D), lambda i: (i, 0)), out_shape=jax.ShapeDtypeStruct((S, D), BF16),
        compiler_params=_cparams(("parallel",)),
    )(x, gain, *extra)


def _rms_bwd(name, x, gain, dh, dres):
    S, D = x.shape
    tm = _tile(S, 256, 8)

    def body(x_ref, g_ref, dh_ref, dres_ref, dx_ref, dxb_ref, dg_ref):
        i = pl.program_id(0)
        xv = x_ref[...]
        r = lax.rsqrt(jnp.mean(xv * xv, axis=-1, keepdims=True) + RMS_EPS)
        xhat = xv * r
        dhv = dh_ref[...]
        dxhat = dhv * g_ref[...]
        dx = dres_ref[...] + r * (dxhat - xhat * jnp.mean(dxhat * xhat, axis=-1, keepdims=True))
        dx_ref[...] = dx
        dxb_ref[...] = dx.astype(BF16)

        @pl.when(i == 0)
        def _():
            dg_ref[...] = jnp.zeros_like(dg_ref)

        dg_ref[...] += jnp.sum(dhv * xhat, axis=0, keepdims=True)

    row = pl.BlockSpec((tm, D), lambda i: (i, 0))
    vec = pl.BlockSpec((1, D), lambda i: (0, 0))
    return pl.pallas_call(
        body, name=name, grid=(S // tm,), in_specs=[row, vec, row, row], out_specs=(row, row, vec),
        out_shape=(jax.ShapeDtypeStruct((S, D), F32), jax.ShapeDtypeStruct((S, D), BF16), jax.ShapeDtypeStruct((1, D), F32)),
        compiler_params=_cparams(("arbitrary",)),
    )(x, gain, dh, dres)


def _loss_grad(name, y, target):
    S, D = y.shape
    tm = _tile(S, 256, 8)

    def body(y_ref, t_ref, dy_ref, dyb_ref, l_ref):
        i = pl.program_id(0)
        e = y_ref[...] - t_ref[...]
        dy_ref[...] = e * (1.0 / D)
        dyb_ref[...] = (e * (1.0 / D)).astype(BF16)
        col = jnp.sum(e * e, axis=0, keepdims=True)
        part = col[:, 0:LANES]
        for k in range(1, D // LANES):
            part = part + col[:, k * LANES:(k + 1) * LANES]

        @pl.when(i == 0)
        def _():
            l_ref[...] = jnp.zeros_like(l_ref)

        l_ref[...] += part

    row = pl.BlockSpec((tm, D), lambda i: (i, 0))
    return pl.pallas_call(
        body, name=name, grid=(S // tm,), in_specs=[row, row],
        out_specs=(row, row, pl.BlockSpec((1, LANES), lambda i: (0, 0))),
        out_shape=(jax.ShapeDtypeStruct((S, D), F32), jax.ShapeDtypeStruct((S, D), BF16), jax.ShapeDtypeStruct((1, LANES), F32)),
        compiler_params=_cparams(("arbitrary",)),
    )(y, target)


def _shift_down(u, k):
    rows = lax.broadcasted_iota(jnp.int32, u.shape, 0)
    return jnp.where(rows >= k, pltpu.roll(u, k, 0), 0.0)


def _shift_up(u, k):
    n = u.shape[0]
    rows = lax.broadcasted_iota(jnp.int32, u.shape, 0)
    return jnp.where(rows < n - k, pltpu.roll(u, n - k, 0), 0.0)


def _conv_specs(S, cw, conv_width):
    nb = conv_width // cw
    col = lambda off: pl.BlockSpec((S, cw), lambda j, off=off: (0, off * nb + j))
    return nb, col(0), col(1), col(2)


def _conv_fwd(name, proj, convw3, conv_width):
    S = proj.shape[0]
    cw = convw3.shape[2]
    nb, xc_s, bg_s, cg_s = _conv_specs(S, cw, conv_width)

    def body(xc_ref, bg_ref, cg_ref, w_ref, o_ref):
        u = cg_ref[...].astype(F32) * xc_ref[...].astype(F32)
        w = w_ref[...]
        cv = w[2:3, :] * u + w[1:2, :] * _shift_down(u, 1) + w[0:1, :] * _shift_down(u, 2)
        o_ref[...] = (bg_ref[...].astype(F32) * cv).astype(BF16)

    return pl.pallas_call(
        body, name=name, grid=(nb,),
        in_specs=[xc_s, bg_s, cg_s, pl.BlockSpec((None, CONV_K, cw), lambda j: (j, 0, 0))],
        out_specs=pl.BlockSpec((S, cw), lambda j: (0, j)), out_shape=jax.ShapeDtypeStruct((S, conv_width), BF16),
        compiler_params=_cparams(("parallel",)),
    )(proj, proj, proj, convw3)


def _conv_bwd(name, proj, convw3, da, conv_width):
    S = proj.shape[0]
    cw = convw3.shape[2]
    nb, xc_s, bg_s, cg_s = _conv_specs(S, cw, conv_width)

    def body(xc_ref, bg_ref, cg_ref, w_ref, da_ref, dxc_ref, dbg_ref, dcg_ref, dw_ref):
        xc, cg = xc_ref[...].astype(F32), cg_ref[...].astype(F32)
        u = cg * xc
        w = w_ref[...]
        u1, u2 = _shift_down(u, 1), _shift_down(u, 2)
        cv = w[2:3, :] * u + w[1:2, :] * u1 + w[0:1, :] * u2
        dav = da_ref[...]
        dbg_ref[...] = (dav * cv).astype(BF16)
        dcv = dav * bg_ref[...].astype(F32)
        du = w[2:3, :] * dcv + w[1:2, :] * _shift_up(dcv, 1) + w[0:1, :] * _shift_up(dcv, 2)
        dxc_ref[...] = (du * cg).astype(BF16)
        dcg_ref[...] = (du * xc).astype(BF16)
        dw_ref[0:1, :] = jnp.sum(dcv * u2, axis=0, keepdims=True)
        dw_ref[1:2, :] = jnp.sum(dcv * u1, axis=0, keepdims=True)
        dw_ref[2:3, :] = jnp.sum(dcv * u, axis=0, keepdims=True)

    wspec = pl.BlockSpec((None, CONV_K, cw), lambda j: (j, 0, 0))
    ospec = pl.BlockSpec((S, cw), lambda j: (0, j))
    act = jax.ShapeDtypeStruct((S, conv_width), BF16)
    return pl.pallas_call(
        body, name=name, grid=(nb,), in_specs=[xc_s, bg_s, cg_s, wspec, ospec],
        out_specs=(ospec, ospec, ospec, wspec),
        out_shape=(act, act, act, jax.ShapeDtypeStruct(convw3.shape, F32)),
        compiler_params=_cparams(("parallel",)),
    )(proj, proj, proj, convw3, da)


def _prep_consts(S, dh, width):
    rot = dh // 4
    half = rot // 2
    inv_freq = 1.0 / (ROPE_THETA ** (jnp.arange(0, rot, 2, dtype=F32) / rot))
    ang = jnp.arange(S, dtype=F32)[:, None] * inv_freq[None, :]
    zeros = jnp.zeros((S, dh - rot), F32)
    cos = jnp.concatenate([jnp.cos(ang), jnp.cos(ang), 1.0 + zeros], axis=1)
    sin_next = jnp.concatenate([-jnp.sin(ang), 0.0 * ang, zeros], axis=1)
    sin_prev = jnp.concatenate([0.0 * ang, jnp.sin(ang), zeros], axis=1)
    reps = min(LANES, width) // dh
    tables = [jnp.tile(t, (1, reps)) for t in (cos, sin_next, sin_prev)]
    mean = np.kron(np.eye(width // dh, dtype=np.float32), np.full((dh, dh), 1.0 / dh, np.float32))
    return (*tables, jnp.asarray(mean, BF16), half)


def _head_mean(p, mean):
    hi = p.astype(BF16)
    lo = (p - hi.astype(F32)).astype(BF16)
    return jnp.dot(hi, mean, preferred_element_type=F32) + jnp.dot(lo, mean, preferred_element_type=F32)


def _prep_specs(S, width, off, tw):
    assert off % width == 0
    tm = _tile(S, 512, 16)
    x = pl.BlockSpec((tm, width), lambda i: (i, off // width))
    row = pl.BlockSpec((tm, width), lambda i: (i, 0))
    tab = pl.BlockSpec((tm, tw), lambda i: (i, 0))
    vec = pl.BlockSpec((1, width), lambda i: (0, 0))
    mat = pl.BlockSpec((width, width), lambda i: (0, 0))
    return tm, x, row, tab, vec, mat


def _qk_prep(name, proj, off, width, gain_row, consts):
    S = proj.shape[0]
    cos, sin_next, sin_prev, mean, half = consts
    tm, x, row, tab, vec, mat = _prep_specs(S, width, off, cos.shape[1])
    reps = width // cos.shape[1]

    def body(x_ref, g_ref, c_ref, sn_ref, sp_ref, m_ref, o_ref):
        xv = x_ref[...].astype(F32)
        y = xv * lax.rsqrt(_head_mean(xv * xv, m_ref[...]) + RMS_EPS) * g_ref[...]
        t = lambda r: jnp.tile(r[...], (1, reps))
        o_ref[...] = (y * t(c_ref) + pltpu.roll(y, width - half, 1) * t(sn_ref) + pltpu.roll(y, half, 1) * t(sp_ref)).astype(BF16)

    return pl.pallas_call(
        body, name=name, grid=(S // tm,), in_specs=[x, vec, tab, tab, tab, mat], out_specs=row,
        out_shape=jax.ShapeDtypeStruct((S, width), BF16), compiler_params=_cparams(("parallel",)),
    )(proj, gain_row, cos, sin_next, sin_prev, mean)


def _qk_prep_bwd(name, proj, off, width, gain_row, consts, dout):
    S = proj.shape[0]
    cos, sin_next, sin_prev, mean, half = consts
    tm, x, row, tab, vec, mat = _prep_specs(S, width, off, cos.shape[1])
    reps = width // cos.shape[1]

    def body(x_ref, g_ref, c_ref, sn_ref, sp_ref, m_ref, do_ref, dx_ref, dg_ref):
        xv = x_ref[...].astype(F32)
        r = lax.rsqrt(_head_mean(xv * xv, m_ref[...]) + RMS_EPS)
        xhat = xv * r
        dov = do_ref[...]
        t = lambda ref: jnp.tile(ref[...], (1, reps))
        dy = dov * t(c_ref) + pltpu.roll(dov * t(sn_ref), half, 1) + pltpu.roll(dov * t(sp_ref), width - half, 1)
        dxhat = dy * g_ref[...]
        dx_ref[...] = (r * (dxhat - xhat * _head_mean(dxhat * xhat, m_ref[...]))).astype(BF16)

        @pl.when(pl.program_id(0) == 0)
        def _():
            dg_ref[...] = jnp.zeros_like(dg_ref)

        dg_ref[...] += jnp.sum(dy * xhat, axis=0, keepdims=True)

    return pl.pallas_call(
        body, name=name, grid=(S // tm,), in_specs=[x, vec, tab, tab, tab, mat, row], out_specs=(row, vec),
        out_shape=(jax.ShapeDtypeStruct((S, width), BF16), jax.ShapeDtypeStruct((1, width), F32)),
        compiler_params=_cparams(("arbitrary",)),
    )(proj, gain_row, cos, sin_next, sin_prev, mean, dout)


def _attn_probs(q, kp, kc, sink_col, n, scale):
    rows = q.shape[0]
    sp = lax.dot_general(q, kp, NT, preferred_element_type=F32) * scale
    sc = lax.dot_general(q, kc, NT, preferred_element_type=F32) * scale
    qi = lax.broadcasted_iota(jnp.int32, (rows, BLOCK), 0) % BLOCK
    kj = lax.broadcasted_iota(jnp.int32, (rows, BLOCK), 1)
    sp = jnp.where((kj > qi) & (n > 0), sp, NEG_INF)
    sc = jnp.where(kj <= qi, sc, NEG_INF)
    m = jnp.maximum(jnp.maximum(jnp.max(sp, axis=-1, keepdims=True), jnp.max(sc, axis=-1, keepdims=True)), sink_col)
    pp, pc, ps = jnp.exp(sp - m), jnp.exp(sc - m), jnp.exp(sink_col - m)
    inv = 1.0 / (jnp.sum(pp, axis=-1, keepdims=True) + jnp.sum(pc, axis=-1, keepdims=True) + ps)
    return pp * inv, pc * inv, ps * inv


def _sink_col(sink_ref, hk, group):
    rows = group * BLOCK
    g = lax.broadcasted_iota(jnp.int32, (rows, 1), 0) // BLOCK
    col = jnp.zeros((rows, 1), F32)
    for i in range(group):
        col = jnp.where(g == i, sink_ref[hk * group + i], col)
    return col


def _attn_specs(group, S, dh):
    heads = pl.BlockSpec((group, S, dh), lambda hk: (hk, 0, 0))
    kv = pl.BlockSpec((None, S, dh), lambda hk: (hk, 0, 0))
    return heads, kv, pl.BlockSpec(memory_space=pltpu.SMEM)


def _block_rows(n):
    cur = pl.ds(pl.multiple_of(n * BLOCK, BLOCK), BLOCK)
    prev = pl.ds(pl.multiple_of(jnp.maximum(n - 1, 0) * BLOCK, BLOCK), BLOCK)
    return cur, prev


def _attn_fwd(name, q, k, v, sinks):
    HQ, S, dh = q.shape
    HKV = k.shape[0]
    group = HQ // HKV
    scale = dh ** -0.5
    heads, kv, smem = _attn_specs(group, S, dh)

    def body(q_ref, k_ref, v_ref, sink_ref, o_ref):
        sink = _sink_col(sink_ref, pl.program_id(0), group)

        def block(n, carry):
            cur, prev = _block_rows(n)
            qv = q_ref[:, cur, :].reshape(group * BLOCK, dh)
            pp, pc, _ = _attn_probs(qv, k_ref[prev, :], k_ref[cur, :], sink, n, scale)
            o = jnp.dot(pp.astype(BF16), v_ref[prev, :], preferred_element_type=F32)
            o = o + jnp.dot(pc.astype(BF16), v_ref[cur, :], preferred_element_type=F32)
            o_ref[:, cur, :] = o.reshape(group, BLOCK, dh).astype(BF16)
            return carry

        lax.fori_loop(0, S // BLOCK, block, 0)

    return pl.pallas_call(
        body, name=name, grid=(HKV,), in_specs=[heads, kv, kv, smem], out_specs=heads,
        out_shape=jax.ShapeDtypeStruct((HQ, S, dh), BF16), compiler_params=_cparams(("parallel",)),
    )(q, k, v, sinks)


def _attn_bwd(name, q, k, v, sinks, do):
    HQ, S, dh = q.shape
    HKV = k.shape[0]
    group = HQ // HKV
    scale = dh ** -0.5
    heads, kv, smem = _attn_specs(group, S, dh)
    sk = pl.BlockSpec((None, group, LANES), lambda hk: (hk, 0, 0))

    def body(q_ref, k_ref, v_ref, sink_ref, do_ref, dq_ref, dk_ref, dv_ref, ds_ref):
        rows = group * BLOCK
        sink = _sink_col(sink_ref, pl.program_id(0), group)
        dk_ref[...] = jnp.zeros_like(dk_ref)
        dv_ref[...] = jnp.zeros_like(dv_ref)
        tdot = lambda a, b: lax.dot_general(a, b, TN, preferred_element_type=F32)

        def block(n, dsink):
            cur, prev = _block_rows(n)
            qv = q_ref[:, cur, :].reshape(rows, dh)
            dov = do_ref[:, cur, :].reshape(rows, dh)
            kp, kc, vp, vc = k_ref[prev, :], k_ref[cur, :], v_ref[prev, :], v_ref[cur, :]
            pp, pc, ps = _attn_probs(qv, kp, kc, sink, n, scale)
            dpp = lax.dot_general(dov, vp, NT, preferred_element_type=F32)
            dpc = lax.dot_general(dov, vc, NT, preferred_element_type=F32)
            delta = jnp.sum(pp * dpp, axis=-1, keepdims=True) + jnp.sum(pc * dpc, axis=-1, keepdims=True)
            dsp = (pp * (dpp - delta) * scale).astype(BF16)
            dsc = (pc * (dpc - delta) * scale).astype(BF16)
            dq = jnp.dot(dsp, kp, preferred_element_type=F32) + jnp.dot(dsc, kc, preferred_element_type=F32)
            dq_ref[:, cur, :] = dq.reshape(group, BLOCK, dh)
            dk_ref[prev, :] += tdot(dsp, qv)
            dv_ref[prev, :] += tdot(pp.astype(BF16), dov)
            dk_ref[cur, :] += tdot(dsc, qv)
            dv_ref[cur, :] += tdot(pc.astype(BF16), dov)
            return dsink - jnp.sum((ps * delta).reshape(group, BLOCK, 1), axis=1)

        dsink = lax.fori_loop(0, S // BLOCK, block, jnp.zeros((group, 1), F32))
        ds_ref[...] = jnp.broadcast_to(dsink, (group, LANES))

    return pl.pallas_call(
        body, name=name, grid=(HKV,), in_specs=[heads, kv, kv, smem, heads], out_specs=(heads, kv, kv, sk),
        out_shape=(jax.ShapeDtypeStruct((HQ, S, dh), F32), jax.ShapeDtypeStruct((HKV, S, dh), F32),
                   jax.ShapeDtypeStruct((HKV, S, dh), F32), jax.ShapeDtypeStruct((HKV, group, LANES), F32)),
        compiler_params=_cparams(("parallel",)),
    )(q, k, v, sinks, do)


RESIDENT = pl.Buffered(1)


def _gate_blocks(tm, Ns, off):
    assert off % Ns == 0
    return [pl.BlockSpec((tm, Ns), lambda i, k=k: (i, off // Ns + k)) for k in range(N_CHIPS)]


def _mixer_out_fwd(name, aconv, o, woc3, woa3, wo, proj, x1, ga_off, gb_off):
    S, D = x1.shape
    Ns = woc3.shape[2]
    tm = _tile(S, 256, 16)

    def body(a_ref, o_ref, woc_ref, woa_ref, wo_ref, x1_ref, *rest):
        ga_refs, gb_refs = rest[:N_CHIPS], rest[N_CHIPS:2 * N_CHIPS]
        ya_ref, yb_ref, m_ref, x2_ref = rest[2 * N_CHIPS:]
        av, ov = a_ref[...], o_ref[...]
        for s in range(N_CHIPS):
            cols = slice(s * Ns, (s + 1) * Ns)
            ya = jnp.dot(av, woc_ref[s], preferred_element_type=F32)
            yb = jnp.dot(ov, woa_ref[s], preferred_element_type=F32)
            ya_ref[:, cols] = ya.astype(BF16)
            yb_ref[:, cols] = yb.astype(BF16)
            ga, gb = ga_refs[s][...].astype(F32), gb_refs[s][...].astype(F32)
            m_ref[:, cols] = (_sigmoid(ga) * ya + _sigmoid(gb) * yb).astype(BF16)
        x2_ref[...] = x1_ref[...] + jnp.dot(m_ref[...], wo_ref[...], preferred_element_type=F32)

    row = lambda w: pl.BlockSpec((tm, w), lambda i: (i, 0))
    whole3 = lambda a: pl.BlockSpec(a.shape, lambda i: (0, 0, 0), pipeline_mode=RESIDENT)
    act = jax.ShapeDtypeStruct((S, D), BF16)
    return pl.pallas_call(
        body, name=name, grid=(S // tm,),
        in_specs=[row(aconv.shape[1]), row(o.shape[1]), whole3(woc3), whole3(woa3),
                  pl.BlockSpec(wo.shape, lambda i: (0, 0), pipeline_mode=RESIDENT), row(D)]
        + _gate_blocks(tm, Ns, ga_off) + _gate_blocks(tm, Ns, gb_off),
        out_specs=(row(D), row(D), row(D), row(D)), out_shape=(act, act, act, jax.ShapeDtypeStruct((S, D), F32)),
        compiler_params=_cparams(("parallel",)),
    )(aconv, o, woc3, woa3, wo, x1, *([proj] * (2 * N_CHIPS)))


def _mixer_out_bwd(name, dx2_b, wo, ya, yb, proj, woc3, woa3, ga_off, gb_off, dep):
    S, D = dx2_b.shape
    K, Ns = woc3.shape[1], woc3.shape[2]
    tm = _tile(S, 256, 16)

    def body(dx_ref, wo_ref, ya_ref, yb_ref, woc_ref, woa_ref, *rest):
        ga_refs, gb_refs = rest[:N_CHIPS], rest[N_CHIPS:2 * N_CHIPS]
        dga_ref, dgb_ref, dya_ref, dyb_ref, da_ref, do_ref = rest[-6:]
        dm = lax.dot_general(dx_ref[...], wo_ref[...], NT, preferred_element_type=F32)
        da = do = None
        for s in range(N_CHIPS):
            cols = slice(s * Ns, (s + 1) * Ns)
            dms = dm[:, cols]
            sa, sb = _sigmoid(ga_refs[s][...].astype(F32)), _sigmoid(gb_refs[s][...].astype(F32))
            dga_ref[:, cols] = (dms * ya_ref[:, cols].astype(F32) * sa * (1.0 - sa)).astype(BF16)
            dgb_ref[:, cols] = (dms * yb_ref[:, cols].astype(F32) * sb * (1.0 - sb)).astype(BF16)
            dya, dyb = (dms * sa).astype(BF16), (dms * sb).astype(BF16)
            dya_ref[:, cols] = dya
            dyb_ref[:, cols] = dyb
            pa = lax.dot_general(dya, woc_ref[s], NT, preferred_element_type=F32)
            pb = lax.dot_general(dyb, woa_ref[s], NT, preferred_element_type=F32)
            da, do = (pa, pb) if da is None else (da + pa, do + pb)
        da_ref[...] = da
        do_ref[...] = do.astype(BF16)

    row = lambda w: pl.BlockSpec((tm, w), lambda i: (i, 0))
    whole3 = lambda a: pl.BlockSpec(a.shape, lambda i: (0, 0, 0), pipeline_mode=RESIDENT)
    act = jax.ShapeDtypeStruct((S, D), BF16)
    return pl.pallas_call(
        body, name=name, grid=(S // tm,),
        in_specs=[row(D), pl.BlockSpec(wo.shape, lambda i: (0, 0), pipeline_mode=RESIDENT), row(D), row(D), whole3(woc3), whole3(woa3)]
        + _gate_blocks(tm, Ns, ga_off) + _gate_blocks(tm, Ns, gb_off) + [pl.BlockSpec(dep.shape, lambda i: (0, 0))],
        out_specs=(row(D), row(D), row(D), row(D), row(K), row(K)),
        out_shape=(act, act, act, act, jax.ShapeDtypeStruct((S, K), F32), jax.ShapeDtypeStruct((S, K), BF16)),
        compiler_params=_cparams(("parallel",)),
    )(dx2_b, wo, ya, yb, woc3, woa3, *([proj] * (2 * N_CHIPS)), dep)


ANY = pl.BlockSpec(memory_space=pl.ANY)


def _row_tile(rows, cols, n_arrays):
    want = max(16, (VMEM_LIMIT_V7X // 2) // (2 * n_arrays * cols * 4))
    return _tile(rows, want, 16)


def _cast_to_slot(name, w, dtype, p_arr, dep=None):
    R, C = w.shape
    tr = _row_tile(R, C, 2)
    extra = () if dep is None else (dep,)

    def body(p_ref, w_ref, *rest):
        rest[-1][...] = w_ref[...].astype(dtype)

    return pl.pallas_call(
        body, name=name,
        grid_spec=pltpu.PrefetchScalarGridSpec(
            num_scalar_prefetch=1, grid=(R // tr,),
            in_specs=[pl.BlockSpec((tr, C), lambda i, p_ref: (i, 0))] + [pl.BlockSpec(d.shape, lambda i, p_ref: (0, 0)) for d in extra],
            out_specs=pl.BlockSpec((None, tr, C), lambda i, p_ref: (p_ref[0], i, 0))),
        out_shape=jax.ShapeDtypeStruct((N_CHIPS, R, C), dtype), compiler_params=_cparams(("parallel",)),
    )(p_arr, w, *extra)


def _add_half(name, g3, r3, c_arr):
    n, h, C = r3.shape
    tr = _row_tile(h, C, 3)
    nb = h // tr

    def body(c_ref, g_ref, r_ref, o_ref):
        o_ref[...] = (g_ref[...].astype(F32) + r_ref[...].astype(F32)).astype(BF16)

    blk = pl.BlockSpec((None, tr, C), lambda s, i, c_ref: (s, i, 0))
    return pl.pallas_call(
        body, name=name,
        grid_spec=pltpu.PrefetchScalarGridSpec(
            num_scalar_prefetch=1, grid=(n, nb),
            in_specs=[pl.BlockSpec((None, tr, C), lambda s, i, c_ref: (s, c_ref[0] * nb + i, 0)), blk], out_specs=blk),
        out_shape=jax.ShapeDtypeStruct(r3.shape, BF16), compiler_params=_cparams(("parallel", "parallel")),
    )(c_arr, g3, r3)


def _add_chips(name, t3, r3, cp_arr):
    n, h, C = r3.shape
    tr = _row_tile(h, C, 6)
    nb = h // tr

    def body(cp_ref, t_ref, r0_ref, r1_ref, r2_ref, r3_ref, o_ref):
        p = cp_ref[1]
        total = None
        for a, r_ref in enumerate((r0_ref, r1_ref, r2_ref, r3_ref)):
            part = jnp.where(p == a, t_ref[...], r_ref[...]).astype(F32)
            total = part if total is None else total + part
        o_ref[...] = total

    def part(a):
        return pl.BlockSpec((None, tr, C), lambda i, cp_ref: (jnp.where(cp_ref[1] == a, (a + 1) % N_CHIPS, a), i, 0))

    return pl.pallas_call(
        body, name=name,
        grid_spec=pltpu.PrefetchScalarGridSpec(
            num_scalar_prefetch=1, grid=(nb,),
            in_specs=[pl.BlockSpec((None, tr, C), lambda i, cp_ref: (cp_ref[1], i, 0)), part(0), part(1), part(2), part(3)],
            out_specs=pl.BlockSpec((tr, C), lambda i, cp_ref: (cp_ref[0] * nb + i, 0))),
        out_shape=jax.ShapeDtypeStruct((2 * h, C), F32), compiler_params=_cparams(("parallel",)),
    )(cp_arr, t3, r3, r3, r3, r3)


def _adamw(name, w, g, m, v, deps=()):
    R, C = w.shape
    extra = tuple(deps)
    tr = _row_tile(R, C, 8)
    c1 = 1.0 - ADAM_B1 ** ADAM_STEP
    c2 = 1.0 - ADAM_B2 ** ADAM_STEP

    def body(w_ref, g_ref, m_ref, v_ref, *rest):
        go_ref, d_ref, nm_ref, nv_ref = rest[-4:]
        gv = g_ref[...]
        go_ref[...] = gv
        nm = ADAM_B1 * m_ref[...] + (1.0 - ADAM_B1) * gv
        nv = ADAM_B2 * v_ref[...] + (1.0 - ADAM_B2) * (gv * gv)
        d_ref[...] = -ADAM_LR * ((nm / c1) / (jnp.sqrt(nv / c2) + ADAM_EPS) + ADAM_WD * w_ref[...])
        nm_ref[...] = nm
        nv_ref[...] = nv

    blk = pl.BlockSpec((tr, C), lambda i: (i, 0))
    o = jax.ShapeDtypeStruct((R, C), F32)
    return pl.pallas_call(
        body, name=name, grid=(R // tr,), in_specs=[blk, blk, blk, blk] + [ANY] * len(extra), out_specs=(blk, blk, blk, blk),
        out_shape=(o, o, o, o), compiler_params=_cparams(("parallel",)),
    )(w, g, m, v, *extra)


def _place():
    x, y, c = lax.axis_index("x"), lax.axis_index("y"), lax.axis_index("c")
    chips = [(1 - x, y), (x, 1 - y), (1 - x, 1 - y)]
    return x, y, c, 2 * x + y, chips


HBM = pl.BlockSpec(memory_space=pltpu.HBM)
SEM = pl.BlockSpec(memory_space=pltpu.SEMAPHORE)
TOKEN = jax.ShapeDtypeStruct((8, LANES), F32)
DATAFLOW = pltpu.SideEffectType.DATAFLOW_SIDE_EFFECTING


def _hbm(a):
    return pltpu.with_memory_space_constraint(a, pltpu.HBM)


def _gather_blocks(bufs, i, c, p, chips):
    if bufs[i].shape[1] % 16:
        return bufs[i].at[p], [bufs[i].at[2 * cx + cy] for cx, cy in chips]
    h = bufs[i].shape[1] // 2
    rows = pl.ds(pl.multiple_of(c * h, 16), h)
    return bufs[i].at[p, rows], [bufs[i].at[2 * cx + cy, rows] for cx, cy in chips]


def _gather_start(name, groups, dep):
    slots = [s for g in groups for s in g]
    n, ng = len(slots), len(groups)

    def body(*refs):
        bufs, sems, token = refs[:n], refs[n + 1:n + 1 + 2 * ng], refs[-1]
        x, y, c, p, chips = _place()
        i = 0
        for gi, g in enumerate(groups):
            send, recv = sems[2 * gi], sems[2 * gi + 1]
            for k in range(len(g)):
                mine, _ = _gather_blocks(bufs, i, c, p, chips)
                for j, chip in enumerate(chips):
                    pltpu.make_async_remote_copy(src_ref=mine, dst_ref=mine, send_sem=send.at[3 * k + j], recv_sem=recv.at[3 * k + j],
                                                 device_id=(*chip, c), device_id_type=MESH).start()
                i += 1
        token[...] = jnp.zeros_like(token)

    sem_shapes = [pltpu.SemaphoreType.DMA((3 * len(g),)) for g in groups for _ in range(2)]
    out = pl.pallas_call(
        body, name=name, in_specs=[HBM] * n + [ANY],
        out_specs=(*([SEM] * (2 * ng)), *([HBM] * n), pl.BlockSpec(memory_space=pltpu.VMEM)),
        out_shape=(*sem_shapes, *[pltpu.HBM(s.shape, s.dtype) for s in slots], TOKEN),
        input_output_aliases={i: 2 * ng + i for i in range(n)},
        compiler_params=pltpu.CompilerParams(has_side_effects=DATAFLOW),
    )(*[_hbm(s) for s in slots], dep)
    started, i = [], 2 * ng
    for gi, g in enumerate(groups):
        started.append((out[2 * gi], out[2 * gi + 1], list(out[i:i + len(g)])))
        i += len(g)
    return started, out[-1]


def _gather_wait(name, send, recv, slots, after):
    n = len(slots)

    def body(*refs):
        bufs, send, recv = refs[:n], refs[n], refs[n + 1]
        x, y, c, p, chips = _place()
        for i in range(n):
            mine, landed = _gather_blocks(bufs, i, c, p, chips)
            for j, chip in enumerate(chips):
                cp = pltpu.make_async_remote_copy(src_ref=mine, dst_ref=landed[j], send_sem=send.at[3 * i + j],
                                                  recv_sem=recv.at[3 * i + j], device_id=(*chip, c), device_id_type=MESH)
                cp.wait_send()
                cp.wait_recv()

    return list(pl.pallas_call(
        body, name=name, in_specs=[HBM] * n + [SEM, SEM, ANY], out_specs=tuple([HBM] * n),
        out_shape=tuple(pltpu.HBM(s.shape, s.dtype) for s in slots),
        input_output_aliases={i: i for i in range(n)},
        compiler_params=pltpu.CompilerParams(has_side_effects=DATAFLOW),
    )(*slots, send, recv, after))


def _gather_forward(name, slots):
    idx = [i for i, s in enumerate(slots) if s.shape[1] % 16 == 0]
    n = len(slots)

    def body(*refs):
        bufs = refs[n:2 * n]
        send, recv = refs[2 * n:]
        x, y, c, p, chips = _place()

        def rdma(k, ref):
            return pltpu.make_async_remote_copy(src_ref=ref, dst_ref=ref, send_sem=send.at[k], recv_sem=recv.at[k],
                                                device_id=(x, y, 1 - c), device_id_type=MESH)

        cps = []
        for k, i in enumerate(idx):
            for j, ref in enumerate(_gather_blocks(bufs, i, c, p, chips)[1]):
                cps.append(rdma(3 * k + j, ref))
                cps[-1].start()
        for k, i in enumerate(idx):
            for j, ref in enumerate(_gather_blocks(bufs, i, 1 - c, p, chips)[1]):
                rdma(3 * k + j, ref).wait_recv()
        for cp in cps:
            cp.wait_send()

    return list(pl.pallas_call(
        body, name=name, in_specs=[ANY] * n, out_specs=tuple([ANY] * n),
        out_shape=tuple(jax.ShapeDtypeStruct(s.shape, s.dtype) for s in slots),
        scratch_shapes=[pltpu.SemaphoreType.DMA((3 * len(idx),)), pltpu.SemaphoreType.DMA((3 * len(idx),))],
        input_output_aliases={i: i for i in range(n)},
        compiler_params=pltpu.CompilerParams(has_side_effects=True),
    )(*slots))


def _swap_copy(grads, lands, send, recv, i, x, y, c):
    h = grads[i].shape[1] // 2
    other = pl.ds(pl.multiple_of((1 - c) * h, 16), h)
    return pltpu.make_async_remote_copy(src_ref=grads[i].at[:, other, :], dst_ref=lands[i], send_sem=send.at[i],
                                        recv_sem=recv.at[i], device_id=(x, y, 1 - c), device_id_type=MESH)


def _swap_wait(name, send, recv, grads, lands, after):
    n = len(grads)

    def body(*refs):
        ins, lands, send, recv = refs[:n], refs[n:2 * n], refs[2 * n], refs[2 * n + 1]
        x, y, c, p, chips = _place()
        for i in range(n):
            cp = _swap_copy(ins, lands, send, recv, i, x, y, c)
            cp.wait_send()
            cp.wait_recv()

    shapes = [pltpu.HBM(t.shape, t.dtype) for t in list(grads) + list(lands)]
    out = pl.pallas_call(
        body, name=name, in_specs=[HBM] * (2 * n) + [SEM, SEM, ANY], out_specs=tuple([HBM] * (2 * n)),
        out_shape=tuple(shapes), input_output_aliases={i: i for i in range(2 * n)},
        compiler_params=pltpu.CompilerParams(has_side_effects=DATAFLOW),
    )(*grads, *lands, send, recv, after)
    return list(out[:n]), list(out[n:])


def _reduce_starts(name, grads, parts):
    ng, npt = len(grads), len(parts)
    halves = [(g.shape[0], g.shape[1] // 2, g.shape[2]) for g in grads]
    arrays = list(grads) + [lax.empty(s, g.dtype) for s, g in zip(halves, grads)] + list(parts) + [lax.empty(t.shape, t.dtype) for t in parts]
    na = len(arrays)
    sems = ([pltpu.SemaphoreType.DMA((ng,))] * 2 if ng else []) + ([pltpu.SemaphoreType.DMA((3 * npt,))] * 2 if npt else [])
    ns = len(sems)

    def body(*refs):
        ins, sem, token = refs[:na], list(refs[na:na + ns]), refs[-1]
        x, y, c, p, chips = _place()
        if ng:
            for i in range(ng):
                _swap_copy(ins[:ng], ins[ng:2 * ng], sem[0], sem[1], i, x, y, c).start()
        if npt:
            src, land, send, recv = ins[2 * ng:2 * ng + npt], ins[2 * ng + npt:], sem[-2], sem[-1]
            for i in range(npt):
                for j, (cx, cy) in enumerate(chips):
                    pltpu.make_async_remote_copy(src_ref=src[i].at[2 * cx + cy], dst_ref=land[i].at[p], send_sem=send.at[3 * i + j],
                                                 recv_sem=recv.at[3 * i + j], device_id=(cx, cy, c), device_id_type=MESH).start()
        token[...] = jnp.zeros_like(token)

    out = pl.pallas_call(
        body, name=name, in_specs=[HBM] * na,
        out_specs=(*([SEM] * ns), *([HBM] * na), pl.BlockSpec(memory_space=pltpu.VMEM)),
        out_shape=(*sems, *[pltpu.HBM(a.shape, a.dtype) for a in arrays], TOKEN),
        input_output_aliases={i: ns + i for i in range(na)},
        compiler_params=pltpu.CompilerParams(has_side_effects=DATAFLOW),
    )(*[_hbm(a) for a in arrays])
    bufs = list(out[ns:ns + na])
    swap = (out[0], out[1], bufs[:ng], bufs[ng:2 * ng]) if ng else None
    exch = (out[ns - 2], out[ns - 1], bufs[2 * ng:2 * ng + npt], bufs[2 * ng + npt:]) if npt else None
    return swap, exch, out[-1]


def _exchange_wait(name, send, recv, parts, lands, after):
    n = len(parts)

    def body(*refs):
        ins, lands, send, recv = refs[:n], refs[n:2 * n], refs[2 * n], refs[2 * n + 1]
        x, y, c, p, chips = _place()
        for i in range(n):
            for j, (cx, cy) in enumerate(chips):
                q = 2 * cx + cy
                cp = pltpu.make_async_remote_copy(src_ref=ins[i].at[q], dst_ref=lands[i].at[q], send_sem=send.at[3 * i + j],
                                                  recv_sem=recv.at[3 * i + j], device_id=(cx, cy, c), device_id_type=MESH)
                cp.wait_send()
                cp.wait_recv()

    shapes = [pltpu.HBM(t.shape, t.dtype) for t in parts]
    out = pl.pallas_call(
        body, name=name, in_specs=[HBM] * (2 * n) + [SEM, SEM, ANY], out_specs=tuple([HBM] * (2 * n)),
        out_shape=(*shapes, *shapes), input_output_aliases={i: i for i in range(2 * n)},
        compiler_params=pltpu.CompilerParams(has_side_effects=DATAFLOW),
    )(*parts, *lands, send, recv, after)
    return list(out[:n]), list(out[n:])


def _join_copy(buf, send_sem, recv_sem, which, x, y, c):
    h = buf.shape[0] // 2
    rows = buf.at[pl.ds(pl.multiple_of(which * h, 8), h)]
    return pltpu.make_async_remote_copy(src_ref=rows, dst_ref=rows, send_sem=send_sem, recv_sem=recv_sem,
                                        device_id=(x, y, 1 - c), device_id_type=MESH)


def _join_start(name, groups):
    bufs = [b for g in groups for b in g]
    n, ng = len(bufs), len(groups)

    def body(*refs):
        ins, sems, token = refs[:n], refs[n:n + 2 * ng], refs[-1]
        x, y, c, p, chips = _place()
        i = 0
        for gi, g in enumerate(groups):
            for k in range(len(g)):
                _join_copy(ins[i], sems[2 * gi].at[k], sems[2 * gi + 1].at[k], c, x, y, c).start()
                i += 1
        token[...] = jnp.zeros_like(token)

    sem_shapes = [pltpu.SemaphoreType.DMA((len(g),)) for g in groups for _ in range(2)]
    out = pl.pallas_call(
        body, name=name, in_specs=[HBM] * n,
        out_specs=(*([SEM] * (2 * ng)), *([HBM] * n), pl.BlockSpec(memory_space=pltpu.VMEM)),
        out_shape=(*sem_shapes, *[pltpu.HBM(t.shape, t.dtype) for t in bufs], TOKEN),
        input_output_aliases={i: 2 * ng + i for i in range(n)},
        compiler_params=pltpu.CompilerParams(has_side_effects=DATAFLOW),
    )(*[_hbm(t) for t in bufs])
    started, i = [], 2 * ng
    for gi, g in enumerate(groups):
        started.append((out[2 * gi], out[2 * gi + 1], list(out[i:i + len(g)])))
        i += len(g)
    return started, out[-1]


def _join_wait(name, send, recv, bufs, after):
    n = len(bufs)

    def body(*refs):
        ins, send, recv = refs[:n], refs[n], refs[n + 1]
        x, y, c, p, chips = _place()
        for i in range(n):
            _join_copy(ins[i], send.at[i], recv.at[i], c, x, y, c).wait_send()
            _join_copy(ins[i], send.at[i], recv.at[i], 1 - c, x, y, c).wait_recv()

    return list(pl.pallas_call(
        body, name=name, in_specs=[HBM] * n + [SEM, SEM, ANY], out_specs=tuple([HBM] * n),
        out_shape=tuple(pltpu.HBM(t.shape, t.dtype) for t in bufs), input_output_aliases={i: i for i in range(n)},
        compiler_params=pltpu.CompilerParams(has_side_effects=DATAFLOW),
    )(*bufs, send, recv, after))


def _allreduce_small(name, pack, dep):
    R, W = pack.shape

    def body(in_ref, dep_ref, out_ref, slots, send, recv):
        x, y, c = lax.axis_index("x"), lax.axis_index("y"), lax.axis_index("c")
        me = 4 * x + 2 * y + c
        slots[0] = in_ref[...]
        cps = []
        for k in range(1, N_DEV):
            peer = (x ^ (k >> 2), y ^ ((k >> 1) & 1), c ^ (k & 1))
            cp = pltpu.make_async_remote_copy(src_ref=in_ref, dst_ref=slots.at[k], send_sem=send.at[k - 1],
                                              recv_sem=recv.at[k - 1], device_id=peer, device_id_type=MESH)
            cp.start()
            cps.append(cp)
        for cp in cps:
            cp.wait()
        total = slots[me]
        for a in range(1, N_DEV):
            total = total + slots[jnp.bitwise_xor(a, me)]
        out_ref[...] = total

    vmem = pl.BlockSpec(memory_space=pltpu.VMEM)
    return pl.pallas_call(
        body, name=name, in_specs=[vmem, ANY], out_specs=vmem, out_shape=jax.ShapeDtypeStruct((R, W), F32),
        scratch_shapes=[pltpu.VMEM((N_DEV, R, W), F32), pltpu.SemaphoreType.DMA((N_DEV - 1,)), pltpu.SemaphoreType.DMA((N_DEV - 1,))],
        compiler_params=pltpu.CompilerParams(has_side_effects=True),
    )(pack, dep)


def _heads(a, n_heads):
    S = a.shape[0]
    return a.reshape(S, n_heads, a.shape[1] // n_heads).transpose(1, 0, 2)


def _unheads(a):
    H, S, dh = a.shape
    return a.transpose(1, 0, 2).reshape(S, H * dh)


def _ffn_bwd(tag, xin, gain, wgu3, wd, saved, dxout, dxo_b, reduce_start, dep, flush=None):
    h, gu, act = saved
    D = xin.shape[1]
    tok = reduce_start({f"w_down{tag}": _mm_tn(f"dw_down_{tag}", act, dxo_b, 0.5, dep=dep).reshape(N_CHIPS, -1, D)})
    dgu = _ffn_down_bwd(f"ffn_down_bwd_{tag}", dxo_b, wd, gu, 0.5, dep=tok)
    tok = reduce_start({f"w_gu{tag}": _mm_tn_cols(f"dw_gu_{tag}", h, dgu, wgu3.shape[2], b_is_gu=True)})
    if flush is not None:
        tok = flush(tok)
    dh = _mm_nt_cols(f"ffn_up_bwd_{tag}", dgu, wgu3, a_is_gu=True, dep=tok)
    dxin, dxin_b, dgain = _rms_bwd(f"rms_bwd_{tag}", xin, gain, dh, dxout)
    return dxin, dxin_b, dgain, tok


def kernel(x, g_ffn1, w_gu1, w_down1, g_mix, w_in, conv_w, q_norm_g, k_norm_g, sinks, w_out_conv, w_out_attn, w_o, g_ffn2, w_gu2, w_down2, loss_target, m_g_ffn1, m_w_gu1, m_w_down1, m_g_mix, m_w_in, m_conv_w, m_q_norm_g, m_k_norm_g, m_sinks, m_w_out_conv, m_w_out_attn, m_w_o, m_g_ffn2, m_w_gu2, m_w_down2, v_g_ffn1, v_w_gu1, v_w_down1, v_g_mix, v_w_in, v_conv_w, v_q_norm_g, v_k_norm_g, v_sinks, v_w_out_conv, v_w_out_attn, v_w_o, v_g_ffn2, v_w_gu2, v_w_down2):
    S, D = x.shape[1], x.shape[2]
    dh = q_norm_g.shape[1]
    HQ = sinks.shape[1]
    HKV = HQ // 4
    AW, KVW, CW = HQ * dh, HKV * dh, D // 2
    off_q, off_k, off_v = 3 * CW, 3 * CW + AW, 3 * CW + AW + KVW
    off_ga, off_gb = off_v + KVW, off_v + KVW + D
    x0, target = x[0], loss_target[0]
    cx, cy, cc = lax.axis_index("x"), lax.axis_index("y"), lax.axis_index("c")
    chip = 2 * cx + cy
    p_arr = jnp.reshape(chip, (1,)).astype(jnp.int32)
    c_arr = jnp.reshape(cc, (1,)).astype(jnp.int32)
    cp_arr = jnp.stack([cc, chip]).astype(jnp.int32)
    wts = dict(g_ffn1=g_ffn1, w_gu1=w_gu1, w_down1=w_down1, g_mix=g_mix, w_in=w_in, conv_w=conv_w, q_norm_g=q_norm_g,
               k_norm_g=k_norm_g, sinks=sinks, w_out_conv=w_out_conv, w_out_attn=w_out_attn, w_o=w_o, g_ffn2=g_ffn2,
               w_gu2=w_gu2, w_down2=w_down2)
    ms = dict(g_ffn1=m_g_ffn1, w_gu1=m_w_gu1, w_down1=m_w_down1, g_mix=m_g_mix, w_in=m_w_in, conv_w=m_conv_w,
              q_norm_g=m_q_norm_g, k_norm_g=m_k_norm_g, sinks=m_sinks, w_out_conv=m_w_out_conv, w_out_attn=m_w_out_attn,
              w_o=m_w_o, g_ffn2=m_g_ffn2, w_gu2=m_w_gu2, w_down2=m_w_down2)
    vs = dict(g_ffn1=v_g_ffn1, w_gu1=v_w_gu1, w_down1=v_w_down1, g_mix=v_g_mix, w_in=v_w_in, conv_w=v_conv_w,
              q_norm_g=v_q_norm_g, k_norm_g=v_k_norm_g, sinks=v_sinks, w_out_conv=v_w_out_conv, w_out_attn=v_w_out_attn,
              w_o=v_w_o, g_ffn2=v_g_ffn2, w_gu2=v_w_gu2, w_down2=v_w_down2)
    order = list(wts)
    small_names = [k for k in order if not k.startswith("w_")]
    grad, delta, new_m, new_v = {}, {}, {}, {}

    def cast(keys, dep=None):
        return [_cast_to_slot(f"cast_{k}", wts[k][0], F32 if k == "conv_w" else BF16, p_arr, dep) for k in keys]

    def gather_finish(tag, started, after):
        send, recv, slots = started
        return _gather_forward(f"gather_forward_{tag}", _gather_wait(f"gather_wait_{tag}", send, recv, slots, after))

    swapping, pending = [], []

    def reduce_start(full, after=None):
        keys = [] if full is None else list(full)
        pkeys, parts = [], []
        if swapping:
            pkeys, send, recv, gs, lands = swapping.pop(0)
            gs, sib = _swap_wait(f"swap_wait_{pkeys[0]}", send, recv, gs, lands, after if full is None else full[keys[0]])
            parts = [_add_half(f"add_half_{k}", g, r, c_arr) for k, g, r in zip(pkeys, gs, sib)]
        swap, exch, tok = _reduce_starts(f"reduce_starts_{keys[0] if keys else 'last'}", [full[k] for k in keys], parts)
        if exch:
            pending.append((pkeys, *exch))
        if swap:
            swapping.append((keys, *swap))
        return tok

    def reduce_finish(entries, after):
        ready = []
        for keys, send, recv, parts, lands in entries:
            parts, lands = _exchange_wait(f"exchange_wait_{keys[0]}", send, recv, parts, lands, after)
            ready.append((keys, [_add_chips(f"add_chips_{k}", t, r, cp_arr) for k, t, r in zip(keys, parts, lands)]))
        started, last = _join_start(f"join_start_{ready[0][0][0]}", [halves for _, halves in ready])
        for (keys, _), (send, recv, halves) in zip(ready, started):
            for k, g2 in zip(keys, _join_wait(f"join_wait_{keys[0]}", send, recv, halves, last)):
                g2, d, nm, nv = _adamw(f"adamw_{k}", wts[k][0], g2, ms[k][0], vs[k][0], (last,))
                grad[k], delta[k], new_m[k], new_v[k] = g2[None], d[None], nm[None], nv[None]
                last = nv
        return last

    (st_gu1, st_d1), tok = _gather_start("gather_start_1", [cast(["w_gu1"]), cast(["w_down1"])], x0)
    later = ["w_in", "conv_w", "w_out_conv", "w_out_attn", "w_o", "w_gu2", "w_down2"]
    slot = dict(zip(later, cast(later, tok)))
    h1 = _rms_fwd("rms_fwd_1", x0, g_ffn1, slot["w_down2"])
    wgu1, = gather_finish("gu1", st_gu1, h1)
    (st_in, st_out, st_gu2, st_d2), tok = _gather_start(
        "gather_start_2", [[slot["w_in"], slot["conv_w"]], [slot["w_out_conv"], slot["w_out_attn"], slot["w_o"]],
                           [slot["w_gu2"]], [slot["w_down2"]]], wgu1)
    q_consts, k_consts = _prep_consts(S, dh, AW), _prep_consts(S, dh, KVW)
    qg_row, kg_row = jnp.tile(q_norm_g, (1, HQ)), jnp.tile(k_norm_g, (1, HKV))
    sink_vec = sinks[0]

    gu1, act1 = _ffn_up("ffn_up_1", h1, wgu1, tok)
    wd1 = gather_finish("d1", st_d1, act1)[0].reshape(-1, D)
    x1 = _mm_res("ffn_down_1", act1, wd1, x0, 0.5)
    win3, convw3 = gather_finish("in", st_in, x1)
    h2 = _rms_fwd("rms_fwd_mix", x1, g_mix)
    proj = _mm_cols("in_proj", h2, win3, BF16)
    aconv = _conv_fwd("conv_fwd", proj, convw3, CW)
    woc3, woa3, wo = gather_finish("out", st_out, aconv)
    wo = wo.reshape(-1, D)
    vh = _heads(proj[:, off_v:off_v + KVW], HKV)
    qn = _heads(_qk_prep("q_prep", proj, off_q, AW, qg_row, q_consts), HQ)
    kn = _heads(_qk_prep("k_prep", proj, off_k, KVW, kg_row, k_consts), HKV)
    oh = _attn_fwd("attn_fwd", qn, kn, vh, sink_vec)
    o = _unheads(oh)
    ya, yb, merged, x2 = _mixer_out_fwd("mixer_out", aconv, o, woc3, woa3, wo, proj, x1, off_ga, off_gb)
    wgu2, = gather_finish("gu2", st_gu2, x2)
    h3 = _rms_fwd("rms_fwd_2", x2, g_ffn2)
    gu2, act2 = _ffn_up("ffn_up_2", h3, wgu2)
    wd2 = gather_finish("d2", st_d2, act2)[0].reshape(-1, D)
    x3 = _mm_res("ffn_down_2", act2, wd2, x2, 0.5)

    dy, dy_b, loss_lanes = _loss_grad("loss_grad", x3, target)
    dx2, dx2_b, dg_ffn2, tok = _ffn_bwd("2", x2, g_ffn2, wgu2, wd2, (h3, gu2, act2), dy, dy_b, reduce_start, None)
    tok = reduce_start(dict(w_o=_mm_tn("dw_o", merged, dx2_b, dep=tok).reshape(N_CHIPS, -1, D)))
    dga, dgb, dya, dyb, daconv, do = _mixer_out_bwd("mixer_out_bwd", dx2_b, wo, ya, yb, proj, woc3, woa3, off_ga, off_gb, tok)
    dwoc = _mm_tn_cols("dw_out_conv", aconv, dya, woc3.shape[2])
    dwoa = _mm_tn_cols("dw_out_attn", o, dyb, woa3.shape[2])
    tok = reduce_start(dict(w_out_conv=dwoc, w_out_attn=dwoa))
    dxc, dbg, dcg, dconvw = _conv_bwd("conv_bwd", proj, convw3, daconv, CW)
    dqn, dkn, dvh, dsink3 = _attn_bwd("attn_bwd", qn, kn, vh, sink_vec, _heads(do, HQ))
    dq_raw, dqg = _qk_prep_bwd("q_prep_bwd", proj, off_q, AW, qg_row, q_consts, _unheads(dqn))
    dk_raw, dkg = _qk_prep_bwd("k_prep_bwd", proj, off_k, KVW, kg_row, k_consts, _unheads(dkn))
    dqg, dkg = dqg.reshape(HQ, dh).sum(axis=0, keepdims=True), dkg.reshape(HKV, dh).sum(axis=0, keepdims=True)
    dproj = jnp.concatenate([dxc, dbg, dcg, dq_raw, dk_raw, _unheads(dvh).astype(BF16), dga, dgb], axis=1)
    dh2 = _mm_nt_cols("in_proj_bwd", dproj, win3, dep=tok)
    tok = reduce_start(dict(w_in=_mm_tn_cols("dw_in", h2, dproj, win3.shape[2])))
    dx1, dx1_b, dg_mix = _rms_bwd("rms_bwd_mix", x1, g_mix, dh2, dx2)
    dx0, _, dg_ffn1, tok = _ffn_bwd("1", x0, g_ffn1, wgu1, wd1, (h1, gu1, act1), dx1, dx1_b, reduce_start, tok, lambda after: reduce_start(None, after))

    def rows8(a):
        a = a.reshape(-1, a.shape[-1])
        return jnp.pad(a, ((0, -a.shape[0] % 8), (0, D - a.shape[1])))

    misc = jnp.concatenate([dqg, dkg, dsink3[:, :, 0].reshape(1, HQ), loss_lanes], axis=1)
    done = reduce_finish(pending[:-2], dx0)
    tot = _allreduce_small("allreduce_small", jnp.concatenate([rows8(a) for a in (dg_ffn1, dg_mix, dg_ffn2, dconvw, misc)], axis=0), done)
    reduce_finish(pending[-2:], tot)

    cw_s = conv_w.shape[2]
    conv_row0, misc_row = 24, 24 + (-(-N_CHIPS * CONV_K // 8)) * 8
    small_g = dict(g_ffn1=tot[0:1], g_mix=tot[8:9], g_ffn2=tot[16:17],
                   conv_w=lax.dynamic_slice(tot, (conv_row0 + CONV_K * chip, 0), (CONV_K, cw_s)),
                   q_norm_g=tot[misc_row:misc_row + 1, 0:dh], k_norm_g=tot[misc_row:misc_row + 1, dh:2 * dh],
                   sinks=tot[misc_row:misc_row + 1, 2 * dh:2 * dh + HQ])
    loss = (0.5 / D) * jnp.sum(tot[misc_row, 2 * dh + HQ:2 * dh + HQ + LANES])

    def small_pack(src):
        return jnp.concatenate([rows8(src[k]) for k in small_names], axis=0)

    _, sd, sm, sv = _adamw("adamw_small", small_pack(wts), small_pack(small_g), small_pack(ms), small_pack(vs))
    for i, k in enumerate(small_names):
        shape = wts[k].shape
        nr, ncol = math.prod(shape[:-1]), shape[-1]
        grad[k] = small_g[k].reshape(shape)
        delta[k], new_m[k], new_v[k] = (a[8 * i:8 * i + nr, 0:ncol].reshape(shape) for a in (sd, sm, sv))
    return (loss, dx0[None], *[grad[k] for k in order], *[delta[k] for k in order],
            *[new_m[k] for k in order], *[new_v[k] for k in order])
```

```python
import math

import numpy as np
import jax
import jax.numpy as jnp
from jax import lax
from jax.experimental import pallas as pl
from jax.experimental.pallas import tpu as pltpu

F32 = jnp.float32
BF16 = jnp.bfloat16
MESH = pl.DeviceIdType.MESH

RMS_EPS = 1e-6
BLOCK = 128
ROPE_THETA = 500000.0
NEG_INF = -1e30
CONV_K = 3
ADAM_LR, ADAM_B1, ADAM_B2, ADAM_EPS, ADAM_WD, ADAM_STEP = 0.001, 0.9, 0.999, 1e-08, 0.01, 10

VMEM_LIMIT_V7X = 56 * 1024 * 1024
LANES = 128
N_CHIPS = 4
N_DEV = 8


def _tile(n, want, align=LANES):
    best = None
    t = align
    while t <= min(n, want):
        if n % t == 0:
            best = t
        t += align
    return best or n


def _cparams(sem):
    return pltpu.CompilerParams(dimension_semantics=sem, vmem_limit_bytes=VMEM_LIMIT_V7X)


def _sigmoid(x):
    return 1.0 / (1.0 + jnp.exp(-x))


NN = (((1,), (0,)), ((), ()))
NT = (((1,), (1,)), ((), ()))
TN = (((0,), (0,)), ((), ()))


def _mm(name, grid, ins, in_specs, compute, out_shape, out_specs, epilogue, dep=None):
    if dep is not None:
        ins, in_specs = tuple(ins) + (dep,), list(in_specs) + [pl.BlockSpec(dep.shape, lambda *_: (0, 0))]
    n_in = len(ins)

    def body(*refs):
        epilogue(compute(refs[:n_in]), refs[:n_in], refs[n_in:])

    return pl.pallas_call(
        body, name=name, grid=grid, in_specs=in_specs, out_specs=out_specs, out_shape=out_shape,
        compiler_params=_cparams(("parallel", "arbitrary")),
    )(*ins)


def _dot(dims, a=0, b=1):
    return lambda refs: [lax.dot_general(refs[a][...], refs[b][...], dims, preferred_element_type=F32)]


def _ffn_up(name, h, wgu3, dep=None):
    S, D = h.shape
    Ns = wgu3.shape[2]
    F = 2 * Ns
    tm, tn = _tile(S, 512), _tile(Ns, 1408)
    nbs = Ns // tn

    def compute(refs):
        hv = refs[0][...]
        return [jnp.dot(hv, refs[1][...], preferred_element_type=F32), jnp.dot(hv, refs[2][...], preferred_element_type=F32)]

    def epi(accs, in_refs, out_refs):
        g, u = accs
        dgu_ref, a_ref = out_refs
        sg = _sigmoid(g)
        silu = g * sg
        dgu_ref[0] = (u * (sg * (1.0 + g * (1.0 - sg)))).astype(BF16)
        dgu_ref[1] = silu.astype(BF16)
        a_ref[...] = (silu * u).astype(BF16)

    return _mm(
        name, (F // tn, S // tm), (h, wgu3, wgu3),
        [pl.BlockSpec((tm, D), lambda j, i: (i, 0)),
         pl.BlockSpec((None, D, tn), lambda j, i: (j // nbs, 0, j % nbs)),
         pl.BlockSpec((None, D, tn), lambda j, i: (2 + j // nbs, 0, j % nbs))],
        compute, (jax.ShapeDtypeStruct((2, S, F), BF16), jax.ShapeDtypeStruct((S, F), BF16)),
        (pl.BlockSpec((2, tm, tn), lambda j, i: (0, i, j)), pl.BlockSpec((tm, tn), lambda j, i: (i, j))), epi, dep=dep)


def _mm_res(name, a, w, res, scale):
    S, K = a.shape
    N = w.shape[1]
    tm, tn = _tile(S, 512), _tile(N, 512 if K > 2816 else 1024)

    def epi(accs, in_refs, out_refs):
        out_refs[0][...] = in_refs[2][...] + scale * accs[0]

    return _mm(
        name, (N // tn, S // tm), (a, w, res),
        [pl.BlockSpec((tm, K), lambda j, i: (i, 0)), pl.BlockSpec((K, tn), lambda j, i: (0, j)),
         pl.BlockSpec((tm, tn), lambda j, i: (i, j))],
        _dot(NN), jax.ShapeDtypeStruct((S, N), F32), pl.BlockSpec((tm, tn), lambda j, i: (i, j)), epi)


def _mm_cols(name, a, w3, out_dtype):
    S, K = a.shape
    Ns = w3.shape[2]
    tm, tn = _tile(S, 512), _tile(Ns, 2304)
    nbs = Ns // tn

    def epi(accs, in_refs, out_refs):
        out_refs[0][...] = accs[0].astype(out_dtype)

    return _mm(
        name, (N_CHIPS * nbs, S // tm), (a, w3),
        [pl.BlockSpec((tm, K), lambda j, i: (i, 0)),
         pl.BlockSpec((None, K, tn), lambda j, i: (j // nbs, 0, j % nbs))],
        _dot(NN), jax.ShapeDtypeStruct((S, N_CHIPS * Ns), out_dtype), pl.BlockSpec((tm, tn), lambda j, i: (i, j)), epi)


def _ffn_down_bwd(name, dy, wd, gu, scale, dep=None):
    S, D = dy.shape
    F = wd.shape[0]
    tm, tn = _tile(S, 512), _tile(F, 1408)

    def epi(accs, in_refs, out_refs):
        da = scale * accs[0]
        out_refs[0][0] = (da * in_refs[2][0].astype(F32)).astype(BF16)
        out_refs[0][1] = (da * in_refs[2][1].astype(F32)).astype(BF16)

    return _mm(
        name, (F // tn, S // tm), (dy, wd, gu),
        [pl.BlockSpec((tm, D), lambda j, i: (i, 0)), pl.BlockSpec((tn, D), lambda j, i: (j, 0)),
         pl.BlockSpec((2, tm, tn), lambda j, i: (0, i, j))],
        _dot(NT), jax.ShapeDtypeStruct((2, S, F), BF16), pl.BlockSpec((2, tm, tn), lambda j, i: (0, i, j)), epi, dep=dep)


def _mm_nt_cols(name, a, w3, a_is_gu=False, dep=None):
    K, Ns = w3.shape[1], w3.shape[2]
    S = a.shape[1] if a_is_gu else a.shape[0]
    tm = _tile(S, 512)
    tn = _tile(K, max(LANES, (12 << 20) // (N_CHIPS * Ns * 2)))
    if a_is_gu:
        a_spec = pl.BlockSpec((2, tm, 2 * Ns), lambda i, j: (0, i, 0))
        part = lambda a_ref, s: a_ref[s // 2, :, (s % 2) * Ns:(s % 2 + 1) * Ns]
    else:
        a_spec = pl.BlockSpec((tm, N_CHIPS * Ns), lambda i, j: (i, 0))
        part = lambda a_ref, s: a_ref[:, s * Ns:(s + 1) * Ns]

    def compute(refs):
        total = None
        for s in range(N_CHIPS):
            prod = lax.dot_general(part(refs[0], s), refs[1][s], NT, preferred_element_type=F32)
            total = prod if total is None else total + prod
        return [total]

    def epi(accs, in_refs, out_refs):
        out_refs[0][...] = accs[0]

    return _mm(
        name, (S // tm, K // tn), (a, w3), [a_spec, pl.BlockSpec((N_CHIPS, tn, Ns), lambda i, j: (0, j, 0))],
        compute, jax.ShapeDtypeStruct((S, K), F32), pl.BlockSpec((tm, tn), lambda i, j: (i, j)), epi, dep=dep)


def _mm_tn(name, a, b, scale=1.0, dep=None):
    S, K = a.shape
    N = b.shape[1]
    tm, tn = _tile(K, 1024), _tile(N, 1024)

    def epi(accs, in_refs, out_refs):
        out_refs[0][...] = (scale * accs[0]).astype(BF16)

    return _mm(
        name, (N // tn, K // tm), (a, b),
        [pl.BlockSpec((S, tm), lambda j, i: (0, i)), pl.BlockSpec((S, tn), lambda j, i: (0, j))],
        _dot(TN), jax.ShapeDtypeStruct((K, N), BF16), pl.BlockSpec((tm, tn), lambda j, i: (i, j)), epi, dep=dep)


def _mm_tn_cols(name, a, b, Ns, b_is_gu=False, dep=None):
    S, K = a.shape
    tm, tn = _tile(K, 1024), _tile(Ns, 2304)
    nbs = Ns // tn
    if b_is_gu:
        b_spec = pl.BlockSpec((None, S, tn), lambda j, i: (j // (2 * nbs), 0, j % (2 * nbs)))
    else:
        b_spec = pl.BlockSpec((S, tn), lambda j, i: (0, j))

    def epi(accs, in_refs, out_refs):
        out_refs[0][...] = accs[0].astype(BF16)

    return _mm(
        name, (N_CHIPS * nbs, K // tm), (a, b), [pl.BlockSpec((S, tm), lambda j, i: (0, i)), b_spec],
        _dot(TN), jax.ShapeDtypeStruct((N_CHIPS, K, Ns), BF16),
        pl.BlockSpec((None, tm, tn), lambda j, i: (j // nbs, i, j % nbs)), epi, dep=dep)


def _rms_fwd(name, x, gain, dep=None):
    S, D = x.shape
    tm = _tile(S, 256, 8)
    extra = () if dep is None else (dep,)

    def body(x_ref, g_ref, *rest):
        h_ref = rest[-1]
        xv = x_ref[...]
        r = lax.rsqrt(jnp.mean(xv * xv, axis=-1, keepdims=True) + RMS_EPS)
        h_ref[...] = (xv * r * g_ref[...]).astype(BF16)

    return pl.pallas_call(
        body, name=name, grid=(S // tm,),
        in_specs=[pl.BlockSpec((tm, D), lambda i: (i, 0)), pl.BlockSpec((1, D), lambda i: (0, 0))]
        + [pl.BlockSpec(memory_space=pl.ANY) for d in extra],
        out_specs=pl.BlockSpec((tm, D), lambda i: (i, 0)), out_shape=jax.ShapeDtypeStruct((S, D), BF16),
        compiler_params=_cparams(("parallel",)),
    )(x, gain, *extra)


def _rms_bwd(name, x, gain, dh, dres):
    S, D = x.shape
    tm = _tile(S, 256, 8)

    def body(x_ref, g_ref, dh_ref, dres_ref, dx_ref, dxb_ref, dg_ref):
        i = pl.program_id(0)
        xv = x_ref[...]
        r = lax.rsqrt(jnp.mean(xv * xv, axis=-1, keepdims=True) + RMS_EPS)
        xhat = xv * r
        dhv = dh_ref[...]
        dxhat = dhv * g_ref[...]
        dx = dres_ref[...] + r * (dxhat - xhat * jnp.mean(dxhat * xhat, axis=-1, keepdims=True))
        dx_ref[...] = dx
        dxb_ref[...] = dx.astype(BF16)

        @pl.when(i == 0)
        def _():
            dg_ref[...] = jnp.zeros_like(dg_ref)

        dg_ref[...] += jnp.sum(dhv * xhat, axis=0, keepdims=True)

    row = pl.BlockSpec((tm, D), lambda i: (i, 0))
    vec = pl.BlockSpec((1, D), lambda i: (0, 0))
    return pl.pallas_call(
        body, name=name, grid=(S // tm,), in_specs=[row, vec, row, row], out_specs=(row, row, vec),
        out_shape=(jax.ShapeDtypeStruct((S, D), F32), jax.ShapeDtypeStruct((S, D), BF16), jax.ShapeDtypeStruct((1, D), F32)),
        compiler_params=_cparams(("arbitrary",)),
    )(x, gain, dh, dres)


def _loss_grad(name, y, target):
    S, D = y.shape
    tm = _tile(S, 256, 8)

    def body(y_ref, t_ref, dy_ref, dyb_ref, l_ref):
        i = pl.program_id(0)
        e = y_ref[...] - t_ref[...]
        dy_ref[...] = e * (1.0 / D)
        dyb_ref[...] = (e * (1.0 / D)).astype(BF16)
        col = jnp.sum(e * e, axis=0, keepdims=True)
        part = col[:, 0:LANES]
        for k in range(1, D // LANES):
            part = part + col[:, k * LANES:(k + 1) * LANES]

        @pl.when(i == 0)
        def _():
            l_ref[...] = jnp.zeros_like(l_ref)

        l_ref[...] += part

    row = pl.BlockSpec((tm, D), lambda i: (i, 0))
    return pl.pallas_call(
        body, name=name, grid=(S // tm,), in_specs=[row, row],
        out_specs=(row, row, pl.BlockSpec((1, LANES), lambda i: (0, 0))),
        out_shape=(jax.ShapeDtypeStruct((S, D), F32), jax.ShapeDtypeStruct((S, D), BF16), jax.ShapeDtypeStruct((1, LANES), F32)),
        compiler_params=_cparams(("arbitrary",)),
    )(y, target)


def _shift_down(u, k):
    rows = lax.broadcasted_iota(jnp.int32, u.shape, 0)
    return jnp.where(rows >= k, pltpu.roll(u, k, 0), 0.0)


def _shift_up(u, k):
    n = u.shape[0]
    rows = lax.broadcasted_iota(jnp.int32, u.shape, 0)
    return jnp.where(rows < n - k, pltpu.roll(u, n - k, 0), 0.0)


def _conv_specs(S, cw, conv_width):
    nb = conv_width // cw
    col = lambda off: pl.BlockSpec((S, cw), lambda j, off=off: (0, off * nb + j))
    return nb, col(0), col(1), col(2)


def _conv_fwd(name, proj, convw3, conv_width):
    S = proj.shape[0]
    cw = convw3.shape[2]
    nb, xc_s, bg_s, cg_s = _conv_specs(S, cw, conv_width)

    def body(xc_ref, bg_ref, cg_ref, w_ref, o_ref):
        u = cg_ref[...].astype(F32) * xc_ref[...].astype(F32)
        w = w_ref[...]
        cv = w[2:3, :] * u + w[1:2, :] * _shift_down(u, 1) + w[0:1, :] * _shift_down(u, 2)
        o_ref[...] = (bg_ref[...].astype(F32) * cv).astype(BF16)

    return pl.pallas_call(
        body, name=name, grid=(nb,),
        in_specs=[xc_s, bg_s, cg_s, pl.BlockSpec((None, CONV_K, cw), lambda j: (j, 0, 0))],
        out_specs=pl.BlockSpec((S, cw), lambda j: (0, j)), out_shape=jax.ShapeDtypeStruct((S, conv_width), BF16),
        compiler_params=_cparams(("parallel",)),
    )(proj, proj, proj, convw3)


def _conv_bwd(name, proj, convw3, da, conv_width):
    S = proj.shape[0]
    cw = convw3.shape[2]
    nb, xc_s, bg_s, cg_s = _conv_specs(S, cw, conv_width)

    def body(xc_ref, bg_ref, cg_ref, w_ref, da_ref, dxc_ref, dbg_ref, dcg_ref, dw_ref):
        xc, cg = xc_ref[...].astype(F32), cg_ref[...].astype(F32)
        u = cg * xc
        w = w_ref[...]
        u1, u2 = _shift_down(u, 1), _shift_down(u, 2)
        cv = w[2:3, :] * u + w[1:2, :] * u1 + w[0:1, :] * u2
        dav = da_ref[...]
        dbg_ref[...] = (dav * cv).astype(BF16)
        dcv = dav * bg_ref[...].astype(F32)
        du = w[2:3, :] * dcv + w[1:2, :] * _shift_up(dcv, 1) + w[0:1, :] * _shift_up(dcv, 2)
        dxc_ref[...] = (du * cg).astype(BF16)
        dcg_ref[...] = (du * xc).astype(BF16)
        dw_ref[0:1, :] = jnp.sum(dcv * u2, axis=0, keepdims=True)
        dw_ref[1:2, :] = jnp.sum(dcv * u1, axis=0, keepdims=True)
        dw_ref[2:3, :] = jnp.sum(dcv * u, axis=0, keepdims=True)

    wspec = pl.BlockSpec((None, CONV_K, cw), lambda j: (j, 0, 0))
    ospec = pl.BlockSpec((S, cw), lambda j: (0, j))
    act = jax.ShapeDtypeStruct((S, conv_width), BF16)
    return pl.pallas_call(
        body, name=name, grid=(nb,), in_specs=[xc_s, bg_s, cg_s, wspec, ospec],
        out_specs=(ospec, ospec, ospec, wspec),
        out_shape=(act, act, act, jax.ShapeDtypeStruct(convw3.shape, F32)),
        compiler_params=_cparams(("parallel",)),
    )(proj, proj, proj, convw3, da)


def _prep_consts(S, dh, width):
    rot = dh // 4
    half = rot // 2
    inv_freq = 1.0 / (ROPE_THETA ** (jnp.arange(0, rot, 2, dtype=F32) / rot))
    ang = jnp.arange(S, dtype=F32)[:, None] * inv_freq[None, :]
    zeros = jnp.zeros((S, dh - rot), F32)
    cos = jnp.concatenate([jnp.cos(ang), jnp.cos(ang), 1.0 + zeros], axis=1)
    sin_next = jnp.concatenate([-jnp.sin(ang), 0.0 * ang, zeros], axis=1)
    sin_prev = jnp.concatenate([0.0 * ang, jnp.sin(ang), zeros], axis=1)
    reps = min(LANES, width) // dh
    tables = [jnp.tile(t, (1, reps)) for t in (cos, sin_next, sin_prev)]
    mean = np.kron(np.eye(width // dh, dtype=np.float32), np.full((dh, dh), 1.0 / dh, np.float32))
    return (*tables, jnp.asarray(mean, BF16), half)


def _head_mean(p, mean):
    hi = p.astype(BF16)
    lo = (p - hi.astype(F32)).astype(BF16)
    return jnp.dot(hi, mean, preferred_element_type=F32) + jnp.dot(lo, mean, preferred_element_type=F32)


def _prep_specs(S, width, off, tw):
    assert off % width == 0
    tm = _tile(S, 512, 16)
    x = pl.BlockSpec((tm, width), lambda i: (i, off // width))
    row = pl.BlockSpec((tm, width), lambda i: (i, 0))
    tab = pl.BlockSpec((tm, tw), lambda i: (i, 0))
    vec = pl.BlockSpec((1, width), lambda i: (0, 0))
    mat = pl.BlockSpec((width, width), lambda i: (0, 0))
    return tm, x, row, tab, vec, mat


def _qk_prep(name, proj, off, width, gain_row, consts):
    S = proj.shape[0]
    cos, sin_next, sin_prev, mean, half = consts
    tm, x, row, tab, vec, mat = _prep_specs(S, width, off, cos.shape[1])
    reps = width // cos.shape[1]

    def body(x_ref, g_ref, c_ref, sn_ref, sp_ref, m_ref, o_ref):
        xv = x_ref[...].astype(F32)
        y = xv * lax.rsqrt(_head_mean(xv * xv, m_ref[...]) + RMS_EPS) * g_ref[...]
        t = lambda r: jnp.tile(r[...], (1, reps))
        o_ref[...] = (y * t(c_ref) + pltpu.roll(y, width - half, 1) * t(sn_ref) + pltpu.roll(y, half, 1) * t(sp_ref)).astype(BF16)

    return pl.pallas_call(
        body, name=name, grid=(S // tm,), in_specs=[x, vec, tab, tab, tab, mat], out_specs=row,
        out_shape=jax.ShapeDtypeStruct((S, width), BF16), compiler_params=_cparams(("parallel",)),
    )(proj, gain_row, cos, sin_next, sin_prev, mean)


def _qk_prep_bwd(name, proj, off, width, gain_row, consts, dout):
    S = proj.shape[0]
    cos, sin_next, sin_prev, mean, half = consts
    tm, x, row, tab, vec, mat = _prep_specs(S, width, off, cos.shape[1])
    reps = width // cos.shape[1]

    def body(x_ref, g_ref, c_ref, sn_ref, sp_ref, m_ref, do_ref, dx_ref, dg_ref):
        xv = x_ref[...].astype(F32)
        r = lax.rsqrt(_head_mean(xv * xv, m_ref[...]) + RMS_EPS)
        xhat = xv * r
        dov = do_ref[...]
        t = lambda ref: jnp.tile(ref[...], (1, reps))
        dy = dov * t(c_ref) + pltpu.roll(dov * t(sn_ref), half, 1) + pltpu.roll(dov * t(sp_ref), width - half, 1)
        dxhat = dy * g_ref[...]
        dx_ref[...] = (r * (dxhat - xhat * _head_mean(dxhat * xhat, m_ref[...]))).astype(BF16)

        @pl.when(pl.program_id(0) == 0)
        def _():
            dg_ref[...] = jnp.zeros_like(dg_ref)

        dg_ref[...] += jnp.sum(dy * xhat, axis=0, keepdims=True)

    return pl.pallas_call(
        body, name=name, grid=(S // tm,), in_specs=[x, vec, tab, tab, tab, mat, row], out_specs=(row, vec),
        out_shape=(jax.ShapeDtypeStruct((S, width), BF16), jax.ShapeDtypeStruct((1, width), F32)),
        compiler_params=_cparams(("arbitrary",)),
    )(proj, gain_row, cos, sin_next, sin_prev, mean, dout)


def _attn_probs(q, kp, kc, sink_col, n, scale):
    rows = q.shape[0]
    sp = lax.dot_general(q, kp, NT, preferred_element_type=F32) * scale
    sc = lax.dot_general(q, kc, NT, preferred_element_type=F32) * scale
    qi = lax.broadcasted_iota(jnp.int32, (rows, BLOCK), 0) % BLOCK
    kj = lax.broadcasted_iota(jnp.int32, (rows, BLOCK), 1)
    sp = jnp.where((kj > qi) & (n > 0), sp, NEG_INF)
    sc = jnp.where(kj <= qi, sc, NEG_INF)
    m = jnp.maximum(jnp.maximum(jnp.max(sp, axis=-1, keepdims=True), jnp.max(sc, axis=-1, keepdims=True)), sink_col)
    pp, pc, ps = jnp.exp(sp - m), jnp.exp(sc - m), jnp.exp(sink_col - m)
    inv = 1.0 / (jnp.sum(pp, axis=-1, keepdims=True) + jnp.sum(pc, axis=-1, keepdims=True) + ps)
    return pp * inv, pc * inv, ps * inv


def _sink_col(sink_ref, hk, group):
    rows = group * BLOCK
    g = lax.broadcasted_iota(jnp.int32, (rows, 1), 0) // BLOCK
    col = jnp.zeros((rows, 1), F32)
    for i in range(group):
        col = jnp.where(g == i, sink_ref[hk * group + i], col)
    return col


def _attn_specs(group, S, dh):
    heads = pl.BlockSpec((group, S, dh), lambda hk: (hk, 0, 0))
    kv = pl.BlockSpec((None, S, dh), lambda hk: (hk, 0, 0))
    return heads, kv, pl.BlockSpec(memory_space=pltpu.SMEM)


def _block_rows(n):
    cur = pl.ds(pl.multiple_of(n * BLOCK, BLOCK), BLOCK)
    prev = pl.ds(pl.multiple_of(jnp.maximum(n - 1, 0) * BLOCK, BLOCK), BLOCK)
    return cur, prev


def _attn_fwd(name, q, k, v, sinks):
    HQ, S, dh = q.shape
    HKV = k.shape[0]
    group = HQ // HKV
    scale = dh ** -0.5
    heads, kv, smem = _attn_specs(group, S, dh)

    def body(q_ref, k_ref, v_ref, sink_ref, o_ref):
        sink = _sink_col(sink_ref, pl.program_id(0), group)

        def block(n, carry):
            cur, prev = _block_rows(n)
            qv = q_ref[:, cur, :].reshape(group * BLOCK, dh)
            pp, pc, _ = _attn_probs(qv, k_ref[prev, :], k_ref[cur, :], sink, n, scale)
            o = jnp.dot(pp.astype(BF16), v_ref[prev, :], preferred_element_type=F32)
            o = o + jnp.dot(pc.astype(BF16), v_ref[cur, :], preferred_element_type=F32)
            o_ref[:, cur, :] = o.reshape(group, BLOCK, dh).astype(BF16)
            return carry

        lax.fori_loop(0, S // BLOCK, block, 0)

    return pl.pallas_call(
        body, name=name, grid=(HKV,), in_specs=[heads, kv, kv, smem], out_specs=heads,
        out_shape=jax.ShapeDtypeStruct((HQ, S, dh), BF16), compiler_params=_cparams(("parallel",)),
    )(q, k, v, sinks)


def _attn_bwd(name, q, k, v, sinks, do):
    HQ, S, dh = q.shape
    HKV = k.shape[0]
    group = HQ // HKV
    scale = dh ** -0.5
    heads, kv, smem = _attn_specs(group, S, dh)
    sk = pl.BlockSpec((None, group, LANES), lambda hk: (hk, 0, 0))

    def body(q_ref, k_ref, v_ref, sink_ref, do_ref, dq_ref, dk_ref, dv_ref, ds_ref):
        rows = group * BLOCK
        sink = _sink_col(sink_ref, pl.program_id(0), group)
        dk_ref[...] = jnp.zeros_like(dk_ref)
        dv_ref[...] = jnp.zeros_like(dv_ref)
        tdot = lambda a, b: lax.dot_general(a, b, TN, preferred_element_type=F32)

        def block(n, dsink):
            cur, prev = _block_rows(n)
            qv = q_ref[:, cur, :].reshape(rows, dh)
            dov = do_ref[:, cur, :].reshape(rows, dh)
            kp, kc, vp, vc = k_ref[prev, :], k_ref[cur, :], v_ref[prev, :], v_ref[cur, :]
            pp, pc, ps = _attn_probs(qv, kp, kc, sink, n, scale)
            dpp = lax.dot_general(dov, vp, NT, preferred_element_type=F32)
            dpc = lax.dot_general(dov, vc, NT, preferred_element_type=F32)
            delta = jnp.sum(pp * dpp, axis=-1, keepdims=True) + jnp.sum(pc * dpc, axis=-1, keepdims=True)
            dsp = (pp * (dpp - delta) * scale).astype(BF16)
            dsc = (pc * (dpc - delta) * scale).astype(BF16)
            dq = jnp.dot(dsp, kp, preferred_element_type=F32) + jnp.dot(dsc, kc, preferred_element_type=F32)
            dq_ref[:, cur, :] = dq.reshape(group, BLOCK, dh)
            dk_ref[prev, :] += tdot(dsp, qv)
            dv_ref[prev, :] += tdot(pp.astype(BF16), dov)
            dk_ref[cur, :] += tdot(dsc, qv)
            dv_ref[cur, :] += tdot(pc.astype(BF16), dov)
            return dsink - jnp.sum((ps * delta).reshape(group, BLOCK, 1), axis=1)

        dsink = lax.fori_loop(0, S // BLOCK, block, jnp.zeros((group, 1), F32))
        ds_ref[...] = jnp.broadcast_to(dsink, (group, LANES))

    return pl.pallas_call(
        body, name=name, grid=(HKV,), in_specs=[heads, kv, kv, smem, heads], out_specs=(heads, kv, kv, sk),
        out_shape=(jax.ShapeDtypeStruct((HQ, S, dh), F32), jax.ShapeDtypeStruct((HKV, S, dh), F32),
                   jax.ShapeDtypeStruct((HKV, S, dh), F32), jax.ShapeDtypeStruct((HKV, group, LANES), F32)),
        compiler_params=_cparams(("parallel",)),
    )(q, k, v, sinks, do)


RESIDENT = pl.Buffered(1)


def _gate_blocks(tm, Ns, off):
    assert off % Ns == 0
    return [pl.BlockSpec((tm, Ns), lambda i, k=k: (i, off // Ns + k)) for k in range(N_CHIPS)]


def _mixer_out_fwd(name, aconv, o, woc3, woa3, wo, proj, x1, ga_off, gb_off):
    S, D = x1.shape
    Ns = woc3.shape[2]
    tm = _tile(S, 256, 16)

    def body(a_ref, o_ref, woc_ref, woa_ref, wo_ref, x1_ref, *rest):
        ga_refs, gb_refs = rest[:N_CHIPS], rest[N_CHIPS:2 * N_CHIPS]
        ya_ref, yb_ref, m_ref, x2_ref = rest[2 * N_CHIPS:]
        av, ov = a_ref[...], o_ref[...]
        for s in range(N_CHIPS):
            cols = slice(s * Ns, (s + 1) * Ns)
            ya = jnp.dot(av, woc_ref[s], preferred_element_type=F32)
            yb = jnp.dot(ov, woa_ref[s], preferred_element_type=F32)
            ya_ref[:, cols] = ya.astype(BF16)
            yb_ref[:, cols] = yb.astype(BF16)
            ga, gb = ga_refs[s][...].astype(F32), gb_refs[s][...].astype(F32)
            m_ref[:, cols] = (_sigmoid(ga) * ya + _sigmoid(gb) * yb).astype(BF16)
        x2_ref[...] = x1_ref[...] + jnp.dot(m_ref[...], wo_ref[...], preferred_element_type=F32)

    row = lambda w: pl.BlockSpec((tm, w), lambda i: (i, 0))
    whole3 = lambda a: pl.BlockSpec(a.shape, lambda i: (0, 0, 0), pipeline_mode=RESIDENT)
    act = jax.ShapeDtypeStruct((S, D), BF16)
    return pl.pallas_call(
        body, name=name, grid=(S // tm,),
        in_specs=[row(aconv.shape[1]), row(o.shape[1]), whole3(woc3), whole3(woa3),
                  pl.BlockSpec(wo.shape, lambda i: (0, 0), pipeline_mode=RESIDENT), row(D)]
        + _gate_blocks(tm, Ns, ga_off) + _gate_blocks(tm, Ns, gb_off),
        out_specs=(row(D), row(D), row(D), row(D)), out_shape=(act, act, act, jax.ShapeDtypeStruct((S, D), F32)),
        compiler_params=_cparams(("parallel",)),
    )(aconv, o, woc3, woa3, wo, x1, *([proj] * (2 * N_CHIPS)))


def _mixer_out_bwd(name, dx2_b, wo, ya, yb, proj, woc3, woa3, ga_off, gb_off, dep):
    S, D = dx2_b.shape
    K, Ns = woc3.shape[1], woc3.shape[2]
    tm = _tile(S, 256, 16)

    def body(dx_ref, wo_ref, ya_ref, yb_ref, woc_ref, woa_ref, *rest):
        ga_refs, gb_refs = rest[:N_CHIPS], rest[N_CHIPS:2 * N_CHIPS]
        dga_ref, dgb_ref, dya_ref, dyb_ref, da_ref, do_ref = rest[-6:]
        dm = lax.dot_general(dx_ref[...], wo_ref[...], NT, preferred_element_type=F32)
        da = do = None
        for s in range(N_CHIPS):
            cols = slice(s * Ns, (s + 1) * Ns)
            dms = dm[:, cols]
            sa, sb = _sigmoid(ga_refs[s][...].astype(F32)), _sigmoid(gb_refs[s][...].astype(F32))
            dga_ref[:, cols] = (dms * ya_ref[:, cols].astype(F32) * sa * (1.0 - sa)).astype(BF16)
            dgb_ref[:, cols] = (dms * yb_ref[:, cols].astype(F32) * sb * (1.0 - sb)).astype(BF16)
            dya, dyb = (dms * sa).astype(BF16), (dms * sb).astype(BF16)
            dya_ref[:, cols] = dya
            dyb_ref[:, cols] = dyb
            pa = lax.dot_general(dya, woc_ref[s], NT, preferred_element_type=F32)
            pb = lax.dot_general(dyb, woa_ref[s], NT, preferred_element_type=F32)
            da, do = (pa, pb) if da is None else (da + pa, do + pb)
        da_ref[...] = da
        do_ref[...] = do.astype(BF16)

    row = lambda w: pl.BlockSpec((tm, w), lambda i: (i, 0))
    whole3 = lambda a: pl.BlockSpec(a.shape, lambda i: (0, 0, 0), pipeline_mode=RESIDENT)
    act = jax.ShapeDtypeStruct((S, D), BF16)
    return pl.pallas_call(
        body, name=name, grid=(S // tm,),
        in_specs=[row(D), pl.BlockSpec(wo.shape, lambda i: (0, 0), pipeline_mode=RESIDENT), row(D), row(D), whole3(woc3), whole3(woa3)]
        + _gate_blocks(tm, Ns, ga_off) + _gate_blocks(tm, Ns, gb_off) + [pl.BlockSpec(dep.shape, lambda i: (0, 0))],
        out_specs=(row(D), row(D), row(D), row(D), row(K), row(K)),
        out_shape=(act, act, act, act, jax.ShapeDtypeStruct((S, K), F32), jax.ShapeDtypeStruct((S, K), BF16)),
        compiler_params=_cparams(("parallel",)),
    )(dx2_b, wo, ya, yb, woc3, woa3, *([proj] * (2 * N_CHIPS)), dep)


ANY = pl.BlockSpec(memory_space=pl.ANY)


def _row_tile(rows, cols, n_arrays):
    want = max(16, (VMEM_LIMIT_V7X // 2) // (2 * n_arrays * cols * 4))
    return _tile(rows, want, 16)


def _cast_to_slot(name, w, dtype, p_arr, dep=None):
    R, C = w.shape
    tr = _row_tile(R, C, 2)
    extra = () if dep is None else (dep,)

    def body(p_ref, w_ref, *rest):
        rest[-1][...] = w_ref[...].astype(dtype)

    return pl.pallas_call(
        body, name=name,
        grid_spec=pltpu.PrefetchScalarGridSpec(
            num_scalar_prefetch=1, grid=(R // tr,),
            in_specs=[pl.BlockSpec((tr, C), lambda i, p_ref: (i, 0))] + [pl.BlockSpec(d.shape, lambda i, p_ref: (0, 0)) for d in extra],
            out_specs=pl.BlockSpec((None, tr, C), lambda i, p_ref: (p_ref[0], i, 0))),
        out_shape=jax.ShapeDtypeStruct((N_CHIPS, R, C), dtype), compiler_params=_cparams(("parallel",)),
    )(p_arr, w, *extra)


def _add_half(name, g3, r3, c_arr):
    n, h, C = r3.shape
    tr = _row_tile(h, C, 3)
    nb = h // tr

    def body(c_ref, g_ref, r_ref, o_ref):
        o_ref[...] = (g_ref[...].astype(F32) + r_ref[...].astype(F32)).astype(BF16)

    blk = pl.BlockSpec((None, tr, C), lambda s, i, c_ref: (s, i, 0))
    return pl.pallas_call(
        body, name=name,
        grid_spec=pltpu.PrefetchScalarGridSpec(
            num_scalar_prefetch=1, grid=(n, nb),
            in_specs=[pl.BlockSpec((None, tr, C), lambda s, i, c_ref: (s, c_ref[0] * nb + i, 0)), blk], out_specs=blk),
        out_shape=jax.ShapeDtypeStruct(r3.shape, BF16), compiler_params=_cparams(("parallel", "parallel")),
    )(c_arr, g3, r3)


def _add_chips(name, t3, r3, cp_arr):
    n, h, C = r3.shape
    tr = _row_tile(h, C, 6)
    nb = h // tr

    def body(cp_ref, t_ref, r0_ref, r1_ref, r2_ref, r3_ref, o_ref):
        p = cp_ref[1]
        total = None
        for a, r_ref in enumerate((r0_ref, r1_ref, r2_ref, r3_ref)):
            part = jnp.where(p == a, t_ref[...], r_ref[...]).astype(F32)
            total = part if total is None else total + part
        o_ref[...] = total

    def part(a):
        return pl.BlockSpec((None, tr, C), lambda i, cp_ref: (jnp.where(cp_ref[1] == a, (a + 1) % N_CHIPS, a), i, 0))

    return pl.pallas_call(
        body, name=name,
        grid_spec=pltpu.PrefetchScalarGridSpec(
            num_scalar_prefetch=1, grid=(nb,),
            in_specs=[pl.BlockSpec((None, tr, C), lambda i, cp_ref: (cp_ref[1], i, 0)), part(0), part(1), part(2), part(3)],
            out_specs=pl.BlockSpec((tr, C), lambda i, cp_ref: (cp_ref[0] * nb + i, 0))),
        out_shape=jax.ShapeDtypeStruct((2 * h, C), F32), compiler_params=_cparams(("parallel",)),
    )(cp_arr, t3, r3, r3, r3, r3)


def _adamw(name, w, g, m, v, deps=()):
    R, C = w.shape
    extra = tuple(deps)
    tr = _row_tile(R, C, 8)
    c1 = 1.0 - ADAM_B1 ** ADAM_STEP
    c2 = 1.0 - ADAM_B2 ** ADAM_STEP

    def body(w_ref, g_ref, m_ref, v_ref, *rest):
        go_ref, d_ref, nm_ref, nv_ref = rest[-4:]
        gv = g_ref[...]
        go_ref[...] = gv
        nm = ADAM_B1 * m_ref[...] + (1.0 - ADAM_B1) * gv
        nv = ADAM_B2 * v_ref[...] + (1.0 - ADAM_B2) * (gv * gv)
        d_ref[...] = -ADAM_LR * ((nm / c1) / (jnp.sqrt(nv / c2) + ADAM_EPS) + ADAM_WD * w_ref[...])
        nm_ref[...] = nm
        nv_ref[...] = nv

    blk = pl.BlockSpec((tr, C), lambda i: (i, 0))
    o = jax.ShapeDtypeStruct((R, C), F32)
    return pl.pallas_call(
        body, name=name, grid=(R // tr,), in_specs=[blk, blk, blk, blk] + [ANY] * len(extra), out_specs=(blk, blk, blk, blk),
        out_shape=(o, o, o, o), compiler_params=_cparams(("parallel",)),
    )(w, g, m, v, *extra)


def _place():
    x, y, c = lax.axis_index("x"), lax.axis_index("y"), lax.axis_index("c")
    chips = [(1 - x, y), (x, 1 - y), (1 - x, 1 - y)]
    return x, y, c, 2 * x + y, chips


HBM = pl.BlockSpec(memory_space=pltpu.HBM)
SEM = pl.BlockSpec(memory_space=pltpu.SEMAPHORE)
TOKEN = jax.ShapeDtypeStruct((8, LANES), F32)
DATAFLOW = pltpu.SideEffectType.DATAFLOW_SIDE_EFFECTING


def _hbm(a):
    return pltpu.with_memory_space_constraint(a, pltpu.HBM)


def _gather_blocks(bufs, i, c, p, chips):
    if bufs[i].shape[1] % 16:
        return bufs[i].at[p], [bufs[i].at[2 * cx + cy] for cx, cy in chips]
    h = bufs[i].shape[1] // 2
    rows = pl.ds(pl.multiple_of(c * h, 16), h)
    return bufs[i].at[p, rows], [bufs[i].at[2 * cx + cy, rows] for cx, cy in chips]


def _gather_start(name, groups, dep):
    slots = [s for g in groups for s in g]
    n, ng = len(slots), len(groups)

    def body(*refs):
        bufs, sems, token = refs[:n], refs[n + 1:n + 1 + 2 * ng], refs[-1]
        x, y, c, p, chips = _place()
        i = 0
        for gi, g in enumerate(groups):
            send, recv = sems[2 * gi], sems[2 * gi + 1]
            for k in range(len(g)):
                mine, _ = _gather_blocks(bufs, i, c, p, chips)
                for j, chip in enumerate(chips):
                    pltpu.make_async_remote_copy(src_ref=mine, dst_ref=mine, send_sem=send.at[3 * k + j], recv_sem=recv.at[3 * k + j],
                                                 device_id=(*chip, c), device_id_type=MESH).start()
                i += 1
        token[...] = jnp.zeros_like(token)

    sem_shapes = [pltpu.SemaphoreType.DMA((3 * len(g),)) for g in groups for _ in range(2)]
    out = pl.pallas_call(
        body, name=name, in_specs=[HBM] * n + [ANY],
        out_specs=(*([SEM] * (2 * ng)), *([HBM] * n), pl.BlockSpec(memory_space=pltpu.VMEM)),
        out_shape=(*sem_shapes, *[pltpu.HBM(s.shape, s.dtype) for s in slots], TOKEN),
        input_output_aliases={i: 2 * ng + i for i in range(n)},
        compiler_params=pltpu.CompilerParams(has_side_effects=DATAFLOW),
    )(*[_hbm(s) for s in slots], dep)
    started, i = [], 2 * ng
    for gi, g in enumerate(groups):
        started.append((out[2 * gi], out[2 * gi + 1], list(out[i:i + len(g)])))
        i += len(g)
    return started, out[-1]


def _gather_wait(name, send, recv, slots, after):
    n = len(slots)

    def body(*refs):
        bufs, send, recv = refs[:n], refs[n], refs[n + 1]
        x, y, c, p, chips = _place()
        for i in range(n):
            mine, landed = _gather_blocks(bufs, i, c, p, chips)
            for j, chip in enumerate(chips):
                cp = pltpu.make_async_remote_copy(src_ref=mine, dst_ref=landed[j], send_sem=send.at[3 * i + j],
                                                  recv_sem=recv.at[3 * i + j], device_id=(*chip, c), device_id_type=MESH)
                cp.wait_send()
                cp.wait_recv()

    return list(pl.pallas_call(
        body, name=name, in_specs=[HBM] * n + [SEM, SEM, ANY], out_specs=tuple([HBM] * n),
        out_shape=tuple(pltpu.HBM(s.shape, s.dtype) for s in slots),
        input_output_aliases={i: i for i in range(n)},
        compiler_params=pltpu.CompilerParams(has_side_effects=DATAFLOW),
    )(*slots, send, recv, after))


def _gather_forward(name, slots):
    idx = [i for i, s in enumerate(slots) if s.shape[1] % 16 == 0]
    n = len(slots)

    def body(*refs):
        bufs = refs[n:2 * n]
        send, recv = refs[2 * n:]
        x, y, c, p, chips = _place()

        def rdma(k, ref):
            return pltpu.make_async_remote_copy(src_ref=ref, dst_ref=ref, send_sem=send.at[k], recv_sem=recv.at[k],
                                                device_id=(x, y, 1 - c), device_id_type=MESH)

        cps = []
        for k, i in enumerate(idx):
            for j, ref in enumerate(_gather_blocks(bufs, i, c, p, chips)[1]):
                cps.append(rdma(3 * k + j, ref))
                cps[-1].start()
        for k, i in enumerate(idx):
            for j, ref in enumerate(_gather_blocks(bufs, i, 1 - c, p, chips)[1]):
                rdma(3 * k + j, ref).wait_recv()
        for cp in cps:
            cp.wait_send()

    return list(pl.pallas_call(
        body, name=name, in_specs=[ANY] * n, out_specs=tuple([ANY] * n),
        out_shape=tuple(jax.ShapeDtypeStruct(s.shape, s.dtype) for s in slots),
        scratch_shapes=[pltpu.SemaphoreType.DMA((3 * len(idx),)), pltpu.SemaphoreType.DMA((3 * len(idx),))],
        input_output_aliases={i: i for i in range(n)},
        compiler_params=pltpu.CompilerParams(has_side_effects=True),
    )(*slots))


def _swap_copy(grads, lands, send, recv, i, x, y, c):
    h = grads[i].shape[1] // 2
    other = pl.ds(pl.multiple_of((1 - c) * h, 16), h)
    return pltpu.make_async_remote_copy(src_ref=grads[i].at[:, other, :], dst_ref=lands[i], send_sem=send.at[i],
                                        recv_sem=recv.at[i], device_id=(x, y, 1 - c), device_id_type=MESH)


def _swap_wait(name, send, recv, grads, lands, after):
    n = len(grads)

    def body(*refs):
        ins, lands, send, recv = refs[:n], refs[n:2 * n], refs[2 * n], refs[2 * n + 1]
        x, y, c, p, chips = _place()
        for i in range(n):
            cp = _swap_copy(ins, lands, send, recv, i, x, y, c)
            cp.wait_send()
            cp.wait_recv()

    shapes = [pltpu.HBM(t.shape, t.dtype) for t in list(grads) + list(lands)]
    out = pl.pallas_call(
        body, name=name, in_specs=[HBM] * (2 * n) + [SEM, SEM, ANY], out_specs=tuple([HBM] * (2 * n)),
        out_shape=tuple(shapes), input_output_aliases={i: i for i in range(2 * n)},
        compiler_params=pltpu.CompilerParams(has_side_effects=DATAFLOW),
    )(*grads, *lands, send, recv, after)
    return list(out[:n]), list(out[n:])


def _reduce_starts(name, grads, parts):
    ng, npt = len(grads), len(parts)
    halves = [(g.shape[0], g.shape[1] // 2, g.shape[2]) for g in grads]
    arrays = list(grads) + [lax.empty(s, g.dtype) for s, g in zip(halves, grads)] + list(parts) + [lax.empty(t.shape, t.dtype) for t in parts]
    na = len(arrays)
    sems = ([pltpu.SemaphoreType.DMA((ng,))] * 2 if ng else []) + ([pltpu.SemaphoreType.DMA((3 * npt,))] * 2 if npt else [])
    ns = len(sems)

    def body(*refs):
        ins, sem, token = refs[:na], list(refs[na:na + ns]), refs[-1]
        x, y, c, p, chips = _place()
        if ng:
            for i in range(ng):
                _swap_copy(ins[:ng], ins[ng:2 * ng], sem[0], sem[1], i, x, y, c).start()
        if npt:
            src, land, send, recv = ins[2 * ng:2 * ng + npt], ins[2 * ng + npt:], sem[-2], sem[-1]
            for i in range(npt):
                for j, (cx, cy) in enumerate(chips):
                    pltpu.make_async_remote_copy(src_ref=src[i].at[2 * cx + cy], dst_ref=land[i].at[p], send_sem=send.at[3 * i + j],
                                                 recv_sem=recv.at[3 * i + j], device_id=(cx, cy, c), device_id_type=MESH).start()
        token[...] = jnp.zeros_like(token)

    out = pl.pallas_call(
        body, name=name, in_specs=[HBM] * na,
        out_specs=(*([SEM] * ns), *([HBM] * na), pl.BlockSpec(memory_space=pltpu.VMEM)),
        out_shape=(*sems, *[pltpu.HBM(a.shape, a.dtype) for a in arrays], TOKEN),
        input_output_aliases={i: ns + i for i in range(na)},
        compiler_params=pltpu.CompilerParams(has_side_effects=DATAFLOW),
    )(*[_hbm(a) for a in arrays])
    bufs = list(out[ns:ns + na])
    swap = (out[0], out[1], bufs[:ng], bufs[ng:2 * ng]) if ng else None
    exch = (out[ns - 2], out[ns - 1], bufs[2 * ng:2 * ng + npt], bufs[2 * ng + npt:]) if npt else None
    return swap, exch, out[-1]


def _exchange_wait(name, send, recv, parts, lands, after):
    n = len(parts)

    def body(*refs):
        ins, lands, send, recv = refs[:n], refs[n:2 * n], refs[2 * n], refs[2 * n + 1]
        x, y, c, p, chips = _place()
        for i in range(n):
            for j, (cx, cy) in enumerate(chips):
                q = 2 * cx + cy
                cp = pltpu.make_async_remote_copy(src_ref=ins[i].at[q], dst_ref=lands[i].at[q], send_sem=send.at[3 * i + j],
                                                  recv_sem=recv.at[3 * i + j], device_id=(cx, cy, c), device_id_type=MESH)
                cp.wait_send()
                cp.wait_recv()

    shapes = [pltpu.HBM(t.shape, t.dtype) for t in parts]
    out = pl.pallas_call(
        body, name=name, in_specs=[HBM] * (2 * n) + [SEM, SEM, ANY], out_specs=tuple([HBM] * (2 * n)),
        out_shape=(*shapes, *shapes), input_output_aliases={i: i for i in range(2 * n)},
        compiler_params=pltpu.CompilerParams(has_side_effects=DATAFLOW),
    )(*parts, *lands, send, recv, after)
    return list(out[:n]), list(out[n:])


def _join_copy(buf, send_sem, recv_sem, which, x, y, c):
    h = buf.shape[0] // 2
    rows = buf.at[pl.ds(pl.multiple_of(which * h, 8), h)]
    return pltpu.make_async_remote_copy(src_ref=rows, dst_ref=rows, send_sem=send_sem, recv_sem=recv_sem,
                                        device_id=(x, y, 1 - c), device_id_type=MESH)


def _join_start(name, groups):
    bufs = [b for g in groups for b in g]
    n, ng = len(bufs), len(groups)

    def body(*refs):
        ins, sems, token = refs[:n], refs[n:n + 2 * ng], refs[-1]
        x, y, c, p, chips = _place()
        i = 0
        for gi, g in enumerate(groups):
            for k in range(len(g)):
                _join_copy(ins[i], sems[2 * gi].at[k], sems[2 * gi + 1].at[k], c, x, y, c).start()
                i += 1
        token[...] = jnp.zeros_like(token)

    sem_shapes = [pltpu.SemaphoreType.DMA((len(g),)) for g in groups for _ in range(2)]
    out = pl.pallas_call(
        body, name=name, in_specs=[HBM] * n,
        out_specs=(*([SEM] * (2 * ng)), *([HBM] * n), pl.BlockSpec(memory_space=pltpu.VMEM)),
        out_shape=(*sem_shapes, *[pltpu.HBM(t.shape, t.dtype) for t in bufs], TOKEN),
        input_output_aliases={i: 2 * ng + i for i in range(n)},
        compiler_params=pltpu.CompilerParams(has_side_effects=DATAFLOW),
    )(*[_hbm(t) for t in bufs])
    started, i = [], 2 * ng
    for gi, g in enumerate(groups):
        started.append((out[2 * gi], out[2 * gi + 1], list(out[i:i + len(g)])))
        i += len(g)
    return started, out[-1]


def _join_wait(name, send, recv, bufs, after):
    n = len(bufs)

    def body(*refs):
        ins, send, recv = refs[:n], refs[n], refs[n + 1]
        x, y, c, p, chips = _place()
        for i in range(n):
            _join_copy(ins[i], send.at[i], recv.at[i], c, x, y, c).wait_send()
            _join_copy(ins[i], send.at[i], recv.at[i], 1 - c, x, y, c).wait_recv()

    return list(pl.pallas_call(
        body, name=name, in_specs=[HBM] * n + [SEM, SEM, ANY], out_specs=tuple([HBM] * n),
        out_shape=tuple(pltpu.HBM(t.shape, t.dtype) for t in bufs), input_output_aliases={i: i for i in range(n)},
        compiler_params=pltpu.CompilerParams(has_side_effects=DATAFLOW),
    )(*bufs, send, recv, after))


def _allreduce_small(name, pack, dep):
    R, W = pack.shape

    def body(in_ref, dep_ref, out_ref, slots, send, recv):
        x, y, c = lax.axis_index("x"), lax.axis_index("y"), lax.axis_index("c")
        me = 4 * x + 2 * y + c
        slots[0] = in_ref[...]
        cps = []
        for k in range(1, N_DEV):
            peer = (x ^ (k >> 2), y ^ ((k >> 1) & 1), c ^ (k & 1))
            cp = pltpu.make_async_remote_copy(src_ref=in_ref, dst_ref=slots.at[k], send_sem=send.at[k - 1],
                                              recv_sem=recv.at[k - 1], device_id=peer, device_id_type=MESH)
            cp.start()
            cps.append(cp)
        for cp in cps:
            cp.wait()
        total = slots[me]
        for a in range(1, N_DEV):
            total = total + slots[jnp.bitwise_xor(a, me)]
        out_ref[...] = total

    vmem = pl.BlockSpec(memory_space=pltpu.VMEM)
    return pl.pallas_call(
        body, name=name, in_specs=[vmem, ANY], out_specs=vmem, out_shape=jax.ShapeDtypeStruct((R, W), F32),
        scratch_shapes=[pltpu.VMEM((N_DEV, R, W), F32), pltpu.SemaphoreType.DMA((N_DEV - 1,)), pltpu.SemaphoreType.DMA((N_DEV - 1,))],
        compiler_params=pltpu.CompilerParams(has_side_effects=True),
    )(pack, dep)


def _heads(a, n_heads):
    S = a.shape[0]
    return a.reshape(S, n_heads, a.shape[1] // n_heads).transpose(1, 0, 2)


def _unheads(a):
    H, S, dh = a.shape
    return a.transpose(1, 0, 2).reshape(S, H * dh)


def _ffn_bwd(tag, xin, gain, wgu3, wd, saved, dxout, dxo_b, reduce_start, dep, flush=None):
    h, gu, act = saved
    D = xin.shape[1]
    tok = reduce_start({f"w_down{tag}": _mm_tn(f"dw_down_{tag}", act, dxo_b, 0.5, dep=dep).reshape(N_CHIPS, -1, D)})
    dgu = _ffn_down_bwd(f"ffn_down_bwd_{tag}", dxo_b, wd, gu, 0.5, dep=tok)
    tok = reduce_start({f"w_gu{tag}": _mm_tn_cols(f"dw_gu_{tag}", h, dgu, wgu3.shape[2], b_is_gu=True)})
    if flush is not None:
        tok = flush(tok)
    dh = _mm_nt_cols(f"ffn_up_bwd_{tag}", dgu, wgu3, a_is_gu=True, dep=tok)
    dxin, dxin_b, dgain = _rms_bwd(f"rms_bwd_{tag}", xin, gain, dh, dxout)
    return dxin, dxin_b, dgain, tok


def kernel(x, g_ffn1, w_gu1, w_down1, g_mix, w_in, conv_w, q_norm_g, k_norm_g, sinks, w_out_conv, w_out_attn, w_o, g_ffn2, w_gu2, w_down2, loss_target, m_g_ffn1, m_w_gu1, m_w_down1, m_g_mix, m_w_in, m_conv_w, m_q_norm_g, m_k_norm_g, m_sinks, m_w_out_conv, m_w_out_attn, m_w_o, m_g_ffn2, m_w_gu2, m_w_down2, v_g_ffn1, v_w_gu1, v_w_down1, v_g_mix, v_w_in, v_conv_w, v_q_norm_g, v_k_norm_g, v_sinks, v_w_out_conv, v_w_out_attn, v_w_o, v_g_ffn2, v_w_gu2, v_w_down2):
    S, D = x.shape[1], x.shape[2]
    dh = q_norm_g.shape[1]
    HQ = sinks.shape[1]
    HKV = HQ // 4
    AW, KVW, CW = HQ * dh, HKV * dh, D // 2
    off_q, off_k, off_v = 3 * CW, 3 * CW + AW, 3 * CW + AW + KVW
    off_ga, off_gb = off_v + KVW, off_v + KVW + D
    x0, target = x[0], loss_target[0]
    cx, cy, cc = lax.axis_index("x"), lax.axis_index("y"), lax.axis_index("c")
    chip = 2 * cx + cy
    p_arr = jnp.reshape(chip, (1,)).astype(jnp.int32)
    c_arr = jnp.reshape(cc, (1,)).astype(jnp.int32)
    cp_arr = jnp.stack([cc, chip]).astype(jnp.int32)
    wts = dict(g_ffn1=g_ffn1, w_gu1=w_gu1, w_down1=w_down1, g_mix=g_mix, w_in=w_in, conv_w=conv_w, q_norm_g=q_norm_g,
               k_norm_g=k_norm_g, sinks=sinks, w_out_conv=w_out_conv, w_out_attn=w_out_attn, w_o=w_o, g_ffn2=g_ffn2,
               w_gu2=w_gu2, w_down2=w_down2)
    ms = dict(g_ffn1=m_g_ffn1, w_gu1=m_w_gu1, w_down1=m_w_down1, g_mix=m_g_mix, w_in=m_w_in, conv_w=m_conv_w,
              q_norm_g=m_q_norm_g, k_norm_g=m_k_norm_g, sinks=m_sinks, w_out_conv=m_w_out_conv, w_out_attn=m_w_out_attn,
              w_o=m_w_o, g_ffn2=m_g_ffn2, w_gu2=m_w_gu2, w_down2=m_w_down2)
    vs = dict(g_ffn1=v_g_ffn1, w_gu1=v_w_gu1, w_down1=v_w_down1, g_mix=v_g_mix, w_in=v_w_in, conv_w=v_conv_w,
              q_norm_g=v_q_norm_g, k_norm_g=v_k_norm_g, sinks=v_sinks, w_out_conv=v_w_out_conv, w_out_attn=v_w_out_attn,
              w_o=v_w_o, g_ffn2=v_g_ffn2, w_gu2=v_w_gu2, w_down2=v_w_down2)
    order = list(wts)
    small_names = [k for k in order if not k.startswith("w_")]
    grad, delta, new_m, new_v = {}, {}, {}, {}

    def cast(keys, dep=None):
        return [_cast_to_slot(f"cast_{k}", wts[k][0], F32 if k == "conv_w" else BF16, p_arr, dep) for k in keys]

    def gather_finish(tag, started, after):
        send, recv, slots = started
        return _gather_forward(f"gather_forward_{tag}", _gather_wait(f"gather_wait_{tag}", send, recv, slots, after))

    swapping, pending = [], []

    def reduce_start(full, after=None):
        keys = [] if full is None else list(full)
        pkeys, parts = [], []
        if swapping:
            pkeys, send, recv, gs, lands = swapping.pop(0)
            gs, sib = _swap_wait(f"swap_wait_{pkeys[0]}", send, recv, gs, lands, after if full is None else full[keys[0]])
            parts = [_add_half(f"add_half_{k}", g, r, c_arr) for k, g, r in zip(pkeys, gs, sib)]
        swap, exch, tok = _reduce_starts(f"reduce_starts_{keys[0] if keys else 'last'}", [full[k] for k in keys], parts)
        if exch:
            pending.append((pkeys, *exch))
        if swap:
            swapping.append((keys, *swap))
        return tok

    def reduce_finish(entries, after):
        ready = []
        for keys, send, recv, parts, lands in entries:
            parts, lands = _exchange_wait(f"exchange_wait_{keys[0]}", send, recv, parts, lands, after)
            ready.append((keys, [_add_chips(f"add_chips_{k}", t, r, cp_arr) for k, t, r in zip(keys, parts, lands)]))
        started, last = _join_start(f"join_start_{ready[0][0][0]}", [halves for _, halves in ready])
        for (keys, _), (send, recv, halves) in zip(ready, started):
            for k, g2 in zip(keys, _join_wait(f"join_wait_{keys[0]}", send, recv, halves, last)):
                g2, d, nm, nv = _adamw(f"adamw_{k}", wts[k][0], g2, ms[k][0], vs[k][0], (last,))
                grad[k], delta[k], new_m[k], new_v[k] = g2[None], d[None], nm[None], nv[None]
                last = nv
        return last

    (st_gu1, st_d1), tok = _gather_start("gather_start_1", [cast(["w_gu1"]), cast(["w_down1"])], x0)
    later = ["w_in", "conv_w", "w_out_conv", "w_out_attn", "w_o", "w_gu2", "w_down2"]
    slot = dict(zip(later, cast(later, tok)))
    h1 = _rms_fwd("rms_fwd_1", x0, g_ffn1, slot["w_down2"])
    wgu1, = gather_finish("gu1", st_gu1, h1)
    (st_in, st_out, st_gu2, st_d2), tok = _gather_start(
        "gather_start_2", [[slot["w_in"], slot["conv_w"]], [slot["w_out_conv"], slot["w_out_attn"], slot["w_o"]],
                           [slot["w_gu2"]], [slot["w_down2"]]], wgu1)
    q_consts, k_consts = _prep_consts(S, dh, AW), _prep_consts(S, dh, KVW)
    qg_row, kg_row = jnp.tile(q_norm_g, (1, HQ)), jnp.tile(k_norm_g, (1, HKV))
    sink_vec = sinks[0]

    gu1, act1 = _ffn_up("ffn_up_1", h1, wgu1, tok)
    wd1 = gather_finish("d1", st_d1, act1)[0].reshape(-1, D)
    x1 = _mm_res("ffn_down_1", act1, wd1, x0, 0.5)
    win3, convw3 = gather_finish("in", st_in, x1)
    h2 = _rms_fwd("rms_fwd_mix", x1, g_mix)
    proj = _mm_cols("in_proj", h2, win3, BF16)
    aconv = _conv_fwd("conv_fwd", proj, convw3, CW)
    woc3, woa3, wo = gather_finish("out", st_out, aconv)
    wo = wo.reshape(-1, D)
    vh = _heads(proj[:, off_v:off_v + KVW], HKV)
    qn = _heads(_qk_prep("q_prep", proj, off_q, AW, qg_row, q_consts), HQ)
    kn = _heads(_qk_prep("k_prep", proj, off_k, KVW, kg_row, k_consts), HKV)
    oh = _attn_fwd("attn_fwd", qn, kn, vh, sink_vec)
    o = _unheads(oh)
    ya, yb, merged, x2 = _mixer_out_fwd("mixer_out", aconv, o, woc3, woa3, wo, proj, x1, off_ga, off_gb)
    wgu2, = gather_finish("gu2", st_gu2, x2)
    h3 = _rms_fwd("rms_fwd_2", x2, g_ffn2)
    gu2, act2 = _ffn_up("ffn_up_2", h3, wgu2)
    wd2 = gather_finish("d2", st_d2, act2)[0].reshape(-1, D)
    x3 = _mm_res("ffn_down_2", act2, wd2, x2, 0.5)

    dy, dy_b, loss_lanes = _loss_grad("loss_grad", x3, target)
    dx2, dx2_b, dg_ffn2, tok = _ffn_bwd("2", x2, g_ffn2, wgu2, wd2, (h3, gu2, act2), dy, dy_b, reduce_start, None)
    tok = reduce_start(dict(w_o=_mm_tn("dw_o", merged, dx2_b, dep=tok).reshape(N_CHIPS, -1, D)))
    dga, dgb, dya, dyb, daconv, do = _mixer_out_bwd("mixer_out_bwd", dx2_b, wo, ya, yb, proj, woc3, woa3, off_ga, off_gb, tok)
    dwoc = _mm_tn_cols("dw_out_conv", aconv, dya, woc3.shape[2])
    dwoa = _mm_tn_cols("dw_out_attn", o, dyb, woa3.shape[2])
    tok = reduce_start(dict(w_out_conv=dwoc, w_out_attn=dwoa))
    dxc, dbg, dcg, dconvw = _conv_bwd("conv_bwd", proj, convw3, daconv, CW)
    dqn, dkn, dvh, dsink3 = _attn_bwd("attn_bwd", qn, kn, vh, sink_vec, _heads(do, HQ))
    dq_raw, dqg = _qk_prep_bwd("q_prep_bwd", proj, off_q, AW, qg_row, q_consts, _unheads(dqn))
    dk_raw, dkg = _qk_prep_bwd("k_prep_bwd", proj, off_k, KVW, kg_row, k_consts, _unheads(dkn))
    dqg, dkg = dqg.reshape(HQ, dh).sum(axis=0, keepdims=True), dkg.reshape(HKV, dh).sum(axis=0, keepdims=True)
    dproj = jnp.concatenate([dxc, dbg, dcg, dq_raw, dk_raw, _unheads(dvh).astype(BF16), dga, dgb], axis=1)
    dh2 = _mm_nt_cols("in_proj_bwd", dproj, win3, dep=tok)
    tok = reduce_start(dict(w_in=_mm_tn_cols("dw_in", h2, dproj, win3.shape[2])))
    dx1, dx1_b, dg_mix = _rms_bwd("rms_bwd_mix", x1, g_mix, dh2, dx2)
    dx0, _, dg_ffn1, tok = _ffn_bwd("1", x0, g_ffn1, wgu1, wd1, (h1, gu1, act1), dx1, dx1_b, reduce_start, tok, lambda after: reduce_start(None, after))

    def rows8(a):
        a = a.reshape(-1, a.shape[-1])
        return jnp.pad(a, ((0, -a.shape[0] % 8), (0, D - a.shape[1])))

    misc = jnp.concatenate([dqg, dkg, dsink3[:, :, 0].reshape(1, HQ), loss_lanes], axis=1)
    done = reduce_finish(pending[:-2], dx0)
    tot = _allreduce_small("allreduce_small", jnp.concatenate([rows8(a) for a in (dg_ffn1, dg_mix, dg_ffn2, dconvw, misc)], axis=0), done)
    reduce_finish(pending[-2:], tot)

    cw_s = conv_w.shape[2]
    conv_row0, misc_row = 24, 24 + (-(-N_CHIPS * CONV_K // 8)) * 8
    small_g = dict(g_ffn1=tot[0:1], g_mix=tot[8:9], g_ffn2=tot[16:17],
                   conv_w=lax.dynamic_slice(tot, (conv_row0 + CONV_K * chip, 0), (CONV_K, cw_s)),
                   q_norm_g=tot[misc_row:misc_row + 1, 0:dh], k_norm_g=tot[misc_row:misc_row + 1, dh:2 * dh],
                   sinks=tot[misc_row:misc_row + 1, 2 * dh:2 * dh + HQ])
    loss = (0.5 / D) * jnp.sum(tot[misc_row, 2 * dh + HQ:2 * dh + HQ + LANES])

    def small_pack(src):
        return jnp.concatenate([rows8(src[k]) for k in small_names], axis=0)

    _, sd, sm, sv = _adamw("adamw_small", small_pack(wts), small_pack(small_g), small_pack(ms), small_pack(vs))
    for i, k in enumerate(small_names):
        shape = wts[k].shape
        nr, ncol = math.prod(shape[:-1]), shape[-1]
        grad[k] = small_g[k].reshape(shape)
        delta[k], new_m[k], new_v[k] = (a[8 * i:8 * i + nr, 0:ncol].reshape(shape) for a in (sd, sm, sv))
    return (loss, dx0[None], *[grad[k] for k in order], *[delta[k] for k in order],
            *[new_m[k] for k in order], *[new_v[k] for k in order])
```

```python
import math

import numpy as np
import jax
import jax.numpy as jnp
from jax import lax
from jax.experimental import pallas as pl
from jax.experimental.pallas import tpu as pltpu

F32 = jnp.float32
BF16 = jnp.bfloat16
MESH = pl.DeviceIdType.MESH

RMS_EPS = 1e-6
BLOCK = 128
ROPE_THETA = 500000.0
NEG_INF = -1e30
CONV_K = 3
ADAM_LR, ADAM_B1, ADAM_B2, ADAM_EPS, ADAM_WD, ADAM_STEP = 0.001, 0.9, 0.999, 1e-08, 0.01, 10

VMEM_LIMIT_V7X = 56 * 1024 * 1024
LANES = 128
N_CHIPS = 4
N_DEV = 8


def _tile(n, want, align=LANES):
    best = None
    t = align
    while t <= min(n, want):
        if n % t == 0:
            best = t
        t += align
    return best or n


def _cparams(sem):
    return pltpu.CompilerParams(dimension_semantics=sem, vmem_limit_bytes=VMEM_LIMIT_V7X)


def _sigmoid(x):
    return 1.0 / (1.0 + jnp.exp(-x))


NN = (((1,), (0,)), ((), ()))
NT = (((1,), (1,)), ((), ()))
TN = (((0,), (0,)), ((), ()))


def _mm(name, grid, ins, in_specs, compute, out_shape, out_specs, epilogue, dep=None, carried=False):
    if dep is not None:
        ins, in_specs = tuple(ins) + (dep,), list(in_specs) + [pl.BlockSpec(dep.shape, lambda *_: (0, 0))]
    n_in = len(ins)

    def body(*refs):
        epilogue(compute(refs[:n_in]), refs[:n_in], refs[n_in:])

    return pl.pallas_call(
        body, name=name, grid=grid, in_specs=in_specs, out_specs=out_specs, out_shape=out_shape,
        compiler_params=_cparams(("arbitrary" if carried else "parallel", "arbitrary")),
    )(*ins)


def _dot(dims, a=0, b=1):
    return lambda refs: [lax.dot_general(refs[a][...], refs[b][...], dims, preferred_element_type=F32)]


def _ffn_up(name, h, wgu3, dep=None):
    S, D = h.shape
    Ns = wgu3.shape[2]
    F = 2 * Ns
    tm, tn = _tile(S, 512), _tile(Ns, 1408)
    nbs = Ns // tn

    def compute(refs):
        hv = refs[0][...]
        return [jnp.dot(hv, refs[1][...], preferred_element_type=F32), jnp.dot(hv, refs[2][...], preferred_element_type=F32)]

    def epi(accs, in_refs, out_refs):
        g, u = accs
        dgu_ref, a_ref = out_refs
        sg = _sigmoid(g)
        silu = g * sg
        dgu_ref[0] = (u * (sg * (1.0 + g * (1.0 - sg)))).astype(BF16)
        dgu_ref[1] = silu.astype(BF16)
        a_ref[...] = (silu * u).astype(BF16)

    return _mm(
        name, (F // tn, S // tm), (h, wgu3, wgu3),
        [pl.BlockSpec((tm, D), lambda j, i: (i, 0)),
         pl.BlockSpec((None, D, tn), lambda j, i: (j // nbs, 0, j % nbs)),
         pl.BlockSpec((None, D, tn), lambda j, i: (2 + j // nbs, 0, j % nbs))],
        compute, (jax.ShapeDtypeStruct((2, S, F), BF16), jax.ShapeDtypeStruct((S, F), BF16)),
        (pl.BlockSpec((2, tm, tn), lambda j, i: (0, i, j)), pl.BlockSpec((tm, tn), lambda j, i: (i, j))), epi, dep=dep)


def _mm_res(name, a, w, res, scale):
    S, K = a.shape
    N = w.shape[1]
    tm, tn = _tile(S, 512), _tile(N, 512 if K > 2816 else 1024)

    def epi(accs, in_refs, out_refs):
        out_refs[0][...] = in_refs[2][...] + scale * accs[0]

    return _mm(
        name, (N // tn, S // tm), (a, w, res),
        [pl.BlockSpec((tm, K), lambda j, i: (i, 0)), pl.BlockSpec((K, tn), lambda j, i: (0, j)),
         pl.BlockSpec((tm, tn), lambda j, i: (i, j))],
        _dot(NN), jax.ShapeDtypeStruct((S, N), F32), pl.BlockSpec((tm, tn), lambda j, i: (i, j)), epi)


def _mm_res_loss(name, a, w, res, scale, target):
    S, K = a.shape
    N = w.shape[1]
    tm, tn = _tile(S, 512), _tile(N, 512 if K > 2816 else 1024)

    def epi(accs, in_refs, out_refs):
        dy_ref, dyb_ref, l_ref = out_refs
        e = in_refs[2][...] + scale * accs[0] - in_refs[3][...]
        dy_ref[...] = e * (1.0 / N)
        dyb_ref[...] = (e * (1.0 / N)).astype(BF16)
        col = jnp.sum(e * e, axis=0, keepdims=True)
        part = col[:, 0:LANES]
        for k in range(1, tn // LANES):
            part = part + col[:, k * LANES:(k + 1) * LANES]

        @pl.when((pl.program_id(0) == 0) & (pl.program_id(1) == 0))
        def _():
            l_ref[...] = jnp.zeros_like(l_ref)

        l_ref[...] += part

    tile = pl.BlockSpec((tm, tn), lambda j, i: (i, j))
    return _mm(
        name, (N // tn, S // tm), (a, w, res, target),
        [pl.BlockSpec((tm, K), lambda j, i: (i, 0)), pl.BlockSpec((K, tn), lambda j, i: (0, j)), tile, tile],
        _dot(NN), (jax.ShapeDtypeStruct((S, N), F32), jax.ShapeDtypeStruct((S, N), BF16), jax.ShapeDtypeStruct((1, LANES), F32)),
        (tile, tile, pl.BlockSpec((1, LANES), lambda j, i: (0, 0))), epi, carried=True)


def _mm_cols(name, a, w3, out_dtype):
    S, K = a.shape
    Ns = w3.shape[2]
    tm, tn = _tile(S, 512), _tile(Ns, 2304)
    nbs = Ns // tn

    def epi(accs, in_refs, out_refs):
        out_refs[0][...] = accs[0].astype(out_dtype)

    return _mm(
        name, (N_CHIPS * nbs, S // tm), (a, w3),
        [pl.BlockSpec((tm, K), lambda j, i: (i, 0)),
         pl.BlockSpec((None, K, tn), lambda j, i: (j // nbs, 0, j % nbs))],
        _dot(NN), jax.ShapeDtypeStruct((S, N_CHIPS * Ns), out_dtype), pl.BlockSpec((tm, tn), lambda j, i: (i, j)), epi)


def _ffn_down_bwd(name, dy, wd, gu, scale, dep=None):
    S, D = dy.shape
    F = wd.shape[0]
    tm, tn = _tile(S, 512), _tile(F, 1408)

    def epi(accs, in_refs, out_refs):
        da = scale * accs[0]
        out_refs[0][0] = (da * in_refs[2][0].astype(F32)).astype(BF16)
        out_refs[0][1] = (da * in_refs[2][1].astype(F32)).astype(BF16)

    return _mm(
        name, (F // tn, S // tm), (dy, wd, gu),
        [pl.BlockSpec((tm, D), lambda j, i: (i, 0)), pl.BlockSpec((tn, D), lambda j, i: (j, 0)),
         pl.BlockSpec((2, tm, tn), lambda j, i: (0, i, j))],
        _dot(NT), jax.ShapeDtypeStruct((2, S, F), BF16), pl.BlockSpec((2, tm, tn), lambda j, i: (0, i, j)), epi, dep=dep)


def _mm_nt_cols(name, a, w3, a_is_gu=False, dep=None):
    K, Ns = w3.shape[1], w3.shape[2]
    S = a.shape[1] if a_is_gu else a.shape[0]
    tm = _tile(S, 512)
    tn = _tile(K, max(LANES, (12 << 20) // (N_CHIPS * Ns * 2)))
    if a_is_gu:
        a_spec = pl.BlockSpec((2, tm, 2 * Ns), lambda i, j: (0, i, 0))
        part = lambda a_ref, s: a_ref[s // 2, :, (s % 2) * Ns:(s % 2 + 1) * Ns]
    else:
        a_spec = pl.BlockSpec((tm, N_CHIPS * Ns), lambda i, j: (i, 0))
        part = lambda a_ref, s: a_ref[:, s * Ns:(s + 1) * Ns]

    def compute(refs):
        total = None
        for s in range(N_CHIPS):
            prod = lax.dot_general(part(refs[0], s), refs[1][s], NT, preferred_element_type=F32)
            total = prod if total is None else total + prod
        return [total]

    def epi(accs, in_refs, out_refs):
        out_refs[0][...] = accs[0]

    return _mm(
        name, (S // tm, K // tn), (a, w3), [a_spec, pl.BlockSpec((N_CHIPS, tn, Ns), lambda i, j: (0, j, 0))],
        compute, jax.ShapeDtypeStruct((S, K), F32), pl.BlockSpec((tm, tn), lambda i, j: (i, j)), epi, dep=dep)


def _mm_tn(name, a, b, scale=1.0, dep=None):
    S, K = a.shape
    N = b.shape[1]
    tm, tn = _tile(K, 1024), _tile(N, 1024)

    def epi(accs, in_refs, out_refs):
        out_refs[0][...] = (scale * accs[0]).astype(BF16)

    return _mm(
        name, (N // tn, K // tm), (a, b),
        [pl.BlockSpec((S, tm), lambda j, i: (0, i)), pl.BlockSpec((S, tn), lambda j, i: (0, j))],
        _dot(TN), jax.ShapeDtypeStruct((K, N), BF16), pl.BlockSpec((tm, tn), lambda j, i: (i, j)), epi, dep=dep)


def _mm_tn_cols(name, a, b, Ns, b_is_gu=False, dep=None):
    S, K = a.shape
    tm, tn = _tile(K, 1024), _tile(Ns, 2304)
    nbs = Ns // tn
    if b_is_gu:
        b_spec = pl.BlockSpec((None, S, tn), lambda j, i: (j // (2 * nbs), 0, j % (2 * nbs)))
    else:
        b_spec = pl.BlockSpec((S, tn), lambda j, i: (0, j))

    def epi(accs, in_refs, out_refs):
        out_refs[0][...] = accs[0].astype(BF16)

    return _mm(
        name, (N_CHIPS * nbs, K // tm), (a, b), [pl.BlockSpec((S, tm), lambda j, i: (0, i)), b_spec],
        _dot(TN), jax.ShapeDtypeStruct((N_CHIPS, K, Ns), BF16),
        pl.BlockSpec((None, tm, tn), lambda j, i: (j // nbs, i, j % nbs)), epi, dep=dep)


def _rms_fwd(name, x, gain, dep=None):
    S, D = x.shape
    tm = _tile(S, 256, 8)
    extra = () if dep is None else (dep,)

    def body(x_ref, g_ref, *rest):
        h_ref = rest[-1]
        xv = x_ref[...]
        r = lax.rsqrt(jnp.mean(xv * xv, axis=-1, keepdims=True) + RMS_EPS)
        h_ref[...] = (xv * r * g_ref[...]).astype(BF16)

    return pl.pallas_call(
        body, name=name, grid=(S // tm,),
        in_specs=[pl.BlockSpec((tm, D), lambda i: (i, 0)), pl.BlockSpec((1, D), lambda i: (0, 0))]
        + [pl.BlockSpec(memory_space=pl.ANY) for d in extra],
        out_specs=pl.BlockSpec((tm, D), lambda i: (i, 0)), out_shape=jax.ShapeDtypeStruct((S, D), BF16),
        compiler_params=_cparams(("parallel",)),
    )(x, gain, *extra)


def _rms_bwd(name, x, gain, dh, dres):
    S, D = x.shape
    tm = _tile(S, 256, 8)

    def body(x_ref, g_ref, dh_ref, dres_ref, dx_ref, dxb_ref, dg_ref):
        i = pl.program_id(0)
        xv = x_ref[...]
        r = lax.rsqrt(jnp.mean(xv * xv, axis=-1, keepdims=True) + RMS_EPS)
        xhat = xv * r
        dhv = dh_ref[...]
        dxhat = dhv * g_ref[...]
        dx = dres_ref[...] + r * (dxhat - xhat * jnp.mean(dxhat * xhat, axis=-1, keepdims=True))
        dx_ref[...] = dx
        dxb_ref[...] = dx.astype(BF16)

        @pl.when(i == 0)
        def _():
            dg_ref[...] = jnp.zeros_like(dg_ref)

        dg_ref[...] += jnp.sum(dhv * xhat, axis=0, keepdims=True)

    row = pl.BlockSpec((tm, D), lambda i: (i, 0))
    vec = pl.BlockSpec((1, D), lambda i: (0, 0))
    return pl.pallas_call(
        body, name=name, grid=(S // tm,), in_specs=[row, vec, row, row], out_specs=(row, row, vec),
        out_shape=(jax.ShapeDtypeStruct((S, D), F32), jax.ShapeDtypeStruct((S, D), BF16), jax.ShapeDtypeStruct((1, D), F32)),
        compiler_params=_cparams(("arbitrary",)),
    )(x, gain, dh, dres)


def _shift_down(u, k):
    rows = lax.broadcasted_iota(jnp.int32, u.shape, 0)
    return jnp.where(rows >= k, pltpu.roll(u, k, 0), 0.0)


def _shift_up(u, k):
    n = u.shape[0]
    rows = lax.broadcasted_iota(jnp.int32, u.shape, 0)
    return jnp.where(rows < n - k, pltpu.roll(u, n - k, 0), 0.0)


def _conv_specs(S, cw, conv_width):
    nb = conv_width // cw
    col = lambda off: pl.BlockSpec((S, cw), lambda j, off=off: (0, off * nb + j))
    return nb, col(0), col(1), col(2)


def _conv_fwd(name, proj, convw3, conv_width):
    S = proj.shape[0]
    cw = convw3.shape[2]
    nb, xc_s, bg_s, cg_s = _conv_specs(S, cw, conv_width)

    def body(xc_ref, bg_ref, cg_ref, w_ref, o_ref):
        u = cg_ref[...].astype(F32) * xc_ref[...].astype(F32)
        w = w_ref[...]
        cv = w[2:3, :] * u + w[1:2, :] * _shift_down(u, 1) + w[0:1, :] * _shift_down(u, 2)
        o_ref[...] = (bg_ref[...].astype(F32) * cv).astype(BF16)

    return pl.pallas_call(
        body, name=name, grid=(nb,),
        in_specs=[xc_s, bg_s, cg_s, pl.BlockSpec((None, CONV_K, cw), lambda j: (j, 0, 0))],
        out_specs=pl.BlockSpec((S, cw), lambda j: (0, j)), out_shape=jax.ShapeDtypeStruct((S, conv_width), BF16),
        compiler_params=_cparams(("parallel",)),
    )(proj, proj, proj, convw3)


def _conv_bwd(name, proj, convw3, da, conv_width):
    S = proj.shape[0]
    cw = convw3.shape[2]
    nb, xc_s, bg_s, cg_s = _conv_specs(S, cw, conv_width)

    def body(xc_ref, bg_ref, cg_ref, w_ref, da_ref, dxc_ref, dbg_ref, dcg_ref, dw_ref):
        xc, cg = xc_ref[...].astype(F32), cg_ref[...].astype(F32)
        u = cg * xc
        w = w_ref[...]
        u1, u2 = _shift_down(u, 1), _shift_down(u, 2)
        cv = w[2:3, :] * u + w[1:2, :] * u1 + w[0:1, :] * u2
        dav = da_ref[...]
        dbg_ref[...] = (dav * cv).astype(BF16)
        dcv = dav * bg_ref[...].astype(F32)
        du = w[2:3, :] * dcv + w[1:2, :] * _shift_up(dcv, 1) + w[0:1, :] * _shift_up(dcv, 2)
        dxc_ref[...] = (du * cg).astype(BF16)
        dcg_ref[...] = (du * xc).astype(BF16)
        dw_ref[0:1, :] = jnp.sum(dcv * u2, axis=0, keepdims=True)
        dw_ref[1:2, :] = jnp.sum(dcv * u1, axis=0, keepdims=True)
        dw_ref[2:3, :] = jnp.sum(dcv * u, axis=0, keepdims=True)

    wspec = pl.BlockSpec((None, CONV_K, cw), lambda j: (j, 0, 0))
    ospec = pl.BlockSpec((S, cw), lambda j: (0, j))
    act = jax.ShapeDtypeStruct((S, conv_width), BF16)
    return pl.pallas_call(
        body, name=name, grid=(nb,), in_specs=[xc_s, bg_s, cg_s, wspec, ospec],
        out_specs=(ospec, ospec, ospec, wspec),
        out_shape=(act, act, act, jax.ShapeDtypeStruct(convw3.shape, F32)),
        compiler_params=_cparams(("parallel",)),
    )(proj, proj, proj, convw3, da)


def _prep_consts(S, dh, width):
    rot = dh // 4
    half = rot // 2
    inv_freq = 1.0 / (ROPE_THETA ** (jnp.arange(0, rot, 2, dtype=F32) / rot))
    ang = jnp.arange(S, dtype=F32)[:, None] * inv_freq[None, :]
    zeros = jnp.zeros((S, dh - rot), F32)
    cos = jnp.concatenate([jnp.cos(ang), jnp.cos(ang), 1.0 + zeros], axis=1)
    sin_next = jnp.concatenate([-jnp.sin(ang), 0.0 * ang, zeros], axis=1)
    sin_prev = jnp.concatenate([0.0 * ang, jnp.sin(ang), zeros], axis=1)
    reps = min(LANES, width) // dh
    tables = [jnp.tile(t, (1, reps)) for t in (cos, sin_next, sin_prev)]
    mean = np.kron(np.eye(width // dh, dtype=np.float32), np.full((dh, dh), 1.0 / dh, np.float32))
    return (*tables, jnp.asarray(mean, BF16), half)


def _head_mean(p, mean):
    hi = p.astype(BF16)
    lo = (p - hi.astype(F32)).astype(BF16)
    return jnp.dot(hi, mean, preferred_element_type=F32) + jnp.dot(lo, mean, preferred_element_type=F32)


def _prep_specs(S, width, off, tw):
    assert off % width == 0
    tm = _tile(S, 512, 16)
    x = pl.BlockSpec((tm, width), lambda i: (i, off // width))
    row = pl.BlockSpec((tm, width), lambda i: (i, 0))
    tab = pl.BlockSpec((tm, tw), lambda i: (i, 0))
    vec = pl.BlockSpec((1, width), lambda i: (0, 0))
    mat = pl.BlockSpec((width, width), lambda i: (0, 0))
    return tm, x, row, tab, vec, mat


def _qk_prep(name, proj, off, width, gain_row, consts):
    S = proj.shape[0]
    cos, sin_next, sin_prev, mean, half = consts
    tm, x, row, tab, vec, mat = _prep_specs(S, width, off, cos.shape[1])
    reps = width // cos.shape[1]

    def body(x_ref, g_ref, c_ref, sn_ref, sp_ref, m_ref, o_ref):
        xv = x_ref[...].astype(F32)
        y = xv * lax.rsqrt(_head_mean(xv * xv, m_ref[...]) + RMS_EPS) * g_ref[...]
        t = lambda r: jnp.tile(r[...], (1, reps))
        o_ref[...] = (y * t(c_ref) + pltpu.roll(y, width - half, 1) * t(sn_ref) + pltpu.roll(y, half, 1) * t(sp_ref)).astype(BF16)

    return pl.pallas_call(
        body, name=name, grid=(S // tm,), in_specs=[x, vec, tab, tab, tab, mat], out_specs=row,
        out_shape=jax.ShapeDtypeStruct((S, width), BF16), compiler_params=_cparams(("parallel",)),
    )(proj, gain_row, cos, sin_next, sin_prev, mean)


def _qk_prep_bwd(name, proj, off, width, gain_row, consts, dout):
    S = proj.shape[0]
    cos, sin_next, sin_prev, mean, half = consts
    tm, x, row, tab, vec, mat = _prep_specs(S, width, off, cos.shape[1])
    reps = width // cos.shape[1]

    def body(x_ref, g_ref, c_ref, sn_ref, sp_ref, m_ref, do_ref, dx_ref, dg_ref):
        xv = x_ref[...].astype(F32)
        r = lax.rsqrt(_head_mean(xv * xv, m_ref[...]) + RMS_EPS)
        xhat = xv * r
        dov = do_ref[...]
        t = lambda ref: jnp.tile(ref[...], (1, reps))
        dy = dov * t(c_ref) + pltpu.roll(dov * t(sn_ref), half, 1) + pltpu.roll(dov * t(sp_ref), width - half, 1)
        dxhat = dy * g_ref[...]
        dx_ref[...] = (r * (dxhat - xhat * _head_mean(dxhat * xhat, m_ref[...]))).astype(BF16)

        @pl.when(pl.program_id(0) == 0)
        def _():
            dg_ref[...] = jnp.zeros_like(dg_ref)

        dg_ref[...] += jnp.sum(dy * xhat, axis=0, keepdims=True)

    return pl.pallas_call(
        body, name=name, grid=(S // tm,), in_specs=[x, vec, tab, tab, tab, mat, row], out_specs=(row, vec),
        out_shape=(jax.ShapeDtypeStruct((S, width), BF16), jax.ShapeDtypeStruct((1, width), F32)),
        compiler_params=_cparams(("arbitrary",)),
    )(proj, gain_row, cos, sin_next, sin_prev, mean, dout)


def _attn_probs(q, kp, kc, sink_col, n, scale):
    rows = q.shape[0]
    sp = lax.dot_general(q, kp, NT, preferred_element_type=F32) * scale
    sc = lax.dot_general(q, kc, NT, preferred_element_type=F32) * scale
    qi = lax.broadcasted_iota(jnp.int32, (rows, BLOCK), 0) % BLOCK
    kj = lax.broadcasted_iota(jnp.int32, (rows, BLOCK), 1)
    sp = jnp.where((kj > qi) & (n > 0), sp, NEG_INF)
    sc = jnp.where(kj <= qi, sc, NEG_INF)
    m = jnp.maximum(jnp.maximum(jnp.max(sp, axis=-1, keepdims=True), jnp.max(sc, axis=-1, keepdims=True)), sink_col)
    pp, pc, ps = jnp.exp(sp - m), jnp.exp(sc - m), jnp.exp(sink_col - m)
    inv = 1.0 / (jnp.sum(pp, axis=-1, keepdims=True) + jnp.sum(pc, axis=-1, keepdims=True) + ps)
    return pp * inv, pc * inv, ps * inv


def _sink_col(sink_ref, hk, group):
    rows = group * BLOCK
    g = lax.broadcasted_iota(jnp.int32, (rows, 1), 0) // BLOCK
    col = jnp.zeros((rows, 1), F32)
    for i in range(group):
        col = jnp.where(g == i, sink_ref[hk * group + i], col)
    return col


def _attn_specs(group, S, dh):
    heads = pl.BlockSpec((group, S, dh), lambda hk: (hk, 0, 0))
    kv = pl.BlockSpec((None, S, dh), lambda hk: (hk, 0, 0))
    return heads, kv, pl.BlockSpec(memory_space=pltpu.SMEM)


def _block_rows(n):
    cur = pl.ds(pl.multiple_of(n * BLOCK, BLOCK), BLOCK)
    prev = pl.ds(pl.multiple_of(jnp.maximum(n - 1, 0) * BLOCK, BLOCK), BLOCK)
    return cur, prev


def _attn_fwd(name, q, k, v, sinks):
    HQ, S, dh = q.shape
    HKV = k.shape[0]
    group = HQ // HKV
    scale = dh ** -0.5
    heads, kv, smem = _attn_specs(group, S, dh)

    def body(q_ref, k_ref, v_ref, sink_ref, o_ref):
        sink = _sink_col(sink_ref, pl.program_id(0), group)

        def block(n, carry):
            cur, prev = _block_rows(n)
            qv = q_ref[:, cur, :].reshape(group * BLOCK, dh)
            pp, pc, _ = _attn_probs(qv, k_ref[prev, :], k_ref[cur, :], sink, n, scale)
            o = jnp.dot(pp.astype(BF16), v_ref[prev, :], preferred_element_type=F32)
            o = o + jnp.dot(pc.astype(BF16), v_ref[cur, :], preferred_element_type=F32)
            o_ref[:, cur, :] = o.reshape(group, BLOCK, dh).astype(BF16)
            return carry

        lax.fori_loop(0, S // BLOCK, block, 0)

    return pl.pallas_call(
        body, name=name, grid=(HKV,), in_specs=[heads, kv, kv, smem], out_specs=heads,
        out_shape=jax.ShapeDtypeStruct((HQ, S, dh), BF16), compiler_params=_cparams(("parallel",)),
    )(q, k, v, sinks)


def _attn_bwd(name, q, k, v, sinks, do):
    HQ, S, dh = q.shape
    HKV = k.shape[0]
    group = HQ // HKV
    scale = dh ** -0.5
    heads, kv, smem = _attn_specs(group, S, dh)
    sk = pl.BlockSpec((None, group, LANES), lambda hk: (hk, 0, 0))

    def body(q_ref, k_ref, v_ref, sink_ref, do_ref, dq_ref, dk_ref, dv_ref, ds_ref):
        rows = group * BLOCK
        sink = _sink_col(sink_ref, pl.program_id(0), group)
        dk_ref[...] = jnp.zeros_like(dk_ref)
        dv_ref[...] = jnp.zeros_like(dv_ref)
        tdot = lambda a, b: lax.dot_general(a, b, TN, preferred_element_type=F32)

        def block(n, dsink):
            cur, prev = _block_rows(n)
            qv = q_ref[:, cur, :].reshape(rows, dh)
            dov = do_ref[:, cur, :].reshape(rows, dh)
            kp, kc, vp, vc = k_ref[prev, :], k_ref[cur, :], v_ref[prev, :], v_ref[cur, :]
            pp, pc, ps = _attn_probs(qv, kp, kc, sink, n, scale)
            dpp = lax.dot_general(dov, vp, NT, preferred_element_type=F32)
            dpc = lax.dot_general(dov, vc, NT, preferred_element_type=F32)
            delta = jnp.sum(pp * dpp, axis=-1, keepdims=True) + jnp.sum(pc * dpc, axis=-1, keepdims=True)
            dsp = (pp * (dpp - delta) * scale).astype(BF16)
            dsc = (pc * (dpc - delta) * scale).astype(BF16)
            dq = jnp.dot(dsp, kp, preferred_element_type=F32) + jnp.dot(dsc, kc, preferred_element_type=F32)
            dq_ref[:, cur, :] = dq.reshape(group, BLOCK, dh)
            dk_ref[prev, :] += tdot(dsp, qv)
            dv_ref[prev, :] += tdot(pp.astype(BF16), dov)
            dk_ref[cur, :] += tdot(dsc, qv)
            dv_ref[cur, :] += tdot(pc.astype(BF16), dov)
            return dsink - jnp.sum((ps * delta).reshape(group, BLOCK, 1), axis=1)

        pair = lambda m, dsink: block(2 * m + 1, block(2 * m, dsink))
        dsink = lax.fori_loop(0, S // (2 * BLOCK), pair, jnp.zeros((group, 1), F32))
        ds_ref[...] = jnp.broadcast_to(dsink, (group, LANES))

    return pl.pallas_call(
        body, name=name, grid=(HKV,), in_specs=[heads, kv, kv, smem, heads], out_specs=(heads, kv, kv, sk),
        out_shape=(jax.ShapeDtypeStruct((HQ, S, dh), F32), jax.ShapeDtypeStruct((HKV, S, dh), F32),
                   jax.ShapeDtypeStruct((HKV, S, dh), F32), jax.ShapeDtypeStruct((HKV, group, LANES), F32)),
        compiler_params=_cparams(("parallel",)),
    )(q, k, v, sinks, do)


RESIDENT = pl.Buffered(1)


def _gate_blocks(tm, Ns, off):
    assert off % Ns == 0
    return [pl.BlockSpec((tm, Ns), lambda i, k=k: (i, off // Ns + k)) for k in range(N_CHIPS)]


def _mixer_out_fwd(name, aconv, o, woc3, woa3, wo, proj, x1, ga_off, gb_off):
    S, D = x1.shape
    Ns = woc3.shape[2]
    tm = _tile(S, 256, 16)

    def body(a_ref, o_ref, woc_ref, woa_ref, wo_ref, x1_ref, *rest):
        ga_refs, gb_refs = rest[:N_CHIPS], rest[N_CHIPS:2 * N_CHIPS]
        ya_ref, yb_ref, m_ref, x2_ref = rest[2 * N_CHIPS:]
        av, ov = a_ref[...], o_ref[...]
        for s in range(N_CHIPS):
            cols = slice(s * Ns, (s + 1) * Ns)
            ya = jnp.dot(av, woc_ref[s], preferred_element_type=F32)
            yb = jnp.dot(ov, woa_ref[s], preferred_element_type=F32)
            ya_ref[:, cols] = ya.astype(BF16)
            yb_ref[:, cols] = yb.astype(BF16)
            ga, gb = ga_refs[s][...].astype(F32), gb_refs[s][...].astype(F32)
            m_ref[:, cols] = (_sigmoid(ga) * ya + _sigmoid(gb) * yb).astype(BF16)
        x2_ref[...] = x1_ref[...] + jnp.dot(m_ref[...], wo_ref[...], preferred_element_type=F32)

    row = lambda w: pl.BlockSpec((tm, w), lambda i: (i, 0))
    whole3 = lambda a: pl.BlockSpec(a.shape, lambda i: (0, 0, 0), pipeline_mode=RESIDENT)
    act = jax.ShapeDtypeStruct((S, D), BF16)
    return pl.pallas_call(
        body, name=name, grid=(S // tm,),
        in_specs=[row(aconv.shape[1]), row(o.shape[1]), whole3(woc3), whole3(woa3),
                  pl.BlockSpec(wo.shape, lambda i: (0, 0), pipeline_mode=RESIDENT), row(D)]
        + _gate_blocks(tm, Ns, ga_off) + _gate_blocks(tm, Ns, gb_off),
        out_specs=(row(D), row(D), row(D), row(D)), out_shape=(act, act, act, jax.ShapeDtypeStruct((S, D), F32)),
        compiler_params=_cparams(("parallel",)),
    )(aconv, o, woc3, woa3, wo, x1, *([proj] * (2 * N_CHIPS)))


def _mixer_out_bwd(name, dx2_b, wo, ya, yb, proj, woc3, woa3, ga_off, gb_off, dep):
    S, D = dx2_b.shape
    K, Ns = woc3.shape[1], woc3.shape[2]
    tm = _tile(S, 256, 16)

    def body(dx_ref, wo_ref, ya_ref, yb_ref, woc_ref, woa_ref, *rest):
        ga_refs, gb_refs = rest[:N_CHIPS], rest[N_CHIPS:2 * N_CHIPS]
        dga_ref, dgb_ref, dya_ref, dyb_ref, da_ref, do_ref = rest[-6:]
        dm = lax.dot_general(dx_ref[...], wo_ref[...], NT, preferred_element_type=F32)
        da = do = None
        for s in range(N_CHIPS):
            cols = slice(s * Ns, (s + 1) * Ns)
            dms = dm[:, cols]
            sa, sb = _sigmoid(ga_refs[s][...].astype(F32)), _sigmoid(gb_refs[s][...].astype(F32))
            dga_ref[:, cols] = (dms * ya_ref[:, cols].astype(F32) * sa * (1.0 - sa)).astype(BF16)
            dgb_ref[:, cols] = (dms * yb_ref[:, cols].astype(F32) * sb * (1.0 - sb)).astype(BF16)
            dya, dyb = (dms * sa).astype(BF16), (dms * sb).astype(BF16)
            dya_ref[:, cols] = dya
            dyb_ref[:, cols] = dyb
            pa = lax.dot_general(dya, woc_ref[s], NT, preferred_element_type=F32)
            pb = lax.dot_general(dyb, woa_ref[s], NT, preferred_element_type=F32)
            da, do = (pa, pb) if da is None else (da + pa, do + pb)
        da_ref[...] = da
        do_ref[...] = do.astype(BF16)

    row = lambda w: pl.BlockSpec((tm, w), lambda i: (i, 0))
    whole3 = lambda a: pl.BlockSpec(a.shape, lambda i: (0, 0, 0), pipeline_mode=RESIDENT)
    act = jax.ShapeDtypeStruct((S, D), BF16)
    return pl.pallas_call(
        body, name=name, grid=(S // tm,),
        in_specs=[row(D), pl.BlockSpec(wo.shape, lambda i: (0, 0), pipeline_mode=RESIDENT), row(D), row(D), whole3(woc3), whole3(woa3)]
        + _gate_blocks(tm, Ns, ga_off) + _gate_blocks(tm, Ns, gb_off) + [pl.BlockSpec(dep.shape, lambda i: (0, 0))],
        out_specs=(row(D), row(D), row(D), row(D), row(K), row(K)),
        out_shape=(act, act, act, act, jax.ShapeDtypeStruct((S, K), F32), jax.ShapeDtypeStruct((S, K), BF16)),
        compiler_params=_cparams(("parallel",)),
    )(dx2_b, wo, ya, yb, woc3, woa3, *([proj] * (2 * N_CHIPS)), dep)


ANY = pl.BlockSpec(memory_space=pl.ANY)


def _row_tile(rows, cols, n_arrays):
    want = max(16, (VMEM_LIMIT_V7X // 2) // (2 * n_arrays * cols * 4))
    return _tile(rows, want, 16)


def _cast_to_slot(name, w, dtype, p_arr, dep=None):
    R, C = w.shape
    tr = _row_tile(R, C, 2)
    extra = () if dep is None else (dep,)

    def body(p_ref, w_ref, *rest):
        rest[-1][...] = w_ref[...].astype(dtype)

    return pl.pallas_call(
        body, name=name,
        grid_spec=pltpu.PrefetchScalarGridSpec(
            num_scalar_prefetch=1, grid=(R // tr,),
            in_specs=[pl.BlockSpec((tr, C), lambda i, p_ref: (i, 0))] + [pl.BlockSpec(d.shape, lambda i, p_ref: (0, 0)) for d in extra],
            out_specs=pl.BlockSpec((None, tr, C), lambda i, p_ref: (p_ref[0], i, 0))),
        out_shape=jax.ShapeDtypeStruct((N_CHIPS, R, C), dtype), compiler_params=_cparams(("parallel",)),
    )(p_arr, w, *extra)


def _add_half(name, g3, r3, c_arr):
    n, h, C = r3.shape
    tr = _row_tile(h, C, 3)
    nb = h // tr

    def body(c_ref, g_ref, r_ref, o_ref):
        o_ref[...] = (g_ref[...].astype(F32) + r_ref[...].astype(F32)).astype(BF16)

    blk = pl.BlockSpec((None, tr, C), lambda s, i, c_ref: (s, i, 0))
    return pl.pallas_call(
        body, name=name,
        grid_spec=pltpu.PrefetchScalarGridSpec(
            num_scalar_prefetch=1, grid=(n, nb),
            in_specs=[pl.BlockSpec((None, tr, C), lambda s, i, c_ref: (s, c_ref[0] * nb + i, 0)), blk], out_specs=blk),
        out_shape=jax.ShapeDtypeStruct(r3.shape, BF16), compiler_params=_cparams(("parallel", "parallel")),
    )(c_arr, g3, r3)


def _add_chips(name, t3, r3, cp_arr):
    n, h, C = r3.shape
    tr = _row_tile(h, C, 6)
    nb = h // tr

    def body(cp_ref, t_ref, r0_ref, r1_ref, r2_ref, r3_ref, o_ref):
        p = cp_ref[1]
        total = None
        for a, r_ref in enumerate((r0_ref, r1_ref, r2_ref, r3_ref)):
            part = jnp.where(p == a, t_ref[...], r_ref[...]).astype(F32)
            total = part if total is None else total + part
        o_ref[...] = total

    def part(a):
        return pl.BlockSpec((None, tr, C), lambda i, cp_ref: (jnp.where(cp_ref[1] == a, (a + 1) % N_CHIPS, a), i, 0))

    return pl.pallas_call(
        body, name=name,
        grid_spec=pltpu.PrefetchScalarGridSpec(
            num_scalar_prefetch=1, grid=(nb,),
            in_specs=[pl.BlockSpec((None, tr, C), lambda i, cp_ref: (cp_ref[1], i, 0)), part(0), part(1), part(2), part(3)],
            out_specs=pl.BlockSpec((tr, C), lambda i, cp_ref: (cp_ref[0] * nb + i, 0))),
        out_shape=jax.ShapeDtypeStruct((2 * h, C), F32), compiler_params=_cparams(("parallel",)),
    )(cp_arr, t3, r3, r3, r3, r3)


def _adamw(name, w, g, m, v, deps=()):
    R, C = w.shape
    extra = tuple(deps)
    tr = _row_tile(R, C, 8)
    c1 = 1.0 - ADAM_B1 ** ADAM_STEP
    c2 = 1.0 - ADAM_B2 ** ADAM_STEP

    def body(w_ref, g_ref, m_ref, v_ref, *rest):
        go_ref, d_ref, nm_ref, nv_ref = rest[-4:]
        gv = g_ref[...]
        go_ref[...] = gv
        nm = ADAM_B1 * m_ref[...] + (1.0 - ADAM_B1) * gv
        nv = ADAM_B2 * v_ref[...] + (1.0 - ADAM_B2) * (gv * gv)
        d_ref[...] = -ADAM_LR * ((nm / c1) / (jnp.sqrt(nv / c2) + ADAM_EPS) + ADAM_WD * w_ref[...])
        nm_ref[...] = nm
        nv_ref[...] = nv

    blk = pl.BlockSpec((tr, C), lambda i: (i, 0))
    o = jax.ShapeDtypeStruct((R, C), F32)
    return pl.pallas_call(
        body, name=name, grid=(R // tr,), in_specs=[blk, blk, blk, blk] + [ANY] * len(extra), out_specs=(blk, blk, blk, blk),
        out_shape=(o, o, o, o), compiler_params=_cparams(("parallel",)),
    )(w, g, m, v, *extra)


def _place():
    x, y, c = lax.axis_index("x"), lax.axis_index("y"), lax.axis_index("c")
    chips = [(1 - x, y), (x, 1 - y), (1 - x, 1 - y)]
    return x, y, c, 2 * x + y, chips


HBM = pl.BlockSpec(memory_space=pltpu.HBM)
SEM = pl.BlockSpec(memory_space=pltpu.SEMAPHORE)
TOKEN = jax.ShapeDtypeStruct((8, LANES), F32)
DATAFLOW = pltpu.SideEffectType.DATAFLOW_SIDE_EFFECTING


def _hbm(a):
    return pltpu.with_memory_space_constraint(a, pltpu.HBM)


def _gather_blocks(bufs, i, c, p, chips):
    if bufs[i].shape[1] % 16:
        return bufs[i].at[p], [bufs[i].at[2 * cx + cy] for cx, cy in chips]
    h = bufs[i].shape[1] // 2
    rows = pl.ds(pl.multiple_of(c * h, 16), h)
    return bufs[i].at[p, rows], [bufs[i].at[2 * cx + cy, rows] for cx, cy in chips]


def _gather_start(name, groups, dep):
    slots = [s for g in groups for s in g]
    n, ng = len(slots), len(groups)

    def body(*refs):
        bufs, sems, token = refs[:n], refs[n + 1:n + 1 + 2 * ng], refs[-1]
        x, y, c, p, chips = _place()
        i = 0
        for gi, g in enumerate(groups):
            send, recv = sems[2 * gi], sems[2 * gi + 1]
            for k in range(len(g)):
                mine, _ = _gather_blocks(bufs, i, c, p, chips)
                for j, chip in enumerate(chips):
                    pltpu.make_async_remote_copy(src_ref=mine, dst_ref=mine, send_sem=send.at[3 * k + j], recv_sem=recv.at[3 * k + j],
                                                 device_id=(*chip, c), device_id_type=MESH).start()
                i += 1
        token[...] = jnp.zeros_like(token)

    sem_shapes = [pltpu.SemaphoreType.DMA((3 * len(g),)) for g in groups for _ in range(2)]
    out = pl.pallas_call(
        body, name=name, in_specs=[HBM] * n + [ANY],
        out_specs=(*([SEM] * (2 * ng)), *([HBM] * n), pl.BlockSpec(memory_space=pltpu.VMEM)),
        out_shape=(*sem_shapes, *[pltpu.HBM(s.shape, s.dtype) for s in slots], TOKEN),
        input_output_aliases={i: 2 * ng + i for i in range(n)},
        compiler_params=pltpu.CompilerParams(has_side_effects=DATAFLOW),
    )(*[_hbm(s) for s in slots], dep)
    started, i = [], 2 * ng
    for gi, g in enumerate(groups):
        started.append((out[2 * gi], out[2 * gi + 1], list(out[i:i + len(g)])))
        i += len(g)
    return started, out[-1]


def _gather_wait(name, send, recv, slots, after):
    n = len(slots)

    def body(*refs):
        bufs, send, recv = refs[:n], refs[n], refs[n + 1]
        x, y, c, p, chips = _place()
        for i in range(n):
            mine, landed = _gather_blocks(bufs, i, c, p, chips)
            for j, chip in enumerate(chips):
                cp = pltpu.make_async_remote_copy(src_ref=mine, dst_ref=landed[j], send_sem=send.at[3 * i + j],
                                                  recv_sem=recv.at[3 * i + j], device_id=(*chip, c), device_id_type=MESH)
                cp.wait_send()
                cp.wait_recv()

    return list(pl.pallas_call(
        body, name=name, in_specs=[HBM] * n + [SEM, SEM, ANY], out_specs=tuple([HBM] * n),
        out_shape=tuple(pltpu.HBM(s.shape, s.dtype) for s in slots),
        input_output_aliases={i: i for i in range(n)},
        compiler_params=pltpu.CompilerParams(has_side_effects=DATAFLOW),
    )(*slots, send, recv, after))


def _gather_forward(name, slots):
    idx = [i for i, s in enumerate(slots) if s.shape[1] % 16 == 0]
    n = len(slots)

    def body(*refs):
        bufs = refs[n:2 * n]
        send, recv = refs[2 * n:]
        x, y, c, p, chips = _place()

        def rdma(k, ref):
            return pltpu.make_async_remote_copy(src_ref=ref, dst_ref=ref, send_sem=send.at[k], recv_sem=recv.at[k],
                                                device_id=(x, y, 1 - c), device_id_type=MESH)

        cps = []
        for k, i in enumerate(idx):
            for j, ref in enumerate(_gather_blocks(bufs, i, c, p, chips)[1]):
                cps.append(rdma(3 * k + j, ref))
                cps[-1].start()
        for k, i in enumerate(idx):
            for j, ref in enumerate(_gather_blocks(bufs, i, 1 - c, p, chips)[1]):
                rdma(3 * k + j, ref).wait_recv()
        for cp in cps:
            cp.wait_send()

    return list(pl.pallas_call(
        body, name=name, in_specs=[ANY] * n, out_specs=tuple([ANY] * n),
        out_shape=tuple(jax.ShapeDtypeStruct(s.shape, s.dtype) for s in slots),
        scratch_shapes=[pltpu.SemaphoreType.DMA((3 * len(idx),)), pltpu.SemaphoreType.DMA((3 * len(idx),))],
        input_output_aliases={i: i for i in range(n)},
        compiler_params=pltpu.CompilerParams(has_side_effects=True),
    )(*slots))


def _swap_copy(grads, lands, send, recv, i, x, y, c):
    h = grads[i].shape[1] // 2
    other = pl.ds(pl.multiple_of((1 - c) * h, 16), h)
    return pltpu.make_async_remote_copy(src_ref=grads[i].at[:, other, :], dst_ref=lands[i], send_sem=send.at[i],
                                        recv_sem=recv.at[i], device_id=(x, y, 1 - c), device_id_type=MESH)


def _swap_wait(name, send, recv, grads, lands, after):
    n = len(grads)

    def body(*refs):
        ins, lands, send, recv = refs[:n], refs[n:2 * n], refs[2 * n], refs[2 * n + 1]
        x, y, c, p, chips = _place()
        for i in range(n):
            cp = _swap_copy(ins, lands, send, recv, i, x, y, c)
            cp.wait_send()
            cp.wait_recv()

    shapes = [pltpu.HBM(t.shape, t.dtype) for t in list(grads) + list(lands)]
    out = pl.pallas_call(
        body, name=name, in_specs=[HBM] * (2 * n) + [SEM, SEM, ANY], out_specs=tuple([HBM] * (2 * n)),
        out_shape=tuple(shapes), input_output_aliases={i: i for i in range(2 * n)},
        compiler_params=pltpu.CompilerParams(has_side_effects=DATAFLOW),
    )(*grads, *lands, send, recv, after)
    return list(out[:n]), list(out[n:])


def _reduce_starts(name, grads, parts):
    ng, npt = len(grads), len(parts)
    halves = [(g.shape[0], g.shape[1] // 2, g.shape[2]) for g in grads]
    arrays = list(grads) + [lax.empty(s, g.dtype) for s, g in zip(halves, grads)] + list(parts) + [lax.empty(t.shape, t.dtype) for t in parts]
    na = len(arrays)
    sems = ([pltpu.SemaphoreType.DMA((ng,))] * 2 if ng else []) + ([pltpu.SemaphoreType.DMA((3 * npt,))] * 2 if npt else [])
    ns = len(sems)

    def body(*refs):
        ins, sem, token = refs[:na], list(refs[na:na + ns]), refs[-1]
        x, y, c, p, chips = _place()
        if ng:
            for i in range(ng):
                _swap_copy(ins[:ng], ins[ng:2 * ng], sem[0], sem[1], i, x, y, c).start()
        if npt:
            src, land, send, recv = ins[2 * ng:2 * ng + npt], ins[2 * ng + npt:], sem[-2], sem[-1]
            for i in range(npt):
                for j, (cx, cy) in enumerate(chips):
                    pltpu.make_async_remote_copy(src_ref=src[i].at[2 * cx + cy], dst_ref=land[i].at[p], send_sem=send.at[3 * i + j],
                                                 recv_sem=recv.at[3 * i + j], device_id=(cx, cy, c), device_id_type=MESH).start()
        token[...] = jnp.zeros_like(token)

    out = pl.pallas_call(
        body, name=name, in_specs=[HBM] * na,
        out_specs=(*([SEM] * ns), *([HBM] * na), pl.BlockSpec(memory_space=pltpu.VMEM)),
        out_shape=(*sems, *[pltpu.HBM(a.shape, a.dtype) for a in arrays], TOKEN),
        input_output_aliases={i: ns + i for i in range(na)},
        compiler_params=pltpu.CompilerParams(has_side_effects=DATAFLOW),
    )(*[_hbm(a) for a in arrays])
    bufs = list(out[ns:ns + na])
    swap = (out[0], out[1], bufs[:ng], bufs[ng:2 * ng]) if ng else None
    exch = (out[ns - 2], out[ns - 1], bufs[2 * ng:2 * ng + npt], bufs[2 * ng + npt:]) if npt else None
    return swap, exch, out[-1]


def _exchange_wait(name, send, recv, parts, lands, after):
    n = len(parts)

    def body(*refs):
        ins, lands, send, recv = refs[:n], refs[n:2 * n], refs[2 * n], refs[2 * n + 1]
        x, y, c, p, chips = _place()
        for i in range(n):
            for j, (cx, cy) in enumerate(chips):
                q = 2 * cx + cy
                cp = pltpu.make_async_remote_copy(src_ref=ins[i].at[q], dst_ref=lands[i].at[q], send_sem=send.at[3 * i + j],
                                                  recv_sem=recv.at[3 * i + j], device_id=(cx, cy, c), device_id_type=MESH)
                cp.wait_send()
                cp.wait_recv()

    shapes = [pltpu.HBM(t.shape, t.dtype) for t in parts]
    out = pl.pallas_call(
        body, name=name, in_specs=[HBM] * (2 * n) + [SEM, SEM, ANY], out_specs=tuple([HBM] * (2 * n)),
        out_shape=(*shapes, *shapes), input_output_aliases={i: i for i in range(2 * n)},
        compiler_params=pltpu.CompilerParams(has_side_effects=DATAFLOW),
    )(*parts, *lands, send, recv, after)
    return list(out[:n]), list(out[n:])


def _join_copy(buf, send_sem, recv_sem, which, x, y, c):
    h = buf.shape[0] // 2
    rows = buf.at[pl.ds(pl.multiple_of(which * h, 8), h)]
    return pltpu.make_async_remote_copy(src_ref=rows, dst_ref=rows, send_sem=send_sem, recv_sem=recv_sem,
                                        device_id=(x, y, 1 - c), device_id_type=MESH)


def _join_start(name, groups):
    bufs = [b for g in groups for b in g]
    n, ng = len(bufs), len(groups)

    def body(*refs):
        ins, sems, token = refs[:n], refs[n:n + 2 * ng], refs[-1]
        x, y, c, p, chips = _place()
        i = 0
        for gi, g in enumerate(groups):
            for k in range(len(g)):
                _join_copy(ins[i], sems[2 * gi].at[k], sems[2 * gi + 1].at[k], c, x, y, c).start()
                i += 1
        token[...] = jnp.zeros_like(token)

    sem_shapes = [pltpu.SemaphoreType.DMA((len(g),)) for g in groups for _ in range(2)]
    out = pl.pallas_call(
        body, name=name, in_specs=[HBM] * n,
        out_specs=(*([SEM] * (2 * ng)), *([HBM] * n), pl.BlockSpec(memory_space=pltpu.VMEM)),
        out_shape=(*sem_shapes, *[pltpu.HBM(t.shape, t.dtype) for t in bufs], TOKEN),
        input_output_aliases={i: 2 * ng + i for i in range(n)},
        compiler_params=pltpu.CompilerParams(has_side_effects=DATAFLOW),
    )(*[_hbm(t) for t in bufs])
    started, i = [], 2 * ng
    for gi, g in enumerate(groups):
        started.append((out[2 * gi], out[2 * gi + 1], list(out[i:i + len(g)])))
        i += len(g)
    return started, out[-1]


def _join_wait(name, send, recv, bufs, after):
    n = len(bufs)

    def body(*refs):
        ins, send, recv = refs[:n], refs[n], refs[n + 1]
        x, y, c, p, chips = _place()
        for i in range(n):
            _join_copy(ins[i], send.at[i], recv.at[i], c, x, y, c).wait_send()
            _join_copy(ins[i], send.at[i], recv.at[i], 1 - c, x, y, c).wait_recv()

    return list(pl.pallas_call(
        body, name=name, in_specs=[HBM] * n + [SEM, SEM, ANY], out_specs=tuple([HBM] * n),
        out_shape=tuple(pltpu.HBM(t.shape, t.dtype) for t in bufs), input_output_aliases={i: i for i in range(n)},
        compiler_params=pltpu.CompilerParams(has_side_effects=DATAFLOW),
    )(*bufs, send, recv, after))


def _allreduce_small(name, pack, dep):
    R, W = pack.shape

    def body(in_ref, dep_ref, out_ref, slots, send, recv):
        x, y, c = lax.axis_index("x"), lax.axis_index("y"), lax.axis_index("c")
        me = 4 * x + 2 * y + c
        slots[0] = in_ref[...]
        cps = []
        for k in range(1, N_DEV):
            peer = (x ^ (k >> 2), y ^ ((k >> 1) & 1), c ^ (k & 1))
            cp = pltpu.make_async_remote_copy(src_ref=in_ref, dst_ref=slots.at[k], send_sem=send.at[k - 1],
                                              recv_sem=recv.at[k - 1], device_id=peer, device_id_type=MESH)
            cp.start()
            cps.append(cp)
        for cp in cps:
            cp.wait()
        total = slots[me]
        for a in range(1, N_DEV):
            total = total + slots[jnp.bitwise_xor(a, me)]
        out_ref[...] = total

    vmem = pl.BlockSpec(memory_space=pltpu.VMEM)
    return pl.pallas_call(
        body, name=name, in_specs=[vmem, ANY], out_specs=vmem, out_shape=jax.ShapeDtypeStruct((R, W), F32),
        scratch_shapes=[pltpu.VMEM((N_DEV, R, W), F32), pltpu.SemaphoreType.DMA((N_DEV - 1,)), pltpu.SemaphoreType.DMA((N_DEV - 1,))],
        compiler_params=pltpu.CompilerParams(has_side_effects=True),
    )(pack, dep)


def _heads(a, n_heads):
    S = a.shape[0]
    return a.reshape(S, n_heads, a.shape[1] // n_heads).transpose(1, 0, 2)


def _unheads(a):
    H, S, dh = a.shape
    return a.transpose(1, 0, 2).reshape(S, H * dh)


def _ffn_bwd(tag, xin, gain, wgu3, wd, saved, dxout, dxo_b, reduce_start, dep, flush=None):
    h, gu, act = saved
    D = xin.shape[1]
    tok = reduce_start({f"w_down{tag}": _mm_tn(f"dw_down_{tag}", act, dxo_b, 0.5, dep=dep).reshape(N_CHIPS, -1, D)})
    dgu = _ffn_down_bwd(f"ffn_down_bwd_{tag}", dxo_b, wd, gu, 0.5, dep=tok)
    tok = reduce_start({f"w_gu{tag}": _mm_tn_cols(f"dw_gu_{tag}", h, dgu, wgu3.shape[2], b_is_gu=True)})
    if flush is not None:
        tok = flush(tok)
    dh = _mm_nt_cols(f"ffn_up_bwd_{tag}", dgu, wgu3, a_is_gu=True, dep=tok)
    dxin, dxin_b, dgain = _rms_bwd(f"rms_bwd_{tag}", xin, gain, dh, dxout)
    return dxin, dxin_b, dgain, tok


def kernel(x, g_ffn1, w_gu1, w_down1, g_mix, w_in, conv_w, q_norm_g, k_norm_g, sinks, w_out_conv, w_out_attn, w_o, g_ffn2, w_gu2, w_down2, loss_target, m_g_ffn1, m_w_gu1, m_w_down1, m_g_mix, m_w_in, m_conv_w, m_q_norm_g, m_k_norm_g, m_sinks, m_w_out_conv, m_w_out_attn, m_w_o, m_g_ffn2, m_w_gu2, m_w_down2, v_g_ffn1, v_w_gu1, v_w_down1, v_g_mix, v_w_in, v_conv_w, v_q_norm_g, v_k_norm_g, v_sinks, v_w_out_conv, v_w_out_attn, v_w_o, v_g_ffn2, v_w_gu2, v_w_down2):
    S, D = x.shape[1], x.shape[2]
    dh = q_norm_g.shape[1]
    HQ = sinks.shape[1]
    HKV = HQ // 4
    AW, KVW, CW = HQ * dh, HKV * dh, D // 2
    off_q, off_k, off_v = 3 * CW, 3 * CW + AW, 3 * CW + AW + KVW
    off_ga, off_gb = off_v + KVW, off_v + KVW + D
    x0, target = x[0], loss_target[0]
    cx, cy, cc = lax.axis_index("x"), lax.axis_index("y"), lax.axis_index("c")
    chip = 2 * cx + cy
    p_arr = jnp.reshape(chip, (1,)).astype(jnp.int32)
    c_arr = jnp.reshape(cc, (1,)).astype(jnp.int32)
    cp_arr = jnp.stack([cc, chip]).astype(jnp.int32)
    wts = dict(g_ffn1=g_ffn1, w_gu1=w_gu1, w_down1=w_down1, g_mix=g_mix, w_in=w_in, conv_w=conv_w, q_norm_g=q_norm_g,
               k_norm_g=k_norm_g, sinks=sinks, w_out_conv=w_out_conv, w_out_attn=w_out_attn, w_o=w_o, g_ffn2=g_ffn2,
               w_gu2=w_gu2, w_down2=w_down2)
    ms = dict(g_ffn1=m_g_ffn1, w_gu1=m_w_gu1, w_down1=m_w_down1, g_mix=m_g_mix, w_in=m_w_in, conv_w=m_conv_w,
              q_norm_g=m_q_norm_g, k_norm_g=m_k_norm_g, sinks=m_sinks, w_out_conv=m_w_out_conv, w_out_attn=m_w_out_attn,
              w_o=m_w_o, g_ffn2=m_g_ffn2, w_gu2=m_w_gu2, w_down2=m_w_down2)
    vs = dict(g_ffn1=v_g_ffn1, w_gu1=v_w_gu1, w_down1=v_w_down1, g_mix=v_g_mix, w_in=v_w_in, conv_w=v_conv_w,
              q_norm_g=v_q_norm_g, k_norm_g=v_k_norm_g, sinks=v_sinks, w_out_conv=v_w_out_conv, w_out_attn=v_w_out_attn,
              w_o=v_w_o, g_ffn2=v_g_ffn2, w_gu2=v_w_gu2, w_down2=v_w_down2)
    order = list(wts)
    small_names = [k for k in order if not k.startswith("w_")]
    grad, delta, new_m, new_v = {}, {}, {}, {}

    def cast(keys, dep=None):
        return [_cast_to_slot(f"cast_{k}", wts[k][0], F32 if k == "conv_w" else BF16, p_arr, dep) for k in keys]

    def gather_finish(tag, started, after):
        send, recv, slots = started
        return _gather_forward(f"gather_forward_{tag}", _gather_wait(f"gather_wait_{tag}", send, recv, slots, after))

    swapping, pending = [], []

    def reduce_start(full, after=None):
        keys = [] if full is None else list(full)
        pkeys, parts = [], []
        if swapping:
            pkeys, send, recv, gs, lands = swapping.pop(0)
            gs, sib = _swap_wait(f"swap_wait_{pkeys[0]}", send, recv, gs, lands, after if full is None else full[keys[0]])
            parts = [_add_half(f"add_half_{k}", g, r, c_arr) for k, g, r in zip(pkeys, gs, sib)]
        swap, exch, tok = _reduce_starts(f"reduce_starts_{keys[0] if keys else 'last'}", [full[k] for k in keys], parts)
        if exch:
            pending.append((pkeys, *exch))
        if swap:
            swapping.append((keys, *swap))
        return tok

    def reduce_finish(entries, after):
        ready = []
        for keys, send, recv, parts, lands in entries:
            parts, lands = _exchange_wait(f"exchange_wait_{keys[0]}", send, recv, parts, lands, after)
            ready.append((keys, [_add_chips(f"add_chips_{k}", t, r, cp_arr) for k, t, r in zip(keys, parts, lands)]))
        started, last = _join_start(f"join_start_{ready[0][0][0]}", [halves for _, halves in ready])
        for (keys, _), (send, recv, halves) in zip(ready, started):
            for k, g2 in zip(keys, _join_wait(f"join_wait_{keys[0]}", send, recv, halves, last)):
                g2, d, nm, nv = _adamw(f"adamw_{k}", wts[k][0], g2, ms[k][0], vs[k][0], (last,))
                grad[k], delta[k], new_m[k], new_v[k] = g2[None], d[None], nm[None], nv[None]
                last = nv
        return last

    (st_gu1, st_d1), tok = _gather_start("gather_start_1", [cast(["w_gu1"]), cast(["w_down1"])], x0)
    later = ["w_in", "conv_w", "w_out_conv", "w_out_attn", "w_o", "w_gu2", "w_down2"]
    slot = dict(zip(later, cast(later, tok)))
    h1 = _rms_fwd("rms_fwd_1", x0, g_ffn1, slot["w_down2"])
    wgu1, = gather_finish("gu1", st_gu1, h1)
    (st_in, st_out, st_gu2, st_d2), tok = _gather_start(
        "gather_start_2", [[slot["w_in"], slot["conv_w"]], [slot["w_out_conv"], slot["w_out_attn"], slot["w_o"]],
                           [slot["w_gu2"]], [slot["w_down2"]]], wgu1)
    q_consts, k_consts = _prep_consts(S, dh, AW), _prep_consts(S, dh, KVW)
    qg_row, kg_row = jnp.tile(q_norm_g, (1, HQ)), jnp.tile(k_norm_g, (1, HKV))
    sink_vec = sinks[0]

    gu1, act1 = _ffn_up("ffn_up_1", h1, wgu1, tok)
    wd1 = gather_finish("d1", st_d1, act1)[0].reshape(-1, D)
    x1 = _mm_res("ffn_down_1", act1, wd1, x0, 0.5)
    win3, convw3 = gather_finish("in", st_in, x1)
    h2 = _rms_fwd("rms_fwd_mix", x1, g_mix)
    proj = _mm_cols("in_proj", h2, win3, BF16)
    aconv = _conv_fwd("conv_fwd", proj, convw3, CW)
    woc3, woa3, wo = gather_finish("out", st_out, aconv)
    wo = wo.reshape(-1, D)
    vh = _heads(proj[:, off_v:off_v + KVW], HKV)
    qn = _heads(_qk_prep("q_prep", proj, off_q, AW, qg_row, q_consts), HQ)
    kn = _heads(_qk_prep("k_prep", proj, off_k, KVW, kg_row, k_consts), HKV)
    oh = _attn_fwd("attn_fwd", qn, kn, vh, sink_vec)
    o = _unheads(oh)
    ya, yb, merged, x2 = _mixer_out_fwd("mixer_out", aconv, o, woc3, woa3, wo, proj, x1, off_ga, off_gb)
    wgu2, = gather_finish("gu2", st_gu2, x2)
    h3 = _rms_fwd("rms_fwd_2", x2, g_ffn2)
    gu2, act2 = _ffn_up("ffn_up_2", h3, wgu2)
    wd2 = gather_finish("d2", st_d2, act2)[0].reshape(-1, D)
    dy, dy_b, loss_lanes = _mm_res_loss("ffn_down_2_loss", act2, wd2, x2, 0.5, target)
    dx2, dx2_b, dg_ffn2, tok = _ffn_bwd("2", x2, g_ffn2, wgu2, wd2, (h3, gu2, act2), dy, dy_b, reduce_start, None)
    tok = reduce_start(dict(w_o=_mm_tn("dw_o", merged, dx2_b, dep=tok).reshape(N_CHIPS, -1, D)))
    dga, dgb, dya, dyb, daconv, do = _mixer_out_bwd("mixer_out_bwd", dx2_b, wo, ya, yb, proj, woc3, woa3, off_ga, off_gb, tok)
    dwoc = _mm_tn_cols("dw_out_conv", aconv, dya, woc3.shape[2])
    dwoa = _mm_tn_cols("dw_out_attn", o, dyb, woa3.shape[2])
    tok = reduce_start(dict(w_out_conv=dwoc, w_out_attn=dwoa))
    dxc, dbg, dcg, dconvw = _conv_bwd("conv_bwd", proj, convw3, daconv, CW)
    dqn, dkn, dvh, dsink3 = _attn_bwd("attn_bwd", qn, kn, vh, sink_vec, _heads(do, HQ))
    dq_raw, dqg = _qk_prep_bwd("q_prep_bwd", proj, off_q, AW, qg_row, q_consts, _unheads(dqn))
    dk_raw, dkg = _qk_prep_bwd("k_prep_bwd", proj, off_k, KVW, kg_row, k_consts, _unheads(dkn))
    dqg, dkg = dqg.reshape(HQ, dh).sum(axis=0, keepdims=True), dkg.reshape(HKV, dh).sum(axis=0, keepdims=True)
    dproj = jnp.concatenate([dxc, dbg, dcg, dq_raw, dk_raw, _unheads(dvh).astype(BF16), dga, dgb], axis=1)
    dh2 = _mm_nt_cols("in_proj_bwd", dproj, win3, dep=tok)
    tok = reduce_start(dict(w_in=_mm_tn_cols("dw_in", h2, dproj, win3.shape[2])))
    dx1, dx1_b, dg_mix = _rms_bwd("rms_bwd_mix", x1, g_mix, dh2, dx2)
    dx0, _, dg_ffn1, tok = _ffn_bwd("1", x0, g_ffn1, wgu1, wd1, (h1, gu1, act1), dx1, dx1_b, reduce_start, tok, lambda after: reduce_start(None, after))

    def rows8(a):
        a = a.reshape(-1, a.shape[-1])
        return jnp.pad(a, ((0, -a.shape[0] % 8), (0, D - a.shape[1])))

    misc = jnp.concatenate([dqg, dkg, dsink3[:, :, 0].reshape(1, HQ), loss_lanes], axis=1)
    done = reduce_finish(pending[:-2], dx0)
    tot = _allreduce_small("allreduce_small", jnp.concatenate([rows8(a) for a in (dg_ffn1, dg_mix, dg_ffn2, dconvw, misc)], axis=0), done)
    reduce_finish(pending[-2:], tot)

    cw_s = conv_w.shape[2]
    conv_row0, misc_row = 24, 24 + (-(-N_CHIPS * CONV_K // 8)) * 8
    small_g = dict(g_ffn1=tot[0:1], g_mix=tot[8:9], g_ffn2=tot[16:17],
                   conv_w=lax.dynamic_slice(tot, (conv_row0 + CONV_K * chip, 0), (CONV_K, cw_s)),
                   q_norm_g=tot[misc_row:misc_row + 1, 0:dh], k_norm_g=tot[misc_row:misc_row + 1, dh:2 * dh],
                   sinks=tot[misc_row:misc_row + 1, 2 * dh:2 * dh + HQ])
    loss = (0.5 / D) * jnp.sum(tot[misc_row, 2 * dh + HQ:2 * dh + HQ + LANES])

    def small_pack(src):
        return jnp.concatenate([rows8(src[k]) for k in small_names], axis=0)

    _, sd, sm, sv = _adamw("adamw_small", small_pack(wts), small_pack(small_g), small_pack(ms), small_pack(vs))
    for i, k in enumerate(small_names):
        shape = wts[k].shape
        nr, ncol = math.prod(shape[:-1]), shape[-1]
        grad[k] = small_g[k].reshape(shape)
        delta[k], new_m[k], new_v[k] = (a[8 * i:8 * i + nr, 0:ncol].reshape(shape) for a in (sd, sm, sv))
    return (loss, dx0[None], *[grad[k] for k in order], *[delta[k] for k in order],
            *[new_m[k] for k in order], *[new_v[k] for k in order])
```

```python
import math

import numpy as np
import jax
import jax.numpy as jnp
from jax import lax
from jax.experimental import pallas as pl
from jax.experimental.pallas import tpu as pltpu

F32 = jnp.float32
BF16 = jnp.bfloat16
MESH = pl.DeviceIdType.MESH

RMS_EPS = 1e-6
BLOCK = 128
ROPE_THETA = 500000.0
NEG_INF = -1e30
CONV_K = 3
ADAM_LR, ADAM_B1, ADAM_B2, ADAM_EPS, ADAM_WD, ADAM_STEP = 0.001, 0.9, 0.999, 1e-08, 0.01, 10

VMEM_LIMIT_V7X = 56 * 1024 * 1024
LANES = 128
N_CHIPS = 4
N_DEV = 8


def _tile(n, want, align=LANES):
    best = None
    t = align
    while t <= min(n, want):
        if n % t == 0:
            best = t
        t += align
    return best or n


def _cparams(sem):
    return pltpu.CompilerParams(dimension_semantics=sem, vmem_limit_bytes=VMEM_LIMIT_V7X)


def _sigmoid(x):
    return 1.0 / (1.0 + jnp.exp(-x))


NN = (((1,), (0,)), ((), ()))
NT = (((1,), (1,)), ((), ()))
TN = (((0,), (0,)), ((), ()))


def _mm(name, grid, ins, in_specs, compute, out_shape, out_specs, epilogue, dep=None, carried=False):
    if dep is not None:
        ins, in_specs = tuple(ins) + (dep,), list(in_specs) + [pl.BlockSpec(dep.shape, lambda *_: (0, 0))]
    n_in = len(ins)

    def body(*refs):
        epilogue(compute(refs[:n_in]), refs[:n_in], refs[n_in:])

    return pl.pallas_call(
        body, name=name, grid=grid, in_specs=in_specs, out_specs=out_specs, out_shape=out_shape,
        compiler_params=_cparams(("arbitrary" if carried else "parallel", "arbitrary")),
    )(*ins)


def _dot(dims, a=0, b=1):
    return lambda refs: [lax.dot_general(refs[a][...], refs[b][...], dims, preferred_element_type=F32)]


def _ffn_up(name, h, wgu3, dep=None):
    S, D = h.shape
    Ns = wgu3.shape[2]
    F = 2 * Ns
    tm, tn = _tile(S, 512), _tile(Ns, 1408)
    nbs = Ns // tn

    def compute(refs):
        hv = refs[0][...]
        return [jnp.dot(hv, refs[1][...], preferred_element_type=F32), jnp.dot(hv, refs[2][...], preferred_element_type=F32)]

    def epi(accs, in_refs, out_refs):
        g, u = accs
        dgu_ref, a_ref = out_refs
        sg = _sigmoid(g)
        silu = g * sg
        dgu_ref[0] = (u * (sg * (1.0 + g * (1.0 - sg)))).astype(BF16)
        dgu_ref[1] = silu.astype(BF16)
        a_ref[...] = (silu * u).astype(BF16)

    return _mm(
        name, (F // tn, S // tm), (h, wgu3, wgu3),
        [pl.BlockSpec((tm, D), lambda j, i: (i, 0)),
         pl.BlockSpec((None, D, tn), lambda j, i: (j // nbs, 0, j % nbs)),
         pl.BlockSpec((None, D, tn), lambda j, i: (2 + j // nbs, 0, j % nbs))],
        compute, (jax.ShapeDtypeStruct((2, S, F), BF16), jax.ShapeDtypeStruct((S, F), BF16)),
        (pl.BlockSpec((2, tm, tn), lambda j, i: (0, i, j)), pl.BlockSpec((tm, tn), lambda j, i: (i, j))), epi, dep=dep)


def _mm_res(name, a, w, res, scale):
    S, K = a.shape
    N = w.shape[1]
    tm, tn = _tile(S, 512), _tile(N, 512 if K > 2816 else 1024)

    def epi(accs, in_refs, out_refs):
        out_refs[0][...] = in_refs[2][...] + scale * accs[0]

    return _mm(
        name, (N // tn, S // tm), (a, w, res),
        [pl.BlockSpec((tm, K), lambda j, i: (i, 0)), pl.BlockSpec((K, tn), lambda j, i: (0, j)),
         pl.BlockSpec((tm, tn), lambda j, i: (i, j))],
        _dot(NN), jax.ShapeDtypeStruct((S, N), F32), pl.BlockSpec((tm, tn), lambda j, i: (i, j)), epi)


def _mm_res_loss(name, a, w, res, scale, target):
    S, K = a.shape
    N = w.shape[1]
    tm, tn = _tile(S, 512), _tile(N, 512 if K > 2816 else 1024)

    def epi(accs, in_refs, out_refs):
        dy_ref, dyb_ref, l_ref = out_refs
        e = in_refs[2][...] + scale * accs[0] - in_refs[3][...]
        dy_ref[...] = e * (1.0 / N)
        dyb_ref[...] = (e * (1.0 / N)).astype(BF16)
        col = jnp.sum(e * e, axis=0, keepdims=True)
        part = col[:, 0:LANES]
        for k in range(1, tn // LANES):
            part = part + col[:, k * LANES:(k + 1) * LANES]

        @pl.when((pl.program_id(0) == 0) & (pl.program_id(1) == 0))
        def _():
            l_ref[...] = jnp.zeros_like(l_ref)

        l_ref[...] += part

    tile = pl.BlockSpec((tm, tn), lambda j, i: (i, j))
    return _mm(
        name, (N // tn, S // tm), (a, w, res, target),
        [pl.BlockSpec((tm, K), lambda j, i: (i, 0)), pl.BlockSpec((K, tn), lambda j, i: (0, j)), tile, tile],
        _dot(NN), (jax.ShapeDtypeStruct((S, N), F32), jax.ShapeDtypeStruct((S, N), BF16), jax.ShapeDtypeStruct((1, LANES), F32)),
        (tile, tile, pl.BlockSpec((1, LANES), lambda j, i: (0, 0))), epi, carried=True)


def _mm_cols(name, a, w3, out_dtype):
    S, K = a.shape
    Ns = w3.shape[2]
    tm, tn = _tile(S, 512), _tile(Ns, 2304)
    nbs = Ns // tn

    def epi(accs, in_refs, out_refs):
        out_refs[0][...] = accs[0].astype(out_dtype)

    return _mm(
        name, (N_CHIPS * nbs, S // tm), (a, w3),
        [pl.BlockSpec((tm, K), lambda j, i: (i, 0)),
         pl.BlockSpec((None, K, tn), lambda j, i: (j // nbs, 0, j % nbs))],
        _dot(NN), jax.ShapeDtypeStruct((S, N_CHIPS * Ns), out_dtype), pl.BlockSpec((tm, tn), lambda j, i: (i, j)), epi)


def _ffn_down_bwd(name, dy, wd, gu, scale, dep=None):
    S, D = dy.shape
    F = wd.shape[0]
    tm, tn = _tile(S, 512), _tile(F, 1408)

    def epi(accs, in_refs, out_refs):
        da = scale * accs[0]
        out_refs[0][0] = (da * in_refs[2][0].astype(F32)).astype(BF16)
        out_refs[0][1] = (da * in_refs[2][1].astype(F32)).astype(BF16)

    return _mm(
        name, (F // tn, S // tm), (dy, wd, gu),
        [pl.BlockSpec((tm, D), lambda j, i: (i, 0)), pl.BlockSpec((tn, D), lambda j, i: (j, 0)),
         pl.BlockSpec((2, tm, tn), lambda j, i: (0, i, j))],
        _dot(NT), jax.ShapeDtypeStruct((2, S, F), BF16), pl.BlockSpec((2, tm, tn), lambda j, i: (0, i, j)), epi, dep=dep)


def _mm_nt_cols(name, a, w3, a_is_gu=False, dep=None):
    K, Ns = w3.shape[1], w3.shape[2]
    S = a.shape[1] if a_is_gu else a.shape[0]
    tm = _tile(S, 512)
    tn = _tile(K, max(LANES, (12 << 20) // (N_CHIPS * Ns * 2)))
    if a_is_gu:
        a_spec = pl.BlockSpec((2, tm, 2 * Ns), lambda i, j: (0, i, 0))
        part = lambda a_ref, s: a_ref[s // 2, :, (s % 2) * Ns:(s % 2 + 1) * Ns]
    else:
        a_spec = pl.BlockSpec((tm, N_CHIPS * Ns), lambda i, j: (i, 0))
        part = lambda a_ref, s: a_ref[:, s * Ns:(s + 1) * Ns]

    def compute(refs):
        total = None
        for s in range(N_CHIPS):
            prod = lax.dot_general(part(refs[0], s), refs[1][s], NT, preferred_element_type=F32)
            total = prod if total is None else total + prod
        return [total]

    def epi(accs, in_refs, out_refs):
        out_refs[0][...] = accs[0]

    return _mm(
        name, (S // tm, K // tn), (a, w3), [a_spec, pl.BlockSpec((N_CHIPS, tn, Ns), lambda i, j: (0, j, 0))],
        compute, jax.ShapeDtypeStruct((S, K), F32), pl.BlockSpec((tm, tn), lambda i, j: (i, j)), epi, dep=dep)


def _mm_tn(name, a, b, scale=1.0, dep=None):
    S, K = a.shape
    N = b.shape[1]
    tm, tn = _tile(K, 1408), _tile(N, 1024)

    def epi(accs, in_refs, out_refs):
        out_refs[0][...] = (scale * accs[0]).astype(BF16)

    return _mm(
        name, (N // tn, K // tm), (a, b),
        [pl.BlockSpec((S, tm), lambda j, i: (0, i)), pl.BlockSpec((S, tn), lambda j, i: (0, j))],
        _dot(TN), jax.ShapeDtypeStruct((K, N), BF16), pl.BlockSpec((tm, tn), lambda j, i: (i, j)), epi, dep=dep)


def _mm_tn_cols(name, a, b, Ns, b_is_gu=False, dep=None):
    S, K = a.shape
    tm, tn = _tile(K, 1024), _tile(Ns, 2304)
    nbs = Ns // tn
    if b_is_gu:
        b_spec = pl.BlockSpec((None, S, tn), lambda j, i: (j // (2 * nbs), 0, j % (2 * nbs)))
    else:
        b_spec = pl.BlockSpec((S, tn), lambda j, i: (0, j))

    def epi(accs, in_refs, out_refs):
        out_refs[0][...] = accs[0].astype(BF16)

    return _mm(
        name, (N_CHIPS * nbs, K // tm), (a, b), [pl.BlockSpec((S, tm), lambda j, i: (0, i)), b_spec],
        _dot(TN), jax.ShapeDtypeStruct((N_CHIPS, K, Ns), BF16),
        pl.BlockSpec((None, tm, tn), lambda j, i: (j // nbs, i, j % nbs)), epi, dep=dep)


def _rms_fwd(name, x, gain, dep=None):
    S, D = x.shape
    tm = _tile(S, 256, 8)
    extra = () if dep is None else (dep,)

    def body(x_ref, g_ref, *rest):
        h_ref = rest[-1]
        xv = x_ref[...]
        r = lax.rsqrt(jnp.mean(xv * xv, axis=-1, keepdims=True) + RMS_EPS)
        h_ref[...] = (xv * r * g_ref[...]).astype(BF16)

    return pl.pallas_call(
        body, name=name, grid=(S // tm,),
        in_specs=[pl.BlockSpec((tm, D), lambda i: (i, 0)), pl.BlockSpec((1, D), lambda i: (0, 0))]
        + [pl.BlockSpec(memory_space=pl.ANY) for d in extra],
        out_specs=pl.BlockSpec((tm, D), lambda i: (i, 0)), out_shape=jax.ShapeDtypeStruct((S, D), BF16),
        compiler_params=_cparams(("parallel",)),
    )(x, gain, *extra)


def _rms_bwd(name, x, gain, dh, dres):
    S, D = x.shape
    tm = _tile(S, 256, 8)

    def body(x_ref, g_ref, dh_ref, dres_ref, dx_ref, dxb_ref, dg_ref):
        i = pl.program_id(0)
        xv = x_ref[...]
        r = lax.rsqrt(jnp.mean(xv * xv, axis=-1, keepdims=True) + RMS_EPS)
        xhat = xv * r
        dhv = dh_ref[...]
        dxhat = dhv * g_ref[...]
        dx = dres_ref[...] + r * (dxhat - xhat * jnp.mean(dxhat * xhat, axis=-1, keepdims=True))
        dx_ref[...] = dx
        dxb_ref[...] = dx.astype(BF16)

        @pl.when(i == 0)
        def _():
            dg_ref[...] = jnp.zeros_like(dg_ref)

        dg_ref[...] += jnp.sum(dhv * xhat, axis=0, keepdims=True)

    row = pl.BlockSpec((tm, D), lambda i: (i, 0))
    vec = pl.BlockSpec((1, D), lambda i: (0, 0))
    return pl.pallas_call(
        body, name=name, grid=(S // tm,), in_specs=[row, vec, row, row], out_specs=(row, row, vec),
        out_shape=(jax.ShapeDtypeStruct((S, D), F32), jax.ShapeDtypeStruct((S, D), BF16), jax.ShapeDtypeStruct((1, D), F32)),
        compiler_params=_cparams(("arbitrary",)),
    )(x, gain, dh, dres)


def _shift_down(u, k):
    rows = lax.broadcasted_iota(jnp.int32, u.shape, 0)
    return jnp.where(rows >= k, pltpu.roll(u, k, 0), 0.0)


def _shift_up(u, k):
    n = u.shape[0]
    rows = lax.broadcasted_iota(jnp.int32, u.shape, 0)
    return jnp.where(rows < n - k, pltpu.roll(u, n - k, 0), 0.0)


def _conv_specs(S, cw, conv_width):
    nb = conv_width // cw
    col = lambda off: pl.BlockSpec((S, cw), lambda j, off=off: (0, off * nb + j))
    return nb, col(0), col(1), col(2)


def _conv_fwd(name, proj, convw3, conv_width):
    S = proj.shape[0]
    cw = convw3.shape[2]
    nb, xc_s, bg_s, cg_s = _conv_specs(S, cw, conv_width)

    def body(xc_ref, bg_ref, cg_ref, w_ref, o_ref):
        u = cg_ref[...].astype(F32) * xc_ref[...].astype(F32)
        w = w_ref[...]
        cv = w[2:3, :] * u + w[1:2, :] * _shift_down(u, 1) + w[0:1, :] * _shift_down(u, 2)
        o_ref[...] = (bg_ref[...].astype(F32) * cv).astype(BF16)

    return pl.pallas_call(
        body, name=name, grid=(nb,),
        in_specs=[xc_s, bg_s, cg_s, pl.BlockSpec((None, CONV_K, cw), lambda j: (j, 0, 0))],
        out_specs=pl.BlockSpec((S, cw), lambda j: (0, j)), out_shape=jax.ShapeDtypeStruct((S, conv_width), BF16),
        compiler_params=_cparams(("parallel",)),
    )(proj, proj, proj, convw3)


def _conv_bwd(name, proj, convw3, da, conv_width):
    S = proj.shape[0]
    cw = convw3.shape[2]
    nb, xc_s, bg_s, cg_s = _conv_specs(S, cw, conv_width)

    def body(xc_ref, bg_ref, cg_ref, w_ref, da_ref, dxc_ref, dbg_ref, dcg_ref, dw_ref):
        xc, cg = xc_ref[...].astype(F32), cg_ref[...].astype(F32)
        u = cg * xc
        w = w_ref[...]
        u1, u2 = _shift_down(u, 1), _shift_down(u, 2)
        cv = w[2:3, :] * u + w[1:2, :] * u1 + w[0:1, :] * u2
        dav = da_ref[...]
        dbg_ref[...] = (dav * cv).astype(BF16)
        dcv = dav * bg_ref[...].astype(F32)
        du = w[2:3, :] * dcv + w[1:2, :] * _shift_up(dcv, 1) + w[0:1, :] * _shift_up(dcv, 2)
        dxc_ref[...] = (du * cg).astype(BF16)
        dcg_ref[...] = (du * xc).astype(BF16)
        dw_ref[0:1, :] = jnp.sum(dcv * u2, axis=0, keepdims=True)
        dw_ref[1:2, :] = jnp.sum(dcv * u1, axis=0, keepdims=True)
        dw_ref[2:3, :] = jnp.sum(dcv * u, axis=0, keepdims=True)

    wspec = pl.BlockSpec((None, CONV_K, cw), lambda j: (j, 0, 0))
    ospec = pl.BlockSpec((S, cw), lambda j: (0, j))
    act = jax.ShapeDtypeStruct((S, conv_width), BF16)
    return pl.pallas_call(
        body, name=name, grid=(nb,), in_specs=[xc_s, bg_s, cg_s, wspec, ospec],
        out_specs=(ospec, ospec, ospec, wspec),
        out_shape=(act, act, act, jax.ShapeDtypeStruct(convw3.shape, F32)),
        compiler_params=_cparams(("parallel",)),
    )(proj, proj, proj, convw3, da)


def _prep_consts(S, dh, width):
    rot = dh // 4
    half = rot // 2
    inv_freq = 1.0 / (ROPE_THETA ** (jnp.arange(0, rot, 2, dtype=F32) / rot))
    ang = jnp.arange(S, dtype=F32)[:, None] * inv_freq[None, :]
    zeros = jnp.zeros((S, dh - rot), F32)
    cos = jnp.concatenate([jnp.cos(ang), jnp.cos(ang), 1.0 + zeros], axis=1)
    sin_next = jnp.concatenate([-jnp.sin(ang), 0.0 * ang, zeros], axis=1)
    sin_prev = jnp.concatenate([0.0 * ang, jnp.sin(ang), zeros], axis=1)
    reps = min(LANES, width) // dh
    tables = [jnp.tile(t, (1, reps)) for t in (cos, sin_next, sin_prev)]
    mean = np.kron(np.eye(width // dh, dtype=np.float32), np.full((dh, dh), 1.0 / dh, np.float32))
    return (*tables, jnp.asarray(mean, BF16), half)


def _head_mean(p, mean):
    hi = p.astype(BF16)
    lo = (p - hi.astype(F32)).astype(BF16)
    return jnp.dot(hi, mean, preferred_element_type=F32) + jnp.dot(lo, mean, preferred_element_type=F32)


def _prep_specs(S, width, off, tw):
    assert off % width == 0
    tm = _tile(S, 512, 16)
    x = pl.BlockSpec((tm, width), lambda i: (i, off // width))
    row = pl.BlockSpec((tm, width), lambda i: (i, 0))
    tab = pl.BlockSpec((tm, tw), lambda i: (i, 0))
    vec = pl.BlockSpec((1, width), lambda i: (0, 0))
    mat = pl.BlockSpec((width, width), lambda i: (0, 0))
    return tm, x, row, tab, vec, mat


def _qk_prep(name, proj, off, width, gain_row, consts):
    S = proj.shape[0]
    cos, sin_next, sin_prev, mean, half = consts
    tm, x, row, tab, vec, mat = _prep_specs(S, width, off, cos.shape[1])
    reps = width // cos.shape[1]

    def body(x_ref, g_ref, c_ref, sn_ref, sp_ref, m_ref, o_ref):
        xv = x_ref[...].astype(F32)
        y = xv * lax.rsqrt(_head_mean(xv * xv, m_ref[...]) + RMS_EPS) * g_ref[...]
        t = lambda r: jnp.tile(r[...], (1, reps))
        o_ref[...] = (y * t(c_ref) + pltpu.roll(y, width - half, 1) * t(sn_ref) + pltpu.roll(y, half, 1) * t(sp_ref)).astype(BF16)

    return pl.pallas_call(
        body, name=name, grid=(S // tm,), in_specs=[x, vec, tab, tab, tab, mat], out_specs=row,
        out_shape=jax.ShapeDtypeStruct((S, width), BF16), compiler_params=_cparams(("parallel",)),
    )(proj, gain_row, cos, sin_next, sin_prev, mean)


def _qk_prep_bwd(name, proj, off, width, gain_row, consts, dout):
    S = proj.shape[0]
    cos, sin_next, sin_prev, mean, half = consts
    tm, x, row, tab, vec, mat = _prep_specs(S, width, off, cos.shape[1])
    reps = width // cos.shape[1]

    def body(x_ref, g_ref, c_ref, sn_ref, sp_ref, m_ref, do_ref, dx_ref, dg_ref):
        xv = x_ref[...].astype(F32)
        r = lax.rsqrt(_head_mean(xv * xv, m_ref[...]) + RMS_EPS)
        xhat = xv * r
        dov = do_ref[...]
        t = lambda ref: jnp.tile(ref[...], (1, reps))
        dy = dov * t(c_ref) + pltpu.roll(dov * t(sn_ref), half, 1) + pltpu.roll(dov * t(sp_ref), width - half, 1)
        dxhat = dy * g_ref[...]
        dx_ref[...] = (r * (dxhat - xhat * _head_mean(dxhat * xhat, m_ref[...]))).astype(BF16)

        @pl.when(pl.program_id(0) == 0)
        def _():
            dg_ref[...] = jnp.zeros_like(dg_ref)

        dg_ref[...] += jnp.sum(dy * xhat, axis=0, keepdims=True)

    return pl.pallas_call(
        body, name=name, grid=(S // tm,), in_specs=[x, vec, tab, tab, tab, mat, row], out_specs=(row, vec),
        out_shape=(jax.ShapeDtypeStruct((S, width), BF16), jax.ShapeDtypeStruct((1, width), F32)),
        compiler_params=_cparams(("arbitrary",)),
    )(proj, gain_row, cos, sin_next, sin_prev, mean, dout)


def _attn_probs(q, kp, kc, sink_col, n, scale):
    rows = q.shape[0]
    sp = lax.dot_general(q, kp, NT, preferred_element_type=F32) * scale
    sc = lax.dot_general(q, kc, NT, preferred_element_type=F32) * scale
    qi = lax.broadcasted_iota(jnp.int32, (rows, BLOCK), 0) % BLOCK
    kj = lax.broadcasted_iota(jnp.int32, (rows, BLOCK), 1)
    sp = jnp.where((kj > qi) & (n > 0), sp, NEG_INF)
    sc = jnp.where(kj <= qi, sc, NEG_INF)
    m = jnp.maximum(jnp.maximum(jnp.max(sp, axis=-1, keepdims=True), jnp.max(sc, axis=-1, keepdims=True)), sink_col)
    pp, pc, ps = jnp.exp(sp - m), jnp.exp(sc - m), jnp.exp(sink_col - m)
    inv = 1.0 / (jnp.sum(pp, axis=-1, keepdims=True) + jnp.sum(pc, axis=-1, keepdims=True) + ps)
    return pp * inv, pc * inv, ps * inv


def _sink_col(sink_ref, hk, group):
    rows = group * BLOCK
    g = lax.broadcasted_iota(jnp.int32, (rows, 1), 0) // BLOCK
    col = jnp.zeros((rows, 1), F32)
    for i in range(group):
        col = jnp.where(g == i, sink_ref[hk * group + i], col)
    return col


def _attn_specs(group, S, dh):
    heads = pl.BlockSpec((group, S, dh), lambda hk: (hk, 0, 0))
    kv = pl.BlockSpec((None, S, dh), lambda hk: (hk, 0, 0))
    return heads, kv, pl.BlockSpec(memory_space=pltpu.SMEM)


def _block_rows(n):
    cur = pl.ds(pl.multiple_of(n * BLOCK, BLOCK), BLOCK)
    prev = pl.ds(pl.multiple_of(jnp.maximum(n - 1, 0) * BLOCK, BLOCK), BLOCK)
    return cur, prev


def _attn_fwd(name, q, k, v, sinks):
    HQ, S, dh = q.shape
    HKV = k.shape[0]
    group = HQ // HKV
    scale = dh ** -0.5
    heads, kv, smem = _attn_specs(group, S, dh)

    def body(q_ref, k_ref, v_ref, sink_ref, o_ref):
        sink = _sink_col(sink_ref, pl.program_id(0), group)

        def block(n, carry):
            cur, prev = _block_rows(n)
            qv = q_ref[:, cur, :].reshape(group * BLOCK, dh)
            pp, pc, _ = _attn_probs(qv, k_ref[prev, :], k_ref[cur, :], sink, n, scale)
            o = jnp.dot(pp.astype(BF16), v_ref[prev, :], preferred_element_type=F32)
            o = o + jnp.dot(pc.astype(BF16), v_ref[cur, :], preferred_element_type=F32)
            o_ref[:, cur, :] = o.reshape(group, BLOCK, dh).astype(BF16)
            return carry

        lax.fori_loop(0, S // BLOCK, block, 0)

    return pl.pallas_call(
        body, name=name, grid=(HKV,), in_specs=[heads, kv, kv, smem], out_specs=heads,
        out_shape=jax.ShapeDtypeStruct((HQ, S, dh), BF16), compiler_params=_cparams(("parallel",)),
    )(q, k, v, sinks)


def _attn_bwd(name, q, k, v, sinks, do):
    HQ, S, dh = q.shape
    HKV = k.shape[0]
    group = HQ // HKV
    scale = dh ** -0.5
    heads, kv, smem = _attn_specs(group, S, dh)
    sk = pl.BlockSpec((None, group, LANES), lambda hk: (hk, 0, 0))

    def body(q_ref, k_ref, v_ref, sink_ref, do_ref, dq_ref, dk_ref, dv_ref, ds_ref):
        rows = group * BLOCK
        sink = _sink_col(sink_ref, pl.program_id(0), group)
        dk_ref[...] = jnp.zeros_like(dk_ref)
        dv_ref[...] = jnp.zeros_like(dv_ref)
        tdot = lambda a, b: lax.dot_general(a, b, TN, preferred_element_type=F32)

        def block(n, dsink):
            cur, prev = _block_rows(n)
            qv = q_ref[:, cur, :].reshape(rows, dh)
            dov = do_ref[:, cur, :].reshape(rows, dh)
            kp, kc, vp, vc = k_ref[prev, :], k_ref[cur, :], v_ref[prev, :], v_ref[cur, :]
            pp, pc, ps = _attn_probs(qv, kp, kc, sink, n, scale)
            dpp = lax.dot_general(dov, vp, NT, preferred_element_type=F32)
            dpc = lax.dot_general(dov, vc, NT, preferred_element_type=F32)
            delta = jnp.sum(pp * dpp, axis=-1, keepdims=True) + jnp.sum(pc * dpc, axis=-1, keepdims=True)
            dsp = (pp * (dpp - delta) * scale).astype(BF16)
            dsc = (pc * (dpc - delta) * scale).astype(BF16)
            dq = jnp.dot(dsp, kp, preferred_element_type=F32) + jnp.dot(dsc, kc, preferred_element_type=F32)
            dq_ref[:, cur, :] = dq.reshape(group, BLOCK, dh)
            dk_ref[prev, :] += tdot(dsp, qv)
            dv_ref[prev, :] += tdot(pp.astype(BF16), dov)
            dk_ref[cur, :] += tdot(dsc, qv)
            dv_ref[cur, :] += tdot(pc.astype(BF16), dov)
            return dsink - jnp.sum((ps * delta).reshape(group, BLOCK, 1), axis=1)

        pair = lambda m, dsink: block(2 * m + 1, block(2 * m, dsink))
        dsink = lax.fori_loop(0, S // (2 * BLOCK), pair, jnp.zeros((group, 1), F32))
        ds_ref[...] = jnp.broadcast_to(dsink, (group, LANES))

    return pl.pallas_call(
        body, name=name, grid=(HKV,), in_specs=[heads, kv, kv, smem, heads], out_specs=(heads, kv, kv, sk),
        out_shape=(jax.ShapeDtypeStruct((HQ, S, dh), F32), jax.ShapeDtypeStruct((HKV, S, dh), F32),
                   jax.ShapeDtypeStruct((HKV, S, dh), F32), jax.ShapeDtypeStruct((HKV, group, LANES), F32)),
        compiler_params=_cparams(("parallel",)),
    )(q, k, v, sinks, do)


RESIDENT = pl.Buffered(1)


def _gate_blocks(tm, Ns, off):
    assert off % Ns == 0
    return [pl.BlockSpec((tm, Ns), lambda i, k=k: (i, off // Ns + k)) for k in range(N_CHIPS)]


def _mixer_out_fwd(name, aconv, o, woc3, woa3, wo, proj, x1, ga_off, gb_off):
    S, D = x1.shape
    Ns = woc3.shape[2]
    tm = _tile(S, 256, 16)

    def body(a_ref, o_ref, woc_ref, woa_ref, wo_ref, x1_ref, *rest):
        ga_refs, gb_refs = rest[:N_CHIPS], rest[N_CHIPS:2 * N_CHIPS]
        ya_ref, yb_ref, m_ref, x2_ref = rest[2 * N_CHIPS:]
        av, ov = a_ref[...], o_ref[...]
        for s in range(N_CHIPS):
            cols = slice(s * Ns, (s + 1) * Ns)
            ya = jnp.dot(av, woc_ref[s], preferred_element_type=F32)
            yb = jnp.dot(ov, woa_ref[s], preferred_element_type=F32)
            ya_ref[:, cols] = ya.astype(BF16)
            yb_ref[:, cols] = yb.astype(BF16)
            ga, gb = ga_refs[s][...].astype(F32), gb_refs[s][...].astype(F32)
            m_ref[:, cols] = (_sigmoid(ga) * ya + _sigmoid(gb) * yb).astype(BF16)
        x2_ref[...] = x1_ref[...] + jnp.dot(m_ref[...], wo_ref[...], preferred_element_type=F32)

    row = lambda w: pl.BlockSpec((tm, w), lambda i: (i, 0))
    whole3 = lambda a: pl.BlockSpec(a.shape, lambda i: (0, 0, 0), pipeline_mode=RESIDENT)
    act = jax.ShapeDtypeStruct((S, D), BF16)
    return pl.pallas_call(
        body, name=name, grid=(S // tm,),
        in_specs=[row(aconv.shape[1]), row(o.shape[1]), whole3(woc3), whole3(woa3),
                  pl.BlockSpec(wo.shape, lambda i: (0, 0), pipeline_mode=RESIDENT), row(D)]
        + _gate_blocks(tm, Ns, ga_off) + _gate_blocks(tm, Ns, gb_off),
        out_specs=(row(D), row(D), row(D), row(D)), out_shape=(act, act, act, jax.ShapeDtypeStruct((S, D), F32)),
        compiler_params=_cparams(("parallel",)),
    )(aconv, o, woc3, woa3, wo, x1, *([proj] * (2 * N_CHIPS)))


def _mixer_out_bwd(name, dx2_b, wo, ya, yb, proj, woc3, woa3, ga_off, gb_off, dep):
    S, D = dx2_b.shape
    K, Ns = woc3.shape[1], woc3.shape[2]
    tm = _tile(S, 256, 16)

    def body(dx_ref, wo_ref, ya_ref, yb_ref, woc_ref, woa_ref, *rest):
        ga_refs, gb_refs = rest[:N_CHIPS], rest[N_CHIPS:2 * N_CHIPS]
        dga_ref, dgb_ref, dya_ref, dyb_ref, da_ref, do_ref = rest[-6:]
        dm = lax.dot_general(dx_ref[...], wo_ref[...], NT, preferred_element_type=F32)
        da = do = None
        for s in range(N_CHIPS):
            cols = slice(s * Ns, (s + 1) * Ns)
            dms = dm[:, cols]
            sa, sb = _sigmoid(ga_refs[s][...].astype(F32)), _sigmoid(gb_refs[s][...].astype(F32))
            dga_ref[:, cols] = (dms * ya_ref[:, cols].astype(F32) * sa * (1.0 - sa)).astype(BF16)
            dgb_ref[:, cols] = (dms * yb_ref[:, cols].astype(F32) * sb * (1.0 - sb)).astype(BF16)
            dya, dyb = (dms * sa).astype(BF16), (dms * sb).astype(BF16)
            dya_ref[:, cols] = dya
            dyb_ref[:, cols] = dyb
            pa = lax.dot_general(dya, woc_ref[s], NT, preferred_element_type=F32)
            pb = lax.dot_general(dyb, woa_ref[s], NT, preferred_element_type=F32)
            da, do = (pa, pb) if da is None else (da + pa, do + pb)
        da_ref[...] = da
        do_ref[...] = do.astype(BF16)

    row = lambda w: pl.BlockSpec((tm, w), lambda i: (i, 0))
    whole3 = lambda a: pl.BlockSpec(a.shape, lambda i: (0, 0, 0), pipeline_mode=RESIDENT)
    act = jax.ShapeDtypeStruct((S, D), BF16)
    return pl.pallas_call(
        body, name=name, grid=(S // tm,),
        in_specs=[row(D), pl.BlockSpec(wo.shape, lambda i: (0, 0), pipeline_mode=RESIDENT), row(D), row(D), whole3(woc3), whole3(woa3)]
        + _gate_blocks(tm, Ns, ga_off) + _gate_blocks(tm, Ns, gb_off) + [pl.BlockSpec(dep.shape, lambda i: (0, 0))],
        out_specs=(row(D), row(D), row(D), row(D), row(K), row(K)),
        out_shape=(act, act, act, act, jax.ShapeDtypeStruct((S, K), F32), jax.ShapeDtypeStruct((S, K), BF16)),
        compiler_params=_cparams(("parallel",)),
    )(dx2_b, wo, ya, yb, woc3, woa3, *([proj] * (2 * N_CHIPS)), dep)


ANY = pl.BlockSpec(memory_space=pl.ANY)


def _row_tile(rows, cols, n_arrays):
    want = max(16, (VMEM_LIMIT_V7X // 2) // (2 * n_arrays * cols * 4))
    return _tile(rows, want, 16)


def _cast_to_slot(name, w, dtype, p_arr, dep=None):
    R, C = w.shape
    tr = _row_tile(R, C, 2)
    extra = () if dep is None else (dep,)

    def body(p_ref, w_ref, *rest):
        rest[-1][...] = w_ref[...].astype(dtype)

    return pl.pallas_call(
        body, name=name,
        grid_spec=pltpu.PrefetchScalarGridSpec(
            num_scalar_prefetch=1, grid=(R // tr,),
            in_specs=[pl.BlockSpec((tr, C), lambda i, p_ref: (i, 0))] + [pl.BlockSpec(d.shape, lambda i, p_ref: (0, 0)) for d in extra],
            out_specs=pl.BlockSpec((None, tr, C), lambda i, p_ref: (p_ref[0], i, 0))),
        out_shape=jax.ShapeDtypeStruct((N_CHIPS, R, C), dtype), compiler_params=_cparams(("parallel",)),
    )(p_arr, w, *extra)


def _add_half(name, g3, r3, c_arr):
    n, h, C = r3.shape
    tr = _row_tile(h, C, 3)
    nb = h // tr

    def body(c_ref, g_ref, r_ref, o_ref):
        o_ref[...] = (g_ref[...].astype(F32) + r_ref[...].astype(F32)).astype(BF16)

    blk = pl.BlockSpec((None, tr, C), lambda s, i, c_ref: (s, i, 0))
    return pl.pallas_call(
        body, name=name,
        grid_spec=pltpu.PrefetchScalarGridSpec(
            num_scalar_prefetch=1, grid=(n, nb),
            in_specs=[pl.BlockSpec((None, tr, C), lambda s, i, c_ref: (s, c_ref[0] * nb + i, 0)), blk], out_specs=blk),
        out_shape=jax.ShapeDtypeStruct(r3.shape, BF16), compiler_params=_cparams(("parallel", "parallel")),
    )(c_arr, g3, r3)


def _add_chips(name, t3, r3, cp_arr):
    n, h, C = r3.shape
    tr = _row_tile(h, C, 6)
    nb = h // tr

    def body(cp_ref, t_ref, r0_ref, r1_ref, r2_ref, r3_ref, o_ref):
        p = cp_ref[1]
        total = None
        for a, r_ref in enumerate((r0_ref, r1_ref, r2_ref, r3_ref)):
            part = jnp.where(p == a, t_ref[...], r_ref[...]).astype(F32)
            total = part if total is None else total + part
        o_ref[...] = total

    def part(a):
        return pl.BlockSpec((None, tr, C), lambda i, cp_ref: (jnp.where(cp_ref[1] == a, (a + 1) % N_CHIPS, a), i, 0))

    return pl.pallas_call(
        body, name=name,
        grid_spec=pltpu.PrefetchScalarGridSpec(
            num_scalar_prefetch=1, grid=(nb,),
            in_specs=[pl.BlockSpec((None, tr, C), lambda i, cp_ref: (cp_ref[1], i, 0)), part(0), part(1), part(2), part(3)],
            out_specs=pl.BlockSpec((tr, C), lambda i, cp_ref: (cp_ref[0] * nb + i, 0))),
        out_shape=jax.ShapeDtypeStruct((2 * h, C), F32), compiler_params=_cparams(("parallel",)),
    )(cp_arr, t3, r3, r3, r3, r3)


def _adamw(name, w, g, m, v, deps=()):
    R, C = w.shape
    extra = tuple(deps)
    tr = _row_tile(R, C, 8)
    c1 = 1.0 - ADAM_B1 ** ADAM_STEP
    c2 = 1.0 - ADAM_B2 ** ADAM_STEP

    def body(w_ref, g_ref, m_ref, v_ref, *rest):
        go_ref, d_ref, nm_ref, nv_ref = rest[-4:]
        gv = g_ref[...]
        go_ref[...] = gv
        nm = ADAM_B1 * m_ref[...] + (1.0 - ADAM_B1) * gv
        nv = ADAM_B2 * v_ref[...] + (1.0 - ADAM_B2) * (gv * gv)
        d_ref[...] = -ADAM_LR * ((nm / c1) / (jnp.sqrt(nv / c2) + ADAM_EPS) + ADAM_WD * w_ref[...])
        nm_ref[...] = nm
        nv_ref[...] = nv

    blk = pl.BlockSpec((tr, C), lambda i: (i, 0))
    o = jax.ShapeDtypeStruct((R, C), F32)
    return pl.pallas_call(
        body, name=name, grid=(R // tr,), in_specs=[blk, blk, blk, blk] + [ANY] * len(extra), out_specs=(blk, blk, blk, blk),
        out_shape=(o, o, o, o), compiler_params=_cparams(("parallel",)),
    )(w, g, m, v, *extra)


def _place():
    x, y, c = lax.axis_index("x"), lax.axis_index("y"), lax.axis_index("c")
    chips = [(1 - x, y), (x, 1 - y), (1 - x, 1 - y)]
    return x, y, c, 2 * x + y, chips


HBM = pl.BlockSpec(memory_space=pltpu.HBM)
SEM = pl.BlockSpec(memory_space=pltpu.SEMAPHORE)
TOKEN = jax.ShapeDtypeStruct((8, LANES), F32)
DATAFLOW = pltpu.SideEffectType.DATAFLOW_SIDE_EFFECTING


def _hbm(a):
    return pltpu.with_memory_space_constraint(a, pltpu.HBM)


def _gather_blocks(bufs, i, c, p, chips):
    if bufs[i].shape[1] % 16:
        return bufs[i].at[p], [bufs[i].at[2 * cx + cy] for cx, cy in chips]
    h = bufs[i].shape[1] // 2
    rows = pl.ds(pl.multiple_of(c * h, 16), h)
    return bufs[i].at[p, rows], [bufs[i].at[2 * cx + cy, rows] for cx, cy in chips]


def _gather_start(name, groups, dep):
    slots = [s for g in groups for s in g]
    n, ng = len(slots), len(groups)

    def body(*refs):
        bufs, sems, token = refs[:n], refs[n + 1:n + 1 + 2 * ng], refs[-1]
        x, y, c, p, chips = _place()
        i = 0
        for gi, g in enumerate(groups):
            send, recv = sems[2 * gi], sems[2 * gi + 1]
            for k in range(len(g)):
                mine, _ = _gather_blocks(bufs, i, c, p, chips)
                for j, chip in enumerate(chips):
                    pltpu.make_async_remote_copy(src_ref=mine, dst_ref=mine, send_sem=send.at[3 * k + j], recv_sem=recv.at[3 * k + j],
                                                 device_id=(*chip, c), device_id_type=MESH).start()
                i += 1
        token[...] = jnp.zeros_like(token)

    sem_shapes = [pltpu.SemaphoreType.DMA((3 * len(g),)) for g in groups for _ in range(2)]
    out = pl.pallas_call(
        body, name=name, in_specs=[HBM] * n + [ANY],
        out_specs=(*([SEM] * (2 * ng)), *([HBM] * n), pl.BlockSpec(memory_space=pltpu.VMEM)),
        out_shape=(*sem_shapes, *[pltpu.HBM(s.shape, s.dtype) for s in slots], TOKEN),
        input_output_aliases={i: 2 * ng + i for i in range(n)},
        compiler_params=pltpu.CompilerParams(has_side_effects=DATAFLOW),
    )(*[_hbm(s) for s in slots], dep)
    started, i = [], 2 * ng
    for gi, g in enumerate(groups):
        started.append((out[2 * gi], out[2 * gi + 1], list(out[i:i + len(g)])))
        i += len(g)
    return started, out[-1]


def _gather_wait(name, send, recv, slots, after):
    n = len(slots)

    def body(*refs):
        bufs, send, recv = refs[:n], refs[n], refs[n + 1]
        x, y, c, p, chips = _place()
        for i in range(n):
            mine, landed = _gather_blocks(bufs, i, c, p, chips)
            for j, chip in enumerate(chips):
                cp = pltpu.make_async_remote_copy(src_ref=mine, dst_ref=landed[j], send_sem=send.at[3 * i + j],
                                                  recv_sem=recv.at[3 * i + j], device_id=(*chip, c), device_id_type=MESH)
                cp.wait_send()
                cp.wait_recv()

    return list(pl.pallas_call(
        body, name=name, in_specs=[HBM] * n + [SEM, SEM, ANY], out_specs=tuple([HBM] * n),
        out_shape=tuple(pltpu.HBM(s.shape, s.dtype) for s in slots),
        input_output_aliases={i: i for i in range(n)},
        compiler_params=pltpu.CompilerParams(has_side_effects=DATAFLOW),
    )(*slots, send, recv, after))


def _gather_forward(name, slots):
    idx = [i for i, s in enumerate(slots) if s.shape[1] % 16 == 0]
    n = len(slots)

    def body(*refs):
        bufs = refs[n:2 * n]
        send, recv = refs[2 * n:]
        x, y, c, p, chips = _place()

        def rdma(k, ref):
            return pltpu.make_async_remote_copy(src_ref=ref, dst_ref=ref, send_sem=send.at[k], recv_sem=recv.at[k],
                                                device_id=(x, y, 1 - c), device_id_type=MESH)

        cps = []
        for k, i in enumerate(idx):
            for j, ref in enumerate(_gather_blocks(bufs, i, c, p, chips)[1]):
                cps.append(rdma(3 * k + j, ref))
                cps[-1].start()
        for k, i in enumerate(idx):
            for j, ref in enumerate(_gather_blocks(bufs, i, 1 - c, p, chips)[1]):
                rdma(3 * k + j, ref).wait_recv()
        for cp in cps:
            cp.wait_send()

    return list(pl.pallas_call(
        body, name=name, in_specs=[ANY] * n, out_specs=tuple([ANY] * n),
        out_shape=tuple(jax.ShapeDtypeStruct(s.shape, s.dtype) for s in slots),
        scratch_shapes=[pltpu.SemaphoreType.DMA((3 * len(idx),)), pltpu.SemaphoreType.DMA((3 * len(idx),))],
        input_output_aliases={i: i for i in range(n)},
        compiler_params=pltpu.CompilerParams(has_side_effects=True),
    )(*slots))


def _swap_copy(grads, lands, send, recv, i, x, y, c):
    h = grads[i].shape[1] // 2
    other = pl.ds(pl.multiple_of((1 - c) * h, 16), h)
    return pltpu.make_async_remote_copy(src_ref=grads[i].at[:, other, :], dst_ref=lands[i], send_sem=send.at[i],
                                        recv_sem=recv.at[i], device_id=(x, y, 1 - c), device_id_type=MESH)


def _swap_wait(name, send, recv, grads, lands, after):
    n = len(grads)

    def body(*refs):
        ins, lands, send, recv = refs[:n], refs[n:2 * n], refs[2 * n], refs[2 * n + 1]
        x, y, c, p, chips = _place()
        for i in range(n):
            cp = _swap_copy(ins, lands, send, recv, i, x, y, c)
            cp.wait_send()
            cp.wait_recv()

    shapes = [pltpu.HBM(t.shape, t.dtype) for t in list(grads) + list(lands)]
    out = pl.pallas_call(
        body, name=name, in_specs=[HBM] * (2 * n) + [SEM, SEM, ANY], out_specs=tuple([HBM] * (2 * n)),
        out_shape=tuple(shapes), input_output_aliases={i: i for i in range(2 * n)},
        compiler_params=pltpu.CompilerParams(has_side_effects=DATAFLOW),
    )(*grads, *lands, send, recv, after)
    return list(out[:n]), list(out[n:])


def _reduce_starts(name, grads, parts):
    ng, npt = len(grads), len(parts)
    halves = [(g.shape[0], g.shape[1] // 2, g.shape[2]) for g in grads]
    arrays = list(grads) + [lax.empty(s, g.dtype) for s, g in zip(halves, grads)] + list(parts) + [lax.empty(t.shape, t.dtype) for t in parts]
    na = len(arrays)
    sems = ([pltpu.SemaphoreType.DMA((ng,))] * 2 if ng else []) + ([pltpu.SemaphoreType.DMA((3 * npt,))] * 2 if npt else [])
    ns = len(sems)

    def body(*refs):
        ins, sem, token = refs[:na], list(refs[na:na + ns]), refs[-1]
        x, y, c, p, chips = _place()
        if ng:
            for i in range(ng):
                _swap_copy(ins[:ng], ins[ng:2 * ng], sem[0], sem[1], i, x, y, c).start()
        if npt:
            src, land, send, recv = ins[2 * ng:2 * ng + npt], ins[2 * ng + npt:], sem[-2], sem[-1]
            for i in range(npt):
                for j, (cx, cy) in enumerate(chips):
                    pltpu.make_async_remote_copy(src_ref=src[i].at[2 * cx + cy], dst_ref=land[i].at[p], send_sem=send.at[3 * i + j],
                                                 recv_sem=recv.at[3 * i + j], device_id=(cx, cy, c), device_id_type=MESH).start()
        token[...] = jnp.zeros_like(token)

    out = pl.pallas_call(
        body, name=name, in_specs=[HBM] * na,
        out_specs=(*([SEM] * ns), *([HBM] * na), pl.BlockSpec(memory_space=pltpu.VMEM)),
        out_shape=(*sems, *[pltpu.HBM(a.shape, a.dtype) for a in arrays], TOKEN),
        input_output_aliases={i: ns + i for i in range(na)},
        compiler_params=pltpu.CompilerParams(has_side_effects=DATAFLOW),
    )(*[_hbm(a) for a in arrays])
    bufs = list(out[ns:ns + na])
    swap = (out[0], out[1], bufs[:ng], bufs[ng:2 * ng]) if ng else None
    exch = (out[ns - 2], out[ns - 1], bufs[2 * ng:2 * ng + npt], bufs[2 * ng + npt:]) if npt else None
    return swap, exch, out[-1]


def _exchange_wait(name, send, recv, parts, lands, after):
    n = len(parts)

    def body(*refs):
        ins, lands, send, recv = refs[:n], refs[n:2 * n], refs[2 * n], refs[2 * n + 1]
        x, y, c, p, chips = _place()
        for i in range(n):
            for j, (cx, cy) in enumerate(chips):
                q = 2 * cx + cy
                cp = pltpu.make_async_remote_copy(src_ref=ins[i].at[q], dst_ref=lands[i].at[q], send_sem=send.at[3 * i + j],
                                                  recv_sem=recv.at[3 * i + j], device_id=(cx, cy, c), device_id_type=MESH)
                cp.wait_send()
                cp.wait_recv()

    shapes = [pltpu.HBM(t.shape, t.dtype) for t in parts]
    out = pl.pallas_call(
        body, name=name, in_specs=[HBM] * (2 * n) + [SEM, SEM, ANY], out_specs=tuple([HBM] * (2 * n)),
        out_shape=(*shapes, *shapes), input_output_aliases={i: i for i in range(2 * n)},
        compiler_params=pltpu.CompilerParams(has_side_effects=DATAFLOW),
    )(*parts, *lands, send, recv, after)
    return list(out[:n]), list(out[n:])


def _join_copy(buf, send_sem, recv_sem, which, x, y, c):
    h = buf.shape[0] // 2
    rows = buf.at[pl.ds(pl.multiple_of(which * h, 8), h)]
    return pltpu.make_async_remote_copy(src_ref=rows, dst_ref=rows, send_sem=send_sem, recv_sem=recv_sem,
                                        device_id=(x, y, 1 - c), device_id_type=MESH)


def _join_start(name, groups):
    bufs = [b for g in groups for b in g]
    n, ng = len(bufs), len(groups)

    def body(*refs):
        ins, sems, token = refs[:n], refs[n:n + 2 * ng], refs[-1]
        x, y, c, p, chips = _place()
        i = 0
        for gi, g in enumerate(groups):
            for k in range(len(g)):
                _join_copy(ins[i], sems[2 * gi].at[k], sems[2 * gi + 1].at[k], c, x, y, c).start()
                i += 1
        token[...] = jnp.zeros_like(token)

    sem_shapes = [pltpu.SemaphoreType.DMA((len(g),)) for g in groups for _ in range(2)]
    out = pl.pallas_call(
        body, name=name, in_specs=[HBM] * n,
        out_specs=(*([SEM] * (2 * ng)), *([HBM] * n), pl.BlockSpec(memory_space=pltpu.VMEM)),
        out_shape=(*sem_shapes, *[pltpu.HBM(t.shape, t.dtype) for t in bufs], TOKEN),
        input_output_aliases={i: 2 * ng + i for i in range(n)},
        compiler_params=pltpu.CompilerParams(has_side_effects=DATAFLOW),
    )(*[_hbm(t) for t in bufs])
    started, i = [], 2 * ng
    for gi, g in enumerate(groups):
        started.append((out[2 * gi], out[2 * gi + 1], list(out[i:i + len(g)])))
        i += len(g)
    return started, out[-1]


def _join_wait(name, send, recv, bufs, after):
    n = len(bufs)

    def body(*refs):
        ins, send, recv = refs[:n], refs[n], refs[n + 1]
        x, y, c, p, chips = _place()
        for i in range(n):
            _join_copy(ins[i], send.at[i], recv.at[i], c, x, y, c).wait_send()
            _join_copy(ins[i], send.at[i], recv.at[i], 1 - c, x, y, c).wait_recv()

    return list(pl.pallas_call(
        body, name=name, in_specs=[HBM] * n + [SEM, SEM, ANY], out_specs=tuple([HBM] * n),
        out_shape=tuple(pltpu.HBM(t.shape, t.dtype) for t in bufs), input_output_aliases={i: i for i in range(n)},
        compiler_params=pltpu.CompilerParams(has_side_effects=DATAFLOW),
    )(*bufs, send, recv, after))


def _allreduce_small(name, pack, dep):
    R, W = pack.shape

    def body(in_ref, dep_ref, out_ref, slots, send, recv):
        x, y, c = lax.axis_index("x"), lax.axis_index("y"), lax.axis_index("c")
        me = 4 * x + 2 * y + c
        slots[0] = in_ref[...]
        cps = []
        for k in range(1, N_DEV):
            peer = (x ^ (k >> 2), y ^ ((k >> 1) & 1), c ^ (k & 1))
            cp = pltpu.make_async_remote_copy(src_ref=in_ref, dst_ref=slots.at[k], send_sem=send.at[k - 1],
                                              recv_sem=recv.at[k - 1], device_id=peer, device_id_type=MESH)
            cp.start()
            cps.append(cp)
        for cp in cps:
            cp.wait()
        total = slots[me]
        for a in range(1, N_DEV):
            total = total + slots[jnp.bitwise_xor(a, me)]
        out_ref[...] = total

    vmem = pl.BlockSpec(memory_space=pltpu.VMEM)
    return pl.pallas_call(
        body, name=name, in_specs=[vmem, ANY], out_specs=vmem, out_shape=jax.ShapeDtypeStruct((R, W), F32),
        scratch_shapes=[pltpu.VMEM((N_DEV, R, W), F32), pltpu.SemaphoreType.DMA((N_DEV - 1,)), pltpu.SemaphoreType.DMA((N_DEV - 1,))],
        compiler_params=pltpu.CompilerParams(has_side_effects=True),
    )(pack, dep)


def _heads(a, n_heads):
    S = a.shape[0]
    return a.reshape(S, n_heads, a.shape[1] // n_heads).transpose(1, 0, 2)


def _unheads(a):
    H, S, dh = a.shape
    return a.transpose(1, 0, 2).reshape(S, H * dh)


def _ffn_bwd(tag, xin, gain, wgu3, wd, saved, dxout, dxo_b, reduce_start, dep, flush=None):
    h, gu, act = saved
    D = xin.shape[1]
    tok = reduce_start({f"w_down{tag}": _mm_tn(f"dw_down_{tag}", act, dxo_b, 0.5, dep=dep).reshape(N_CHIPS, -1, D)})
    dgu = _ffn_down_bwd(f"ffn_down_bwd_{tag}", dxo_b, wd, gu, 0.5, dep=tok)
    tok = reduce_start({f"w_gu{tag}": _mm_tn_cols(f"dw_gu_{tag}", h, dgu, wgu3.shape[2], b_is_gu=True)})
    if flush is not None:
        tok = flush(tok)
    dh = _mm_nt_cols(f"ffn_up_bwd_{tag}", dgu, wgu3, a_is_gu=True, dep=tok)
    dxin, dxin_b, dgain = _rms_bwd(f"rms_bwd_{tag}", xin, gain, dh, dxout)
    return dxin, dxin_b, dgain, tok


def kernel(x, g_ffn1, w_gu1, w_down1, g_mix, w_in, conv_w, q_norm_g, k_norm_g, sinks, w_out_conv, w_out_attn, w_o, g_ffn2, w_gu2, w_down2, loss_target, m_g_ffn1, m_w_gu1, m_w_down1, m_g_mix, m_w_in, m_conv_w, m_q_norm_g, m_k_norm_g, m_sinks, m_w_out_conv, m_w_out_attn, m_w_o, m_g_ffn2, m_w_gu2, m_w_down2, v_g_ffn1, v_w_gu1, v_w_down1, v_g_mix, v_w_in, v_conv_w, v_q_norm_g, v_k_norm_g, v_sinks, v_w_out_conv, v_w_out_attn, v_w_o, v_g_ffn2, v_w_gu2, v_w_down2):
    S, D = x.shape[1], x.shape[2]
    dh = q_norm_g.shape[1]
    HQ = sinks.shape[1]
    HKV = HQ // 4
    AW, KVW, CW = HQ * dh, HKV * dh, D // 2
    off_q, off_k, off_v = 3 * CW, 3 * CW + AW, 3 * CW + AW + KVW
    off_ga, off_gb = off_v + KVW, off_v + KVW + D
    x0, target = x[0], loss_target[0]
    cx, cy, cc = lax.axis_index("x"), lax.axis_index("y"), lax.axis_index("c")
    chip = 2 * cx + cy
    p_arr = jnp.reshape(chip, (1,)).astype(jnp.int32)
    c_arr = jnp.reshape(cc, (1,)).astype(jnp.int32)
    cp_arr = jnp.stack([cc, chip]).astype(jnp.int32)
    wts = dict(g_ffn1=g_ffn1, w_gu1=w_gu1, w_down1=w_down1, g_mix=g_mix, w_in=w_in, conv_w=conv_w, q_norm_g=q_norm_g,
               k_norm_g=k_norm_g, sinks=sinks, w_out_conv=w_out_conv, w_out_attn=w_out_attn, w_o=w_o, g_ffn2=g_ffn2,
               w_gu2=w_gu2, w_down2=w_down2)
    ms = dict(g_ffn1=m_g_ffn1, w_gu1=m_w_gu1, w_down1=m_w_down1, g_mix=m_g_mix, w_in=m_w_in, conv_w=m_conv_w,
              q_norm_g=m_q_norm_g, k_norm_g=m_k_norm_g, sinks=m_sinks, w_out_conv=m_w_out_conv, w_out_attn=m_w_out_attn,
              w_o=m_w_o, g_ffn2=m_g_ffn2, w_gu2=m_w_gu2, w_down2=m_w_down2)
    vs = dict(g_ffn1=v_g_ffn1, w_gu1=v_w_gu1, w_down1=v_w_down1, g_mix=v_g_mix, w_in=v_w_in, conv_w=v_conv_w,
              q_norm_g=v_q_norm_g, k_norm_g=v_k_norm_g, sinks=v_sinks, w_out_conv=v_w_out_conv, w_out_attn=v_w_out_attn,
              w_o=v_w_o, g_ffn2=v_g_ffn2, w_gu2=v_w_gu2, w_down2=v_w_down2)
    order = list(wts)
    small_names = [k for k in order if not k.startswith("w_")]
    grad, delta, new_m, new_v = {}, {}, {}, {}

    def cast(keys, dep=None):
        return [_cast_to_slot(f"cast_{k}", wts[k][0], F32 if k == "conv_w" else BF16, p_arr, dep) for k in keys]

    def gather_finish(tag, started, after):
        send, recv, slots = started
        return _gather_forward(f"gather_forward_{tag}", _gather_wait(f"gather_wait_{tag}", send, recv, slots, after))

    swapping, pending = [], []

    def reduce_start(full, after=None):
        keys = [] if full is None else list(full)
        pkeys, parts = [], []
        if swapping:
            pkeys, send, recv, gs, lands = swapping.pop(0)
            gs, sib = _swap_wait(f"swap_wait_{pkeys[0]}", send, recv, gs, lands, after if full is None else full[keys[0]])
            parts = [_add_half(f"add_half_{k}", g, r, c_arr) for k, g, r in zip(pkeys, gs, sib)]
        swap, exch, tok = _reduce_starts(f"reduce_starts_{keys[0] if keys else 'last'}", [full[k] for k in keys], parts)
        if exch:
            pending.append((pkeys, *exch))
        if swap:
            swapping.append((keys, *swap))
        return tok

    def reduce_finish(entries, after):
        ready = []
        for keys, send, recv, parts, lands in entries:
            parts, lands = _exchange_wait(f"exchange_wait_{keys[0]}", send, recv, parts, lands, after)
            ready.append((keys, [_add_chips(f"add_chips_{k}", t, r, cp_arr) for k, t, r in zip(keys, parts, lands)]))
        started, last = _join_start(f"join_start_{ready[0][0][0]}", [halves for _, halves in ready])
        for (keys, _), (send, recv, halves) in zip(ready, started):
            for k, g2 in zip(keys, _join_wait(f"join_wait_{keys[0]}", send, recv, halves, last)):
                g2, d, nm, nv = _adamw(f"adamw_{k}", wts[k][0], g2, ms[k][0], vs[k][0], (last,))
                grad[k], delta[k], new_m[k], new_v[k] = g2[None], d[None], nm[None], nv[None]
                last = nv
        return last

    (st_gu1, st_d1), tok = _gather_start("gather_start_1", [cast(["w_gu1"]), cast(["w_down1"])], x0)
    later = ["w_in", "conv_w", "w_out_conv", "w_out_attn", "w_o", "w_gu2", "w_down2"]
    slot = dict(zip(later, cast(later, tok)))
    h1 = _rms_fwd("rms_fwd_1", x0, g_ffn1, slot["w_down2"])
    wgu1, = gather_finish("gu1", st_gu1, h1)
    (st_in, st_out, st_gu2, st_d2), tok = _gather_start(
        "gather_start_2", [[slot["w_in"], slot["conv_w"]], [slot["w_out_conv"], slot["w_out_attn"], slot["w_o"]],
                           [slot["w_gu2"]], [slot["w_down2"]]], wgu1)
    q_consts, k_consts = _prep_consts(S, dh, AW), _prep_consts(S, dh, KVW)
    qg_row, kg_row = jnp.tile(q_norm_g, (1, HQ)), jnp.tile(k_norm_g, (1, HKV))
    sink_vec = sinks[0]

    gu1, act1 = _ffn_up("ffn_up_1", h1, wgu1, tok)
    wd1 = gather_finish("d1", st_d1, act1)[0].reshape(-1, D)
    x1 = _mm_res("ffn_down_1", act1, wd1, x0, 0.5)
    win3, convw3 = gather_finish("in", st_in, x1)
    h2 = _rms_fwd("rms_fwd_mix", x1, g_mix)
    proj = _mm_cols("in_proj", h2, win3, BF16)
    aconv = _conv_fwd("conv_fwd", proj, convw3, CW)
    woc3, woa3, wo = gather_finish("out", st_out, aconv)
    wo = wo.reshape(-1, D)
    vh = _heads(proj[:, off_v:off_v + KVW], HKV)
    qn = _heads(_qk_prep("q_prep", proj, off_q, AW, qg_row, q_consts), HQ)
    kn = _heads(_qk_prep("k_prep", proj, off_k, KVW, kg_row, k_consts), HKV)
    oh = _attn_fwd("attn_fwd", qn, kn, vh, sink_vec)
    o = _unheads(oh)
    ya, yb, merged, x2 = _mixer_out_fwd("mixer_out", aconv, o, woc3, woa3, wo, proj, x1, off_ga, off_gb)
    wgu2, = gather_finish("gu2", st_gu2, x2)
    h3 = _rms_fwd("rms_fwd_2", x2, g_ffn2)
    gu2, act2 = _ffn_up("ffn_up_2", h3, wgu2)
    wd2 = gather_finish("d2", st_d2, act2)[0].reshape(-1, D)
    dy, dy_b, loss_lanes = _mm_res_loss("ffn_down_2_loss", act2, wd2, x2, 0.5, target)
    dx2, dx2_b, dg_ffn2, tok = _ffn_bwd("2", x2, g_ffn2, wgu2, wd2, (h3, gu2, act2), dy, dy_b, reduce_start, None)
    tok = reduce_start(dict(w_o=_mm_tn("dw_o", merged, dx2_b, dep=tok).reshape(N_CHIPS, -1, D)))
    dga, dgb, dya, dyb, daconv, do = _mixer_out_bwd("mixer_out_bwd", dx2_b, wo, ya, yb, proj, woc3, woa3, off_ga, off_gb, tok)
    dwoc = _mm_tn_cols("dw_out_conv", aconv, dya, woc3.shape[2])
    dwoa = _mm_tn_cols("dw_out_attn", o, dyb, woa3.shape[2])
    tok = reduce_start(dict(w_out_conv=dwoc, w_out_attn=dwoa))
    dxc, dbg, dcg, dconvw = _conv_bwd("conv_bwd", proj, convw3, daconv, CW)
    dqn, dkn, dvh, dsink3 = _attn_bwd("attn_bwd", qn, kn, vh, sink_vec, _heads(do, HQ))
    dq_raw, dqg = _qk_prep_bwd("q_prep_bwd", proj, off_q, AW, qg_row, q_consts, _unheads(dqn))
    dk_raw, dkg = _qk_prep_bwd("k_prep_bwd", proj, off_k, KVW, kg_row, k_consts, _unheads(dkn))
    dqg, dkg = dqg.reshape(HQ, dh).sum(axis=0, keepdims=True), dkg.reshape(HKV, dh).sum(axis=0, keepdims=True)
    dproj = jnp.concatenate([dxc, dbg, dcg, dq_raw, dk_raw, _unheads(dvh).astype(BF16), dga, dgb], axis=1)
    dh2 = _mm_nt_cols("in_proj_bwd", dproj, win3, dep=tok)
    tok = reduce_start(dict(w_in=_mm_tn_cols("dw_in", h2, dproj, win3.shape[2])))
    dx1, dx1_b, dg_mix = _rms_bwd("rms_bwd_mix", x1, g_mix, dh2, dx2)
    dx0, _, dg_ffn1, tok = _ffn_bwd("1", x0, g_ffn1, wgu1, wd1, (h1, gu1, act1), dx1, dx1_b, reduce_start, tok, lambda after: reduce_start(None, after))

    def rows8(a):
        a = a.reshape(-1, a.shape[-1])
        return jnp.pad(a, ((0, -a.shape[0] % 8), (0, D - a.shape[1])))

    misc = jnp.concatenate([dqg, dkg, dsink3[:, :, 0].reshape(1, HQ), loss_lanes], axis=1)
    done = reduce_finish(pending[:-2], dx0)
    tot = _allreduce_small("allreduce_small", jnp.concatenate([rows8(a) for a in (dg_ffn1, dg_mix, dg_ffn2, dconvw, misc)], axis=0), done)
    reduce_finish(pending[-2:], tot)

    cw_s = conv_w.shape[2]
    conv_row0, misc_row = 24, 24 + (-(-N_CHIPS * CONV_K // 8)) * 8
    small_g = dict(g_ffn1=tot[0:1], g_mix=tot[8:9], g_ffn2=tot[16:17],
                   conv_w=lax.dynamic_slice(tot, (conv_row0 + CONV_K * chip, 0), (CONV_K, cw_s)),
                   q_norm_g=tot[misc_row:misc_row + 1, 0:dh], k_norm_g=tot[misc_row:misc_row + 1, dh:2 * dh],
                   sinks=tot[misc_row:misc_row + 1, 2 * dh:2 * dh + HQ])
    loss = (0.5 / D) * jnp.sum(tot[misc_row, 2 * dh + HQ:2 * dh + HQ + LANES])

    def small_pack(src):
        return jnp.concatenate([rows8(src[k]) for k in small_names], axis=0)

    _, sd, sm, sv = _adamw("adamw_small", small_pack(wts), small_pack(small_g), small_pack(ms), small_pack(vs))
    for i, k in enumerate(small_names):
        shape = wts[k].shape
        nr, ncol = math.prod(shape[:-1]), shape[-1]
        grad[k] = small_g[k].reshape(shape)
        delta[k], new_m[k], new_v[k] = (a[8 * i:8 * i + nr, 0:ncol].reshape(shape) for a in (sd, sm, sv))
    return (loss, dx0[None], *[grad[k] for k in order], *[delta[k] for k in order],
            *[new_m[k] for k in order], *[new_v[k] for k in order])
```

```python
import math

import numpy as np
import jax
import jax.numpy as jnp
from jax import lax
from jax.experimental import pallas as pl
from jax.experimental.pallas import tpu as pltpu

F32 = jnp.float32
BF16 = jnp.bfloat16
MESH = pl.DeviceIdType.MESH

RMS_EPS = 1e-6
BLOCK = 128
ROPE_THETA = 500000.0
NEG_INF = -1e30
CONV_K = 3
ADAM_LR, ADAM_B1, ADAM_B2, ADAM_EPS, ADAM_WD, ADAM_STEP = 0.001, 0.9, 0.999, 1e-08, 0.01, 10

VMEM_LIMIT_V7X = 56 * 1024 * 1024
LANES = 128
N_CHIPS = 4
N_DEV = 8


def _tile(n, want, align=LANES):
    best = None
    t = align
    while t <= min(n, want):
        if n % t == 0:
            best = t
        t += align
    return best or n


def _cparams(sem):
    return pltpu.CompilerParams(dimension_semantics=sem, vmem_limit_bytes=VMEM_LIMIT_V7X)


def _sigmoid(x):
    return 1.0 / (1.0 + jnp.exp(-x))


NN = (((1,), (0,)), ((), ()))
NT = (((1,), (1,)), ((), ()))
TN = (((0,), (0,)), ((), ()))


def _mm(name, grid, ins, in_specs, compute, out_shape, out_specs, epilogue, dep=None, carried=False):
    if dep is not None:
        ins, in_specs = tuple(ins) + (dep,), list(in_specs) + [pl.BlockSpec(dep.shape, lambda *_: (0, 0))]
    n_in = len(ins)

    def body(*refs):
        epilogue(compute(refs[:n_in]), refs[:n_in], refs[n_in:])

    return pl.pallas_call(
        body, name=name, grid=grid, in_specs=in_specs, out_specs=out_specs, out_shape=out_shape,
        compiler_params=_cparams(("arbitrary" if carried else "parallel", "arbitrary")),
    )(*ins)


def _dot(dims, a=0, b=1):
    return lambda refs: [lax.dot_general(refs[a][...], refs[b][...], dims, preferred_element_type=F32)]


def _ffn_up(name, h, wgu3, dep=None):
    S, D = h.shape
    Ns = wgu3.shape[2]
    F = 2 * Ns
    tm, tn = _tile(S, 512), _tile(Ns, 1408)
    nbs = Ns // tn

    def compute(refs):
        hv = refs[0][...]
        return [jnp.dot(hv, refs[1][...], preferred_element_type=F32), jnp.dot(hv, refs[2][...], preferred_element_type=F32)]

    def epi(accs, in_refs, out_refs):
        g, u = accs
        dgu_ref, a_ref = out_refs
        sg = _sigmoid(g)
        silu = g * sg
        dgu_ref[0] = (u * (sg * (1.0 + g * (1.0 - sg)))).astype(BF16)
        dgu_ref[1] = silu.astype(BF16)
        a_ref[...] = (silu * u).astype(BF16)

    return _mm(
        name, (F // tn, S // tm), (h, wgu3, wgu3),
        [pl.BlockSpec((tm, D), lambda j, i: (i, 0)),
         pl.BlockSpec((None, D, tn), lambda j, i: (j // nbs, 0, j % nbs)),
         pl.BlockSpec((None, D, tn), lambda j, i: (2 + j // nbs, 0, j % nbs))],
        compute, (jax.ShapeDtypeStruct((2, S, F), BF16), jax.ShapeDtypeStruct((S, F), BF16)),
        (pl.BlockSpec((2, tm, tn), lambda j, i: (0, i, j)), pl.BlockSpec((tm, tn), lambda j, i: (i, j))), epi, dep=dep)


def _mm_res(name, a, w, res, scale):
    S, K = a.shape
    N = w.shape[1]
    tm, tn = _tile(S, 512), _tile(N, 512 if K > 2816 else 1024)

    def epi(accs, in_refs, out_refs):
        out_refs[0][...] = in_refs[2][...] + scale * accs[0]

    return _mm(
        name, (N // tn, S // tm), (a, w, res),
        [pl.BlockSpec((tm, K), lambda j, i: (i, 0)), pl.BlockSpec((K, tn), lambda j, i: (0, j)),
         pl.BlockSpec((tm, tn), lambda j, i: (i, j))],
        _dot(NN), jax.ShapeDtypeStruct((S, N), F32), pl.BlockSpec((tm, tn), lambda j, i: (i, j)), epi)


def _mm_res_loss(name, a, w, res, scale, target):
    S, K = a.shape
    N = w.shape[1]
    tm, tn = _tile(S, 512), _tile(N, 512 if K > 2816 else 1024)

    def epi(accs, in_refs, out_refs):
        dy_ref, dyb_ref, l_ref = out_refs
        e = in_refs[2][...] + scale * accs[0] - in_refs[3][...]
        dy_ref[...] = e * (1.0 / N)
        dyb_ref[...] = (e * (1.0 / N)).astype(BF16)
        col = jnp.sum(e * e, axis=0, keepdims=True)
        part = col[:, 0:LANES]
        for k in range(1, tn // LANES):
            part = part + col[:, k * LANES:(k + 1) * LANES]

        @pl.when((pl.program_id(0) == 0) & (pl.program_id(1) == 0))
        def _():
            l_ref[...] = jnp.zeros_like(l_ref)

        l_ref[...] += part

    tile = pl.BlockSpec((tm, tn), lambda j, i: (i, j))
    return _mm(
        name, (N // tn, S // tm), (a, w, res, target),
        [pl.BlockSpec((tm, K), lambda j, i: (i, 0)), pl.BlockSpec((K, tn), lambda j, i: (0, j)), tile, tile],
        _dot(NN), (jax.ShapeDtypeStruct((S, N), F32), jax.ShapeDtypeStruct((S, N), BF16), jax.ShapeDtypeStruct((1, LANES), F32)),
        (tile, tile, pl.BlockSpec((1, LANES), lambda j, i: (0, 0))), epi, carried=True)


def _mm_cols(name, a, w3, out_dtype):
    S, K = a.shape
    Ns = w3.shape[2]
    tm, tn = _tile(S, 512), _tile(Ns, 2304)
    nbs = Ns // tn

    def epi(accs, in_refs, out_refs):
        out_refs[0][...] = accs[0].astype(out_dtype)

    return _mm(
        name, (N_CHIPS * nbs, S // tm), (a, w3),
        [pl.BlockSpec((tm, K), lambda j, i: (i, 0)),
         pl.BlockSpec((None, K, tn), lambda j, i: (j // nbs, 0, j % nbs))],
        _dot(NN), jax.ShapeDtypeStruct((S, N_CHIPS * Ns), out_dtype), pl.BlockSpec((tm, tn), lambda j, i: (i, j)), epi)


def _ffn_down_bwd(name, dy, wd, gu, scale, dep=None):
    S, D = dy.shape
    F = wd.shape[0]
    tm, tn = _tile(S, 512), _tile(F, 1408)

    def epi(accs, in_refs, out_refs):
        da = scale * accs[0]
        out_refs[0][0] = (da * in_refs[2][0].astype(F32)).astype(BF16)
        out_refs[0][1] = (da * in_refs[2][1].astype(F32)).astype(BF16)

    return _mm(
        name, (F // tn, S // tm), (dy, wd, gu),
        [pl.BlockSpec((tm, D), lambda j, i: (i, 0)), pl.BlockSpec((tn, D), lambda j, i: (j, 0)),
         pl.BlockSpec((2, tm, tn), lambda j, i: (0, i, j))],
        _dot(NT), jax.ShapeDtypeStruct((2, S, F), BF16), pl.BlockSpec((2, tm, tn), lambda j, i: (0, i, j)), epi, dep=dep)


def _mm_nt_cols(name, a, w3, a_is_gu=False, dep=None):
    K, Ns = w3.shape[1], w3.shape[2]
    S = a.shape[1] if a_is_gu else a.shape[0]
    tm = _tile(S, 512)
    tn = _tile(K, max(LANES, (12 << 20) // (N_CHIPS * Ns * 2)))
    if a_is_gu:
        a_spec = pl.BlockSpec((2, tm, 2 * Ns), lambda i, j: (0, i, 0))
        part = lambda a_ref, s: a_ref[s // 2, :, (s % 2) * Ns:(s % 2 + 1) * Ns]
    else:
        a_spec = pl.BlockSpec((tm, N_CHIPS * Ns), lambda i, j: (i, 0))
        part = lambda a_ref, s: a_ref[:, s * Ns:(s + 1) * Ns]

    def compute(refs):
        total = None
        for s in range(N_CHIPS):
            prod = lax.dot_general(part(refs[0], s), refs[1][s], NT, preferred_element_type=F32)
            total = prod if total is None else total + prod
        return [total]

    def epi(accs, in_refs, out_refs):
        out_refs[0][...] = accs[0]

    return _mm(
        name, (S // tm, K // tn), (a, w3), [a_spec, pl.BlockSpec((N_CHIPS, tn, Ns), lambda i, j: (0, j, 0))],
        compute, jax.ShapeDtypeStruct((S, K), F32), pl.BlockSpec((tm, tn), lambda i, j: (i, j)), epi, dep=dep)


def _mm_tn(name, a, b, scale=1.0, dep=None):
    S, K = a.shape
    N = b.shape[1]
    tm, tn = _tile(K, 1408), _tile(N, 1024)

    def epi(accs, in_refs, out_refs):
        out_refs[0][...] = (scale * accs[0]).astype(BF16)

    return _mm(
        name, (N // tn, K // tm), (a, b),
        [pl.BlockSpec((S, tm), lambda j, i: (0, i)), pl.BlockSpec((S, tn), lambda j, i: (0, j))],
        _dot(TN), jax.ShapeDtypeStruct((K, N), BF16), pl.BlockSpec((tm, tn), lambda j, i: (i, j)), epi, dep=dep)


def _mm_tn_cols(name, a, b, Ns, b_is_gu=False, dep=None):
    S, K = a.shape
    tm, tn = _tile(K, 1024), _tile(Ns, 2304)
    nbs = Ns // tn
    if b_is_gu:
        b_spec = pl.BlockSpec((None, S, tn), lambda j, i: (j // (2 * nbs), 0, j % (2 * nbs)))
    else:
        b_spec = pl.BlockSpec((S, tn), lambda j, i: (0, j))

    def epi(accs, in_refs, out_refs):
        out_refs[0][...] = accs[0].astype(BF16)

    return _mm(
        name, (N_CHIPS * nbs, K // tm), (a, b), [pl.BlockSpec((S, tm), lambda j, i: (0, i)), b_spec],
        _dot(TN), jax.ShapeDtypeStruct((N_CHIPS, K, Ns), BF16),
        pl.BlockSpec((None, tm, tn), lambda j, i: (j // nbs, i, j % nbs)), epi, dep=dep)


def _rms_fwd(name, x, gain, dep=None):
    S, D = x.shape
    tm = _tile(S, 256, 8)
    extra = () if dep is None else (dep,)

    def body(x_ref, g_ref, *rest):
        h_ref = rest[-1]
        xv = x_ref[...]
        r = lax.rsqrt(jnp.mean(xv * xv, axis=-1, keepdims=True) + RMS_EPS)
        h_ref[...] = (xv * r * g_ref[...]).astype(BF16)

    return pl.pallas_call(
        body, name=name, grid=(S // tm,),
        in_specs=[pl.BlockSpec((tm, D), lambda i: (i, 0)), pl.BlockSpec((1, D), lambda i: (0, 0))]
        + [pl.BlockSpec(memory_space=pl.ANY) for d in extra],
        out_specs=pl.BlockSpec((tm, D), lambda i: (i, 0)), out_shape=jax.ShapeDtypeStruct((S, D), BF16),
        compiler_params=_cparams(("parallel",)),
    )(x, gain, *extra)


def _rms_bwd(name, x, gain, dh, dres):
    S, D = x.shape
    tm = _tile(S, 256, 8)

    def body(x_ref, g_ref, dh_ref, dres_ref, dx_ref, dxb_ref, dg_ref):
        i = pl.program_id(0)
        xv = x_ref[...]
        r = lax.rsqrt(jnp.mean(xv * xv, axis=-1, keepdims=True) + RMS_EPS)
        xhat = xv * r
        dhv = dh_ref[...]
        dxhat = dhv * g_ref[...]
        dx = dres_ref[...] + r * (dxhat - xhat * jnp.mean(dxhat * xhat, axis=-1, keepdims=True))
        dx_ref[...] = dx
        dxb_ref[...] = dx.astype(BF16)

        @pl.when(i == 0)
        def _():
            dg_ref[...] = jnp.zeros_like(dg_ref)

        dg_ref[...] += jnp.sum(dhv * xhat, axis=0, keepdims=True)

    row = pl.BlockSpec((tm, D), lambda i: (i, 0))
    vec = pl.BlockSpec((1, D), lambda i: (0, 0))
    return pl.pallas_call(
        body, name=name, grid=(S // tm,), in_specs=[row, vec, row, row], out_specs=(row, row, vec),
        out_shape=(jax.ShapeDtypeStruct((S, D), F32), jax.ShapeDtypeStruct((S, D), BF16), jax.ShapeDtypeStruct((1, D), F32)),
        compiler_params=_cparams(("arbitrary",)),
    )(x, gain, dh, dres)


def _shift_down(u, k):
    rows = lax.broadcasted_iota(jnp.int32, u.shape, 0)
    return jnp.where(rows >= k, pltpu.roll(u, k, 0), 0.0)


def _shift_up(u, k):
    n = u.shape[0]
    rows = lax.broadcasted_iota(jnp.int32, u.shape, 0)
    return jnp.where(rows < n - k, pltpu.roll(u, n - k, 0), 0.0)


def _conv_specs(S, cw, conv_width):
    nb = conv_width // cw
    col = lambda off: pl.BlockSpec((S, cw), lambda j, off=off: (0, off * nb + j))
    return nb, col(0), col(1), col(2)


def _conv_fwd(name, proj, convw3, conv_width):
    S = proj.shape[0]
    cw = convw3.shape[2]
    nb, xc_s, bg_s, cg_s = _conv_specs(S, cw, conv_width)

    def body(xc_ref, bg_ref, cg_ref, w_ref, o_ref):
        u = cg_ref[...].astype(F32) * xc_ref[...].astype(F32)
        w = w_ref[...]
        cv = w[2:3, :] * u + w[1:2, :] * _shift_down(u, 1) + w[0:1, :] * _shift_down(u, 2)
        o_ref[...] = (bg_ref[...].astype(F32) * cv).astype(BF16)

    return pl.pallas_call(
        body, name=name, grid=(nb,),
        in_specs=[xc_s, bg_s, cg_s, pl.BlockSpec((None, CONV_K, cw), lambda j: (j, 0, 0))],
        out_specs=pl.BlockSpec((S, cw), lambda j: (0, j)), out_shape=jax.ShapeDtypeStruct((S, conv_width), BF16),
        compiler_params=_cparams(("parallel",)),
    )(proj, proj, proj, convw3)


def _conv_bwd(name, proj, convw3, da, conv_width):
    S = proj.shape[0]
    cw = convw3.shape[2]
    nb, xc_s, bg_s, cg_s = _conv_specs(S, cw, conv_width)

    def body(xc_ref, bg_ref, cg_ref, w_ref, da_ref, dxc_ref, dbg_ref, dcg_ref, dw_ref):
        xc, cg = xc_ref[...].astype(F32), cg_ref[...].astype(F32)
        u = cg * xc
        w = w_ref[...]
        u1, u2 = _shift_down(u, 1), _shift_down(u, 2)
        cv = w[2:3, :] * u + w[1:2, :] * u1 + w[0:1, :] * u2
        dav = da_ref[...]
        dbg_ref[...] = (dav * cv).astype(BF16)
        dcv = dav * bg_ref[...].astype(F32)
        du = w[2:3, :] * dcv + w[1:2, :] * _shift_up(dcv, 1) + w[0:1, :] * _shift_up(dcv, 2)
        dxc_ref[...] = (du * cg).astype(BF16)
        dcg_ref[...] = (du * xc).astype(BF16)
        dw_ref[0:1, :] = jnp.sum(dcv * u2, axis=0, keepdims=True)
        dw_ref[1:2, :] = jnp.sum(dcv * u1, axis=0, keepdims=True)
        dw_ref[2:3, :] = jnp.sum(dcv * u, axis=0, keepdims=True)

    wspec = pl.BlockSpec((None, CONV_K, cw), lambda j: (j, 0, 0))
    ospec = pl.BlockSpec((S, cw), lambda j: (0, j))
    act = jax.ShapeDtypeStruct((S, conv_width), BF16)
    return pl.pallas_call(
        body, name=name, grid=(nb,), in_specs=[xc_s, bg_s, cg_s, wspec, ospec],
        out_specs=(ospec, ospec, ospec, wspec),
        out_shape=(act, act, act, jax.ShapeDtypeStruct(convw3.shape, F32)),
        compiler_params=_cparams(("parallel",)),
    )(proj, proj, proj, convw3, da)


def _prep_consts(S, dh, width):
    rot = dh // 4
    half = rot // 2
    inv_freq = 1.0 / (ROPE_THETA ** (jnp.arange(0, rot, 2, dtype=F32) / rot))
    ang = jnp.arange(S, dtype=F32)[:, None] * inv_freq[None, :]
    zeros = jnp.zeros((S, dh - rot), F32)
    cos = jnp.concatenate([jnp.cos(ang), jnp.cos(ang), 1.0 + zeros], axis=1)
    sin_next = jnp.concatenate([-jnp.sin(ang), 0.0 * ang, zeros], axis=1)
    sin_prev = jnp.concatenate([0.0 * ang, jnp.sin(ang), zeros], axis=1)
    reps = min(LANES, width) // dh
    tables = [jnp.tile(t, (1, reps)) for t in (cos, sin_next, sin_prev)]
    mean = np.kron(np.eye(width // dh, dtype=np.float32), np.full((dh, dh), 1.0 / dh, np.float32))
    return (*tables, jnp.asarray(mean, BF16), half)


def _head_mean(p, mean):
    hi = p.astype(BF16)
    lo = (p - hi.astype(F32)).astype(BF16)
    return jnp.dot(hi, mean, preferred_element_type=F32) + jnp.dot(lo, mean, preferred_element_type=F32)


def _prep_specs(S, width, off, tw):
    assert off % width == 0
    tm = _tile(S, 512, 16)
    x = pl.BlockSpec((tm, width), lambda i: (i, off // width))
    row = pl.BlockSpec((tm, width), lambda i: (i, 0))
    tab = pl.BlockSpec((tm, tw), lambda i: (i, 0))
    vec = pl.BlockSpec((1, width), lambda i: (0, 0))
    mat = pl.BlockSpec((width, width), lambda i: (0, 0))
    return tm, x, row, tab, vec, mat


def _qk_prep(name, proj, off, width, gain_row, consts):
    S = proj.shape[0]
    cos, sin_next, sin_prev, mean, half = consts
    tm, x, row, tab, vec, mat = _prep_specs(S, width, off, cos.shape[1])
    reps = width // cos.shape[1]

    def body(x_ref, g_ref, c_ref, sn_ref, sp_ref, m_ref, o_ref):
        xv = x_ref[...].astype(F32)
        y = xv * lax.rsqrt(_head_mean(xv * xv, m_ref[...]) + RMS_EPS) * g_ref[...]
        t = lambda r: jnp.tile(r[...], (1, reps))
        o_ref[...] = (y * t(c_ref) + pltpu.roll(y, width - half, 1) * t(sn_ref) + pltpu.roll(y, half, 1) * t(sp_ref)).astype(BF16)

    return pl.pallas_call(
        body, name=name, grid=(S // tm,), in_specs=[x, vec, tab, tab, tab, mat], out_specs=row,
        out_shape=jax.ShapeDtypeStruct((S, width), BF16), compiler_params=_cparams(("parallel",)),
    )(proj, gain_row, cos, sin_next, sin_prev, mean)


def _qk_prep_bwd(name, proj, off, width, gain_row, consts, dout):
    S = proj.shape[0]
    cos, sin_next, sin_prev, mean, half = consts
    tm, x, row, tab, vec, mat = _prep_specs(S, width, off, cos.shape[1])
    reps = width // cos.shape[1]

    def body(x_ref, g_ref, c_ref, sn_ref, sp_ref, m_ref, do_ref, dx_ref, dg_ref):
        xv = x_ref[...].astype(F32)
        r = lax.rsqrt(_head_mean(xv * xv, m_ref[...]) + RMS_EPS)
        xhat = xv * r
        dov = do_ref[...]
        t = lambda ref: jnp.tile(ref[...], (1, reps))
        dy = dov * t(c_ref) + pltpu.roll(dov * t(sn_ref), half, 1) + pltpu.roll(dov * t(sp_ref), width - half, 1)
        dxhat = dy * g_ref[...]
        dx_ref[...] = (r * (dxhat - xhat * _head_mean(dxhat * xhat, m_ref[...]))).astype(BF16)

        @pl.when(pl.program_id(0) == 0)
        def _():
            dg_ref[...] = jnp.zeros_like(dg_ref)

        dg_ref[...] += jnp.sum(dy * xhat, axis=0, keepdims=True)

    return pl.pallas_call(
        body, name=name, grid=(S // tm,), in_specs=[x, vec, tab, tab, tab, mat, row], out_specs=(row, vec),
        out_shape=(jax.ShapeDtypeStruct((S, width), BF16), jax.ShapeDtypeStruct((1, width), F32)),
        compiler_params=_cparams(("arbitrary",)),
    )(proj, gain_row, cos, sin_next, sin_prev, mean, dout)


def _attn_probs(q, kp, kc, sink_col, n, scale):
    rows = q.shape[0]
    sp = lax.dot_general(q, kp, NT, preferred_element_type=F32) * scale
    sc = lax.dot_general(q, kc, NT, preferred_element_type=F32) * scale
    qi = lax.broadcasted_iota(jnp.int32, (rows, BLOCK), 0) % BLOCK
    kj = lax.broadcasted_iota(jnp.int32, (rows, BLOCK), 1)
    sp = jnp.where((kj > qi) & (n > 0), sp, NEG_INF)
    sc = jnp.where(kj <= qi, sc, NEG_INF)
    m = jnp.maximum(jnp.maximum(jnp.max(sp, axis=-1, keepdims=True), jnp.max(sc, axis=-1, keepdims=True)), sink_col)
    pp, pc, ps = jnp.exp(sp - m), jnp.exp(sc - m), jnp.exp(sink_col - m)
    inv = 1.0 / (jnp.sum(pp, axis=-1, keepdims=True) + jnp.sum(pc, axis=-1, keepdims=True) + ps)
    return pp * inv, pc * inv, ps * inv


def _sink_col(sink_ref, hk, group):
    rows = group * BLOCK
    g = lax.broadcasted_iota(jnp.int32, (rows, 1), 0) // BLOCK
    col = jnp.zeros((rows, 1), F32)
    for i in range(group):
        col = jnp.where(g == i, sink_ref[hk * group + i], col)
    return col


def _attn_specs(group, S, dh):
    heads = pl.BlockSpec((group, S, dh), lambda hk: (hk, 0, 0))
    kv = pl.BlockSpec((None, S, dh), lambda hk: (hk, 0, 0))
    return heads, kv, pl.BlockSpec(memory_space=pltpu.SMEM)


def _block_rows(n):
    cur = pl.ds(pl.multiple_of(n * BLOCK, BLOCK), BLOCK)
    prev = pl.ds(pl.multiple_of(jnp.maximum(n - 1, 0) * BLOCK, BLOCK), BLOCK)
    return cur, prev


def _attn_fwd(name, q, k, v, sinks):
    HQ, S, dh = q.shape
    HKV = k.shape[0]
    group = HQ // HKV
    scale = dh ** -0.5
    heads, kv, smem = _attn_specs(group, S, dh)

    def body(q_ref, k_ref, v_ref, sink_ref, o_ref):
        sink = _sink_col(sink_ref, pl.program_id(0), group)

        def block(n, carry):
            cur, prev = _block_rows(n)
            qv = q_ref[:, cur, :].reshape(group * BLOCK, dh)
            pp, pc, _ = _attn_probs(qv, k_ref[prev, :], k_ref[cur, :], sink, n, scale)
            o = jnp.dot(pp.astype(BF16), v_ref[prev, :], preferred_element_type=F32)
            o = o + jnp.dot(pc.astype(BF16), v_ref[cur, :], preferred_element_type=F32)
            o_ref[:, cur, :] = o.reshape(group, BLOCK, dh).astype(BF16)
            return carry

        lax.fori_loop(0, S // BLOCK, block, 0)

    return pl.pallas_call(
        body, name=name, grid=(HKV,), in_specs=[heads, kv, kv, smem], out_specs=heads,
        out_shape=jax.ShapeDtypeStruct((HQ, S, dh), BF16), compiler_params=_cparams(("parallel",)),
    )(q, k, v, sinks)


def _attn_bwd(name, q, k, v, sinks, do):
    HQ, S, dh = q.shape
    HKV = k.shape[0]
    group = HQ // HKV
    scale = dh ** -0.5
    heads, kv, smem = _attn_specs(group, S, dh)
    sk = pl.BlockSpec((None, group, LANES), lambda hk: (hk, 0, 0))

    def body(q_ref, k_ref, v_ref, sink_ref, do_ref, dq_ref, dk_ref, dv_ref, ds_ref):
        rows = group * BLOCK
        sink = _sink_col(sink_ref, pl.program_id(0), group)
        dk_ref[...] = jnp.zeros_like(dk_ref)
        dv_ref[...] = jnp.zeros_like(dv_ref)
        tdot = lambda a, b: lax.dot_general(a, b, TN, preferred_element_type=F32)

        def block(n, dsink):
            cur, prev = _block_rows(n)
            qv = q_ref[:, cur, :].reshape(rows, dh)
            dov = do_ref[:, cur, :].reshape(rows, dh)
            kp, kc, vp, vc = k_ref[prev, :], k_ref[cur, :], v_ref[prev, :], v_ref[cur, :]
            pp, pc, ps = _attn_probs(qv, kp, kc, sink, n, scale)
            dpp = lax.dot_general(dov, vp, NT, preferred_element_type=F32)
            dpc = lax.dot_general(dov, vc, NT, preferred_element_type=F32)
            delta = jnp.sum(pp * dpp, axis=-1, keepdims=True) + jnp.sum(pc * dpc, axis=-1, keepdims=True)
            dsp = (pp * (dpp - delta) * scale).astype(BF16)
            dsc = (pc * (dpc - delta) * scale).astype(BF16)
            dq = jnp.dot(dsp, kp, preferred_element_type=F32) + jnp.dot(dsc, kc, preferred_element_type=F32)
            dq_ref[:, cur, :] = dq.reshape(group, BLOCK, dh)
            dk_ref[prev, :] += tdot(dsp, qv)
            dv_ref[prev, :] += tdot(pp.astype(BF16), dov)
            dk_ref[cur, :] += tdot(dsc, qv)
            dv_ref[cur, :] += tdot(pc.astype(BF16), dov)
            return dsink - jnp.sum((ps * delta).reshape(group, BLOCK, 1), axis=1)

        pair = lambda m, dsink: block(2 * m + 1, block(2 * m, dsink))
        dsink = lax.fori_loop(0, S // (2 * BLOCK), pair, jnp.zeros((group, 1), F32))
        ds_ref[...] = jnp.broadcast_to(dsink, (group, LANES))

    return pl.pallas_call(
        body, name=name, grid=(HKV,), in_specs=[heads, kv, kv, smem, heads], out_specs=(heads, kv, kv, sk),
        out_shape=(jax.ShapeDtypeStruct((HQ, S, dh), F32), jax.ShapeDtypeStruct((HKV, S, dh), F32),
                   jax.ShapeDtypeStruct((HKV, S, dh), F32), jax.ShapeDtypeStruct((HKV, group, LANES), F32)),
        compiler_params=_cparams(("parallel",)),
    )(q, k, v, sinks, do)


RESIDENT = pl.Buffered(1)


def _gate_blocks(tm, Ns, off):
    assert off % Ns == 0
    return [pl.BlockSpec((tm, Ns), lambda i, k=k: (i, off // Ns + k)) for k in range(N_CHIPS)]


def _mixer_out_fwd(name, aconv, o, woc3, woa3, wo, proj, x1, ga_off, gb_off):
    S, D = x1.shape
    Ns = woc3.shape[2]
    tm = _tile(S, 256, 16)

    def body(a_ref, o_ref, woc_ref, woa_ref, wo_ref, x1_ref, *rest):
        ga_refs, gb_refs = rest[:N_CHIPS], rest[N_CHIPS:2 * N_CHIPS]
        ya_ref, yb_ref, m_ref, x2_ref = rest[2 * N_CHIPS:]
        av, ov = a_ref[...], o_ref[...]
        for s in range(N_CHIPS):
            cols = slice(s * Ns, (s + 1) * Ns)
            ya = jnp.dot(av, woc_ref[s], preferred_element_type=F32)
            yb = jnp.dot(ov, woa_ref[s], preferred_element_type=F32)
            ya_ref[:, cols] = ya.astype(BF16)
            yb_ref[:, cols] = yb.astype(BF16)
            ga, gb = ga_refs[s][...].astype(F32), gb_refs[s][...].astype(F32)
            m_ref[:, cols] = (_sigmoid(ga) * ya + _sigmoid(gb) * yb).astype(BF16)
        x2_ref[...] = x1_ref[...] + jnp.dot(m_ref[...], wo_ref[...], preferred_element_type=F32)

    row = lambda w: pl.BlockSpec((tm, w), lambda i: (i, 0))
    whole3 = lambda a: pl.BlockSpec(a.shape, lambda i: (0, 0, 0), pipeline_mode=RESIDENT)
    act = jax.ShapeDtypeStruct((S, D), BF16)
    return pl.pallas_call(
        body, name=name, grid=(S // tm,),
        in_specs=[row(aconv.shape[1]), row(o.shape[1]), whole3(woc3), whole3(woa3),
                  pl.BlockSpec(wo.shape, lambda i: (0, 0), pipeline_mode=RESIDENT), row(D)]
        + _gate_blocks(tm, Ns, ga_off) + _gate_blocks(tm, Ns, gb_off),
        out_specs=(row(D), row(D), row(D), row(D)), out_shape=(act, act, act, jax.ShapeDtypeStruct((S, D), F32)),
        compiler_params=_cparams(("parallel",)),
    )(aconv, o, woc3, woa3, wo, x1, *([proj] * (2 * N_CHIPS)))


def _mixer_out_bwd(name, dx2_b, wo, ya, yb, proj, woc3, woa3, ga_off, gb_off, dep):
    S, D = dx2_b.shape
    K, Ns = woc3.shape[1], woc3.shape[2]
    tm = _tile(S, 256, 16)

    def body(dx_ref, wo_ref, ya_ref, yb_ref, woc_ref, woa_ref, *rest):
        ga_refs, gb_refs = rest[:N_CHIPS], rest[N_CHIPS:2 * N_CHIPS]
        dga_ref, dgb_ref, dya_ref, dyb_ref, da_ref, do_ref = rest[-6:]
        dm = lax.dot_general(dx_ref[...], wo_ref[...], NT, preferred_element_type=F32)
        da = do = None
        for s in range(N_CHIPS):
            cols = slice(s * Ns, (s + 1) * Ns)
            dms = dm[:, cols]
            sa, sb = _sigmoid(ga_refs[s][...].astype(F32)), _sigmoid(gb_refs[s][...].astype(F32))
            dga_ref[:, cols] = (dms * ya_ref[:, cols].astype(F32) * sa * (1.0 - sa)).astype(BF16)
            dgb_ref[:, cols] = (dms * yb_ref[:, cols].astype(F32) * sb * (1.0 - sb)).astype(BF16)
            dya, dyb = (dms * sa).astype(BF16), (dms * sb).astype(BF16)
            dya_ref[:, cols] = dya
            dyb_ref[:, cols] = dyb
            pa = lax.dot_general(dya, woc_ref[s], NT, preferred_element_type=F32)
            pb = lax.dot_general(dyb, woa_ref[s], NT, preferred_element_type=F32)
            da, do = (pa, pb) if da is None else (da + pa, do + pb)
        da_ref[...] = da
        do_ref[...] = do.astype(BF16)

    row = lambda w: pl.BlockSpec((tm, w), lambda i: (i, 0))
    whole3 = lambda a: pl.BlockSpec(a.shape, lambda i: (0, 0, 0), pipeline_mode=RESIDENT)
    act = jax.ShapeDtypeStruct((S, D), BF16)
    return pl.pallas_call(
        body, name=name, grid=(S // tm,),
        in_specs=[row(D), pl.BlockSpec(wo.shape, lambda i: (0, 0), pipeline_mode=RESIDENT), row(D), row(D), whole3(woc3), whole3(woa3)]
        + _gate_blocks(tm, Ns, ga_off) + _gate_blocks(tm, Ns, gb_off) + [pl.BlockSpec(dep.shape, lambda i: (0, 0))],
        out_specs=(row(D), row(D), row(D), row(D), row(K), row(K)),
        out_shape=(act, act, act, act, jax.ShapeDtypeStruct((S, K), F32), jax.ShapeDtypeStruct((S, K), BF16)),
        compiler_params=_cparams(("parallel",)),
    )(dx2_b, wo, ya, yb, woc3, woa3, *([proj] * (2 * N_CHIPS)), dep)


ANY = pl.BlockSpec(memory_space=pl.ANY)


def _row_tile(rows, cols, n_arrays):
    want = max(16, (VMEM_LIMIT_V7X // 2) // (2 * n_arrays * cols * 4))
    return _tile(rows, want, 16)


def _cast_to_slot(name, w, dtype, p_arr, dep=None):
    R, C = w.shape
    tr = _row_tile(R, C, 2)
    extra = () if dep is None else (dep,)

    def body(p_ref, w_ref, *rest):
        rest[-1][...] = w_ref[...].astype(dtype)

    return pl.pallas_call(
        body, name=name,
        grid_spec=pltpu.PrefetchScalarGridSpec(
            num_scalar_prefetch=1, grid=(R // tr,),
            in_specs=[pl.BlockSpec((tr, C), lambda i, p_ref: (i, 0))] + [pl.BlockSpec(d.shape, lambda i, p_ref: (0, 0)) for d in extra],
            out_specs=pl.BlockSpec((None, tr, C), lambda i, p_ref: (p_ref[0], i, 0))),
        out_shape=jax.ShapeDtypeStruct((N_CHIPS, R, C), dtype), compiler_params=_cparams(("parallel",)),
    )(p_arr, w, *extra)


def _add_half(name, g3, r3, c_arr):
    n, h, C = r3.shape
    tr = _row_tile(h, C, 3)
    nb = h // tr

    def body(c_ref, g_ref, r_ref, o_ref):
        o_ref[...] = (g_ref[...].astype(F32) + r_ref[...].astype(F32)).astype(BF16)

    blk = pl.BlockSpec((None, tr, C), lambda s, i, c_ref: (s, i, 0))
    return pl.pallas_call(
        body, name=name,
        grid_spec=pltpu.PrefetchScalarGridSpec(
            num_scalar_prefetch=1, grid=(n, nb),
            in_specs=[pl.BlockSpec((None, tr, C), lambda s, i, c_ref: (s, c_ref[0] * nb + i, 0)), blk], out_specs=blk),
        out_shape=jax.ShapeDtypeStruct(r3.shape, BF16), compiler_params=_cparams(("parallel", "parallel")),
    )(c_arr, g3, r3)


def _add_chips(name, t3, r3, cp_arr):
    n, h, C = r3.shape
    tr = _row_tile(h, C, 6)
    nb = h // tr

    def body(cp_ref, t_ref, r0_ref, r1_ref, r2_ref, r3_ref, o_ref):
        p = cp_ref[1]
        total = None
        for a, r_ref in enumerate((r0_ref, r1_ref, r2_ref, r3_ref)):
            part = jnp.where(p == a, t_ref[...], r_ref[...]).astype(F32)
            total = part if total is None else total + part
        o_ref[...] = total

    def part(a):
        return pl.BlockSpec((None, tr, C), lambda i, cp_ref: (jnp.where(cp_ref[1] == a, (a + 1) % N_CHIPS, a), i, 0))

    return pl.pallas_call(
        body, name=name,
        grid_spec=pltpu.PrefetchScalarGridSpec(
            num_scalar_prefetch=1, grid=(nb,),
            in_specs=[pl.BlockSpec((None, tr, C), lambda i, cp_ref: (cp_ref[1], i, 0)), part(0), part(1), part(2), part(3)],
            out_specs=pl.BlockSpec((tr, C), lambda i, cp_ref: (cp_ref[0] * nb + i, 0))),
        out_shape=jax.ShapeDtypeStruct((2 * h, C), F32), compiler_params=_cparams(("parallel",)),
    )(cp_arr, t3, r3, r3, r3, r3)


def _adamw(name, w, g, m, v, deps=()):
    R, C = w.shape
    extra = tuple(deps)
    tr = _row_tile(R, C, 8)
    c1 = 1.0 - ADAM_B1 ** ADAM_STEP
    c2 = 1.0 - ADAM_B2 ** ADAM_STEP

    def body(w_ref, g_ref, m_ref, v_ref, *rest):
        go_ref, d_ref, nm_ref, nv_ref = rest[-4:]
        gv = g_ref[...]
        go_ref[...] = gv
        nm = ADAM_B1 * m_ref[...] + (1.0 - ADAM_B1) * gv
        nv = ADAM_B2 * v_ref[...] + (1.0 - ADAM_B2) * (gv * gv)
        d_ref[...] = -ADAM_LR * ((nm / c1) / (jnp.sqrt(nv / c2) + ADAM_EPS) + ADAM_WD * w_ref[...])
        nm_ref[...] = nm
        nv_ref[...] = nv

    blk = pl.BlockSpec((tr, C), lambda i: (i, 0))
    o = jax.ShapeDtypeStruct((R, C), F32)
    return pl.pallas_call(
        body, name=name, grid=(R // tr,), in_specs=[blk, blk, blk, blk] + [ANY] * len(extra), out_specs=(blk, blk, blk, blk),
        out_shape=(o, o, o, o), compiler_params=_cparams(("parallel",)),
    )(w, g, m, v, *extra)


def _place():
    x, y, c = lax.axis_index("x"), lax.axis_index("y"), lax.axis_index("c")
    chips = [(1 - x, y), (x, 1 - y), (1 - x, 1 - y)]
    return x, y, c, 2 * x + y, chips


HBM = pl.BlockSpec(memory_space=pltpu.HBM)
SEM = pl.BlockSpec(memory_space=pltpu.SEMAPHORE)
TOKEN = jax.ShapeDtypeStruct((8, LANES), F32)
DATAFLOW = pltpu.SideEffectType.DATAFLOW_SIDE_EFFECTING


def _hbm(a):
    return pltpu.with_memory_space_constraint(a, pltpu.HBM)


def _gather_blocks(bufs, i, c, p, chips):
    if bufs[i].shape[1] % 16:
        return bufs[i].at[p], [bufs[i].at[2 * cx + cy] for cx, cy in chips]
    h = bufs[i].shape[1] // 2
    rows = pl.ds(pl.multiple_of(c * h, 16), h)
    return bufs[i].at[p, rows], [bufs[i].at[2 * cx + cy, rows] for cx, cy in chips]


def _gather_start(name, groups, dep):
    slots = [s for g in groups for s in g]
    n, ng = len(slots), len(groups)

    def body(*refs):
        bufs, sems, token = refs[:n], refs[n + 1:n + 1 + 2 * ng], refs[-1]
        x, y, c, p, chips = _place()
        i = 0
        for gi, g in enumerate(groups):
            send, recv = sems[2 * gi], sems[2 * gi + 1]
            for k in range(len(g)):
                mine, _ = _gather_blocks(bufs, i, c, p, chips)
                for j, chip in enumerate(chips):
                    pltpu.make_async_remote_copy(src_ref=mine, dst_ref=mine, send_sem=send.at[3 * k + j], recv_sem=recv.at[3 * k + j],
                                                 device_id=(*chip, c), device_id_type=MESH).start()
                i += 1
        token[...] = jnp.zeros_like(token)

    sem_shapes = [pltpu.SemaphoreType.DMA((3 * len(g),)) for g in groups for _ in range(2)]
    out = pl.pallas_call(
        body, name=name, in_specs=[HBM] * n + [ANY],
        out_specs=(*([SEM] * (2 * ng)), *([HBM] * n), pl.BlockSpec(memory_space=pltpu.VMEM)),
        out_shape=(*sem_shapes, *[pltpu.HBM(s.shape, s.dtype) for s in slots], TOKEN),
        input_output_aliases={i: 2 * ng + i for i in range(n)},
        compiler_params=pltpu.CompilerParams(has_side_effects=DATAFLOW),
    )(*[_hbm(s) for s in slots], dep)
    started, i = [], 2 * ng
    for gi, g in enumerate(groups):
        started.append((out[2 * gi], out[2 * gi + 1], list(out[i:i + len(g)])))
        i += len(g)
    return started, out[-1]


def _gather_wait(name, send, recv, slots, after):
    n = len(slots)

    def body(*refs):
        bufs, send, recv = refs[:n], refs[n], refs[n + 1]
        x, y, c, p, chips = _place()
        for i in range(n):
            mine, landed = _gather_blocks(bufs, i, c, p, chips)
            for j, chip in enumerate(chips):
                cp = pltpu.make_async_remote_copy(src_ref=mine, dst_ref=landed[j], send_sem=send.at[3 * i + j],
                                                  recv_sem=recv.at[3 * i + j], device_id=(*chip, c), device_id_type=MESH)
                cp.wait_send()
                cp.wait_recv()

    return list(pl.pallas_call(
        body, name=name, in_specs=[HBM] * n + [SEM, SEM, ANY], out_specs=tuple([HBM] * n),
        out_shape=tuple(pltpu.HBM(s.shape, s.dtype) for s in slots),
        input_output_aliases={i: i for i in range(n)},
        compiler_params=pltpu.CompilerParams(has_side_effects=DATAFLOW),
    )(*slots, send, recv, after))


def _gather_forward(name, slots):
    idx = [i for i, s in enumerate(slots) if s.shape[1] % 16 == 0]
    n = len(slots)

    def body(*refs):
        bufs = refs[n:2 * n]
        send, recv = refs[2 * n:]
        x, y, c, p, chips = _place()

        def rdma(k, ref):
            return pltpu.make_async_remote_copy(src_ref=ref, dst_ref=ref, send_sem=send.at[k], recv_sem=recv.at[k],
                                                device_id=(x, y, 1 - c), device_id_type=MESH)

        cps = []
        for k, i in enumerate(idx):
            for j, ref in enumerate(_gather_blocks(bufs, i, c, p, chips)[1]):
                cps.append(rdma(3 * k + j, ref))
                cps[-1].start()
        for k, i in enumerate(idx):
            for j, ref in enumerate(_gather_blocks(bufs, i, 1 - c, p, chips)[1]):
                rdma(3 * k + j, ref).wait_recv()
        for cp in cps:
            cp.wait_send()

    return list(pl.pallas_call(
        body, name=name, in_specs=[ANY] * n, out_specs=tuple([ANY] * n),
        out_shape=tuple(jax.ShapeDtypeStruct(s.shape, s.dtype) for s in slots),
        scratch_shapes=[pltpu.SemaphoreType.DMA((3 * len(idx),)), pltpu.SemaphoreType.DMA((3 * len(idx),))],
        input_output_aliases={i: i for i in range(n)},
        compiler_params=pltpu.CompilerParams(has_side_effects=True),
    )(*slots))


def _swap_copy(grads, lands, send, recv, i, x, y, c):
    h = grads[i].shape[1] // 2
    other = pl.ds(pl.multiple_of((1 - c) * h, 16), h)
    return pltpu.make_async_remote_copy(src_ref=grads[i].at[:, other, :], dst_ref=lands[i], send_sem=send.at[i],
                                        recv_sem=recv.at[i], device_id=(x, y, 1 - c), device_id_type=MESH)


def _swap_wait(name, send, recv, grads, lands, after):
    n = len(grads)

    def body(*refs):
        ins, lands, send, recv = refs[:n], refs[n:2 * n], refs[2 * n], refs[2 * n + 1]
        x, y, c, p, chips = _place()
        for i in range(n):
            cp = _swap_copy(ins, lands, send, recv, i, x, y, c)
            cp.wait_send()
            cp.wait_recv()

    shapes = [pltpu.HBM(t.shape, t.dtype) for t in list(grads) + list(lands)]
    out = pl.pallas_call(
        body, name=name, in_specs=[HBM] * (2 * n) + [SEM, SEM, ANY], out_specs=tuple([HBM] * (2 * n)),
        out_shape=tuple(shapes), input_output_aliases={i: i for i in range(2 * n)},
        compiler_params=pltpu.CompilerParams(has_side_effects=DATAFLOW),
    )(*grads, *lands, send, recv, after)
    return list(out[:n]), list(out[n:])


def _reduce_starts(name, grads, parts):
    ng, npt = len(grads), len(parts)
    halves = [(g.shape[0], g.shape[1] // 2, g.shape[2]) for g in grads]
    arrays = list(grads) + [lax.empty(s, g.dtype) for s, g in zip(halves, grads)] + list(parts) + [lax.empty(t.shape, t.dtype) for t in parts]
    na = len(arrays)
    sems = ([pltpu.SemaphoreType.DMA((ng,))] * 2 if ng else []) + ([pltpu.SemaphoreType.DMA((3 * npt,))] * 2 if npt else [])
    ns = len(sems)

    def body(*refs):
        ins, sem, token = refs[:na], list(refs[na:na + ns]), refs[-1]
        x, y, c, p, chips = _place()
        if ng:
            for i in range(ng):
                _swap_copy(ins[:ng], ins[ng:2 * ng], sem[0], sem[1], i, x, y, c).start()
        if npt:
            src, land, send, recv = ins[2 * ng:2 * ng + npt], ins[2 * ng + npt:], sem[-2], sem[-1]
            for i in range(npt):
                for j, (cx, cy) in enumerate(chips):
                    pltpu.make_async_remote_copy(src_ref=src[i].at[2 * cx + cy], dst_ref=land[i].at[p], send_sem=send.at[3 * i + j],
                                                 recv_sem=recv.at[3 * i + j], device_id=(cx, cy, c), device_id_type=MESH).start()
        token[...] = jnp.zeros_like(token)

    out = pl.pallas_call(
        body, name=name, in_specs=[HBM] * na,
        out_specs=(*([SEM] * ns), *([HBM] * na), pl.BlockSpec(memory_space=pltpu.VMEM)),
        out_shape=(*sems, *[pltpu.HBM(a.shape, a.dtype) for a in arrays], TOKEN),
        input_output_aliases={i: ns + i for i in range(na)},
        compiler_params=pltpu.CompilerParams(has_side_effects=DATAFLOW),
    )(*[_hbm(a) for a in arrays])
    bufs = list(out[ns:ns + na])
    swap = (out[0], out[1], bufs[:ng], bufs[ng:2 * ng]) if ng else None
    exch = (out[ns - 2], out[ns - 1], bufs[2 * ng:2 * ng + npt], bufs[2 * ng + npt:]) if npt else None
    return swap, exch, out[-1]


def _exchange_wait(name, send, recv, parts, lands, after):
    n = len(parts)

    def body(*refs):
        ins, lands, send, recv = refs[:n], refs[n:2 * n], refs[2 * n], refs[2 * n + 1]
        x, y, c, p, chips = _place()
        for i in range(n):
            for j, (cx, cy) in enumerate(chips):
                q = 2 * cx + cy
                cp = pltpu.make_async_remote_copy(src_ref=ins[i].at[q], dst_ref=lands[i].at[q], send_sem=send.at[3 * i + j],
                                                  recv_sem=recv.at[3 * i + j], device_id=(cx, cy, c), device_id_type=MESH)
                cp.wait_send()
                cp.wait_recv()

    shapes = [pltpu.HBM(t.shape, t.dtype) for t in parts]
    out = pl.pallas_call(
        body, name=name, in_specs=[HBM] * (2 * n) + [SEM, SEM, ANY], out_specs=tuple([HBM] * (2 * n)),
        out_shape=(*shapes, *shapes), input_output_aliases={i: i for i in range(2 * n)},
        compiler_params=pltpu.CompilerParams(has_side_effects=DATAFLOW),
    )(*parts, *lands, send, recv, after)
    return list(out[:n]), list(out[n:])


def _join_copy(buf, send_sem, recv_sem, which, x, y, c):
    h = buf.shape[0] // 2
    rows = buf.at[pl.ds(pl.multiple_of(which * h, 8), h)]
    return pltpu.make_async_remote_copy(src_ref=rows, dst_ref=rows, send_sem=send_sem, recv_sem=recv_sem,
                                        device_id=(x, y, 1 - c), device_id_type=MESH)


def _join_start(name, groups):
    bufs = [b for g in groups for b in g]
    n, ng = len(bufs), len(groups)

    def body(*refs):
        ins, sems, token = refs[:n], refs[n:n + 2 * ng], refs[-1]
        x, y, c, p, chips = _place()
        i = 0
        for gi, g in enumerate(groups):
            for k in range(len(g)):
                _join_copy(ins[i], sems[2 * gi].at[k], sems[2 * gi + 1].at[k], c, x, y, c).start()
                i += 1
        token[...] = jnp.zeros_like(token)

    sem_shapes = [pltpu.SemaphoreType.DMA((len(g),)) for g in groups for _ in range(2)]
    out = pl.pallas_call(
        body, name=name, in_specs=[HBM] * n,
        out_specs=(*([SEM] * (2 * ng)), *([HBM] * n), pl.BlockSpec(memory_space=pltpu.VMEM)),
        out_shape=(*sem_shapes, *[pltpu.HBM(t.shape, t.dtype) for t in bufs], TOKEN),
        input_output_aliases={i: 2 * ng + i for i in range(n)},
        compiler_params=pltpu.CompilerParams(has_side_effects=DATAFLOW),
    )(*[_hbm(t) for t in bufs])
    started, i = [], 2 * ng
    for gi, g in enumerate(groups):
        started.append((out[2 * gi], out[2 * gi + 1], list(out[i:i + len(g)])))
        i += len(g)
    return started, out[-1]


def _join_wait(name, send, recv, bufs, after):
    n = len(bufs)

    def body(*refs):
        ins, send, recv = refs[:n], refs[n], refs[n + 1]
        x, y, c, p, chips = _place()
        for i in range(n):
            _join_copy(ins[i], send.at[i], recv.at[i], c, x, y, c).wait_send()
            _join_copy(ins[i], send.at[i], recv.at[i], 1 - c, x, y, c).wait_recv()

    return list(pl.pallas_call(
        body, name=name, in_specs=[HBM] * n + [SEM, SEM, ANY], out_specs=tuple([HBM] * n),
        out_shape=tuple(pltpu.HBM(t.shape, t.dtype) for t in bufs), input_output_aliases={i: i for i in range(n)},
        compiler_params=pltpu.CompilerParams(has_side_effects=DATAFLOW),
    )(*bufs, send, recv, after))


def _allreduce_small(name, pack, dep):
    R, W = pack.shape

    def body(in_ref, dep_ref, out_ref, slots, send, recv):
        x, y, c = lax.axis_index("x"), lax.axis_index("y"), lax.axis_index("c")
        me = 4 * x + 2 * y + c
        slots[0] = in_ref[...]
        cps = []
        for k in range(1, N_DEV):
            peer = (x ^ (k >> 2), y ^ ((k >> 1) & 1), c ^ (k & 1))
            cp = pltpu.make_async_remote_copy(src_ref=in_ref, dst_ref=slots.at[k], send_sem=send.at[k - 1],
                                              recv_sem=recv.at[k - 1], device_id=peer, device_id_type=MESH)
            cp.start()
            cps.append(cp)
        for cp in cps:
            cp.wait()
        total = slots[me]
        for a in range(1, N_DEV):
            total = total + slots[jnp.bitwise_xor(a, me)]
        out_ref[...] = total

    vmem = pl.BlockSpec(memory_space=pltpu.VMEM)
    return pl.pallas_call(
        body, name=name, in_specs=[vmem, ANY], out_specs=vmem, out_shape=jax.ShapeDtypeStruct((R, W), F32),
        scratch_shapes=[pltpu.VMEM((N_DEV, R, W), F32), pltpu.SemaphoreType.DMA((N_DEV - 1,)), pltpu.SemaphoreType.DMA((N_DEV - 1,))],
        compiler_params=pltpu.CompilerParams(has_side_effects=True),
    )(pack, dep)


def _heads(a, n_heads):
    S = a.shape[0]
    return a.reshape(S, n_heads, a.shape[1] // n_heads).transpose(1, 0, 2)


def _unheads(a):
    H, S, dh = a.shape
    return a.transpose(1, 0, 2).reshape(S, H * dh)


def _ffn_bwd(tag, xin, gain, wgu3, wd, saved, dxout, dxo_b, reduce_start, dep, flush=None):
    h, gu, act = saved
    D = xin.shape[1]
    tok = reduce_start({f"w_down{tag}": _mm_tn(f"dw_down_{tag}", act, dxo_b, 0.5, dep=dep).reshape(N_CHIPS, -1, D)})
    dgu = _ffn_down_bwd(f"ffn_down_bwd_{tag}", dxo_b, wd, gu, 0.5, dep=tok)
    tok = reduce_start({f"w_gu{tag}": _mm_tn_cols(f"dw_gu_{tag}", h, dgu, wgu3.shape[2], b_is_gu=True)})
    if flush is not None:
        tok = flush(tok)
    dh = _mm_nt_cols(f"ffn_up_bwd_{tag}", dgu, wgu3, a_is_gu=True, dep=tok)
    dxin, dxin_b, dgain = _rms_bwd(f"rms_bwd_{tag}", xin, gain, dh, dxout)
    return dxin, dxin_b, dgain, tok


def kernel(x, g_ffn1, w_gu1, w_down1, g_mix, w_in, conv_w, q_norm_g, k_norm_g, sinks, w_out_conv, w_out_attn, w_o, g_ffn2, w_gu2, w_down2, loss_target, m_g_ffn1, m_w_gu1, m_w_down1, m_g_mix, m_w_in, m_conv_w, m_q_norm_g, m_k_norm_g, m_sinks, m_w_out_conv, m_w_out_attn, m_w_o, m_g_ffn2, m_w_gu2, m_w_down2, v_g_ffn1, v_w_gu1, v_w_down1, v_g_mix, v_w_in, v_conv_w, v_q_norm_g, v_k_norm_g, v_sinks, v_w_out_conv, v_w_out_attn, v_w_o, v_g_ffn2, v_w_gu2, v_w_down2):
    S, D = x.shape[1], x.shape[2]
    dh = q_norm_g.shape[1]
    HQ = sinks.shape[1]
    HKV = HQ // 4
    AW, KVW, CW = HQ * dh, HKV * dh, D // 2
    off_q, off_k, off_v = 3 * CW, 3 * CW + AW, 3 * CW + AW + KVW
    off_ga, off_gb = off_v + KVW, off_v + KVW + D
    x0, target = x[0], loss_target[0]
    cx, cy, cc = lax.axis_index("x"), lax.axis_index("y"), lax.axis_index("c")
    chip = 2 * cx + cy
    p_arr = jnp.reshape(chip, (1,)).astype(jnp.int32)
    c_arr = jnp.reshape(cc, (1,)).astype(jnp.int32)
    cp_arr = jnp.stack([cc, chip]).astype(jnp.int32)
    wts = dict(g_ffn1=g_ffn1, w_gu1=w_gu1, w_down1=w_down1, g_mix=g_mix, w_in=w_in, conv_w=conv_w, q_norm_g=q_norm_g,
               k_norm_g=k_norm_g, sinks=sinks, w_out_conv=w_out_conv, w_out_attn=w_out_attn, w_o=w_o, g_ffn2=g_ffn2,
               w_gu2=w_gu2, w_down2=w_down2)
    ms = dict(g_ffn1=m_g_ffn1, w_gu1=m_w_gu1, w_down1=m_w_down1, g_mix=m_g_mix, w_in=m_w_in, conv_w=m_conv_w,
              q_norm_g=m_q_norm_g, k_norm_g=m_k_norm_g, sinks=m_sinks, w_out_conv=m_w_out_conv, w_out_attn=m_w_out_attn,
              w_o=m_w_o, g_ffn2=m_g_ffn2, w_gu2=m_w_gu2, w_down2=m_w_down2)
    vs = dict(g_ffn1=v_g_ffn1, w_gu1=v_w_gu1, w_down1=v_w_down1, g_mix=v_g_mix, w_in=v_w_in, conv_w=v_conv_w,
              q_norm_g=v_q_norm_g, k_norm_g=v_k_norm_g, sinks=v_sinks, w_out_conv=v_w_out_conv, w_out_attn=v_w_out_attn,
              w_o=v_w_o, g_ffn2=v_g_ffn2, w_gu2=v_w_gu2, w_down2=v_w_down2)
    order = list(wts)
    small_names = [k for k in order if not k.startswith("w_")]
    grad, delta, new_m, new_v = {}, {}, {}, {}

    def cast(keys, dep=None):
        return [_cast_to_slot(f"cast_{k}", wts[k][0], F32 if k == "conv_w" else BF16, p_arr, dep) for k in keys]

    def gather_finish(tag, started, after):
        send, recv, slots = started
        return _gather_forward(f"gather_forward_{tag}", _gather_wait(f"gather_wait_{tag}", send, recv, slots, after))

    swapping, pending = [], []

    def reduce_start(full, after=None):
        keys = [] if full is None else list(full)
        pkeys, parts = [], []
        if swapping:
            pkeys, send, recv, gs, lands = swapping.pop(0)
            gs, sib = _swap_wait(f"swap_wait_{pkeys[0]}", send, recv, gs, lands, after if full is None else full[keys[0]])
            parts = [_add_half(f"add_half_{k}", g, r, c_arr) for k, g, r in zip(pkeys, gs, sib)]
        swap, exch, tok = _reduce_starts(f"reduce_starts_{keys[0] if keys else 'last'}", [full[k] for k in keys], parts)
        if exch:
            pending.append((pkeys, *exch))
        if swap:
            swapping.append((keys, *swap))
        return tok

    def reduce_finish(entries, after, behind_start=None):
        ready = []
        for keys, send, recv, parts, lands in entries:
            parts, lands = _exchange_wait(f"exchange_wait_{keys[0]}", send, recv, parts, lands, after)
            ready.append((keys, [_add_chips(f"add_chips_{k}", t, r, cp_arr) for k, t, r in zip(keys, parts, lands)]))
        started, last = _join_start(f"join_start_{ready[0][0][0]}", [halves for _, halves in ready])
        if behind_start is not None:
            last = behind_start(last)
        for (keys, _), (send, recv, halves) in zip(ready, started):
            for k, g2 in zip(keys, _join_wait(f"join_wait_{keys[0]}", send, recv, halves, last)):
                g2, d, nm, nv = _adamw(f"adamw_{k}", wts[k][0], g2, ms[k][0], vs[k][0], (last,))
                grad[k], delta[k], new_m[k], new_v[k] = g2[None], d[None], nm[None], nv[None]
                last = nv
        return last

    (st_gu1, st_d1), tok = _gather_start("gather_start_1", [cast(["w_gu1"]), cast(["w_down1"])], x0)
    later = ["w_in", "conv_w", "w_out_conv", "w_out_attn", "w_o", "w_gu2", "w_down2"]
    slot = dict(zip(later, cast(later, tok)))
    h1 = _rms_fwd("rms_fwd_1", x0, g_ffn1, slot["w_down2"])
    wgu1, = gather_finish("gu1", st_gu1, h1)
    (st_in, st_out, st_gu2, st_d2), tok = _gather_start(
        "gather_start_2", [[slot["w_in"], slot["conv_w"]], [slot["w_out_conv"], slot["w_out_attn"], slot["w_o"]],
                           [slot["w_gu2"]], [slot["w_down2"]]], wgu1)
    q_consts, k_consts = _prep_consts(S, dh, AW), _prep_consts(S, dh, KVW)
    qg_row, kg_row = jnp.tile(q_norm_g, (1, HQ)), jnp.tile(k_norm_g, (1, HKV))
    sink_vec = sinks[0]

    gu1, act1 = _ffn_up("ffn_up_1", h1, wgu1, tok)
    wd1 = gather_finish("d1", st_d1, act1)[0].reshape(-1, D)
    x1 = _mm_res("ffn_down_1", act1, wd1, x0, 0.5)
    win3, convw3 = gather_finish("in", st_in, x1)
    h2 = _rms_fwd("rms_fwd_mix", x1, g_mix)
    proj = _mm_cols("in_proj", h2, win3, BF16)
    aconv = _conv_fwd("conv_fwd", proj, convw3, CW)
    woc3, woa3, wo = gather_finish("out", st_out, aconv)
    wo = wo.reshape(-1, D)
    vh = _heads(proj[:, off_v:off_v + KVW], HKV)
    qn = _heads(_qk_prep("q_prep", proj, off_q, AW, qg_row, q_consts), HQ)
    kn = _heads(_qk_prep("k_prep", proj, off_k, KVW, kg_row, k_consts), HKV)
    oh = _attn_fwd("attn_fwd", qn, kn, vh, sink_vec)
    o = _unheads(oh)
    ya, yb, merged, x2 = _mixer_out_fwd("mixer_out", aconv, o, woc3, woa3, wo, proj, x1, off_ga, off_gb)
    wgu2, = gather_finish("gu2", st_gu2, x2)
    h3 = _rms_fwd("rms_fwd_2", x2, g_ffn2)
    gu2, act2 = _ffn_up("ffn_up_2", h3, wgu2)
    wd2 = gather_finish("d2", st_d2, act2)[0].reshape(-1, D)
    dy, dy_b, loss_lanes = _mm_res_loss("ffn_down_2_loss", act2, wd2, x2, 0.5, target)
    dx2, dx2_b, dg_ffn2, tok = _ffn_bwd("2", x2, g_ffn2, wgu2, wd2, (h3, gu2, act2), dy, dy_b, reduce_start, None)
    tok = reduce_start(dict(w_o=_mm_tn("dw_o", merged, dx2_b, dep=tok).reshape(N_CHIPS, -1, D)))
    dga, dgb, dya, dyb, daconv, do = _mixer_out_bwd("mixer_out_bwd", dx2_b, wo, ya, yb, proj, woc3, woa3, off_ga, off_gb, tok)
    dwoc = _mm_tn_cols("dw_out_conv", aconv, dya, woc3.shape[2])
    dwoa = _mm_tn_cols("dw_out_attn", o, dyb, woa3.shape[2])
    tok = reduce_start(dict(w_out_conv=dwoc, w_out_attn=dwoa))
    dxc, dbg, dcg, dconvw = _conv_bwd("conv_bwd", proj, convw3, daconv, CW)
    dqn, dkn, dvh, dsink3 = _attn_bwd("attn_bwd", qn, kn, vh, sink_vec, _heads(do, HQ))
    dq_raw, dqg = _qk_prep_bwd("q_prep_bwd", proj, off_q, AW, qg_row, q_consts, _unheads(dqn))
    dk_raw, dkg = _qk_prep_bwd("k_prep_bwd", proj, off_k, KVW, kg_row, k_consts, _unheads(dkn))
    dqg, dkg = dqg.reshape(HQ, dh).sum(axis=0, keepdims=True), dkg.reshape(HKV, dh).sum(axis=0, keepdims=True)
    dproj = jnp.concatenate([dxc, dbg, dcg, dq_raw, dk_raw, _unheads(dvh).astype(BF16), dga, dgb], axis=1)
    dh2 = _mm_nt_cols("in_proj_bwd", dproj, win3, dep=tok)
    tok = reduce_start(dict(w_in=_mm_tn_cols("dw_in", h2, dproj, win3.shape[2])))
    dx1, dx1_b, dg_mix = _rms_bwd("rms_bwd_mix", x1, g_mix, dh2, dx2)
    dx0, _, dg_ffn1, tok = _ffn_bwd("1", x0, g_ffn1, wgu1, wd1, (h1, gu1, act1), dx1, dx1_b, reduce_start, tok, lambda after: reduce_start(None, after))

    def rows8(a):
        a = a.reshape(-1, a.shape[-1])
        return jnp.pad(a, ((0, -a.shape[0] % 8), (0, D - a.shape[1])))

    misc = jnp.concatenate([dqg, dkg, dsink3[:, :, 0].reshape(1, HQ), loss_lanes], axis=1)
    done = reduce_finish(pending[:-2], dx0)
    small_pack_in = jnp.concatenate([rows8(a) for a in (dg_ffn1, dg_mix, dg_ffn2, dconvw, misc)], axis=0)
    summed = []

    def small_allreduce(dep):
        summed.append(_allreduce_small("allreduce_small", small_pack_in, dep))
        return summed[0]

    reduce_finish(pending[-2:], done, small_allreduce)
    tot = summed[0]

    cw_s = conv_w.shape[2]
    conv_row0, misc_row = 24, 24 + (-(-N_CHIPS * CONV_K // 8)) * 8
    small_g = dict(g_ffn1=tot[0:1], g_mix=tot[8:9], g_ffn2=tot[16:17],
                   conv_w=lax.dynamic_slice(tot, (conv_row0 + CONV_K * chip, 0), (CONV_K, cw_s)),
                   q_norm_g=tot[misc_row:misc_row + 1, 0:dh], k_norm_g=tot[misc_row:misc_row + 1, dh:2 * dh],
                   sinks=tot[misc_row:misc_row + 1, 2 * dh:2 * dh + HQ])
    loss = (0.5 / D) * jnp.sum(tot[misc_row, 2 * dh + HQ:2 * dh + HQ + LANES])

    def small_pack(src):
        return jnp.concatenate([rows8(src[k]) for k in small_names], axis=0)

    _, sd, sm, sv = _adamw("adamw_small", small_pack(wts), small_pack(small_g), small_pack(ms), small_pack(vs))
    for i, k in enumerate(small_names):
        shape = wts[k].shape
        nr, ncol = math.prod(shape[:-1]), shape[-1]
        grad[k] = small_g[k].reshape(shape)
        delta[k], new_m[k], new_v[k] = (a[8 * i:8 * i + nr, 0:ncol].reshape(shape) for a in (sd, sm, sv))
    return (loss, dx0[None], *[grad[k] for k in order], *[delta[k] for k in order],
            *[new_m[k] for k in order], *[new_v[k] for k in order])
```
